```python
import jax, jax.numpy as jnp
from jax import lax
import numpy as np

D_MODEL = 2048
BATCH = 8
SEQ = 8192
DEPTH = 2

GRID_W = 64
Q_BLOCK = 128
NORM_EPS = 1e-6
ROPE_THETA = 500000.0
AXIAL_THETA = 10000.0
MLA_HEADS = 8
MLA_Q_LORA = 512
MLA_KV_LORA = 256
MLA_NOPE_DIM = 128
MLA_ROPE_DIM = 64
MLA_QK_DIM = MLA_NOPE_DIM + MLA_ROPE_DIM
MLA_V_DIM = 128
GQA_HEADS = 8
GQA_KV_HEADS = 2
GQA_HEAD_DIM = 128
SWA_HEADS = 32
SWA_KV_HEADS = 4
SWA_HEAD_DIM = 64
SWA_WINDOW = 128
SWA_ROT_DIM = SWA_HEAD_DIM // 4
D_FF = 4 * D_MODEL
EVEN_IN = MLA_Q_LORA + MLA_KV_LORA + MLA_ROPE_DIM + (GQA_HEADS + 2 * GQA_KV_HEADS) * GQA_HEAD_DIM
EVEN_OUT = MLA_HEADS * MLA_V_DIM + GQA_HEADS * GQA_HEAD_DIM
ODD_IN = (SWA_HEADS + 2 * SWA_KV_HEADS) * SWA_HEAD_DIM
ODD_OUT = SWA_HEADS * SWA_HEAD_DIM
N_EVEN = (DEPTH + 1) // 2
N_ODD = DEPTH // 2

kernel_name = 'hybrid_mla_gridgqa_swa_sqrelu_encoder'


def rms_norm(x, gain):
    xf = x.astype(jnp.float32)
    y = xf * lax.rsqrt(jnp.mean(xf * xf, axis=-1, keepdims=True) + NORM_EPS)
    return (y * gain.astype(jnp.float32)).astype(x.dtype)


def rope_table(pos, dim, theta):
    inv = jnp.float32(theta) ** (-jnp.arange(0, dim, 2, dtype=jnp.float32) / dim)
    ang = pos.astype(jnp.float32)[:, None] * inv[None, :]
    return jnp.cos(ang), jnp.sin(ang)


def apply_rope(x, cos, sin):
    half = x.shape[-1] // 2
    c = cos[None, :, None, :].astype(x.dtype)
    s = sin[None, :, None, :].astype(x.dtype)
    x1 = x[..., :half]
    x2 = x[..., half:]
    return jnp.concatenate([x1 * c - x2 * s, x2 * c + x1 * s], axis=-1)


def dense_block_attention(q, k, v, scale):
    B, S, Hq, dk = q.shape
    Hkv = k.shape[2]
    G = Hq // Hkv
    dv = v.shape[-1]
    nb = S // Q_BLOCK
    qb = jnp.swapaxes(q.reshape(B, nb, Q_BLOCK, Hkv, G, dk), 0, 1)

    def one_block(qblk):
        s = jnp.einsum('bqhgd,bkhd->bhgqk', qblk, k, preferred_element_type=jnp.float32) * scale
        p = jax.nn.softmax(s, axis=-1).astype(v.dtype)
        return jnp.einsum('bhgqk,bkhd->bqhgd', p, v)

    o = lax.map(one_block, qb)
    return jnp.swapaxes(o, 0, 1).reshape(B, S, Hq, dv)


def banded_window_attention(q, k, v, sink, scale):
    B, S, Hq, d = q.shape
    Hkv = k.shape[2]
    G = Hq // Hkv
    nb = S // Q_BLOCK
    span = Q_BLOCK + 2 * SWA_WINDOW
    pad = ((0, 0), (SWA_WINDOW, SWA_WINDOW), (0, 0), (0, 0))
    kp = jnp.pad(k, pad)
    vp = jnp.pad(v, pad)
    qb = jnp.swapaxes(q.reshape(B, nb, Q_BLOCK, Hkv, G, d), 0, 1)
    sink_b = sink.astype(jnp.float32).reshape(1, Hkv, G, 1, 1)
    offs_q = jnp.arange(Q_BLOCK)
    offs_k = jnp.arange(span) - SWA_WINDOW

    def one_block(args):
        i, qblk = args
        start = i * Q_BLOCK
        kblk = lax.dynamic_slice_in_dim(kp, start, span, axis=1)
        vblk = lax.dynamic_slice_in_dim(vp, start, span, axis=1)
        q_pos = start + offs_q
        k_pos = start + offs_k
        valid = (jnp.abs(q_pos[:, None] - k_pos[None, :]) <= SWA_WINDOW) & ((k_pos >= 0) & (k_pos < S))[None, :]
        s = jnp.einsum('bqhgd,bkhd->bhgqk', qblk, kblk, preferred_element_type=jnp.float32) * scale
        s = jnp.where(valid, s, -jnp.inf)
        m = jnp.maximum(jnp.max(s, axis=-1, keepdims=True), sink_b)
        p = jnp.exp(s - m)
        denom = jnp.sum(p, axis=-1, keepdims=True) + jnp.exp(sink_b - m)
        p = (p / denom).astype(v.dtype)
        return jnp.einsum('bhgqk,bkhd->bqhgd', p, vblk)

    o = lax.map(one_block, (jnp.arange(nb), qb))
    return jnp.swapaxes(o, 0, 1).reshape(B, S, Hq, d)


def even_mixer(h, w_in, q_lat_norm, kv_lat_norm, w_uq, w_ukv, q_norm, k_nope_norm, k_rope_norm,
               g_q_norm, g_k_norm, w_out, mla_cos, mla_sin, row_cos, row_sin, col_cos, col_sin):
    B, S, _ = h.shape
    proj = h @ w_in
    o1 = MLA_Q_LORA
    o2 = o1 + MLA_KV_LORA
    o3 = o2 + MLA_ROPE_DIM
    o4 = o3 + GQA_HEADS * GQA_HEAD_DIM
    o5 = o4 + GQA_KV_HEADS * GQA_HEAD_DIM
    c_q = proj[..., :o1]
    c_kv = proj[..., o1:o2]
    k_rope = proj[..., o2:o3]
    q_g = proj[..., o3:o4].reshape(B, S, GQA_HEADS, GQA_HEAD_DIM)
    k_g = proj[..., o4:o5].reshape(B, S, GQA_KV_HEADS, GQA_HEAD_DIM)
    v_g = proj[..., o5:].reshape(B, S, GQA_KV_HEADS, GQA_HEAD_DIM)

    q_a = (rms_norm(c_q, q_lat_norm) @ w_uq).reshape(B, S, MLA_HEADS, MLA_QK_DIM)
    q_a = rms_norm(q_a, q_norm)
    q_a = jnp.concatenate([q_a[..., :MLA_NOPE_DIM], apply_rope(q_a[..., MLA_NOPE_DIM:], mla_cos, mla_sin)], axis=-1)
    kv = (rms_norm(c_kv, kv_lat_norm) @ w_ukv).reshape(B, S, MLA_HEADS, MLA_NOPE_DIM + MLA_V_DIM)
    k_nope = rms_norm(kv[..., :MLA_NOPE_DIM], k_nope_norm)
    v_a = kv[..., MLA_NOPE_DIM:]
    k_r = apply_rope(rms_norm(k_rope, k_rope_norm)[:, :, None, :], mla_cos, mla_sin)
    k_a = jnp.concatenate([k_nope, jnp.broadcast_to(k_r, (B, S, MLA_HEADS, MLA_ROPE_DIM))], axis=-1)
    o_a = dense_block_attention(q_a, k_a, v_a, MLA_QK_DIM ** -0.5)

    half = GQA_HEAD_DIM // 2
    def axial(t):
        return jnp.concatenate([apply_rope(t[..., :half], row_cos, row_sin), apply_rope(t[..., half:], col_cos, col_sin)], axis=-1)
    q_g = axial(rms_norm(q_g, g_q_norm))
    k_g = axial(rms_norm(k_g, g_k_norm))
    o_g = dense_block_attention(q_g, k_g, v_g, GQA_HEAD_DIM ** -0.5)

    merged = jnp.concatenate([o_a.reshape(B, S, -1), o_g.reshape(B, S, -1)], axis=-1)
    return merged @ w_out


def odd_mixer(h, w_qkv, q_norm, k_norm, sink, w_out, swa_cos, swa_sin):
    B, S, _ = h.shape
    qkv = h @ w_qkv
    nq = SWA_HEADS * SWA_HEAD_DIM
    nk = SWA_KV_HEADS * SWA_HEAD_DIM
    q = rms_norm(qkv[..., :nq].reshape(B, S, SWA_HEADS, SWA_HEAD_DIM), q_norm)
    k = rms_norm(qkv[..., nq:nq + nk].reshape(B, S, SWA_KV_HEADS, SWA_HEAD_DIM), k_norm)
    v = qkv[..., nq + nk:].reshape(B, S, SWA_KV_HEADS, SWA_HEAD_DIM)
    q = jnp.concatenate([apply_rope(q[..., :SWA_ROT_DIM], swa_cos, swa_sin), q[..., SWA_ROT_DIM:]], axis=-1)
    k = jnp.concatenate([apply_rope(k[..., :SWA_ROT_DIM], swa_cos, swa_sin), k[..., SWA_ROT_DIM:]], axis=-1)
    o = banded_window_attention(q, k, v, sink, SWA_HEAD_DIM ** -0.5)
    return o.reshape(B, S, -1) @ w_out


def squared_relu_mlp(x, gain, w_up, w_down):
    h = rms_norm(x, gain) @ w_up
    return jnp.square(jax.nn.relu(h)) @ w_down


def _dense(k, shape):
    return jax.random.normal(k, shape, jnp.float32) * (shape[-2] ** -0.5)


def _gain(k, shape):
    return 1.0 + 0.02 * jax.random.normal(k, shape, jnp.float32)


def _fwd_setup_inputs(seed: int = 0) -> dict:
    key = jax.random.key(seed)
    ks = jax.random.split(key, 22)
    return {
        'x': jax.random.normal(ks[0], (BATCH, SEQ, D_MODEL), jnp.float32),
        'even_norm': _gain(ks[1], (N_EVEN, D_MODEL)),
        'even_w_in': _dense(ks[2], (N_EVEN, D_MODEL, EVEN_IN)),
        'mla_q_lat_norm': _gain(ks[3], (N_EVEN, MLA_Q_LORA)),
        'mla_kv_lat_norm': _gain(ks[4], (N_EVEN, MLA_KV_LORA)),
        'mla_w_uq': _dense(ks[5], (N_EVEN, MLA_Q_LORA, MLA_HEADS * MLA_QK_DIM)),
        'mla_w_ukv': _dense(ks[6], (N_EVEN, MLA_KV_LORA, MLA_HEADS * (MLA_NOPE_DIM + MLA_V_DIM))),
        'mla_q_norm': _gain(ks[7], (N_EVEN, MLA_QK_DIM)),
        'mla_k_nope_norm': _gain(ks[8], (N_EVEN, MLA_NOPE_DIM)),
        'mla_k_rope_norm': _gain(ks[9], (N_EVEN, MLA_ROPE_DIM)),
        'gqa_q_norm': _gain(ks[10], (N_EVEN, GQA_HEAD_DIM)),
        'gqa_k_norm': _gain(ks[11], (N_EVEN, GQA_HEAD_DIM)),
        'even_w_out': _dense(ks[12], (N_EVEN, EVEN_OUT, D_MODEL)),
        'odd_norm': _gain(ks[13], (N_ODD, D_MODEL)),
        'odd_w_qkv': _dense(ks[14], (N_ODD, D_MODEL, ODD_IN)),
        'swa_q_norm': _gain(ks[15], (N_ODD, SWA_HEAD_DIM)),
        'swa_k_norm': _gain(ks[16], (N_ODD, SWA_HEAD_DIM)),
        'swa_sink': jax.random.normal(ks[17], (N_ODD, SWA_HEADS), jnp.float32),
        'odd_w_out': _dense(ks[18], (N_ODD, ODD_OUT, D_MODEL)),
        'mlp_norm': _gain(ks[19], (DEPTH, D_MODEL)),
        'mlp_w_up': _dense(ks[20], (DEPTH, D_MODEL, D_FF)),
        'mlp_w_down': _dense(ks[21], (DEPTH, D_FF, D_MODEL)),
    }


def _fwd_reference(x, even_norm, even_w_in, mla_q_lat_norm, mla_kv_lat_norm, mla_w_uq, mla_w_ukv,
              mla_q_norm, mla_k_nope_norm, mla_k_rope_norm, gqa_q_norm, gqa_k_norm, even_w_out,
              odd_norm, odd_w_qkv, swa_q_norm, swa_k_norm, swa_sink, odd_w_out,
              mlp_norm, mlp_w_up, mlp_w_down):
    B, S, _ = x.shape
    rows = S // GRID_W
    pos = jnp.arange(S)
    row_pos = jnp.repeat(jnp.arange(rows), GRID_W)
    col_pos = jnp.tile(jnp.arange(GRID_W), rows)
    mla_cos, mla_sin = rope_table(pos, MLA_ROPE_DIM, ROPE_THETA)
    row_cos, row_sin = rope_table(row_pos, GQA_HEAD_DIM // 2, AXIAL_THETA)
    col_cos, col_sin = rope_table(col_pos, GQA_HEAD_DIM // 2, AXIAL_THETA)
    swa_cos, swa_sin = rope_table(pos, SWA_ROT_DIM, ROPE_THETA)
    for layer in range(DEPTH):
        i = layer // 2
        if layer % 2 == 0:
            x = x + even_mixer(rms_norm(x, even_norm[i]), even_w_in[i], mla_q_lat_norm[i], mla_kv_lat_norm[i],
                               mla_w_uq[i], mla_w_ukv[i], mla_q_norm[i], mla_k_nope_norm[i], mla_k_rope_norm[i],
                               gqa_q_norm[i], gqa_k_norm[i], even_w_out[i],
                               mla_cos, mla_sin, row_cos, row_sin, col_cos, col_sin)
        else:
            x = x + odd_mixer(rms_norm(x, odd_norm[i]), odd_w_qkv[i], swa_q_norm[i], swa_k_norm[i],
                              swa_sink[i], odd_w_out[i], swa_cos, swa_sin)
        x = x + squared_relu_mlp(x, mlp_norm[layer], mlp_w_up[layer], mlp_w_down[layer])
    return x


import jax as _jax
import jax.numpy as _jnp

TWIN_FORMAT = 'train_step'
FWD_PARAMS = ['x', 'even_norm', 'even_w_in', 'mla_q_lat_norm', 'mla_kv_lat_norm', 'mla_w_uq', 'mla_w_ukv', 'mla_q_norm', 'mla_k_nope_norm', 'mla_k_rope_norm', 'gqa_q_norm', 'gqa_k_norm', 'even_w_out', 'odd_norm', 'odd_w_qkv', 'swa_q_norm', 'swa_k_norm', 'swa_sink', 'odd_w_out', 'mlp_norm', 'mlp_w_up', 'mlp_w_down']
TWIN_WEIGHTS = ['even_norm', 'even_w_in', 'mla_q_lat_norm', 'mla_kv_lat_norm', 'mla_w_uq', 'mla_w_ukv', 'mla_q_norm', 'mla_k_nope_norm', 'mla_k_rope_norm', 'gqa_q_norm', 'gqa_k_norm', 'even_w_out', 'odd_norm', 'odd_w_qkv', 'swa_q_norm', 'swa_k_norm', 'swa_sink', 'odd_w_out', 'mlp_norm', 'mlp_w_up', 'mlp_w_down']
TWIN_DIFF_INPUT = 'x'
TWIN_INPUTS = ['x', 'even_norm', 'even_w_in', 'mla_q_lat_norm', 'mla_kv_lat_norm', 'mla_w_uq', 'mla_w_ukv', 'mla_q_norm', 'mla_k_nope_norm', 'mla_k_rope_norm', 'gqa_q_norm', 'gqa_k_norm', 'even_w_out', 'odd_norm', 'odd_w_qkv', 'swa_q_norm', 'swa_k_norm', 'swa_sink', 'odd_w_out', 'mlp_norm', 'mlp_w_up', 'mlp_w_down', 'loss_target', 'm_even_norm', 'm_even_w_in', 'm_mla_q_lat_norm', 'm_mla_kv_lat_norm', 'm_mla_w_uq', 'm_mla_w_ukv', 'm_mla_q_norm', 'm_mla_k_nope_norm', 'm_mla_k_rope_norm', 'm_gqa_q_norm', 'm_gqa_k_norm', 'm_even_w_out', 'm_odd_norm', 'm_odd_w_qkv', 'm_swa_q_norm', 'm_swa_k_norm', 'm_swa_sink', 'm_odd_w_out', 'm_mlp_norm', 'm_mlp_w_up', 'm_mlp_w_down', 'v_even_norm', 'v_even_w_in', 'v_mla_q_lat_norm', 'v_mla_kv_lat_norm', 'v_mla_w_uq', 'v_mla_w_ukv', 'v_mla_q_norm', 'v_mla_k_nope_norm', 'v_mla_k_rope_norm', 'v_gqa_q_norm', 'v_gqa_k_norm', 'v_even_w_out', 'v_odd_norm', 'v_odd_w_qkv', 'v_swa_q_norm', 'v_swa_k_norm', 'v_swa_sink', 'v_odd_w_out', 'v_mlp_norm', 'v_mlp_w_up', 'v_mlp_w_down']
TWIN_OUTPUTS = ['loss', 'grad_x', 'grad_even_norm', 'grad_even_w_in', 'grad_mla_q_lat_norm', 'grad_mla_kv_lat_norm', 'grad_mla_w_uq', 'grad_mla_w_ukv', 'grad_mla_q_norm', 'grad_mla_k_nope_norm', 'grad_mla_k_rope_norm', 'grad_gqa_q_norm', 'grad_gqa_k_norm', 'grad_even_w_out', 'grad_odd_norm', 'grad_odd_w_qkv', 'grad_swa_q_norm', 'grad_swa_k_norm', 'grad_swa_sink', 'grad_odd_w_out', 'grad_mlp_norm', 'grad_mlp_w_up', 'grad_mlp_w_down', 'delta_even_norm', 'delta_even_w_in', 'delta_mla_q_lat_norm', 'delta_mla_kv_lat_norm', 'delta_mla_w_uq', 'delta_mla_w_ukv', 'delta_mla_q_norm', 'delta_mla_k_nope_norm', 'delta_mla_k_rope_norm', 'delta_gqa_q_norm', 'delta_gqa_k_norm', 'delta_even_w_out', 'delta_odd_norm', 'delta_odd_w_qkv', 'delta_swa_q_norm', 'delta_swa_k_norm', 'delta_swa_sink', 'delta_odd_w_out', 'delta_mlp_norm', 'delta_mlp_w_up', 'delta_mlp_w_down', 'new_m_even_norm', 'new_m_even_w_in', 'new_m_mla_q_lat_norm', 'new_m_mla_kv_lat_norm', 'new_m_mla_w_uq', 'new_m_mla_w_ukv', 'new_m_mla_q_norm', 'new_m_mla_k_nope_norm', 'new_m_mla_k_rope_norm', 'new_m_gqa_q_norm', 'new_m_gqa_k_norm', 'new_m_even_w_out', 'new_m_odd_norm', 'new_m_odd_w_qkv', 'new_m_swa_q_norm', 'new_m_swa_k_norm', 'new_m_swa_sink', 'new_m_odd_w_out', 'new_m_mlp_norm', 'new_m_mlp_w_up', 'new_m_mlp_w_down', 'new_v_even_norm', 'new_v_even_w_in', 'new_v_mla_q_lat_norm', 'new_v_mla_kv_lat_norm', 'new_v_mla_w_uq', 'new_v_mla_w_ukv', 'new_v_mla_q_norm', 'new_v_mla_k_nope_norm', 'new_v_mla_k_rope_norm', 'new_v_gqa_q_norm', 'new_v_gqa_k_norm', 'new_v_even_w_out', 'new_v_odd_norm', 'new_v_odd_w_qkv', 'new_v_swa_q_norm', 'new_v_swa_k_norm', 'new_v_swa_sink', 'new_v_odd_w_out', 'new_v_mlp_norm', 'new_v_mlp_w_up', 'new_v_mlp_w_down']
TWIN_LEAF_KINDS = {'loss': 'loss', 'grad_x': 'grad_x', 'grad_even_norm': 'grad_w', 'grad_even_w_in': 'grad_w', 'grad_mla_q_lat_norm': 'grad_w', 'grad_mla_kv_lat_norm': 'grad_w', 'grad_mla_w_uq': 'grad_w', 'grad_mla_w_ukv': 'grad_w', 'grad_mla_q_norm': 'grad_w', 'grad_mla_k_nope_norm': 'grad_w', 'grad_mla_k_rope_norm': 'grad_w', 'grad_gqa_q_norm': 'grad_w', 'grad_gqa_k_norm': 'grad_w', 'grad_even_w_out': 'grad_w', 'grad_odd_norm': 'grad_w', 'grad_odd_w_qkv': 'grad_w', 'grad_swa_q_norm': 'grad_w', 'grad_swa_k_norm': 'grad_w', 'grad_swa_sink': 'grad_w', 'grad_odd_w_out': 'grad_w', 'grad_mlp_norm': 'grad_w', 'grad_mlp_w_up': 'grad_w', 'grad_mlp_w_down': 'grad_w', 'delta_even_norm': 'delta_w', 'delta_even_w_in': 'delta_w', 'delta_mla_q_lat_norm': 'delta_w', 'delta_mla_kv_lat_norm': 'delta_w', 'delta_mla_w_uq': 'delta_w', 'delta_mla_w_ukv': 'delta_w', 'delta_mla_q_norm': 'delta_w', 'delta_mla_k_nope_norm': 'delta_w', 'delta_mla_k_rope_norm': 'delta_w', 'delta_gqa_q_norm': 'delta_w', 'delta_gqa_k_norm': 'delta_w', 'delta_even_w_out': 'delta_w', 'delta_odd_norm': 'delta_w', 'delta_odd_w_qkv': 'delta_w', 'delta_swa_q_norm': 'delta_w', 'delta_swa_k_norm': 'delta_w', 'delta_swa_sink': 'delta_w', 'delta_odd_w_out': 'delta_w', 'delta_mlp_norm': 'delta_w', 'delta_mlp_w_up': 'delta_w', 'delta_mlp_w_down': 'delta_w', 'new_m_even_norm': 'new_m', 'new_m_even_w_in': 'new_m', 'new_m_mla_q_lat_norm': 'new_m', 'new_m_mla_kv_lat_norm': 'new_m', 'new_m_mla_w_uq': 'new_m', 'new_m_mla_w_ukv': 'new_m', 'new_m_mla_q_norm': 'new_m', 'new_m_mla_k_nope_norm': 'new_m', 'new_m_mla_k_rope_norm': 'new_m', 'new_m_gqa_q_norm': 'new_m', 'new_m_gqa_k_norm': 'new_m', 'new_m_even_w_out': 'new_m', 'new_m_odd_norm': 'new_m', 'new_m_odd_w_qkv': 'new_m', 'new_m_swa_q_norm': 'new_m', 'new_m_swa_k_norm': 'new_m', 'new_m_swa_sink': 'new_m', 'new_m_odd_w_out': 'new_m', 'new_m_mlp_norm': 'new_m', 'new_m_mlp_w_up': 'new_m', 'new_m_mlp_w_down': 'new_m', 'new_v_even_norm': 'new_v', 'new_v_even_w_in': 'new_v', 'new_v_mla_q_lat_norm': 'new_v', 'new_v_mla_kv_lat_norm': 'new_v', 'new_v_mla_w_uq': 'new_v', 'new_v_mla_w_ukv': 'new_v', 'new_v_mla_q_norm': 'new_v', 'new_v_mla_k_nope_norm': 'new_v', 'new_v_mla_k_rope_norm': 'new_v', 'new_v_gqa_q_norm': 'new_v', 'new_v_gqa_k_norm': 'new_v', 'new_v_even_w_out': 'new_v', 'new_v_odd_norm': 'new_v', 'new_v_odd_w_qkv': 'new_v', 'new_v_swa_q_norm': 'new_v', 'new_v_swa_k_norm': 'new_v', 'new_v_swa_sink': 'new_v', 'new_v_odd_w_out': 'new_v', 'new_v_mlp_norm': 'new_v', 'new_v_mlp_w_up': 'new_v', 'new_v_mlp_w_down': 'new_v'}


def _forward(args):
    return _fwd_reference(*[args[k] for k in FWD_PARAMS])


def _output_shape():
    def fwd():
        inp = _fwd_setup_inputs(0)
        return _fwd_reference(*[inp[k] for k in FWD_PARAMS])
    out = _jax.eval_shape(fwd)
    return out.shape, out.dtype

N_MICROBATCH = 1
ADAM_LR = 0.001
ADAM_B1 = 0.9
ADAM_B2 = 0.999
ADAM_EPS = 1e-08
ADAM_WD = 0.01
ADAM_STEP = 10
PER_EXAMPLE_BATCH_AXIS = {'x': 0, 'loss_target': 0}
SHARED_INPUTS = []
_WEIGHT_DTYPES = {'even_norm': _jnp.float32, 'even_w_in': _jnp.float32, 'mla_q_lat_norm': _jnp.float32, 'mla_kv_lat_norm': _jnp.float32, 'mla_w_uq': _jnp.float32, 'mla_w_ukv': _jnp.float32, 'mla_q_norm': _jnp.float32, 'mla_k_nope_norm': _jnp.float32, 'mla_k_rope_norm': _jnp.float32, 'gqa_q_norm': _jnp.float32, 'gqa_k_norm': _jnp.float32, 'even_w_out': _jnp.float32, 'odd_norm': _jnp.float32, 'odd_w_qkv': _jnp.float32, 'swa_q_norm': _jnp.float32, 'swa_k_norm': _jnp.float32, 'swa_sink': _jnp.float32, 'odd_w_out': _jnp.float32, 'mlp_norm': _jnp.float32, 'mlp_w_up': _jnp.float32, 'mlp_w_down': _jnp.float32}
MOMENT_SCALE = {'even_norm': 3.305550e-01, 'even_w_in': 3.023924e-01, 'mla_q_lat_norm': 1.383440e-01, 'mla_kv_lat_norm': 7.698040e-01, 'mla_w_uq': 8.069713e-02, 'mla_w_ukv': 2.216988e-01, 'mla_q_norm': 4.146983e-01, 'mla_k_nope_norm': 4.661663e-01, 'mla_k_rope_norm': 3.330916e-01, 'gqa_q_norm': 4.817140e-01, 'gqa_k_norm': 4.847268e-01, 'even_w_out': 2.943859e-01, 'odd_norm': 1.461158e+01, 'odd_w_qkv': 1.254014e+01, 'swa_q_norm': 6.059946e+00, 'swa_k_norm': 6.049205e+00, 'swa_sink': 8.975850e-01, 'odd_w_out': 1.104806e+01, 'mlp_norm': 9.888615e+01, 'mlp_w_up': 4.856297e+00, 'mlp_w_down': 2.035336e+01}


def _to_microbatches(a, axis):
    t = _jnp.moveaxis(a, axis, 0)
    t = t.reshape((N_MICROBATCH, t.shape[0] // N_MICROBATCH) + t.shape[1:])
    return _jnp.moveaxis(t, 1, axis + 1)


def setup_inputs(seed: int = 0) -> dict:
    inp = _fwd_setup_inputs(seed)
    key = _jax.random.fold_in(_jax.random.key(seed), 7919)
    shape, _ = _output_shape()
    out = dict(inp)
    out["loss_target"] = _jax.random.normal(_jax.random.fold_in(key, 0), shape, _jnp.float32)
    for i, name in enumerate(TWIN_WEIGHTS):
        w = inp[name].astype(_jnp.float32)
        if MOMENT_SCALE is None:
            s = _jnp.sqrt(_jnp.mean(_jnp.square(w)) + 1e-30)
        else:
            s = MOMENT_SCALE[name]
        km, kv = _jax.random.split(_jax.random.fold_in(key, i + 1))
        out[name] = w
        out["m_" + name] = s * _jax.random.normal(km, w.shape, _jnp.float32)
        out["v_" + name] = (s * s) * _jax.random.uniform(kv, w.shape, _jnp.float32, 0.5, 1.5)
    if N_MICROBATCH > 1:
        for name, axis in PER_EXAMPLE_BATCH_AXIS.items():
            out[name] = _to_microbatches(out[name], axis)
    return {'x': out['x'], 'even_norm': out['even_norm'], 'even_w_in': out['even_w_in'], 'mla_q_lat_norm': out['mla_q_lat_norm'], 'mla_kv_lat_norm': out['mla_kv_lat_norm'], 'mla_w_uq': out['mla_w_uq'], 'mla_w_ukv': out['mla_w_ukv'], 'mla_q_norm': out['mla_q_norm'], 'mla_k_nope_norm': out['mla_k_nope_norm'], 'mla_k_rope_norm': out['mla_k_rope_norm'], 'gqa_q_norm': out['gqa_q_norm'], 'gqa_k_norm': out['gqa_k_norm'], 'even_w_out': out['even_w_out'], 'odd_norm': out['odd_norm'], 'odd_w_qkv': out['odd_w_qkv'], 'swa_q_norm': out['swa_q_norm'], 'swa_k_norm': out['swa_k_norm'], 'swa_sink': out['swa_sink'], 'odd_w_out': out['odd_w_out'], 'mlp_norm': out['mlp_norm'], 'mlp_w_up': out['mlp_w_up'], 'mlp_w_down': out['mlp_w_down'], 'loss_target': out['loss_target'], 'm_even_norm': out['m_even_norm'], 'm_even_w_in': out['m_even_w_in'], 'm_mla_q_lat_norm': out['m_mla_q_lat_norm'], 'm_mla_kv_lat_norm': out['m_mla_kv_lat_norm'], 'm_mla_w_uq': out['m_mla_w_uq'], 'm_mla_w_ukv': out['m_mla_w_ukv'], 'm_mla_q_norm': out['m_mla_q_norm'], 'm_mla_k_nope_norm': out['m_mla_k_nope_norm'], 'm_mla_k_rope_norm': out['m_mla_k_rope_norm'], 'm_gqa_q_norm': out['m_gqa_q_norm'], 'm_gqa_k_norm': out['m_gqa_k_norm'], 'm_even_w_out': out['m_even_w_out'], 'm_odd_norm': out['m_odd_norm'], 'm_odd_w_qkv': out['m_odd_w_qkv'], 'm_swa_q_norm': out['m_swa_q_norm'], 'm_swa_k_norm': out['m_swa_k_norm'], 'm_swa_sink': out['m_swa_sink'], 'm_odd_w_out': out['m_odd_w_out'], 'm_mlp_norm': out['m_mlp_norm'], 'm_mlp_w_up': out['m_mlp_w_up'], 'm_mlp_w_down': out['m_mlp_w_down'], 'v_even_norm': out['v_even_norm'], 'v_even_w_in': out['v_even_w_in'], 'v_mla_q_lat_norm': out['v_mla_q_lat_norm'], 'v_mla_kv_lat_norm': out['v_mla_kv_lat_norm'], 'v_mla_w_uq': out['v_mla_w_uq'], 'v_mla_w_ukv': out['v_mla_w_ukv'], 'v_mla_q_norm': out['v_mla_q_norm'], 'v_mla_k_nope_norm': out['v_mla_k_nope_norm'], 'v_mla_k_rope_norm': out['v_mla_k_rope_norm'], 'v_gqa_q_norm': out['v_gqa_q_norm'], 'v_gqa_k_norm': out['v_gqa_k_norm'], 'v_even_w_out': out['v_even_w_out'], 'v_odd_norm': out['v_odd_norm'], 'v_odd_w_qkv': out['v_odd_w_qkv'], 'v_swa_q_norm': out['v_swa_q_norm'], 'v_swa_k_norm': out['v_swa_k_norm'], 'v_swa_sink': out['v_swa_sink'], 'v_odd_w_out': out['v_odd_w_out'], 'v_mlp_norm': out['v_mlp_norm'], 'v_mlp_w_up': out['v_mlp_w_up'], 'v_mlp_w_down': out['v_mlp_w_down']}


def _loss(weights, diff, rest, loss_target):
    with _jax.named_scope("forward"):
        args = {**rest, TWIN_DIFF_INPUT: diff, **{k: w.astype(_WEIGHT_DTYPES[k]) for k, w in weights.items()}}
        y = _forward(args)
    with _jax.named_scope("loss_head"):
        err = _jnp.square(y.astype(_jnp.float32) - loss_target)
        return 0.5 * _jnp.sum(_jnp.mean(err, axis=-1)) if err.ndim else 0.5 * err


def _adamw(w, g, m, v):
    m = ADAM_B1 * m + (1.0 - ADAM_B1) * g
    v = ADAM_B2 * v + (1.0 - ADAM_B2) * _jnp.square(g)
    m_hat = m / (1.0 - ADAM_B1 ** ADAM_STEP)
    v_hat = v / (1.0 - ADAM_B2 ** ADAM_STEP)
    delta = -ADAM_LR * (m_hat / (_jnp.sqrt(v_hat) + ADAM_EPS) + ADAM_WD * w)
    return delta, m, v


def reference(x, even_norm, even_w_in, mla_q_lat_norm, mla_kv_lat_norm, mla_w_uq, mla_w_ukv, mla_q_norm, mla_k_nope_norm, mla_k_rope_norm, gqa_q_norm, gqa_k_norm, even_w_out, odd_norm, odd_w_qkv, swa_q_norm, swa_k_norm, swa_sink, odd_w_out, mlp_norm, mlp_w_up, mlp_w_down, loss_target, m_even_norm, m_even_w_in, m_mla_q_lat_norm, m_mla_kv_lat_norm, m_mla_w_uq, m_mla_w_ukv, m_mla_q_norm, m_mla_k_nope_norm, m_mla_k_rope_norm, m_gqa_q_norm, m_gqa_k_norm, m_even_w_out, m_odd_norm, m_odd_w_qkv, m_swa_q_norm, m_swa_k_norm, m_swa_sink, m_odd_w_out, m_mlp_norm, m_mlp_w_up, m_mlp_w_down, v_even_norm, v_even_w_in, v_mla_q_lat_norm, v_mla_kv_lat_norm, v_mla_w_uq, v_mla_w_ukv, v_mla_q_norm, v_mla_k_nope_norm, v_mla_k_rope_norm, v_gqa_q_norm, v_gqa_k_norm, v_even_w_out, v_odd_norm, v_odd_w_qkv, v_swa_q_norm, v_swa_k_norm, v_swa_sink, v_odd_w_out, v_mlp_norm, v_mlp_w_up, v_mlp_w_down):
    given = dict(x=x, even_norm=even_norm, even_w_in=even_w_in, mla_q_lat_norm=mla_q_lat_norm, mla_kv_lat_norm=mla_kv_lat_norm, mla_w_uq=mla_w_uq, mla_w_ukv=mla_w_ukv, mla_q_norm=mla_q_norm, mla_k_nope_norm=mla_k_nope_norm, mla_k_rope_norm=mla_k_rope_norm, gqa_q_norm=gqa_q_norm, gqa_k_norm=gqa_k_norm, even_w_out=even_w_out, odd_norm=odd_norm, odd_w_qkv=odd_w_qkv, swa_q_norm=swa_q_norm, swa_k_norm=swa_k_norm, swa_sink=swa_sink, odd_w_out=odd_w_out, mlp_norm=mlp_norm, mlp_w_up=mlp_w_up, mlp_w_down=mlp_w_down, loss_target=loss_target, m_even_norm=m_even_norm, m_even_w_in=m_even_w_in, m_mla_q_lat_norm=m_mla_q_lat_norm, m_mla_kv_lat_norm=m_mla_kv_lat_norm, m_mla_w_uq=m_mla_w_uq, m_mla_w_ukv=m_mla_w_ukv, m_mla_q_norm=m_mla_q_norm, m_mla_k_nope_norm=m_mla_k_nope_norm, m_mla_k_rope_norm=m_mla_k_rope_norm, m_gqa_q_norm=m_gqa_q_norm, m_gqa_k_norm=m_gqa_k_norm, m_even_w_out=m_even_w_out, m_odd_norm=m_odd_norm, m_odd_w_qkv=m_odd_w_qkv, m_swa_q_norm=m_swa_q_norm, m_swa_k_norm=m_swa_k_norm, m_swa_sink=m_swa_sink, m_odd_w_out=m_odd_w_out, m_mlp_norm=m_mlp_norm, m_mlp_w_up=m_mlp_w_up, m_mlp_w_down=m_mlp_w_down, v_even_norm=v_even_norm, v_even_w_in=v_even_w_in, v_mla_q_lat_norm=v_mla_q_lat_norm, v_mla_kv_lat_norm=v_mla_kv_lat_norm, v_mla_w_uq=v_mla_w_uq, v_mla_w_ukv=v_mla_w_ukv, v_mla_q_norm=v_mla_q_norm, v_mla_k_nope_norm=v_mla_k_nope_norm, v_mla_k_rope_norm=v_mla_k_rope_norm, v_gqa_q_norm=v_gqa_q_norm, v_gqa_k_norm=v_gqa_k_norm, v_even_w_out=v_even_w_out, v_odd_norm=v_odd_norm, v_odd_w_qkv=v_odd_w_qkv, v_swa_q_norm=v_swa_q_norm, v_swa_k_norm=v_swa_k_norm, v_swa_sink=v_swa_sink, v_odd_w_out=v_odd_w_out, v_mlp_norm=v_mlp_norm, v_mlp_w_up=v_mlp_w_up, v_mlp_w_down=v_mlp_w_down)
    weights = {n: given[n] for n in TWIN_WEIGHTS}
    shared = {n: given[n] for n in SHARED_INPUTS}
    per_example = {n: given[n] for n in ['x']}
    grad_fn = _jax.value_and_grad(_loss, argnums=(0, 1))

    def one_microbatch(ex, loss_target):
        ex = dict(ex)
        diff = ex.pop(TWIN_DIFF_INPUT)
        return grad_fn(weights, diff, {**shared, **ex}, loss_target)

    if N_MICROBATCH == 1:
        loss, (grad_w, grad_x) = one_microbatch(per_example, given["loss_target"])
    else:
        def body(carry, xs):
            loss_sum, grad_sum = carry
            l_k, (gw_k, gx_k) = one_microbatch(xs[0], xs[1])
            with _jax.named_scope("update"):
                return (loss_sum + l_k, _jax.tree.map(_jnp.add, grad_sum, gw_k)), gx_k

        init = (_jnp.zeros((), _jnp.float32), _jax.tree.map(_jnp.zeros_like, weights))
        (loss, grad_w), grad_x = _jax.lax.scan(body, init, (per_example, given["loss_target"]))
    with _jax.named_scope("update"):
        delta_w, new_m, new_v = {}, {}, {}
        for n in TWIN_WEIGHTS:
            delta_w[n], new_m[n], new_v[n] = _adamw(weights[n], grad_w[n], given["m_" + n], given["v_" + n])
    return (loss, grad_x, *[grad_w[n] for n in TWIN_WEIGHTS], *[delta_w[n] for n in TWIN_WEIGHTS],
            *[new_m[n] for n in TWIN_WEIGHTS], *[new_v[n] for n in TWIN_WEIGHTS])
```

```python
import functools

import jax
import jax.numpy as jnp
from jax import lax
from jax.experimental import pallas as pl
from jax.experimental.pallas import tpu as pltpu

F32 = jnp.float32
BF16 = jnp.bfloat16

N_DEV = 8
NORM_EPS = 1e-6
ROPE_THETA = 500000.0
AXIAL_THETA = 10000.0
GRID_W = 64
HEADS = 8
GQA_KV = 2
HD = 128
MLA_ROPE = 64
MLA_QK = HD + MLA_ROPE
Q_LORA = 512
KV_LORA = 256
SWA_HEADS = 32
SWA_KV = 4
SWA_D = 64
SWA_ROT = 16
SWA_WINDOW = 128
SWA_BLOCK = 128
LANES = 128
ADAM_LR, ADAM_B1, ADAM_B2, ADAM_EPS, ADAM_WD, ADAM_STEP = 0.001, 0.9, 0.999, 1e-08, 0.01, 10
VMEM_LIMIT = 56 * 1024 * 1024

P_CQ, P_CKV, P_KR, P_QG = 0, Q_LORA, Q_LORA + KV_LORA, Q_LORA + KV_LORA + LANES
P_KG = P_QG + HEADS * HD
P_VG = P_KG + GQA_KV * HD
P_END = P_VG + GQA_KV * HD
KR_END = Q_LORA + KV_LORA + MLA_ROPE


def _tile(n, prefs):
    for t in prefs:
        if n % t == 0 and t <= n:
            return t
    return n


def _params(sem):
    return pltpu.CompilerParams(dimension_semantics=sem, vmem_limit_bytes=VMEM_LIMIT)


_DIMS = {"nn": ((1,), (0,)), "nt": ((1,), (1,)), "tn": ((0,), (0,))}


def _mm(name, a, b, mode, out_dtypes, epilogue=None, extras=(), tm=1024, tn=1024, tk=512):
    if mode == "nn":
        (M, K), (_, N) = a.shape, b.shape
    elif mode == "nt":
        (M, K), (N, _) = a.shape, b.shape
    else:
        (K, M), (_, N) = a.shape, b.shape
    tm = _tile(M, (tm, 512, 256, 128))
    tn = _tile(N, (tn, 512, 256, 128))
    tk = _tile(K, (tk, 256, 128))
    nk = K // tk
    ne, no = len(extras), len(out_dtypes)
    if mode == "tn":
        a_spec = pl.BlockSpec((tk, tm), lambda i, j, k: (k, i))
    else:
        a_spec = pl.BlockSpec((tm, tk), lambda i, j, k: (i, k))
    if mode == "nt":
        b_spec = pl.BlockSpec((tn, tk), lambda i, j, k: (j, k))
    else:
        b_spec = pl.BlockSpec((tk, tn), lambda i, j, k: (k, j))
    o_spec = pl.BlockSpec((tm, tn), lambda i, j, k: (i, j))
    dims = (_DIMS[mode], ((), ()))

    def body(a_ref, b_ref, *rest):
        ex, outs, acc = rest[:ne], rest[ne:ne + no], rest[ne + no]
        k = pl.program_id(2)

        @pl.when(k == 0)
        def _():
            acc[...] = jnp.zeros_like(acc)

        acc[...] += lax.dot_general(a_ref[...].astype(BF16), b_ref[...].astype(BF16), dims,
                                    preferred_element_type=F32)

        @pl.when(k == nk - 1)
        def _():
            res = epilogue(acc[...], *[e[...] for e in ex]) if epilogue else (acc[...],)
            for o, r in zip(outs, res):
                o[...] = r.astype(o.dtype)

    out = pl.pallas_call(
        body, name=name, grid=(M // tm, N // tn, nk),
        in_specs=[a_spec, b_spec] + [o_spec] * ne,
        out_specs=[o_spec] * no,
        out_shape=[jax.ShapeDtypeStruct((M, N), d) for d in out_dtypes],
        scratch_shapes=[pltpu.VMEM((tm, tn), F32)],
        compiler_params=_params(("parallel", "parallel", "arbitrary")),
    )(a, b, *extras)
    return out[0] if no == 1 else out


def _rowwise(name, fn, rows, consts, outs, accs=(), tm=256):
    S = rows[0].shape[-2]
    tm = _tile(S, (tm, 128, 64, 32, 16, 8))
    nr, nc, no, na = len(rows), len(consts), len(outs), len(accs)

    def rspec(shape):
        if len(shape) == 2:
            return pl.BlockSpec((tm, shape[1]), lambda i: (i, 0))
        return pl.BlockSpec((shape[0], tm, shape[2]), lambda i: (0, i, 0))

    def cspec(shape):
        return pl.BlockSpec(tuple(shape), lambda i: (0,) * len(shape))

    def body(*refs):
        r, c = refs[:nr], refs[nr:nr + nc]
        o, a = refs[nr + nc:nr + nc + no], refs[nr + nc + no:]
        vals = fn(r, c, o)
        if na:
            @pl.when(pl.program_id(0) == 0)
            def _():
                for ar in a:
                    ar[...] = jnp.zeros_like(ar)

            for ar, v in zip(a, vals):
                ar[...] += v

    res = pl.pallas_call(
        body, name=name, grid=(S // tm,),
        in_specs=[rspec(x.shape) for x in rows] + [cspec(x.shape) for x in consts],
        out_specs=[rspec(x.shape) for x in outs] + [cspec(x.shape) for x in accs],
        out_shape=list(outs) + list(accs),
        compiler_params=_params(("arbitrary",) if na else ("parallel",)),
    )(*rows, *consts)
    return res


def _sds(shape, dtype):
    return jax.ShapeDtypeStruct(tuple(shape), dtype)


def _rsum(x):
    return jnp.sum(x, axis=-1, keepdims=True)


def _csum(x):
    return jnp.sum(x, axis=0, keepdims=True)


def _lane(shape):
    return lax.broadcasted_iota(jnp.int32, shape, 1)


def _partner(x, seg, half):
    lane = _lane(x.shape) % seg
    return jnp.where(lane < half, pltpu.roll(x, LANES - half, 1), pltpu.roll(x, half, 1))


def _rope(x, c, s, seg, half):
    return x * c + _partner(x, seg, half) * s


def _rope_bwd(dy, c, s, seg, half):
    t = _partner(dy * s, seg, half)
    if seg != 2 * half:
        t = jnp.where(_lane(dy.shape) % seg < 2 * half, t, 0.0)
    return dy * c + t


def _rms_r(ss, n):
    return lax.rsqrt(ss * (1.0 / n) + NORM_EPS)


def _rms_bwd(x, r, g, dy, dot_scale):
    dyg = dy * g
    return r * dyg - x * (r * r * r) * (_rsum(dyg * x) * dot_scale)


def _rmsnorm(name, x, g):
    D = x.shape[1]

    def fn(r, c, o):
        xv = r[0][...]
        o[0][...] = (xv * _rms_r(_rsum(xv * xv), D) * c[0][...]).astype(BF16)
        return ()

    return _rowwise(name, fn, [x], [g], [_sds(x.shape, BF16)])[0]


def _rmsnorm_bwd(name, dres, x, dh, g):
    D = x.shape[1]

    def fn(r, c, o):
        xv, dhv, gv = r[1][...], r[2][...], c[0][...]
        rr = _rms_r(_rsum(xv * xv), D)
        o[0][...] = r[0][...] + _rms_bwd(xv, rr, gv, dhv, 1.0 / D)
        return (_csum(dhv * xv * rr),)

    return _rowwise(name, fn, [dres, x, dh], [g], [_sds(x.shape, F32)], [_sds((1, D), F32)])


def _loss(name, y, t):
    D = y.shape[1]

    def fn(r, c, o):
        e = r[0][...] - r[1][...]
        o[0][...] = e * (1.0 / D)
        return (_csum(_rsum(e * e)),)

    return _rowwise(name, fn, [y, t], [], [_sds(y.shape, F32)], [_sds((1, 1), F32)])


def _even_prep(name, proj, tabs, gains):
    S = proj.shape[0]
    cm, sm, ca, sa = tabs

    def fn(r, c, o):
        p, cmv, smv, cav, sav = r[0], r[1][...], r[2][...], r[3][...], r[4][...]
        g_ql, g_kvl, g_kr, g_q, g_k = (x[...] for x in c)
        cq = p[:, P_CQ:P_CKV]
        o[0][...] = (cq * _rms_r(_rsum(cq * cq), Q_LORA) * g_ql).astype(BF16)
        ckv = p[:, P_CKV:P_KR]
        o[1][...] = (ckv * _rms_r(_rsum(ckv * ckv), KV_LORA) * g_kvl).astype(BF16)
        kr = p[:, P_KR:P_QG]
        y = _rope(kr * _rms_r(_rsum(kr * kr), MLA_ROPE) * g_kr, cmv, smv, 64, 32)
        o[2][...] = (y + pltpu.roll(y, 64, 1)).astype(BF16)
        for h in range(HEADS):
            xh = p[:, P_QG + HD * h:P_QG + HD * (h + 1)]
            o[3][h] = _rope(xh * _rms_r(_rsum(xh * xh), HD) * g_q, cav, sav, 64, 32).astype(BF16)
        for h in range(GQA_KV):
            xh = p[:, P_KG + HD * h:P_KG + HD * (h + 1)]
            o[4][h] = _rope(xh * _rms_r(_rsum(xh * xh), HD) * g_k, cav, sav, 64, 32).astype(BF16)
            o[5][h] = p[:, P_VG + HD * h:P_VG + HD * (h + 1)].astype(BF16)
        return ()

    outs = [_sds((S, Q_LORA), BF16), _sds((S, KV_LORA), BF16), _sds((S, LANES), BF16),
            _sds((HEADS, S, HD), BF16), _sds((GQA_KV, S, HD), BF16), _sds((GQA_KV, S, HD), BF16)]
    return _rowwise(name, fn, [proj, cm, sm, ca, sa], gains, outs)


def _even_prep_bwd(name, dcqn, dckvn, dkrd, dqg, dkg, dvg, proj, tabs, gains):
    S = proj.shape[0]
    cm, sm, ca, sa = tabs

    def fn(r, c, o):
        p, cmv, smv, cav, sav = r[6], r[7][...], r[8][...], r[9][...], r[10][...]
        g_ql, g_kvl, g_kr, g_q, g_k = (x[...] for x in c)
        out = o[0]
        cq = p[:, P_CQ:P_CKV]
        rr = _rms_r(_rsum(cq * cq), Q_LORA)
        d = r[0][...]
        out[:, P_CQ:P_CKV] = _rms_bwd(cq, rr, g_ql, d, 1.0 / Q_LORA).astype(BF16)
        a_ql = _csum(d * cq * rr)
        ckv = p[:, P_CKV:P_KR]
        rr = _rms_r(_rsum(ckv * ckv), KV_LORA)
        d = r[1][...]
        out[:, P_CKV:P_KR] = _rms_bwd(ckv, rr, g_kvl, d, 1.0 / KV_LORA).astype(BF16)
        a_kvl = _csum(d * ckv * rr)
        kr = p[:, P_KR:P_QG]
        rr = _rms_r(_rsum(kr * kr), MLA_ROPE)
        d = r[2][...]
        d = d + pltpu.roll(d, 64, 1)
        d = _rope_bwd(d, cmv, smv, 64, 32)
        low = _lane(d.shape) < 64
        out[:, P_KR:P_QG] = jnp.where(low, _rms_bwd(kr, rr, g_kr, d, 1.0 / MLA_ROPE), 0.0).astype(BF16)
        a_kr = _csum(d * kr * rr)
        a_q = jnp.zeros((1, HD), F32)
        for h in range(HEADS):
            xh = p[:, P_QG + HD * h:P_QG + HD * (h + 1)]
            rr = _rms_r(_rsum(xh * xh), HD)
            d = _rope_bwd(r[3][h], cav, sav, 64, 32)
            out[:, P_QG + HD * h:P_QG + HD * (h + 1)] = _rms_bwd(xh, rr, g_q, d, 1.0 / HD).astype(BF16)
            a_q = a_q + _csum(d * xh * rr)
        a_k = jnp.zeros((1, HD), F32)
        grp = HEADS // GQA_KV
        for h in range(GQA_KV):
            xh = p[:, P_KG + HD * h:P_KG + HD * (h + 1)]
            rr = _rms_r(_rsum(xh * xh), HD)
            d = r[4][grp * h]
            dv = r[5][grp * h]
            for e in range(1, grp):
                d = d + r[4][grp * h + e]
                dv = dv + r[5][grp * h + e]
            d = _rope_bwd(d, cav, sav, 64, 32)
            out[:, P_KG + HD * h:P_KG + HD * (h + 1)] = _rms_bwd(xh, rr, g_k, d, 1.0 / HD).astype(BF16)
            out[:, P_VG + HD * h:P_VG + HD * (h + 1)] = dv.astype(BF16)
            a_k = a_k + _csum(d * xh * rr)
        return (a_ql, a_kvl, a_kr, a_q, a_k)

    accs = [_sds((1, Q_LORA), F32), _sds((1, KV_LORA), F32), _sds((1, LANES), F32),
            _sds((1, HD), F32), _sds((1, HD), F32)]
    return _rowwise(name, fn, [dcqn, dckvn, dkrd, dqg, dkg, dvg, proj, cm, sm, ca, sa], gains,
                    [_sds((S, P_END), BF16)], accs, tm=128)


def _mla_q_parts(qa, h, rp, g_n, g_r):
    nope = qa[:, HD * h:HD * (h + 1)]
    mine = (_lane(rp.shape) >= 64) == bool(h % 2)
    rpm = jnp.where(mine, rp, 0.0)
    rr = _rms_r(_rsum(nope * nope) + _rsum(rpm * rpm), MLA_QK)
    return nope, rpm, mine, rr


def _mla_prep(name, qa, kv, krd, tabs, gains):
    S = qa.shape[0]
    cm, sm = tabs

    def fn(r, c, o):
        qa_r, kv_r, krd_v, cmv, smv = r[0], r[1], r[2][...], r[3][...], r[4][...]
        g_n, g_r, g_kn = (x[...] for x in c)
        for h in range(HEADS):
            rp = qa_r[:, HEADS * HD + LANES * (h // 2):HEADS * HD + LANES * (h // 2 + 1)]
            nope, rpm, mine, rr = _mla_q_parts(qa_r, h, rp, g_n, g_r)
            o[0][h, :, 0:HD] = (nope * rr * g_n).astype(BF16)
            o[0][h, :, HD:2 * HD] = _rope(rpm * rr * g_r, cmv, smv, 64, 32).astype(BF16)
            kn = kv_r[:, 2 * HD * h:2 * HD * h + HD]
            o[1][h, :, 0:HD] = (kn * _rms_r(_rsum(kn * kn), HD) * g_kn).astype(BF16)
            o[1][h, :, HD:2 * HD] = krd_v
            o[2][h] = kv_r[:, 2 * HD * h + HD:2 * HD * (h + 1)].astype(BF16)
        return ()

    outs = [_sds((HEADS, S, 2 * HD), BF16), _sds((HEADS, S, 2 * HD), BF16), _sds((HEADS, S, HD), BF16)]
    return _rowwise(name, fn, [qa, kv, krd, cm, sm], gains, outs)


def _mla_prep_bwd(name, dq, dk, dv, qa, kv, tabs, gains):
    S = qa.shape[0]
    cm, sm = tabs

    def fn(r, c, o):
        dq_r, dk_r, dv_r, qa_r, kv_r, cmv, smv = r[0], r[1], r[2], r[3], r[4], r[5][...], r[6][...]
        g_n, g_r, g_kn = (x[...] for x in c)
        a_n = jnp.zeros((1, HD), F32)
        a_r = jnp.zeros((1, LANES), F32)
        a_kn = jnp.zeros((1, HD), F32)
        dkrd = jnp.zeros(cmv.shape, F32)
        drp = None
        for h in range(HEADS):
            rp = qa_r[:, HEADS * HD + LANES * (h // 2):HEADS * HD + LANES * (h // 2 + 1)]
            nope, rpm, mine, rr = _mla_q_parts(qa_r, h, rp, g_n, g_r)
            dn = dq_r[h, :, 0:HD]
            dr = _rope_bwd(jnp.where(mine, dq_r[h, :, HD:2 * HD], 0.0), cmv, smv, 64, 32)
            dot = (_rsum(dn * g_n * nope) + _rsum(dr * g_r * rpm)) * (1.0 / MLA_QK)
            r3 = rr * rr * rr
            o[0][:, HD * h:HD * (h + 1)] = (rr * dn * g_n - nope * r3 * dot).astype(BF16)
            part = jnp.where(mine, rr * dr * g_r - rpm * r3 * dot, 0.0)
            drp = part if h % 2 == 0 else drp + part
            if h % 2 == 1:
                o[0][:, HEADS * HD + LANES * (h // 2):HEADS * HD + LANES * (h // 2 + 1)] = drp.astype(BF16)
            a_n = a_n + _csum(dn * nope * rr)
            a_r = a_r + _csum(dr * rpm * rr)
            kn = kv_r[:, 2 * HD * h:2 * HD * h + HD]
            rk = _rms_r(_rsum(kn * kn), HD)
            dkn = dk_r[h, :, 0:HD]
            o[1][:, 2 * HD * h:2 * HD * h + HD] = _rms_bwd(kn, rk, g_kn, dkn, 1.0 / HD).astype(BF16)
            o[1][:, 2 * HD * h + HD:2 * HD * (h + 1)] = dv_r[h].astype(BF16)
            a_kn = a_kn + _csum(dkn * kn * rk)
            dkrd = dkrd + dk_r[h, :, HD:2 * HD]
        o[2][...] = dkrd
        return (a_n, a_r, a_kn)

    outs = [_sds(qa.shape, BF16), _sds(kv.shape, BF16), _sds((S, LANES), F32)]
    accs = [_sds((1, HD), F32), _sds((1, LANES), F32), _sds((1, HD), F32)]
    return _rowwise(name, fn, [dq, dk, dv, qa, kv, cm, sm], gains, outs, accs, tm=128)


def _seg64_r(x):
    low = _lane(x.shape) < 64
    x2 = x * x
    s0 = _rsum(jnp.where(low, x2, 0.0))
    s1 = _rsum(jnp.where(low, 0.0, x2))
    return jnp.where(low, _rms_r(s0, SWA_D), _rms_r(s1, SWA_D)), low


def _swa_prep(name, qkv, tabs, gains):
    S = qkv.shape[0]
    nq, nk = SWA_HEADS * SWA_D, SWA_KV * SWA_D
    cs, ss = tabs

    def fn(r, c, o):
        x_r, csv, ssv = r[0], r[1][...], r[2][...]
        g_q, g_k = c[0][...], c[1][...]
        for g in range((nq + nk) // LANES):
            x = x_r[:, LANES * g:LANES * (g + 1)]
            rr, _ = _seg64_r(x)
            y = _rope(x * rr * (g_q if g < nq // LANES else g_k), csv, ssv, SWA_D, SWA_ROT // 2).astype(BF16)
            if g < nq // LANES:
                o[0][:, LANES * g:LANES * (g + 1)] = y
            else:
                o[1][:, LANES * g - nq:LANES * (g + 1) - nq] = y
        o[2][...] = x_r[:, nq + nk:nq + 2 * nk].astype(BF16)
        return ()

    outs = [_sds((S, nq), BF16), _sds((S, nk), BF16), _sds((S, nk), BF16)]
    return _rowwise(name, fn, [qkv, cs, ss], gains, outs)


def _swa_prep_bwd(name, dq, dk, dv, qkv, tabs, gains):
    nq, nk = SWA_HEADS * SWA_D, SWA_KV * SWA_D
    cs, ss = tabs

    def fn(r, c, o):
        dq_r, dk_r, x_r, csv, ssv = r[0], r[1], r[3], r[4][...], r[5][...]
        g_q, g_k = c[0][...], c[1][...]
        acc = [jnp.zeros((1, LANES), F32), jnp.zeros((1, LANES), F32)]
        for g in range((nq + nk) // LANES):
            isq = g < nq // LANES
            x = x_r[:, LANES * g:LANES * (g + 1)]
            rr, low = _seg64_r(x)
            d = dq_r[:, LANES * g:LANES * (g + 1)] if isq else dk_r[:, LANES * g - nq:LANES * (g + 1) - nq]
            d = _rope_bwd(d, csv, ssv, SWA_D, SWA_ROT // 2)
            dyg = d * (g_q if isq else g_k)
            t = dyg * x
            d0 = _rsum(jnp.where(low, t, 0.0))
            d1 = _rsum(jnp.where(low, 0.0, t))
            dot = jnp.where(low, d0, d1) * (1.0 / SWA_D)
            o[0][:, LANES * g:LANES * (g + 1)] = (rr * dyg - x * (rr * rr * rr) * dot).astype(BF16)
            acc[0 if isq else 1] = acc[0 if isq else 1] + _csum(d * x * rr)
        o[0][:, nq + nk:nq + 2 * nk] = r[2][...].astype(BF16)
        return tuple(acc)

    return _rowwise(name, fn, [dq, dk, dv, qkv, cs, ss], gains, [_sds(qkv.shape, BF16)],
                    [_sds((1, LANES), F32), _sds((1, LANES), F32)], tm=128)


def _delta(name, do, o_, width):
    S, C = do.shape
    nh = C // width

    def fn(r, c, o):
        for g in range(C // LANES):
            t = r[0][:, LANES * g:LANES * (g + 1)].astype(F32) * r[1][:, LANES * g:LANES * (g + 1)].astype(F32)
            if width == LANES:
                o[0][g] = _rsum(t)
            else:
                low = _lane(t.shape) < 64
                o[0][2 * g] = _rsum(jnp.where(low, t, 0.0))
                o[0][2 * g + 1] = _rsum(jnp.where(low, 0.0, t))
        return ()

    return _rowwise(name, fn, [do, o_], [], [_sds((nh, S, 1), F32)])[0]


_NT = (((1,), (1,)), ((), ()))
_NN = (((1,), (0,)), ((), ()))
_TN = (((0,), (0,)), ((), ()))


def _flash_fwd(name, q, k, v, scale, tq=512, tk=512):
    H, S, dk = q.shape
    G = H // k.shape[0]
    dv = v.shape[2]
    tq, tk = _tile(S, (tq, 256, 128)), _tile(S, (tk, 256, 128))
    nk = S // tk

    def body(q_ref, k_ref, v_ref, o_ref, lse_ref, m_s, l_s, acc_s):
        j = pl.program_id(2)

        @pl.when(j == 0)
        def _():
            m_s[...] = jnp.full_like(m_s, -jnp.inf)
            l_s[...] = jnp.zeros_like(l_s)
            acc_s[...] = jnp.zeros_like(acc_s)

        s = lax.dot_general(q_ref[...], k_ref[...], _NT, preferred_element_type=F32) * scale
        m_new = jnp.maximum(m_s[...], jnp.max(s, axis=-1, keepdims=True))
        alpha = jnp.exp(m_s[...] - m_new)
        p = jnp.exp(s - m_new)
        l_s[...] = alpha * l_s[...] + _rsum(p)
        acc_s[...] = alpha * acc_s[...] + lax.dot_general(p.astype(BF16), v_ref[...], _NN,
                                                          preferred_element_type=F32)
        m_s[...] = m_new

        @pl.when(j == nk - 1)
        def _():
            o_ref[...] = (acc_s[...] / l_s[...]).astype(o_ref.dtype)
            lse_ref[...] = m_s[...] + jnp.log(l_s[...])

    return pl.pallas_call(
        body, name=name, grid=(H, S // tq, nk),
        in_specs=[pl.BlockSpec((None, tq, dk), lambda h, i, j: (h, i, 0)),
                  pl.BlockSpec((None, tk, dk), lambda h, i, j: (h // G, j, 0)),
                  pl.BlockSpec((None, tk, dv), lambda h, i, j: (h // G, j, 0))],
        out_specs=[pl.BlockSpec((tq, dv), lambda h, i, j: (i, h)),
                   pl.BlockSpec((None, tq, 1), lambda h, i, j: (h, i, 0))],
        out_shape=[_sds((S, H * dv), BF16), _sds((H, S, 1), F32)],
        scratch_shapes=[pltpu.VMEM((tq, 1), F32), pltpu.VMEM((tq, 1), F32), pltpu.VMEM((tq, dv), F32)],
        compiler_params=_params(("parallel", "parallel", "arbitrary")),
    )(q, k, v)


def _flash_bwd(name, q, k, v, do, head0, lse, delta, scale, tq=512, tk=512):
    H, S, dk = q.shape
    G = H // k.shape[0]
    dv = v.shape[2]
    tq, tk = _tile(S, (tq, 256, 128)), _tile(S, (tk, 256, 128))
    nq = S // tq

    def body(q_ref, k_ref, v_ref, do_ref, lse_ref, dl_ref, dq_ref, dk_ref, dv_ref, dk_s, dv_s):
        j, i = pl.program_id(1), pl.program_id(2)

        @pl.when(i == 0)
        def _():
            dk_s[...] = jnp.zeros_like(dk_s)
            dv_s[...] = jnp.zeros_like(dv_s)

        qv, kv_, dov = q_ref[...], k_ref[...], do_ref[...]
        s = lax.dot_general(qv, kv_, _NT, preferred_element_type=F32) * scale
        p = jnp.exp(s - lse_ref[...])
        dp = lax.dot_general(dov, v_ref[...], _NT, preferred_element_type=F32)
        ds = (p * (dp - dl_ref[...]) * scale).astype(BF16)
        dv_s[...] += lax.dot_general(p.astype(BF16), dov, _TN, preferred_element_type=F32)
        dk_s[...] += lax.dot_general(ds, qv, _TN, preferred_element_type=F32)
        dqi = lax.dot_general(ds, kv_, _NN, preferred_element_type=F32)
        rows = pl.ds(pl.multiple_of(i * tq, tq), tq)

        @pl.when(j == 0)
        def _():
            dq_ref[rows, :] = dqi

        @pl.when(j > 0)
        def _():
            dq_ref[rows, :] += dqi

        @pl.when(i == nq - 1)
        def _():
            dk_ref[...] = dk_s[...]
            dv_ref[...] = dv_s[...]

    return pl.pallas_call(
        body, name=name, grid=(H, S // tk, nq),
        in_specs=[pl.BlockSpec((None, tq, dk), lambda h, j, i: (h, i, 0)),
                  pl.BlockSpec((None, tk, dk), lambda h, j, i: (h // G, j, 0)),
                  pl.BlockSpec((None, tk, dv), lambda h, j, i: (h // G, j, 0)),
                  pl.BlockSpec((tq, dv), lambda h, j, i: (i, head0 + h)),
                  pl.BlockSpec((None, tq, 1), lambda h, j, i: (head0 + h, i, 0)),
                  pl.BlockSpec((None, tq, 1), lambda h, j, i: (head0 + h, i, 0))],
        out_specs=[pl.BlockSpec((None, S, dk), lambda h, j, i: (h, 0, 0)),
                   pl.BlockSpec((None, tk, dk), lambda h, j, i: (h, j, 0)),
                   pl.BlockSpec((None, tk, dv), lambda h, j, i: (h, j, 0))],
        out_shape=[_sds((H, S, dk), F32), _sds((H, S, dk), F32), _sds((H, S, dv), F32)],
        scratch_shapes=[pltpu.VMEM((tk, dk), F32), pltpu.VMEM((tk, dv), F32)],
        compiler_params=_params(("parallel", "arbitrary", "arbitrary")),
    )(q, k, v, do, lse, delta)


def _swa_place(ref128, h):
    e, t = h % 2, (h // (SWA_HEADS // SWA_KV)) % 2
    x = ref128.astype(F32)
    if e != t:
        x = pltpu.roll(x, 64, 1)
    return jnp.where((_lane(x.shape) >= 64) == bool(t), x, 0.0).astype(BF16)


def _swa_unplace(y, h):
    e, t = h % 2, (h // (SWA_HEADS // SWA_KV)) % 2
    return pltpu.roll(y, 64, 1) if e != t else y


def _swa_specs(width, nb):
    prev = pl.BlockSpec((SWA_BLOCK, width), lambda i: (jnp.maximum(i - 1, 0), 0))
    cur = pl.BlockSpec((SWA_BLOCK, width), lambda i: (i, 0))
    nxt = pl.BlockSpec((SWA_BLOCK, width), lambda i: (jnp.minimum(i + 1, nb - 1), 0))
    return [prev, cur, nxt]


def _swa_specs3(nh, nb):
    prev = pl.BlockSpec((nh, SWA_BLOCK, 1), lambda i: (0, jnp.maximum(i - 1, 0), 0))
    cur = pl.BlockSpec((nh, SWA_BLOCK, 1), lambda i: (0, i, 0))
    nxt = pl.BlockSpec((nh, SWA_BLOCK, 1), lambda i: (0, jnp.minimum(i + 1, nb - 1), 0))
    return [prev, cur, nxt]


def _swa_valid_q(i, S):
    qpos = i * SWA_BLOCK + lax.broadcasted_iota(jnp.int32, (SWA_BLOCK, 3 * SWA_BLOCK), 0)
    kpos = (i - 1) * SWA_BLOCK + lax.broadcasted_iota(jnp.int32, (SWA_BLOCK, 3 * SWA_BLOCK), 1)
    return (jnp.abs(qpos - kpos) <= SWA_WINDOW) & (kpos >= 0) & (kpos < S)


def _swa_fwd(name, q, k, v, sink, scale):
    S = q.shape[0]
    nb = S // SWA_BLOCK
    grp = SWA_HEADS // SWA_KV
    smem = pl.BlockSpec(memory_space=pltpu.SMEM)

    def body(sink_ref, q_ref, kp, kc, kn, vp, vc, vn, o_ref, lse_ref):
        i = pl.program_id(0)
        valid = _swa_valid_q(i, S)
        kcat = [jnp.concatenate([r[:, LANES * u:LANES * (u + 1)] for r in (kp, kc, kn)], axis=0) for u in range(2)]
        vcat = [jnp.concatenate([r[:, LANES * u:LANES * (u + 1)] for r in (vp, vc, vn)], axis=0) for u in range(2)]
        for g in range(SWA_HEADS // 2):
            pair = []
            for h in (2 * g, 2 * g + 1):
                u = (h // grp) // 2
                xq = _swa_place(q_ref[:, LANES * g:LANES * (g + 1)], h)
                s = lax.dot_general(xq, kcat[u], _NT, preferred_element_type=F32) * scale
                s = jnp.where(valid, s, -jnp.inf)
                sk = sink_ref[0, h]
                m = jnp.maximum(jnp.max(s, axis=-1, keepdims=True), sk)
                p = jnp.exp(s - m)
                den = _rsum(p) + jnp.exp(sk - m)
                oh = lax.dot_general((p / den).astype(BF16), vcat[u], _NN, preferred_element_type=F32)
                lse_ref[h] = m + jnp.log(den)
                pair.append(_swa_unplace(oh, h))
            o_ref[:, LANES * g:LANES * (g + 1)] = jnp.where(_lane(pair[0].shape) < 64, pair[0], pair[1]).astype(BF16)

    return pl.pallas_call(
        body, name=name, grid=(nb,),
        in_specs=[smem, pl.BlockSpec((SWA_BLOCK, q.shape[1]), lambda i: (i, 0))]
        + _swa_specs(k.shape[1], nb) + _swa_specs(v.shape[1], nb),
        out_specs=[pl.BlockSpec((SWA_BLOCK, q.shape[1]), lambda i: (i, 0)),
                   pl.BlockSpec((SWA_HEADS, SWA_BLOCK, 1), lambda i: (0, i, 0))],
        out_shape=[_sds(q.shape, BF16), _sds((SWA_HEADS, S, 1), F32)],
        compiler_params=_params(("parallel",)),
    )(sink, q, k, k, k, v, v, v)


def _swa_bwd_q(name, q, k, v, do, lse, delta, sink, scale):
    S = q.shape[0]
    nb = S // SWA_BLOCK
    grp = SWA_HEADS // SWA_KV
    smem = pl.BlockSpec(memory_space=pltpu.SMEM)
    row = pl.BlockSpec((SWA_BLOCK, q.shape[1]), lambda i: (i, 0))
    hrow = pl.BlockSpec((SWA_HEADS, SWA_BLOCK, 1), lambda i: (0, i, 0))

    def body(sink_ref, q_ref, do_ref, lse_ref, dl_ref, kp, kc, kn, vp, vc, vn, dq_ref, ds_ref):
        i = pl.program_id(0)

        @pl.when(i == 0)
        def _():
            ds_ref[...] = jnp.zeros_like(ds_ref)

        valid = _swa_valid_q(i, S)
        kcat = [jnp.concatenate([r[:, LANES * u:LANES * (u + 1)] for r in (kp, kc, kn)], axis=0) for u in range(2)]
        vcat = [jnp.concatenate([r[:, LANES * u:LANES * (u + 1)] for r in (vp, vc, vn)], axis=0) for u in range(2)]
        for g in range(SWA_HEADS // 2):
            pair = []
            for h in (2 * g, 2 * g + 1):
                u = (h // grp) // 2
                xq = _swa_place(q_ref[:, LANES * g:LANES * (g + 1)], h)
                xdo = _swa_place(do_ref[:, LANES * g:LANES * (g + 1)], h)
                lse_h, dl_h = lse_ref[h], dl_ref[h]
                s = lax.dot_general(xq, kcat[u], _NT, preferred_element_type=F32) * scale
                p = jnp.exp(jnp.where(valid, s, -jnp.inf) - lse_h)
                dp = lax.dot_general(xdo, vcat[u], _NT, preferred_element_type=F32)
                dsv = (p * (dp - dl_h) * scale).astype(BF16)
                dqh = lax.dot_general(dsv, kcat[u], _NN, preferred_element_type=F32)
                pair.append(_swa_unplace(dqh, h))
                dsink = -_csum(jnp.exp(sink_ref[0, h] - lse_h) * dl_h)
                ds_ref[h:h + 1, :] += jnp.broadcast_to(dsink, (1, LANES))
            dq_ref[:, LANES * g:LANES * (g + 1)] = jnp.where(_lane(pair[0].shape) < 64, pair[0], pair[1])

    return pl.pallas_call(
        body, name=name, grid=(nb,),
        in_specs=[smem, row, row, hrow, hrow] + _swa_specs(k.shape[1], nb) + _swa_specs(v.shape[1], nb),
        out_specs=[row, pl.BlockSpec((SWA_HEADS, LANES), lambda i: (0, 0))],
        out_shape=[_sds(q.shape, F32), _sds((SWA_HEADS, LANES), F32)],
        compiler_params=_params(("arbitrary",)),
    )(sink, q, do, lse, delta, k, k, k, v, v, v)


def _swa_bwd_kv(name, q, k, v, do, lse, delta, scale):
    S = q.shape[0]
    nb = S // SWA_BLOCK
    grp = SWA_HEADS // SWA_KV
    krow = pl.BlockSpec((SWA_BLOCK, k.shape[1]), lambda i: (i, 0))

    def body(qp, qc, qn, dop, doc, don, lp, lc, ln, dp_, dc_, dn_, k_ref, v_ref, dk_ref, dv_ref):
        j = pl.program_id(0)
        qpos = (j - 1) * SWA_BLOCK + lax.broadcasted_iota(jnp.int32, (3 * SWA_BLOCK, SWA_BLOCK), 0)
        kpos = j * SWA_BLOCK + lax.broadcasted_iota(jnp.int32, (3 * SWA_BLOCK, SWA_BLOCK), 1)
        valid = (jnp.abs(qpos - kpos) <= SWA_WINDOW) & (qpos >= 0) & (qpos < S)
        for u in range(SWA_KV // 2):
            kc = k_ref[:, LANES * u:LANES * (u + 1)]
            vc = v_ref[:, LANES * u:LANES * (u + 1)]
            dk_a = jnp.zeros((SWA_BLOCK, LANES), F32)
            dv_a = jnp.zeros((SWA_BLOCK, LANES), F32)
            for h in range(2 * u * grp, (2 * u + 2) * grp):
                g = h // 2
                xq = jnp.concatenate([_swa_place(r[:, LANES * g:LANES * (g + 1)], h) for r in (qp, qc, qn)], axis=0)
                xdo = jnp.concatenate([_swa_place(r[:, LANES * g:LANES * (g + 1)], h) for r in (dop, doc, don)], axis=0)
                lse_h = jnp.concatenate([lp[h], lc[h], ln[h]], axis=0)
                dl_h = jnp.concatenate([dp_[h], dc_[h], dn_[h]], axis=0)
                s = lax.dot_general(xq, kc, _NT, preferred_element_type=F32) * scale
                p = jnp.where(valid, jnp.exp(s - lse_h), 0.0)
                dpv = lax.dot_general(xdo, vc, _NT, preferred_element_type=F32)
                dsv = (p * (dpv - dl_h) * scale).astype(BF16)
                dv_a = dv_a + lax.dot_general(p.astype(BF16), xdo, _TN, preferred_element_type=F32)
                dk_a = dk_a + lax.dot_general(dsv, xq, _TN, preferred_element_type=F32)
            dk_ref[:, LANES * u:LANES * (u + 1)] = dk_a
            dv_ref[:, LANES * u:LANES * (u + 1)] = dv_a

    return pl.pallas_call(
        body, name=name, grid=(nb,),
        in_specs=_swa_specs(q.shape[1], nb) + _swa_specs(do.shape[1], nb)
        + _swa_specs3(SWA_HEADS, nb) + _swa_specs3(SWA_HEADS, nb) + [krow, krow],
        out_specs=[krow, krow],
        out_shape=[_sds(k.shape, F32), _sds(v.shape, F32)],
        compiler_params=_params(("parallel",)),
    )(q, q, q, do, do, do, lse, lse, lse, delta, delta, delta, k, v)


def _exchange(name, tensors):
    n = len(tensors)
    outs = [_sds(a.shape if sc else (N_DEV,) + a.shape, a.dtype) for a, sc in tensors]

    def body(*refs):
        ins, dst, send, recv, loc = refs[:n], refs[n:2 * n], refs[2 * n], refs[2 * n + 1], refs[2 * n + 2]
        x, y, c = lax.axis_index("x"), lax.axis_index("y"), lax.axis_index("c")
        me = 4 * x + 2 * y + c

        def peer(kk):
            return (x ^ ((kk >> 2) & 1), y ^ ((kk >> 1) & 1), c ^ (kk & 1))

        def copy(t, kk):
            px, py, pc = peer(kk)
            src = ins[t].at[4 * px + 2 * py + pc] if tensors[t][1] else ins[t]
            return pltpu.make_async_remote_copy(src_ref=src, dst_ref=dst[t].at[me], send_sem=send.at[t, kk - 1],
                                                recv_sem=recv.at[t, kk - 1], device_id=(px, py, pc),
                                                device_id_type=pl.DeviceIdType.MESH)

        def arrival(t, kk):
            px, py, pc = peer(kk)
            src = ins[t].at[me] if tensors[t][1] else ins[t]
            return pltpu.make_async_remote_copy(src_ref=src, dst_ref=dst[t].at[4 * px + 2 * py + pc],
                                                send_sem=send.at[t, kk - 1], recv_sem=recv.at[t, kk - 1],
                                                device_id=(px, py, pc), device_id_type=pl.DeviceIdType.MESH)

        own = [pltpu.make_async_copy(ins[t].at[me] if tensors[t][1] else ins[t], dst[t].at[me], loc.at[t])
               for t in range(n)]
        sent = [copy(t, kk) for t in range(n) for kk in range(1, N_DEV)]
        for cp in own + sent:
            cp.start()
        for t in range(n):
            for kk in range(1, N_DEV):
                arrival(t, kk).wait_recv()
        for cp in sent:
            cp.wait_send()
        for cp in own:
            cp.wait()

    any_spec = pl.BlockSpec(memory_space=pl.ANY)
    return pl.pallas_call(
        body, name=name, in_specs=[any_spec] * n, out_specs=[any_spec] * n, out_shape=outs,
        scratch_shapes=[pltpu.SemaphoreType.DMA((n, N_DEV - 1)), pltpu.SemaphoreType.DMA((n, N_DEV - 1)),
                        pltpu.SemaphoreType.DMA((n,))],
        compiler_params=pltpu.CompilerParams(has_side_effects=True),
    )(*[a for a, _ in tensors])


def _sum8(name, parts):
    _, R, C = parts.shape

    def fn(r, c, o):
        acc = r[0][0].astype(F32)
        for s in range(1, N_DEV):
            acc = acc + r[0][s].astype(F32)
        o[0][...] = acc
        return ()

    tm = R if R % 256 else 256
    return _rowwise(name, fn, [parts], [], [_sds((R, C), F32)], tm=tm)[0]


def _adamw(name, w, g, m, v):
    bc1 = 1.0 - ADAM_B1 ** ADAM_STEP
    bc2 = 1.0 - ADAM_B2 ** ADAM_STEP

    def fn(r, c, o):
        wv, gv, mv, vv = (x[...] for x in r)
        mn = ADAM_B1 * mv + (1.0 - ADAM_B1) * gv
        vn = ADAM_B2 * vv + (1.0 - ADAM_B2) * (gv * gv)
        o[0][...] = -ADAM_LR * ((mn / bc1) / (jnp.sqrt(vn / bc2) + ADAM_EPS) + ADAM_WD * wv)
        o[1][...] = mn
        o[2][...] = vn
        return ()

    tm = max(8, min(512, 1 << ((16 << 20) // (56 * w.shape[1])).bit_length() - 1))
    return _rowwise(name, fn, [w, g, m, v], [], [_sds(w.shape, F32)] * 3, tm=tm)


def _rope_cs(pos, dim, theta):
    inv = jnp.float32(theta) ** (-jnp.arange(0, dim, 2, dtype=jnp.float32) / dim)
    ang = pos.astype(jnp.float32)[:, None] * inv[None, :]
    return jnp.cos(ang), jnp.sin(ang)


def _tables(S):
    pos = jnp.arange(S)
    c, s = _rope_cs(pos, MLA_ROPE, ROPE_THETA)
    mla = (jnp.concatenate([c, c, c, c], 1), jnp.concatenate([-s, s, -s, s], 1))
    rc, rs = _rope_cs(pos // GRID_W, HD // 2, AXIAL_THETA)
    cc, cs = _rope_cs(pos % GRID_W, HD // 2, AXIAL_THETA)
    axial = (jnp.concatenate([rc, rc, cc, cc], 1), jnp.concatenate([-rs, rs, -cs, cs], 1))
    c, s = _rope_cs(pos, SWA_ROT, ROPE_THETA)
    one, zero = jnp.ones((S, SWA_D - SWA_ROT), F32), jnp.zeros((S, SWA_D - SWA_ROT), F32)
    swa = (jnp.concatenate([c, c, one, c, c, one], 1), jnp.concatenate([-s, s, zero, -s, s, zero], 1))
    return mla, axial, swa


_WEIGHTS = ['even_norm', 'even_w_in', 'mla_q_lat_norm', 'mla_kv_lat_norm', 'mla_w_uq', 'mla_w_ukv', 'mla_q_norm',
            'mla_k_nope_norm', 'mla_k_rope_norm', 'gqa_q_norm', 'gqa_k_norm', 'even_w_out', 'odd_norm', 'odd_w_qkv',
            'swa_q_norm', 'swa_k_norm', 'swa_sink', 'odd_w_out', 'mlp_norm', 'mlp_w_up', 'mlp_w_down']
_SMALL = ['even_norm', 'mla_q_lat_norm', 'mla_kv_lat_norm', 'mla_q_norm', 'mla_k_nope_norm', 'mla_k_rope_norm',
          'gqa_q_norm', 'gqa_k_norm', 'swa_q_norm', 'swa_k_norm', 'swa_sink', 'mlp_norm']


def _relu2(acc):
    rl = jnp.maximum(acc, 0.0)
    return rl * rl, rl


def _mul2(acc, rl):
    return (acc * (2.0 * rl.astype(F32)),)


def _add(acc, res):
    return (acc + res,)


def _step(x, tgt, w, m, v):
    S, D = x.shape
    nw = len(_WEIGHTS)
    tab_mla, tab_ax, tab_swa = _tables(S)
    bf = lambda a: a.astype(BF16)

    w_up_s, w_dn_s = w['mlp_w_up'], w['mlp_w_down']
    gathered = _exchange("gather_weights", [
        (bf(w['even_w_in'][0].T), False), (bf(w['mla_w_uq'][0].T), False), (bf(w['mla_w_ukv'][0].T), False),
        (bf(w['even_w_out'][0]), False), (bf(w['odd_w_qkv'][0].T), False), (bf(w['odd_w_out'][0]), False),
        (bf(w_up_s[0].T), False), (bf(w_up_s[1].T), False), (bf(w_dn_s[0]), False), (bf(w_dn_s[1]), False),
        (w['odd_norm'], False)])
    flat = lambda a: a.reshape((a.shape[0] * a.shape[1],) + a.shape[2:])
    win_t, wuq_g, wukv_t, wout_e, wqkv_t, wout_o, wup0_t, wup1_t, wdn0, wdn1 = [flat(a) for a in gathered[:10]]
    odd_norm = gathered[10].reshape(1, D)
    win_t = jnp.concatenate([win_t[:KR_END], jnp.zeros((LANES - MLA_ROPE, D), BF16), win_t[KR_END:]], 0)
    wuq_g = gathered[1]
    wuq_t = jnp.concatenate([wuq_g[:, :HD].reshape(HEADS * HD, Q_LORA),
                             wuq_g[:, HD:].reshape(HEADS * MLA_ROPE, Q_LORA)], 0)
    wup_t, wdn = (wup0_t, wup1_t), (wdn0, wdn1)

    z64 = jnp.zeros((1, 64), F32)
    qn = w['mla_q_norm']
    g_even = [w['mla_q_lat_norm'], w['mla_kv_lat_norm'], jnp.concatenate([w['mla_k_rope_norm'], z64], 1),
              w['gqa_q_norm'], w['gqa_k_norm']]
    g_mla = [qn[:, :HD], jnp.concatenate([qn[:, HD:], qn[:, HD:]], 1), w['mla_k_nope_norm']]
    g_swa = [jnp.concatenate([w['swa_q_norm']] * 2, 1), jnp.concatenate([w['swa_k_norm']] * 2, 1)]

    def mlp_fwd(l, xin):
        hn = _rmsnorm(f"mlp{l}_norm", xin, w['mlp_norm'][l:l + 1])
        a, rl = _mm(f"mlp{l}_up", hn, wup_t[l], "nt", (BF16, BF16), epilogue=_relu2)
        xout = _mm(f"mlp{l}_down", a, wdn[l], "nn", (F32,), epilogue=_add, extras=(xin,))
        return xout, (hn, a, rl)

    h0 = _rmsnorm("even_norm", x, w['even_norm'])
    proj = _mm("even_in", h0, win_t, "nt", (F32,), tm=512, tn=P_END)
    cqn, ckvn, krd, qg, kg, vg = _even_prep("even_prep", proj, tab_mla + tab_ax, g_even)
    qa = _mm("mla_uq", cqn, wuq_t, "nt", (F32,))
    kv = _mm("mla_ukv", ckvn, wukv_t, "nt", (F32,))
    q_a, k_a, v_a = _mla_prep("mla_prep", qa, kv, krd, tab_mla, g_mla)
    o_a, lse_a = _flash_fwd("mla_attn", q_a, k_a, v_a, MLA_QK ** -0.5)
    o_g, lse_g = _flash_fwd("gqa_attn", qg, kg, vg, HD ** -0.5)
    merged = jnp.concatenate([o_a, o_g], 1)
    x1 = _mm("even_out", merged, wout_e, "nn", (F32,), epilogue=_add, extras=(x,))
    x2, mlp0 = mlp_fwd(0, x1)

    h1 = _rmsnorm("odd_norm", x2, odd_norm)
    qkv = _mm("odd_qkv", h1, wqkv_t, "nt", (F32,))
    q_s, k_s, v_s = _swa_prep("swa_prep", qkv, tab_swa, g_swa)
    o_s, lse_s = _swa_fwd("swa_attn", q_s, k_s, v_s, w['swa_sink'], SWA_D ** -0.5)
    x3 = _mm("odd_out", o_s, wout_o, "nn", (F32,), epilogue=_add, extras=(x2,))
    x4, mlp1 = mlp_fwd(1, x3)

    dy, loss_acc = _loss("loss", x4, tgt)
    loss = lax.psum(0.5 / D * loss_acc[0, 0], ("x", "y", "c"))

    gsm = {}

    def mlp_bwd(l, dout, xin, saved):
        hn, a, rl = saved
        du = _mm(f"mlp{l}_dact", dout, wdn[l], "nt", (BF16,), epilogue=_mul2, extras=(rl,))
        g_dn = _mm(f"mlp{l}_gdown", a, dout, "tn", (BF16,))
        g_up = _mm(f"mlp{l}_gup", du, hn, "tn", (BF16,))
        dhn = _mm(f"mlp{l}_dnorm", du, wup_t[l], "nn", (F32,))
        din, g_n = _rmsnorm_bwd(f"mlp{l}_norm_bwd", dout, xin, dhn, w['mlp_norm'][l:l + 1])
        return din, g_dn, g_up, g_n

    dx3, g_dn1, g_up1, g_mn1 = mlp_bwd(1, dy, x3, mlp1)

    g_wout_o = _mm("odd_gout", o_s, dx3, "tn", (BF16,))
    do_s = _mm("odd_dattn", dx3, wout_o, "nt", (BF16,))
    dl_s = _delta("swa_delta", do_s, o_s, SWA_D)
    dq_s, dsink = _swa_bwd_q("swa_bwd_q", q_s, k_s, v_s, do_s, lse_s, dl_s, w['swa_sink'], SWA_D ** -0.5)
    dk_s, dv_s = _swa_bwd_kv("swa_bwd_kv", q_s, k_s, v_s, do_s, lse_s, dl_s, SWA_D ** -0.5)
    dqkv, g_sq, g_sk = _swa_prep_bwd("swa_prep_bwd", dq_s, dk_s, dv_s, qkv, tab_swa, g_swa)
    g_wqkv = _mm("odd_gqkv", dqkv, h1, "tn", (BF16,))
    dh1 = _mm("odd_dnorm", dqkv, wqkv_t, "nn", (F32,))
    dx2, g_on = _rmsnorm_bwd("odd_norm_bwd", dx3, x2, dh1, odd_norm)
    gsm['swa_q_norm'] = g_sq[:, :64] + g_sq[:, 64:]
    gsm['swa_k_norm'] = g_sk[:, :64] + g_sk[:, 64:]
    gsm['swa_sink'] = dsink[:, 0].reshape(1, SWA_HEADS)

    dx1, g_dn0, g_up0, g_mn0 = mlp_bwd(0, dx2, x1, mlp0)
    gsm['mlp_norm'] = jnp.concatenate([g_mn0, g_mn1], 0)

    g_wout_e = _mm("even_gout", merged, dx1, "tn", (BF16,))
    dmerged = _mm("even_dattn", dx1, wout_e, "nt", (BF16,))
    dl_e = _delta("even_delta", dmerged, merged, HD)
    lse_e = jnp.concatenate([lse_a, lse_g], 0)
    dq_a, dk_a, dv_a = _flash_bwd("mla_attn_bwd", q_a, k_a, v_a, dmerged, 0, lse_e, dl_e, MLA_QK ** -0.5)
    dq_g, dk_g, dv_g = _flash_bwd("gqa_attn_bwd", qg, kg, vg, dmerged, HEADS, lse_e, dl_e, HD ** -0.5)
    dqa, dkv, dkrd, g_qnn, g_qnr, g_kn = _mla_prep_bwd("mla_prep_bwd", dq_a, dk_a, dv_a, qa, kv, tab_mla, g_mla)
    g_wuq = _mm("mla_guq", dqa, cqn, "tn", (BF16,))
    dcqn = _mm("mla_dq_lat", dqa, wuq_t, "nn", (F32,))
    g_wukv = _mm("mla_gukv", dkv, ckvn, "tn", (BF16,))
    dckvn = _mm("mla_dkv_lat", dkv, wukv_t, "nn", (F32,))
    dproj, g_ql, g_kvl, g_kr, g_gq, g_gk = _even_prep_bwd("even_prep_bwd", dcqn, dckvn, dkrd, dq_g, dk_g, dv_g,
                                                          proj, tab_mla + tab_ax, g_even)
    g_win = _mm("even_gin", dproj, h0, "tn", (BF16,), tm=P_END)
    dh0 = _mm("even_dnorm", dproj, win_t, "nn", (F32,), tk=P_END)
    grad_x, g_en = _rmsnorm_bwd("even_norm_bwd", dx1, x, dh0, w['even_norm'])
    gsm.update(even_norm=g_en, mla_q_lat_norm=g_ql, mla_kv_lat_norm=g_kvl,
               mla_q_norm=jnp.concatenate([g_qnn, g_qnr[:, :64] + g_qnr[:, 64:]], 1), mla_k_nope_norm=g_kn,
               mla_k_rope_norm=g_kr[:, :64], gqa_q_norm=g_gq, gqa_k_norm=g_gk)

    g_win = jnp.concatenate([g_win[:KR_END], g_win[P_QG:]], 0)
    g_wuq = jnp.concatenate([g_wuq[:HEADS * HD].reshape(HEADS, HD, Q_LORA),
                             g_wuq[HEADS * HD:].reshape(HEADS, MLA_ROPE, Q_LORA)], 1)
    split = lambda a: a.reshape((N_DEV, a.shape[0] // N_DEV) + a.shape[1:])
    small_sizes = [w[n].size for n in _SMALL] + [D]
    small_vec = jnp.concatenate([gsm[n].reshape(1, -1) for n in _SMALL] + [g_on], 1)
    pad = (-small_vec.shape[1]) % LANES
    small_vec = jnp.pad(small_vec, ((0, 0), (0, pad)))
    parts = _exchange("scatter_grads", [
        (split(g_win), True), (g_wuq, True), (split(g_wukv), True), (split(g_wout_e), True), (split(g_wqkv), True),
        (split(g_wout_o), True), (split(g_up0), True), (split(g_up1), True), (split(g_dn0), True),
        (split(g_dn1), True), (small_vec, False)])
    names = ['even_w_in', 'mla_w_uq', 'mla_w_ukv', 'even_w_out', 'odd_w_qkv', 'odd_w_out', 'up0', 'up1', 'dn0', 'dn1']
    red = {n: _sum8("sum_" + n, p) for n, p in zip(names, parts[:10])}
    small_g = _sum8("sum_small", parts[10])

    grads = {
        'even_w_in': red['even_w_in'].T[None], 'mla_w_uq': red['mla_w_uq'].T[None],
        'mla_w_ukv': red['mla_w_ukv'].T[None], 'even_w_out': red['even_w_out'][None],
        'odd_w_qkv': red['odd_w_qkv'].T[None], 'odd_w_out': red['odd_w_out'][None],
        'mlp_w_up': jnp.stack([red['up0'].T, red['up1'].T]), 'mlp_w_down': jnp.stack([red['dn0'], red['dn1']]),
    }
    off = 0
    for n, sz in zip(_SMALL + ['odd_norm_full'], small_sizes):
        seg = small_g[:, off:off + sz]
        off += sz
        if n == 'odd_norm_full':
            me = 4 * lax.axis_index("x") + 2 * lax.axis_index("y") + lax.axis_index("c")
            grads['odd_norm'] = lax.dynamic_slice(seg, (0, me * (D // N_DEV)), (1, D // N_DEV))
        else:
            grads[n] = seg.reshape(w[n].shape)

    delta, new_m, new_v = {}, {}, {}
    big = ['even_w_in', 'mla_w_uq', 'mla_w_ukv', 'even_w_out', 'odd_w_qkv', 'odd_w_out', 'mlp_w_up', 'mlp_w_down']
    for n in big:
        shp = w[n].shape
        two = lambda a: a.reshape(shp[0] * shp[1], shp[2])
        d_, m_, v_ = _adamw("adamw_" + n, two(w[n]), two(grads[n]), two(m[n]), two(v[n]))
        delta[n], new_m[n], new_v[n] = d_.reshape(shp), m_.reshape(shp), v_.reshape(shp)
    sm_names = _SMALL + ['odd_norm']
    pack = lambda d: jnp.concatenate([d[n].reshape(1, -1) for n in sm_names], 1)
    pw, pg, pm, pv = pack(w), pack(grads), pack(m), pack(v)
    padw = (-pw.shape[1]) % LANES
    padf = lambda a: jnp.pad(a, ((0, 0), (0, padw)))
    d_, m_, v_ = _adamw("adamw_small", padf(pw), padf(pg), padf(pm), jnp.pad(pv, ((0, 0), (0, padw)), constant_values=1.0))
    off = 0
    for n in sm_names:
        sz = w[n].size
        delta[n] = d_[:, off:off + sz].reshape(w[n].shape)
        new_m[n] = m_[:, off:off + sz].reshape(w[n].shape)
        new_v[n] = v_[:, off:off + sz].reshape(w[n].shape)
        off += sz

    return (loss, grad_x[None], *[grads[n] for n in _WEIGHTS], *[delta[n] for n in _WEIGHTS],
            *[new_m[n] for n in _WEIGHTS], *[new_v[n] for n in _WEIGHTS])


def kernel(x, even_norm, even_w_in, mla_q_lat_norm, mla_kv_lat_norm, mla_w_uq, mla_w_ukv, mla_q_norm, mla_k_nope_norm, mla_k_rope_norm, gqa_q_norm, gqa_k_norm, even_w_out, odd_norm, odd_w_qkv, swa_q_norm, swa_k_norm, swa_sink, odd_w_out, mlp_norm, mlp_w_up, mlp_w_down, loss_target, m_even_norm, m_even_w_in, m_mla_q_lat_norm, m_mla_kv_lat_norm, m_mla_w_uq, m_mla_w_ukv, m_mla_q_norm, m_mla_k_nope_norm, m_mla_k_rope_norm, m_gqa_q_norm, m_gqa_k_norm, m_even_w_out, m_odd_norm, m_odd_w_qkv, m_swa_q_norm, m_swa_k_norm, m_swa_sink, m_odd_w_out, m_mlp_norm, m_mlp_w_up, m_mlp_w_down, v_even_norm, v_even_w_in, v_mla_q_lat_norm, v_mla_kv_lat_norm, v_mla_w_uq, v_mla_w_ukv, v_mla_q_norm, v_mla_k_nope_norm, v_mla_k_rope_norm, v_gqa_q_norm, v_gqa_k_norm, v_even_w_out, v_odd_norm, v_odd_w_qkv, v_swa_q_norm, v_swa_k_norm, v_swa_sink, v_odd_w_out, v_mlp_norm, v_mlp_w_up, v_mlp_w_down):
    ws = (even_norm, even_w_in, mla_q_lat_norm, mla_kv_lat_norm, mla_w_uq, mla_w_ukv, mla_q_norm, mla_k_nope_norm, mla_k_rope_norm, gqa_q_norm, gqa_k_norm, even_w_out, odd_norm, odd_w_qkv, swa_q_norm, swa_k_norm, swa_sink, odd_w_out, mlp_norm, mlp_w_up, mlp_w_down)
    ms = (m_even_norm, m_even_w_in, m_mla_q_lat_norm, m_mla_kv_lat_norm, m_mla_w_uq, m_mla_w_ukv, m_mla_q_norm, m_mla_k_nope_norm, m_mla_k_rope_norm, m_gqa_q_norm, m_gqa_k_norm, m_even_w_out, m_odd_norm, m_odd_w_qkv, m_swa_q_norm, m_swa_k_norm, m_swa_sink, m_odd_w_out, m_mlp_norm, m_mlp_w_up, m_mlp_w_down)
    vs = (v_even_norm, v_even_w_in, v_mla_q_lat_norm, v_mla_kv_lat_norm, v_mla_w_uq, v_mla_w_ukv, v_mla_q_norm, v_mla_k_nope_norm, v_mla_k_rope_norm, v_gqa_q_norm, v_gqa_k_norm, v_even_w_out, v_odd_norm, v_odd_w_qkv, v_swa_q_norm, v_swa_k_norm, v_swa_sink, v_odd_w_out, v_mlp_norm, v_mlp_w_up, v_mlp_w_down)
    return _step(x[0], loss_target[0], dict(zip(_WEIGHTS, ws)), dict(zip(_WEIGHTS, ms)), dict(zip(_WEIGHTS, vs)))
```

```python
import functools

import jax
import jax.numpy as jnp
from jax import lax
from jax.experimental import pallas as pl
from jax.experimental.pallas import tpu as pltpu

F32 = jnp.float32
BF16 = jnp.bfloat16

N_DEV = 8
NORM_EPS = 1e-6
ROPE_THETA = 500000.0
AXIAL_THETA = 10000.0
GRID_W = 64
HEADS = 8
GQA_KV = 2
HD = 128
MLA_ROPE = 64
MLA_QK = HD + MLA_ROPE
Q_LORA = 512
KV_LORA = 256
SWA_HEADS = 32
SWA_KV = 4
SWA_D = 64
SWA_ROT = 16
SWA_WINDOW = 128
SWA_BLOCK = 128
MLA_SCALE, GQA_SCALE, SWA_SCALE = MLA_QK ** -0.5, HD ** -0.5, SWA_D ** -0.5
LANES = 128
ADAM_LR, ADAM_B1, ADAM_B2, ADAM_EPS, ADAM_WD, ADAM_STEP = 0.001, 0.9, 0.999, 1e-08, 0.01, 10
VMEM_LIMIT = 56 * 1024 * 1024

P_CQ, P_CKV, P_KR, P_QG = 0, Q_LORA, Q_LORA + KV_LORA, Q_LORA + KV_LORA + LANES
P_KG = P_QG + HEADS * HD
P_VG = P_KG + GQA_KV * HD
P_END = P_VG + GQA_KV * HD
KR_END = Q_LORA + KV_LORA + MLA_ROPE


def _tile(n, prefs):
    for t in prefs:
        if n % t == 0 and t <= n:
            return t
    return n


def _params(sem):
    return pltpu.CompilerParams(dimension_semantics=sem, vmem_limit_bytes=VMEM_LIMIT)


_DIMS = {"nn": ((1,), (0,)), "nt": ((1,), (1,)), "tn": ((0,), (0,))}


def _mm(name, a, b, mode, out_dtypes, epilogue=None, extras=(), tm=1024, tn=1024, tk=512):
    if mode == "nn":
        (M, K), (_, N) = a.shape, b.shape
    elif mode == "nt":
        (M, K), (N, _) = a.shape, b.shape
    else:
        (K, M), (_, N) = a.shape, b.shape
    tm = _tile(M, (tm, 512, 256, 128))
    tn = _tile(N, (tn, 512, 256, 128))
    tk = _tile(K, (tk, 256, 128))
    nk = K // tk
    ne, no = len(extras), len(out_dtypes)
    if mode == "tn":
        a_spec = pl.BlockSpec((tk, tm), lambda i, j, k: (k, i))
    else:
        a_spec = pl.BlockSpec((tm, tk), lambda i, j, k: (i, k))
    if mode == "nt":
        b_spec = pl.BlockSpec((tn, tk), lambda i, j, k: (j, k))
    else:
        b_spec = pl.BlockSpec((tk, tn), lambda i, j, k: (k, j))
    o_spec = pl.BlockSpec((tm, tn), lambda i, j, k: (i, j))
    dims = (_DIMS[mode], ((), ()))

    def body(a_ref, b_ref, *rest):
        ex, outs, acc = rest[:ne], rest[ne:ne + no], rest[ne + no]
        k = pl.program_id(2)

        @pl.when(k == 0)
        def _():
            acc[...] = jnp.zeros_like(acc)

        acc[...] += lax.dot_general(a_ref[...].astype(BF16), b_ref[...].astype(BF16), dims,
                                    preferred_element_type=F32)

        @pl.when(k == nk - 1)
        def _():
            res = epilogue(acc[...], *[e[...] for e in ex]) if epilogue else (acc[...],)
            for o, r in zip(outs, res):
                o[...] = r.astype(o.dtype)

    out = pl.pallas_call(
        body, name=name, grid=(M // tm, N // tn, nk),
        in_specs=[a_spec, b_spec] + [o_spec] * ne,
        out_specs=[o_spec] * no,
        out_shape=[jax.ShapeDtypeStruct((M, N), d) for d in out_dtypes],
        scratch_shapes=[pltpu.VMEM((tm, tn), F32)],
        compiler_params=_params(("parallel", "parallel", "arbitrary")),
    )(a, b, *extras)
    return out[0] if no == 1 else out


def _rowwise(name, fn, rows, consts, outs, accs=(), tm=256):
    S = rows[0].shape[-2]
    tm = _tile(S, (tm, 128, 64, 32, 16, 8))
    nr, nc, no, na = len(rows), len(consts), len(outs), len(accs)

    def rspec(shape):
        if len(shape) == 2:
            return pl.BlockSpec((tm, shape[1]), lambda i: (i, 0))
        return pl.BlockSpec((shape[0], tm, shape[2]), lambda i: (0, i, 0))

    def cspec(shape):
        return pl.BlockSpec(tuple(shape), lambda i: (0,) * len(shape))

    def body(*refs):
        r, c = refs[:nr], refs[nr:nr + nc]
        o, a = refs[nr + nc:nr + nc + no], refs[nr + nc + no:]
        vals = fn(r, c, o)
        if na:
            @pl.when(pl.program_id(0) == 0)
            def _():
                for ar in a:
                    ar[...] = jnp.zeros_like(ar)

            for ar, v in zip(a, vals):
                ar[...] += v

    res = pl.pallas_call(
        body, name=name, grid=(S // tm,),
        in_specs=[rspec(x.shape) for x in rows] + [cspec(x.shape) for x in consts],
        out_specs=[rspec(x.shape) for x in outs] + [cspec(x.shape) for x in accs],
        out_shape=list(outs) + list(accs),
        compiler_params=_params(("arbitrary",) if na else ("parallel",)),
    )(*rows, *consts)
    return res


def _sds(shape, dtype):
    return jax.ShapeDtypeStruct(tuple(shape), dtype)


def _rsum(x):
    return jnp.sum(x, axis=-1, keepdims=True)


def _csum(x):
    return jnp.sum(x, axis=0, keepdims=True)


def _lane(shape):
    return lax.broadcasted_iota(jnp.int32, shape, 1)


def _partner(x, seg, half):
    lane = _lane(x.shape) % seg
    return jnp.where(lane < half, pltpu.roll(x, LANES - half, 1), pltpu.roll(x, half, 1))


def _rope(x, c, s, seg, half):
    return x * c + _partner(x, seg, half) * s


def _rope_bwd(dy, c, s, seg, half):
    t = _partner(dy * s, seg, half)
    if seg != 2 * half:
        t = jnp.where(_lane(dy.shape) % seg < 2 * half, t, 0.0)
    return dy * c + t


def _rms_r(ss, n):
    return lax.rsqrt(ss * (1.0 / n) + NORM_EPS)


def _rms_bwd(x, r, g, dy, dot_scale):
    dyg = dy * g
    return r * dyg - x * (r * r * r) * (_rsum(dyg * x) * dot_scale)


def _rmsnorm(name, x, g):
    D = x.shape[1]

    def fn(r, c, o):
        xv = r[0][...]
        o[0][...] = (xv * _rms_r(_rsum(xv * xv), D) * c[0][...]).astype(BF16)
        return ()

    return _rowwise(name, fn, [x], [g], [_sds(x.shape, BF16)])[0]


def _rmsnorm_bwd(name, dres, x, dh, g):
    D = x.shape[1]

    def fn(r, c, o):
        xv, dhv, gv = r[1][...], r[2][...], c[0][...]
        rr = _rms_r(_rsum(xv * xv), D)
        o[0][...] = r[0][...] + _rms_bwd(xv, rr, gv, dhv, 1.0 / D)
        return (_csum(dhv * xv * rr),)

    return _rowwise(name, fn, [dres, x, dh], [g], [_sds(x.shape, F32)], [_sds((1, D), F32)])


def _loss(name, y, t):
    D = y.shape[1]

    def fn(r, c, o):
        e = r[0][...] - r[1][...]
        o[0][...] = e * (1.0 / D)
        return (_csum(_rsum(e * e)),)

    return _rowwise(name, fn, [y, t], [], [_sds(y.shape, F32)], [_sds((1, 1), F32)])


def _even_prep(name, proj, tabs, gains):
    S = proj.shape[0]
    cm, sm, ca, sa = tabs

    def fn(r, c, o):
        p, cmv, smv, cav, sav = r[0], r[1][...], r[2][...], r[3][...], r[4][...]
        g_ql, g_kvl, g_kr, g_q, g_k = (x[...] for x in c)
        cq = p[:, P_CQ:P_CKV]
        o[0][...] = (cq * _rms_r(_rsum(cq * cq), Q_LORA) * g_ql).astype(BF16)
        ckv = p[:, P_CKV:P_KR]
        o[1][...] = (ckv * _rms_r(_rsum(ckv * ckv), KV_LORA) * g_kvl).astype(BF16)
        kr = p[:, P_KR:P_QG]
        y = _rope(kr * _rms_r(_rsum(kr * kr), MLA_ROPE) * g_kr, cmv, smv, 64, 32)
        o[2][...] = (y + pltpu.roll(y, 64, 1)).astype(BF16)
        for h in range(HEADS):
            xh = p[:, P_QG + HD * h:P_QG + HD * (h + 1)]
            o[3][h] = (_rope(xh * _rms_r(_rsum(xh * xh), HD) * g_q, cav, sav, 64, 32) * GQA_SCALE).astype(BF16)
        for h in range(GQA_KV):
            xh = p[:, P_KG + HD * h:P_KG + HD * (h + 1)]
            o[4][h] = _rope(xh * _rms_r(_rsum(xh * xh), HD) * g_k, cav, sav, 64, 32).astype(BF16)
            o[5][h] = p[:, P_VG + HD * h:P_VG + HD * (h + 1)].astype(BF16)
        return ()

    outs = [_sds((S, Q_LORA), BF16), _sds((S, KV_LORA), BF16), _sds((S, LANES), BF16),
            _sds((HEADS, S, HD), BF16), _sds((GQA_KV, S, HD), BF16), _sds((GQA_KV, S, HD), BF16)]
    return _rowwise(name, fn, [proj, cm, sm, ca, sa], gains, outs)


def _even_prep_bwd(name, dcqn, dckvn, dkrd, dqg, dkg, dvg, proj, tabs, gains):
    S = proj.shape[0]
    cm, sm, ca, sa = tabs

    def fn(r, c, o):
        p, cmv, smv, cav, sav = r[6], r[7][...], r[8][...], r[9][...], r[10][...]
        g_ql, g_kvl, g_kr, g_q, g_k = (x[...] for x in c)
        out = o[0]
        cq = p[:, P_CQ:P_CKV]
        rr = _rms_r(_rsum(cq * cq), Q_LORA)
        d = r[0][...]
        out[:, P_CQ:P_CKV] = _rms_bwd(cq, rr, g_ql, d, 1.0 / Q_LORA).astype(BF16)
        a_ql = _csum(d * cq * rr)
        ckv = p[:, P_CKV:P_KR]
        rr = _rms_r(_rsum(ckv * ckv), KV_LORA)
        d = r[1][...]
        out[:, P_CKV:P_KR] = _rms_bwd(ckv, rr, g_kvl, d, 1.0 / KV_LORA).astype(BF16)
        a_kvl = _csum(d * ckv * rr)
        kr = p[:, P_KR:P_QG]
        rr = _rms_r(_rsum(kr * kr), MLA_ROPE)
        d = r[2][...]
        d = d + pltpu.roll(d, 64, 1)
        d = _rope_bwd(d, cmv, smv, 64, 32)
        low = _lane(d.shape) < 64
        out[:, P_KR:P_QG] = jnp.where(low, _rms_bwd(kr, rr, g_kr, d, 1.0 / MLA_ROPE), 0.0).astype(BF16)
        a_kr = _csum(d * kr * rr)
        a_q = jnp.zeros((1, HD), F32)
        for h in range(HEADS):
            xh = p[:, P_QG + HD * h:P_QG + HD * (h + 1)]
            rr = _rms_r(_rsum(xh * xh), HD)
            d = _rope_bwd(r[3][h] * GQA_SCALE, cav, sav, 64, 32)
            out[:, P_QG + HD * h:P_QG + HD * (h + 1)] = _rms_bwd(xh, rr, g_q, d, 1.0 / HD).astype(BF16)
            a_q = a_q + _csum(d * xh * rr)
        a_k = jnp.zeros((1, HD), F32)
        grp = HEADS // GQA_KV
        for h in range(GQA_KV):
            xh = p[:, P_KG + HD * h:P_KG + HD * (h + 1)]
            rr = _rms_r(_rsum(xh * xh), HD)
            d = r[4][grp * h]
            dv = r[5][grp * h]
            for e in range(1, grp):
                d = d + r[4][grp * h + e]
                dv = dv + r[5][grp * h + e]
            d = _rope_bwd(d, cav, sav, 64, 32)
            out[:, P_KG + HD * h:P_KG + HD * (h + 1)] = _rms_bwd(xh, rr, g_k, d, 1.0 / HD).astype(BF16)
            out[:, P_VG + HD * h:P_VG + HD * (h + 1)] = dv.astype(BF16)
            a_k = a_k + _csum(d * xh * rr)
        return (a_ql, a_kvl, a_kr, a_q, a_k)

    accs = [_sds((1, Q_LORA), F32), _sds((1, KV_LORA), F32), _sds((1, LANES), F32),
            _sds((1, HD), F32), _sds((1, HD), F32)]
    return _rowwise(name, fn, [dcqn, dckvn, dkrd, dqg, dkg, dvg, proj, cm, sm, ca, sa], gains,
                    [_sds((S, P_END), BF16)], accs, tm=128)


def _mla_q_parts(qa, h, rp, g_n, g_r):
    nope = qa[:, HD * h:HD * (h + 1)]
    mine = (_lane(rp.shape) >= 64) == bool(h % 2)
    rpm = jnp.where(mine, rp, 0.0)
    rr = _rms_r(_rsum(nope * nope) + _rsum(rpm * rpm), MLA_QK)
    return nope, rpm, mine, rr


def _mla_prep(name, qa, kv, krd, tabs, gains):
    S = qa.shape[0]
    cm, sm = tabs

    def fn(r, c, o):
        qa_r, kv_r, krd_v, cmv, smv = r[0], r[1], r[2][...], r[3][...], r[4][...]
        g_n, g_r, g_kn = (x[...] for x in c)
        for h in range(HEADS):
            rp = qa_r[:, HEADS * HD + LANES * (h // 2):HEADS * HD + LANES * (h // 2 + 1)]
            nope, rpm, mine, rr = _mla_q_parts(qa_r, h, rp, g_n, g_r)
            o[0][h, :, 0:HD] = (nope * (rr * MLA_SCALE) * g_n).astype(BF16)
            o[0][h, :, HD:2 * HD] = _rope(rpm * (rr * MLA_SCALE) * g_r, cmv, smv, 64, 32).astype(BF16)
            kn = kv_r[:, 2 * HD * h:2 * HD * h + HD]
            o[1][h, :, 0:HD] = (kn * _rms_r(_rsum(kn * kn), HD) * g_kn).astype(BF16)
            o[1][h, :, HD:2 * HD] = krd_v
            o[2][h] = kv_r[:, 2 * HD * h + HD:2 * HD * (h + 1)].astype(BF16)
        return ()

    outs = [_sds((HEADS, S, 2 * HD), BF16), _sds((HEADS, S, 2 * HD), BF16), _sds((HEADS, S, HD), BF16)]
    return _rowwise(name, fn, [qa, kv, krd, cm, sm], gains, outs)


def _mla_prep_bwd(name, dq, dk, dv, qa, kv, tabs, gains):
    S = qa.shape[0]
    cm, sm = tabs

    def fn(r, c, o):
        dq_r, dk_r, dv_r, qa_r, kv_r, cmv, smv = r[0], r[1], r[2], r[3], r[4], r[5][...], r[6][...]
        g_n, g_r, g_kn = (x[...] for x in c)
        a_n = jnp.zeros((1, HD), F32)
        a_r = jnp.zeros((1, LANES), F32)
        a_kn = jnp.zeros((1, HD), F32)
        dkrd = jnp.zeros(cmv.shape, F32)
        drp = None
        for h in range(HEADS):
            rp = qa_r[:, HEADS * HD + LANES * (h // 2):HEADS * HD + LANES * (h // 2 + 1)]
            nope, rpm, mine, rr = _mla_q_parts(qa_r, h, rp, g_n, g_r)
            dn = dq_r[h, :, 0:HD] * MLA_SCALE
            dr = _rope_bwd(jnp.where(mine, dq_r[h, :, HD:2 * HD] * MLA_SCALE, 0.0), cmv, smv, 64, 32)
            dot = (_rsum(dn * g_n * nope) + _rsum(dr * g_r * rpm)) * (1.0 / MLA_QK)
            r3 = rr * rr * rr
            o[0][:, HD * h:HD * (h + 1)] = (rr * dn * g_n - nope * r3 * dot).astype(BF16)
            part = jnp.where(mine, rr * dr * g_r - rpm * r3 * dot, 0.0)
            drp = part if h % 2 == 0 else drp + part
            if h % 2 == 1:
                o[0][:, HEADS * HD + LANES * (h // 2):HEADS * HD + LANES * (h // 2 + 1)] = drp.astype(BF16)
            a_n = a_n + _csum(dn * nope * rr)
            a_r = a_r + _csum(dr * rpm * rr)
            kn = kv_r[:, 2 * HD * h:2 * HD * h + HD]
            rk = _rms_r(_rsum(kn * kn), HD)
            dkn = dk_r[h, :, 0:HD]
            o[1][:, 2 * HD * h:2 * HD * h + HD] = _rms_bwd(kn, rk, g_kn, dkn, 1.0 / HD).astype(BF16)
            o[1][:, 2 * HD * h + HD:2 * HD * (h + 1)] = dv_r[h].astype(BF16)
            a_kn = a_kn + _csum(dkn * kn * rk)
            dkrd = dkrd + dk_r[h, :, HD:2 * HD]
        o[2][...] = dkrd
        return (a_n, a_r, a_kn)

    outs = [_sds(qa.shape, BF16), _sds(kv.shape, BF16), _sds((S, LANES), F32)]
    accs = [_sds((1, HD), F32), _sds((1, LANES), F32), _sds((1, HD), F32)]
    return _rowwise(name, fn, [dq, dk, dv, qa, kv, cm, sm], gains, outs, accs, tm=128)


def _seg64_r(x):
    low = _lane(x.shape) < 64
    x2 = x * x
    s0 = _rsum(jnp.where(low, x2, 0.0))
    s1 = _rsum(jnp.where(low, 0.0, x2))
    return jnp.where(low, _rms_r(s0, SWA_D), _rms_r(s1, SWA_D)), low


def _swa_prep(name, qkv, tabs, gains):
    S = qkv.shape[0]
    nq, nk = SWA_HEADS * SWA_D, SWA_KV * SWA_D
    cs, ss = tabs

    def fn(r, c, o):
        x_r, csv, ssv = r[0], r[1][...], r[2][...]
        g_q, g_k = c[0][...], c[1][...]
        for g in range((nq + nk) // LANES):
            x = x_r[:, LANES * g:LANES * (g + 1)]
            rr, _ = _seg64_r(x)
            y = _rope(x * rr * (g_q if g < nq // LANES else g_k), csv, ssv, SWA_D, SWA_ROT // 2)
            if g < nq // LANES:
                o[0][:, LANES * g:LANES * (g + 1)] = (y * SWA_SCALE).astype(BF16)
            else:
                o[1][:, LANES * g - nq:LANES * (g + 1) - nq] = y.astype(BF16)
        o[2][...] = x_r[:, nq + nk:nq + 2 * nk].astype(BF16)
        return ()

    outs = [_sds((S, nq), BF16), _sds((S, nk), BF16), _sds((S, nk), BF16)]
    return _rowwise(name, fn, [qkv, cs, ss], gains, outs)


def _swa_prep_bwd(name, dq, dk, dv, qkv, tabs, gains):
    nq, nk = SWA_HEADS * SWA_D, SWA_KV * SWA_D
    cs, ss = tabs

    def fn(r, c, o):
        dq_r, dk_r, x_r, csv, ssv = r[0], r[1], r[3], r[4][...], r[5][...]
        g_q, g_k = c[0][...], c[1][...]
        acc = [jnp.zeros((1, LANES), F32), jnp.zeros((1, LANES), F32)]
        for g in range((nq + nk) // LANES):
            isq = g < nq // LANES
            x = x_r[:, LANES * g:LANES * (g + 1)]
            rr, low = _seg64_r(x)
            d = dq_r[:, LANES * g:LANES * (g + 1)] * SWA_SCALE if isq else dk_r[:, LANES * g - nq:LANES * (g + 1) - nq]
            d = _rope_bwd(d, csv, ssv, SWA_D, SWA_ROT // 2)
            dyg = d * (g_q if isq else g_k)
            t = dyg * x
            d0 = _rsum(jnp.where(low, t, 0.0))
            d1 = _rsum(jnp.where(low, 0.0, t))
            dot = jnp.where(low, d0, d1) * (1.0 / SWA_D)
            o[0][:, LANES * g:LANES * (g + 1)] = (rr * dyg - x * (rr * rr * rr) * dot).astype(BF16)
            acc[0 if isq else 1] = acc[0 if isq else 1] + _csum(d * x * rr)
        o[0][:, nq + nk:nq + 2 * nk] = r[2][...].astype(BF16)
        return tuple(acc)

    return _rowwise(name, fn, [dq, dk, dv, qkv, cs, ss], gains, [_sds(qkv.shape, BF16)],
                    [_sds((1, LANES), F32), _sds((1, LANES), F32)], tm=128)


def _delta(name, do, o_, width):
    S, C = do.shape
    nh = C // width

    def fn(r, c, o):
        for g in range(C // LANES):
            t = r[0][:, LANES * g:LANES * (g + 1)].astype(F32) * r[1][:, LANES * g:LANES * (g + 1)].astype(F32)
            if width == LANES:
                o[0][g] = _rsum(t)
            else:
                low = _lane(t.shape) < 64
                o[0][2 * g] = _rsum(jnp.where(low, t, 0.0))
                o[0][2 * g + 1] = _rsum(jnp.where(low, 0.0, t))
        return ()

    return _rowwise(name, fn, [do, o_], [], [_sds((nh, S, 1), F32)])[0]


_NT = (((1,), (1,)), ((), ()))
_NN = (((1,), (0,)), ((), ()))
_TN = (((0,), (0,)), ((), ()))


def _flash_fwd(name, q, k, v, tq=512, tk=2048):
    H, S, dk = q.shape
    G = H // k.shape[0]
    dv = v.shape[2]
    tq, tk = _tile(S, (tq, 256, 128)), _tile(S, (tk, 256, 128))
    nk = S // tk

    def body(q_ref, k_ref, v_ref, o_ref, lse_ref, m_s, l_s, acc_s):
        j = pl.program_id(2)

        @pl.when(j == 0)
        def _():
            m_s[...] = jnp.full_like(m_s, -jnp.inf)
            l_s[...] = jnp.zeros_like(l_s)
            acc_s[...] = jnp.zeros_like(acc_s)

        s = lax.dot_general(q_ref[...], k_ref[...], _NT, preferred_element_type=F32)
        m_new = jnp.maximum(m_s[...], jnp.max(s, axis=-1, keepdims=True))
        alpha = jnp.exp(m_s[...] - m_new)
        p = jnp.exp(s - m_new)
        l_s[...] = alpha * l_s[...] + _rsum(p)
        acc_s[...] = alpha * acc_s[...] + lax.dot_general(p.astype(BF16), v_ref[...], _NN,
                                                          preferred_element_type=F32)
        m_s[...] = m_new

        @pl.when(j == nk - 1)
        def _():
            o_ref[...] = (acc_s[...] / l_s[...]).astype(o_ref.dtype)
            lse_ref[...] = m_s[...] + jnp.log(l_s[...])

    return pl.pallas_call(
        body, name=name, grid=(H, S // tq, nk),
        in_specs=[pl.BlockSpec((None, tq, dk), lambda h, i, j: (h, i, 0)),
                  pl.BlockSpec((None, tk, dk), lambda h, i, j: (h // G, j, 0)),
                  pl.BlockSpec((None, tk, dv), lambda h, i, j: (h // G, j, 0))],
        out_specs=[pl.BlockSpec((tq, dv), lambda h, i, j: (i, h)),
                   pl.BlockSpec((None, tq, 1), lambda h, i, j: (h, i, 0))],
        out_shape=[_sds((S, H * dv), BF16), _sds((H, S, 1), F32)],
        scratch_shapes=[pltpu.VMEM((tq, 1), F32), pltpu.VMEM((tq, 1), F32), pltpu.VMEM((tq, dv), F32)],
        compiler_params=_params(("parallel", "parallel", "arbitrary")),
    )(q, k, v)


def _flash_bwd(name, q, k, v, do, head0, lse, delta, tq=512, tk=1024):
    H, S, dk = q.shape
    G = H // k.shape[0]
    dv = v.shape[2]
    tq, tk = _tile(S, (tq, 256, 128)), _tile(S, (tk, 256, 128))
    nq = S // tq

    def body(q_ref, k_ref, v_ref, do_ref, lse_ref, dl_ref, dq_ref, dk_ref, dv_ref, dk_s, dv_s):
        j, i = pl.program_id(1), pl.program_id(2)

        @pl.when(i == 0)
        def _():
            dk_s[...] = jnp.zeros_like(dk_s)
            dv_s[...] = jnp.zeros_like(dv_s)

        qv, kv_, dov = q_ref[...], k_ref[...], do_ref[...]
        s = lax.dot_general(qv, kv_, _NT, preferred_element_type=F32)
        p = jnp.exp(s - lse_ref[...])
        dp = lax.dot_general(dov, v_ref[...], _NT, preferred_element_type=F32)
        ds = (p * (dp - dl_ref[...])).astype(BF16)
        dv_s[...] += lax.dot_general(p.astype(BF16), dov, _TN, preferred_element_type=F32)
        dk_s[...] += lax.dot_general(ds, qv, _TN, preferred_element_type=F32)
        dqi = lax.dot_general(ds, kv_, _NN, preferred_element_type=F32)
        rows = pl.ds(pl.multiple_of(i * tq, tq), tq)

        @pl.when(j == 0)
        def _():
            dq_ref[rows, :] = dqi

        @pl.when(j > 0)
        def _():
            dq_ref[rows, :] += dqi

        @pl.when(i == nq - 1)
        def _():
            dk_ref[...] = dk_s[...]
            dv_ref[...] = dv_s[...]

    return pl.pallas_call(
        body, name=name, grid=(H, S // tk, nq),
        in_specs=[pl.BlockSpec((None, tq, dk), lambda h, j, i: (h, i, 0)),
                  pl.BlockSpec((None, tk, dk), lambda h, j, i: (h // G, j, 0)),
                  pl.BlockSpec((None, tk, dv), lambda h, j, i: (h // G, j, 0)),
                  pl.BlockSpec((tq, dv), lambda h, j, i: (i, head0 + h)),
                  pl.BlockSpec((None, tq, 1), lambda h, j, i: (head0 + h, i, 0)),
                  pl.BlockSpec((None, tq, 1), lambda h, j, i: (head0 + h, i, 0))],
        out_specs=[pl.BlockSpec((None, S, dk), lambda h, j, i: (h, 0, 0)),
                   pl.BlockSpec((None, tk, dk), lambda h, j, i: (h, j, 0)),
                   pl.BlockSpec((None, tk, dv), lambda h, j, i: (h, j, 0))],
        out_shape=[_sds((H, S, dk), F32), _sds((H, S, dk), F32), _sds((H, S, dv), F32)],
        scratch_shapes=[pltpu.VMEM((tk, dk), F32), pltpu.VMEM((tk, dv), F32)],
        compiler_params=_params(("parallel", "arbitrary", "arbitrary")),
    )(q, k, v, do, lse, delta)


def _swa_place(ref128, h):
    e, t = h % 2, (h // (SWA_HEADS // SWA_KV)) % 2
    x = ref128.astype(F32)
    if e != t:
        x = pltpu.roll(x, 64, 1)
    return jnp.where((_lane(x.shape) >= 64) == bool(t), x, 0.0).astype(BF16)


def _swa_unplace(y, h):
    e, t = h % 2, (h // (SWA_HEADS // SWA_KV)) % 2
    return pltpu.roll(y, 64, 1) if e != t else y


def _swa_specs(width, nb):
    prev = pl.BlockSpec((SWA_BLOCK, width), lambda i: (jnp.maximum(i - 1, 0), 0))
    cur = pl.BlockSpec((SWA_BLOCK, width), lambda i: (i, 0))
    nxt = pl.BlockSpec((SWA_BLOCK, width), lambda i: (jnp.minimum(i + 1, nb - 1), 0))
    return [prev, cur, nxt]


def _swa_specs3(nh, nb):
    prev = pl.BlockSpec((nh, SWA_BLOCK, 1), lambda i: (0, jnp.maximum(i - 1, 0), 0))
    cur = pl.BlockSpec((nh, SWA_BLOCK, 1), lambda i: (0, i, 0))
    nxt = pl.BlockSpec((nh, SWA_BLOCK, 1), lambda i: (0, jnp.minimum(i + 1, nb - 1), 0))
    return [prev, cur, nxt]


def _swa_valid_q(i, S):
    qpos = i * SWA_BLOCK + lax.broadcasted_iota(jnp.int32, (SWA_BLOCK, 3 * SWA_BLOCK), 0)
    kpos = (i - 1) * SWA_BLOCK + lax.broadcasted_iota(jnp.int32, (SWA_BLOCK, 3 * SWA_BLOCK), 1)
    return (jnp.abs(qpos - kpos) <= SWA_WINDOW) & (kpos >= 0) & (kpos < S)


def _swa_fwd(name, q, k, v, sink):
    S = q.shape[0]
    nb = S // SWA_BLOCK
    grp = SWA_HEADS // SWA_KV
    smem = pl.BlockSpec(memory_space=pltpu.SMEM)

    def body(sink_ref, q_ref, kp, kc, kn, vp, vc, vn, o_ref, lse_ref):
        i = pl.program_id(0)
        valid = _swa_valid_q(i, S)
        kcat = [jnp.concatenate([r[:, LANES * u:LANES * (u + 1)] for r in (kp, kc, kn)], axis=0) for u in range(2)]
        vcat = [jnp.concatenate([r[:, LANES * u:LANES * (u + 1)] for r in (vp, vc, vn)], axis=0) for u in range(2)]
        for g in range(SWA_HEADS // 2):
            pair = []
            for h in (2 * g, 2 * g + 1):
                u = (h // grp) // 2
                xq = _swa_place(q_ref[:, LANES * g:LANES * (g + 1)], h)
                s = lax.dot_general(xq, kcat[u], _NT, preferred_element_type=F32)
                s = jnp.where(valid, s, -jnp.inf)
                sk = sink_ref[0, h]
                m = jnp.maximum(jnp.max(s, axis=-1, keepdims=True), sk)
                p = jnp.exp(s - m)
                den = _rsum(p) + jnp.exp(sk - m)
                oh = lax.dot_general((p / den).astype(BF16), vcat[u], _NN, preferred_element_type=F32)
                lse_ref[h] = m + jnp.log(den)
                pair.append(_swa_unplace(oh, h))
            o_ref[:, LANES * g:LANES * (g + 1)] = jnp.where(_lane(pair[0].shape) < 64, pair[0], pair[1]).astype(BF16)

    return pl.pallas_call(
        body, name=name, grid=(nb,),
        in_specs=[smem, pl.BlockSpec((SWA_BLOCK, q.shape[1]), lambda i: (i, 0))]
        + _swa_specs(k.shape[1], nb) + _swa_specs(v.shape[1], nb),
        out_specs=[pl.BlockSpec((SWA_BLOCK, q.shape[1]), lambda i: (i, 0)),
                   pl.BlockSpec((SWA_HEADS, SWA_BLOCK, 1), lambda i: (0, i, 0))],
        out_shape=[_sds(q.shape, BF16), _sds((SWA_HEADS, S, 1), F32)],
        compiler_params=_params(("parallel",)),
    )(sink, q, k, k, k, v, v, v)


def _swa_bwd_q(name, q, k, v, do, lse, delta, sink):
    S = q.shape[0]
    nb = S // SWA_BLOCK
    grp = SWA_HEADS // SWA_KV
    smem = pl.BlockSpec(memory_space=pltpu.SMEM)
    row = pl.BlockSpec((SWA_BLOCK, q.shape[1]), lambda i: (i, 0))
    hrow = pl.BlockSpec((SWA_HEADS, SWA_BLOCK, 1), lambda i: (0, i, 0))

    def body(sink_ref, q_ref, do_ref, lse_ref, dl_ref, kp, kc, kn, vp, vc, vn, dq_ref, ds_ref):
        i = pl.program_id(0)

        @pl.when(i == 0)
        def _():
            ds_ref[...] = jnp.zeros_like(ds_ref)

        valid = _swa_valid_q(i, S)
        kcat = [jnp.concatenate([r[:, LANES * u:LANES * (u + 1)] for r in (kp, kc, kn)], axis=0) for u in range(2)]
        vcat = [jnp.concatenate([r[:, LANES * u:LANES * (u + 1)] for r in (vp, vc, vn)], axis=0) for u in range(2)]
        for g in range(SWA_HEADS // 2):
            pair = []
            for h in (2 * g, 2 * g + 1):
                u = (h // grp) // 2
                xq = _swa_place(q_ref[:, LANES * g:LANES * (g + 1)], h)
                xdo = _swa_place(do_ref[:, LANES * g:LANES * (g + 1)], h)
                lse_h, dl_h = lse_ref[h], dl_ref[h]
                s = lax.dot_general(xq, kcat[u], _NT, preferred_element_type=F32)
                p = jnp.exp(jnp.where(valid, s, -jnp.inf) - lse_h)
                dp = lax.dot_general(xdo, vcat[u], _NT, preferred_element_type=F32)
                dsv = (p * (dp - dl_h)).astype(BF16)
                dqh = lax.dot_general(dsv, kcat[u], _NN, preferred_element_type=F32)
                pair.append(_swa_unplace(dqh, h))
                dsink = -_csum(jnp.exp(sink_ref[0, h] - lse_h) * dl_h)
                ds_ref[h:h + 1, :] += jnp.broadcast_to(dsink, (1, LANES))
            dq_ref[:, LANES * g:LANES * (g + 1)] = jnp.where(_lane(pair[0].shape) < 64, pair[0], pair[1])

    return pl.pallas_call(
        body, name=name, grid=(nb,),
        in_specs=[smem, row, row, hrow, hrow] + _swa_specs(k.shape[1], nb) + _swa_specs(v.shape[1], nb),
        out_specs=[row, pl.BlockSpec((SWA_HEADS, LANES), lambda i: (0, 0))],
        out_shape=[_sds(q.shape, F32), _sds((SWA_HEADS, LANES), F32)],
        compiler_params=_params(("arbitrary",)),
    )(sink, q, do, lse, delta, k, k, k, v, v, v)


def _swa_bwd_kv(name, q, k, v, do, lse, delta):
    S = q.shape[0]
    nb = S // SWA_BLOCK
    grp = SWA_HEADS // SWA_KV
    krow = pl.BlockSpec((SWA_BLOCK, k.shape[1]), lambda i: (i, 0))

    def body(qp, qc, qn, dop, doc, don, lp, lc, ln, dp_, dc_, dn_, k_ref, v_ref, dk_ref, dv_ref):
        j = pl.program_id(0)
        qpos = (j - 1) * SWA_BLOCK + lax.broadcasted_iota(jnp.int32, (3 * SWA_BLOCK, SWA_BLOCK), 0)
        kpos = j * SWA_BLOCK + lax.broadcasted_iota(jnp.int32, (3 * SWA_BLOCK, SWA_BLOCK), 1)
        valid = (jnp.abs(qpos - kpos) <= SWA_WINDOW) & (qpos >= 0) & (qpos < S)
        for u in range(SWA_KV // 2):
            kc = k_ref[:, LANES * u:LANES * (u + 1)]
            vc = v_ref[:, LANES * u:LANES * (u + 1)]
            dk_a = jnp.zeros((SWA_BLOCK, LANES), F32)
            dv_a = jnp.zeros((SWA_BLOCK, LANES), F32)
            for h in range(2 * u * grp, (2 * u + 2) * grp):
                g = h // 2
                xq = jnp.concatenate([_swa_place(r[:, LANES * g:LANES * (g + 1)], h) for r in (qp, qc, qn)], axis=0)
                xdo = jnp.concatenate([_swa_place(r[:, LANES * g:LANES * (g + 1)], h) for r in (dop, doc, don)], axis=0)
                lse_h = jnp.concatenate([lp[h], lc[h], ln[h]], axis=0)
                dl_h = jnp.concatenate([dp_[h], dc_[h], dn_[h]], axis=0)
                s = lax.dot_general(xq, kc, _NT, preferred_element_type=F32)
                p = jnp.where(valid, jnp.exp(s - lse_h), 0.0)
                dpv = lax.dot_general(xdo, vc, _NT, preferred_element_type=F32)
                dsv = (p * (dpv - dl_h)).astype(BF16)
                dv_a = dv_a + lax.dot_general(p.astype(BF16), xdo, _TN, preferred_element_type=F32)
                dk_a = dk_a + lax.dot_general(dsv, xq, _TN, preferred_element_type=F32)
            dk_ref[:, LANES * u:LANES * (u + 1)] = dk_a
            dv_ref[:, LANES * u:LANES * (u + 1)] = dv_a

    return pl.pallas_call(
        body, name=name, grid=(nb,),
        in_specs=_swa_specs(q.shape[1], nb) + _swa_specs(do.shape[1], nb)
        + _swa_specs3(SWA_HEADS, nb) + _swa_specs3(SWA_HEADS, nb) + [krow, krow],
        out_specs=[krow, krow],
        out_shape=[_sds(k.shape, F32), _sds(v.shape, F32)],
        compiler_params=_params(("parallel",)),
    )(q, q, q, do, do, do, lse, lse, lse, delta, delta, delta, k, v)


def _exchange(name, tensors):
    n = len(tensors)
    outs = [_sds(a.shape if sc else (N_DEV,) + a.shape, a.dtype) for a, sc in tensors]

    def body(*refs):
        ins, dst, send, recv, loc = refs[:n], refs[n:2 * n], refs[2 * n], refs[2 * n + 1], refs[2 * n + 2]
        x, y, c = lax.axis_index("x"), lax.axis_index("y"), lax.axis_index("c")
        me = 4 * x + 2 * y + c

        def peer(kk):
            return (x ^ ((kk >> 2) & 1), y ^ ((kk >> 1) & 1), c ^ (kk & 1))

        def copy(t, kk):
            px, py, pc = peer(kk)
            src = ins[t].at[4 * px + 2 * py + pc] if tensors[t][1] else ins[t]
            return pltpu.make_async_remote_copy(src_ref=src, dst_ref=dst[t].at[me], send_sem=send.at[t, kk - 1],
                                                recv_sem=recv.at[t, kk - 1], device_id=(px, py, pc),
                                                device_id_type=pl.DeviceIdType.MESH)

        def arrival(t, kk):
            px, py, pc = peer(kk)
            src = ins[t].at[me] if tensors[t][1] else ins[t]
            return pltpu.make_async_remote_copy(src_ref=src, dst_ref=dst[t].at[4 * px + 2 * py + pc],
                                                send_sem=send.at[t, kk - 1], recv_sem=recv.at[t, kk - 1],
                                                device_id=(px, py, pc), device_id_type=pl.DeviceIdType.MESH)

        own = [pltpu.make_async_copy(ins[t].at[me] if tensors[t][1] else ins[t], dst[t].at[me], loc.at[t])
               for t in range(n)]
        sent = [copy(t, kk) for t in range(n) for kk in range(1, N_DEV)]
        for cp in own + sent:
            cp.start()
        for t in range(n):
            for kk in range(1, N_DEV):
                arrival(t, kk).wait_recv()
        for cp in sent:
            cp.wait_send()
        for cp in own:
            cp.wait()

    any_spec = pl.BlockSpec(memory_space=pl.ANY)
    return pl.pallas_call(
        body, name=name, in_specs=[any_spec] * n, out_specs=[any_spec] * n, out_shape=outs,
        scratch_shapes=[pltpu.SemaphoreType.DMA((n, N_DEV - 1)), pltpu.SemaphoreType.DMA((n, N_DEV - 1)),
                        pltpu.SemaphoreType.DMA((n,))],
        compiler_params=pltpu.CompilerParams(has_side_effects=True),
    )(*[a for a, _ in tensors])


def _sum8(name, parts):
    _, R, C = parts.shape

    def fn(r, c, o):
        acc = r[0][0].astype(F32)
        for s in range(1, N_DEV):
            acc = acc + r[0][s].astype(F32)
        o[0][...] = acc
        return ()

    tm = R if R % 256 else 256
    return _rowwise(name, fn, [parts], [], [_sds((R, C), F32)], tm=tm)[0]


def _adamw(name, w, g, m, v):
    bc1 = 1.0 - ADAM_B1 ** ADAM_STEP
    bc2 = 1.0 - ADAM_B2 ** ADAM_STEP

    def fn(r, c, o):
        wv, gv, mv, vv = (x[...] for x in r)
        mn = ADAM_B1 * mv + (1.0 - ADAM_B1) * gv
        vn = ADAM_B2 * vv + (1.0 - ADAM_B2) * (gv * gv)
        o[0][...] = -ADAM_LR * ((mn / bc1) / (jnp.sqrt(vn / bc2) + ADAM_EPS) + ADAM_WD * wv)
        o[1][...] = mn
        o[2][...] = vn
        return ()

    tm = max(8, min(512, 1 << ((16 << 20) // (56 * w.shape[1])).bit_length() - 1))
    return _rowwise(name, fn, [w, g, m, v], [], [_sds(w.shape, F32)] * 3, tm=tm)


def _rope_cs(pos, dim, theta):
    inv = jnp.float32(theta) ** (-jnp.arange(0, dim, 2, dtype=jnp.float32) / dim)
    ang = pos.astype(jnp.float32)[:, None] * inv[None, :]
    return jnp.cos(ang), jnp.sin(ang)


def _tables(S):
    pos = jnp.arange(S)
    c, s = _rope_cs(pos, MLA_ROPE, ROPE_THETA)
    mla = (jnp.concatenate([c, c, c, c], 1), jnp.concatenate([-s, s, -s, s], 1))
    rc, rs = _rope_cs(pos // GRID_W, HD // 2, AXIAL_THETA)
    cc, cs = _rope_cs(pos % GRID_W, HD // 2, AXIAL_THETA)
    axial = (jnp.concatenate([rc, rc, cc, cc], 1), jnp.concatenate([-rs, rs, -cs, cs], 1))
    c, s = _rope_cs(pos, SWA_ROT, ROPE_THETA)
    one, zero = jnp.ones((S, SWA_D - SWA_ROT), F32), jnp.zeros((S, SWA_D - SWA_ROT), F32)
    swa = (jnp.concatenate([c, c, one, c, c, one], 1), jnp.concatenate([-s, s, zero, -s, s, zero], 1))
    return mla, axial, swa


_WEIGHTS = ['even_norm', 'even_w_in', 'mla_q_lat_norm', 'mla_kv_lat_norm', 'mla_w_uq', 'mla_w_ukv', 'mla_q_norm',
            'mla_k_nope_norm', 'mla_k_rope_norm', 'gqa_q_norm', 'gqa_k_norm', 'even_w_out', 'odd_norm', 'odd_w_qkv',
            'swa_q_norm', 'swa_k_norm', 'swa_sink', 'odd_w_out', 'mlp_norm', 'mlp_w_up', 'mlp_w_down']
_SMALL = ['even_norm', 'mla_q_lat_norm', 'mla_kv_lat_norm', 'mla_q_norm', 'mla_k_nope_norm', 'mla_k_rope_norm',
          'gqa_q_norm', 'gqa_k_norm', 'swa_q_norm', 'swa_k_norm', 'swa_sink', 'mlp_norm']


def _relu2(acc):
    rl = jnp.maximum(acc, 0.0)
    return rl * rl, rl


def _mul2(acc, rl):
    return (acc * (2.0 * rl.astype(F32)),)


def _add(acc, res):
    return (acc + res,)


def _step(x, tgt, w, m, v):
    S, D = x.shape
    nw = len(_WEIGHTS)
    tab_mla, tab_ax, tab_swa = _tables(S)
    bf = lambda a: a.astype(BF16)

    w_up_s, w_dn_s = w['mlp_w_up'], w['mlp_w_down']
    gathered = _exchange("gather_weights", [
        (bf(w['even_w_in'][0].T), False), (bf(w['mla_w_uq'][0].T), False), (bf(w['mla_w_ukv'][0].T), False),
        (bf(w['even_w_out'][0]), False), (bf(w['odd_w_qkv'][0].T), False), (bf(w['odd_w_out'][0]), False),
        (bf(w_up_s[0].T), False), (bf(w_up_s[1].T), False), (bf(w_dn_s[0]), False), (bf(w_dn_s[1]), False),
        (w['odd_norm'], False)])
    flat = lambda a: a.reshape((a.shape[0] * a.shape[1],) + a.shape[2:])
    win_t, wuq_g, wukv_t, wout_e, wqkv_t, wout_o, wup0_t, wup1_t, wdn0, wdn1 = [flat(a) for a in gathered[:10]]
    odd_norm = gathered[10].reshape(1, D)
    win_t = jnp.concatenate([win_t[:KR_END], jnp.zeros((LANES - MLA_ROPE, D), BF16), win_t[KR_END:]], 0)
    wuq_g = gathered[1]
    wuq_t = jnp.concatenate([wuq_g[:, :HD].reshape(HEADS * HD, Q_LORA),
                             wuq_g[:, HD:].reshape(HEADS * MLA_ROPE, Q_LORA)], 0)
    wup_t, wdn = (wup0_t, wup1_t), (wdn0, wdn1)

    z64 = jnp.zeros((1, 64), F32)
    qn = w['mla_q_norm']
    g_even = [w['mla_q_lat_norm'], w['mla_kv_lat_norm'], jnp.concatenate([w['mla_k_rope_norm'], z64], 1),
              w['gqa_q_norm'], w['gqa_k_norm']]
    g_mla = [qn[:, :HD], jnp.concatenate([qn[:, HD:], qn[:, HD:]], 1), w['mla_k_nope_norm']]
    g_swa = [jnp.concatenate([w['swa_q_norm']] * 2, 1), jnp.concatenate([w['swa_k_norm']] * 2, 1)]

    def mlp_fwd(l, xin):
        hn = _rmsnorm(f"mlp{l}_norm", xin, w['mlp_norm'][l:l + 1])
        a, rl = _mm(f"mlp{l}_up", hn, wup_t[l], "nt", (BF16, BF16), epilogue=_relu2)
        xout = _mm(f"mlp{l}_down", a, wdn[l], "nn", (F32,), epilogue=_add, extras=(xin,))
        return xout, (hn, a, rl)

    h0 = _rmsnorm("even_norm", x, w['even_norm'])
    proj = _mm("even_in", h0, win_t, "nt", (F32,), tm=512, tn=P_END)
    cqn, ckvn, krd, qg, kg, vg = _even_prep("even_prep", proj, tab_mla + tab_ax, g_even)
    qa = _mm("mla_uq", cqn, wuq_t, "nt", (F32,))
    kv = _mm("mla_ukv", ckvn, wukv_t, "nt", (F32,))
    q_a, k_a, v_a = _mla_prep("mla_prep", qa, kv, krd, tab_mla, g_mla)
    o_a, lse_a = _flash_fwd("mla_attn", q_a, k_a, v_a)
    o_g, lse_g = _flash_fwd("gqa_attn", qg, kg, vg)
    merged = jnp.concatenate([o_a, o_g], 1)
    x1 = _mm("even_out", merged, wout_e, "nn", (F32,), epilogue=_add, extras=(x,))
    x2, mlp0 = mlp_fwd(0, x1)

    h1 = _rmsnorm("odd_norm", x2, odd_norm)
    qkv = _mm("odd_qkv", h1, wqkv_t, "nt", (F32,))
    q_s, k_s, v_s = _swa_prep("swa_prep", qkv, tab_swa, g_swa)
    o_s, lse_s = _swa_fwd("swa_attn", q_s, k_s, v_s, w['swa_sink'])
    x3 = _mm("odd_out", o_s, wout_o, "nn", (F32,), epilogue=_add, extras=(x2,))
    x4, mlp1 = mlp_fwd(1, x3)

    dy, loss_acc = _loss("loss", x4, tgt)
    loss = lax.psum(0.5 / D * loss_acc[0, 0], ("x", "y", "c"))

    gsm = {}

    def mlp_bwd(l, dout, xin, saved):
        hn, a, rl = saved
        du = _mm(f"mlp{l}_dact", dout, wdn[l], "nt", (BF16,), epilogue=_mul2, extras=(rl,))
        g_dn = _mm(f"mlp{l}_gdown", a, dout, "tn", (BF16,))
        g_up = _mm(f"mlp{l}_gup", du, hn, "tn", (BF16,))
        dhn = _mm(f"mlp{l}_dnorm", du, wup_t[l], "nn", (F32,))
        din, g_n = _rmsnorm_bwd(f"mlp{l}_norm_bwd", dout, xin, dhn, w['mlp_norm'][l:l + 1])
        return din, g_dn, g_up, g_n

    dx3, g_dn1, g_up1, g_mn1 = mlp_bwd(1, dy, x3, mlp1)

    g_wout_o = _mm("odd_gout", o_s, dx3, "tn", (BF16,))
    do_s = _mm("odd_dattn", dx3, wout_o, "nt", (BF16,))
    dl_s = _delta("swa_delta", do_s, o_s, SWA_D)
    dq_s, dsink = _swa_bwd_q("swa_bwd_q", q_s, k_s, v_s, do_s, lse_s, dl_s, w['swa_sink'])
    dk_s, dv_s = _swa_bwd_kv("swa_bwd_kv", q_s, k_s, v_s, do_s, lse_s, dl_s)
    dqkv, g_sq, g_sk = _swa_prep_bwd("swa_prep_bwd", dq_s, dk_s, dv_s, qkv, tab_swa, g_swa)
    g_wqkv = _mm("odd_gqkv", dqkv, h1, "tn", (BF16,))
    dh1 = _mm("odd_dnorm", dqkv, wqkv_t, "nn", (F32,))
    dx2, g_on = _rmsnorm_bwd("odd_norm_bwd", dx3, x2, dh1, odd_norm)
    gsm['swa_q_norm'] = g_sq[:, :64] + g_sq[:, 64:]
    gsm['swa_k_norm'] = g_sk[:, :64] + g_sk[:, 64:]
    gsm['swa_sink'] = dsink[:, 0].reshape(1, SWA_HEADS)

    dx1, g_dn0, g_up0, g_mn0 = mlp_bwd(0, dx2, x1, mlp0)
    gsm['mlp_norm'] = jnp.concatenate([g_mn0, g_mn1], 0)

    g_wout_e = _mm("even_gout", merged, dx1, "tn", (BF16,))
    dmerged = _mm("even_dattn", dx1, wout_e, "nt", (BF16,))
    dl_e = _delta("even_delta", dmerged, merged, HD)
    lse_e = jnp.concatenate([lse_a, lse_g], 0)
    dq_a, dk_a, dv_a = _flash_bwd("mla_attn_bwd", q_a, k_a, v_a, dmerged, 0, lse_e, dl_e)
    dq_g, dk_g, dv_g = _flash_bwd("gqa_attn_bwd", qg, kg, vg, dmerged, HEADS, lse_e, dl_e)
    dqa, dkv, dkrd, g_qnn, g_qnr, g_kn = _mla_prep_bwd("mla_prep_bwd", dq_a, dk_a, dv_a, qa, kv, tab_mla, g_mla)
    g_wuq = _mm("mla_guq", dqa, cqn, "tn", (BF16,))
    dcqn = _mm("mla_dq_lat", dqa, wuq_t, "nn", (F32,))
    g_wukv = _mm("mla_gukv", dkv, ckvn, "tn", (BF16,))
    dckvn = _mm("mla_dkv_lat", dkv, wukv_t, "nn", (F32,))
    dproj, g_ql, g_kvl, g_kr, g_gq, g_gk = _even_prep_bwd("even_prep_bwd", dcqn, dckvn, dkrd, dq_g, dk_g, dv_g,
                                                          proj, tab_mla + tab_ax, g_even)
    g_win = _mm("even_gin", dproj, h0, "tn", (BF16,), tm=P_END)
    dh0 = _mm("even_dnorm", dproj, win_t, "nn", (F32,), tk=P_END)
    grad_x, g_en = _rmsnorm_bwd("even_norm_bwd", dx1, x, dh0, w['even_norm'])
    gsm.update(even_norm=g_en, mla_q_lat_norm=g_ql, mla_kv_lat_norm=g_kvl,
               mla_q_norm=jnp.concatenate([g_qnn, g_qnr[:, :64] + g_qnr[:, 64:]], 1), mla_k_nope_norm=g_kn,
               mla_k_rope_norm=g_kr[:, :64], gqa_q_norm=g_gq, gqa_k_norm=g_gk)

    g_win = jnp.concatenate([g_win[:KR_END], g_win[P_QG:]], 0)
    g_wuq = jnp.concatenate([g_wuq[:HEADS * HD].reshape(HEADS, HD, Q_LORA),
                             g_wuq[HEADS * HD:].reshape(HEADS, MLA_ROPE, Q_LORA)], 1)
    split = lambda a: a.reshape((N_DEV, a.shape[0] // N_DEV) + a.shape[1:])
    small_sizes = [w[n].size for n in _SMALL] + [D]
    small_vec = jnp.concatenate([gsm[n].reshape(1, -1) for n in _SMALL] + [g_on], 1)
    pad = (-small_vec.shape[1]) % LANES
    small_vec = jnp.pad(small_vec, ((0, 0), (0, pad)))
    parts = _exchange("scatter_grads", [
        (split(g_win), True), (g_wuq, True), (split(g_wukv), True), (split(g_wout_e), True), (split(g_wqkv), True),
        (split(g_wout_o), True), (split(g_up0), True), (split(g_up1), True), (split(g_dn0), True),
        (split(g_dn1), True), (small_vec, False)])
    names = ['even_w_in', 'mla_w_uq', 'mla_w_ukv', 'even_w_out', 'odd_w_qkv', 'odd_w_out', 'up0', 'up1', 'dn0', 'dn1']
    red = {n: _sum8("sum_" + n, p) for n, p in zip(names, parts[:10])}
    small_g = _sum8("sum_small", parts[10])

    grads = {
        'even_w_in': red['even_w_in'].T[None], 'mla_w_uq': red['mla_w_uq'].T[None],
        'mla_w_ukv': red['mla_w_ukv'].T[None], 'even_w_out': red['even_w_out'][None],
        'odd_w_qkv': red['odd_w_qkv'].T[None], 'odd_w_out': red['odd_w_out'][None],
        'mlp_w_up': jnp.stack([red['up0'].T, red['up1'].T]), 'mlp_w_down': jnp.stack([red['dn0'], red['dn1']]),
    }
    off = 0
    for n, sz in zip(_SMALL + ['odd_norm_full'], small_sizes):
        seg = small_g[:, off:off + sz]
        off += sz
        if n == 'odd_norm_full':
            me = 4 * lax.axis_index("x") + 2 * lax.axis_index("y") + lax.axis_index("c")
            grads['odd_norm'] = lax.dynamic_slice(seg, (0, me * (D // N_DEV)), (1, D // N_DEV))
        else:
            grads[n] = seg.reshape(w[n].shape)

    delta, new_m, new_v = {}, {}, {}
    big = ['even_w_in', 'mla_w_uq', 'mla_w_ukv', 'even_w_out', 'odd_w_qkv', 'odd_w_out', 'mlp_w_up', 'mlp_w_down']
    for n in big:
        shp = w[n].shape
        two = lambda a: a.reshape(shp[0] * shp[1], shp[2])
        d_, m_, v_ = _adamw("adamw_" + n, two(w[n]), two(grads[n]), two(m[n]), two(v[n]))
        delta[n], new_m[n], new_v[n] = d_.reshape(shp), m_.reshape(shp), v_.reshape(shp)
    sm_names = _SMALL + ['odd_norm']
    pack = lambda d: jnp.concatenate([d[n].reshape(1, -1) for n in sm_names], 1)
    pw, pg, pm, pv = pack(w), pack(grads), pack(m), pack(v)
    padw = (-pw.shape[1]) % LANES
    padf = lambda a: jnp.pad(a, ((0, 0), (0, padw)))
    d_, m_, v_ = _adamw("adamw_small", padf(pw), padf(pg), padf(pm), jnp.pad(pv, ((0, 0), (0, padw)), constant_values=1.0))
    off = 0
    for n in sm_names:
        sz = w[n].size
        delta[n] = d_[:, off:off + sz].reshape(w[n].shape)
        new_m[n] = m_[:, off:off + sz].reshape(w[n].shape)
        new_v[n] = v_[:, off:off + sz].reshape(w[n].shape)
        off += sz

    return (loss, grad_x[None], *[grads[n] for n in _WEIGHTS], *[delta[n] for n in _WEIGHTS],
            *[new_m[n] for n in _WEIGHTS], *[new_v[n] for n in _WEIGHTS])


def kernel(x, even_norm, even_w_in, mla_q_lat_norm, mla_kv_lat_norm, mla_w_uq, mla_w_ukv, mla_q_norm, mla_k_nope_norm, mla_k_rope_norm, gqa_q_norm, gqa_k_norm, even_w_out, odd_norm, odd_w_qkv, swa_q_norm, swa_k_norm, swa_sink, odd_w_out, mlp_norm, mlp_w_up, mlp_w_down, loss_target, m_even_norm, m_even_w_in, m_mla_q_lat_norm, m_mla_kv_lat_norm, m_mla_w_uq, m_mla_w_ukv, m_mla_q_norm, m_mla_k_nope_norm, m_mla_k_rope_norm, m_gqa_q_norm, m_gqa_k_norm, m_even_w_out, m_odd_norm, m_odd_w_qkv, m_swa_q_norm, m_swa_k_norm, m_swa_sink, m_odd_w_out, m_mlp_norm, m_mlp_w_up, m_mlp_w_down, v_even_norm, v_even_w_in, v_mla_q_lat_norm, v_mla_kv_lat_norm, v_mla_w_uq, v_mla_w_ukv, v_mla_q_norm, v_mla_k_nope_norm, v_mla_k_rope_norm, v_gqa_q_norm, v_gqa_k_norm, v_even_w_out, v_odd_norm, v_odd_w_qkv, v_swa_q_norm, v_swa_k_norm, v_swa_sink, v_odd_w_out, v_mlp_norm, v_mlp_w_up, v_mlp_w_down):
    ws = (even_norm, even_w_in, mla_q_lat_norm, mla_kv_lat_norm, mla_w_uq, mla_w_ukv, mla_q_norm, mla_k_nope_norm, mla_k_rope_norm, gqa_q_norm, gqa_k_norm, even_w_out, odd_norm, odd_w_qkv, swa_q_norm, swa_k_norm, swa_sink, odd_w_out, mlp_norm, mlp_w_up, mlp_w_down)
    ms = (m_even_norm, m_even_w_in, m_mla_q_lat_norm, m_mla_kv_lat_norm, m_mla_w_uq, m_mla_w_ukv, m_mla_q_norm, m_mla_k_nope_norm, m_mla_k_rope_norm, m_gqa_q_norm, m_gqa_k_norm, m_even_w_out, m_odd_norm, m_odd_w_qkv, m_swa_q_norm, m_swa_k_norm, m_swa_sink, m_odd_w_out, m_mlp_norm, m_mlp_w_up, m_mlp_w_down)
    vs = (v_even_norm, v_even_w_in, v_mla_q_lat_norm, v_mla_kv_lat_norm, v_mla_w_uq, v_mla_w_ukv, v_mla_q_norm, v_mla_k_nope_norm, v_mla_k_rope_norm, v_gqa_q_norm, v_gqa_k_norm, v_even_w_out, v_odd_norm, v_odd_w_qkv, v_swa_q_norm, v_swa_k_norm, v_swa_sink, v_odd_w_out, v_mlp_norm, v_mlp_w_up, v_mlp_w_down)
    return _step(x[0], loss_target[0], dict(zip(_WEIGHTS, ws)), dict(zip(_WEIGHTS, ms)), dict(zip(_WEIGHTS, vs)))
```

```python
import functools

import jax
import jax.numpy as jnp
from jax import lax
from jax.experimental import pallas as pl
from jax.experimental.pallas import tpu as pltpu

F32 = jnp.float32
BF16 = jnp.bfloat16

N_DEV = 8
NORM_EPS = 1e-6
ROPE_THETA = 500000.0
AXIAL_THETA = 10000.0
GRID_W = 64
HEADS = 8
GQA_KV = 2
HD = 128
MLA_ROPE = 64
MLA_QK = HD + MLA_ROPE
Q_LORA = 512
KV_LORA = 256
SWA_HEADS = 32
SWA_KV = 4
SWA_D = 64
SWA_ROT = 16
SWA_WINDOW = 128
SWA_BLOCK = 128
MLA_SCALE, GQA_SCALE, SWA_SCALE = MLA_QK ** -0.5, HD ** -0.5, SWA_D ** -0.5
LANES = 128
ADAM_LR, ADAM_B1, ADAM_B2, ADAM_EPS, ADAM_WD, ADAM_STEP = 0.001, 0.9, 0.999, 1e-08, 0.01, 10
VMEM_LIMIT = 56 * 1024 * 1024

P_CQ, P_CKV, P_KR, P_QG = 0, Q_LORA, Q_LORA + KV_LORA, Q_LORA + KV_LORA + LANES
P_KG = P_QG + HEADS * HD
P_VG = P_KG + GQA_KV * HD
P_END = P_VG + GQA_KV * HD
KR_END = Q_LORA + KV_LORA + MLA_ROPE


def _tile(n, prefs):
    for t in prefs:
        if n % t == 0 and t <= n:
            return t
    return n


def _params(sem):
    return pltpu.CompilerParams(dimension_semantics=sem, vmem_limit_bytes=VMEM_LIMIT)


_DIMS = {"nn": ((1,), (0,)), "nt": ((1,), (1,)), "tn": ((0,), (0,))}


def _mm(name, a, b, mode, out_dtypes, epilogue=None, extras=(), tm=1024, tn=1024, tk=512):
    if mode == "nn":
        (M, K), (_, N) = a.shape, b.shape
    elif mode == "nt":
        (M, K), (N, _) = a.shape, b.shape
    else:
        (K, M), (_, N) = a.shape, b.shape
    tm = _tile(M, (tm, 512, 256, 128))
    tn = _tile(N, (tn, 512, 256, 128))
    tk = _tile(K, (tk, 256, 128))
    nk = K // tk
    ne, no = len(extras), len(out_dtypes)
    if mode == "tn":
        a_spec = pl.BlockSpec((tk, tm), lambda i, j, k: (k, i))
    else:
        a_spec = pl.BlockSpec((tm, tk), lambda i, j, k: (i, k))
    if mode == "nt":
        b_spec = pl.BlockSpec((tn, tk), lambda i, j, k: (j, k))
    else:
        b_spec = pl.BlockSpec((tk, tn), lambda i, j, k: (k, j))
    o_spec = pl.BlockSpec((tm, tn), lambda i, j, k: (i, j))
    dims = (_DIMS[mode], ((), ()))

    def body(a_ref, b_ref, *rest):
        ex, outs, acc = rest[:ne], rest[ne:ne + no], rest[ne + no]
        k = pl.program_id(2)

        @pl.when(k == 0)
        def _():
            acc[...] = jnp.zeros_like(acc)

        acc[...] += lax.dot_general(a_ref[...].astype(BF16), b_ref[...].astype(BF16), dims,
                                    preferred_element_type=F32)

        @pl.when(k == nk - 1)
        def _():
            res = epilogue(acc[...], *[e[...] for e in ex]) if epilogue else (acc[...],)
            for o, r in zip(outs, res):
                o[...] = r.astype(o.dtype)

    out = pl.pallas_call(
        body, name=name, grid=(M // tm, N // tn, nk),
        in_specs=[a_spec, b_spec] + [o_spec] * ne,
        out_specs=[o_spec] * no,
        out_shape=[jax.ShapeDtypeStruct((M, N), d) for d in out_dtypes],
        scratch_shapes=[pltpu.VMEM((tm, tn), F32)],
        compiler_params=_params(("parallel", "parallel", "arbitrary")),
    )(a, b, *extras)
    return out[0] if no == 1 else out


def _rowwise(name, fn, rows, consts, outs, accs=(), tm=256):
    S = rows[0].shape[-2]
    tm = _tile(S, (tm, 128, 64, 32, 16, 8))
    nr, nc, no, na = len(rows), len(consts), len(outs), len(accs)

    def rspec(shape):
        if len(shape) == 2:
            return pl.BlockSpec((tm, shape[1]), lambda i: (i, 0))
        return pl.BlockSpec((shape[0], tm, shape[2]), lambda i: (0, i, 0))

    def cspec(shape):
        return pl.BlockSpec(tuple(shape), lambda i: (0,) * len(shape))

    def body(*refs):
        r, c = refs[:nr], refs[nr:nr + nc]
        o, a = refs[nr + nc:nr + nc + no], refs[nr + nc + no:]
        vals = fn(r, c, o)
        if na:
            @pl.when(pl.program_id(0) == 0)
            def _():
                for ar in a:
                    ar[...] = jnp.zeros_like(ar)

            for ar, v in zip(a, vals):
                ar[...] += v

    res = pl.pallas_call(
        body, name=name, grid=(S // tm,),
        in_specs=[rspec(x.shape) for x in rows] + [cspec(x.shape) for x in consts],
        out_specs=[rspec(x.shape) for x in outs] + [cspec(x.shape) for x in accs],
        out_shape=list(outs) + list(accs),
        compiler_params=_params(("arbitrary",) if na else ("parallel",)),
    )(*rows, *consts)
    return res


def _sds(shape, dtype):
    return jax.ShapeDtypeStruct(tuple(shape), dtype)


def _rsum(x):
    return jnp.sum(x, axis=-1, keepdims=True)


def _csum(x):
    return jnp.sum(x, axis=0, keepdims=True)


def _lane(shape):
    return lax.broadcasted_iota(jnp.int32, shape, 1)


def _partner(x, seg, half):
    lane = _lane(x.shape) % seg
    return jnp.where(lane < half, pltpu.roll(x, LANES - half, 1), pltpu.roll(x, half, 1))


def _rope(x, c, s, seg, half):
    return x * c + _partner(x, seg, half) * s


def _rope_bwd(dy, c, s, seg, half):
    t = _partner(dy * s, seg, half)
    if seg != 2 * half:
        t = jnp.where(_lane(dy.shape) % seg < 2 * half, t, 0.0)
    return dy * c + t


def _rms_r(ss, n):
    return lax.rsqrt(ss * (1.0 / n) + NORM_EPS)


def _rms_bwd(x, r, g, dy, dot_scale):
    dyg = dy * g
    return r * dyg - x * (r * r * r) * (_rsum(dyg * x) * dot_scale)


def _rmsnorm(name, x, g):
    D = x.shape[1]

    def fn(r, c, o):
        xv = r[0][...]
        o[0][...] = (xv * _rms_r(_rsum(xv * xv), D) * c[0][...]).astype(BF16)
        return ()

    return _rowwise(name, fn, [x], [g], [_sds(x.shape, BF16)])[0]


def _rmsnorm_bwd(name, dres, x, dh, g):
    D = x.shape[1]

    def fn(r, c, o):
        xv, dhv, gv = r[1][...], r[2][...], c[0][...]
        rr = _rms_r(_rsum(xv * xv), D)
        o[0][...] = r[0][...] + _rms_bwd(xv, rr, gv, dhv, 1.0 / D)
        return (_csum(dhv * xv * rr),)

    return _rowwise(name, fn, [dres, x, dh], [g], [_sds(x.shape, F32)], [_sds((1, D), F32)])


def _loss(name, y, t):
    D = y.shape[1]

    def fn(r, c, o):
        e = r[0][...] - r[1][...]
        o[0][...] = e * (1.0 / D)
        return (_csum(_rsum(e * e)),)

    return _rowwise(name, fn, [y, t], [], [_sds(y.shape, F32)], [_sds((1, 1), F32)])


def _even_prep(name, proj, tabs, gains):
    S = proj.shape[0]
    cm, sm, ca, sa = tabs

    def fn(r, c, o):
        p, cmv, smv, cav, sav = r[0], r[1][...], r[2][...], r[3][...], r[4][...]
        g_ql, g_kvl, g_kr, g_q, g_k = (x[...] for x in c)
        cq = p[:, P_CQ:P_CKV]
        o[0][...] = (cq * _rms_r(_rsum(cq * cq), Q_LORA) * g_ql).astype(BF16)
        ckv = p[:, P_CKV:P_KR]
        o[1][...] = (ckv * _rms_r(_rsum(ckv * ckv), KV_LORA) * g_kvl).astype(BF16)
        kr = p[:, P_KR:P_QG]
        y = _rope(kr * _rms_r(_rsum(kr * kr), MLA_ROPE) * g_kr, cmv, smv, 64, 32)
        o[2][...] = (y + pltpu.roll(y, 64, 1)).astype(BF16)
        for h in range(HEADS):
            xh = p[:, P_QG + HD * h:P_QG + HD * (h + 1)]
            o[3][h] = (_rope(xh * _rms_r(_rsum(xh * xh), HD) * g_q, cav, sav, 64, 32) * GQA_SCALE).astype(BF16)
        for h in range(GQA_KV):
            xh = p[:, P_KG + HD * h:P_KG + HD * (h + 1)]
            o[4][h] = _rope(xh * _rms_r(_rsum(xh * xh), HD) * g_k, cav, sav, 64, 32).astype(BF16)
            o[5][h] = p[:, P_VG + HD * h:P_VG + HD * (h + 1)].astype(BF16)
        return ()

    outs = [_sds((S, Q_LORA), BF16), _sds((S, KV_LORA), BF16), _sds((S, LANES), BF16),
            _sds((HEADS, S, HD), BF16), _sds((GQA_KV, S, HD), BF16), _sds((GQA_KV, S, HD), BF16)]
    return _rowwise(name, fn, [proj, cm, sm, ca, sa], gains, outs)


def _even_prep_bwd(name, dcqn, dckvn, dkrd, dqg, dkg, dvg, proj, tabs, gains):
    S = proj.shape[0]
    cm, sm, ca, sa = tabs

    def fn(r, c, o):
        p, cmv, smv, cav, sav = r[6], r[7][...], r[8][...], r[9][...], r[10][...]
        g_ql, g_kvl, g_kr, g_q, g_k = (x[...] for x in c)
        out = o[0]
        cq = p[:, P_CQ:P_CKV]
        rr = _rms_r(_rsum(cq * cq), Q_LORA)
        d = r[0][...]
        out[:, P_CQ:P_CKV] = _rms_bwd(cq, rr, g_ql, d, 1.0 / Q_LORA).astype(BF16)
        a_ql = _csum(d * cq * rr)
        ckv = p[:, P_CKV:P_KR]
        rr = _rms_r(_rsum(ckv * ckv), KV_LORA)
        d = r[1][...]
        out[:, P_CKV:P_KR] = _rms_bwd(ckv, rr, g_kvl, d, 1.0 / KV_LORA).astype(BF16)
        a_kvl = _csum(d * ckv * rr)
        kr = p[:, P_KR:P_QG]
        rr = _rms_r(_rsum(kr * kr), MLA_ROPE)
        d = r[2][...]
        d = d + pltpu.roll(d, 64, 1)
        d = _rope_bwd(d, cmv, smv, 64, 32)
        low = _lane(d.shape) < 64
        out[:, P_KR:P_QG] = jnp.where(low, _rms_bwd(kr, rr, g_kr, d, 1.0 / MLA_ROPE), 0.0).astype(BF16)
        a_kr = _csum(d * kr * rr)
        a_q = jnp.zeros((1, HD), F32)
        for h in range(HEADS):
            xh = p[:, P_QG + HD * h:P_QG + HD * (h + 1)]
            rr = _rms_r(_rsum(xh * xh), HD)
            d = _rope_bwd(r[3][h] * GQA_SCALE, cav, sav, 64, 32)
            out[:, P_QG + HD * h:P_QG + HD * (h + 1)] = _rms_bwd(xh, rr, g_q, d, 1.0 / HD).astype(BF16)
            a_q = a_q + _csum(d * xh * rr)
        a_k = jnp.zeros((1, HD), F32)
        grp = HEADS // GQA_KV
        for h in range(GQA_KV):
            xh = p[:, P_KG + HD * h:P_KG + HD * (h + 1)]
            rr = _rms_r(_rsum(xh * xh), HD)
            d = r[4][grp * h]
            dv = r[5][grp * h]
            for e in range(1, grp):
                d = d + r[4][grp * h + e]
                dv = dv + r[5][grp * h + e]
            d = _rope_bwd(d, cav, sav, 64, 32)
            out[:, P_KG + HD * h:P_KG + HD * (h + 1)] = _rms_bwd(xh, rr, g_k, d, 1.0 / HD).astype(BF16)
            out[:, P_VG + HD * h:P_VG + HD * (h + 1)] = dv.astype(BF16)
            a_k = a_k + _csum(d * xh * rr)
        return (a_ql, a_kvl, a_kr, a_q, a_k)

    accs = [_sds((1, Q_LORA), F32), _sds((1, KV_LORA), F32), _sds((1, LANES), F32),
            _sds((1, HD), F32), _sds((1, HD), F32)]
    return _rowwise(name, fn, [dcqn, dckvn, dkrd, dqg, dkg, dvg, proj, cm, sm, ca, sa], gains,
                    [_sds((S, P_END), BF16)], accs, tm=128)


def _mla_q_parts(qa, h, rp, g_n, g_r):
    nope = qa[:, HD * h:HD * (h + 1)]
    mine = (_lane(rp.shape) >= 64) == bool(h % 2)
    rpm = jnp.where(mine, rp, 0.0)
    rr = _rms_r(_rsum(nope * nope) + _rsum(rpm * rpm), MLA_QK)
    return nope, rpm, mine, rr


def _mla_prep(name, qa, kv, krd, tabs, gains):
    S = qa.shape[0]
    cm, sm = tabs

    def fn(r, c, o):
        qa_r, kv_r, krd_v, cmv, smv = r[0], r[1], r[2][...], r[3][...], r[4][...]
        g_n, g_r, g_kn = (x[...] for x in c)
        for h in range(HEADS):
            rp = qa_r[:, HEADS * HD + LANES * (h // 2):HEADS * HD + LANES * (h // 2 + 1)]
            nope, rpm, mine, rr = _mla_q_parts(qa_r, h, rp, g_n, g_r)
            o[0][h, :, 0:HD] = (nope * (rr * MLA_SCALE) * g_n).astype(BF16)
            o[0][h, :, HD:2 * HD] = _rope(rpm * (rr * MLA_SCALE) * g_r, cmv, smv, 64, 32).astype(BF16)
            kn = kv_r[:, 2 * HD * h:2 * HD * h + HD]
            o[1][h, :, 0:HD] = (kn * _rms_r(_rsum(kn * kn), HD) * g_kn).astype(BF16)
            o[1][h, :, HD:2 * HD] = krd_v
            o[2][h] = kv_r[:, 2 * HD * h + HD:2 * HD * (h + 1)].astype(BF16)
        return ()

    outs = [_sds((HEADS, S, 2 * HD), BF16), _sds((HEADS, S, 2 * HD), BF16), _sds((HEADS, S, HD), BF16)]
    return _rowwise(name, fn, [qa, kv, krd, cm, sm], gains, outs)


def _mla_prep_bwd(name, dq, dk, dv, qa, kv, tabs, gains):
    S = qa.shape[0]
    cm, sm = tabs

    def fn(r, c, o):
        dq_r, dk_r, dv_r, qa_r, kv_r, cmv, smv = r[0], r[1], r[2], r[3], r[4], r[5][...], r[6][...]
        g_n, g_r, g_kn = (x[...] for x in c)
        a_n = jnp.zeros((1, HD), F32)
        a_r = jnp.zeros((1, LANES), F32)
        a_kn = jnp.zeros((1, HD), F32)
        dkrd = jnp.zeros(cmv.shape, F32)
        drp = None
        for h in range(HEADS):
            rp = qa_r[:, HEADS * HD + LANES * (h // 2):HEADS * HD + LANES * (h // 2 + 1)]
            nope, rpm, mine, rr = _mla_q_parts(qa_r, h, rp, g_n, g_r)
            dn = dq_r[h, :, 0:HD] * MLA_SCALE
            dr = _rope_bwd(jnp.where(mine, dq_r[h, :, HD:2 * HD] * MLA_SCALE, 0.0), cmv, smv, 64, 32)
            dot = (_rsum(dn * g_n * nope) + _rsum(dr * g_r * rpm)) * (1.0 / MLA_QK)
            r3 = rr * rr * rr
            o[0][:, HD * h:HD * (h + 1)] = (rr * dn * g_n - nope * r3 * dot).astype(BF16)
            part = jnp.where(mine, rr * dr * g_r - rpm * r3 * dot, 0.0)
            drp = part if h % 2 == 0 else drp + part
            if h % 2 == 1:
                o[0][:, HEADS * HD + LANES * (h // 2):HEADS * HD + LANES * (h // 2 + 1)] = drp.astype(BF16)
            a_n = a_n + _csum(dn * nope * rr)
            a_r = a_r + _csum(dr * rpm * rr)
            kn = kv_r[:, 2 * HD * h:2 * HD * h + HD]
            rk = _rms_r(_rsum(kn * kn), HD)
            dkn = dk_r[h, :, 0:HD]
            o[1][:, 2 * HD * h:2 * HD * h + HD] = _rms_bwd(kn, rk, g_kn, dkn, 1.0 / HD).astype(BF16)
            o[1][:, 2 * HD * h + HD:2 * HD * (h + 1)] = dv_r[h].astype(BF16)
            a_kn = a_kn + _csum(dkn * kn * rk)
            dkrd = dkrd + dk_r[h, :, HD:2 * HD]
        o[2][...] = dkrd
        return (a_n, a_r, a_kn)

    outs = [_sds(qa.shape, BF16), _sds(kv.shape, BF16), _sds((S, LANES), F32)]
    accs = [_sds((1, HD), F32), _sds((1, LANES), F32), _sds((1, HD), F32)]
    return _rowwise(name, fn, [dq, dk, dv, qa, kv, cm, sm], gains, outs, accs, tm=128)


def _seg64_r(x):
    low = _lane(x.shape) < 64
    x2 = x * x
    s0 = _rsum(jnp.where(low, x2, 0.0))
    s1 = _rsum(jnp.where(low, 0.0, x2))
    return jnp.where(low, _rms_r(s0, SWA_D), _rms_r(s1, SWA_D)), low


def _swa_prep(name, qkv, tabs, gains):
    S = qkv.shape[0]
    nq, nk = SWA_HEADS * SWA_D, SWA_KV * SWA_D
    cs, ss = tabs

    def fn(r, c, o):
        x_r, csv, ssv = r[0], r[1][...], r[2][...]
        g_q, g_k = c[0][...], c[1][...]
        for g in range((nq + nk) // LANES):
            x = x_r[:, LANES * g:LANES * (g + 1)]
            rr, _ = _seg64_r(x)
            y = _rope(x * rr * (g_q if g < nq // LANES else g_k), csv, ssv, SWA_D, SWA_ROT // 2)
            if g < nq // LANES:
                o[0][:, LANES * g:LANES * (g + 1)] = (y * SWA_SCALE).astype(BF16)
            else:
                o[1][:, LANES * g - nq:LANES * (g + 1) - nq] = y.astype(BF16)
        o[2][...] = x_r[:, nq + nk:nq + 2 * nk].astype(BF16)
        return ()

    outs = [_sds((S, nq), BF16), _sds((S, nk), BF16), _sds((S, nk), BF16)]
    return _rowwise(name, fn, [qkv, cs, ss], gains, outs)


def _swa_prep_bwd(name, dq, dk, dv, qkv, tabs, gains):
    nq, nk = SWA_HEADS * SWA_D, SWA_KV * SWA_D
    cs, ss = tabs

    def fn(r, c, o):
        dq_r, dk_r, x_r, csv, ssv = r[0], r[1], r[3], r[4][...], r[5][...]
        g_q, g_k = c[0][...], c[1][...]
        acc = [jnp.zeros((1, LANES), F32), jnp.zeros((1, LANES), F32)]
        for g in range((nq + nk) // LANES):
            isq = g < nq // LANES
            x = x_r[:, LANES * g:LANES * (g + 1)]
            rr, low = _seg64_r(x)
            d = dq_r[:, LANES * g:LANES * (g + 1)] * SWA_SCALE if isq else dk_r[:, LANES * g - nq:LANES * (g + 1) - nq]
            d = _rope_bwd(d, csv, ssv, SWA_D, SWA_ROT // 2)
            dyg = d * (g_q if isq else g_k)
            t = dyg * x
            d0 = _rsum(jnp.where(low, t, 0.0))
            d1 = _rsum(jnp.where(low, 0.0, t))
            dot = jnp.where(low, d0, d1) * (1.0 / SWA_D)
            o[0][:, LANES * g:LANES * (g + 1)] = (rr * dyg - x * (rr * rr * rr) * dot).astype(BF16)
            acc[0 if isq else 1] = acc[0 if isq else 1] + _csum(d * x * rr)
        o[0][:, nq + nk:nq + 2 * nk] = r[2][...].astype(BF16)
        return tuple(acc)

    return _rowwise(name, fn, [dq, dk, dv, qkv, cs, ss], gains, [_sds(qkv.shape, BF16)],
                    [_sds((1, LANES), F32), _sds((1, LANES), F32)], tm=128)


def _delta(name, do, o_, width):
    S, C = do.shape
    nh = C // width

    def fn(r, c, o):
        for g in range(C // LANES):
            t = r[0][:, LANES * g:LANES * (g + 1)].astype(F32) * r[1][:, LANES * g:LANES * (g + 1)].astype(F32)
            if width == LANES:
                o[0][g] = _rsum(t)
            else:
                low = _lane(t.shape) < 64
                o[0][2 * g] = _rsum(jnp.where(low, t, 0.0))
                o[0][2 * g + 1] = _rsum(jnp.where(low, 0.0, t))
        return ()

    return _rowwise(name, fn, [do, o_], [], [_sds((nh, S, 1), F32)])[0]


_NT = (((1,), (1,)), ((), ()))
_NN = (((1,), (0,)), ((), ()))
_TN = (((0,), (0,)), ((), ()))


def _flash_fwd(name, q, k, v, tq=512, tk=2048):
    H, S, dk = q.shape
    G = H // k.shape[0]
    dv = v.shape[2]
    tq, tk = _tile(S, (tq, 256, 128)), _tile(S, (tk, 256, 128))
    nk = S // tk

    def body(q_ref, k_ref, v_ref, o_ref, lse_ref, m_s, l_s, acc_s):
        j = pl.program_id(2)

        @pl.when(j == 0)
        def _():
            m_s[...] = jnp.full_like(m_s, -jnp.inf)
            l_s[...] = jnp.zeros_like(l_s)
            acc_s[...] = jnp.zeros_like(acc_s)

        s = lax.dot_general(q_ref[...], k_ref[...], _NT, preferred_element_type=F32)
        m_new = jnp.maximum(m_s[...], jnp.max(s, axis=-1, keepdims=True))
        alpha = jnp.exp(m_s[...] - m_new)
        p = jnp.exp(s - m_new)
        l_s[...] = alpha * l_s[...] + _rsum(p)
        acc_s[...] = alpha * acc_s[...] + lax.dot_general(p.astype(BF16), v_ref[...], _NN,
                                                          preferred_element_type=F32)
        m_s[...] = m_new

        @pl.when(j == nk - 1)
        def _():
            o_ref[...] = (acc_s[...] / l_s[...]).astype(o_ref.dtype)
            lse_ref[...] = m_s[...] + jnp.log(l_s[...])

    return pl.pallas_call(
        body, name=name, grid=(H, S // tq, nk),
        in_specs=[pl.BlockSpec((None, tq, dk), lambda h, i, j: (h, i, 0)),
                  pl.BlockSpec((None, tk, dk), lambda h, i, j: (h // G, j, 0)),
                  pl.BlockSpec((None, tk, dv), lambda h, i, j: (h // G, j, 0))],
        out_specs=[pl.BlockSpec((tq, dv), lambda h, i, j: (i, h)),
                   pl.BlockSpec((None, tq, 1), lambda h, i, j: (h, i, 0))],
        out_shape=[_sds((S, H * dv), BF16), _sds((H, S, 1), F32)],
        scratch_shapes=[pltpu.VMEM((tq, 1), F32), pltpu.VMEM((tq, 1), F32), pltpu.VMEM((tq, dv), F32)],
        compiler_params=_params(("parallel", "parallel", "arbitrary")),
    )(q, k, v)


def _flash_bwd(name, q, k, v, do, head0, lse, delta, tq=512, tk=1024):
    H, S, dk = q.shape
    G = H // k.shape[0]
    dv = v.shape[2]
    tq, tk = _tile(S, (tq, 256, 128)), _tile(S, (tk, 256, 128))
    nq = S // tq

    def body(q_ref, k_ref, v_ref, do_ref, lse_ref, dl_ref, dq_ref, dk_ref, dv_ref, dk_s, dv_s):
        j, i = pl.program_id(1), pl.program_id(2)

        @pl.when(i == 0)
        def _():
            dk_s[...] = jnp.zeros_like(dk_s)
            dv_s[...] = jnp.zeros_like(dv_s)

        qv, kv_, dov = q_ref[...], k_ref[...], do_ref[...]
        s = lax.dot_general(qv, kv_, _NT, preferred_element_type=F32)
        p = jnp.exp(s - lse_ref[...])
        dp = lax.dot_general(dov, v_ref[...], _NT, preferred_element_type=F32)
        ds = (p * (dp - dl_ref[...])).astype(BF16)
        dv_s[...] += lax.dot_general(p.astype(BF16), dov, _TN, preferred_element_type=F32)
        dk_s[...] += lax.dot_general(ds, qv, _TN, preferred_element_type=F32)
        dqi = lax.dot_general(ds, kv_, _NN, preferred_element_type=F32)
        rows = pl.ds(pl.multiple_of(i * tq, tq), tq)

        @pl.when(j == 0)
        def _():
            dq_ref[rows, :] = dqi

        @pl.when(j > 0)
        def _():
            dq_ref[rows, :] += dqi

        @pl.when(i == nq - 1)
        def _():
            dk_ref[...] = dk_s[...]
            dv_ref[...] = dv_s[...]

    return pl.pallas_call(
        body, name=name, grid=(H, S // tk, nq),
        in_specs=[pl.BlockSpec((None, tq, dk), lambda h, j, i: (h, i, 0)),
                  pl.BlockSpec((None, tk, dk), lambda h, j, i: (h // G, j, 0)),
                  pl.BlockSpec((None, tk, dv), lambda h, j, i: (h // G, j, 0)),
                  pl.BlockSpec((tq, dv), lambda h, j, i: (i, head0 + h)),
                  pl.BlockSpec((None, tq, 1), lambda h, j, i: (head0 + h, i, 0)),
                  pl.BlockSpec((None, tq, 1), lambda h, j, i: (head0 + h, i, 0))],
        out_specs=[pl.BlockSpec((None, S, dk), lambda h, j, i: (h, 0, 0)),
                   pl.BlockSpec((None, tk, dk), lambda h, j, i: (h, j, 0)),
                   pl.BlockSpec((None, tk, dv), lambda h, j, i: (h, j, 0))],
        out_shape=[_sds((H, S, dk), F32), _sds((H, S, dk), F32), _sds((H, S, dv), F32)],
        scratch_shapes=[pltpu.VMEM((tk, dk), F32), pltpu.VMEM((tk, dv), F32)],
        compiler_params=_params(("parallel", "arbitrary", "arbitrary")),
    )(q, k, v, do, lse, delta)


def _swa_place(ref128, h):
    e, t = h % 2, (h // (SWA_HEADS // SWA_KV)) % 2
    x = ref128.astype(F32)
    if e != t:
        x = pltpu.roll(x, 64, 1)
    return jnp.where((_lane(x.shape) >= 64) == bool(t), x, 0.0).astype(BF16)


def _swa_unplace(y, h):
    e, t = h % 2, (h // (SWA_HEADS // SWA_KV)) % 2
    return pltpu.roll(y, 64, 1) if e != t else y


def _swa_specs(width, nb):
    prev = pl.BlockSpec((SWA_BLOCK, width), lambda i: (jnp.maximum(i - 1, 0), 0))
    cur = pl.BlockSpec((SWA_BLOCK, width), lambda i: (i, 0))
    nxt = pl.BlockSpec((SWA_BLOCK, width), lambda i: (jnp.minimum(i + 1, nb - 1), 0))
    return [prev, cur, nxt]


def _swa_specs3(nh, nb):
    prev = pl.BlockSpec((nh, SWA_BLOCK, 1), lambda i: (0, jnp.maximum(i - 1, 0), 0))
    cur = pl.BlockSpec((nh, SWA_BLOCK, 1), lambda i: (0, i, 0))
    nxt = pl.BlockSpec((nh, SWA_BLOCK, 1), lambda i: (0, jnp.minimum(i + 1, nb - 1), 0))
    return [prev, cur, nxt]


def _swa_valid_q(i, S):
    qpos = i * SWA_BLOCK + lax.broadcasted_iota(jnp.int32, (SWA_BLOCK, 3 * SWA_BLOCK), 0)
    kpos = (i - 1) * SWA_BLOCK + lax.broadcasted_iota(jnp.int32, (SWA_BLOCK, 3 * SWA_BLOCK), 1)
    return (jnp.abs(qpos - kpos) <= SWA_WINDOW) & (kpos >= 0) & (kpos < S)


def _swa_fwd(name, q, k, v, sink):
    S = q.shape[0]
    nb = S // SWA_BLOCK
    grp = SWA_HEADS // SWA_KV
    smem = pl.BlockSpec(memory_space=pltpu.SMEM)

    def body(sink_ref, q_ref, kp, kc, kn, vp, vc, vn, o_ref, lse_ref):
        i = pl.program_id(0)
        valid = _swa_valid_q(i, S)
        kcat = [jnp.concatenate([r[:, LANES * u:LANES * (u + 1)] for r in (kp, kc, kn)], axis=0) for u in range(2)]
        vcat = [jnp.concatenate([r[:, LANES * u:LANES * (u + 1)] for r in (vp, vc, vn)], axis=0) for u in range(2)]
        for g in range(SWA_HEADS // 2):
            pair = []
            for h in (2 * g, 2 * g + 1):
                u = (h // grp) // 2
                xq = _swa_place(q_ref[:, LANES * g:LANES * (g + 1)], h)
                s = lax.dot_general(xq, kcat[u], _NT, preferred_element_type=F32)
                s = jnp.where(valid, s, -jnp.inf)
                sk = sink_ref[0, h]
                m = jnp.maximum(jnp.max(s, axis=-1, keepdims=True), sk)
                p = jnp.exp(s - m)
                den = _rsum(p) + jnp.exp(sk - m)
                oh = lax.dot_general((p / den).astype(BF16), vcat[u], _NN, preferred_element_type=F32)
                lse_ref[h] = m + jnp.log(den)
                pair.append(_swa_unplace(oh, h))
            o_ref[:, LANES * g:LANES * (g + 1)] = jnp.where(_lane(pair[0].shape) < 64, pair[0], pair[1]).astype(BF16)

    return pl.pallas_call(
        body, name=name, grid=(nb,),
        in_specs=[smem, pl.BlockSpec((SWA_BLOCK, q.shape[1]), lambda i: (i, 0))]
        + _swa_specs(k.shape[1], nb) + _swa_specs(v.shape[1], nb),
        out_specs=[pl.BlockSpec((SWA_BLOCK, q.shape[1]), lambda i: (i, 0)),
                   pl.BlockSpec((SWA_HEADS, SWA_BLOCK, 1), lambda i: (0, i, 0))],
        out_shape=[_sds(q.shape, BF16), _sds((SWA_HEADS, S, 1), F32)],
        compiler_params=_params(("parallel",)),
    )(sink, q, k, k, k, v, v, v)


def _swa_bwd_q(name, q, k, v, do, lse, delta, sink):
    S = q.shape[0]
    nb = S // SWA_BLOCK
    grp = SWA_HEADS // SWA_KV
    smem = pl.BlockSpec(memory_space=pltpu.SMEM)
    row = pl.BlockSpec((SWA_BLOCK, q.shape[1]), lambda i: (i, 0))
    hrow = pl.BlockSpec((SWA_HEADS, SWA_BLOCK, 1), lambda i: (0, i, 0))

    def body(sink_ref, q_ref, do_ref, lse_ref, dl_ref, kp, kc, kn, vp, vc, vn, dq_ref, ds_ref):
        i = pl.program_id(0)

        @pl.when(i == 0)
        def _():
            ds_ref[...] = jnp.zeros_like(ds_ref)

        valid = _swa_valid_q(i, S)
        kcat = [jnp.concatenate([r[:, LANES * u:LANES * (u + 1)] for r in (kp, kc, kn)], axis=0) for u in range(2)]
        vcat = [jnp.concatenate([r[:, LANES * u:LANES * (u + 1)] for r in (vp, vc, vn)], axis=0) for u in range(2)]
        for g in range(SWA_HEADS // 2):
            pair = []
            for h in (2 * g, 2 * g + 1):
                u = (h // grp) // 2
                xq = _swa_place(q_ref[:, LANES * g:LANES * (g + 1)], h)
                xdo = _swa_place(do_ref[:, LANES * g:LANES * (g + 1)], h)
                lse_h, dl_h = lse_ref[h], dl_ref[h]
                s = lax.dot_general(xq, kcat[u], _NT, preferred_element_type=F32)
                p = jnp.exp(jnp.where(valid, s, -jnp.inf) - lse_h)
                dp = lax.dot_general(xdo, vcat[u], _NT, preferred_element_type=F32)
                dsv = (p * (dp - dl_h)).astype(BF16)
                dqh = lax.dot_general(dsv, kcat[u], _NN, preferred_element_type=F32)
                pair.append(_swa_unplace(dqh, h))
                dsink = -_csum(jnp.exp(sink_ref[0, h] - lse_h) * dl_h)
                ds_ref[h:h + 1, :] += jnp.broadcast_to(dsink, (1, LANES))
            dq_ref[:, LANES * g:LANES * (g + 1)] = jnp.where(_lane(pair[0].shape) < 64, pair[0], pair[1])

    return pl.pallas_call(
        body, name=name, grid=(nb,),
        in_specs=[smem, row, row, hrow, hrow] + _swa_specs(k.shape[1], nb) + _swa_specs(v.shape[1], nb),
        out_specs=[row, pl.BlockSpec((SWA_HEADS, LANES), lambda i: (0, 0))],
        out_shape=[_sds(q.shape, F32), _sds((SWA_HEADS, LANES), F32)],
        compiler_params=_params(("arbitrary",)),
    )(sink, q, do, lse, delta, k, k, k, v, v, v)


def _swa_bwd_kv(name, q, k, v, do, lse, delta):
    S = q.shape[0]
    nb = S // SWA_BLOCK
    grp = SWA_HEADS // SWA_KV
    krow = pl.BlockSpec((SWA_BLOCK, k.shape[1]), lambda i: (i, 0))

    def body(qp, qc, qn, dop, doc, don, lp, lc, ln, dp_, dc_, dn_, k_ref, v_ref, dk_ref, dv_ref):
        j = pl.program_id(0)
        qpos = (j - 1) * SWA_BLOCK + lax.broadcasted_iota(jnp.int32, (3 * SWA_BLOCK, SWA_BLOCK), 0)
        kpos = j * SWA_BLOCK + lax.broadcasted_iota(jnp.int32, (3 * SWA_BLOCK, SWA_BLOCK), 1)
        valid = (jnp.abs(qpos - kpos) <= SWA_WINDOW) & (qpos >= 0) & (qpos < S)
        for u in range(SWA_KV // 2):
            kc = k_ref[:, LANES * u:LANES * (u + 1)]
            vc = v_ref[:, LANES * u:LANES * (u + 1)]
            dk_a = jnp.zeros((SWA_BLOCK, LANES), F32)
            dv_a = jnp.zeros((SWA_BLOCK, LANES), F32)
            for h in range(2 * u * grp, (2 * u + 2) * grp):
                g = h // 2
                xq = jnp.concatenate([_swa_place(r[:, LANES * g:LANES * (g + 1)], h) for r in (qp, qc, qn)], axis=0)
                xdo = jnp.concatenate([_swa_place(r[:, LANES * g:LANES * (g + 1)], h) for r in (dop, doc, don)], axis=0)
                lse_h = jnp.concatenate([lp[h], lc[h], ln[h]], axis=0)
                dl_h = jnp.concatenate([dp_[h], dc_[h], dn_[h]], axis=0)
                s = lax.dot_general(xq, kc, _NT, preferred_element_type=F32)
                p = jnp.where(valid, jnp.exp(s - lse_h), 0.0)
                dpv = lax.dot_general(xdo, vc, _NT, preferred_element_type=F32)
                dsv = (p * (dpv - dl_h)).astype(BF16)
                dv_a = dv_a + lax.dot_general(p.astype(BF16), xdo, _TN, preferred_element_type=F32)
                dk_a = dk_a + lax.dot_general(dsv, xq, _TN, preferred_element_type=F32)
            dk_ref[:, LANES * u:LANES * (u + 1)] = dk_a
            dv_ref[:, LANES * u:LANES * (u + 1)] = dv_a

    return pl.pallas_call(
        body, name=name, grid=(nb,),
        in_specs=_swa_specs(q.shape[1], nb) + _swa_specs(do.shape[1], nb)
        + _swa_specs3(SWA_HEADS, nb) + _swa_specs3(SWA_HEADS, nb) + [krow, krow],
        out_specs=[krow, krow],
        out_shape=[_sds(k.shape, F32), _sds(v.shape, F32)],
        compiler_params=_params(("parallel",)),
    )(q, q, q, do, do, do, lse, lse, lse, delta, delta, delta, k, v)


def _exchange(name, tensors):
    n = len(tensors)
    outs = [_sds(a.shape if sc else (N_DEV,) + a.shape, a.dtype) for a, sc in tensors]

    def body(*refs):
        ins, dst, send, recv, loc = refs[:n], refs[n:2 * n], refs[2 * n], refs[2 * n + 1], refs[2 * n + 2]
        x, y, c = lax.axis_index("x"), lax.axis_index("y"), lax.axis_index("c")
        me = 4 * x + 2 * y + c

        def peer(kk):
            return (x ^ ((kk >> 2) & 1), y ^ ((kk >> 1) & 1), c ^ (kk & 1))

        def copy(t, kk):
            px, py, pc = peer(kk)
            src = ins[t].at[4 * px + 2 * py + pc] if tensors[t][1] else ins[t]
            return pltpu.make_async_remote_copy(src_ref=src, dst_ref=dst[t].at[me], send_sem=send.at[t, kk - 1],
                                                recv_sem=recv.at[t, kk - 1], device_id=(px, py, pc),
                                                device_id_type=pl.DeviceIdType.MESH)

        def arrival(t, kk):
            px, py, pc = peer(kk)
            src = ins[t].at[me] if tensors[t][1] else ins[t]
            return pltpu.make_async_remote_copy(src_ref=src, dst_ref=dst[t].at[4 * px + 2 * py + pc],
                                                send_sem=send.at[t, kk - 1], recv_sem=recv.at[t, kk - 1],
                                                device_id=(px, py, pc), device_id_type=pl.DeviceIdType.MESH)

        own = [pltpu.make_async_copy(ins[t].at[me] if tensors[t][1] else ins[t], dst[t].at[me], loc.at[t])
               for t in range(n)]
        sent = [copy(t, kk) for t in range(n) for kk in range(1, N_DEV)]
        for cp in own + sent:
            cp.start()
        for t in range(n):
            for kk in range(1, N_DEV):
                arrival(t, kk).wait_recv()
        for cp in sent:
            cp.wait_send()
        for cp in own:
            cp.wait()

    any_spec = pl.BlockSpec(memory_space=pl.ANY)
    return pl.pallas_call(
        body, name=name, in_specs=[any_spec] * n, out_specs=[any_spec] * n, out_shape=outs,
        scratch_shapes=[pltpu.SemaphoreType.DMA((n, N_DEV - 1)), pltpu.SemaphoreType.DMA((n, N_DEV - 1)),
                        pltpu.SemaphoreType.DMA((n,))],
        compiler_params=pltpu.CompilerParams(has_side_effects=True),
    )(*[a for a, _ in tensors])


def _peer_of(x, y, c, kk):
    return (x ^ ((kk >> 2) & 1), y ^ ((kk >> 1) & 1), c ^ (kk & 1))


def _own_slot(a, scatter):
    me = 4 * lax.axis_index("x") + 2 * lax.axis_index("y") + lax.axis_index("c")
    shape = a.shape if scatter else (N_DEV,) + a.shape
    own = lax.dynamic_slice_in_dim(a, me, 1, 0) if scatter else a[None]
    return lax.dynamic_update_slice_in_dim(lax.empty(shape, a.dtype), own, me, 0)


def _exchange_start(name, tensors):
    n = len(tensors)
    hbm = pl.BlockSpec(memory_space=pltpu.HBM)
    sem = pl.BlockSpec(memory_space=pltpu.SEMAPHORE)
    srcs = [pltpu.with_memory_space_constraint(a, pltpu.HBM) for a, _ in tensors]
    lands = [pltpu.with_memory_space_constraint(_own_slot(a, sc), pltpu.HBM) for a, sc in tensors]

    def body(*refs):
        ins, dst = refs[:n], refs[n:2 * n]
        send, recv, token = refs[2 * n], refs[2 * n + 1], refs[4 * n + 2]
        x, y, c = lax.axis_index("x"), lax.axis_index("y"), lax.axis_index("c")
        me = 4 * x + 2 * y + c
        for t in range(n):
            for kk in range(1, N_DEV):
                px, py, pc = _peer_of(x, y, c, kk)
                src = ins[t].at[4 * px + 2 * py + pc] if tensors[t][1] else ins[t]
                k1 = t * (N_DEV - 1) + kk - 1
                pltpu.make_async_remote_copy(src_ref=src, dst_ref=dst[t].at[me], send_sem=send.at[k1],
                                             recv_sem=recv.at[k1], device_id=(px, py, pc),
                                             device_id_type=pl.DeviceIdType.MESH).start()
        token[...] = jnp.zeros_like(token)

    out = pl.pallas_call(
        body, name=name,
        in_specs=[hbm] * (2 * n),
        out_specs=[sem, sem] + [hbm] * (2 * n) + [pl.BlockSpec(memory_space=pltpu.VMEM)],
        out_shape=[pltpu.SemaphoreType.DMA((n * (N_DEV - 1),)), pltpu.SemaphoreType.DMA((n * (N_DEV - 1),))]
        + [pltpu.HBM(a.shape, a.dtype) for a in srcs] + [pltpu.HBM(a.shape, a.dtype) for a in lands]
        + [_sds((8, LANES), F32)],
        input_output_aliases={i: i + 2 for i in range(2 * n)},
        compiler_params=pltpu.CompilerParams(has_side_effects=pltpu.SideEffectType.DATAFLOW_SIDE_EFFECTING),
    )(*srcs, *lands)
    return (out[0], out[1], out[2:2 + n], out[2 + n:2 + 2 * n], [sc for _, sc in tensors]), out[2 + 2 * n]


def _exchange_wait(name, started, after):
    send_s, recv_s, srcs, lands, flags = started
    n = len(srcs)
    hbm = pl.BlockSpec(memory_space=pltpu.HBM)
    sem = pl.BlockSpec(memory_space=pltpu.SEMAPHORE)

    def body(*refs):
        ins, dst, send, recv = refs[:n], refs[n:2 * n], refs[2 * n], refs[2 * n + 1]
        x, y, c = lax.axis_index("x"), lax.axis_index("y"), lax.axis_index("c")
        me = 4 * x + 2 * y + c
        for t in range(n):
            for kk in range(1, N_DEV):
                px, py, pc = _peer_of(x, y, c, kk)
                src = ins[t].at[me] if flags[t] else ins[t]
                k1 = t * (N_DEV - 1) + kk - 1
                cp = pltpu.make_async_remote_copy(src_ref=src, dst_ref=dst[t].at[4 * px + 2 * py + pc],
                                                  send_sem=send.at[k1], recv_sem=recv.at[k1],
                                                  device_id=(px, py, pc), device_id_type=pl.DeviceIdType.MESH)
                cp.wait_send()
                cp.wait_recv()

    out = pl.pallas_call(
        body, name=name,
        in_specs=[hbm] * (2 * n) + [sem, sem, pl.BlockSpec(memory_space=pl.ANY)],
        out_specs=[hbm] * (2 * n),
        out_shape=[pltpu.HBM(a.shape, a.dtype) for a in srcs] + [pltpu.HBM(a.shape, a.dtype) for a in lands],
        input_output_aliases={i: i for i in range(2 * n)},
        compiler_params=pltpu.CompilerParams(has_side_effects=pltpu.SideEffectType.DATAFLOW_SIDE_EFFECTING),
    )(*srcs, *lands, send_s, recv_s, after)
    return out[n:]


def _sum8(name, parts):
    _, R, C = parts.shape

    def fn(r, c, o):
        acc = r[0][0].astype(F32)
        for s in range(1, N_DEV):
            acc = acc + r[0][s].astype(F32)
        o[0][...] = acc
        return ()

    tm = R if R % 256 else 256
    return _rowwise(name, fn, [parts], [], [_sds((R, C), F32)], tm=tm)[0]


def _adamw(name, w, g, m, v):
    bc1 = 1.0 - ADAM_B1 ** ADAM_STEP
    bc2 = 1.0 - ADAM_B2 ** ADAM_STEP

    def fn(r, c, o):
        wv, gv, mv, vv = (x[...] for x in r)
        mn = ADAM_B1 * mv + (1.0 - ADAM_B1) * gv
        vn = ADAM_B2 * vv + (1.0 - ADAM_B2) * (gv * gv)
        o[0][...] = -ADAM_LR * ((mn / bc1) / (jnp.sqrt(vn / bc2) + ADAM_EPS) + ADAM_WD * wv)
        o[1][...] = mn
        o[2][...] = vn
        return ()

    tm = max(8, min(512, 1 << ((16 << 20) // (56 * w.shape[1])).bit_length() - 1))
    return _rowwise(name, fn, [w, g, m, v], [], [_sds(w.shape, F32)] * 3, tm=tm)


def _rope_cs(pos, dim, theta):
    inv = jnp.float32(theta) ** (-jnp.arange(0, dim, 2, dtype=jnp.float32) / dim)
    ang = pos.astype(jnp.float32)[:, None] * inv[None, :]
    return jnp.cos(ang), jnp.sin(ang)


def _tables(S):
    pos = jnp.arange(S)
    c, s = _rope_cs(pos, MLA_ROPE, ROPE_THETA)
    mla = (jnp.concatenate([c, c, c, c], 1), jnp.concatenate([-s, s, -s, s], 1))
    rc, rs = _rope_cs(pos // GRID_W, HD // 2, AXIAL_THETA)
    cc, cs = _rope_cs(pos % GRID_W, HD // 2, AXIAL_THETA)
    axial = (jnp.concatenate([rc, rc, cc, cc], 1), jnp.concatenate([-rs, rs, -cs, cs], 1))
    c, s = _rope_cs(pos, SWA_ROT, ROPE_THETA)
    one, zero = jnp.ones((S, SWA_D - SWA_ROT), F32), jnp.zeros((S, SWA_D - SWA_ROT), F32)
    swa = (jnp.concatenate([c, c, one, c, c, one], 1), jnp.concatenate([-s, s, zero, -s, s, zero], 1))
    return mla, axial, swa


_WEIGHTS = ['even_norm', 'even_w_in', 'mla_q_lat_norm', 'mla_kv_lat_norm', 'mla_w_uq', 'mla_w_ukv', 'mla_q_norm',
            'mla_k_nope_norm', 'mla_k_rope_norm', 'gqa_q_norm', 'gqa_k_norm', 'even_w_out', 'odd_norm', 'odd_w_qkv',
            'swa_q_norm', 'swa_k_norm', 'swa_sink', 'odd_w_out', 'mlp_norm', 'mlp_w_up', 'mlp_w_down']
_SMALL = ['even_norm', 'mla_q_lat_norm', 'mla_kv_lat_norm', 'mla_q_norm', 'mla_k_nope_norm', 'mla_k_rope_norm',
          'gqa_q_norm', 'gqa_k_norm', 'swa_q_norm', 'swa_k_norm', 'swa_sink', 'mlp_norm']


def _relu2(acc):
    rl = jnp.maximum(acc, 0.0)
    return rl * rl, rl


def _mul2(acc, rl):
    return (acc * (2.0 * rl.astype(F32)),)


def _add(acc, res):
    return (acc + res,)


def _step(x, tgt, w, m, v):
    S, D = x.shape
    nw = len(_WEIGHTS)
    tab_mla, tab_ax, tab_swa = _tables(S)
    bf = lambda a: a.astype(BF16)

    w_up_s, w_dn_s = w['mlp_w_up'], w['mlp_w_down']
    later, tok = _exchange_start("gather_rest_start", [
        (bf(w['even_w_out'][0]), False), (bf(w['odd_w_qkv'][0].T), False), (bf(w['odd_w_out'][0]), False),
        (bf(w_up_s[0].T), False), (bf(w_up_s[1].T), False), (bf(w_dn_s[0]), False), (bf(w_dn_s[1]), False),
        (w['odd_norm'], False)])
    gathered = _exchange("gather_first", [
        (bf(w['even_w_in'][0].T), False), (bf(w['mla_w_uq'][0].T) + tok[0, 0].astype(BF16), False),
        (bf(w['mla_w_ukv'][0].T), False)])
    flat = lambda a: a.reshape((a.shape[0] * a.shape[1],) + a.shape[2:])
    win_t, wuq_g, wukv_t = [flat(a) for a in gathered]
    win_t = jnp.concatenate([win_t[:KR_END], jnp.zeros((LANES - MLA_ROPE, D), BF16), win_t[KR_END:]], 0)
    wuq_g = gathered[1]
    wuq_t = jnp.concatenate([wuq_g[:, :HD].reshape(HEADS * HD, Q_LORA),
                             wuq_g[:, HD:].reshape(HEADS * MLA_ROPE, Q_LORA)], 0)

    z64 = jnp.zeros((1, 64), F32)
    qn = w['mla_q_norm']
    g_even = [w['mla_q_lat_norm'], w['mla_kv_lat_norm'], jnp.concatenate([w['mla_k_rope_norm'], z64], 1),
              w['gqa_q_norm'], w['gqa_k_norm']]
    g_mla = [qn[:, :HD], jnp.concatenate([qn[:, HD:], qn[:, HD:]], 1), w['mla_k_nope_norm']]
    g_swa = [jnp.concatenate([w['swa_q_norm']] * 2, 1), jnp.concatenate([w['swa_k_norm']] * 2, 1)]

    def mlp_fwd(l, xin):
        hn = _rmsnorm(f"mlp{l}_norm", xin, w['mlp_norm'][l:l + 1])
        a, rl = _mm(f"mlp{l}_up", hn, wup_t[l], "nt", (BF16, BF16), epilogue=_relu2)
        xout = _mm(f"mlp{l}_down", a, wdn[l], "nn", (F32,), epilogue=_add, extras=(xin,))
        return xout, (hn, a, rl)

    h0 = _rmsnorm("even_norm", x, w['even_norm'])
    proj = _mm("even_in", h0, win_t, "nt", (F32,), tm=512, tn=P_END)
    cqn, ckvn, krd, qg, kg, vg = _even_prep("even_prep", proj, tab_mla + tab_ax, g_even)
    qa = _mm("mla_uq", cqn, wuq_t, "nt", (F32,))
    kv = _mm("mla_ukv", ckvn, wukv_t, "nt", (F32,))
    q_a, k_a, v_a = _mla_prep("mla_prep", qa, kv, krd, tab_mla, g_mla)
    o_a, lse_a = _flash_fwd("mla_attn", q_a, k_a, v_a)
    o_g, lse_g = _flash_fwd("gqa_attn", qg, kg, vg)
    merged = jnp.concatenate([o_a, o_g], 1)
    rest = _exchange_wait("gather_rest_wait", later, merged)
    wout_e, wqkv_t, wout_o, wup0_t, wup1_t, wdn0, wdn1 = [flat(a) for a in rest[:7]]
    odd_norm = rest[7].reshape(1, D)
    wup_t, wdn = (wup0_t, wup1_t), (wdn0, wdn1)
    x1 = _mm("even_out", merged, wout_e, "nn", (F32,), epilogue=_add, extras=(x,))
    x2, mlp0 = mlp_fwd(0, x1)

    h1 = _rmsnorm("odd_norm", x2, odd_norm)
    qkv = _mm("odd_qkv", h1, wqkv_t, "nt", (F32,))
    q_s, k_s, v_s = _swa_prep("swa_prep", qkv, tab_swa, g_swa)
    o_s, lse_s = _swa_fwd("swa_attn", q_s, k_s, v_s, w['swa_sink'])
    x3 = _mm("odd_out", o_s, wout_o, "nn", (F32,), epilogue=_add, extras=(x2,))
    x4, mlp1 = mlp_fwd(1, x3)

    dy, loss_acc = _loss("loss", x4, tgt)
    loss = lax.psum(0.5 / D * loss_acc[0, 0], ("x", "y", "c"))

    gsm = {}

    def mlp_bwd(l, dout, xin, saved):
        hn, a, rl = saved
        du = _mm(f"mlp{l}_dact", dout, wdn[l], "nt", (BF16,), epilogue=_mul2, extras=(rl,))
        g_dn = _mm(f"mlp{l}_gdown", a, dout, "tn", (BF16,))
        g_up = _mm(f"mlp{l}_gup", du, hn, "tn", (BF16,))
        dhn = _mm(f"mlp{l}_dnorm", du, wup_t[l], "nn", (F32,))
        din, g_n = _rmsnorm_bwd(f"mlp{l}_norm_bwd", dout, xin, dhn, w['mlp_norm'][l:l + 1])
        return din, g_dn, g_up, g_n

    dx3, g_dn1, g_up1, g_mn1 = mlp_bwd(1, dy, x3, mlp1)
    split = lambda a: a.reshape((N_DEV, a.shape[0] // N_DEV) + a.shape[1:])
    sc1, tok1 = _exchange_start("scatter_mlp1_start", [(split(g_up1), True), (split(g_dn1), True)])
    sink = w['swa_sink'] + tok1[0:1, 0:SWA_HEADS]

    g_wout_o = _mm("odd_gout", o_s, dx3, "tn", (BF16,))
    do_s = _mm("odd_dattn", dx3, wout_o, "nt", (BF16,))
    dl_s = _delta("swa_delta", do_s, o_s, SWA_D)
    dq_s, dsink = _swa_bwd_q("swa_bwd_q", q_s, k_s, v_s, do_s, lse_s, dl_s, sink)
    dk_s, dv_s = _swa_bwd_kv("swa_bwd_kv", q_s, k_s, v_s, do_s, lse_s, dl_s)
    dqkv, g_sq, g_sk = _swa_prep_bwd("swa_prep_bwd", dq_s, dk_s, dv_s, qkv, tab_swa, g_swa)
    g_wqkv = _mm("odd_gqkv", dqkv, h1, "tn", (BF16,))
    dh1 = _mm("odd_dnorm", dqkv, wqkv_t, "nn", (F32,))
    dx2, g_on = _rmsnorm_bwd("odd_norm_bwd", dx3, x2, dh1, odd_norm)
    gsm['swa_q_norm'] = g_sq[:, :64] + g_sq[:, 64:]
    gsm['swa_k_norm'] = g_sk[:, :64] + g_sk[:, 64:]
    gsm['swa_sink'] = dsink[:, 0].reshape(1, SWA_HEADS)

    dx1, g_dn0, g_up0, g_mn0 = mlp_bwd(0, dx2, x1, mlp0)
    sc2, tok2 = _exchange_start("scatter_mid_start", [(split(g_wout_o), True), (split(g_wqkv), True),
                                                      (split(g_up0), True), (split(g_dn0), True)])
    gsm['mlp_norm'] = jnp.concatenate([g_mn0, g_mn1], 0)

    g_wout_e = _mm("even_gout", merged, dx1, "tn", (BF16,))
    dmerged = _mm("even_dattn", dx1, wout_e, "nt", (BF16,))
    dl_e = _delta("even_delta", dmerged, merged, HD)
    lse_e = jnp.concatenate([lse_a, lse_g], 0) + tok2[0, 0]
    dq_a, dk_a, dv_a = _flash_bwd("mla_attn_bwd", q_a, k_a, v_a, dmerged, 0, lse_e, dl_e)
    dq_g, dk_g, dv_g = _flash_bwd("gqa_attn_bwd", qg, kg, vg, dmerged, HEADS, lse_e, dl_e)
    dqa, dkv, dkrd, g_qnn, g_qnr, g_kn = _mla_prep_bwd("mla_prep_bwd", dq_a, dk_a, dv_a, qa, kv, tab_mla, g_mla)
    g_wuq = _mm("mla_guq", dqa, cqn, "tn", (BF16,))
    dcqn = _mm("mla_dq_lat", dqa, wuq_t, "nn", (F32,))
    g_wukv = _mm("mla_gukv", dkv, ckvn, "tn", (BF16,))
    dckvn = _mm("mla_dkv_lat", dkv, wukv_t, "nn", (F32,))
    dproj, g_ql, g_kvl, g_kr, g_gq, g_gk = _even_prep_bwd("even_prep_bwd", dcqn, dckvn, dkrd, dq_g, dk_g, dv_g,
                                                          proj, tab_mla + tab_ax, g_even)
    g_win = _mm("even_gin", dproj, h0, "tn", (BF16,), tm=P_END)
    dh0 = _mm("even_dnorm", dproj, win_t, "nn", (F32,), tk=P_END)
    grad_x, g_en = _rmsnorm_bwd("even_norm_bwd", dx1, x, dh0, w['even_norm'])
    gsm.update(even_norm=g_en, mla_q_lat_norm=g_ql, mla_kv_lat_norm=g_kvl,
               mla_q_norm=jnp.concatenate([g_qnn, g_qnr[:, :64] + g_qnr[:, 64:]], 1), mla_k_nope_norm=g_kn,
               mla_k_rope_norm=g_kr[:, :64], gqa_q_norm=g_gq, gqa_k_norm=g_gk)

    g_win = jnp.concatenate([g_win[:KR_END], g_win[P_QG:]], 0)
    g_wuq = jnp.concatenate([g_wuq[:HEADS * HD].reshape(HEADS, HD, Q_LORA),
                             g_wuq[HEADS * HD:].reshape(HEADS, MLA_ROPE, Q_LORA)], 1)
    small_sizes = [w[n].size for n in _SMALL] + [D]
    small_vec = jnp.concatenate([gsm[n].reshape(1, -1) for n in _SMALL] + [g_on], 1)
    pad = (-small_vec.shape[1]) % LANES
    small_vec = jnp.pad(small_vec, ((0, 0), (0, pad)))
    last = _exchange("scatter_last", [(split(g_win), True), (g_wuq, True), (split(g_wukv), True),
                                      (split(g_wout_e), True), (small_vec, False)])
    p_up1, p_dn1 = _exchange_wait("scatter_mlp1_wait", sc1, last[0])
    p_wout_o, p_wqkv, p_up0, p_dn0 = _exchange_wait("scatter_mid_wait", sc2, last[0])
    parts = dict(even_w_in=last[0], mla_w_uq=last[1], mla_w_ukv=last[2], even_w_out=last[3], odd_w_qkv=p_wqkv,
                 odd_w_out=p_wout_o, up0=p_up0, up1=p_up1, dn0=p_dn0, dn1=p_dn1)
    red = {n: _sum8("sum_" + n, p) for n, p in parts.items()}
    small_g = _sum8("sum_small", last[4])

    grads = {
        'even_w_in': red['even_w_in'].T[None], 'mla_w_uq': red['mla_w_uq'].T[None],
        'mla_w_ukv': red['mla_w_ukv'].T[None], 'even_w_out': red['even_w_out'][None],
        'odd_w_qkv': red['odd_w_qkv'].T[None], 'odd_w_out': red['odd_w_out'][None],
        'mlp_w_up': jnp.stack([red['up0'].T, red['up1'].T]), 'mlp_w_down': jnp.stack([red['dn0'], red['dn1']]),
    }
    off = 0
    for n, sz in zip(_SMALL + ['odd_norm_full'], small_sizes):
        seg = small_g[:, off:off + sz]
        off += sz
        if n == 'odd_norm_full':
            me = 4 * lax.axis_index("x") + 2 * lax.axis_index("y") + lax.axis_index("c")
            grads['odd_norm'] = lax.dynamic_slice(seg, (0, me * (D // N_DEV)), (1, D // N_DEV))
        else:
            grads[n] = seg.reshape(w[n].shape)

    delta, new_m, new_v = {}, {}, {}
    big = ['even_w_in', 'mla_w_uq', 'mla_w_ukv', 'even_w_out', 'odd_w_qkv', 'odd_w_out', 'mlp_w_up', 'mlp_w_down']
    for n in big:
        shp = w[n].shape
        two = lambda a: a.reshape(shp[0] * shp[1], shp[2])
        d_, m_, v_ = _adamw("adamw_" + n, two(w[n]), two(grads[n]), two(m[n]), two(v[n]))
        delta[n], new_m[n], new_v[n] = d_.reshape(shp), m_.reshape(shp), v_.reshape(shp)
    sm_names = _SMALL + ['odd_norm']
    pack = lambda d: jnp.concatenate([d[n].reshape(1, -1) for n in sm_names], 1)
    pw, pg, pm, pv = pack(w), pack(grads), pack(m), pack(v)
    padw = (-pw.shape[1]) % LANES
    padf = lambda a: jnp.pad(a, ((0, 0), (0, padw)))
    d_, m_, v_ = _adamw("adamw_small", padf(pw), padf(pg), padf(pm), jnp.pad(pv, ((0, 0), (0, padw)), constant_values=1.0))
    off = 0
    for n in sm_names:
        sz = w[n].size
        delta[n] = d_[:, off:off + sz].reshape(w[n].shape)
        new_m[n] = m_[:, off:off + sz].reshape(w[n].shape)
        new_v[n] = v_[:, off:off + sz].reshape(w[n].shape)
        off += sz

    return (loss, grad_x[None], *[grads[n] for n in _WEIGHTS], *[delta[n] for n in _WEIGHTS],
            *[new_m[n] for n in _WEIGHTS], *[new_v[n] for n in _WEIGHTS])


def kernel(x, even_norm, even_w_in, mla_q_lat_norm, mla_kv_lat_norm, mla_w_uq, mla_w_ukv, mla_q_norm, mla_k_nope_norm, mla_k_rope_norm, gqa_q_norm, gqa_k_norm, even_w_out, odd_norm, odd_w_qkv, swa_q_norm, swa_k_norm, swa_sink, odd_w_out, mlp_norm, mlp_w_up, mlp_w_down, loss_target, m_even_norm, m_even_w_in, m_mla_q_lat_norm, m_mla_kv_lat_norm, m_mla_w_uq, m_mla_w_ukv, m_mla_q_norm, m_mla_k_nope_norm, m_mla_k_rope_norm, m_gqa_q_norm, m_gqa_k_norm, m_even_w_out, m_odd_norm, m_odd_w_qkv, m_swa_q_norm, m_swa_k_norm, m_swa_sink, m_odd_w_out, m_mlp_norm, m_mlp_w_up, m_mlp_w_down, v_even_norm, v_even_w_in, v_mla_q_lat_norm, v_mla_kv_lat_norm, v_mla_w_uq, v_mla_w_ukv, v_mla_q_norm, v_mla_k_nope_norm, v_mla_k_rope_norm, v_gqa_q_norm, v_gqa_k_norm, v_even_w_out, v_odd_norm, v_odd_w_qkv, v_swa_q_norm, v_swa_k_norm, v_swa_sink, v_odd_w_out, v_mlp_norm, v_mlp_w_up, v_mlp_w_down):
    ws = (even_norm, even_w_in, mla_q_lat_norm, mla_kv_lat_norm, mla_w_uq, mla_w_ukv, mla_q_norm, mla_k_nope_norm, mla_k_rope_norm, gqa_q_norm, gqa_k_norm, even_w_out, odd_norm, odd_w_qkv, swa_q_norm, swa_k_norm, swa_sink, odd_w_out, mlp_norm, mlp_w_up, mlp_w_down)
    ms = (m_even_norm, m_even_w_in, m_mla_q_lat_norm, m_mla_kv_lat_norm, m_mla_w_uq, m_mla_w_ukv, m_mla_q_norm, m_mla_k_nope_norm, m_mla_k_rope_norm, m_gqa_q_norm, m_gqa_k_norm, m_even_w_out, m_odd_norm, m_odd_w_qkv, m_swa_q_norm, m_swa_k_norm, m_swa_sink, m_odd_w_out, m_mlp_norm, m_mlp_w_up, m_mlp_w_down)
    vs = (v_even_norm, v_even_w_in, v_mla_q_lat_norm, v_mla_kv_lat_norm, v_mla_w_uq, v_mla_w_ukv, v_mla_q_norm, v_mla_k_nope_norm, v_mla_k_rope_norm, v_gqa_q_norm, v_gqa_k_norm, v_even_w_out, v_odd_norm, v_odd_w_qkv, v_swa_q_norm, v_swa_k_norm, v_swa_sink, v_odd_w_out, v_mlp_norm, v_mlp_w_up, v_mlp_w_down)
    return _step(x[0], loss_target[0], dict(zip(_WEIGHTS, ws)), dict(zip(_WEIGHTS, ms)), dict(zip(_WEIGHTS, vs)))
```

```python
import functools

import jax
import jax.numpy as jnp
from jax import lax
from jax.experimental import pallas as pl
from jax.experimental.pallas import tpu as pltpu

F32 = jnp.float32
BF16 = jnp.bfloat16

N_DEV = 8
NORM_EPS = 1e-6
ROPE_THETA = 500000.0
AXIAL_THETA = 10000.0
GRID_W = 64
HEADS = 8
GQA_KV = 2
HD = 128
MLA_ROPE = 64
MLA_QK = HD + MLA_ROPE
Q_LORA = 512
KV_LORA = 256
SWA_HEADS = 32
SWA_KV = 4
SWA_D = 64
SWA_ROT = 16
SWA_WINDOW = 128
SWA_BLOCK = 128
MLA_SCALE, GQA_SCALE, SWA_SCALE = MLA_QK ** -0.5, HD ** -0.5, SWA_D ** -0.5
LANES = 128
ADAM_LR, ADAM_B1, ADAM_B2, ADAM_EPS, ADAM_WD, ADAM_STEP = 0.001, 0.9, 0.999, 1e-08, 0.01, 10
VMEM_LIMIT = 56 * 1024 * 1024

P_CQ, P_CKV, P_KR, P_QG = 0, Q_LORA, Q_LORA + KV_LORA, Q_LORA + KV_LORA + LANES
P_KG = P_QG + HEADS * HD
P_VG = P_KG + GQA_KV * HD
P_END = P_VG + GQA_KV * HD
KR_END = Q_LORA + KV_LORA + MLA_ROPE


def _tile(n, prefs):
    for t in prefs:
        if n % t == 0 and t <= n:
            return t
    return n


def _params(sem):
    return pltpu.CompilerParams(dimension_semantics=sem, vmem_limit_bytes=VMEM_LIMIT)


_DIMS = {"nn": ((1,), (0,)), "nt": ((1,), (1,)), "tn": ((0,), (0,))}


def _mm(name, a, b, mode, out_dtypes, epilogue=None, extras=(), tm=1024, tn=1024, tk=512):
    if mode == "nn":
        (M, K), (_, N) = a.shape, b.shape
    elif mode == "nt":
        (M, K), (N, _) = a.shape, b.shape
    else:
        (K, M), (_, N) = a.shape, b.shape
    tm = _tile(M, (tm, 512, 256, 128))
    tn = _tile(N, (tn, 512, 256, 128))
    tk = _tile(K, (tk, 256, 128))
    nk = K // tk
    ne, no = len(extras), len(out_dtypes)
    if mode == "tn":
        a_spec = pl.BlockSpec((tk, tm), lambda i, j, k: (k, i))
    else:
        a_spec = pl.BlockSpec((tm, tk), lambda i, j, k: (i, k))
    if mode == "nt":
        b_spec = pl.BlockSpec((tn, tk), lambda i, j, k: (j, k))
    else:
        b_spec = pl.BlockSpec((tk, tn), lambda i, j, k: (k, j))
    o_spec = pl.BlockSpec((tm, tn), lambda i, j, k: (i, j))
    dims = (_DIMS[mode], ((), ()))

    def body(a_ref, b_ref, *rest):
        ex, outs, acc = rest[:ne], rest[ne:ne + no], rest[ne + no]
        k = pl.program_id(2)

        @pl.when(k == 0)
        def _():
            acc[...] = jnp.zeros_like(acc)

        acc[...] += lax.dot_general(a_ref[...].astype(BF16), b_ref[...].astype(BF16), dims,
                                    preferred_element_type=F32)

        @pl.when(k == nk - 1)
        def _():
            res = epilogue(acc[...], *[e[...] for e in ex]) if epilogue else (acc[...],)
            for o, r in zip(outs, res):
                o[...] = r.astype(o.dtype)

    out = pl.pallas_call(
        body, name=name, grid=(M // tm, N // tn, nk),
        in_specs=[a_spec, b_spec] + [o_spec] * ne,
        out_specs=[o_spec] * no,
        out_shape=[jax.ShapeDtypeStruct((M, N), d) for d in out_dtypes],
        scratch_shapes=[pltpu.VMEM((tm, tn), F32)],
        compiler_params=_params(("parallel", "parallel", "arbitrary")),
    )(a, b, *extras)
    return out[0] if no == 1 else out


def _rowwise(name, fn, rows, consts, outs, accs=(), tm=256):
    S = rows[0].shape[-2]
    tm = _tile(S, (tm, 128, 64, 32, 16, 8))
    nr, nc, no, na = len(rows), len(consts), len(outs), len(accs)

    def rspec(shape):
        if len(shape) == 2:
            return pl.BlockSpec((tm, shape[1]), lambda i: (i, 0))
        return pl.BlockSpec((shape[0], tm, shape[2]), lambda i: (0, i, 0))

    def cspec(shape):
        return pl.BlockSpec(tuple(shape), lambda i: (0,) * len(shape))

    def body(*refs):
        r, c = refs[:nr], refs[nr:nr + nc]
        o, a = refs[nr + nc:nr + nc + no], refs[nr + nc + no:]
        vals = fn(r, c, o)
        if na:
            @pl.when(pl.program_id(0) == 0)
            def _():
                for ar in a:
                    ar[...] = jnp.zeros_like(ar)

            for ar, v in zip(a, vals):
                ar[...] += v

    res = pl.pallas_call(
        body, name=name, grid=(S // tm,),
        in_specs=[rspec(x.shape) for x in rows] + [cspec(x.shape) for x in consts],
        out_specs=[rspec(x.shape) for x in outs] + [cspec(x.shape) for x in accs],
        out_shape=list(outs) + list(accs),
        compiler_params=_params(("arbitrary",) if na else ("parallel",)),
    )(*rows, *consts)
    return res


def _sds(shape, dtype):
    return jax.ShapeDtypeStruct(tuple(shape), dtype)


def _rsum(x):
    return jnp.sum(x, axis=-1, keepdims=True)


def _csum(x):
    return jnp.sum(x, axis=0, keepdims=True)


def _lane(shape):
    return lax.broadcasted_iota(jnp.int32, shape, 1)


def _partner(x, seg, half):
    lane = _lane(x.shape) % seg
    return jnp.where(lane < half, pltpu.roll(x, LANES - half, 1), pltpu.roll(x, half, 1))


def _rope(x, c, s, seg, half):
    return x * c + _partner(x, seg, half) * s


def _rope_bwd(dy, c, s, seg, half):
    t = _partner(dy * s, seg, half)
    if seg != 2 * half:
        t = jnp.where(_lane(dy.shape) % seg < 2 * half, t, 0.0)
    return dy * c + t


def _rms_r(ss, n):
    return lax.rsqrt(ss * (1.0 / n) + NORM_EPS)


def _rms_bwd(x, r, g, dy, dot_scale):
    dyg = dy * g
    return r * dyg - x * (r * r * r) * (_rsum(dyg * x) * dot_scale)


def _rmsnorm(name, x, g):
    D = x.shape[1]

    def fn(r, c, o):
        xv = r[0][...]
        o[0][...] = (xv * _rms_r(_rsum(xv * xv), D) * c[0][...]).astype(BF16)
        return ()

    return _rowwise(name, fn, [x], [g], [_sds(x.shape, BF16)])[0]


def _rmsnorm_bwd(name, dres, x, dh, g):
    D = x.shape[1]

    def fn(r, c, o):
        xv, dhv, gv = r[1][...], r[2][...], c[0][...]
        rr = _rms_r(_rsum(xv * xv), D)
        o[0][...] = r[0][...] + _rms_bwd(xv, rr, gv, dhv, 1.0 / D)
        return (_csum(dhv * xv * rr),)

    return _rowwise(name, fn, [dres, x, dh], [g], [_sds(x.shape, F32)], [_sds((1, D), F32)])


def _loss(name, y, t):
    D = y.shape[1]

    def fn(r, c, o):
        e = r[0][...] - r[1][...]
        o[0][...] = e * (1.0 / D)
        return (_csum(_rsum(e * e)),)

    return _rowwise(name, fn, [y, t], [], [_sds(y.shape, F32)], [_sds((1, 1), F32)])


def _even_prep(name, proj, tabs, gains):
    S = proj.shape[0]
    cm, sm, ca, sa = tabs

    def fn(r, c, o):
        p, cmv, smv, cav, sav = r[0], r[1][...], r[2][...], r[3][...], r[4][...]
        g_ql, g_kvl, g_kr, g_q, g_k = (x[...] for x in c)
        cq = p[:, P_CQ:P_CKV]
        o[0][...] = (cq * _rms_r(_rsum(cq * cq), Q_LORA) * g_ql).astype(BF16)
        ckv = p[:, P_CKV:P_KR]
        o[1][...] = (ckv * _rms_r(_rsum(ckv * ckv), KV_LORA) * g_kvl).astype(BF16)
        kr = p[:, P_KR:P_QG]
        y = _rope(kr * _rms_r(_rsum(kr * kr), MLA_ROPE) * g_kr, cmv, smv, 64, 32)
        o[2][...] = (y + pltpu.roll(y, 64, 1)).astype(BF16)
        for h in range(HEADS):
            xh = p[:, P_QG + HD * h:P_QG + HD * (h + 1)]
            o[3][h] = (_rope(xh * _rms_r(_rsum(xh * xh), HD) * g_q, cav, sav, 64, 32) * GQA_SCALE).astype(BF16)
        for h in range(GQA_KV):
            xh = p[:, P_KG + HD * h:P_KG + HD * (h + 1)]
            o[4][h] = _rope(xh * _rms_r(_rsum(xh * xh), HD) * g_k, cav, sav, 64, 32).astype(BF16)
            o[5][h] = p[:, P_VG + HD * h:P_VG + HD * (h + 1)].astype(BF16)
        return ()

    outs = [_sds((S, Q_LORA), BF16), _sds((S, KV_LORA), BF16), _sds((S, LANES), BF16),
            _sds((HEADS, S, HD), BF16), _sds((GQA_KV, S, HD), BF16), _sds((GQA_KV, S, HD), BF16)]
    return _rowwise(name, fn, [proj, cm, sm, ca, sa], gains, outs)


def _even_prep_bwd(name, dcqn, dckvn, dkrd, dqg, dkg, dvg, proj, tabs, gains):
    S = proj.shape[0]
    cm, sm, ca, sa = tabs

    def fn(r, c, o):
        p, cmv, smv, cav, sav = r[6], r[7][...], r[8][...], r[9][...], r[10][...]
        g_ql, g_kvl, g_kr, g_q, g_k = (x[...] for x in c)
        out = o[0]
        cq = p[:, P_CQ:P_CKV]
        rr = _rms_r(_rsum(cq * cq), Q_LORA)
        d = r[0][...]
        out[:, P_CQ:P_CKV] = _rms_bwd(cq, rr, g_ql, d, 1.0 / Q_LORA).astype(BF16)
        a_ql = _csum(d * cq * rr)
        ckv = p[:, P_CKV:P_KR]
        rr = _rms_r(_rsum(ckv * ckv), KV_LORA)
        d = r[1][...]
        out[:, P_CKV:P_KR] = _rms_bwd(ckv, rr, g_kvl, d, 1.0 / KV_LORA).astype(BF16)
        a_kvl = _csum(d * ckv * rr)
        kr = p[:, P_KR:P_QG]
        rr = _rms_r(_rsum(kr * kr), MLA_ROPE)
        d = r[2][...]
        d = d + pltpu.roll(d, 64, 1)
        d = _rope_bwd(d, cmv, smv, 64, 32)
        low = _lane(d.shape) < 64
        out[:, P_KR:P_QG] = jnp.where(low, _rms_bwd(kr, rr, g_kr, d, 1.0 / MLA_ROPE), 0.0).astype(BF16)
        a_kr = _csum(d * kr * rr)
        a_q = jnp.zeros((1, HD), F32)
        for h in range(HEADS):
            xh = p[:, P_QG + HD * h:P_QG + HD * (h + 1)]
            rr = _rms_r(_rsum(xh * xh), HD)
            d = _rope_bwd(r[3][h] * GQA_SCALE, cav, sav, 64, 32)
            out[:, P_QG + HD * h:P_QG + HD * (h + 1)] = _rms_bwd(xh, rr, g_q, d, 1.0 / HD).astype(BF16)
            a_q = a_q + _csum(d * xh * rr)
        a_k = jnp.zeros((1, HD), F32)
        grp = HEADS // GQA_KV
        for h in range(GQA_KV):
            xh = p[:, P_KG + HD * h:P_KG + HD * (h + 1)]
            rr = _rms_r(_rsum(xh * xh), HD)
            d = r[4][grp * h]
            dv = r[5][grp * h]
            for e in range(1, grp):
                d = d + r[4][grp * h + e]
                dv = dv + r[5][grp * h + e]
            d = _rope_bwd(d, cav, sav, 64, 32)
            out[:, P_KG + HD * h:P_KG + HD * (h + 1)] = _rms_bwd(xh, rr, g_k, d, 1.0 / HD).astype(BF16)
            out[:, P_VG + HD * h:P_VG + HD * (h + 1)] = dv.astype(BF16)
            a_k = a_k + _csum(d * xh * rr)
        return (a_ql, a_kvl, a_kr, a_q, a_k)

    accs = [_sds((1, Q_LORA), F32), _sds((1, KV_LORA), F32), _sds((1, LANES), F32),
            _sds((1, HD), F32), _sds((1, HD), F32)]
    return _rowwise(name, fn, [dcqn, dckvn, dkrd, dqg, dkg, dvg, proj, cm, sm, ca, sa], gains,
                    [_sds((S, P_END), BF16)], accs, tm=128)


def _mla_q_parts(qa, h, rp, g_n, g_r):
    nope = qa[:, HD * h:HD * (h + 1)]
    mine = (_lane(rp.shape) >= 64) == bool(h % 2)
    rpm = jnp.where(mine, rp, 0.0)
    rr = _rms_r(_rsum(nope * nope) + _rsum(rpm * rpm), MLA_QK)
    return nope, rpm, mine, rr


def _mla_prep(name, qa, kv, krd, tabs, gains):
    S = qa.shape[0]
    cm, sm = tabs

    def fn(r, c, o):
        qa_r, kv_r, krd_v, cmv, smv = r[0], r[1], r[2][...], r[3][...], r[4][...]
        g_n, g_r, g_kn = (x[...] for x in c)
        for h in range(HEADS):
            rp = qa_r[:, HEADS * HD + LANES * (h // 2):HEADS * HD + LANES * (h // 2 + 1)]
            nope, rpm, mine, rr = _mla_q_parts(qa_r, h, rp, g_n, g_r)
            o[0][h, :, 0:HD] = (nope * (rr * MLA_SCALE) * g_n).astype(BF16)
            o[0][h, :, HD:2 * HD] = _rope(rpm * (rr * MLA_SCALE) * g_r, cmv, smv, 64, 32).astype(BF16)
            kn = kv_r[:, 2 * HD * h:2 * HD * h + HD]
            o[1][h, :, 0:HD] = (kn * _rms_r(_rsum(kn * kn), HD) * g_kn).astype(BF16)
            o[1][h, :, HD:2 * HD] = krd_v
            o[2][h] = kv_r[:, 2 * HD * h + HD:2 * HD * (h + 1)].astype(BF16)
        return ()

    outs = [_sds((HEADS, S, 2 * HD), BF16), _sds((HEADS, S, 2 * HD), BF16), _sds((HEADS, S, HD), BF16)]
    return _rowwise(name, fn, [qa, kv, krd, cm, sm], gains, outs)


def _mla_prep_bwd(name, dq, dk, dv, qa, kv, tabs, gains):
    S = qa.shape[0]
    cm, sm = tabs

    def fn(r, c, o):
        dq_r, dk_r, dv_r, qa_r, kv_r, cmv, smv = r[0], r[1], r[2], r[3], r[4], r[5][...], r[6][...]
        g_n, g_r, g_kn = (x[...] for x in c)
        a_n = jnp.zeros((1, HD), F32)
        a_r = jnp.zeros((1, LANES), F32)
        a_kn = jnp.zeros((1, HD), F32)
        dkrd = jnp.zeros(cmv.shape, F32)
        drp = None
        for h in range(HEADS):
            rp = qa_r[:, HEADS * HD + LANES * (h // 2):HEADS * HD + LANES * (h // 2 + 1)]
            nope, rpm, mine, rr = _mla_q_parts(qa_r, h, rp, g_n, g_r)
            dn = dq_r[h, :, 0:HD] * MLA_SCALE
            dr = _rope_bwd(jnp.where(mine, dq_r[h, :, HD:2 * HD] * MLA_SCALE, 0.0), cmv, smv, 64, 32)
            dot = (_rsum(dn * g_n * nope) + _rsum(dr * g_r * rpm)) * (1.0 / MLA_QK)
            r3 = rr * rr * rr
            o[0][:, HD * h:HD * (h + 1)] = (rr * dn * g_n - nope * r3 * dot).astype(BF16)
            part = jnp.where(mine, rr * dr * g_r - rpm * r3 * dot, 0.0)
            drp = part if h % 2 == 0 else drp + part
            if h % 2 == 1:
                o[0][:, HEADS * HD + LANES * (h // 2):HEADS * HD + LANES * (h // 2 + 1)] = drp.astype(BF16)
            a_n = a_n + _csum(dn * nope * rr)
            a_r = a_r + _csum(dr * rpm * rr)
            kn = kv_r[:, 2 * HD * h:2 * HD * h + HD]
            rk = _rms_r(_rsum(kn * kn), HD)
            dkn = dk_r[h, :, 0:HD]
            o[1][:, 2 * HD * h:2 * HD * h + HD] = _rms_bwd(kn, rk, g_kn, dkn, 1.0 / HD).astype(BF16)
            o[1][:, 2 * HD * h + HD:2 * HD * (h + 1)] = dv_r[h].astype(BF16)
            a_kn = a_kn + _csum(dkn * kn * rk)
            dkrd = dkrd + dk_r[h, :, HD:2 * HD]
        o[2][...] = dkrd
        return (a_n, a_r, a_kn)

    outs = [_sds(qa.shape, BF16), _sds(kv.shape, BF16), _sds((S, LANES), F32)]
    accs = [_sds((1, HD), F32), _sds((1, LANES), F32), _sds((1, HD), F32)]
    return _rowwise(name, fn, [dq, dk, dv, qa, kv, cm, sm], gains, outs, accs, tm=128)


def _seg64_r(x):
    low = _lane(x.shape) < 64
    x2 = x * x
    s0 = _rsum(jnp.where(low, x2, 0.0))
    s1 = _rsum(jnp.where(low, 0.0, x2))
    return jnp.where(low, _rms_r(s0, SWA_D), _rms_r(s1, SWA_D)), low


def _swa_prep(name, qkv, tabs, gains):
    S = qkv.shape[0]
    nq, nk = SWA_HEADS * SWA_D, SWA_KV * SWA_D
    cs, ss = tabs

    def fn(r, c, o):
        x_r, csv, ssv = r[0], r[1][...], r[2][...]
        g_q, g_k = c[0][...], c[1][...]
        for g in range((nq + nk) // LANES):
            x = x_r[:, LANES * g:LANES * (g + 1)]
            rr, _ = _seg64_r(x)
            y = _rope(x * rr * (g_q if g < nq // LANES else g_k), csv, ssv, SWA_D, SWA_ROT // 2)
            if g < nq // LANES:
                o[0][:, LANES * g:LANES * (g + 1)] = (y * SWA_SCALE).astype(BF16)
            else:
                o[1][:, LANES * g - nq:LANES * (g + 1) - nq] = y.astype(BF16)
        o[2][...] = x_r[:, nq + nk:nq + 2 * nk].astype(BF16)
        return ()

    outs = [_sds((S, nq), BF16), _sds((S, nk), BF16), _sds((S, nk), BF16)]
    return _rowwise(name, fn, [qkv, cs, ss], gains, outs)


def _swa_prep_bwd(name, dq, dk, dv, qkv, tabs, gains):
    nq, nk = SWA_HEADS * SWA_D, SWA_KV * SWA_D
    cs, ss = tabs

    def fn(r, c, o):
        dq_r, dk_r, x_r, csv, ssv = r[0], r[1], r[3], r[4][...], r[5][...]
        g_q, g_k = c[0][...], c[1][...]
        acc = [jnp.zeros((1, LANES), F32), jnp.zeros((1, LANES), F32)]
        for g in range((nq + nk) // LANES):
            isq = g < nq // LANES
            x = x_r[:, LANES * g:LANES * (g + 1)]
            rr, low = _seg64_r(x)
            d = dq_r[:, LANES * g:LANES * (g + 1)] * SWA_SCALE if isq else dk_r[:, LANES * g - nq:LANES * (g + 1) - nq]
            d = _rope_bwd(d, csv, ssv, SWA_D, SWA_ROT // 2)
            dyg = d * (g_q if isq else g_k)
            t = dyg * x
            d0 = _rsum(jnp.where(low, t, 0.0))
            d1 = _rsum(jnp.where(low, 0.0, t))
            dot = jnp.where(low, d0, d1) * (1.0 / SWA_D)
            o[0][:, LANES * g:LANES * (g + 1)] = (rr * dyg - x * (rr * rr * rr) * dot).astype(BF16)
            acc[0 if isq else 1] = acc[0 if isq else 1] + _csum(d * x * rr)
        o[0][:, nq + nk:nq + 2 * nk] = r[2][...].astype(BF16)
        return tuple(acc)

    return _rowwise(name, fn, [dq, dk, dv, qkv, cs, ss], gains, [_sds(qkv.shape, BF16)],
                    [_sds((1, LANES), F32), _sds((1, LANES), F32)], tm=128)


def _delta(name, do, o_, width):
    S, C = do.shape
    nh = C // width

    def fn(r, c, o):
        for g in range(C // LANES):
            t = r[0][:, LANES * g:LANES * (g + 1)].astype(F32) * r[1][:, LANES * g:LANES * (g + 1)].astype(F32)
            if width == LANES:
                o[0][g] = _rsum(t)
            else:
                low = _lane(t.shape) < 64
                o[0][2 * g] = _rsum(jnp.where(low, t, 0.0))
                o[0][2 * g + 1] = _rsum(jnp.where(low, 0.0, t))
        return ()

    return _rowwise(name, fn, [do, o_], [], [_sds((nh, S, 1), F32)])[0]


_NT = (((1,), (1,)), ((), ()))
_NN = (((1,), (0,)), ((), ()))
_TN = (((0,), (0,)), ((), ()))


def _flash_fwd(name, q, k, v, tq=512, tk=2048):
    H, S, dk = q.shape
    G = H // k.shape[0]
    dv = v.shape[2]
    tq, tk = _tile(S, (tq, 256, 128)), _tile(S, (tk, 256, 128))
    nk = S // tk

    def body(q_ref, k_ref, v_ref, o_ref, lse_ref, m_s, l_s, acc_s):
        j = pl.program_id(2)

        @pl.when(j == 0)
        def _():
            m_s[...] = jnp.full_like(m_s, -jnp.inf)
            l_s[...] = jnp.zeros_like(l_s)
            acc_s[...] = jnp.zeros_like(acc_s)

        s = lax.dot_general(q_ref[...], k_ref[...], _NT, preferred_element_type=F32)
        m_new = jnp.maximum(m_s[...], jnp.max(s, axis=-1, keepdims=True))
        alpha = jnp.exp(m_s[...] - m_new)
        p = jnp.exp(s - m_new)
        l_s[...] = alpha * l_s[...] + _rsum(p)
        acc_s[...] = alpha * acc_s[...] + lax.dot_general(p.astype(BF16), v_ref[...], _NN,
                                                          preferred_element_type=F32)
        m_s[...] = m_new

        @pl.when(j == nk - 1)
        def _():
            o_ref[...] = (acc_s[...] / l_s[...]).astype(o_ref.dtype)
            lse_ref[...] = m_s[...] + jnp.log(l_s[...])

    return pl.pallas_call(
        body, name=name, grid=(H, S // tq, nk),
        in_specs=[pl.BlockSpec((None, tq, dk), lambda h, i, j: (h, i, 0)),
                  pl.BlockSpec((None, tk, dk), lambda h, i, j: (h // G, j, 0)),
                  pl.BlockSpec((None, tk, dv), lambda h, i, j: (h // G, j, 0))],
        out_specs=[pl.BlockSpec((tq, dv), lambda h, i, j: (i, h)),
                   pl.BlockSpec((None, tq, 1), lambda h, i, j: (h, i, 0))],
        out_shape=[_sds((S, H * dv), BF16), _sds((H, S, 1), F32)],
        scratch_shapes=[pltpu.VMEM((tq, 1), F32), pltpu.VMEM((tq, 1), F32), pltpu.VMEM((tq, dv), F32)],
        compiler_params=_params(("parallel", "parallel", "arbitrary")),
    )(q, k, v)


def _flash_bwd(name, q, k, v, do, head0, lse, delta, tq=512, tk=1024):
    H, S, dk = q.shape
    G = H // k.shape[0]
    dv = v.shape[2]
    tq, tk = _tile(S, (tq, 256, 128)), _tile(S, (tk, 256, 128))
    nq = S // tq

    def body(q_ref, k_ref, v_ref, do_ref, lse_ref, dl_ref, dq_ref, dk_ref, dv_ref, dk_s, dv_s):
        j, i = pl.program_id(1), pl.program_id(2)

        @pl.when(i == 0)
        def _():
            dk_s[...] = jnp.zeros_like(dk_s)
            dv_s[...] = jnp.zeros_like(dv_s)

        qv, kv_, dov = q_ref[...], k_ref[...], do_ref[...]
        s = lax.dot_general(qv, kv_, _NT, preferred_element_type=F32)
        p = jnp.exp(s - lse_ref[...])
        dp = lax.dot_general(dov, v_ref[...], _NT, preferred_element_type=F32)
        ds = (p * (dp - dl_ref[...])).astype(BF16)
        dv_s[...] += lax.dot_general(p.astype(BF16), dov, _TN, preferred_element_type=F32)
        dk_s[...] += lax.dot_general(ds, qv, _TN, preferred_element_type=F32)
        dqi = lax.dot_general(ds, kv_, _NN, preferred_element_type=F32)
        rows = pl.ds(pl.multiple_of(i * tq, tq), tq)

        @pl.when(j == 0)
        def _():
            dq_ref[rows, :] = dqi

        @pl.when(j > 0)
        def _():
            dq_ref[rows, :] += dqi

        @pl.when(i == nq - 1)
        def _():
            dk_ref[...] = dk_s[...]
            dv_ref[...] = dv_s[...]

    return pl.pallas_call(
        body, name=name, grid=(H, S // tk, nq),
        in_specs=[pl.BlockSpec((None, tq, dk), lambda h, j, i: (h, i, 0)),
                  pl.BlockSpec((None, tk, dk), lambda h, j, i: (h // G, j, 0)),
                  pl.BlockSpec((None, tk, dv), lambda h, j, i: (h // G, j, 0)),
                  pl.BlockSpec((tq, dv), lambda h, j, i: (i, head0 + h)),
                  pl.BlockSpec((None, tq, 1), lambda h, j, i: (head0 + h, i, 0)),
                  pl.BlockSpec((None, tq, 1), lambda h, j, i: (head0 + h, i, 0))],
        out_specs=[pl.BlockSpec((None, S, dk), lambda h, j, i: (h, 0, 0)),
                   pl.BlockSpec((None, tk, dk), lambda h, j, i: (h, j, 0)),
                   pl.BlockSpec((None, tk, dv), lambda h, j, i: (h, j, 0))],
        out_shape=[_sds((H, S, dk), F32), _sds((H, S, dk), F32), _sds((H, S, dv), F32)],
        scratch_shapes=[pltpu.VMEM((tk, dk), F32), pltpu.VMEM((tk, dv), F32)],
        compiler_params=_params(("parallel", "arbitrary", "arbitrary")),
    )(q, k, v, do, lse, delta)


def _swa_place(ref128, h):
    e, t = h % 2, (h // (SWA_HEADS // SWA_KV)) % 2
    x = ref128.astype(F32)
    if e != t:
        x = pltpu.roll(x, 64, 1)
    return jnp.where((_lane(x.shape) >= 64) == bool(t), x, 0.0).astype(BF16)


def _swa_unplace(y, h):
    e, t = h % 2, (h // (SWA_HEADS // SWA_KV)) % 2
    return pltpu.roll(y, 64, 1) if e != t else y


def _swa_specs(width, nb):
    prev = pl.BlockSpec((SWA_BLOCK, width), lambda i: (jnp.maximum(i - 1, 0), 0))
    cur = pl.BlockSpec((SWA_BLOCK, width), lambda i: (i, 0))
    nxt = pl.BlockSpec((SWA_BLOCK, width), lambda i: (jnp.minimum(i + 1, nb - 1), 0))
    return [prev, cur, nxt]


def _swa_specs3(nh, nb):
    prev = pl.BlockSpec((nh, SWA_BLOCK, 1), lambda i: (0, jnp.maximum(i - 1, 0), 0))
    cur = pl.BlockSpec((nh, SWA_BLOCK, 1), lambda i: (0, i, 0))
    nxt = pl.BlockSpec((nh, SWA_BLOCK, 1), lambda i: (0, jnp.minimum(i + 1, nb - 1), 0))
    return [prev, cur, nxt]


def _swa_valid_q(i, S):
    qpos = i * SWA_BLOCK + lax.broadcasted_iota(jnp.int32, (SWA_BLOCK, 3 * SWA_BLOCK), 0)
    kpos = (i - 1) * SWA_BLOCK + lax.broadcasted_iota(jnp.int32, (SWA_BLOCK, 3 * SWA_BLOCK), 1)
    return (jnp.abs(qpos - kpos) <= SWA_WINDOW) & (kpos >= 0) & (kpos < S)


def _swa_fwd(name, q, k, v, sink):
    S = q.shape[0]
    nb = S // SWA_BLOCK
    grp = SWA_HEADS // SWA_KV
    smem = pl.BlockSpec(memory_space=pltpu.SMEM)

    def body(sink_ref, q_ref, kp, kc, kn, vp, vc, vn, o_ref, lse_ref):
        i = pl.program_id(0)
        valid = _swa_valid_q(i, S)
        kcat = [jnp.concatenate([r[:, LANES * u:LANES * (u + 1)] for r in (kp, kc, kn)], axis=0) for u in range(2)]
        vcat = [jnp.concatenate([r[:, LANES * u:LANES * (u + 1)] for r in (vp, vc, vn)], axis=0) for u in range(2)]
        for g in range(SWA_HEADS // 2):
            pair = []
            for h in (2 * g, 2 * g + 1):
                u = (h // grp) // 2
                xq = _swa_place(q_ref[:, LANES * g:LANES * (g + 1)], h)
                s = lax.dot_general(xq, kcat[u], _NT, preferred_element_type=F32)
                s = jnp.where(valid, s, -jnp.inf)
                sk = sink_ref[0, h]
                m = jnp.maximum(jnp.max(s, axis=-1, keepdims=True), sk)
                p = jnp.exp(s - m)
                den = _rsum(p) + jnp.exp(sk - m)
                oh = lax.dot_general((p / den).astype(BF16), vcat[u], _NN, preferred_element_type=F32)
                lse_ref[h] = m + jnp.log(den)
                pair.append(_swa_unplace(oh, h))
            o_ref[:, LANES * g:LANES * (g + 1)] = jnp.where(_lane(pair[0].shape) < 64, pair[0], pair[1]).astype(BF16)

    return pl.pallas_call(
        body, name=name, grid=(nb,),
        in_specs=[smem, pl.BlockSpec((SWA_BLOCK, q.shape[1]), lambda i: (i, 0))]
        + _swa_specs(k.shape[1], nb) + _swa_specs(v.shape[1], nb),
        out_specs=[pl.BlockSpec((SWA_BLOCK, q.shape[1]), lambda i: (i, 0)),
                   pl.BlockSpec((SWA_HEADS, SWA_BLOCK, 1), lambda i: (0, i, 0))],
        out_shape=[_sds(q.shape, BF16), _sds((SWA_HEADS, S, 1), F32)],
        compiler_params=_params(("parallel",)),
    )(sink, q, k, k, k, v, v, v)


def _swa_bwd_q(name, q, k, v, do, lse, delta, sink):
    S = q.shape[0]
    nb = S // SWA_BLOCK
    grp = SWA_HEADS // SWA_KV
    smem = pl.BlockSpec(memory_space=pltpu.SMEM)
    row = pl.BlockSpec((SWA_BLOCK, q.shape[1]), lambda i: (i, 0))
    hrow = pl.BlockSpec((SWA_HEADS, SWA_BLOCK, 1), lambda i: (0, i, 0))

    def body(sink_ref, q_ref, do_ref, lse_ref, dl_ref, kp, kc, kn, vp, vc, vn, dq_ref, ds_ref):
        i = pl.program_id(0)

        @pl.when(i == 0)
        def _():
            ds_ref[...] = jnp.zeros_like(ds_ref)

        valid = _swa_valid_q(i, S)
        kcat = [jnp.concatenate([r[:, LANES * u:LANES * (u + 1)] for r in (kp, kc, kn)], axis=0) for u in range(2)]
        vcat = [jnp.concatenate([r[:, LANES * u:LANES * (u + 1)] for r in (vp, vc, vn)], axis=0) for u in range(2)]
        for g in range(SWA_HEADS // 2):
            pair = []
            for h in (2 * g, 2 * g + 1):
                u = (h // grp) // 2
                xq = _swa_place(q_ref[:, LANES * g:LANES * (g + 1)], h)
                xdo = _swa_place(do_ref[:, LANES * g:LANES * (g + 1)], h)
                lse_h, dl_h = lse_ref[h], dl_ref[h]
                s = lax.dot_general(xq, kcat[u], _NT, preferred_element_type=F32)
                p = jnp.exp(jnp.where(valid, s, -jnp.inf) - lse_h)
                dp = lax.dot_general(xdo, vcat[u], _NT, preferred_element_type=F32)
                dsv = (p * (dp - dl_h)).astype(BF16)
                dqh = lax.dot_general(dsv, kcat[u], _NN, preferred_element_type=F32)
                pair.append(_swa_unplace(dqh, h))
                dsink = -_csum(jnp.exp(sink_ref[0, h] - lse_h) * dl_h)
                ds_ref[h:h + 1, :] += jnp.broadcast_to(dsink, (1, LANES))
            dq_ref[:, LANES * g:LANES * (g + 1)] = jnp.where(_lane(pair[0].shape) < 64, pair[0], pair[1])

    return pl.pallas_call(
        body, name=name, grid=(nb,),
        in_specs=[smem, row, row, hrow, hrow] + _swa_specs(k.shape[1], nb) + _swa_specs(v.shape[1], nb),
        out_specs=[row, pl.BlockSpec((SWA_HEADS, LANES), lambda i: (0, 0))],
        out_shape=[_sds(q.shape, F32), _sds((SWA_HEADS, LANES), F32)],
        compiler_params=_params(("arbitrary",)),
    )(sink, q, do, lse, delta, k, k, k, v, v, v)


def _swa_bwd_kv(name, q, k, v, do, lse, delta):
    S = q.shape[0]
    nb = S // SWA_BLOCK
    grp = SWA_HEADS // SWA_KV
    krow = pl.BlockSpec((SWA_BLOCK, k.shape[1]), lambda i: (i, 0))

    def body(qp, qc, qn, dop, doc, don, lp, lc, ln, dp_, dc_, dn_, k_ref, v_ref, dk_ref, dv_ref):
        j = pl.program_id(0)
        qpos = (j - 1) * SWA_BLOCK + lax.broadcasted_iota(jnp.int32, (3 * SWA_BLOCK, SWA_BLOCK), 0)
        kpos = j * SWA_BLOCK + lax.broadcasted_iota(jnp.int32, (3 * SWA_BLOCK, SWA_BLOCK), 1)
        valid = (jnp.abs(qpos - kpos) <= SWA_WINDOW) & (qpos >= 0) & (qpos < S)
        for u in range(SWA_KV // 2):
            kc = k_ref[:, LANES * u:LANES * (u + 1)]
            vc = v_ref[:, LANES * u:LANES * (u + 1)]
            dk_a = jnp.zeros((SWA_BLOCK, LANES), F32)
            dv_a = jnp.zeros((SWA_BLOCK, LANES), F32)
            for h in range(2 * u * grp, (2 * u + 2) * grp):
                g = h // 2
                xq = jnp.concatenate([_swa_place(r[:, LANES * g:LANES * (g + 1)], h) for r in (qp, qc, qn)], axis=0)
                xdo = jnp.concatenate([_swa_place(r[:, LANES * g:LANES * (g + 1)], h) for r in (dop, doc, don)], axis=0)
                lse_h = jnp.concatenate([lp[h], lc[h], ln[h]], axis=0)
                dl_h = jnp.concatenate([dp_[h], dc_[h], dn_[h]], axis=0)
                s = lax.dot_general(xq, kc, _NT, preferred_element_type=F32)
                p = jnp.where(valid, jnp.exp(s - lse_h), 0.0)
                dpv = lax.dot_general(xdo, vc, _NT, preferred_element_type=F32)
                dsv = (p * (dpv - dl_h)).astype(BF16)
                dv_a = dv_a + lax.dot_general(p.astype(BF16), xdo, _TN, preferred_element_type=F32)
                dk_a = dk_a + lax.dot_general(dsv, xq, _TN, preferred_element_type=F32)
            dk_ref[:, LANES * u:LANES * (u + 1)] = dk_a
            dv_ref[:, LANES * u:LANES * (u + 1)] = dv_a

    return pl.pallas_call(
        body, name=name, grid=(nb,),
        in_specs=_swa_specs(q.shape[1], nb) + _swa_specs(do.shape[1], nb)
        + _swa_specs3(SWA_HEADS, nb) + _swa_specs3(SWA_HEADS, nb) + [krow, krow],
        out_specs=[krow, krow],
        out_shape=[_sds(k.shape, F32), _sds(v.shape, F32)],
        compiler_params=_params(("parallel",)),
    )(q, q, q, do, do, do, lse, lse, lse, delta, delta, delta, k, v)


def _exchange(name, tensors):
    n = len(tensors)
    outs = [_sds(a.shape if sc else (N_DEV,) + a.shape, a.dtype) for a, sc in tensors]

    def body(*refs):
        ins, dst, send, recv, loc = refs[:n], refs[n:2 * n], refs[2 * n], refs[2 * n + 1], refs[2 * n + 2]
        x, y, c = lax.axis_index("x"), lax.axis_index("y"), lax.axis_index("c")
        me = 4 * x + 2 * y + c

        def peer(kk):
            return (x ^ ((kk >> 2) & 1), y ^ ((kk >> 1) & 1), c ^ (kk & 1))

        def copy(t, kk):
            px, py, pc = peer(kk)
            src = ins[t].at[4 * px + 2 * py + pc] if tensors[t][1] else ins[t]
            return pltpu.make_async_remote_copy(src_ref=src, dst_ref=dst[t].at[me], send_sem=send.at[t, kk - 1],
                                                recv_sem=recv.at[t, kk - 1], device_id=(px, py, pc),
                                                device_id_type=pl.DeviceIdType.MESH)

        def arrival(t, kk):
            px, py, pc = peer(kk)
            src = ins[t].at[me] if tensors[t][1] else ins[t]
            return pltpu.make_async_remote_copy(src_ref=src, dst_ref=dst[t].at[4 * px + 2 * py + pc],
                                                send_sem=send.at[t, kk - 1], recv_sem=recv.at[t, kk - 1],
                                                device_id=(px, py, pc), device_id_type=pl.DeviceIdType.MESH)

        own = [pltpu.make_async_copy(ins[t].at[me] if tensors[t][1] else ins[t], dst[t].at[me], loc.at[t])
               for t in range(n)]
        sent = [copy(t, kk) for t in range(n) for kk in range(1, N_DEV)]
        for cp in own + sent:
            cp.start()
        for t in range(n):
            for kk in range(1, N_DEV):
                arrival(t, kk).wait_recv()
        for cp in sent:
            cp.wait_send()
        for cp in own:
            cp.wait()

    any_spec = pl.BlockSpec(memory_space=pl.ANY)
    return pl.pallas_call(
        body, name=name, in_specs=[any_spec] * n, out_specs=[any_spec] * n, out_shape=outs,
        scratch_shapes=[pltpu.SemaphoreType.DMA((n, N_DEV - 1)), pltpu.SemaphoreType.DMA((n, N_DEV - 1)),
                        pltpu.SemaphoreType.DMA((n,))],
        compiler_params=pltpu.CompilerParams(has_side_effects=True),
    )(*[a for a, _ in tensors])


def _peer_of(x, y, c, kk):
    return (x ^ ((kk >> 2) & 1), y ^ ((kk >> 1) & 1), c ^ (kk & 1))


def _own_slot(a, scatter):
    me = 4 * lax.axis_index("x") + 2 * lax.axis_index("y") + lax.axis_index("c")
    shape = a.shape if scatter else (N_DEV,) + a.shape
    own = lax.dynamic_slice_in_dim(a, me, 1, 0) if scatter else a[None]
    return lax.dynamic_update_slice_in_dim(lax.empty(shape, a.dtype), own, me, 0)


def _exchange_start(name, tensors):
    n = len(tensors)
    hbm = pl.BlockSpec(memory_space=pltpu.HBM)
    sem = pl.BlockSpec(memory_space=pltpu.SEMAPHORE)
    srcs = [pltpu.with_memory_space_constraint(a, pltpu.HBM) for a, _ in tensors]
    lands = [pltpu.with_memory_space_constraint(_own_slot(a, sc), pltpu.HBM) for a, sc in tensors]

    def body(*refs):
        ins, dst = refs[:n], refs[n:2 * n]
        send, recv, token = refs[2 * n], refs[2 * n + 1], refs[4 * n + 2]
        x, y, c = lax.axis_index("x"), lax.axis_index("y"), lax.axis_index("c")
        me = 4 * x + 2 * y + c
        for t in range(n):
            for kk in range(1, N_DEV):
                px, py, pc = _peer_of(x, y, c, kk)
                src = ins[t].at[4 * px + 2 * py + pc] if tensors[t][1] else ins[t]
                k1 = t * (N_DEV - 1) + kk - 1
                pltpu.make_async_remote_copy(src_ref=src, dst_ref=dst[t].at[me], send_sem=send.at[k1],
                                             recv_sem=recv.at[k1], device_id=(px, py, pc),
                                             device_id_type=pl.DeviceIdType.MESH).start()
        token[...] = jnp.zeros_like(token)

    out = pl.pallas_call(
        body, name=name,
        in_specs=[hbm] * (2 * n),
        out_specs=[sem, sem] + [hbm] * (2 * n) + [pl.BlockSpec(memory_space=pltpu.VMEM)],
        out_shape=[pltpu.SemaphoreType.DMA((n * (N_DEV - 1),)), pltpu.SemaphoreType.DMA((n * (N_DEV - 1),))]
        + [pltpu.HBM(a.shape, a.dtype) for a in srcs] + [pltpu.HBM(a.shape, a.dtype) for a in lands]
        + [_sds((8, LANES), F32)],
        input_output_aliases={i: i + 2 for i in range(2 * n)},
        compiler_params=pltpu.CompilerParams(has_side_effects=pltpu.SideEffectType.DATAFLOW_SIDE_EFFECTING),
    )(*srcs, *lands)
    return (out[0], out[1], out[2:2 + n], out[2 + n:2 + 2 * n], [sc for _, sc in tensors]), out[2 + 2 * n]


def _exchange_wait(name, started, after):
    send_s, recv_s, srcs, lands, flags = started
    n = len(srcs)
    hbm = pl.BlockSpec(memory_space=pltpu.HBM)
    sem = pl.BlockSpec(memory_space=pltpu.SEMAPHORE)

    def body(*refs):
        ins, dst, send, recv = refs[:n], refs[n:2 * n], refs[2 * n], refs[2 * n + 1]
        x, y, c = lax.axis_index("x"), lax.axis_index("y"), lax.axis_index("c")
        me = 4 * x + 2 * y + c
        for t in range(n):
            for kk in range(1, N_DEV):
                px, py, pc = _peer_of(x, y, c, kk)
                src = ins[t].at[me] if flags[t] else ins[t]
                k1 = t * (N_DEV - 1) + kk - 1
                cp = pltpu.make_async_remote_copy(src_ref=src, dst_ref=dst[t].at[4 * px + 2 * py + pc],
                                                  send_sem=send.at[k1], recv_sem=recv.at[k1],
                                                  device_id=(px, py, pc), device_id_type=pl.DeviceIdType.MESH)
                cp.wait_send()
                cp.wait_recv()

    out = pl.pallas_call(
        body, name=name,
        in_specs=[hbm] * (2 * n) + [sem, sem, pl.BlockSpec(memory_space=pl.ANY)],
        out_specs=[hbm] * (2 * n),
        out_shape=[pltpu.HBM(a.shape, a.dtype) for a in srcs] + [pltpu.HBM(a.shape, a.dtype) for a in lands],
        input_output_aliases={i: i for i in range(2 * n)},
        compiler_params=pltpu.CompilerParams(has_side_effects=pltpu.SideEffectType.DATAFLOW_SIDE_EFFECTING),
    )(*srcs, *lands, send_s, recv_s, after)
    return out[n:]


def _sum8(name, parts):
    _, R, C = parts.shape

    def fn(r, c, o):
        acc = r[0][0].astype(F32)
        for s in range(1, N_DEV):
            acc = acc + r[0][s].astype(F32)
        o[0][...] = acc
        return ()

    tm = R if R % 256 else 256
    return _rowwise(name, fn, [parts], [], [_sds((R, C), F32)], tm=tm)[0]


def _adamw(name, w, g, m, v):
    bc1 = 1.0 - ADAM_B1 ** ADAM_STEP
    bc2 = 1.0 - ADAM_B2 ** ADAM_STEP

    def fn(r, c, o):
        wv, gv, mv, vv = (x[...] for x in r)
        mn = ADAM_B1 * mv + (1.0 - ADAM_B1) * gv
        vn = ADAM_B2 * vv + (1.0 - ADAM_B2) * (gv * gv)
        o[0][...] = -ADAM_LR * ((mn / bc1) / (jnp.sqrt(vn / bc2) + ADAM_EPS) + ADAM_WD * wv)
        o[1][...] = mn
        o[2][...] = vn
        return ()

    tm = max(8, min(512, 1 << ((16 << 20) // (56 * w.shape[1])).bit_length() - 1))
    return _rowwise(name, fn, [w, g, m, v], [], [_sds(w.shape, F32)] * 3, tm=tm)


def _rope_cs(pos, dim, theta):
    inv = jnp.float32(theta) ** (-jnp.arange(0, dim, 2, dtype=jnp.float32) / dim)
    ang = pos.astype(jnp.float32)[:, None] * inv[None, :]
    return jnp.cos(ang), jnp.sin(ang)


def _tables(S):
    pos = jnp.arange(S)
    c, s = _rope_cs(pos, MLA_ROPE, ROPE_THETA)
    mla = (jnp.concatenate([c, c, c, c], 1), jnp.concatenate([-s, s, -s, s], 1))
    rc, rs = _rope_cs(pos // GRID_W, HD // 2, AXIAL_THETA)
    cc, cs = _rope_cs(pos % GRID_W, HD // 2, AXIAL_THETA)
    axial = (jnp.concatenate([rc, rc, cc, cc], 1), jnp.concatenate([-rs, rs, -cs, cs], 1))
    c, s = _rope_cs(pos, SWA_ROT, ROPE_THETA)
    one, zero = jnp.ones((S, SWA_D - SWA_ROT), F32), jnp.zeros((S, SWA_D - SWA_ROT), F32)
    swa = (jnp.concatenate([c, c, one, c, c, one], 1), jnp.concatenate([-s, s, zero, -s, s, zero], 1))
    return mla, axial, swa


_WEIGHTS = ['even_norm', 'even_w_in', 'mla_q_lat_norm', 'mla_kv_lat_norm', 'mla_w_uq', 'mla_w_ukv', 'mla_q_norm',
            'mla_k_nope_norm', 'mla_k_rope_norm', 'gqa_q_norm', 'gqa_k_norm', 'even_w_out', 'odd_norm', 'odd_w_qkv',
            'swa_q_norm', 'swa_k_norm', 'swa_sink', 'odd_w_out', 'mlp_norm', 'mlp_w_up', 'mlp_w_down']
_SMALL = ['even_norm', 'mla_q_lat_norm', 'mla_kv_lat_norm', 'mla_q_norm', 'mla_k_nope_norm', 'mla_k_rope_norm',
          'gqa_q_norm', 'gqa_k_norm', 'swa_q_norm', 'swa_k_norm', 'swa_sink', 'mlp_norm']


def _relu2(acc):
    rl = jnp.maximum(acc, 0.0)
    return rl * rl, rl


def _mul2(acc, rl):
    return (acc * (2.0 * rl.astype(F32)),)


def _add(acc, res):
    return (acc + res,)


def _step(x, tgt, w, m, v):
    S, D = x.shape
    nw = len(_WEIGHTS)
    tab_mla, tab_ax, tab_swa = _tables(S)
    bf = lambda a: a.astype(BF16)

    w_up_s, w_dn_s = w['mlp_w_up'], w['mlp_w_down']
    gathered = _exchange("gather_first", [
        (bf(w['even_w_in'][0].T), False), (bf(w['mla_w_uq'][0].T), False), (bf(w['mla_w_ukv'][0].T), False)])
    zero = jnp.minimum(jnp.abs(gathered[2][0, 0:1, 0:1].astype(F32)), 0.0)
    later, tok = _exchange_start("gather_rest_start", [
        (bf(w['even_w_out'][0]), False), (bf(w['odd_w_qkv'][0].T), False), (bf(w['odd_w_out'][0]), False),
        (bf(w_up_s[0].T), False), (bf(w_up_s[1].T), False), (bf(w_dn_s[0]), False), (bf(w_dn_s[1]), False),
        (w['odd_norm'] + zero, False)])
    flat = lambda a: a.reshape((a.shape[0] * a.shape[1],) + a.shape[2:])
    win_t, wuq_g, wukv_t = [flat(a) for a in gathered]
    win_t = jnp.concatenate([win_t[:KR_END], jnp.zeros((LANES - MLA_ROPE, D), BF16), win_t[KR_END:]], 0)
    wuq_g = gathered[1]
    wuq_t = jnp.concatenate([wuq_g[:, :HD].reshape(HEADS * HD, Q_LORA),
                             wuq_g[:, HD:].reshape(HEADS * MLA_ROPE, Q_LORA)], 0)

    z64 = jnp.zeros((1, 64), F32)
    qn = w['mla_q_norm']
    g_even = [w['mla_q_lat_norm'], w['mla_kv_lat_norm'], jnp.concatenate([w['mla_k_rope_norm'], z64], 1),
              w['gqa_q_norm'], w['gqa_k_norm']]
    g_mla = [qn[:, :HD], jnp.concatenate([qn[:, HD:], qn[:, HD:]], 1), w['mla_k_nope_norm']]
    g_swa = [jnp.concatenate([w['swa_q_norm']] * 2, 1), jnp.concatenate([w['swa_k_norm']] * 2, 1)]

    def mlp_fwd(l, xin):
        hn = _rmsnorm(f"mlp{l}_norm", xin, w['mlp_norm'][l:l + 1])
        a, rl = _mm(f"mlp{l}_up", hn, wup_t[l], "nt", (BF16, BF16), epilogue=_relu2)
        xout = _mm(f"mlp{l}_down", a, wdn[l], "nn", (F32,), epilogue=_add, extras=(xin,))
        return xout, (hn, a, rl)

    h0 = _rmsnorm("even_norm", x, w['even_norm'] + tok[0:1, 0:1])
    proj = _mm("even_in", h0, win_t, "nt", (F32,), tm=512, tn=P_END)
    cqn, ckvn, krd, qg, kg, vg = _even_prep("even_prep", proj, tab_mla + tab_ax, g_even)
    qa = _mm("mla_uq", cqn, wuq_t, "nt", (F32,))
    kv = _mm("mla_ukv", ckvn, wukv_t, "nt", (F32,))
    q_a, k_a, v_a = _mla_prep("mla_prep", qa, kv, krd, tab_mla, g_mla)
    o_a, lse_a = _flash_fwd("mla_attn", q_a, k_a, v_a)
    o_g, lse_g = _flash_fwd("gqa_attn", qg, kg, vg)
    merged = jnp.concatenate([o_a, o_g], 1)
    rest = _exchange_wait("gather_rest_wait", later, merged)
    wout_e, wqkv_t, wout_o, wup0_t, wup1_t, wdn0, wdn1 = [flat(a) for a in rest[:7]]
    odd_norm = rest[7].reshape(1, D)
    wup_t, wdn = (wup0_t, wup1_t), (wdn0, wdn1)
    x1 = _mm("even_out", merged, wout_e, "nn", (F32,), epilogue=_add, extras=(x,))
    x2, mlp0 = mlp_fwd(0, x1)

    h1 = _rmsnorm("odd_norm", x2, odd_norm)
    qkv = _mm("odd_qkv", h1, wqkv_t, "nt", (F32,))
    q_s, k_s, v_s = _swa_prep("swa_prep", qkv, tab_swa, g_swa)
    o_s, lse_s = _swa_fwd("swa_attn", q_s, k_s, v_s, w['swa_sink'])
    x3 = _mm("odd_out", o_s, wout_o, "nn", (F32,), epilogue=_add, extras=(x2,))
    x4, mlp1 = mlp_fwd(1, x3)

    dy, loss_acc = _loss("loss", x4, tgt)
    loss = lax.psum(0.5 / D * loss_acc[0, 0], ("x", "y", "c"))

    gsm = {}

    def mlp_bwd(l, dout, xin, saved):
        hn, a, rl = saved
        du = _mm(f"mlp{l}_dact", dout, wdn[l], "nt", (BF16,), epilogue=_mul2, extras=(rl,))
        g_dn = _mm(f"mlp{l}_gdown", a, dout, "tn", (BF16,))
        g_up = _mm(f"mlp{l}_gup", du, hn, "tn", (BF16,))
        dhn = _mm(f"mlp{l}_dnorm", du, wup_t[l], "nn", (F32,))
        din, g_n = _rmsnorm_bwd(f"mlp{l}_norm_bwd", dout, xin, dhn, w['mlp_norm'][l:l + 1])
        return din, g_dn, g_up, g_n

    dx3, g_dn1, g_up1, g_mn1 = mlp_bwd(1, dy, x3, mlp1)
    split = lambda a: a.reshape((N_DEV, a.shape[0] // N_DEV) + a.shape[1:])
    sc1, tok1 = _exchange_start("scatter_mlp1_start", [(split(g_up1), True), (split(g_dn1), True)])
    sink = w['swa_sink'] + tok1[0:1, 0:SWA_HEADS]

    g_wout_o = _mm("odd_gout", o_s, dx3, "tn", (BF16,))
    do_s = _mm("odd_dattn", dx3, wout_o, "nt", (BF16,))
    dl_s = _delta("swa_delta", do_s, o_s, SWA_D)
    dq_s, dsink = _swa_bwd_q("swa_bwd_q", q_s, k_s, v_s, do_s, lse_s, dl_s, sink)
    dk_s, dv_s = _swa_bwd_kv("swa_bwd_kv", q_s, k_s, v_s, do_s, lse_s, dl_s)
    dqkv, g_sq, g_sk = _swa_prep_bwd("swa_prep_bwd", dq_s, dk_s, dv_s, qkv, tab_swa, g_swa)
    g_wqkv = _mm("odd_gqkv", dqkv, h1, "tn", (BF16,))
    dh1 = _mm("odd_dnorm", dqkv, wqkv_t, "nn", (F32,))
    dx2, g_on = _rmsnorm_bwd("odd_norm_bwd", dx3, x2, dh1, odd_norm)
    gsm['swa_q_norm'] = g_sq[:, :64] + g_sq[:, 64:]
    gsm['swa_k_norm'] = g_sk[:, :64] + g_sk[:, 64:]
    gsm['swa_sink'] = dsink[:, 0].reshape(1, SWA_HEADS)

    dx1, g_dn0, g_up0, g_mn0 = mlp_bwd(0, dx2, x1, mlp0)
    sc2, tok2 = _exchange_start("scatter_mid_start", [(split(g_wout_o), True), (split(g_wqkv), True),
                                                      (split(g_up0), True), (split(g_dn0), True)])
    gsm['mlp_norm'] = jnp.concatenate([g_mn0, g_mn1], 0)

    g_wout_e = _mm("even_gout", merged, dx1, "tn", (BF16,))
    dmerged = _mm("even_dattn", dx1, wout_e, "nt", (BF16,))
    dl_e = _delta("even_delta", dmerged, merged, HD)
    lse_e = jnp.concatenate([lse_a, lse_g], 0) + tok2[0, 0]
    dq_a, dk_a, dv_a = _flash_bwd("mla_attn_bwd", q_a, k_a, v_a, dmerged, 0, lse_e, dl_e)
    dq_g, dk_g, dv_g = _flash_bwd("gqa_attn_bwd", qg, kg, vg, dmerged, HEADS, lse_e, dl_e)
    dqa, dkv, dkrd, g_qnn, g_qnr, g_kn = _mla_prep_bwd("mla_prep_bwd", dq_a, dk_a, dv_a, qa, kv, tab_mla, g_mla)
    g_wuq = _mm("mla_guq", dqa, cqn, "tn", (BF16,))
    dcqn = _mm("mla_dq_lat", dqa, wuq_t, "nn", (F32,))
    g_wukv = _mm("mla_gukv", dkv, ckvn, "tn", (BF16,))
    dckvn = _mm("mla_dkv_lat", dkv, wukv_t, "nn", (F32,))
    dproj, g_ql, g_kvl, g_kr, g_gq, g_gk = _even_prep_bwd("even_prep_bwd", dcqn, dckvn, dkrd, dq_g, dk_g, dv_g,
                                                          proj, tab_mla + tab_ax, g_even)
    g_win = _mm("even_gin", dproj, h0, "tn", (BF16,), tm=P_END)
    dh0 = _mm("even_dnorm", dproj, win_t, "nn", (F32,), tk=P_END)
    grad_x, g_en = _rmsnorm_bwd("even_norm_bwd", dx1, x, dh0, w['even_norm'])
    gsm.update(even_norm=g_en, mla_q_lat_norm=g_ql, mla_kv_lat_norm=g_kvl,
               mla_q_norm=jnp.concatenate([g_qnn, g_qnr[:, :64] + g_qnr[:, 64:]], 1), mla_k_nope_norm=g_kn,
               mla_k_rope_norm=g_kr[:, :64], gqa_q_norm=g_gq, gqa_k_norm=g_gk)

    g_win = jnp.concatenate([g_win[:KR_END], g_win[P_QG:]], 0)
    g_wuq = jnp.concatenate([g_wuq[:HEADS * HD].reshape(HEADS, HD, Q_LORA),
                             g_wuq[HEADS * HD:].reshape(HEADS, MLA_ROPE, Q_LORA)], 1)
    small_sizes = [w[n].size for n in _SMALL] + [D]
    small_vec = jnp.concatenate([gsm[n].reshape(1, -1) for n in _SMALL] + [g_on], 1)
    pad = (-small_vec.shape[1]) % LANES
    small_vec = jnp.pad(small_vec, ((0, 0), (0, pad)))
    last = _exchange("scatter_last", [(split(g_win), True), (g_wuq, True), (split(g_wukv), True),
                                      (split(g_wout_e), True), (small_vec, False)])
    p_up1, p_dn1 = _exchange_wait("scatter_mlp1_wait", sc1, last[0])
    p_wout_o, p_wqkv, p_up0, p_dn0 = _exchange_wait("scatter_mid_wait", sc2, last[0])
    parts = dict(even_w_in=last[0], mla_w_uq=last[1], mla_w_ukv=last[2], even_w_out=last[3], odd_w_qkv=p_wqkv,
                 odd_w_out=p_wout_o, up0=p_up0, up1=p_up1, dn0=p_dn0, dn1=p_dn1)
    red = {n: _sum8("sum_" + n, p) for n, p in parts.items()}
    small_g = _sum8("sum_small", last[4])

    grads = {
        'even_w_in': red['even_w_in'].T[None], 'mla_w_uq': red['mla_w_uq'].T[None],
        'mla_w_ukv': red['mla_w_ukv'].T[None], 'even_w_out': red['even_w_out'][None],
        'odd_w_qkv': red['odd_w_qkv'].T[None], 'odd_w_out': red['odd_w_out'][None],
        'mlp_w_up': jnp.stack([red['up0'].T, red['up1'].T]), 'mlp_w_down': jnp.stack([red['dn0'], red['dn1']]),
    }
    off = 0
    for n, sz in zip(_SMALL + ['odd_norm_full'], small_sizes):
        seg = small_g[:, off:off + sz]
        off += sz
        if n == 'odd_norm_full':
            me = 4 * lax.axis_index("x") + 2 * lax.axis_index("y") + lax.axis_index("c")
            grads['odd_norm'] = lax.dynamic_slice(seg, (0, me * (D // N_DEV)), (1, D // N_DEV))
        else:
            grads[n] = seg.reshape(w[n].shape)

    delta, new_m, new_v = {}, {}, {}
    big = ['even_w_in', 'mla_w_uq', 'mla_w_ukv', 'even_w_out', 'odd_w_qkv', 'odd_w_out', 'mlp_w_up', 'mlp_w_down']
    for n in big:
        shp = w[n].shape
        two = lambda a: a.reshape(shp[0] * shp[1], shp[2])
        d_, m_, v_ = _adamw("adamw_" + n, two(w[n]), two(grads[n]), two(m[n]), two(v[n]))
        delta[n], new_m[n], new_v[n] = d_.reshape(shp), m_.reshape(shp), v_.reshape(shp)
    sm_names = _SMALL + ['odd_norm']
    pack = lambda d: jnp.concatenate([d[n].reshape(1, -1) for n in sm_names], 1)
    pw, pg, pm, pv = pack(w), pack(grads), pack(m), pack(v)
    padw = (-pw.shape[1]) % LANES
    padf = lambda a: jnp.pad(a, ((0, 0), (0, padw)))
    d_, m_, v_ = _adamw("adamw_small", padf(pw), padf(pg), padf(pm), jnp.pad(pv, ((0, 0), (0, padw)), constant_values=1.0))
    off = 0
    for n in sm_names:
        sz = w[n].size
        delta[n] = d_[:, off:off + sz].reshape(w[n].shape)
        new_m[n] = m_[:, off:off + sz].reshape(w[n].shape)
        new_v[n] = v_[:, off:off + sz].reshape(w[n].shape)
        off += sz

    return (loss, grad_x[None], *[grads[n] for n in _WEIGHTS], *[delta[n] for n in _WEIGHTS],
            *[new_m[n] for n in _WEIGHTS], *[new_v[n] for n in _WEIGHTS])


def kernel(x, even_norm, even_w_in, mla_q_lat_norm, mla_kv_lat_norm, mla_w_uq, mla_w_ukv, mla_q_norm, mla_k_nope_norm, mla_k_rope_norm, gqa_q_norm, gqa_k_norm, even_w_out, odd_norm, odd_w_qkv, swa_q_norm, swa_k_norm, swa_sink, odd_w_out, mlp_norm, mlp_w_up, mlp_w_down, loss_target, m_even_norm, m_even_w_in, m_mla_q_lat_norm, m_mla_kv_lat_norm, m_mla_w_uq, m_mla_w_ukv, m_mla_q_norm, m_mla_k_nope_norm, m_mla_k_rope_norm, m_gqa_q_norm, m_gqa_k_norm, m_even_w_out, m_odd_norm, m_odd_w_qkv, m_swa_q_norm, m_swa_k_norm, m_swa_sink, m_odd_w_out, m_mlp_norm, m_mlp_w_up, m_mlp_w_down, v_even_norm, v_even_w_in, v_mla_q_lat_norm, v_mla_kv_lat_norm, v_mla_w_uq, v_mla_w_ukv, v_mla_q_norm, v_mla_k_nope_norm, v_mla_k_rope_norm, v_gqa_q_norm, v_gqa_k_norm, v_even_w_out, v_odd_norm, v_odd_w_qkv, v_swa_q_norm, v_swa_k_norm, v_swa_sink, v_odd_w_out, v_mlp_norm, v_mlp_w_up, v_mlp_w_down):
    ws = (even_norm, even_w_in, mla_q_lat_norm, mla_kv_lat_norm, mla_w_uq, mla_w_ukv, mla_q_norm, mla_k_nope_norm, mla_k_rope_norm, gqa_q_norm, gqa_k_norm, even_w_out, odd_norm, odd_w_qkv, swa_q_norm, swa_k_norm, swa_sink, odd_w_out, mlp_norm, mlp_w_up, mlp_w_down)
    ms = (m_even_norm, m_even_w_in, m_mla_q_lat_norm, m_mla_kv_lat_norm, m_mla_w_uq, m_mla_w_ukv, m_mla_q_norm, m_mla_k_nope_norm, m_mla_k_rope_norm, m_gqa_q_norm, m_gqa_k_norm, m_even_w_out, m_odd_norm, m_odd_w_qkv, m_swa_q_norm, m_swa_k_norm, m_swa_sink, m_odd_w_out, m_mlp_norm, m_mlp_w_up, m_mlp_w_down)
    vs = (v_even_norm, v_even_w_in, v_mla_q_lat_norm, v_mla_kv_lat_norm, v_mla_w_uq, v_mla_w_ukv, v_mla_q_norm, v_mla_k_nope_norm, v_mla_k_rope_norm, v_gqa_q_norm, v_gqa_k_norm, v_even_w_out, v_odd_norm, v_odd_w_qkv, v_swa_q_norm, v_swa_k_norm, v_swa_sink, v_odd_w_out, v_mlp_norm, v_mlp_w_up, v_mlp_w_down)
    return _step(x[0], loss_target[0], dict(zip(_WEIGHTS, ws)), dict(zip(_WEIGHTS, ms)), dict(zip(_WEIGHTS, vs)))
```

```python
import functools

import jax
import jax.numpy as jnp
from jax import lax
from jax.experimental import pallas as pl
from jax.experimental.pallas import tpu as pltpu

F32 = jnp.float32
BF16 = jnp.bfloat16

N_DEV = 8
NORM_EPS = 1e-6
ROPE_THETA = 500000.0
AXIAL_THETA = 10000.0
GRID_W = 64
HEADS = 8
GQA_KV = 2
HD = 128
MLA_ROPE = 64
MLA_QK = HD + MLA_ROPE
Q_LORA = 512
KV_LORA = 256
SWA_HEADS = 32
SWA_KV = 4
SWA_D = 64
SWA_ROT = 16
SWA_WINDOW = 128
SWA_BLOCK = 128
MLA_SCALE, GQA_SCALE, SWA_SCALE = MLA_QK ** -0.5, HD ** -0.5, SWA_D ** -0.5
LANES = 128
ADAM_LR, ADAM_B1, ADAM_B2, ADAM_EPS, ADAM_WD, ADAM_STEP = 0.001, 0.9, 0.999, 1e-08, 0.01, 10
VMEM_LIMIT = 56 * 1024 * 1024

P_CQ, P_CKV, P_KR, P_QG = 0, Q_LORA, Q_LORA + KV_LORA, Q_LORA + KV_LORA + LANES
P_KG = P_QG + HEADS * HD
P_VG = P_KG + GQA_KV * HD
P_END = P_VG + GQA_KV * HD
KR_END = Q_LORA + KV_LORA + MLA_ROPE


def _tile(n, prefs):
    for t in prefs:
        if n % t == 0 and t <= n:
            return t
    return n


def _params(sem):
    return pltpu.CompilerParams(dimension_semantics=sem, vmem_limit_bytes=VMEM_LIMIT)


_DIMS = {"nn": ((1,), (0,)), "nt": ((1,), (1,)), "tn": ((0,), (0,))}


def _mm(name, a, b, mode, out_dtypes, epilogue=None, extras=(), tm=1024, tn=1024, tk=2048):
    if mode == "nn":
        (M, K), (_, N) = a.shape, b.shape
    elif mode == "nt":
        (M, K), (N, _) = a.shape, b.shape
    else:
        (K, M), (_, N) = a.shape, b.shape
    tm = _tile(M, (tm, 512, 256, 128))
    tn = _tile(N, (tn, 512, 256, 128))
    tk = _tile(K, (tk, 1024, 512, 256, 128))
    nk = K // tk
    ne, no = len(extras), len(out_dtypes)
    if mode == "tn":
        a_spec = pl.BlockSpec((tk, tm), lambda i, j, k: (k, i))
    else:
        a_spec = pl.BlockSpec((tm, tk), lambda i, j, k: (i, k))
    if mode == "nt":
        b_spec = pl.BlockSpec((tn, tk), lambda i, j, k: (j, k))
    else:
        b_spec = pl.BlockSpec((tk, tn), lambda i, j, k: (k, j))
    o_spec = pl.BlockSpec((tm, tn), lambda i, j, k: (i, j))
    dims = (_DIMS[mode], ((), ()))

    def body(a_ref, b_ref, *rest):
        ex, outs = rest[:ne], rest[ne:ne + no]
        k = pl.program_id(2)
        part = lax.dot_general(a_ref[...].astype(BF16), b_ref[...].astype(BF16), dims, preferred_element_type=F32)

        def finish(total):
            res = epilogue(total, *[e[...] for e in ex]) if epilogue else (total,)
            for o, r in zip(outs, res):
                o[...] = r.astype(o.dtype)

        if nk == 1:
            finish(part)
            return
        acc = rest[ne + no]

        @pl.when(k == 0)
        def _():
            acc[...] = part

        @pl.when((k > 0) & (k < nk - 1))
        def _():
            acc[...] += part

        @pl.when(k == nk - 1)
        def _():
            finish(acc[...] + part)

    out = pl.pallas_call(
        body, name=name, grid=(M // tm, N // tn, nk),
        in_specs=[a_spec, b_spec] + [o_spec] * ne,
        out_specs=[o_spec] * no,
        out_shape=[jax.ShapeDtypeStruct((M, N), d) for d in out_dtypes],
        scratch_shapes=[pltpu.VMEM((tm, tn), F32)] if nk > 1 else [],
        compiler_params=_params(("parallel", "parallel", "arbitrary")),
    )(a, b, *extras)
    return out[0] if no == 1 else out


def _rowwise(name, fn, rows, consts, outs, accs=(), tm=256):
    S = rows[0].shape[-2]
    tm = _tile(S, (tm, 128, 64, 32, 16, 8))
    nr, nc, no, na = len(rows), len(consts), len(outs), len(accs)

    def rspec(shape):
        if len(shape) == 2:
            return pl.BlockSpec((tm, shape[1]), lambda i: (i, 0))
        return pl.BlockSpec((shape[0], tm, shape[2]), lambda i: (0, i, 0))

    def cspec(shape):
        return pl.BlockSpec(tuple(shape), lambda i: (0,) * len(shape))

    def body(*refs):
        r, c = refs[:nr], refs[nr:nr + nc]
        o, a = refs[nr + nc:nr + nc + no], refs[nr + nc + no:]
        vals = fn(r, c, o)
        if na:
            @pl.when(pl.program_id(0) == 0)
            def _():
                for ar in a:
                    ar[...] = jnp.zeros_like(ar)

            for ar, v in zip(a, vals):
                ar[...] += v

    res = pl.pallas_call(
        body, name=name, grid=(S // tm,),
        in_specs=[rspec(x.shape) for x in rows] + [cspec(x.shape) for x in consts],
        out_specs=[rspec(x.shape) for x in outs] + [cspec(x.shape) for x in accs],
        out_shape=list(outs) + list(accs),
        compiler_params=_params(("arbitrary",) if na else ("parallel",)),
    )(*rows, *consts)
    return res


def _sds(shape, dtype):
    return jax.ShapeDtypeStruct(tuple(shape), dtype)


def _rsum(x):
    return jnp.sum(x, axis=-1, keepdims=True)


def _csum(x):
    return jnp.sum(x, axis=0, keepdims=True)


def _lane(shape):
    return lax.broadcasted_iota(jnp.int32, shape, 1)


def _partner(x, seg, half):
    lane = _lane(x.shape) % seg
    return jnp.where(lane < half, pltpu.roll(x, LANES - half, 1), pltpu.roll(x, half, 1))


def _rope(x, c, s, seg, half):
    return x * c + _partner(x, seg, half) * s


def _rope_bwd(dy, c, s, seg, half):
    t = _partner(dy * s, seg, half)
    if seg != 2 * half:
        t = jnp.where(_lane(dy.shape) % seg < 2 * half, t, 0.0)
    return dy * c + t


def _rms_r(ss, n):
    return lax.rsqrt(ss * (1.0 / n) + NORM_EPS)


def _rms_bwd(x, r, g, dy, dot_scale):
    dyg = dy * g
    return r * dyg - x * (r * r * r) * (_rsum(dyg * x) * dot_scale)


def _rmsnorm(name, x, g):
    D = x.shape[1]

    def fn(r, c, o):
        xv = r[0][...]
        o[0][...] = (xv * _rms_r(_rsum(xv * xv), D) * c[0][...]).astype(BF16)
        return ()

    return _rowwise(name, fn, [x], [g], [_sds(x.shape, BF16)])[0]


def _rmsnorm_bwd(name, dres, x, dh, g):
    D = x.shape[1]

    def fn(r, c, o):
        xv, dhv, gv = r[1][...], r[2][...], c[0][...]
        rr = _rms_r(_rsum(xv * xv), D)
        o[0][...] = r[0][...] + _rms_bwd(xv, rr, gv, dhv, 1.0 / D)
        return (_csum(dhv * xv * rr),)

    return _rowwise(name, fn, [dres, x, dh], [g], [_sds(x.shape, F32)], [_sds((1, D), F32)])


def _loss(name, y, t):
    D = y.shape[1]

    def fn(r, c, o):
        e = r[0][...] - r[1][...]
        o[0][...] = e * (1.0 / D)
        return (_csum(_rsum(e * e)),)

    return _rowwise(name, fn, [y, t], [], [_sds(y.shape, F32)], [_sds((1, 1), F32)])


def _even_prep(name, proj, tabs, gains):
    S = proj.shape[0]
    cm, sm, ca, sa = tabs

    def fn(r, c, o):
        p, cmv, smv, cav, sav = r[0], r[1][...], r[2][...], r[3][...], r[4][...]
        g_ql, g_kvl, g_kr, g_q, g_k = (x[...] for x in c)
        cq = p[:, P_CQ:P_CKV]
        o[0][...] = (cq * _rms_r(_rsum(cq * cq), Q_LORA) * g_ql).astype(BF16)
        ckv = p[:, P_CKV:P_KR]
        o[1][...] = (ckv * _rms_r(_rsum(ckv * ckv), KV_LORA) * g_kvl).astype(BF16)
        kr = p[:, P_KR:P_QG]
        y = _rope(kr * _rms_r(_rsum(kr * kr), MLA_ROPE) * g_kr, cmv, smv, 64, 32)
        o[2][...] = (y + pltpu.roll(y, 64, 1)).astype(BF16)
        for h in range(HEADS):
            xh = p[:, P_QG + HD * h:P_QG + HD * (h + 1)]
            o[3][h] = (_rope(xh * _rms_r(_rsum(xh * xh), HD) * g_q, cav, sav, 64, 32) * GQA_SCALE).astype(BF16)
        for h in range(GQA_KV):
            xh = p[:, P_KG + HD * h:P_KG + HD * (h + 1)]
            o[4][h] = _rope(xh * _rms_r(_rsum(xh * xh), HD) * g_k, cav, sav, 64, 32).astype(BF16)
            o[5][h] = p[:, P_VG + HD * h:P_VG + HD * (h + 1)].astype(BF16)
        return ()

    outs = [_sds((S, Q_LORA), BF16), _sds((S, KV_LORA), BF16), _sds((S, LANES), BF16),
            _sds((HEADS, S, HD), BF16), _sds((GQA_KV, S, HD), BF16), _sds((GQA_KV, S, HD), BF16)]
    return _rowwise(name, fn, [proj, cm, sm, ca, sa], gains, outs)


def _even_prep_bwd(name, dcqn, dckvn, dkrd, dqg, dkg, dvg, proj, tabs, gains):
    S = proj.shape[0]
    cm, sm, ca, sa = tabs

    def fn(r, c, o):
        p, cmv, smv, cav, sav = r[6], r[7][...], r[8][...], r[9][...], r[10][...]
        g_ql, g_kvl, g_kr, g_q, g_k = (x[...] for x in c)
        out = o[0]
        cq = p[:, P_CQ:P_CKV]
        rr = _rms_r(_rsum(cq * cq), Q_LORA)
        d = r[0][...]
        out[:, P_CQ:P_CKV] = _rms_bwd(cq, rr, g_ql, d, 1.0 / Q_LORA).astype(BF16)
        a_ql = _csum(d * cq * rr)
        ckv = p[:, P_CKV:P_KR]
        rr = _rms_r(_rsum(ckv * ckv), KV_LORA)
        d = r[1][...]
        out[:, P_CKV:P_KR] = _rms_bwd(ckv, rr, g_kvl, d, 1.0 / KV_LORA).astype(BF16)
        a_kvl = _csum(d * ckv * rr)
        kr = p[:, P_KR:P_QG]
        rr = _rms_r(_rsum(kr * kr), MLA_ROPE)
        d = r[2][...]
        d = d + pltpu.roll(d, 64, 1)
        d = _rope_bwd(d, cmv, smv, 64, 32)
        low = _lane(d.shape) < 64
        out[:, P_KR:P_QG] = jnp.where(low, _rms_bwd(kr, rr, g_kr, d, 1.0 / MLA_ROPE), 0.0).astype(BF16)
        a_kr = _csum(d * kr * rr)
        a_q = jnp.zeros((1, HD), F32)
        for h in range(HEADS):
            xh = p[:, P_QG + HD * h:P_QG + HD * (h + 1)]
            rr = _rms_r(_rsum(xh * xh), HD)
            d = _rope_bwd(r[3][h] * GQA_SCALE, cav, sav, 64, 32)
            out[:, P_QG + HD * h:P_QG + HD * (h + 1)] = _rms_bwd(xh, rr, g_q, d, 1.0 / HD).astype(BF16)
            a_q = a_q + _csum(d * xh * rr)
        a_k = jnp.zeros((1, HD), F32)
        grp = HEADS // GQA_KV
        for h in range(GQA_KV):
            xh = p[:, P_KG + HD * h:P_KG + HD * (h + 1)]
            rr = _rms_r(_rsum(xh * xh), HD)
            d = r[4][grp * h]
            dv = r[5][grp * h]
            for e in range(1, grp):
                d = d + r[4][grp * h + e]
                dv = dv + r[5][grp * h + e]
            d = _rope_bwd(d, cav, sav, 64, 32)
            out[:, P_KG + HD * h:P_KG + HD * (h + 1)] = _rms_bwd(xh, rr, g_k, d, 1.0 / HD).astype(BF16)
            out[:, P_VG + HD * h:P_VG + HD * (h + 1)] = dv.astype(BF16)
            a_k = a_k + _csum(d * xh * rr)
        return (a_ql, a_kvl, a_kr, a_q, a_k)

    accs = [_sds((1, Q_LORA), F32), _sds((1, KV_LORA), F32), _sds((1, LANES), F32),
            _sds((1, HD), F32), _sds((1, HD), F32)]
    return _rowwise(name, fn, [dcqn, dckvn, dkrd, dqg, dkg, dvg, proj, cm, sm, ca, sa], gains,
                    [_sds((S, P_END), BF16)], accs, tm=128)


def _mla_q_parts(qa, h, rp, g_n, g_r):
    nope = qa[:, HD * h:HD * (h + 1)]
    mine = (_lane(rp.shape) >= 64) == bool(h % 2)
    rpm = jnp.where(mine, rp, 0.0)
    rr = _rms_r(_rsum(nope * nope) + _rsum(rpm * rpm), MLA_QK)
    return nope, rpm, mine, rr


def _mla_prep(name, qa, kv, krd, tabs, gains):
    S = qa.shape[0]
    cm, sm = tabs

    def fn(r, c, o):
        qa_r, kv_r, krd_v, cmv, smv = r[0], r[1], r[2][...], r[3][...], r[4][...]
        g_n, g_r, g_kn = (x[...] for x in c)
        for h in range(HEADS):
            rp = qa_r[:, HEADS * HD + LANES * (h // 2):HEADS * HD + LANES * (h // 2 + 1)]
            nope, rpm, mine, rr = _mla_q_parts(qa_r, h, rp, g_n, g_r)
            o[0][h, :, 0:HD] = (nope * (rr * MLA_SCALE) * g_n).astype(BF16)
            o[0][h, :, HD:2 * HD] = _rope(rpm * (rr * MLA_SCALE) * g_r, cmv, smv, 64, 32).astype(BF16)
            kn = kv_r[:, 2 * HD * h:2 * HD * h + HD]
            o[1][h, :, 0:HD] = (kn * _rms_r(_rsum(kn * kn), HD) * g_kn).astype(BF16)
            o[1][h, :, HD:2 * HD] = krd_v
            o[2][h] = kv_r[:, 2 * HD * h + HD:2 * HD * (h + 1)].astype(BF16)
        return ()

    outs = [_sds((HEADS, S, 2 * HD), BF16), _sds((HEADS, S, 2 * HD), BF16), _sds((HEADS, S, HD), BF16)]
    return _rowwise(name, fn, [qa, kv, krd, cm, sm], gains, outs)


def _mla_prep_bwd(name, dq, dk, dv, qa, kv, tabs, gains):
    S = qa.shape[0]
    cm, sm = tabs

    def fn(r, c, o):
        dq_r, dk_r, dv_r, qa_r, kv_r, cmv, smv = r[0], r[1], r[2], r[3], r[4], r[5][...], r[6][...]
        g_n, g_r, g_kn = (x[...] for x in c)
        a_n = jnp.zeros((1, HD), F32)
        a_r = jnp.zeros((1, LANES), F32)
        a_kn = jnp.zeros((1, HD), F32)
        dkrd = jnp.zeros(cmv.shape, F32)
        drp = None
        for h in range(HEADS):
            rp = qa_r[:, HEADS * HD + LANES * (h // 2):HEADS * HD + LANES * (h // 2 + 1)]
            nope, rpm, mine, rr = _mla_q_parts(qa_r, h, rp, g_n, g_r)
            dn = dq_r[h, :, 0:HD] * MLA_SCALE
            dr = _rope_bwd(jnp.where(mine, dq_r[h, :, HD:2 * HD] * MLA_SCALE, 0.0), cmv, smv, 64, 32)
            dot = (_rsum(dn * g_n * nope) + _rsum(dr * g_r * rpm)) * (1.0 / MLA_QK)
            r3 = rr * rr * rr
            o[0][:, HD * h:HD * (h + 1)] = (rr * dn * g_n - nope * r3 * dot).astype(BF16)
            part = jnp.where(mine, rr * dr * g_r - rpm * r3 * dot, 0.0)
            drp = part if h % 2 == 0 else drp + part
            if h % 2 == 1:
                o[0][:, HEADS * HD + LANES * (h // 2):HEADS * HD + LANES * (h // 2 + 1)] = drp.astype(BF16)
            a_n = a_n + _csum(dn * nope * rr)
            a_r = a_r + _csum(dr * rpm * rr)
            kn = kv_r[:, 2 * HD * h:2 * HD * h + HD]
            rk = _rms_r(_rsum(kn * kn), HD)
            dkn = dk_r[h, :, 0:HD]
            o[1][:, 2 * HD * h:2 * HD * h + HD] = _rms_bwd(kn, rk, g_kn, dkn, 1.0 / HD).astype(BF16)
            o[1][:, 2 * HD * h + HD:2 * HD * (h + 1)] = dv_r[h].astype(BF16)
            a_kn = a_kn + _csum(dkn * kn * rk)
            dkrd = dkrd + dk_r[h, :, HD:2 * HD]
        o[2][...] = dkrd
        return (a_n, a_r, a_kn)

    outs = [_sds(qa.shape, BF16), _sds(kv.shape, BF16), _sds((S, LANES), F32)]
    accs = [_sds((1, HD), F32), _sds((1, LANES), F32), _sds((1, HD), F32)]
    return _rowwise(name, fn, [dq, dk, dv, qa, kv, cm, sm], gains, outs, accs, tm=128)


def _seg64_r(x):
    low = _lane(x.shape) < 64
    x2 = x * x
    s0 = _rsum(jnp.where(low, x2, 0.0))
    s1 = _rsum(jnp.where(low, 0.0, x2))
    return jnp.where(low, _rms_r(s0, SWA_D), _rms_r(s1, SWA_D)), low


def _swa_prep(name, qkv, tabs, gains):
    S = qkv.shape[0]
    nq, nk = SWA_HEADS * SWA_D, SWA_KV * SWA_D
    cs, ss = tabs

    def fn(r, c, o):
        x_r, csv, ssv = r[0], r[1][...], r[2][...]
        g_q, g_k = c[0][...], c[1][...]
        for g in range((nq + nk) // LANES):
            x = x_r[:, LANES * g:LANES * (g + 1)]
            rr, _ = _seg64_r(x)
            y = _rope(x * rr * (g_q if g < nq // LANES else g_k), csv, ssv, SWA_D, SWA_ROT // 2)
            if g < nq // LANES:
                o[0][:, LANES * g:LANES * (g + 1)] = (y * SWA_SCALE).astype(BF16)
            else:
                o[1][:, LANES * g - nq:LANES * (g + 1) - nq] = y.astype(BF16)
        o[2][...] = x_r[:, nq + nk:nq + 2 * nk].astype(BF16)
        return ()

    outs = [_sds((S, nq), BF16), _sds((S, nk), BF16), _sds((S, nk), BF16)]
    return _rowwise(name, fn, [qkv, cs, ss], gains, outs)


def _swa_prep_bwd(name, dq, dk, dv, qkv, tabs, gains):
    nq, nk = SWA_HEADS * SWA_D, SWA_KV * SWA_D
    cs, ss = tabs

    def fn(r, c, o):
        dq_r, dk_r, x_r, csv, ssv = r[0], r[1], r[3], r[4][...], r[5][...]
        g_q, g_k = c[0][...], c[1][...]
        acc = [jnp.zeros((1, LANES), F32), jnp.zeros((1, LANES), F32)]
        for g in range((nq + nk) // LANES):
            isq = g < nq // LANES
            x = x_r[:, LANES * g:LANES * (g + 1)]
            rr, low = _seg64_r(x)
            d = dq_r[:, LANES * g:LANES * (g + 1)] * SWA_SCALE if isq else dk_r[:, LANES * g - nq:LANES * (g + 1) - nq]
            d = _rope_bwd(d, csv, ssv, SWA_D, SWA_ROT // 2)
            dyg = d * (g_q if isq else g_k)
            t = dyg * x
            d0 = _rsum(jnp.where(low, t, 0.0))
            d1 = _rsum(jnp.where(low, 0.0, t))
            dot = jnp.where(low, d0, d1) * (1.0 / SWA_D)
            o[0][:, LANES * g:LANES * (g + 1)] = (rr * dyg - x * (rr * rr * rr) * dot).astype(BF16)
            acc[0 if isq else 1] = acc[0 if isq else 1] + _csum(d * x * rr)
        o[0][:, nq + nk:nq + 2 * nk] = r[2][...].astype(BF16)
        return tuple(acc)

    return _rowwise(name, fn, [dq, dk, dv, qkv, cs, ss], gains, [_sds(qkv.shape, BF16)],
                    [_sds((1, LANES), F32), _sds((1, LANES), F32)], tm=128)


def _delta(name, do, o_, width):
    S, C = do.shape
    nh = C // width

    def fn(r, c, o):
        for g in range(C // LANES):
            t = r[0][:, LANES * g:LANES * (g + 1)].astype(F32) * r[1][:, LANES * g:LANES * (g + 1)].astype(F32)
            if width == LANES:
                o[0][g] = _rsum(t)
            else:
                low = _lane(t.shape) < 64
                o[0][2 * g] = _rsum(jnp.where(low, t, 0.0))
                o[0][2 * g + 1] = _rsum(jnp.where(low, 0.0, t))
        return ()

    return _rowwise(name, fn, [do, o_], [], [_sds((nh, S, 1), F32)])[0]


_NT = (((1,), (1,)), ((), ()))
_NN = (((1,), (0,)), ((), ()))
_TN = (((0,), (0,)), ((), ()))


def _flash_fwd(name, q, k, v, tq=512, tk=4096):
    H, S, dk = q.shape
    G = H // k.shape[0]
    dv = v.shape[2]
    tq, tk = _tile(S, (tq, 256, 128)), _tile(S, (tk, 256, 128))
    nk = S // tk

    def body(q_ref, k_ref, v_ref, o_ref, lse_ref, m_s, l_s, acc_s):
        j = pl.program_id(2)

        @pl.when(j == 0)
        def _():
            m_s[...] = jnp.full_like(m_s, -jnp.inf)
            l_s[...] = jnp.zeros_like(l_s)
            acc_s[...] = jnp.zeros_like(acc_s)

        s = lax.dot_general(q_ref[...], k_ref[...], _NT, preferred_element_type=F32)
        m_new = jnp.maximum(m_s[...], jnp.max(s, axis=-1, keepdims=True))
        alpha = jnp.exp(m_s[...] - m_new)
        p = jnp.exp(s - m_new)
        l_s[...] = alpha * l_s[...] + _rsum(p)
        acc_s[...] = alpha * acc_s[...] + lax.dot_general(p.astype(BF16), v_ref[...], _NN,
                                                          preferred_element_type=F32)
        m_s[...] = m_new

        @pl.when(j == nk - 1)
        def _():
            o_ref[...] = (acc_s[...] / l_s[...]).astype(o_ref.dtype)
            lse_ref[...] = m_s[...] + jnp.log(l_s[...])

    return pl.pallas_call(
        body, name=name, grid=(H, S // tq, nk),
        in_specs=[pl.BlockSpec((None, tq, dk), lambda h, i, j: (h, i, 0)),
                  pl.BlockSpec((None, tk, dk), lambda h, i, j: (h // G, j, 0)),
                  pl.BlockSpec((None, tk, dv), lambda h, i, j: (h // G, j, 0))],
        out_specs=[pl.BlockSpec((tq, dv), lambda h, i, j: (i, h)),
                   pl.BlockSpec((None, tq, 1), lambda h, i, j: (h, i, 0))],
        out_shape=[_sds((S, H * dv), BF16), _sds((H, S, 1), F32)],
        scratch_shapes=[pltpu.VMEM((tq, 1), F32), pltpu.VMEM((tq, 1), F32), pltpu.VMEM((tq, dv), F32)],
        compiler_params=_params(("parallel", "parallel", "arbitrary")),
    )(q, k, v)


def _flash_bwd(name, q, k, v, do, head0, lse, delta, tq=1024, tk=1024):
    H, S, dk = q.shape
    G = H // k.shape[0]
    dv = v.shape[2]
    tq, tk = _tile(S, (tq, 256, 128)), _tile(S, (tk, 256, 128))
    nq = S // tq

    def body(q_ref, k_ref, v_ref, do_ref, lse_ref, dl_ref, dq_ref, dk_ref, dv_ref, dk_s, dv_s):
        j, i = pl.program_id(1), pl.program_id(2)

        @pl.when(i == 0)
        def _():
            dk_s[...] = jnp.zeros_like(dk_s)
            dv_s[...] = jnp.zeros_like(dv_s)

        qv, kv_, dov = q_ref[...], k_ref[...], do_ref[...]
        s = lax.dot_general(qv, kv_, _NT, preferred_element_type=F32)
        p = jnp.exp(s - lse_ref[...])
        dp = lax.dot_general(dov, v_ref[...], _NT, preferred_element_type=F32)
        ds = (p * (dp - dl_ref[...])).astype(BF16)
        dv_s[...] += lax.dot_general(p.astype(BF16), dov, _TN, preferred_element_type=F32)
        dk_s[...] += lax.dot_general(ds, qv, _TN, preferred_element_type=F32)
        dqi = lax.dot_general(ds, kv_, _NN, preferred_element_type=F32)
        rows = pl.ds(pl.multiple_of(i * tq, tq), tq)

        @pl.when(j == 0)
        def _():
            dq_ref[rows, :] = dqi

        @pl.when(j > 0)
        def _():
            dq_ref[rows, :] += dqi

        @pl.when(i == nq - 1)
        def _():
            dk_ref[...] = dk_s[...]
            dv_ref[...] = dv_s[...]

    return pl.pallas_call(
        body, name=name, grid=(H, S // tk, nq),
        in_specs=[pl.BlockSpec((None, tq, dk), lambda h, j, i: (h, i, 0)),
                  pl.BlockSpec((None, tk, dk), lambda h, j, i: (h // G, j, 0)),
                  pl.BlockSpec((None, tk, dv), lambda h, j, i: (h // G, j, 0)),
                  pl.BlockSpec((tq, dv), lambda h, j, i: (i, head0 + h)),
                  pl.BlockSpec((None, tq, 1), lambda h, j, i: (head0 + h, i, 0)),
                  pl.BlockSpec((None, tq, 1), lambda h, j, i: (head0 + h, i, 0))],
        out_specs=[pl.BlockSpec((None, S, dk), lambda h, j, i: (h, 0, 0)),
                   pl.BlockSpec((None, tk, dk), lambda h, j, i: (h, j, 0)),
                   pl.BlockSpec((None, tk, dv), lambda h, j, i: (h, j, 0))],
        out_shape=[_sds((H, S, dk), F32), _sds((H, S, dk), F32), _sds((H, S, dv), F32)],
        scratch_shapes=[pltpu.VMEM((tk, dk), F32), pltpu.VMEM((tk, dv), F32)],
        compiler_params=_params(("parallel", "arbitrary", "arbitrary")),
    )(q, k, v, do, lse, delta)


def _swa_place(ref128, h):
    e, t = h % 2, (h // (SWA_HEADS // SWA_KV)) % 2
    x = ref128.astype(F32)
    if e != t:
        x = pltpu.roll(x, 64, 1)
    return jnp.where((_lane(x.shape) >= 64) == bool(t), x, 0.0).astype(BF16)


def _swa_unplace(y, h):
    e, t = h % 2, (h // (SWA_HEADS // SWA_KV)) % 2
    return pltpu.roll(y, 64, 1) if e != t else y


def _swa_specs(width, nb):
    prev = pl.BlockSpec((SWA_BLOCK, width), lambda i: (jnp.maximum(i - 1, 0), 0))
    cur = pl.BlockSpec((SWA_BLOCK, width), lambda i: (i, 0))
    nxt = pl.BlockSpec((SWA_BLOCK, width), lambda i: (jnp.minimum(i + 1, nb - 1), 0))
    return [prev, cur, nxt]


def _swa_specs3(nh, nb):
    prev = pl.BlockSpec((nh, SWA_BLOCK, 1), lambda i: (0, jnp.maximum(i - 1, 0), 0))
    cur = pl.BlockSpec((nh, SWA_BLOCK, 1), lambda i: (0, i, 0))
    nxt = pl.BlockSpec((nh, SWA_BLOCK, 1), lambda i: (0, jnp.minimum(i + 1, nb - 1), 0))
    return [prev, cur, nxt]


def _swa_valid_q(i, S):
    qpos = i * SWA_BLOCK + lax.broadcasted_iota(jnp.int32, (SWA_BLOCK, 3 * SWA_BLOCK), 0)
    kpos = (i - 1) * SWA_BLOCK + lax.broadcasted_iota(jnp.int32, (SWA_BLOCK, 3 * SWA_BLOCK), 1)
    return (jnp.abs(qpos - kpos) <= SWA_WINDOW) & (kpos >= 0) & (kpos < S)


def _swa_fwd(name, q, k, v, sink):
    S = q.shape[0]
    nb = S // SWA_BLOCK
    grp = SWA_HEADS // SWA_KV
    smem = pl.BlockSpec(memory_space=pltpu.SMEM)

    def body(sink_ref, q_ref, kp, kc, kn, vp, vc, vn, o_ref, lse_ref):
        i = pl.program_id(0)
        valid = _swa_valid_q(i, S)
        kcat = [jnp.concatenate([r[:, LANES * u:LANES * (u + 1)] for r in (kp, kc, kn)], axis=0) for u in range(2)]
        vcat = [jnp.concatenate([r[:, LANES * u:LANES * (u + 1)] for r in (vp, vc, vn)], axis=0) for u in range(2)]
        for g in range(SWA_HEADS // 2):
            pair = []
            for h in (2 * g, 2 * g + 1):
                u = (h // grp) // 2
                xq = _swa_place(q_ref[:, LANES * g:LANES * (g + 1)], h)
                s = lax.dot_general(xq, kcat[u], _NT, preferred_element_type=F32)
                s = jnp.where(valid, s, -jnp.inf)
                sk = sink_ref[0, h]
                m = jnp.maximum(jnp.max(s, axis=-1, keepdims=True), sk)
                p = jnp.exp(s - m)
                den = _rsum(p) + jnp.exp(sk - m)
                oh = lax.dot_general((p / den).astype(BF16), vcat[u], _NN, preferred_element_type=F32)
                lse_ref[h] = m + jnp.log(den)
                pair.append(_swa_unplace(oh, h))
            o_ref[:, LANES * g:LANES * (g + 1)] = jnp.where(_lane(pair[0].shape) < 64, pair[0], pair[1]).astype(BF16)

    return pl.pallas_call(
        body, name=name, grid=(nb,),
        in_specs=[smem, pl.BlockSpec((SWA_BLOCK, q.shape[1]), lambda i: (i, 0))]
        + _swa_specs(k.shape[1], nb) + _swa_specs(v.shape[1], nb),
        out_specs=[pl.BlockSpec((SWA_BLOCK, q.shape[1]), lambda i: (i, 0)),
                   pl.BlockSpec((SWA_HEADS, SWA_BLOCK, 1), lambda i: (0, i, 0))],
        out_shape=[_sds(q.shape, BF16), _sds((SWA_HEADS, S, 1), F32)],
        compiler_params=_params(("parallel",)),
    )(sink, q, k, k, k, v, v, v)


def _swa_bwd_q(name, q, k, v, do, lse, delta, sink):
    S = q.shape[0]
    nb = S // SWA_BLOCK
    grp = SWA_HEADS // SWA_KV
    smem = pl.BlockSpec(memory_space=pltpu.SMEM)
    row = pl.BlockSpec((SWA_BLOCK, q.shape[1]), lambda i: (i, 0))
    hrow = pl.BlockSpec((SWA_HEADS, SWA_BLOCK, 1), lambda i: (0, i, 0))

    def body(sink_ref, q_ref, do_ref, lse_ref, dl_ref, kp, kc, kn, vp, vc, vn, dq_ref, ds_ref):
        i = pl.program_id(0)

        @pl.when(i == 0)
        def _():
            ds_ref[...] = jnp.zeros_like(ds_ref)

        valid = _swa_valid_q(i, S)
        kcat = [jnp.concatenate([r[:, LANES * u:LANES * (u + 1)] for r in (kp, kc, kn)], axis=0) for u in range(2)]
        vcat = [jnp.concatenate([r[:, LANES * u:LANES * (u + 1)] for r in (vp, vc, vn)], axis=0) for u in range(2)]
        for g in range(SWA_HEADS // 2):
            pair = []
            for h in (2 * g, 2 * g + 1):
                u = (h // grp) // 2
                xq = _swa_place(q_ref[:, LANES * g:LANES * (g + 1)], h)
                xdo = _swa_place(do_ref[:, LANES * g:LANES * (g + 1)], h)
                lse_h, dl_h = lse_ref[h], dl_ref[h]
                s = lax.dot_general(xq, kcat[u], _NT, preferred_element_type=F32)
                p = jnp.exp(jnp.where(valid, s, -jnp.inf) - lse_h)
                dp = lax.dot_general(xdo, vcat[u], _NT, preferred_element_type=F32)
                dsv = (p * (dp - dl_h)).astype(BF16)
                dqh = lax.dot_general(dsv, kcat[u], _NN, preferred_element_type=F32)
                pair.append(_swa_unplace(dqh, h))
                dsink = -_csum(jnp.exp(sink_ref[0, h] - lse_h) * dl_h)
                ds_ref[h:h + 1, :] += jnp.broadcast_to(dsink, (1, LANES))
            dq_ref[:, LANES * g:LANES * (g + 1)] = jnp.where(_lane(pair[0].shape) < 64, pair[0], pair[1])

    return pl.pallas_call(
        body, name=name, grid=(nb,),
        in_specs=[smem, row, row, hrow, hrow] + _swa_specs(k.shape[1], nb) + _swa_specs(v.shape[1], nb),
        out_specs=[row, pl.BlockSpec((SWA_HEADS, LANES), lambda i: (0, 0))],
        out_shape=[_sds(q.shape, F32), _sds((SWA_HEADS, LANES), F32)],
        compiler_params=_params(("arbitrary",)),
    )(sink, q, do, lse, delta, k, k, k, v, v, v)


def _swa_bwd_kv(name, q, k, v, do, lse, delta):
    S = q.shape[0]
    nb = S // SWA_BLOCK
    grp = SWA_HEADS // SWA_KV
    krow = pl.BlockSpec((SWA_BLOCK, k.shape[1]), lambda i: (i, 0))

    def body(qp, qc, qn, dop, doc, don, lp, lc, ln, dp_, dc_, dn_, k_ref, v_ref, dk_ref, dv_ref):
        j = pl.program_id(0)
        qpos = (j - 1) * SWA_BLOCK + lax.broadcasted_iota(jnp.int32, (3 * SWA_BLOCK, SWA_BLOCK), 0)
        kpos = j * SWA_BLOCK + lax.broadcasted_iota(jnp.int32, (3 * SWA_BLOCK, SWA_BLOCK), 1)
        valid = (jnp.abs(qpos - kpos) <= SWA_WINDOW) & (qpos >= 0) & (qpos < S)
        for u in range(SWA_KV // 2):
            kc = k_ref[:, LANES * u:LANES * (u + 1)]
            vc = v_ref[:, LANES * u:LANES * (u + 1)]
            dk_a = jnp.zeros((SWA_BLOCK, LANES), F32)
            dv_a = jnp.zeros((SWA_BLOCK, LANES), F32)
            for h in range(2 * u * grp, (2 * u + 2) * grp):
                g = h // 2
                xq = jnp.concatenate([_swa_place(r[:, LANES * g:LANES * (g + 1)], h) for r in (qp, qc, qn)], axis=0)
                xdo = jnp.concatenate([_swa_place(r[:, LANES * g:LANES * (g + 1)], h) for r in (dop, doc, don)], axis=0)
                lse_h = jnp.concatenate([lp[h], lc[h], ln[h]], axis=0)
                dl_h = jnp.concatenate([dp_[h], dc_[h], dn_[h]], axis=0)
                s = lax.dot_general(xq, kc, _NT, preferred_element_type=F32)
                p = jnp.where(valid, jnp.exp(s - lse_h), 0.0)
                dpv = lax.dot_general(xdo, vc, _NT, preferred_element_type=F32)
                dsv = (p * (dpv - dl_h)).astype(BF16)
                dv_a = dv_a + lax.dot_general(p.astype(BF16), xdo, _TN, preferred_element_type=F32)
                dk_a = dk_a + lax.dot_general(dsv, xq, _TN, preferred_element_type=F32)
            dk_ref[:, LANES * u:LANES * (u + 1)] = dk_a
            dv_ref[:, LANES * u:LANES * (u + 1)] = dv_a

    return pl.pallas_call(
        body, name=name, grid=(nb,),
        in_specs=_swa_specs(q.shape[1], nb) + _swa_specs(do.shape[1], nb)
        + _swa_specs3(SWA_HEADS, nb) + _swa_specs3(SWA_HEADS, nb) + [krow, krow],
        out_specs=[krow, krow],
        out_shape=[_sds(k.shape, F32), _sds(v.shape, F32)],
        compiler_params=_params(("parallel",)),
    )(q, q, q, do, do, do, lse, lse, lse, delta, delta, delta, k, v)


def _exchange(name, tensors):
    n = len(tensors)
    outs = [_sds(a.shape if sc else (N_DEV,) + a.shape, a.dtype) for a, sc in tensors]

    def body(*refs):
        ins, dst, send, recv, loc = refs[:n], refs[n:2 * n], refs[2 * n], refs[2 * n + 1], refs[2 * n + 2]
        x, y, c = lax.axis_index("x"), lax.axis_index("y"), lax.axis_index("c")
        me = 4 * x + 2 * y + c

        def peer(kk):
            return (x ^ ((kk >> 2) & 1), y ^ ((kk >> 1) & 1), c ^ (kk & 1))

        def copy(t, kk):
            px, py, pc = peer(kk)
            src = ins[t].at[4 * px + 2 * py + pc] if tensors[t][1] else ins[t]
            return pltpu.make_async_remote_copy(src_ref=src, dst_ref=dst[t].at[me], send_sem=send.at[t, kk - 1],
                                                recv_sem=recv.at[t, kk - 1], device_id=(px, py, pc),
                                                device_id_type=pl.DeviceIdType.MESH)

        def arrival(t, kk):
            px, py, pc = peer(kk)
            src = ins[t].at[me] if tensors[t][1] else ins[t]
            return pltpu.make_async_remote_copy(src_ref=src, dst_ref=dst[t].at[4 * px + 2 * py + pc],
                                                send_sem=send.at[t, kk - 1], recv_sem=recv.at[t, kk - 1],
                                                device_id=(px, py, pc), device_id_type=pl.DeviceIdType.MESH)

        own = [pltpu.make_async_copy(ins[t].at[me] if tensors[t][1] else ins[t], dst[t].at[me], loc.at[t])
               for t in range(n)]
        sent = [copy(t, kk) for t in range(n) for kk in range(1, N_DEV)]
        for cp in own + sent:
            cp.start()
        for t in range(n):
            for kk in range(1, N_DEV):
                arrival(t, kk).wait_recv()
        for cp in sent:
            cp.wait_send()
        for cp in own:
            cp.wait()

    any_spec = pl.BlockSpec(memory_space=pl.ANY)
    return pl.pallas_call(
        body, name=name, in_specs=[any_spec] * n, out_specs=[any_spec] * n, out_shape=outs,
        scratch_shapes=[pltpu.SemaphoreType.DMA((n, N_DEV - 1)), pltpu.SemaphoreType.DMA((n, N_DEV - 1)),
                        pltpu.SemaphoreType.DMA((n,))],
        compiler_params=pltpu.CompilerParams(has_side_effects=True),
    )(*[a for a, _ in tensors])


def _peer_of(x, y, c, kk):
    return (x ^ ((kk >> 2) & 1), y ^ ((kk >> 1) & 1), c ^ (kk & 1))


def _own_slot(a, scatter):
    me = 4 * lax.axis_index("x") + 2 * lax.axis_index("y") + lax.axis_index("c")
    shape = a.shape if scatter else (N_DEV,) + a.shape
    own = lax.dynamic_slice_in_dim(a, me, 1, 0) if scatter else a[None]
    return lax.dynamic_update_slice_in_dim(lax.empty(shape, a.dtype), own, me, 0)


def _exchange_start(name, tensors):
    n = len(tensors)
    hbm = pl.BlockSpec(memory_space=pltpu.HBM)
    sem = pl.BlockSpec(memory_space=pltpu.SEMAPHORE)
    srcs = [pltpu.with_memory_space_constraint(a, pltpu.HBM) for a, _ in tensors]
    lands = [pltpu.with_memory_space_constraint(_own_slot(a, sc), pltpu.HBM) for a, sc in tensors]

    def body(*refs):
        ins, dst = refs[:n], refs[n:2 * n]
        send, recv, token = refs[2 * n], refs[2 * n + 1], refs[4 * n + 2]
        x, y, c = lax.axis_index("x"), lax.axis_index("y"), lax.axis_index("c")
        me = 4 * x + 2 * y + c
        for t in range(n):
            for kk in range(1, N_DEV):
                px, py, pc = _peer_of(x, y, c, kk)
                src = ins[t].at[4 * px + 2 * py + pc] if tensors[t][1] else ins[t]
                k1 = t * (N_DEV - 1) + kk - 1
                pltpu.make_async_remote_copy(src_ref=src, dst_ref=dst[t].at[me], send_sem=send.at[k1],
                                             recv_sem=recv.at[k1], device_id=(px, py, pc),
                                             device_id_type=pl.DeviceIdType.MESH).start()
        token[...] = jnp.zeros_like(token)

    out = pl.pallas_call(
        body, name=name,
        in_specs=[hbm] * (2 * n),
        out_specs=[sem, sem] + [hbm] * (2 * n) + [pl.BlockSpec(memory_space=pltpu.VMEM)],
        out_shape=[pltpu.SemaphoreType.DMA((n * (N_DEV - 1),)), pltpu.SemaphoreType.DMA((n * (N_DEV - 1),))]
        + [pltpu.HBM(a.shape, a.dtype) for a in srcs] + [pltpu.HBM(a.shape, a.dtype) for a in lands]
        + [_sds((8, LANES), F32)],
        input_output_aliases={i: i + 2 for i in range(2 * n)},
        compiler_params=pltpu.CompilerParams(has_side_effects=pltpu.SideEffectType.DATAFLOW_SIDE_EFFECTING),
    )(*srcs, *lands)
    return (out[0], out[1], out[2:2 + n], out[2 + n:2 + 2 * n], [sc for _, sc in tensors]), out[2 + 2 * n]


def _exchange_wait(name, started, after):
    send_s, recv_s, srcs, lands, flags = started
    n = len(srcs)
    hbm = pl.BlockSpec(memory_space=pltpu.HBM)
    sem = pl.BlockSpec(memory_space=pltpu.SEMAPHORE)

    def body(*refs):
        ins, dst, send, recv = refs[:n], refs[n:2 * n], refs[2 * n], refs[2 * n + 1]
        x, y, c = lax.axis_index("x"), lax.axis_index("y"), lax.axis_index("c")
        me = 4 * x + 2 * y + c
        for t in range(n):
            for kk in range(1, N_DEV):
                px, py, pc = _peer_of(x, y, c, kk)
                src = ins[t].at[me] if flags[t] else ins[t]
                k1 = t * (N_DEV - 1) + kk - 1
                cp = pltpu.make_async_remote_copy(src_ref=src, dst_ref=dst[t].at[4 * px + 2 * py + pc],
                                                  send_sem=send.at[k1], recv_sem=recv.at[k1],
                                                  device_id=(px, py, pc), device_id_type=pl.DeviceIdType.MESH)
                cp.wait_send()
                cp.wait_recv()

    out = pl.pallas_call(
        body, name=name,
        in_specs=[hbm] * (2 * n) + [sem, sem, pl.BlockSpec(memory_space=pl.ANY)],
        out_specs=[hbm] * (2 * n),
        out_shape=[pltpu.HBM(a.shape, a.dtype) for a in srcs] + [pltpu.HBM(a.shape, a.dtype) for a in lands],
        input_output_aliases={i: i for i in range(2 * n)},
        compiler_params=pltpu.CompilerParams(has_side_effects=pltpu.SideEffectType.DATAFLOW_SIDE_EFFECTING),
    )(*srcs, *lands, send_s, recv_s, after)
    return out[n:]


def _sum8(name, parts):
    _, R, C = parts.shape

    def fn(r, c, o):
        acc = r[0][0].astype(F32)
        for s in range(1, N_DEV):
            acc = acc + r[0][s].astype(F32)
        o[0][...] = acc
        return ()

    tm = R if R % 256 else 256
    return _rowwise(name, fn, [parts], [], [_sds((R, C), F32)], tm=tm)[0]


def _adamw(name, w, g, m, v):
    bc1 = 1.0 - ADAM_B1 ** ADAM_STEP
    bc2 = 1.0 - ADAM_B2 ** ADAM_STEP

    def fn(r, c, o):
        wv, gv, mv, vv = (x[...] for x in r)
        mn = ADAM_B1 * mv + (1.0 - ADAM_B1) * gv
        vn = ADAM_B2 * vv + (1.0 - ADAM_B2) * (gv * gv)
        o[0][...] = -ADAM_LR * ((mn / bc1) / (jnp.sqrt(vn / bc2) + ADAM_EPS) + ADAM_WD * wv)
        o[1][...] = mn
        o[2][...] = vn
        return ()

    tm = max(8, min(512, 1 << ((16 << 20) // (56 * w.shape[1])).bit_length() - 1))
    return _rowwise(name, fn, [w, g, m, v], [], [_sds(w.shape, F32)] * 3, tm=tm)


def _rope_cs(pos, dim, theta):
    inv = jnp.float32(theta) ** (-jnp.arange(0, dim, 2, dtype=jnp.float32) / dim)
    ang = pos.astype(jnp.float32)[:, None] * inv[None, :]
    return jnp.cos(ang), jnp.sin(ang)


def _tables(S):
    pos = jnp.arange(S)
    c, s = _rope_cs(pos, MLA_ROPE, ROPE_THETA)
    mla = (jnp.concatenate([c, c, c, c], 1), jnp.concatenate([-s, s, -s, s], 1))
    rc, rs = _rope_cs(pos // GRID_W, HD // 2, AXIAL_THETA)
    cc, cs = _rope_cs(pos % GRID_W, HD // 2, AXIAL_THETA)
    axial = (jnp.concatenate([rc, rc, cc, cc], 1), jnp.concatenate([-rs, rs, -cs, cs], 1))
    c, s = _rope_cs(pos, SWA_ROT, ROPE_THETA)
    one, zero = jnp.ones((S, SWA_D - SWA_ROT), F32), jnp.zeros((S, SWA_D - SWA_ROT), F32)
    swa = (jnp.concatenate([c, c, one, c, c, one], 1), jnp.concatenate([-s, s, zero, -s, s, zero], 1))
    return mla, axial, swa


_WEIGHTS = ['even_norm', 'even_w_in', 'mla_q_lat_norm', 'mla_kv_lat_norm', 'mla_w_uq', 'mla_w_ukv', 'mla_q_norm',
            'mla_k_nope_norm', 'mla_k_rope_norm', 'gqa_q_norm', 'gqa_k_norm', 'even_w_out', 'odd_norm', 'odd_w_qkv',
            'swa_q_norm', 'swa_k_norm', 'swa_sink', 'odd_w_out', 'mlp_norm', 'mlp_w_up', 'mlp_w_down']
_SMALL = ['even_norm', 'mla_q_lat_norm', 'mla_kv_lat_norm', 'mla_q_norm', 'mla_k_nope_norm', 'mla_k_rope_norm',
          'gqa_q_norm', 'gqa_k_norm', 'swa_q_norm', 'swa_k_norm', 'swa_sink', 'mlp_norm']


def _relu2(acc):
    rl = jnp.maximum(acc, 0.0)
    return rl * rl, rl


def _mul2(acc, rl):
    return (acc * (2.0 * rl.astype(F32)),)


def _add(acc, res):
    return (acc + res,)


def _step(x, tgt, w, m, v):
    S, D = x.shape
    nw = len(_WEIGHTS)
    tab_mla, tab_ax, tab_swa = _tables(S)
    bf = lambda a: a.astype(BF16)

    w_up_s, w_dn_s = w['mlp_w_up'], w['mlp_w_down']
    gathered = _exchange("gather_first", [
        (bf(w['even_w_in'][0].T), False), (bf(w['mla_w_uq'][0].T), False), (bf(w['mla_w_ukv'][0].T), False)])
    zero = jnp.minimum(jnp.abs(gathered[2][0, 0:1, 0:1].astype(F32)), 0.0)
    later, tok = _exchange_start("gather_rest_start", [
        (bf(w['even_w_out'][0]), False), (bf(w['odd_w_qkv'][0].T), False), (bf(w['odd_w_out'][0]), False),
        (bf(w_up_s[0].T), False), (bf(w_up_s[1].T), False), (bf(w_dn_s[0]), False), (bf(w_dn_s[1]), False),
        (w['odd_norm'] + zero, False)])
    flat = lambda a: a.reshape((a.shape[0] * a.shape[1],) + a.shape[2:])
    win_t, wuq_g, wukv_t = [flat(a) for a in gathered]
    win_t = jnp.concatenate([win_t[:KR_END], jnp.zeros((LANES - MLA_ROPE, D), BF16), win_t[KR_END:]], 0)
    wuq_g = gathered[1]
    wuq_t = jnp.concatenate([wuq_g[:, :HD].reshape(HEADS * HD, Q_LORA),
                             wuq_g[:, HD:].reshape(HEADS * MLA_ROPE, Q_LORA)], 0)

    z64 = jnp.zeros((1, 64), F32)
    qn = w['mla_q_norm']
    g_even = [w['mla_q_lat_norm'], w['mla_kv_lat_norm'], jnp.concatenate([w['mla_k_rope_norm'], z64], 1),
              w['gqa_q_norm'], w['gqa_k_norm']]
    g_mla = [qn[:, :HD], jnp.concatenate([qn[:, HD:], qn[:, HD:]], 1), w['mla_k_nope_norm']]
    g_swa = [jnp.concatenate([w['swa_q_norm']] * 2, 1), jnp.concatenate([w['swa_k_norm']] * 2, 1)]

    def mlp_fwd(l, xin):
        hn = _rmsnorm(f"mlp{l}_norm", xin, w['mlp_norm'][l:l + 1])
        a, rl = _mm(f"mlp{l}_up", hn, wup_t[l], "nt", (BF16, BF16), epilogue=_relu2)
        xout = _mm(f"mlp{l}_down", a, wdn[l], "nn", (F32,), epilogue=_add, extras=(xin,))
        return xout, (hn, a, rl)

    h0 = _rmsnorm("even_norm", x, w['even_norm'] + tok[0:1, 0:1])
    proj = _mm("even_in", h0, win_t, "nt", (F32,), tm=512, tn=P_END)
    cqn, ckvn, krd, qg, kg, vg = _even_prep("even_prep", proj, tab_mla + tab_ax, g_even)
    qa = _mm("mla_uq", cqn, wuq_t, "nt", (F32,))
    kv = _mm("mla_ukv", ckvn, wukv_t, "nt", (F32,))
    q_a, k_a, v_a = _mla_prep("mla_prep", qa, kv, krd, tab_mla, g_mla)
    o_a, lse_a = _flash_fwd("mla_attn", q_a, k_a, v_a)
    o_g, lse_g = _flash_fwd("gqa_attn", qg, kg, vg)
    merged = jnp.concatenate([o_a, o_g], 1)
    rest = _exchange_wait("gather_rest_wait", later, merged)
    wout_e, wqkv_t, wout_o, wup0_t, wup1_t, wdn0, wdn1 = [flat(a) for a in rest[:7]]
    odd_norm = rest[7].reshape(1, D)
    wup_t, wdn = (wup0_t, wup1_t), (wdn0, wdn1)
    x1 = _mm("even_out", merged, wout_e, "nn", (F32,), epilogue=_add, extras=(x,))
    x2, mlp0 = mlp_fwd(0, x1)

    h1 = _rmsnorm("odd_norm", x2, odd_norm)
    qkv = _mm("odd_qkv", h1, wqkv_t, "nt", (F32,))
    q_s, k_s, v_s = _swa_prep("swa_prep", qkv, tab_swa, g_swa)
    o_s, lse_s = _swa_fwd("swa_attn", q_s, k_s, v_s, w['swa_sink'])
    x3 = _mm("odd_out", o_s, wout_o, "nn", (F32,), epilogue=_add, extras=(x2,))
    x4, mlp1 = mlp_fwd(1, x3)

    dy, loss_acc = _loss("loss", x4, tgt)
    loss = lax.psum(0.5 / D * loss_acc[0, 0], ("x", "y", "c"))

    gsm = {}

    def mlp_bwd(l, dout, xin, saved):
        hn, a, rl = saved
        du = _mm(f"mlp{l}_dact", dout, wdn[l], "nt", (BF16,), epilogue=_mul2, extras=(rl,))
        g_dn = _mm(f"mlp{l}_gdown", a, dout, "tn", (BF16,))
        g_up = _mm(f"mlp{l}_gup", du, hn, "tn", (BF16,))
        dhn = _mm(f"mlp{l}_dnorm", du, wup_t[l], "nn", (F32,))
        din, g_n = _rmsnorm_bwd(f"mlp{l}_norm_bwd", dout, xin, dhn, w['mlp_norm'][l:l + 1])
        return din, g_dn, g_up, g_n

    dx3, g_dn1, g_up1, g_mn1 = mlp_bwd(1, dy, x3, mlp1)
    split = lambda a: a.reshape((N_DEV, a.shape[0] // N_DEV) + a.shape[1:])
    sc1, tok1 = _exchange_start("scatter_mlp1_start", [(split(g_up1), True), (split(g_dn1), True)])
    sink = w['swa_sink'] + tok1[0:1, 0:SWA_HEADS]

    g_wout_o = _mm("odd_gout", o_s, dx3, "tn", (BF16,))
    do_s = _mm("odd_dattn", dx3, wout_o, "nt", (BF16,))
    dl_s = _delta("swa_delta", do_s, o_s, SWA_D)
    dq_s, dsink = _swa_bwd_q("swa_bwd_q", q_s, k_s, v_s, do_s, lse_s, dl_s, sink)
    dk_s, dv_s = _swa_bwd_kv("swa_bwd_kv", q_s, k_s, v_s, do_s, lse_s, dl_s)
    dqkv, g_sq, g_sk = _swa_prep_bwd("swa_prep_bwd", dq_s, dk_s, dv_s, qkv, tab_swa, g_swa)
    g_wqkv = _mm("odd_gqkv", dqkv, h1, "tn", (BF16,))
    dh1 = _mm("odd_dnorm", dqkv, wqkv_t, "nn", (F32,))
    dx2, g_on = _rmsnorm_bwd("odd_norm_bwd", dx3, x2, dh1, odd_norm)
    gsm['swa_q_norm'] = g_sq[:, :64] + g_sq[:, 64:]
    gsm['swa_k_norm'] = g_sk[:, :64] + g_sk[:, 64:]
    gsm['swa_sink'] = dsink[:, 0].reshape(1, SWA_HEADS)

    dx1, g_dn0, g_up0, g_mn0 = mlp_bwd(0, dx2, x1, mlp0)
    sc2, tok2 = _exchange_start("scatter_mid_start", [(split(g_wout_o), True), (split(g_wqkv), True),
                                                      (split(g_up0), True), (split(g_dn0), True)])
    gsm['mlp_norm'] = jnp.concatenate([g_mn0, g_mn1], 0)

    g_wout_e = _mm("even_gout", merged, dx1, "tn", (BF16,))
    dmerged = _mm("even_dattn", dx1, wout_e, "nt", (BF16,))
    dl_e = _delta("even_delta", dmerged, merged, HD)
    lse_e = jnp.concatenate([lse_a, lse_g], 0) + tok2[0, 0]
    dq_a, dk_a, dv_a = _flash_bwd("mla_attn_bwd", q_a, k_a, v_a, dmerged, 0, lse_e, dl_e)
    dq_g, dk_g, dv_g = _flash_bwd("gqa_attn_bwd", qg, kg, vg, dmerged, HEADS, lse_e, dl_e)
    dqa, dkv, dkrd, g_qnn, g_qnr, g_kn = _mla_prep_bwd("mla_prep_bwd", dq_a, dk_a, dv_a, qa, kv, tab_mla, g_mla)
    g_wuq = _mm("mla_guq", dqa, cqn, "tn", (BF16,))
    dcqn = _mm("mla_dq_lat", dqa, wuq_t, "nn", (F32,))
    g_wukv = _mm("mla_gukv", dkv, ckvn, "tn", (BF16,))
    dckvn = _mm("mla_dkv_lat", dkv, wukv_t, "nn", (F32,))
    dproj, g_ql, g_kvl, g_kr, g_gq, g_gk = _even_prep_bwd("even_prep_bwd", dcqn, dckvn, dkrd, dq_g, dk_g, dv_g,
                                                          proj, tab_mla + tab_ax, g_even)
    g_win = _mm("even_gin", dproj, h0, "tn", (BF16,), tm=P_END, tk=1024)
    dh0 = _mm("even_dnorm", dproj, win_t, "nn", (F32,), tk=P_END)
    grad_x, g_en = _rmsnorm_bwd("even_norm_bwd", dx1, x, dh0, w['even_norm'])
    gsm.update(even_norm=g_en, mla_q_lat_norm=g_ql, mla_kv_lat_norm=g_kvl,
               mla_q_norm=jnp.concatenate([g_qnn, g_qnr[:, :64] + g_qnr[:, 64:]], 1), mla_k_nope_norm=g_kn,
               mla_k_rope_norm=g_kr[:, :64], gqa_q_norm=g_gq, gqa_k_norm=g_gk)

    g_win = jnp.concatenate([g_win[:KR_END], g_win[P_QG:]], 0)
    g_wuq = jnp.concatenate([g_wuq[:HEADS * HD].reshape(HEADS, HD, Q_LORA),
                             g_wuq[HEADS * HD:].reshape(HEADS, MLA_ROPE, Q_LORA)], 1)
    small_sizes = [w[n].size for n in _SMALL] + [D]
    small_vec = jnp.concatenate([gsm[n].reshape(1, -1) for n in _SMALL] + [g_on], 1)
    pad = (-small_vec.shape[1]) % LANES
    small_vec = jnp.pad(small_vec, ((0, 0), (0, pad)))
    last_st, tok3 = _exchange_start("scatter_last_start", [(split(g_win), True), (g_wuq, True), (split(g_wukv), True),
                                                           (split(g_wout_e), True), (small_vec, False)])
    p_up1, p_dn1 = _exchange_wait("scatter_mlp1_wait", sc1, tok3)
    p_wout_o, p_wqkv, p_up0, p_dn0 = _exchange_wait("scatter_mid_wait", sc2, tok3)
    parts = dict(odd_w_qkv=p_wqkv, odd_w_out=p_wout_o, up0=p_up0, up1=p_up1, dn0=p_dn0, dn1=p_dn1)
    red = {n: _sum8("sum_" + n, p) for n, p in parts.items()}
    grads = {
        'odd_w_qkv': red['odd_w_qkv'].T[None], 'odd_w_out': red['odd_w_out'][None],
        'mlp_w_up': jnp.stack([red['up0'].T, red['up1'].T]), 'mlp_w_down': jnp.stack([red['dn0'], red['dn1']]),
    }
    delta, new_m, new_v = {}, {}, {}

    def apply(names):
        for n in names:
            shp = w[n].shape
            two = lambda a: a.reshape(shp[0] * shp[1], shp[2])
            d_, m_, v_ = _adamw("adamw_" + n, two(w[n]), two(grads[n]), two(m[n]), two(v[n]))
            delta[n], new_m[n], new_v[n] = d_.reshape(shp), m_.reshape(shp), v_.reshape(shp)

    apply(['odd_w_qkv', 'odd_w_out', 'mlp_w_up', 'mlp_w_down'])
    last = _exchange_wait("scatter_last_wait", last_st, new_v['mlp_w_down'])
    for n, p in zip(['even_w_in', 'mla_w_uq', 'mla_w_ukv', 'even_w_out'], last[:4]):
        red[n] = _sum8("sum_" + n, p)
    small_g = _sum8("sum_small", last[4])
    grads.update({'even_w_in': red['even_w_in'].T[None], 'mla_w_uq': red['mla_w_uq'].T[None],
                  'mla_w_ukv': red['mla_w_ukv'].T[None], 'even_w_out': red['even_w_out'][None]})
    apply(['even_w_in', 'mla_w_uq', 'mla_w_ukv', 'even_w_out'])
    off = 0
    for n, sz in zip(_SMALL + ['odd_norm_full'], small_sizes):
        seg = small_g[:, off:off + sz]
        off += sz
        if n == 'odd_norm_full':
            me = 4 * lax.axis_index("x") + 2 * lax.axis_index("y") + lax.axis_index("c")
            grads['odd_norm'] = lax.dynamic_slice(seg, (0, me * (D // N_DEV)), (1, D // N_DEV))
        else:
            grads[n] = seg.reshape(w[n].shape)

    sm_names = _SMALL + ['odd_norm']
    pack = lambda d: jnp.concatenate([d[n].reshape(1, -1) for n in sm_names], 1)
    pw, pg, pm, pv = pack(w), pack(grads), pack(m), pack(v)
    padw = (-pw.shape[1]) % LANES
    padf = lambda a: jnp.pad(a, ((0, 0), (0, padw)))
    d_, m_, v_ = _adamw("adamw_small", padf(pw), padf(pg), padf(pm), jnp.pad(pv, ((0, 0), (0, padw)), constant_values=1.0))
    off = 0
    for n in sm_names:
        sz = w[n].size
        delta[n] = d_[:, off:off + sz].reshape(w[n].shape)
        new_m[n] = m_[:, off:off + sz].reshape(w[n].shape)
        new_v[n] = v_[:, off:off + sz].reshape(w[n].shape)
        off += sz

    return (loss, grad_x[None], *[grads[n] for n in _WEIGHTS], *[delta[n] for n in _WEIGHTS],
            *[new_m[n] for n in _WEIGHTS], *[new_v[n] for n in _WEIGHTS])


def kernel(x, even_norm, even_w_in, mla_q_lat_norm, mla_kv_lat_norm, mla_w_uq, mla_w_ukv, mla_q_norm, mla_k_nope_norm, mla_k_rope_norm, gqa_q_norm, gqa_k_norm, even_w_out, odd_norm, odd_w_qkv, swa_q_norm, swa_k_norm, swa_sink, odd_w_out, mlp_norm, mlp_w_up, mlp_w_down, loss_target, m_even_norm, m_even_w_in, m_mla_q_lat_norm, m_mla_kv_lat_norm, m_mla_w_uq, m_mla_w_ukv, m_mla_q_norm, m_mla_k_nope_norm, m_mla_k_rope_norm, m_gqa_q_norm, m_gqa_k_norm, m_even_w_out, m_odd_norm, m_odd_w_qkv, m_swa_q_norm, m_swa_k_norm, m_swa_sink, m_odd_w_out, m_mlp_norm, m_mlp_w_up, m_mlp_w_down, v_even_norm, v_even_w_in, v_mla_q_lat_norm, v_mla_kv_lat_norm, v_mla_w_uq, v_mla_w_ukv, v_mla_q_norm, v_mla_k_nope_norm, v_mla_k_rope_norm, v_gqa_q_norm, v_gqa_k_norm, v_even_w_out, v_odd_norm, v_odd_w_qkv, v_swa_q_norm, v_swa_k_norm, v_swa_sink, v_odd_w_out, v_mlp_norm, v_mlp_w_up, v_mlp_w_down):
    ws = (even_norm, even_w_in, mla_q_lat_norm, mla_kv_lat_norm, mla_w_uq, mla_w_ukv, mla_q_norm, mla_k_nope_norm, mla_k_rope_norm, gqa_q_norm, gqa_k_norm, even_w_out, odd_norm, odd_w_qkv, swa_q_norm, swa_k_norm, swa_sink, odd_w_out, mlp_norm, mlp_w_up, mlp_w_down)
    ms = (m_even_norm, m_even_w_in, m_mla_q_lat_norm, m_mla_kv_lat_norm, m_mla_w_uq, m_mla_w_ukv, m_mla_q_norm, m_mla_k_nope_norm, m_mla_k_rope_norm, m_gqa_q_norm, m_gqa_k_norm, m_even_w_out, m_odd_norm, m_odd_w_qkv, m_swa_q_norm, m_swa_k_norm, m_swa_sink, m_odd_w_out, m_mlp_norm, m_mlp_w_up, m_mlp_w_down)
    vs = (v_even_norm, v_even_w_in, v_mla_q_lat_norm, v_mla_kv_lat_norm, v_mla_w_uq, v_mla_w_ukv, v_mla_q_norm, v_mla_k_nope_norm, v_mla_k_rope_norm, v_gqa_q_norm, v_gqa_k_norm, v_even_w_out, v_odd_norm, v_odd_w_qkv, v_swa_q_norm, v_swa_k_norm, v_swa_sink, v_odd_w_out, v_mlp_norm, v_mlp_w_up, v_mlp_w_down)
    return _step(x[0], loss_target[0], dict(zip(_WEIGHTS, ws)), dict(zip(_WEIGHTS, ms)), dict(zip(_WEIGHTS, vs)))
```

```python
import functools

import jax
import jax.numpy as jnp
from jax import lax
from jax.experimental import pallas as pl
from jax.experimental.pallas import tpu as pltpu

F32 = jnp.float32
BF16 = jnp.bfloat16

N_DEV = 8
NORM_EPS = 1e-6
ROPE_THETA = 500000.0
AXIAL_THETA = 10000.0
GRID_W = 64
HEADS = 8
GQA_KV = 2
HD = 128
MLA_ROPE = 64
MLA_QK = HD + MLA_ROPE
Q_LORA = 512
KV_LORA = 256
SWA_HEADS = 32
SWA_KV = 4
SWA_D = 64
SWA_ROT = 16
SWA_WINDOW = 128
SWA_BLOCK = 128
MLA_SCALE, GQA_SCALE, SWA_SCALE = MLA_QK ** -0.5, HD ** -0.5, SWA_D ** -0.5
LANES = 128
ADAM_LR, ADAM_B1, ADAM_B2, ADAM_EPS, ADAM_WD, ADAM_STEP = 0.001, 0.9, 0.999, 1e-08, 0.01, 10
VMEM_LIMIT = 56 * 1024 * 1024

P_CQ, P_CKV, P_KR, P_QG = 0, Q_LORA, Q_LORA + KV_LORA, Q_LORA + KV_LORA + LANES
P_KG = P_QG + HEADS * HD
P_VG = P_KG + GQA_KV * HD
P_END = P_VG + GQA_KV * HD
KR_END = Q_LORA + KV_LORA + MLA_ROPE


def _tile(n, prefs):
    for t in prefs:
        if n % t == 0 and t <= n:
            return t
    return n


def _params(sem):
    return pltpu.CompilerParams(dimension_semantics=sem, vmem_limit_bytes=VMEM_LIMIT)


_DIMS = {"nn": ((1,), (0,)), "nt": ((1,), (1,)), "tn": ((0,), (0,))}


def _mm(name, a, b, mode, out_dtypes, epilogue=None, extras=(), tm=1024, tn=1024, tk=2048):
    if mode == "nn":
        (M, K), (_, N) = a.shape, b.shape
    elif mode == "nt":
        (M, K), (N, _) = a.shape, b.shape
    else:
        (K, M), (_, N) = a.shape, b.shape
    tm = _tile(M, (tm, 512, 256, 128))
    tn = _tile(N, (tn, 512, 256, 128))
    tk = _tile(K, (tk, 1024, 512, 256, 128))
    nk = K // tk
    ne, no = len(extras), len(out_dtypes)
    if mode == "tn":
        a_spec = pl.BlockSpec((tk, tm), lambda i, j, k: (k, i))
    else:
        a_spec = pl.BlockSpec((tm, tk), lambda i, j, k: (i, k))
    if mode == "nt":
        b_spec = pl.BlockSpec((tn, tk), lambda i, j, k: (j, k))
    else:
        b_spec = pl.BlockSpec((tk, tn), lambda i, j, k: (k, j))
    o_spec = pl.BlockSpec((tm, tn), lambda i, j, k: (i, j))
    dims = (_DIMS[mode], ((), ()))

    def body(a_ref, b_ref, *rest):
        ex, outs = rest[:ne], rest[ne:ne + no]
        k = pl.program_id(2)
        part = lax.dot_general(a_ref[...].astype(BF16), b_ref[...].astype(BF16), dims, preferred_element_type=F32)

        def finish(total):
            res = epilogue(total, *[e[...] for e in ex]) if epilogue else (total,)
            for o, r in zip(outs, res):
                o[...] = r.astype(o.dtype)

        if nk == 1:
            finish(part)
            return
        acc = rest[ne + no]

        @pl.when(k == 0)
        def _():
            acc[...] = part

        @pl.when((k > 0) & (k < nk - 1))
        def _():
            acc[...] += part

        @pl.when(k == nk - 1)
        def _():
            finish(acc[...] + part)

    out = pl.pallas_call(
        body, name=name, grid=(M // tm, N // tn, nk),
        in_specs=[a_spec, b_spec] + [o_spec] * ne,
        out_specs=[o_spec] * no,
        out_shape=[jax.ShapeDtypeStruct((M, N), d) for d in out_dtypes],
        scratch_shapes=[pltpu.VMEM((tm, tn), F32)] if nk > 1 else [],
        compiler_params=_params(("parallel", "parallel", "arbitrary")),
    )(a, b, *extras)
    return out[0] if no == 1 else out


def _rowwise(name, fn, rows, consts, outs, accs=(), tm=256):
    S = rows[0].shape[-2]
    tm = _tile(S, (tm, 128, 64, 32, 16, 8))
    nr, nc, no, na = len(rows), len(consts), len(outs), len(accs)

    def rspec(shape):
        if len(shape) == 2:
            return pl.BlockSpec((tm, shape[1]), lambda i: (i, 0))
        return pl.BlockSpec((shape[0], tm, shape[2]), lambda i: (0, i, 0))

    def cspec(shape):
        return pl.BlockSpec(tuple(shape), lambda i: (0,) * len(shape))

    def body(*refs):
        r, c = refs[:nr], refs[nr:nr + nc]
        o, a = refs[nr + nc:nr + nc + no], refs[nr + nc + no:]
        vals = fn(r, c, o)
        if na:
            @pl.when(pl.program_id(0) == 0)
            def _():
                for ar in a:
                    ar[...] = jnp.zeros_like(ar)

            for ar, v in zip(a, vals):
                ar[...] += v

    res = pl.pallas_call(
        body, name=name, grid=(S // tm,),
        in_specs=[rspec(x.shape) for x in rows] + [cspec(x.shape) for x in consts],
        out_specs=[rspec(x.shape) for x in outs] + [cspec(x.shape) for x in accs],
        out_shape=list(outs) + list(accs),
        compiler_params=_params(("arbitrary",) if na else ("parallel",)),
    )(*rows, *consts)
    return res


def _sds(shape, dtype):
    return jax.ShapeDtypeStruct(tuple(shape), dtype)


def _rsum(x):
    return jnp.sum(x, axis=-1, keepdims=True)


def _csum(x):
    return jnp.sum(x, axis=0, keepdims=True)


def _lane(shape):
    return lax.broadcasted_iota(jnp.int32, shape, 1)


def _partner(x, seg, half):
    lane = _lane(x.shape) % seg
    return jnp.where(lane < half, pltpu.roll(x, LANES - half, 1), pltpu.roll(x, half, 1))


def _rope(x, c, s, seg, half):
    return x * c + _partner(x, seg, half) * s


def _rope_bwd(dy, c, s, seg, half):
    t = _partner(dy * s, seg, half)
    if seg != 2 * half:
        t = jnp.where(_lane(dy.shape) % seg < 2 * half, t, 0.0)
    return dy * c + t


def _rms_r(ss, n):
    return lax.rsqrt(ss * (1.0 / n) + NORM_EPS)


def _rms_bwd(x, r, g, dy, dot_scale):
    dyg = dy * g
    return r * dyg - x * (r * r * r) * (_rsum(dyg * x) * dot_scale)


def _rmsnorm(name, x, g):
    D = x.shape[1]

    def fn(r, c, o):
        xv = r[0][...]
        o[0][...] = (xv * _rms_r(_rsum(xv * xv), D) * c[0][...]).astype(BF16)
        return ()

    return _rowwise(name, fn, [x], [g], [_sds(x.shape, BF16)])[0]


def _rmsnorm_bwd(name, dres, x, dh, g):
    D = x.shape[1]

    def fn(r, c, o):
        xv, dhv, gv = r[1][...], r[2][...], c[0][...]
        rr = _rms_r(_rsum(xv * xv), D)
        o[0][...] = r[0][...] + _rms_bwd(xv, rr, gv, dhv, 1.0 / D)
        return (_csum(dhv * xv * rr),)

    return _rowwise(name, fn, [dres, x, dh], [g], [_sds(x.shape, F32)], [_sds((1, D), F32)])


def _loss(name, y, t):
    D = y.shape[1]

    def fn(r, c, o):
        e = r[0][...] - r[1][...]
        o[0][...] = e * (1.0 / D)
        return (_csum(_rsum(e * e)),)

    return _rowwise(name, fn, [y, t], [], [_sds(y.shape, F32)], [_sds((1, 1), F32)])


def _even_prep(name, proj, tabs, gains):
    S = proj.shape[0]
    cm, sm, ca, sa = tabs

    def fn(r, c, o):
        p, cmv, smv, cav, sav = r[0], r[1][...], r[2][...], r[3][...], r[4][...]
        g_ql, g_kvl, g_kr, g_q, g_k = (x[...] for x in c)
        cq = p[:, P_CQ:P_CKV]
        o[0][...] = (cq * _rms_r(_rsum(cq * cq), Q_LORA) * g_ql).astype(BF16)
        ckv = p[:, P_CKV:P_KR]
        o[1][...] = (ckv * _rms_r(_rsum(ckv * ckv), KV_LORA) * g_kvl).astype(BF16)
        kr = p[:, P_KR:P_QG]
        y = _rope(kr * _rms_r(_rsum(kr * kr), MLA_ROPE) * g_kr, cmv, smv, 64, 32)
        o[2][...] = (y + pltpu.roll(y, 64, 1)).astype(BF16)
        for h in range(HEADS):
            xh = p[:, P_QG + HD * h:P_QG + HD * (h + 1)]
            o[3][h] = (_rope(xh * _rms_r(_rsum(xh * xh), HD) * g_q, cav, sav, 64, 32) * GQA_SCALE).astype(BF16)
        for h in range(GQA_KV):
            xh = p[:, P_KG + HD * h:P_KG + HD * (h + 1)]
            o[4][h] = _rope(xh * _rms_r(_rsum(xh * xh), HD) * g_k, cav, sav, 64, 32).astype(BF16)
            o[5][h] = p[:, P_VG + HD * h:P_VG + HD * (h + 1)].astype(BF16)
        return ()

    outs = [_sds((S, Q_LORA), BF16), _sds((S, KV_LORA), BF16), _sds((S, LANES), BF16),
            _sds((HEADS, S, HD), BF16), _sds((GQA_KV, S, HD), BF16), _sds((GQA_KV, S, HD), BF16)]
    return _rowwise(name, fn, [proj, cm, sm, ca, sa], gains, outs)


def _even_prep_bwd(name, dcqn, dckvn, dkrd, dqg, dkg, dvg, proj, tabs, gains):
    S = proj.shape[0]
    cm, sm, ca, sa = tabs

    def fn(r, c, o):
        p, cmv, smv, cav, sav = r[6], r[7][...], r[8][...], r[9][...], r[10][...]
        g_ql, g_kvl, g_kr, g_q, g_k = (x[...] for x in c)
        out = o[0]
        cq = p[:, P_CQ:P_CKV]
        rr = _rms_r(_rsum(cq * cq), Q_LORA)
        d = r[0][...]
        out[:, P_CQ:P_CKV] = _rms_bwd(cq, rr, g_ql, d, 1.0 / Q_LORA).astype(BF16)
        a_ql = _csum(d * cq * rr)
        ckv = p[:, P_CKV:P_KR]
        rr = _rms_r(_rsum(ckv * ckv), KV_LORA)
        d = r[1][...]
        out[:, P_CKV:P_KR] = _rms_bwd(ckv, rr, g_kvl, d, 1.0 / KV_LORA).astype(BF16)
        a_kvl = _csum(d * ckv * rr)
        kr = p[:, P_KR:P_QG]
        rr = _rms_r(_rsum(kr * kr), MLA_ROPE)
        d = r[2][...]
        d = d + pltpu.roll(d, 64, 1)
        d = _rope_bwd(d, cmv, smv, 64, 32)
        low = _lane(d.shape) < 64
        out[:, P_KR:P_QG] = jnp.where(low, _rms_bwd(kr, rr, g_kr, d, 1.0 / MLA_ROPE), 0.0).astype(BF16)
        a_kr = _csum(d * kr * rr)
        a_q = jnp.zeros((1, HD), F32)
        for h in range(HEADS):
            xh = p[:, P_QG + HD * h:P_QG + HD * (h + 1)]
            rr = _rms_r(_rsum(xh * xh), HD)
            d = _rope_bwd(r[3][h] * GQA_SCALE, cav, sav, 64, 32)
            out[:, P_QG + HD * h:P_QG + HD * (h + 1)] = _rms_bwd(xh, rr, g_q, d, 1.0 / HD).astype(BF16)
            a_q = a_q + _csum(d * xh * rr)
        a_k = jnp.zeros((1, HD), F32)
        grp = HEADS // GQA_KV
        for h in range(GQA_KV):
            xh = p[:, P_KG + HD * h:P_KG + HD * (h + 1)]
            rr = _rms_r(_rsum(xh * xh), HD)
            d = r[4][grp * h]
            dv = r[5][grp * h]
            for e in range(1, grp):
                d = d + r[4][grp * h + e]
                dv = dv + r[5][grp * h + e]
            d = _rope_bwd(d, cav, sav, 64, 32)
            out[:, P_KG + HD * h:P_KG + HD * (h + 1)] = _rms_bwd(xh, rr, g_k, d, 1.0 / HD).astype(BF16)
            out[:, P_VG + HD * h:P_VG + HD * (h + 1)] = dv.astype(BF16)
            a_k = a_k + _csum(d * xh * rr)
        return (a_ql, a_kvl, a_kr, a_q, a_k)

    accs = [_sds((1, Q_LORA), F32), _sds((1, KV_LORA), F32), _sds((1, LANES), F32),
            _sds((1, HD), F32), _sds((1, HD), F32)]
    return _rowwise(name, fn, [dcqn, dckvn, dkrd, dqg, dkg, dvg, proj, cm, sm, ca, sa], gains,
                    [_sds((S, P_END), BF16)], accs, tm=128)


def _mla_q_parts(qa, h, rp, g_n, g_r):
    nope = qa[:, HD * h:HD * (h + 1)]
    mine = (_lane(rp.shape) >= 64) == bool(h % 2)
    rpm = jnp.where(mine, rp, 0.0)
    rr = _rms_r(_rsum(nope * nope) + _rsum(rpm * rpm), MLA_QK)
    return nope, rpm, mine, rr


def _mla_prep(name, qa, kv, krd, tabs, gains):
    S = qa.shape[0]
    cm, sm = tabs

    def fn(r, c, o):
        qa_r, kv_r, krd_v, cmv, smv = r[0], r[1], r[2][...], r[3][...], r[4][...]
        g_n, g_r, g_kn = (x[...] for x in c)
        for h in range(HEADS):
            rp = qa_r[:, HEADS * HD + LANES * (h // 2):HEADS * HD + LANES * (h // 2 + 1)]
            nope, rpm, mine, rr = _mla_q_parts(qa_r, h, rp, g_n, g_r)
            o[0][h, :, 0:HD] = (nope * (rr * MLA_SCALE) * g_n).astype(BF16)
            o[0][h, :, HD:2 * HD] = _rope(rpm * (rr * MLA_SCALE) * g_r, cmv, smv, 64, 32).astype(BF16)
            kn = kv_r[:, 2 * HD * h:2 * HD * h + HD]
            o[1][h, :, 0:HD] = (kn * _rms_r(_rsum(kn * kn), HD) * g_kn).astype(BF16)
            o[1][h, :, HD:2 * HD] = krd_v
            o[2][h] = kv_r[:, 2 * HD * h + HD:2 * HD * (h + 1)].astype(BF16)
        return ()

    outs = [_sds((HEADS, S, 2 * HD), BF16), _sds((HEADS, S, 2 * HD), BF16), _sds((HEADS, S, HD), BF16)]
    return _rowwise(name, fn, [qa, kv, krd, cm, sm], gains, outs)


def _mla_prep_bwd(name, dq, dk, dv, qa, kv, tabs, gains):
    S = qa.shape[0]
    cm, sm = tabs

    def fn(r, c, o):
        dq_r, dk_r, dv_r, qa_r, kv_r, cmv, smv = r[0], r[1], r[2], r[3], r[4], r[5][...], r[6][...]
        g_n, g_r, g_kn = (x[...] for x in c)
        a_n = jnp.zeros((1, HD), F32)
        a_r = jnp.zeros((1, LANES), F32)
        a_kn = jnp.zeros((1, HD), F32)
        dkrd = jnp.zeros(cmv.shape, F32)
        drp = None
        for h in range(HEADS):
            rp = qa_r[:, HEADS * HD + LANES * (h // 2):HEADS * HD + LANES * (h // 2 + 1)]
            nope, rpm, mine, rr = _mla_q_parts(qa_r, h, rp, g_n, g_r)
            dn = dq_r[h, :, 0:HD] * MLA_SCALE
            dr = _rope_bwd(jnp.where(mine, dq_r[h, :, HD:2 * HD] * MLA_SCALE, 0.0), cmv, smv, 64, 32)
            dot = (_rsum(dn * g_n * nope) + _rsum(dr * g_r * rpm)) * (1.0 / MLA_QK)
            r3 = rr * rr * rr
            o[0][:, HD * h:HD * (h + 1)] = (rr * dn * g_n - nope * r3 * dot).astype(BF16)
            part = jnp.where(mine, rr * dr * g_r - rpm * r3 * dot, 0.0)
            drp = part if h % 2 == 0 else drp + part
            if h % 2 == 1:
                o[0][:, HEADS * HD + LANES * (h // 2):HEADS * HD + LANES * (h // 2 + 1)] = drp.astype(BF16)
            a_n = a_n + _csum(dn * nope * rr)
            a_r = a_r + _csum(dr * rpm * rr)
            kn = kv_r[:, 2 * HD * h:2 * HD * h + HD]
            rk = _rms_r(_rsum(kn * kn), HD)
            dkn = dk_r[h, :, 0:HD]
            o[1][:, 2 * HD * h:2 * HD * h + HD] = _rms_bwd(kn, rk, g_kn, dkn, 1.0 / HD).astype(BF16)
            o[1][:, 2 * HD * h + HD:2 * HD * (h + 1)] = dv_r[h].astype(BF16)
            a_kn = a_kn + _csum(dkn * kn * rk)
            dkrd = dkrd + dk_r[h, :, HD:2 * HD]
        o[2][...] = dkrd
        return (a_n, a_r, a_kn)

    outs = [_sds(qa.shape, BF16), _sds(kv.shape, BF16), _sds((S, LANES), F32)]
    accs = [_sds((1, HD), F32), _sds((1, LANES), F32), _sds((1, HD), F32)]
    return _rowwise(name, fn, [dq, dk, dv, qa, kv, cm, sm], gains, outs, accs, tm=128)


def _seg64_r(x):
    low = _lane(x.shape) < 64
    x2 = x * x
    s0 = _rsum(jnp.where(low, x2, 0.0))
    s1 = _rsum(jnp.where(low, 0.0, x2))
    return jnp.where(low, _rms_r(s0, SWA_D), _rms_r(s1, SWA_D)), low


def _swa_prep(name, qkv, tabs, gains):
    S = qkv.shape[0]
    nq, nk = SWA_HEADS * SWA_D, SWA_KV * SWA_D
    cs, ss = tabs

    def fn(r, c, o):
        x_r, csv, ssv = r[0], r[1][...], r[2][...]
        g_q, g_k = c[0][...], c[1][...]
        for g in range((nq + nk) // LANES):
            x = x_r[:, LANES * g:LANES * (g + 1)]
            rr, _ = _seg64_r(x)
            y = _rope(x * rr * (g_q if g < nq // LANES else g_k), csv, ssv, SWA_D, SWA_ROT // 2)
            if g < nq // LANES:
                o[0][:, LANES * g:LANES * (g + 1)] = (y * SWA_SCALE).astype(BF16)
            else:
                o[1][:, LANES * g - nq:LANES * (g + 1) - nq] = y.astype(BF16)
        o[2][...] = x_r[:, nq + nk:nq + 2 * nk].astype(BF16)
        return ()

    outs = [_sds((S, nq), BF16), _sds((S, nk), BF16), _sds((S, nk), BF16)]
    return _rowwise(name, fn, [qkv, cs, ss], gains, outs)


def _swa_prep_bwd(name, dq, dk, dv, qkv, tabs, gains):
    nq, nk = SWA_HEADS * SWA_D, SWA_KV * SWA_D
    cs, ss = tabs

    def fn(r, c, o):
        dq_r, dk_r, x_r, csv, ssv = r[0], r[1], r[3], r[4][...], r[5][...]
        g_q, g_k = c[0][...], c[1][...]
        acc = [jnp.zeros((1, LANES), F32), jnp.zeros((1, LANES), F32)]
        for g in range((nq + nk) // LANES):
            isq = g < nq // LANES
            x = x_r[:, LANES * g:LANES * (g + 1)]
            rr, low = _seg64_r(x)
            d = dq_r[:, LANES * g:LANES * (g + 1)] * SWA_SCALE if isq else dk_r[:, LANES * g - nq:LANES * (g + 1) - nq]
            d = _rope_bwd(d, csv, ssv, SWA_D, SWA_ROT // 2)
            dyg = d * (g_q if isq else g_k)
            t = dyg * x
            d0 = _rsum(jnp.where(low, t, 0.0))
            d1 = _rsum(jnp.where(low, 0.0, t))
            dot = jnp.where(low, d0, d1) * (1.0 / SWA_D)
            o[0][:, LANES * g:LANES * (g + 1)] = (rr * dyg - x * (rr * rr * rr) * dot).astype(BF16)
            acc[0 if isq else 1] = acc[0 if isq else 1] + _csum(d * x * rr)
        o[0][:, nq + nk:nq + 2 * nk] = r[2][...].astype(BF16)
        return tuple(acc)

    return _rowwise(name, fn, [dq, dk, dv, qkv, cs, ss], gains, [_sds(qkv.shape, BF16)],
                    [_sds((1, LANES), F32), _sds((1, LANES), F32)], tm=128)


def _delta(name, do, o_, width):
    S, C = do.shape
    nh = C // width

    def fn(r, c, o):
        for g in range(C // LANES):
            t = r[0][:, LANES * g:LANES * (g + 1)].astype(F32) * r[1][:, LANES * g:LANES * (g + 1)].astype(F32)
            if width == LANES:
                o[0][g] = _rsum(t)
            else:
                low = _lane(t.shape) < 64
                o[0][2 * g] = _rsum(jnp.where(low, t, 0.0))
                o[0][2 * g + 1] = _rsum(jnp.where(low, 0.0, t))
        return ()

    return _rowwise(name, fn, [do, o_], [], [_sds((nh, S, 1), F32)])[0]


_NT = (((1,), (1,)), ((), ()))
_NN = (((1,), (0,)), ((), ()))
_TN = (((0,), (0,)), ((), ()))


def _flash_fwd(name, q, k, v, tq=512, tk=4096):
    H, S, dk = q.shape
    G = H // k.shape[0]
    dv = v.shape[2]
    tq, tk = _tile(S, (tq, 256, 128)), _tile(S, (tk, 256, 128))
    nk = S // tk

    def body(q_ref, k_ref, v_ref, o_ref, lse_ref, m_s, l_s, acc_s):
        j = pl.program_id(2)

        @pl.when(j == 0)
        def _():
            m_s[...] = jnp.full_like(m_s, -jnp.inf)
            l_s[...] = jnp.zeros_like(l_s)
            acc_s[...] = jnp.zeros_like(acc_s)

        s = lax.dot_general(q_ref[...], k_ref[...], _NT, preferred_element_type=F32)
        m_new = jnp.maximum(m_s[...], jnp.max(s, axis=-1, keepdims=True))
        alpha = jnp.exp(m_s[...] - m_new)
        p = jnp.exp(s - m_new)
        l_s[...] = alpha * l_s[...] + _rsum(p)
        acc_s[...] = alpha * acc_s[...] + lax.dot_general(p.astype(BF16), v_ref[...], _NN,
                                                          preferred_element_type=F32)
        m_s[...] = m_new

        @pl.when(j == nk - 1)
        def _():
            o_ref[...] = (acc_s[...] / l_s[...]).astype(o_ref.dtype)
            lse_ref[...] = m_s[...] + jnp.log(l_s[...])

    return pl.pallas_call(
        body, name=name, grid=(H, S // tq, nk),
        in_specs=[pl.BlockSpec((None, tq, dk), lambda h, i, j: (h, i, 0)),
                  pl.BlockSpec((None, tk, dk), lambda h, i, j: (h // G, j, 0)),
                  pl.BlockSpec((None, tk, dv), lambda h, i, j: (h // G, j, 0))],
        out_specs=[pl.BlockSpec((tq, dv), lambda h, i, j: (i, h)),
                   pl.BlockSpec((None, tq, 1), lambda h, i, j: (h, i, 0))],
        out_shape=[_sds((S, H * dv), BF16), _sds((H, S, 1), F32)],
        scratch_shapes=[pltpu.VMEM((tq, 1), F32), pltpu.VMEM((tq, 1), F32), pltpu.VMEM((tq, dv), F32)],
        compiler_params=_params(("parallel", "parallel", "arbitrary")),
    )(q, k, v)


def _flash_bwd(name, q, k, v, do, head0, lse, delta, tq=1024, tk=1024):
    H, S, dk = q.shape
    G = H // k.shape[0]
    dv = v.shape[2]
    tq, tk = _tile(S, (tq, 256, 128)), _tile(S, (tk, 256, 128))
    nq = S // tq

    def body(q_ref, k_ref, v_ref, do_ref, lse_ref, dl_ref, dq_ref, dk_ref, dv_ref, dk_s, dv_s):
        j, i = pl.program_id(1), pl.program_id(2)

        @pl.when(i == 0)
        def _():
            dk_s[...] = jnp.zeros_like(dk_s)
            dv_s[...] = jnp.zeros_like(dv_s)

        qv, kv_, dov = q_ref[...], k_ref[...], do_ref[...]
        s = lax.dot_general(qv, kv_, _NT, preferred_element_type=F32)
        p = jnp.exp(s - lse_ref[...])
        dp = lax.dot_general(dov, v_ref[...], _NT, preferred_element_type=F32)
        ds = (p * (dp - dl_ref[...])).astype(BF16)
        dv_s[...] += lax.dot_general(p.astype(BF16), dov, _TN, preferred_element_type=F32)
        dk_s[...] += lax.dot_general(ds, qv, _TN, preferred_element_type=F32)
        dqi = lax.dot_general(ds, kv_, _NN, preferred_element_type=F32)
        rows = pl.ds(pl.multiple_of(i * tq, tq), tq)

        @pl.when(j == 0)
        def _():
            dq_ref[rows, :] = dqi

        @pl.when(j > 0)
        def _():
            dq_ref[rows, :] += dqi

        @pl.when(i == nq - 1)
        def _():
            dk_ref[...] = dk_s[...]
            dv_ref[...] = dv_s[...]

    return pl.pallas_call(
        body, name=name, grid=(H, S // tk, nq),
        in_specs=[pl.BlockSpec((None, tq, dk), lambda h, j, i: (h, i, 0)),
                  pl.BlockSpec((None, tk, dk), lambda h, j, i: (h // G, j, 0)),
                  pl.BlockSpec((None, tk, dv), lambda h, j, i: (h // G, j, 0)),
                  pl.BlockSpec((tq, dv), lambda h, j, i: (i, head0 + h)),
                  pl.BlockSpec((None, tq, 1), lambda h, j, i: (head0 + h, i, 0)),
                  pl.BlockSpec((None, tq, 1), lambda h, j, i: (head0 + h, i, 0))],
        out_specs=[pl.BlockSpec((None, S, dk), lambda h, j, i: (h, 0, 0)),
                   pl.BlockSpec((None, tk, dk), lambda h, j, i: (h, j, 0)),
                   pl.BlockSpec((None, tk, dv), lambda h, j, i: (h, j, 0))],
        out_shape=[_sds((H, S, dk), F32), _sds((H, S, dk), F32), _sds((H, S, dv), F32)],
        scratch_shapes=[pltpu.VMEM((tk, dk), F32), pltpu.VMEM((tk, dv), F32)],
        compiler_params=_params(("parallel", "arbitrary", "arbitrary")),
    )(q, k, v, do, lse, delta)


def _swa_place(ref128, h):
    e, t = h % 2, (h // (SWA_HEADS // SWA_KV)) % 2
    x = ref128.astype(F32)
    if e != t:
        x = pltpu.roll(x, 64, 1)
    return jnp.where((_lane(x.shape) >= 64) == bool(t), x, 0.0).astype(BF16)


def _swa_unplace(y, h):
    e, t = h % 2, (h // (SWA_HEADS // SWA_KV)) % 2
    return pltpu.roll(y, 64, 1) if e != t else y


def _swa_specs(width, nb):
    prev = pl.BlockSpec((SWA_BLOCK, width), lambda i: (jnp.maximum(i - 1, 0), 0))
    cur = pl.BlockSpec((SWA_BLOCK, width), lambda i: (i, 0))
    nxt = pl.BlockSpec((SWA_BLOCK, width), lambda i: (jnp.minimum(i + 1, nb - 1), 0))
    return [prev, cur, nxt]


def _swa_valid_t(i, S, reps):
    shape = (3 * SWA_BLOCK, reps * SWA_BLOCK)
    kpos = (i - 1) * SWA_BLOCK + lax.broadcasted_iota(jnp.int32, shape, 0)
    qpos = i * SWA_BLOCK + lax.broadcasted_iota(jnp.int32, shape, 1) % SWA_BLOCK
    return (jnp.abs(qpos - kpos) <= SWA_WINDOW) & (kpos >= 0) & (kpos < S)


def _swa_stack(refs, heads):
    return jnp.concatenate([_swa_place(r[:, LANES * (h // 2):LANES * (h // 2 + 1)], h) for h in heads for r in refs],
                           axis=0)


def _swa_unstack_t(ot, j, out_ref):
    grp = SWA_HEADS // SWA_KV
    for pair in range(grp // 2):
        h = grp * j + 2 * pair
        a = _swa_unplace(ot[:, SWA_BLOCK * 2 * pair:SWA_BLOCK * (2 * pair + 1)].T, h)
        b = _swa_unplace(ot[:, SWA_BLOCK * (2 * pair + 1):SWA_BLOCK * (2 * pair + 2)].T, h + 1)
        out_ref[:, LANES * (h // 2):LANES * (h // 2 + 1)] = jnp.where(_lane(a.shape) < 64, a, b).astype(out_ref.dtype)


def _swa_fwd(name, q, k, v, sink):
    S = q.shape[0]
    nb = S // SWA_BLOCK
    grp = SWA_HEADS // SWA_KV
    smem = pl.BlockSpec(memory_space=pltpu.SMEM)

    def body(sink_ref, q_ref, kp, kc, kn, vp, vc, vn, o_ref, lse_ref):
        i = pl.program_id(0)
        valid = _swa_valid_t(i, S, grp)
        kcat = [jnp.concatenate([r[:, LANES * u:LANES * (u + 1)] for r in (kp, kc, kn)], axis=0) for u in range(2)]
        vcat = [jnp.concatenate([r[:, LANES * u:LANES * (u + 1)] for r in (vp, vc, vn)], axis=0) for u in range(2)]
        for j in range(SWA_KV):
            heads = range(grp * j, grp * (j + 1))
            xq = _swa_stack([q_ref], heads)
            sk = jnp.concatenate([jnp.full((1, SWA_BLOCK), sink_ref[0, h], F32) for h in heads], axis=1)
            st = lax.dot_general(kcat[j // 2], xq, _NT, preferred_element_type=F32)
            st = jnp.where(valid, st, -jnp.inf)
            m = jnp.maximum(jnp.max(st, axis=0, keepdims=True), sk)
            pt = jnp.exp(st - m)
            den = jnp.sum(pt, axis=0, keepdims=True) + jnp.exp(sk - m)
            ot = lax.dot_general(vcat[j // 2], pt.astype(BF16), _TN, preferred_element_type=F32) * (1.0 / den)
            lse = m + jnp.log(den)
            for e, h in enumerate(heads):
                lse_ref[h] = lse[:, SWA_BLOCK * e:SWA_BLOCK * (e + 1)]
            _swa_unstack_t(ot, j, o_ref)

    return pl.pallas_call(
        body, name=name, grid=(nb,),
        in_specs=[smem, pl.BlockSpec((SWA_BLOCK, q.shape[1]), lambda i: (i, 0))]
        + _swa_specs(k.shape[1], nb) + _swa_specs(v.shape[1], nb),
        out_specs=[pl.BlockSpec((SWA_BLOCK, q.shape[1]), lambda i: (i, 0)),
                   pl.BlockSpec((SWA_HEADS, 1, SWA_BLOCK), lambda i: (0, 0, i))],
        out_shape=[_sds(q.shape, BF16), _sds((SWA_HEADS, 1, S), F32)],
        compiler_params=_params(("parallel",)),
    )(sink, q, k, k, k, v, v, v)


def _swa_bwd_q(name, q, k, v, do, lse_t, delta_t, sink):
    S = q.shape[0]
    nb = S // SWA_BLOCK
    grp = SWA_HEADS // SWA_KV
    smem = pl.BlockSpec(memory_space=pltpu.SMEM)
    row = pl.BlockSpec((SWA_BLOCK, q.shape[1]), lambda i: (i, 0))
    hrow = pl.BlockSpec((SWA_HEADS, 1, SWA_BLOCK), lambda i: (0, 0, i))

    def body(sink_ref, q_ref, do_ref, lse_ref, dl_ref, kp, kc, kn, vp, vc, vn, dq_ref, ds_ref):
        i = pl.program_id(0)

        @pl.when(i == 0)
        def _():
            ds_ref[...] = jnp.zeros_like(ds_ref)

        valid = _swa_valid_t(i, S, grp)
        kcat = [jnp.concatenate([r[:, LANES * u:LANES * (u + 1)] for r in (kp, kc, kn)], axis=0) for u in range(2)]
        vcat = [jnp.concatenate([r[:, LANES * u:LANES * (u + 1)] for r in (vp, vc, vn)], axis=0) for u in range(2)]
        for j in range(SWA_KV):
            heads = range(grp * j, grp * (j + 1))
            xq = _swa_stack([q_ref], heads)
            xdo = _swa_stack([do_ref], heads)
            lse_r = jnp.concatenate([lse_ref[h] for h in heads], axis=1)
            dl_r = jnp.concatenate([dl_ref[h] for h in heads], axis=1)
            st = lax.dot_general(kcat[j // 2], xq, _NT, preferred_element_type=F32)
            pt = jnp.exp(jnp.where(valid, st, -jnp.inf) - lse_r)
            dpt = lax.dot_general(vcat[j // 2], xdo, _NT, preferred_element_type=F32)
            dst = (pt * (dpt - dl_r)).astype(BF16)
            _swa_unstack_t(lax.dot_general(kcat[j // 2], dst, _TN, preferred_element_type=F32), j, dq_ref)
            for h in heads:
                dsink = -_rsum(jnp.exp(sink_ref[0, h] - lse_ref[h]) * dl_ref[h])
                ds_ref[h:h + 1, :] += jnp.broadcast_to(dsink, (1, LANES))

    return pl.pallas_call(
        body, name=name, grid=(nb,),
        in_specs=[smem, row, row, hrow, hrow] + _swa_specs(k.shape[1], nb) + _swa_specs(v.shape[1], nb),
        out_specs=[row, pl.BlockSpec((SWA_HEADS, LANES), lambda i: (0, 0))],
        out_shape=[_sds(q.shape, F32), _sds((SWA_HEADS, LANES), F32)],
        compiler_params=_params(("arbitrary",)),
    )(sink, q, do, lse_t, delta_t, k, k, k, v, v, v)


def _swa_bwd_kv(name, q, k, v, do, lse_t, delta_t):
    S = q.shape[0]
    nb = S // SWA_BLOCK
    grp = SWA_HEADS // SWA_KV
    krow = pl.BlockSpec((SWA_BLOCK, k.shape[1]), lambda i: (i, 0))

    def stat3():
        return [pl.BlockSpec((SWA_HEADS, 1, SWA_BLOCK), lambda i: (0, 0, jnp.maximum(i - 1, 0))),
                pl.BlockSpec((SWA_HEADS, 1, SWA_BLOCK), lambda i: (0, 0, i)),
                pl.BlockSpec((SWA_HEADS, 1, SWA_BLOCK), lambda i: (0, 0, jnp.minimum(i + 1, nb - 1)))]

    def body(qp, qc, qn, dop, doc, don, lp, lc, ln, dp_, dc_, dn_, k_ref, v_ref, dk_ref, dv_ref):
        j = pl.program_id(0)
        nh = 2 * grp
        shape = (SWA_BLOCK, nh * 3 * SWA_BLOCK)
        qpos = (j - 1) * SWA_BLOCK + lax.broadcasted_iota(jnp.int32, shape, 1) % (3 * SWA_BLOCK)
        kpos = j * SWA_BLOCK + lax.broadcasted_iota(jnp.int32, shape, 0)
        valid = (jnp.abs(qpos - kpos) <= SWA_WINDOW) & (qpos >= 0) & (qpos < S)
        for u in range(SWA_KV // 2):
            heads = range(nh * u, nh * (u + 1))
            kc = k_ref[:, LANES * u:LANES * (u + 1)]
            vc = v_ref[:, LANES * u:LANES * (u + 1)]
            xq = _swa_stack([qp, qc, qn], heads)
            xdo = _swa_stack([dop, doc, don], heads)
            lse_r = jnp.concatenate([r[h] for h in heads for r in (lp, lc, ln)], axis=1)
            dl_r = jnp.concatenate([r[h] for h in heads for r in (dp_, dc_, dn_)], axis=1)
            st = lax.dot_general(kc, xq, _NT, preferred_element_type=F32)
            pt = jnp.where(valid, jnp.exp(st - lse_r), 0.0)
            dpt = lax.dot_general(vc, xdo, _NT, preferred_element_type=F32)
            dst = (pt * (dpt - dl_r)).astype(BF16)
            dv_ref[:, LANES * u:LANES * (u + 1)] = lax.dot_general(pt.astype(BF16), xdo, _NN, preferred_element_type=F32)
            dk_ref[:, LANES * u:LANES * (u + 1)] = lax.dot_general(dst, xq, _NN, preferred_element_type=F32)

    return pl.pallas_call(
        body, name=name, grid=(nb,),
        in_specs=_swa_specs(q.shape[1], nb) + _swa_specs(do.shape[1], nb) + stat3() + stat3() + [krow, krow],
        out_specs=[krow, krow],
        out_shape=[_sds(k.shape, F32), _sds(v.shape, F32)],
        compiler_params=_params(("parallel",)),
    )(q, q, q, do, do, do, lse_t, lse_t, lse_t, delta_t, delta_t, delta_t, k, v)


def _exchange(name, tensors):
    n = len(tensors)
    outs = [_sds(a.shape if sc else (N_DEV,) + a.shape, a.dtype) for a, sc in tensors]

    def body(*refs):
        ins, dst, send, recv, loc = refs[:n], refs[n:2 * n], refs[2 * n], refs[2 * n + 1], refs[2 * n + 2]
        x, y, c = lax.axis_index("x"), lax.axis_index("y"), lax.axis_index("c")
        me = 4 * x + 2 * y + c

        def peer(kk):
            return (x ^ ((kk >> 2) & 1), y ^ ((kk >> 1) & 1), c ^ (kk & 1))

        def copy(t, kk):
            px, py, pc = peer(kk)
            src = ins[t].at[4 * px + 2 * py + pc] if tensors[t][1] else ins[t]
            return pltpu.make_async_remote_copy(src_ref=src, dst_ref=dst[t].at[me], send_sem=send.at[t, kk - 1],
                                                recv_sem=recv.at[t, kk - 1], device_id=(px, py, pc),
                                                device_id_type=pl.DeviceIdType.MESH)

        def arrival(t, kk):
            px, py, pc = peer(kk)
            src = ins[t].at[me] if tensors[t][1] else ins[t]
            return pltpu.make_async_remote_copy(src_ref=src, dst_ref=dst[t].at[4 * px + 2 * py + pc],
                                                send_sem=send.at[t, kk - 1], recv_sem=recv.at[t, kk - 1],
                                                device_id=(px, py, pc), device_id_type=pl.DeviceIdType.MESH)

        own = [pltpu.make_async_copy(ins[t].at[me] if tensors[t][1] else ins[t], dst[t].at[me], loc.at[t])
               for t in range(n)]
        sent = [copy(t, kk) for t in range(n) for kk in range(1, N_DEV)]
        for cp in own + sent:
            cp.start()
        for t in range(n):
            for kk in range(1, N_DEV):
                arrival(t, kk).wait_recv()
        for cp in sent:
            cp.wait_send()
        for cp in own:
            cp.wait()

    any_spec = pl.BlockSpec(memory_space=pl.ANY)
    return pl.pallas_call(
        body, name=name, in_specs=[any_spec] * n, out_specs=[any_spec] * n, out_shape=outs,
        scratch_shapes=[pltpu.SemaphoreType.DMA((n, N_DEV - 1)), pltpu.SemaphoreType.DMA((n, N_DEV - 1)),
                        pltpu.SemaphoreType.DMA((n,))],
        compiler_params=pltpu.CompilerParams(has_side_effects=True),
    )(*[a for a, _ in tensors])


def _peer_of(x, y, c, kk):
    return (x ^ ((kk >> 2) & 1), y ^ ((kk >> 1) & 1), c ^ (kk & 1))


def _own_slot(a, scatter):
    me = 4 * lax.axis_index("x") + 2 * lax.axis_index("y") + lax.axis_index("c")
    shape = a.shape if scatter else (N_DEV,) + a.shape
    own = lax.dynamic_slice_in_dim(a, me, 1, 0) if scatter else a[None]
    return lax.dynamic_update_slice_in_dim(lax.empty(shape, a.dtype), own, me, 0)


def _exchange_start(name, tensors):
    n = len(tensors)
    hbm = pl.BlockSpec(memory_space=pltpu.HBM)
    sem = pl.BlockSpec(memory_space=pltpu.SEMAPHORE)
    srcs = [pltpu.with_memory_space_constraint(a, pltpu.HBM) for a, _ in tensors]
    lands = [pltpu.with_memory_space_constraint(_own_slot(a, sc), pltpu.HBM) for a, sc in tensors]

    def body(*refs):
        ins, dst = refs[:n], refs[n:2 * n]
        send, recv, token = refs[2 * n], refs[2 * n + 1], refs[4 * n + 2]
        x, y, c = lax.axis_index("x"), lax.axis_index("y"), lax.axis_index("c")
        me = 4 * x + 2 * y + c
        for t in range(n):
            for kk in range(1, N_DEV):
                px, py, pc = _peer_of(x, y, c, kk)
                src = ins[t].at[4 * px + 2 * py + pc] if tensors[t][1] else ins[t]
                k1 = t * (N_DEV - 1) + kk - 1
                pltpu.make_async_remote_copy(src_ref=src, dst_ref=dst[t].at[me], send_sem=send.at[k1],
                                             recv_sem=recv.at[k1], device_id=(px, py, pc),
                                             device_id_type=pl.DeviceIdType.MESH).start()
        token[...] = jnp.zeros_like(token)

    out = pl.pallas_call(
        body, name=name,
        in_specs=[hbm] * (2 * n),
        out_specs=[sem, sem] + [hbm] * (2 * n) + [pl.BlockSpec(memory_space=pltpu.VMEM)],
        out_shape=[pltpu.SemaphoreType.DMA((n * (N_DEV - 1),)), pltpu.SemaphoreType.DMA((n * (N_DEV - 1),))]
        + [pltpu.HBM(a.shape, a.dtype) for a in srcs] + [pltpu.HBM(a.shape, a.dtype) for a in lands]
        + [_sds((8, LANES), F32)],
        input_output_aliases={i: i + 2 for i in range(2 * n)},
        compiler_params=pltpu.CompilerParams(has_side_effects=pltpu.SideEffectType.DATAFLOW_SIDE_EFFECTING),
    )(*srcs, *lands)
    return (out[0], out[1], out[2:2 + n], out[2 + n:2 + 2 * n], [sc for _, sc in tensors]), out[2 + 2 * n]


def _exchange_wait(name, started, after):
    send_s, recv_s, srcs, lands, flags = started
    n = len(srcs)
    hbm = pl.BlockSpec(memory_space=pltpu.HBM)
    sem = pl.BlockSpec(memory_space=pltpu.SEMAPHORE)

    def body(*refs):
        ins, dst, send, recv = refs[:n], refs[n:2 * n], refs[2 * n], refs[2 * n + 1]
        x, y, c = lax.axis_index("x"), lax.axis_index("y"), lax.axis_index("c")
        me = 4 * x + 2 * y + c
        for t in range(n):
            for kk in range(1, N_DEV):
                px, py, pc = _peer_of(x, y, c, kk)
                src = ins[t].at[me] if flags[t] else ins[t]
                k1 = t * (N_DEV - 1) + kk - 1
                cp = pltpu.make_async_remote_copy(src_ref=src, dst_ref=dst[t].at[4 * px + 2 * py + pc],
                                                  send_sem=send.at[k1], recv_sem=recv.at[k1],
                                                  device_id=(px, py, pc), device_id_type=pl.DeviceIdType.MESH)
                cp.wait_send()
                cp.wait_recv()

    out = pl.pallas_call(
        body, name=name,
        in_specs=[hbm] * (2 * n) + [sem, sem, pl.BlockSpec(memory_space=pl.ANY)],
        out_specs=[hbm] * (2 * n),
        out_shape=[pltpu.HBM(a.shape, a.dtype) for a in srcs] + [pltpu.HBM(a.shape, a.dtype) for a in lands],
        input_output_aliases={i: i for i in range(2 * n)},
        compiler_params=pltpu.CompilerParams(has_side_effects=pltpu.SideEffectType.DATAFLOW_SIDE_EFFECTING),
    )(*srcs, *lands, send_s, recv_s, after)
    return out[n:]


def _sum8(name, parts):
    _, R, C = parts.shape

    def fn(r, c, o):
        acc = r[0][0].astype(F32)
        for s in range(1, N_DEV):
            acc = acc + r[0][s].astype(F32)
        o[0][...] = acc
        return ()

    tm = R if R % 256 else 256
    return _rowwise(name, fn, [parts], [], [_sds((R, C), F32)], tm=tm)[0]


def _adamw(name, w, g, m, v):
    bc1 = 1.0 - ADAM_B1 ** ADAM_STEP
    bc2 = 1.0 - ADAM_B2 ** ADAM_STEP

    def fn(r, c, o):
        wv, gv, mv, vv = (x[...] for x in r)
        mn = ADAM_B1 * mv + (1.0 - ADAM_B1) * gv
        vn = ADAM_B2 * vv + (1.0 - ADAM_B2) * (gv * gv)
        o[0][...] = -ADAM_LR * ((mn / bc1) / (jnp.sqrt(vn / bc2) + ADAM_EPS) + ADAM_WD * wv)
        o[1][...] = mn
        o[2][...] = vn
        return ()

    tm = max(8, min(512, 1 << ((16 << 20) // (56 * w.shape[1])).bit_length() - 1))
    return _rowwise(name, fn, [w, g, m, v], [], [_sds(w.shape, F32)] * 3, tm=tm)


def _rope_cs(pos, dim, theta):
    inv = jnp.float32(theta) ** (-jnp.arange(0, dim, 2, dtype=jnp.float32) / dim)
    ang = pos.astype(jnp.float32)[:, None] * inv[None, :]
    return jnp.cos(ang), jnp.sin(ang)


def _tables(S):
    pos = jnp.arange(S)
    c, s = _rope_cs(pos, MLA_ROPE, ROPE_THETA)
    mla = (jnp.concatenate([c, c, c, c], 1), jnp.concatenate([-s, s, -s, s], 1))
    rc, rs = _rope_cs(pos // GRID_W, HD // 2, AXIAL_THETA)
    cc, cs = _rope_cs(pos % GRID_W, HD // 2, AXIAL_THETA)
    axial = (jnp.concatenate([rc, rc, cc, cc], 1), jnp.concatenate([-rs, rs, -cs, cs], 1))
    c, s = _rope_cs(pos, SWA_ROT, ROPE_THETA)
    one, zero = jnp.ones((S, SWA_D - SWA_ROT), F32), jnp.zeros((S, SWA_D - SWA_ROT), F32)
    swa = (jnp.concatenate([c, c, one, c, c, one], 1), jnp.concatenate([-s, s, zero, -s, s, zero], 1))
    return mla, axial, swa


_WEIGHTS = ['even_norm', 'even_w_in', 'mla_q_lat_norm', 'mla_kv_lat_norm', 'mla_w_uq', 'mla_w_ukv', 'mla_q_norm',
            'mla_k_nope_norm', 'mla_k_rope_norm', 'gqa_q_norm', 'gqa_k_norm', 'even_w_out', 'odd_norm', 'odd_w_qkv',
            'swa_q_norm', 'swa_k_norm', 'swa_sink', 'odd_w_out', 'mlp_norm', 'mlp_w_up', 'mlp_w_down']
_SMALL = ['even_norm', 'mla_q_lat_norm', 'mla_kv_lat_norm', 'mla_q_norm', 'mla_k_nope_norm', 'mla_k_rope_norm',
          'gqa_q_norm', 'gqa_k_norm', 'swa_q_norm', 'swa_k_norm', 'swa_sink', 'mlp_norm']


def _relu2(acc):
    rl = jnp.maximum(acc, 0.0)
    return rl * rl, rl


def _mul2(acc, rl):
    return (acc * (2.0 * rl.astype(F32)),)


def _add(acc, res):
    return (acc + res,)


def _step(x, tgt, w, m, v):
    S, D = x.shape
    nw = len(_WEIGHTS)
    tab_mla, tab_ax, tab_swa = _tables(S)
    bf = lambda a: a.astype(BF16)

    w_up_s, w_dn_s = w['mlp_w_up'], w['mlp_w_down']
    gathered = _exchange("gather_first", [
        (bf(w['even_w_in'][0].T), False), (bf(w['mla_w_uq'][0].T), False), (bf(w['mla_w_ukv'][0].T), False)])
    zero = jnp.minimum(jnp.abs(gathered[2][0, 0:1, 0:1].astype(F32)), 0.0)
    later, tok = _exchange_start("gather_rest_start", [
        (bf(w['even_w_out'][0]), False), (bf(w['odd_w_qkv'][0].T), False), (bf(w['odd_w_out'][0]), False),
        (bf(w_up_s[0].T), False), (bf(w_up_s[1].T), False), (bf(w_dn_s[0]), False), (bf(w_dn_s[1]), False),
        (w['odd_norm'] + zero, False)])
    flat = lambda a: a.reshape((a.shape[0] * a.shape[1],) + a.shape[2:])
    win_t, wuq_g, wukv_t = [flat(a) for a in gathered]
    win_t = jnp.concatenate([win_t[:KR_END], jnp.zeros((LANES - MLA_ROPE, D), BF16), win_t[KR_END:]], 0)
    wuq_g = gathered[1]
    wuq_t = jnp.concatenate([wuq_g[:, :HD].reshape(HEADS * HD, Q_LORA),
                             wuq_g[:, HD:].reshape(HEADS * MLA_ROPE, Q_LORA)], 0)

    z64 = jnp.zeros((1, 64), F32)
    qn = w['mla_q_norm']
    g_even = [w['mla_q_lat_norm'], w['mla_kv_lat_norm'], jnp.concatenate([w['mla_k_rope_norm'], z64], 1),
              w['gqa_q_norm'], w['gqa_k_norm']]
    g_mla = [qn[:, :HD], jnp.concatenate([qn[:, HD:], qn[:, HD:]], 1), w['mla_k_nope_norm']]
    g_swa = [jnp.concatenate([w['swa_q_norm']] * 2, 1), jnp.concatenate([w['swa_k_norm']] * 2, 1)]

    def mlp_fwd(l, xin):
        hn = _rmsnorm(f"mlp{l}_norm", xin, w['mlp_norm'][l:l + 1])
        a, rl = _mm(f"mlp{l}_up", hn, wup_t[l], "nt", (BF16, BF16), epilogue=_relu2)
        xout = _mm(f"mlp{l}_down", a, wdn[l], "nn", (F32,), epilogue=_add, extras=(xin,))
        return xout, (hn, a, rl)

    h0 = _rmsnorm("even_norm", x, w['even_norm'] + tok[0:1, 0:1])
    proj = _mm("even_in", h0, win_t, "nt", (F32,), tm=512, tn=P_END)
    cqn, ckvn, krd, qg, kg, vg = _even_prep("even_prep", proj, tab_mla + tab_ax, g_even)
    qa = _mm("mla_uq", cqn, wuq_t, "nt", (F32,))
    kv = _mm("mla_ukv", ckvn, wukv_t, "nt", (F32,))
    q_a, k_a, v_a = _mla_prep("mla_prep", qa, kv, krd, tab_mla, g_mla)
    o_a, lse_a = _flash_fwd("mla_attn", q_a, k_a, v_a)
    o_g, lse_g = _flash_fwd("gqa_attn", qg, kg, vg)
    merged = jnp.concatenate([o_a, o_g], 1)
    rest = _exchange_wait("gather_rest_wait", later, merged)
    wout_e, wqkv_t, wout_o, wup0_t, wup1_t, wdn0, wdn1 = [flat(a) for a in rest[:7]]
    odd_norm = rest[7].reshape(1, D)
    wup_t, wdn = (wup0_t, wup1_t), (wdn0, wdn1)
    x1 = _mm("even_out", merged, wout_e, "nn", (F32,), epilogue=_add, extras=(x,))
    x2, mlp0 = mlp_fwd(0, x1)

    h1 = _rmsnorm("odd_norm", x2, odd_norm)
    qkv = _mm("odd_qkv", h1, wqkv_t, "nt", (F32,))
    q_s, k_s, v_s = _swa_prep("swa_prep", qkv, tab_swa, g_swa)
    o_s, lse_s = _swa_fwd("swa_attn", q_s, k_s, v_s, w['swa_sink'])
    x3 = _mm("odd_out", o_s, wout_o, "nn", (F32,), epilogue=_add, extras=(x2,))
    x4, mlp1 = mlp_fwd(1, x3)

    dy, loss_acc = _loss("loss", x4, tgt)
    loss = lax.psum(0.5 / D * loss_acc[0, 0], ("x", "y", "c"))

    gsm = {}

    def mlp_bwd(l, dout, xin, saved):
        hn, a, rl = saved
        du = _mm(f"mlp{l}_dact", dout, wdn[l], "nt", (BF16,), epilogue=_mul2, extras=(rl,))
        g_dn = _mm(f"mlp{l}_gdown", a, dout, "tn", (BF16,))
        g_up = _mm(f"mlp{l}_gup", du, hn, "tn", (BF16,))
        dhn = _mm(f"mlp{l}_dnorm", du, wup_t[l], "nn", (F32,))
        din, g_n = _rmsnorm_bwd(f"mlp{l}_norm_bwd", dout, xin, dhn, w['mlp_norm'][l:l + 1])
        return din, g_dn, g_up, g_n

    dx3, g_dn1, g_up1, g_mn1 = mlp_bwd(1, dy, x3, mlp1)
    split = lambda a: a.reshape((N_DEV, a.shape[0] // N_DEV) + a.shape[1:])
    sc1, tok1 = _exchange_start("scatter_mlp1_start", [(split(g_up1), True), (split(g_dn1), True)])
    sink = w['swa_sink'] + tok1[0:1, 0:SWA_HEADS]

    g_wout_o = _mm("odd_gout", o_s, dx3, "tn", (BF16,))
    do_s = _mm("odd_dattn", dx3, wout_o, "nt", (BF16,))
    dl_s = _delta("swa_delta", do_s, o_s, SWA_D).reshape(SWA_HEADS, 1, S)
    dq_s, dsink = _swa_bwd_q("swa_bwd_q", q_s, k_s, v_s, do_s, lse_s, dl_s, sink)
    dk_s, dv_s = _swa_bwd_kv("swa_bwd_kv", q_s, k_s, v_s, do_s, lse_s, dl_s)
    dqkv, g_sq, g_sk = _swa_prep_bwd("swa_prep_bwd", dq_s, dk_s, dv_s, qkv, tab_swa, g_swa)
    g_wqkv = _mm("odd_gqkv", dqkv, h1, "tn", (BF16,))
    dh1 = _mm("odd_dnorm", dqkv, wqkv_t, "nn", (F32,))
    dx2, g_on = _rmsnorm_bwd("odd_norm_bwd", dx3, x2, dh1, odd_norm)
    gsm['swa_q_norm'] = g_sq[:, :64] + g_sq[:, 64:]
    gsm['swa_k_norm'] = g_sk[:, :64] + g_sk[:, 64:]
    gsm['swa_sink'] = dsink[:, 0].reshape(1, SWA_HEADS)

    dx1, g_dn0, g_up0, g_mn0 = mlp_bwd(0, dx2, x1, mlp0)
    sc2, tok2 = _exchange_start("scatter_mid_start", [(split(g_wout_o), True), (split(g_wqkv), True),
                                                      (split(g_up0), True), (split(g_dn0), True)])
    gsm['mlp_norm'] = jnp.concatenate([g_mn0, g_mn1], 0)

    g_wout_e = _mm("even_gout", merged, dx1, "tn", (BF16,))
    dmerged = _mm("even_dattn", dx1, wout_e, "nt", (BF16,))
    dl_e = _delta("even_delta", dmerged, merged, HD)
    lse_e = jnp.concatenate([lse_a, lse_g], 0) + tok2[0, 0]
    dq_a, dk_a, dv_a = _flash_bwd("mla_attn_bwd", q_a, k_a, v_a, dmerged, 0, lse_e, dl_e)
    dq_g, dk_g, dv_g = _flash_bwd("gqa_attn_bwd", qg, kg, vg, dmerged, HEADS, lse_e, dl_e)
    dqa, dkv, dkrd, g_qnn, g_qnr, g_kn = _mla_prep_bwd("mla_prep_bwd", dq_a, dk_a, dv_a, qa, kv, tab_mla, g_mla)
    g_wuq = _mm("mla_guq", dqa, cqn, "tn", (BF16,))
    dcqn = _mm("mla_dq_lat", dqa, wuq_t, "nn", (F32,))
    g_wukv = _mm("mla_gukv", dkv, ckvn, "tn", (BF16,))
    dckvn = _mm("mla_dkv_lat", dkv, wukv_t, "nn", (F32,))
    dproj, g_ql, g_kvl, g_kr, g_gq, g_gk = _even_prep_bwd("even_prep_bwd", dcqn, dckvn, dkrd, dq_g, dk_g, dv_g,
                                                          proj, tab_mla + tab_ax, g_even)
    g_win = _mm("even_gin", dproj, h0, "tn", (BF16,), tm=P_END, tk=1024)
    dh0 = _mm("even_dnorm", dproj, win_t, "nn", (F32,), tk=P_END)
    grad_x, g_en = _rmsnorm_bwd("even_norm_bwd", dx1, x, dh0, w['even_norm'])
    gsm.update(even_norm=g_en, mla_q_lat_norm=g_ql, mla_kv_lat_norm=g_kvl,
               mla_q_norm=jnp.concatenate([g_qnn, g_qnr[:, :64] + g_qnr[:, 64:]], 1), mla_k_nope_norm=g_kn,
               mla_k_rope_norm=g_kr[:, :64], gqa_q_norm=g_gq, gqa_k_norm=g_gk)

    g_win = jnp.concatenate([g_win[:KR_END], g_win[P_QG:]], 0)
    g_wuq = jnp.concatenate([g_wuq[:HEADS * HD].reshape(HEADS, HD, Q_LORA),
                             g_wuq[HEADS * HD:].reshape(HEADS, MLA_ROPE, Q_LORA)], 1)
    small_sizes = [w[n].size for n in _SMALL] + [D]
    small_vec = jnp.concatenate([gsm[n].reshape(1, -1) for n in _SMALL] + [g_on], 1)
    pad = (-small_vec.shape[1]) % LANES
    small_vec = jnp.pad(small_vec, ((0, 0), (0, pad)))
    last_st, tok3 = _exchange_start("scatter_last_start", [(split(g_win), True), (g_wuq, True), (split(g_wukv), True),
                                                           (split(g_wout_e), True), (small_vec, False)])
    p_up1, p_dn1 = _exchange_wait("scatter_mlp1_wait", sc1, tok3)
    p_wout_o, p_wqkv, p_up0, p_dn0 = _exchange_wait("scatter_mid_wait", sc2, tok3)
    parts = dict(odd_w_qkv=p_wqkv, odd_w_out=p_wout_o, up0=p_up0, up1=p_up1, dn0=p_dn0, dn1=p_dn1)
    red = {n: _sum8("sum_" + n, p) for n, p in parts.items()}
    grads = {
        'odd_w_qkv': red['odd_w_qkv'].T[None], 'odd_w_out': red['odd_w_out'][None],
        'mlp_w_up': jnp.stack([red['up0'].T, red['up1'].T]), 'mlp_w_down': jnp.stack([red['dn0'], red['dn1']]),
    }
    delta, new_m, new_v = {}, {}, {}

    def apply(names):
        for n in names:
            shp = w[n].shape
            two = lambda a: a.reshape(shp[0] * shp[1], shp[2])
            d_, m_, v_ = _adamw("adamw_" + n, two(w[n]), two(grads[n]), two(m[n]), two(v[n]))
            delta[n], new_m[n], new_v[n] = d_.reshape(shp), m_.reshape(shp), v_.reshape(shp)

    apply(['odd_w_qkv', 'odd_w_out', 'mlp_w_up', 'mlp_w_down'])
    last = _exchange_wait("scatter_last_wait", last_st, new_v['mlp_w_down'])
    for n, p in zip(['even_w_in', 'mla_w_uq', 'mla_w_ukv', 'even_w_out'], last[:4]):
        red[n] = _sum8("sum_" + n, p)
    small_g = _sum8("sum_small", last[4])
    grads.update({'even_w_in': red['even_w_in'].T[None], 'mla_w_uq': red['mla_w_uq'].T[None],
                  'mla_w_ukv': red['mla_w_ukv'].T[None], 'even_w_out': red['even_w_out'][None]})
    apply(['even_w_in', 'mla_w_uq', 'mla_w_ukv', 'even_w_out'])
    off = 0
    for n, sz in zip(_SMALL + ['odd_norm_full'], small_sizes):
        seg = small_g[:, off:off + sz]
        off += sz
        if n == 'odd_norm_full':
            me = 4 * lax.axis_index("x") + 2 * lax.axis_index("y") + lax.axis_index("c")
            grads['odd_norm'] = lax.dynamic_slice(seg, (0, me * (D // N_DEV)), (1, D // N_DEV))
        else:
            grads[n] = seg.reshape(w[n].shape)

    sm_names = _SMALL + ['odd_norm']
    pack = lambda d: jnp.concatenate([d[n].reshape(1, -1) for n in sm_names], 1)
    pw, pg, pm, pv = pack(w), pack(grads), pack(m), pack(v)
    padw = (-pw.shape[1]) % LANES
    padf = lambda a: jnp.pad(a, ((0, 0), (0, padw)))
    d_, m_, v_ = _adamw("adamw_small", padf(pw), padf(pg), padf(pm), jnp.pad(pv, ((0, 0), (0, padw)), constant_values=1.0))
    off = 0
    for n in sm_names:
        sz = w[n].size
        delta[n] = d_[:, off:off + sz].reshape(w[n].shape)
        new_m[n] = m_[:, off:off + sz].reshape(w[n].shape)
        new_v[n] = v_[:, off:off + sz].reshape(w[n].shape)
        off += sz

    return (loss, grad_x[None], *[grads[n] for n in _WEIGHTS], *[delta[n] for n in _WEIGHTS],
            *[new_m[n] for n in _WEIGHTS], *[new_v[n] for n in _WEIGHTS])


def kernel(x, even_norm, even_w_in, mla_q_lat_norm, mla_kv_lat_norm, mla_w_uq, mla_w_ukv, mla_q_norm, mla_k_nope_norm, mla_k_rope_norm, gqa_q_norm, gqa_k_norm, even_w_out, odd_norm, odd_w_qkv, swa_q_norm, swa_k_norm, swa_sink, odd_w_out, mlp_norm, mlp_w_up, mlp_w_down, loss_target, m_even_norm, m_even_w_in, m_mla_q_lat_norm, m_mla_kv_lat_norm, m_mla_w_uq, m_mla_w_ukv, m_mla_q_norm, m_mla_k_nope_norm, m_mla_k_rope_norm, m_gqa_q_norm, m_gqa_k_norm, m_even_w_out, m_odd_norm, m_odd_w_qkv, m_swa_q_norm, m_swa_k_norm, m_swa_sink, m_odd_w_out, m_mlp_norm, m_mlp_w_up, m_mlp_w_down, v_even_norm, v_even_w_in, v_mla_q_lat_norm, v_mla_kv_lat_norm, v_mla_w_uq, v_mla_w_ukv, v_mla_q_norm, v_mla_k_nope_norm, v_mla_k_rope_norm, v_gqa_q_norm, v_gqa_k_norm, v_even_w_out, v_odd_norm, v_odd_w_qkv, v_swa_q_norm, v_swa_k_norm, v_swa_sink, v_odd_w_out, v_mlp_norm, v_mlp_w_up, v_mlp_w_down):
    ws = (even_norm, even_w_in, mla_q_lat_norm, mla_kv_lat_norm, mla_w_uq, mla_w_ukv, mla_q_norm, mla_k_nope_norm, mla_k_rope_norm, gqa_q_norm, gqa_k_norm, even_w_out, odd_norm, odd_w_qkv, swa_q_norm, swa_k_norm, swa_sink, odd_w_out, mlp_norm, mlp_w_up, mlp_w_down)
    ms = (m_even_norm, m_even_w_in, m_mla_q_lat_norm, m_mla_kv_lat_norm, m_mla_w_uq, m_mla_w_ukv, m_mla_q_norm, m_mla_k_nope_norm, m_mla_k_rope_norm, m_gqa_q_norm, m_gqa_k_norm, m_even_w_out, m_odd_norm, m_odd_w_qkv, m_swa_q_norm, m_swa_k_norm, m_swa_sink, m_odd_w_out, m_mlp_norm, m_mlp_w_up, m_mlp_w_down)
    vs = (v_even_norm, v_even_w_in, v_mla_q_lat_norm, v_mla_kv_lat_norm, v_mla_w_uq, v_mla_w_ukv, v_mla_q_norm, v_mla_k_nope_norm, v_mla_k_rope_norm, v_gqa_q_norm, v_gqa_k_norm, v_even_w_out, v_odd_norm, v_odd_w_qkv, v_swa_q_norm, v_swa_k_norm, v_swa_sink, v_odd_w_out, v_mlp_norm, v_mlp_w_up, v_mlp_w_down)
    return _step(x[0], loss_target[0], dict(zip(_WEIGHTS, ws)), dict(zip(_WEIGHTS, ms)), dict(zip(_WEIGHTS, vs)))
```

```python
import functools

import jax
import jax.numpy as jnp
from jax import lax
from jax.experimental import pallas as pl
from jax.experimental.pallas import tpu as pltpu

F32 = jnp.float32
BF16 = jnp.bfloat16

N_DEV = 8
NORM_EPS = 1e-6
ROPE_THETA = 500000.0
AXIAL_THETA = 10000.0
GRID_W = 64
HEADS = 8
GQA_KV = 2
HD = 128
MLA_ROPE = 64
MLA_QK = HD + MLA_ROPE
Q_LORA = 512
KV_LORA = 256
SWA_HEADS = 32
SWA_KV = 4
SWA_D = 64
SWA_ROT = 16
SWA_WINDOW = 128
SWA_BLOCK = 128
MLA_SCALE, GQA_SCALE, SWA_SCALE = MLA_QK ** -0.5, HD ** -0.5, SWA_D ** -0.5
LANES = 128
ADAM_LR, ADAM_B1, ADAM_B2, ADAM_EPS, ADAM_WD, ADAM_STEP = 0.001, 0.9, 0.999, 1e-08, 0.01, 10
VMEM_LIMIT = 56 * 1024 * 1024

P_CQ, P_CKV, P_KR, P_QG = 0, Q_LORA, Q_LORA + KV_LORA, Q_LORA + KV_LORA + LANES
P_KG = P_QG + HEADS * HD
P_VG = P_KG + GQA_KV * HD
P_END = P_VG + GQA_KV * HD
KR_END = Q_LORA + KV_LORA + MLA_ROPE


def _tile(n, prefs):
    for t in prefs:
        if n % t == 0 and t <= n:
            return t
    return n


def _params(sem):
    return pltpu.CompilerParams(dimension_semantics=sem, vmem_limit_bytes=VMEM_LIMIT)


_DIMS = {"nn": ((1,), (0,)), "nt": ((1,), (1,)), "tn": ((0,), (0,))}


def _mm(name, a, b, mode, out_dtypes, epilogue=None, extras=(), tm=1024, tn=1024, tk=2048):
    if mode == "nn":
        (M, K), (_, N) = a.shape, b.shape
    elif mode == "nt":
        (M, K), (N, _) = a.shape, b.shape
    else:
        (K, M), (_, N) = a.shape, b.shape
    tm = _tile(M, (tm, 512, 256, 128))
    tn = _tile(N, (tn, 512, 256, 128))
    tk = _tile(K, (tk, 1024, 512, 256, 128))
    nk = K // tk
    ne, no = len(extras), len(out_dtypes)
    if mode == "tn":
        a_spec = pl.BlockSpec((tk, tm), lambda i, j, k: (k, i))
    else:
        a_spec = pl.BlockSpec((tm, tk), lambda i, j, k: (i, k))
    if mode == "nt":
        b_spec = pl.BlockSpec((tn, tk), lambda i, j, k: (j, k))
    else:
        b_spec = pl.BlockSpec((tk, tn), lambda i, j, k: (k, j))
    o_spec = pl.BlockSpec((tm, tn), lambda i, j, k: (i, j))
    dims = (_DIMS[mode], ((), ()))

    def body(a_ref, b_ref, *rest):
        ex, outs = rest[:ne], rest[ne:ne + no]
        k = pl.program_id(2)
        part = lax.dot_general(a_ref[...].astype(BF16), b_ref[...].astype(BF16), dims, preferred_element_type=F32)

        def finish(total):
            res = epilogue(total, *[e[...] for e in ex]) if epilogue else (total,)
            for o, r in zip(outs, res):
                o[...] = r.astype(o.dtype)

        if nk == 1:
            finish(part)
            return
        acc = rest[ne + no]

        @pl.when(k == 0)
        def _():
            acc[...] = part

        @pl.when((k > 0) & (k < nk - 1))
        def _():
            acc[...] += part

        @pl.when(k == nk - 1)
        def _():
            finish(acc[...] + part)

    out = pl.pallas_call(
        body, name=name, grid=(M // tm, N // tn, nk),
        in_specs=[a_spec, b_spec] + [o_spec] * ne,
        out_specs=[o_spec] * no,
        out_shape=[jax.ShapeDtypeStruct((M, N), d) for d in out_dtypes],
        scratch_shapes=[pltpu.VMEM((tm, tn), F32)] if nk > 1 else [],
        compiler_params=_params(("parallel", "parallel", "arbitrary")),
    )(a, b, *extras)
    return out[0] if no == 1 else out


def _rowwise(name, fn, rows, consts, outs, accs=(), tm=256):
    S = rows[0].shape[-2]
    tm = _tile(S, (tm, 128, 64, 32, 16, 8))
    nr, nc, no, na = len(rows), len(consts), len(outs), len(accs)

    def rspec(shape):
        if len(shape) == 2:
            return pl.BlockSpec((tm, shape[1]), lambda i: (i, 0))
        return pl.BlockSpec((shape[0], tm, shape[2]), lambda i: (0, i, 0))

    def cspec(shape):
        return pl.BlockSpec(tuple(shape), lambda i: (0,) * len(shape))

    def body(*refs):
        r, c = refs[:nr], refs[nr:nr + nc]
        o, a = refs[nr + nc:nr + nc + no], refs[nr + nc + no:]
        vals = fn(r, c, o)
        if na:
            @pl.when(pl.program_id(0) == 0)
            def _():
                for ar in a:
                    ar[...] = jnp.zeros_like(ar)

            for ar, v in zip(a, vals):
                ar[...] += v

    res = pl.pallas_call(
        body, name=name, grid=(S // tm,),
        in_specs=[rspec(x.shape) for x in rows] + [cspec(x.shape) for x in consts],
        out_specs=[rspec(x.shape) for x in outs] + [cspec(x.shape) for x in accs],
        out_shape=list(outs) + list(accs),
        compiler_params=_params(("arbitrary",) if na else ("parallel",)),
    )(*rows, *consts)
    return res


def _sds(shape, dtype):
    return jax.ShapeDtypeStruct(tuple(shape), dtype)


def _rsum(x):
    return jnp.sum(x, axis=-1, keepdims=True)


def _csum(x):
    return jnp.sum(x, axis=0, keepdims=True)


def _lane(shape):
    return lax.broadcasted_iota(jnp.int32, shape, 1)


def _partner(x, seg, half):
    lane = _lane(x.shape) % seg
    return jnp.where(lane < half, pltpu.roll(x, LANES - half, 1), pltpu.roll(x, half, 1))


def _rope(x, c, s, seg, half):
    return x * c + _partner(x, seg, half) * s


def _rope_bwd(dy, c, s, seg, half):
    t = _partner(dy * s, seg, half)
    if seg != 2 * half:
        t = jnp.where(_lane(dy.shape) % seg < 2 * half, t, 0.0)
    return dy * c + t


def _rms_r(ss, n):
    return lax.rsqrt(ss * (1.0 / n) + NORM_EPS)


def _rms_bwd(x, r, g, dy, dot_scale):
    dyg = dy * g
    return r * dyg - x * (r * r * r) * (_rsum(dyg * x) * dot_scale)


def _rmsnorm(name, x, g):
    D = x.shape[1]

    def fn(r, c, o):
        xv = r[0][...]
        o[0][...] = (xv * _rms_r(_rsum(xv * xv), D) * c[0][...]).astype(BF16)
        return ()

    return _rowwise(name, fn, [x], [g], [_sds(x.shape, BF16)])[0]


def _rmsnorm_bwd(name, dres, x, dh, g):
    D = x.shape[1]

    def fn(r, c, o):
        xv, dhv, gv = r[1][...], r[2][...], c[0][...]
        rr = _rms_r(_rsum(xv * xv), D)
        o[0][...] = r[0][...] + _rms_bwd(xv, rr, gv, dhv, 1.0 / D)
        return (_csum(dhv * xv * rr),)

    return _rowwise(name, fn, [dres, x, dh], [g], [_sds(x.shape, F32)], [_sds((1, D), F32)])


def _loss(name, y, t):
    D = y.shape[1]

    def fn(r, c, o):
        e = r[0][...] - r[1][...]
        o[0][...] = e * (1.0 / D)
        return (_csum(_rsum(e * e)),)

    return _rowwise(name, fn, [y, t], [], [_sds(y.shape, F32)], [_sds((1, 1), F32)])


def _even_prep(name, proj, tabs, gains):
    S = proj.shape[0]
    cm, sm, ca, sa = tabs

    def fn(r, c, o):
        p, cmv, smv, cav, sav = r[0], r[1][...], r[2][...], r[3][...], r[4][...]
        g_ql, g_kvl, g_kr, g_q, g_k = (x[...] for x in c)
        cq = p[:, P_CQ:P_CKV]
        o[0][...] = (cq * _rms_r(_rsum(cq * cq), Q_LORA) * g_ql).astype(BF16)
        ckv = p[:, P_CKV:P_KR]
        o[1][...] = (ckv * _rms_r(_rsum(ckv * ckv), KV_LORA) * g_kvl).astype(BF16)
        kr = p[:, P_KR:P_QG]
        y = _rope(kr * _rms_r(_rsum(kr * kr), MLA_ROPE) * g_kr, cmv, smv, 64, 32)
        o[2][...] = (y + pltpu.roll(y, 64, 1)).astype(BF16)
        for h in range(HEADS):
            xh = p[:, P_QG + HD * h:P_QG + HD * (h + 1)]
            o[3][h] = (_rope(xh * _rms_r(_rsum(xh * xh), HD) * g_q, cav, sav, 64, 32) * GQA_SCALE).astype(BF16)
        for h in range(GQA_KV):
            xh = p[:, P_KG + HD * h:P_KG + HD * (h + 1)]
            o[4][h] = _rope(xh * _rms_r(_rsum(xh * xh), HD) * g_k, cav, sav, 64, 32).astype(BF16)
            o[5][h] = p[:, P_VG + HD * h:P_VG + HD * (h + 1)].astype(BF16)
        return ()

    outs = [_sds((S, Q_LORA), BF16), _sds((S, KV_LORA), BF16), _sds((S, LANES), BF16),
            _sds((HEADS, S, HD), BF16), _sds((GQA_KV, S, HD), BF16), _sds((GQA_KV, S, HD), BF16)]
    return _rowwise(name, fn, [proj, cm, sm, ca, sa], gains, outs)


def _even_prep_bwd(name, dcqn, dckvn, dkrd, dqg, dkg, dvg, proj, tabs, gains):
    S = proj.shape[0]
    cm, sm, ca, sa = tabs

    def fn(r, c, o):
        p, cmv, smv, cav, sav = r[6], r[7][...], r[8][...], r[9][...], r[10][...]
        g_ql, g_kvl, g_kr, g_q, g_k = (x[...] for x in c)
        out = o[0]
        cq = p[:, P_CQ:P_CKV]
        rr = _rms_r(_rsum(cq * cq), Q_LORA)
        d = r[0][...]
        out[:, P_CQ:P_CKV] = _rms_bwd(cq, rr, g_ql, d, 1.0 / Q_LORA).astype(BF16)
        a_ql = _csum(d * cq * rr)
        ckv = p[:, P_CKV:P_KR]
        rr = _rms_r(_rsum(ckv * ckv), KV_LORA)
        d = r[1][...]
        out[:, P_CKV:P_KR] = _rms_bwd(ckv, rr, g_kvl, d, 1.0 / KV_LORA).astype(BF16)
        a_kvl = _csum(d * ckv * rr)
        kr = p[:, P_KR:P_QG]
        rr = _rms_r(_rsum(kr * kr), MLA_ROPE)
        d = r[2][...]
        d = d + pltpu.roll(d, 64, 1)
        d = _rope_bwd(d, cmv, smv, 64, 32)
        low = _lane(d.shape) < 64
        out[:, P_KR:P_QG] = jnp.where(low, _rms_bwd(kr, rr, g_kr, d, 1.0 / MLA_ROPE), 0.0).astype(BF16)
        a_kr = _csum(d * kr * rr)
        a_q = jnp.zeros((1, HD), F32)
        for h in range(HEADS):
            xh = p[:, P_QG + HD * h:P_QG + HD * (h + 1)]
            rr = _rms_r(_rsum(xh * xh), HD)
            d = _rope_bwd(r[3][h] * GQA_SCALE, cav, sav, 64, 32)
            out[:, P_QG + HD * h:P_QG + HD * (h + 1)] = _rms_bwd(xh, rr, g_q, d, 1.0 / HD).astype(BF16)
            a_q = a_q + _csum(d * xh * rr)
        a_k = jnp.zeros((1, HD), F32)
        grp = HEADS // GQA_KV
        for h in range(GQA_KV):
            xh = p[:, P_KG + HD * h:P_KG + HD * (h + 1)]
            rr = _rms_r(_rsum(xh * xh), HD)
            d = r[4][grp * h]
            dv = r[5][grp * h]
            for e in range(1, grp):
                d = d + r[4][grp * h + e]
                dv = dv + r[5][grp * h + e]
            d = _rope_bwd(d, cav, sav, 64, 32)
            out[:, P_KG + HD * h:P_KG + HD * (h + 1)] = _rms_bwd(xh, rr, g_k, d, 1.0 / HD).astype(BF16)
            out[:, P_VG + HD * h:P_VG + HD * (h + 1)] = dv.astype(BF16)
            a_k = a_k + _csum(d * xh * rr)
        return (a_ql, a_kvl, a_kr, a_q, a_k)

    accs = [_sds((1, Q_LORA), F32), _sds((1, KV_LORA), F32), _sds((1, LANES), F32),
            _sds((1, HD), F32), _sds((1, HD), F32)]
    return _rowwise(name, fn, [dcqn, dckvn, dkrd, dqg, dkg, dvg, proj, cm, sm, ca, sa], gains,
                    [_sds((S, P_END), BF16)], accs, tm=128)


def _mla_q_parts(qa, h, rp, g_n, g_r):
    nope = qa[:, HD * h:HD * (h + 1)]
    mine = (_lane(rp.shape) >= 64) == bool(h % 2)
    rpm = jnp.where(mine, rp, 0.0)
    rr = _rms_r(_rsum(nope * nope) + _rsum(rpm * rpm), MLA_QK)
    return nope, rpm, mine, rr


def _mla_prep(name, qa, kv, krd, tabs, gains):
    S = qa.shape[0]
    cm, sm = tabs

    def fn(r, c, o):
        qa_r, kv_r, krd_v, cmv, smv = r[0], r[1], r[2][...], r[3][...], r[4][...]
        g_n, g_r, g_kn = (x[...] for x in c)
        for h in range(HEADS):
            rp = qa_r[:, HEADS * HD + LANES * (h // 2):HEADS * HD + LANES * (h // 2 + 1)]
            nope, rpm, mine, rr = _mla_q_parts(qa_r, h, rp, g_n, g_r)
            o[0][h, :, 0:HD] = (nope * (rr * MLA_SCALE) * g_n).astype(BF16)
            o[0][h, :, HD:2 * HD] = _rope(rpm * (rr * MLA_SCALE) * g_r, cmv, smv, 64, 32).astype(BF16)
            kn = kv_r[:, 2 * HD * h:2 * HD * h + HD]
            o[1][h, :, 0:HD] = (kn * _rms_r(_rsum(kn * kn), HD) * g_kn).astype(BF16)
            o[1][h, :, HD:2 * HD] = krd_v
            o[2][h] = kv_r[:, 2 * HD * h + HD:2 * HD * (h + 1)].astype(BF16)
        return ()

    outs = [_sds((HEADS, S, 2 * HD), BF16), _sds((HEADS, S, 2 * HD), BF16), _sds((HEADS, S, HD), BF16)]
    return _rowwise(name, fn, [qa, kv, krd, cm, sm], gains, outs)


def _mla_prep_bwd(name, dq, dk, dv, qa, kv, tabs, gains):
    S = qa.shape[0]
    cm, sm = tabs

    def fn(r, c, o):
        dq_r, dk_r, dv_r, qa_r, kv_r, cmv, smv = r[0], r[1], r[2], r[3], r[4], r[5][...], r[6][...]
        g_n, g_r, g_kn = (x[...] for x in c)
        a_n = jnp.zeros((1, HD), F32)
        a_r = jnp.zeros((1, LANES), F32)
        a_kn = jnp.zeros((1, HD), F32)
        dkrd = jnp.zeros(cmv.shape, F32)
        drp = None
        for h in range(HEADS):
            rp = qa_r[:, HEADS * HD + LANES * (h // 2):HEADS * HD + LANES * (h // 2 + 1)]
            nope, rpm, mine, rr = _mla_q_parts(qa_r, h, rp, g_n, g_r)
            dn = dq_r[h, :, 0:HD] * MLA_SCALE
            dr = _rope_bwd(jnp.where(mine, dq_r[h, :, HD:2 * HD] * MLA_SCALE, 0.0), cmv, smv, 64, 32)
            dot = (_rsum(dn * g_n * nope) + _rsum(dr * g_r * rpm)) * (1.0 / MLA_QK)
            r3 = rr * rr * rr
            o[0][:, HD * h:HD * (h + 1)] = (rr * dn * g_n - nope * r3 * dot).astype(BF16)
            part = jnp.where(mine, rr * dr * g_r - rpm * r3 * dot, 0.0)
            drp = part if h % 2 == 0 else drp + part
            if h % 2 == 1:
                o[0][:, HEADS * HD + LANES * (h // 2):HEADS * HD + LANES * (h // 2 + 1)] = drp.astype(BF16)
            a_n = a_n + _csum(dn * nope * rr)
            a_r = a_r + _csum(dr * rpm * rr)
            kn = kv_r[:, 2 * HD * h:2 * HD * h + HD]
            rk = _rms_r(_rsum(kn * kn), HD)
            dkn = dk_r[h, :, 0:HD]
            o[1][:, 2 * HD * h:2 * HD * h + HD] = _rms_bwd(kn, rk, g_kn, dkn, 1.0 / HD).astype(BF16)
            o[1][:, 2 * HD * h + HD:2 * HD * (h + 1)] = dv_r[h].astype(BF16)
            a_kn = a_kn + _csum(dkn * kn * rk)
            dkrd = dkrd + dk_r[h, :, HD:2 * HD]
        o[2][...] = dkrd
        return (a_n, a_r, a_kn)

    outs = [_sds(qa.shape, BF16), _sds(kv.shape, BF16), _sds((S, LANES), F32)]
    accs = [_sds((1, HD), F32), _sds((1, LANES), F32), _sds((1, HD), F32)]
    return _rowwise(name, fn, [dq, dk, dv, qa, kv, cm, sm], gains, outs, accs, tm=128)


def _seg_ones(seg):
    r = lax.broadcasted_iota(jnp.int32, (LANES, LANES), 0) // seg
    c = lax.broadcasted_iota(jnp.int32, (LANES, LANES), 1) // seg
    return (r == c).astype(F32)


def _seg_sum(x, ones):
    return lax.dot_general(x, ones, _NN, preferred_element_type=F32, precision=lax.Precision.HIGHEST)


def _seg64_r(x, ones):
    return _rms_r(_seg_sum(x * x, ones), SWA_D)


def _swa_prep(name, qkv, tabs, gains):
    S = qkv.shape[0]
    nq, nk = SWA_HEADS * SWA_D, SWA_KV * SWA_D
    cs, ss = tabs

    def fn(r, c, o):
        x_r, csv, ssv = r[0], r[1][...], r[2][...]
        g_q, g_k = c[0][...], c[1][...]
        ones = _seg_ones(SWA_D)
        for g in range((nq + nk) // LANES):
            x = x_r[:, LANES * g:LANES * (g + 1)]
            rr = _seg64_r(x, ones)
            y = _rope(x * rr * (g_q if g < nq // LANES else g_k), csv, ssv, SWA_D, SWA_ROT // 2)
            if g < nq // LANES:
                o[0][:, LANES * g:LANES * (g + 1)] = (y * SWA_SCALE).astype(BF16)
            else:
                o[1][:, LANES * g - nq:LANES * (g + 1) - nq] = y.astype(BF16)
        o[2][...] = x_r[:, nq + nk:nq + 2 * nk].astype(BF16)
        return ()

    outs = [_sds((S, nq), BF16), _sds((S, nk), BF16), _sds((S, nk), BF16)]
    return _rowwise(name, fn, [qkv, cs, ss], gains, outs)


def _swa_prep_bwd(name, dq, dk, dv, qkv, tabs, gains):
    nq, nk = SWA_HEADS * SWA_D, SWA_KV * SWA_D
    cs, ss = tabs

    def fn(r, c, o):
        dq_r, dk_r, x_r, csv, ssv = r[0], r[1], r[3], r[4][...], r[5][...]
        g_q, g_k = c[0][...], c[1][...]
        acc = [jnp.zeros((1, LANES), F32), jnp.zeros((1, LANES), F32)]
        ones = _seg_ones(SWA_D)
        for g in range((nq + nk) // LANES):
            isq = g < nq // LANES
            x = x_r[:, LANES * g:LANES * (g + 1)]
            rr = _seg64_r(x, ones)
            d = dq_r[:, LANES * g:LANES * (g + 1)] * SWA_SCALE if isq else dk_r[:, LANES * g - nq:LANES * (g + 1) - nq]
            d = _rope_bwd(d, csv, ssv, SWA_D, SWA_ROT // 2)
            dyg = d * (g_q if isq else g_k)
            dot = _seg_sum(dyg * x, ones) * (1.0 / SWA_D)
            o[0][:, LANES * g:LANES * (g + 1)] = (rr * dyg - x * (rr * rr * rr) * dot).astype(BF16)
            acc[0 if isq else 1] = acc[0 if isq else 1] + _csum(d * x * rr)
        o[0][:, nq + nk:nq + 2 * nk] = r[2][...].astype(BF16)
        return tuple(acc)

    return _rowwise(name, fn, [dq, dk, dv, qkv, cs, ss], gains, [_sds(qkv.shape, BF16)],
                    [_sds((1, LANES), F32), _sds((1, LANES), F32)], tm=128)


def _delta(name, do, o_, width):
    S, C = do.shape
    nh = C // width

    def fn(r, c, o):
        for g in range(C // LANES):
            t = r[0][:, LANES * g:LANES * (g + 1)].astype(F32) * r[1][:, LANES * g:LANES * (g + 1)].astype(F32)
            if width == LANES:
                o[0][g] = _rsum(t)
            else:
                low = _lane(t.shape) < 64
                o[0][2 * g] = _rsum(jnp.where(low, t, 0.0))
                o[0][2 * g + 1] = _rsum(jnp.where(low, 0.0, t))
        return ()

    return _rowwise(name, fn, [do, o_], [], [_sds((nh, S, 1), F32)])[0]


_NT = (((1,), (1,)), ((), ()))
_NN = (((1,), (0,)), ((), ()))
_TN = (((0,), (0,)), ((), ()))


def _flash_fwd(name, q, k, v, tq=1024, tk=4096):
    H, S, dk = q.shape
    G = H // k.shape[0]
    dv = v.shape[2]
    tq, tk = _tile(S, (tq, 256, 128)), _tile(S, (tk, 256, 128))
    nk = S // tk

    def body(q_ref, k_ref, v_ref, o_ref, lse_ref, m_s, l_s, acc_s):
        j = pl.program_id(2)

        @pl.when(j == 0)
        def _():
            m_s[...] = jnp.full_like(m_s, -jnp.inf)
            l_s[...] = jnp.zeros_like(l_s)
            acc_s[...] = jnp.zeros_like(acc_s)

        s = lax.dot_general(q_ref[...], k_ref[...], _NT, preferred_element_type=F32)
        m_new = jnp.maximum(m_s[...], jnp.max(s, axis=-1, keepdims=True))
        alpha = jnp.exp(m_s[...] - m_new)
        p = jnp.exp(s - m_new)
        l_s[...] = alpha * l_s[...] + _rsum(p)
        acc_s[...] = alpha * acc_s[...] + lax.dot_general(p.astype(BF16), v_ref[...], _NN,
                                                          preferred_element_type=F32)
        m_s[...] = m_new

        @pl.when(j == nk - 1)
        def _():
            o_ref[...] = (acc_s[...] / l_s[...]).astype(o_ref.dtype)
            lse_ref[...] = m_s[...] + jnp.log(l_s[...])

    return pl.pallas_call(
        body, name=name, grid=(H, S // tq, nk),
        in_specs=[pl.BlockSpec((None, tq, dk), lambda h, i, j: (h, i, 0)),
                  pl.BlockSpec((None, tk, dk), lambda h, i, j: (h // G, j, 0)),
                  pl.BlockSpec((None, tk, dv), lambda h, i, j: (h // G, j, 0))],
        out_specs=[pl.BlockSpec((tq, dv), lambda h, i, j: (i, h)),
                   pl.BlockSpec((None, tq, 1), lambda h, i, j: (h, i, 0))],
        out_shape=[_sds((S, H * dv), BF16), _sds((H, S, 1), F32)],
        scratch_shapes=[pltpu.VMEM((tq, 1), F32), pltpu.VMEM((tq, 1), F32), pltpu.VMEM((tq, dv), F32)],
        compiler_params=_params(("parallel", "parallel", "arbitrary")),
    )(q, k, v)


def _flash_bwd(name, q, k, v, do, head0, lse, delta, tq=1024, tk=1024):
    H, S, dk = q.shape
    G = H // k.shape[0]
    dv = v.shape[2]
    tq, tk = _tile(S, (tq, 256, 128)), _tile(S, (tk, 256, 128))
    nq = S // tq

    def body(q_ref, k_ref, v_ref, do_ref, lse_ref, dl_ref, dq_ref, dk_ref, dv_ref, dk_s, dv_s):
        j, i = pl.program_id(1), pl.program_id(2)

        @pl.when(i == 0)
        def _():
            dk_s[...] = jnp.zeros_like(dk_s)
            dv_s[...] = jnp.zeros_like(dv_s)

        qv, kv_, dov = q_ref[...], k_ref[...], do_ref[...]
        s = lax.dot_general(qv, kv_, _NT, preferred_element_type=F32)
        p = jnp.exp(s - lse_ref[...])
        dp = lax.dot_general(dov, v_ref[...], _NT, preferred_element_type=F32)
        ds = (p * (dp - dl_ref[...])).astype(BF16)
        dv_s[...] += lax.dot_general(p.astype(BF16), dov, _TN, preferred_element_type=F32)
        dk_s[...] += lax.dot_general(ds, qv, _TN, preferred_element_type=F32)
        dqi = lax.dot_general(ds, kv_, _NN, preferred_element_type=F32)
        rows = pl.ds(pl.multiple_of(i * tq, tq), tq)

        @pl.when(j == 0)
        def _():
            dq_ref[rows, :] = dqi

        @pl.when(j > 0)
        def _():
            dq_ref[rows, :] += dqi

        @pl.when(i == nq - 1)
        def _():
            dk_ref[...] = dk_s[...]
            dv_ref[...] = dv_s[...]

    return pl.pallas_call(
        body, name=name, grid=(H, S // tk, nq),
        in_specs=[pl.BlockSpec((None, tq, dk), lambda h, j, i: (h, i, 0)),
                  pl.BlockSpec((None, tk, dk), lambda h, j, i: (h // G, j, 0)),
                  pl.BlockSpec((None, tk, dv), lambda h, j, i: (h // G, j, 0)),
                  pl.BlockSpec((tq, dv), lambda h, j, i: (i, head0 + h)),
                  pl.BlockSpec((None, tq, 1), lambda h, j, i: (head0 + h, i, 0)),
                  pl.BlockSpec((None, tq, 1), lambda h, j, i: (head0 + h, i, 0))],
        out_specs=[pl.BlockSpec((None, S, dk), lambda h, j, i: (h, 0, 0)),
                   pl.BlockSpec((None, tk, dk), lambda h, j, i: (h, j, 0)),
                   pl.BlockSpec((None, tk, dv), lambda h, j, i: (h, j, 0))],
        out_shape=[_sds((H, S, dk), F32), _sds((H, S, dk), F32), _sds((H, S, dv), F32)],
        scratch_shapes=[pltpu.VMEM((tk, dk), F32), pltpu.VMEM((tk, dv), F32)],
        compiler_params=_params(("parallel", "arbitrary", "arbitrary")),
    )(q, k, v, do, lse, delta)


def _swa_place(ref128, h):
    e, t = h % 2, (h // (SWA_HEADS // SWA_KV)) % 2
    x = ref128.astype(F32)
    if e != t:
        x = pltpu.roll(x, 64, 1)
    return jnp.where((_lane(x.shape) >= 64) == bool(t), x, 0.0).astype(BF16)


def _swa_unplace(y, h):
    e, t = h % 2, (h // (SWA_HEADS // SWA_KV)) % 2
    return pltpu.roll(y, 64, 1) if e != t else y


def _swa_specs(width, nb):
    prev = pl.BlockSpec((SWA_BLOCK, width), lambda i: (jnp.maximum(i - 1, 0), 0))
    cur = pl.BlockSpec((SWA_BLOCK, width), lambda i: (i, 0))
    nxt = pl.BlockSpec((SWA_BLOCK, width), lambda i: (jnp.minimum(i + 1, nb - 1), 0))
    return [prev, cur, nxt]


def _swa_valid_t(i, S, reps):
    shape = (3 * SWA_BLOCK, reps * SWA_BLOCK)
    kpos = (i - 1) * SWA_BLOCK + lax.broadcasted_iota(jnp.int32, shape, 0)
    qpos = i * SWA_BLOCK + lax.broadcasted_iota(jnp.int32, shape, 1) % SWA_BLOCK
    return (jnp.abs(qpos - kpos) <= SWA_WINDOW) & (kpos >= 0) & (kpos < S)


def _swa_stack(refs, heads):
    return jnp.concatenate([_swa_place(r[:, LANES * (h // 2):LANES * (h // 2 + 1)], h) for h in heads for r in refs],
                           axis=0)


def _swa_unstack_t(ot, j, out_ref):
    grp = SWA_HEADS // SWA_KV
    for pair in range(grp // 2):
        h = grp * j + 2 * pair
        a = _swa_unplace(ot[:, SWA_BLOCK * 2 * pair:SWA_BLOCK * (2 * pair + 1)].T, h)
        b = _swa_unplace(ot[:, SWA_BLOCK * (2 * pair + 1):SWA_BLOCK * (2 * pair + 2)].T, h + 1)
        out_ref[:, LANES * (h // 2):LANES * (h // 2 + 1)] = jnp.where(_lane(a.shape) < 64, a, b).astype(out_ref.dtype)


def _swa_fwd(name, q, k, v, sink):
    S = q.shape[0]
    nb = S // SWA_BLOCK
    grp = SWA_HEADS // SWA_KV
    smem = pl.BlockSpec(memory_space=pltpu.SMEM)

    def body(sink_ref, q_ref, kp, kc, kn, vp, vc, vn, o_ref, lse_ref):
        i = pl.program_id(0)
        valid = _swa_valid_t(i, S, grp)
        kcat = [jnp.concatenate([r[:, LANES * u:LANES * (u + 1)] for r in (kp, kc, kn)], axis=0) for u in range(2)]
        vcat = [jnp.concatenate([r[:, LANES * u:LANES * (u + 1)] for r in (vp, vc, vn)], axis=0) for u in range(2)]
        for j in range(SWA_KV):
            heads = range(grp * j, grp * (j + 1))
            xq = _swa_stack([q_ref], heads)
            sk = jnp.concatenate([jnp.full((1, SWA_BLOCK), sink_ref[0, h], F32) for h in heads], axis=1)
            st = lax.dot_general(kcat[j // 2], xq, _NT, preferred_element_type=F32)
            st = jnp.where(valid, st, -jnp.inf)
            m = jnp.maximum(jnp.max(st, axis=0, keepdims=True), sk)
            pt = jnp.exp(st - m)
            den = jnp.sum(pt, axis=0, keepdims=True) + jnp.exp(sk - m)
            ot = lax.dot_general(vcat[j // 2], pt.astype(BF16), _TN, preferred_element_type=F32) * (1.0 / den)
            lse = m + jnp.log(den)
            for e, h in enumerate(heads):
                lse_ref[h] = lse[:, SWA_BLOCK * e:SWA_BLOCK * (e + 1)]
            _swa_unstack_t(ot, j, o_ref)

    return pl.pallas_call(
        body, name=name, grid=(nb,),
        in_specs=[smem, pl.BlockSpec((SWA_BLOCK, q.shape[1]), lambda i: (i, 0))]
        + _swa_specs(k.shape[1], nb) + _swa_specs(v.shape[1], nb),
        out_specs=[pl.BlockSpec((SWA_BLOCK, q.shape[1]), lambda i: (i, 0)),
                   pl.BlockSpec((SWA_HEADS, 1, SWA_BLOCK), lambda i: (0, 0, i))],
        out_shape=[_sds(q.shape, BF16), _sds((SWA_HEADS, 1, S), F32)],
        compiler_params=_params(("parallel",)),
    )(sink, q, k, k, k, v, v, v)


def _swa_bwd_q(name, q, k, v, do, lse_t, delta_t, sink):
    S = q.shape[0]
    nb = S // SWA_BLOCK
    grp = SWA_HEADS // SWA_KV
    smem = pl.BlockSpec(memory_space=pltpu.SMEM)
    row = pl.BlockSpec((SWA_BLOCK, q.shape[1]), lambda i: (i, 0))
    hrow = pl.BlockSpec((SWA_HEADS, 1, SWA_BLOCK), lambda i: (0, 0, i))

    def body(sink_ref, q_ref, do_ref, lse_ref, dl_ref, kp, kc, kn, vp, vc, vn, dq_ref, ds_ref):
        i = pl.program_id(0)

        @pl.when(i == 0)
        def _():
            ds_ref[...] = jnp.zeros_like(ds_ref)

        valid = _swa_valid_t(i, S, grp)
        kcat = [jnp.concatenate([r[:, LANES * u:LANES * (u + 1)] for r in (kp, kc, kn)], axis=0) for u in range(2)]
        vcat = [jnp.concatenate([r[:, LANES * u:LANES * (u + 1)] for r in (vp, vc, vn)], axis=0) for u in range(2)]
        for j in range(SWA_KV):
            heads = range(grp * j, grp * (j + 1))
            xq = _swa_stack([q_ref], heads)
            xdo = _swa_stack([do_ref], heads)
            lse_r = jnp.concatenate([lse_ref[h] for h in heads], axis=1)
            dl_r = jnp.concatenate([dl_ref[h] for h in heads], axis=1)
            st = lax.dot_general(kcat[j // 2], xq, _NT, preferred_element_type=F32)
            pt = jnp.exp(jnp.where(valid, st, -jnp.inf) - lse_r)
            dpt = lax.dot_general(vcat[j // 2], xdo, _NT, preferred_element_type=F32)
            dst = (pt * (dpt - dl_r)).astype(BF16)
            _swa_unstack_t(lax.dot_general(kcat[j // 2], dst, _TN, preferred_element_type=F32), j, dq_ref)
            for h in heads:
                dsink = -_rsum(jnp.exp(sink_ref[0, h] - lse_ref[h]) * dl_ref[h])
                ds_ref[h:h + 1, :] += jnp.broadcast_to(dsink, (1, LANES))

    return pl.pallas_call(
        body, name=name, grid=(nb,),
        in_specs=[smem, row, row, hrow, hrow] + _swa_specs(k.shape[1], nb) + _swa_specs(v.shape[1], nb),
        out_specs=[row, pl.BlockSpec((SWA_HEADS, LANES), lambda i: (0, 0))],
        out_shape=[_sds(q.shape, F32), _sds((SWA_HEADS, LANES), F32)],
        compiler_params=_params(("arbitrary",)),
    )(sink, q, do, lse_t, delta_t, k, k, k, v, v, v)


def _swa_bwd_kv(name, q, k, v, do, lse_t, delta_t):
    S = q.shape[0]
    nb = S // SWA_BLOCK
    grp = SWA_HEADS // SWA_KV
    krow = pl.BlockSpec((SWA_BLOCK, k.shape[1]), lambda i: (i, 0))

    def stat3():
        return [pl.BlockSpec((SWA_HEADS, 1, SWA_BLOCK), lambda i: (0, 0, jnp.maximum(i - 1, 0))),
                pl.BlockSpec((SWA_HEADS, 1, SWA_BLOCK), lambda i: (0, 0, i)),
                pl.BlockSpec((SWA_HEADS, 1, SWA_BLOCK), lambda i: (0, 0, jnp.minimum(i + 1, nb - 1)))]

    def body(qp, qc, qn, dop, doc, don, lp, lc, ln, dp_, dc_, dn_, k_ref, v_ref, dk_ref, dv_ref):
        j = pl.program_id(0)
        nh = 2 * grp
        shape = (SWA_BLOCK, nh * 3 * SWA_BLOCK)
        qpos = (j - 1) * SWA_BLOCK + lax.broadcasted_iota(jnp.int32, shape, 1) % (3 * SWA_BLOCK)
        kpos = j * SWA_BLOCK + lax.broadcasted_iota(jnp.int32, shape, 0)
        valid = (jnp.abs(qpos - kpos) <= SWA_WINDOW) & (qpos >= 0) & (qpos < S)
        for u in range(SWA_KV // 2):
            heads = range(nh * u, nh * (u + 1))
            kc = k_ref[:, LANES * u:LANES * (u + 1)]
            vc = v_ref[:, LANES * u:LANES * (u + 1)]
            xq = _swa_stack([qp, qc, qn], heads)
            xdo = _swa_stack([dop, doc, don], heads)
            lse_r = jnp.concatenate([r[h] for h in heads for r in (lp, lc, ln)], axis=1)
            dl_r = jnp.concatenate([r[h] for h in heads for r in (dp_, dc_, dn_)], axis=1)
            st = lax.dot_general(kc, xq, _NT, preferred_element_type=F32)
            pt = jnp.where(valid, jnp.exp(st - lse_r), 0.0)
            dpt = lax.dot_general(vc, xdo, _NT, preferred_element_type=F32)
            dst = (pt * (dpt - dl_r)).astype(BF16)
            dv_ref[:, LANES * u:LANES * (u + 1)] = lax.dot_general(pt.astype(BF16), xdo, _NN, preferred_element_type=F32)
            dk_ref[:, LANES * u:LANES * (u + 1)] = lax.dot_general(dst, xq, _NN, preferred_element_type=F32)

    return pl.pallas_call(
        body, name=name, grid=(nb,),
        in_specs=_swa_specs(q.shape[1], nb) + _swa_specs(do.shape[1], nb) + stat3() + stat3() + [krow, krow],
        out_specs=[krow, krow],
        out_shape=[_sds(k.shape, F32), _sds(v.shape, F32)],
        compiler_params=_params(("parallel",)),
    )(q, q, q, do, do, do, lse_t, lse_t, lse_t, delta_t, delta_t, delta_t, k, v)


def _exchange(name, tensors):
    n = len(tensors)
    outs = [_sds(a.shape if sc else (N_DEV,) + a.shape, a.dtype) for a, sc in tensors]

    def body(*refs):
        ins, dst, send, recv, loc = refs[:n], refs[n:2 * n], refs[2 * n], refs[2 * n + 1], refs[2 * n + 2]
        x, y, c = lax.axis_index("x"), lax.axis_index("y"), lax.axis_index("c")
        me = 4 * x + 2 * y + c

        def peer(kk):
            return (x ^ ((kk >> 2) & 1), y ^ ((kk >> 1) & 1), c ^ (kk & 1))

        def copy(t, kk):
            px, py, pc = peer(kk)
            src = ins[t].at[4 * px + 2 * py + pc] if tensors[t][1] else ins[t]
            return pltpu.make_async_remote_copy(src_ref=src, dst_ref=dst[t].at[me], send_sem=send.at[t, kk - 1],
                                                recv_sem=recv.at[t, kk - 1], device_id=(px, py, pc),
                                                device_id_type=pl.DeviceIdType.MESH)

        def arrival(t, kk):
            px, py, pc = peer(kk)
            src = ins[t].at[me] if tensors[t][1] else ins[t]
            return pltpu.make_async_remote_copy(src_ref=src, dst_ref=dst[t].at[4 * px + 2 * py + pc],
                                                send_sem=send.at[t, kk - 1], recv_sem=recv.at[t, kk - 1],
                                                device_id=(px, py, pc), device_id_type=pl.DeviceIdType.MESH)

        own = [pltpu.make_async_copy(ins[t].at[me] if tensors[t][1] else ins[t], dst[t].at[me], loc.at[t])
               for t in range(n)]
        sent = [copy(t, kk) for t in range(n) for kk in range(1, N_DEV)]
        for cp in own + sent:
            cp.start()
        for t in range(n):
            for kk in range(1, N_DEV):
                arrival(t, kk).wait_recv()
        for cp in sent:
            cp.wait_send()
        for cp in own:
            cp.wait()

    any_spec = pl.BlockSpec(memory_space=pl.ANY)
    return pl.pallas_call(
        body, name=name, in_specs=[any_spec] * n, out_specs=[any_spec] * n, out_shape=outs,
        scratch_shapes=[pltpu.SemaphoreType.DMA((n, N_DEV - 1)), pltpu.SemaphoreType.DMA((n, N_DEV - 1)),
                        pltpu.SemaphoreType.DMA((n,))],
        compiler_params=pltpu.CompilerParams(has_side_effects=True),
    )(*[a for a, _ in tensors])


def _peer_of(x, y, c, kk):
    return (x ^ ((kk >> 2) & 1), y ^ ((kk >> 1) & 1), c ^ (kk & 1))


def _own_slot(a, scatter):
    me = 4 * lax.axis_index("x") + 2 * lax.axis_index("y") + lax.axis_index("c")
    shape = a.shape if scatter else (N_DEV,) + a.shape
    own = lax.dynamic_slice_in_dim(a, me, 1, 0) if scatter else a[None]
    return lax.dynamic_update_slice_in_dim(lax.empty(shape, a.dtype), own, me, 0)


def _exchange_start(name, tensors):
    n = len(tensors)
    hbm = pl.BlockSpec(memory_space=pltpu.HBM)
    sem = pl.BlockSpec(memory_space=pltpu.SEMAPHORE)
    srcs = [pltpu.with_memory_space_constraint(a, pltpu.HBM) for a, _ in tensors]
    lands = [pltpu.with_memory_space_constraint(_own_slot(a, sc), pltpu.HBM) for a, sc in tensors]

    def body(*refs):
        ins, dst = refs[:n], refs[n:2 * n]
        send, recv, token = refs[2 * n], refs[2 * n + 1], refs[4 * n + 2]
        x, y, c = lax.axis_index("x"), lax.axis_index("y"), lax.axis_index("c")
        me = 4 * x + 2 * y + c
        for t in range(n):
            for kk in range(1, N_DEV):
                px, py, pc = _peer_of(x, y, c, kk)
                src = ins[t].at[4 * px + 2 * py + pc] if tensors[t][1] else ins[t]
                k1 = t * (N_DEV - 1) + kk - 1
                pltpu.make_async_remote_copy(src_ref=src, dst_ref=dst[t].at[me], send_sem=send.at[k1],
                                             recv_sem=recv.at[k1], device_id=(px, py, pc),
                                             device_id_type=pl.DeviceIdType.MESH).start()
        token[...] = jnp.zeros_like(token)

    out = pl.pallas_call(
        body, name=name,
        in_specs=[hbm] * (2 * n),
        out_specs=[sem, sem] + [hbm] * (2 * n) + [pl.BlockSpec(memory_space=pltpu.VMEM)],
        out_shape=[pltpu.SemaphoreType.DMA((n * (N_DEV - 1),)), pltpu.SemaphoreType.DMA((n * (N_DEV - 1),))]
        + [pltpu.HBM(a.shape, a.dtype) for a in srcs] + [pltpu.HBM(a.shape, a.dtype) for a in lands]
        + [_sds((8, LANES), F32)],
        input_output_aliases={i: i + 2 for i in range(2 * n)},
        compiler_params=pltpu.CompilerParams(has_side_effects=pltpu.SideEffectType.DATAFLOW_SIDE_EFFECTING),
    )(*srcs, *lands)
    return (out[0], out[1], out[2:2 + n], out[2 + n:2 + 2 * n], [sc for _, sc in tensors]), out[2 + 2 * n]


def _exchange_wait(name, started, after):
    send_s, recv_s, srcs, lands, flags = started
    n = len(srcs)
    hbm = pl.BlockSpec(memory_space=pltpu.HBM)
    sem = pl.BlockSpec(memory_space=pltpu.SEMAPHORE)

    def body(*refs):
        ins, dst, send, recv = refs[:n], refs[n:2 * n], refs[2 * n], refs[2 * n + 1]
        x, y, c = lax.axis_index("x"), lax.axis_index("y"), lax.axis_index("c")
        me = 4 * x + 2 * y + c
        for t in range(n):
            for kk in range(1, N_DEV):
                px, py, pc = _peer_of(x, y, c, kk)
                src = ins[t].at[me] if flags[t] else ins[t]
                k1 = t * (N_DEV - 1) + kk - 1
                cp = pltpu.make_async_remote_copy(src_ref=src, dst_ref=dst[t].at[4 * px + 2 * py + pc],
                                                  send_sem=send.at[k1], recv_sem=recv.at[k1],
                                                  device_id=(px, py, pc), device_id_type=pl.DeviceIdType.MESH)
                cp.wait_send()
                cp.wait_recv()

    out = pl.pallas_call(
        body, name=name,
        in_specs=[hbm] * (2 * n) + [sem, sem, pl.BlockSpec(memory_space=pl.ANY)],
        out_specs=[hbm] * (2 * n),
        out_shape=[pltpu.HBM(a.shape, a.dtype) for a in srcs] + [pltpu.HBM(a.shape, a.dtype) for a in lands],
        input_output_aliases={i: i for i in range(2 * n)},
        compiler_params=pltpu.CompilerParams(has_side_effects=pltpu.SideEffectType.DATAFLOW_SIDE_EFFECTING),
    )(*srcs, *lands, send_s, recv_s, after)
    return out[n:]


def _sum8(name, parts):
    _, R, C = parts.shape

    def fn(r, c, o):
        acc = r[0][0].astype(F32)
        for s in range(1, N_DEV):
            acc = acc + r[0][s].astype(F32)
        o[0][...] = acc
        return ()

    tm = R if R % 256 else 256
    return _rowwise(name, fn, [parts], [], [_sds((R, C), F32)], tm=tm)[0]


def _adamw(name, w, g, m, v):
    bc1 = 1.0 - ADAM_B1 ** ADAM_STEP
    bc2 = 1.0 - ADAM_B2 ** ADAM_STEP

    def fn(r, c, o):
        wv, gv, mv, vv = (x[...] for x in r)
        mn = ADAM_B1 * mv + (1.0 - ADAM_B1) * gv
        vn = ADAM_B2 * vv + (1.0 - ADAM_B2) * (gv * gv)
        o[0][...] = -ADAM_LR * ((mn / bc1) / (jnp.sqrt(vn / bc2) + ADAM_EPS) + ADAM_WD * wv)
        o[1][...] = mn
        o[2][...] = vn
        return ()

    tm = max(8, min(512, 1 << ((16 << 20) // (56 * w.shape[1])).bit_length() - 1))
    return _rowwise(name, fn, [w, g, m, v], [], [_sds(w.shape, F32)] * 3, tm=tm)


def _rope_cs(pos, dim, theta):
    inv = jnp.float32(theta) ** (-jnp.arange(0, dim, 2, dtype=jnp.float32) / dim)
    ang = pos.astype(jnp.float32)[:, None] * inv[None, :]
    return jnp.cos(ang), jnp.sin(ang)


def _tables(S):
    pos = jnp.arange(S)
    c, s = _rope_cs(pos, MLA_ROPE, ROPE_THETA)
    mla = (jnp.concatenate([c, c, c, c], 1), jnp.concatenate([-s, s, -s, s], 1))
    rc, rs = _rope_cs(pos // GRID_W, HD // 2, AXIAL_THETA)
    cc, cs = _rope_cs(pos % GRID_W, HD // 2, AXIAL_THETA)
    axial = (jnp.concatenate([rc, rc, cc, cc], 1), jnp.concatenate([-rs, rs, -cs, cs], 1))
    c, s = _rope_cs(pos, SWA_ROT, ROPE_THETA)
    one, zero = jnp.ones((S, SWA_D - SWA_ROT), F32), jnp.zeros((S, SWA_D - SWA_ROT), F32)
    swa = (jnp.concatenate([c, c, one, c, c, one], 1), jnp.concatenate([-s, s, zero, -s, s, zero], 1))
    return mla, axial, swa


_WEIGHTS = ['even_norm', 'even_w_in', 'mla_q_lat_norm', 'mla_kv_lat_norm', 'mla_w_uq', 'mla_w_ukv', 'mla_q_norm',
            'mla_k_nope_norm', 'mla_k_rope_norm', 'gqa_q_norm', 'gqa_k_norm', 'even_w_out', 'odd_norm', 'odd_w_qkv',
            'swa_q_norm', 'swa_k_norm', 'swa_sink', 'odd_w_out', 'mlp_norm', 'mlp_w_up', 'mlp_w_down']
_SMALL = ['even_norm', 'mla_q_lat_norm', 'mla_kv_lat_norm', 'mla_q_norm', 'mla_k_nope_norm', 'mla_k_rope_norm',
          'gqa_q_norm', 'gqa_k_norm', 'swa_q_norm', 'swa_k_norm', 'swa_sink', 'mlp_norm']


def _relu2(acc):
    rl = jnp.maximum(acc, 0.0)
    return rl * rl, rl


def _mul2(acc, rl):
    return (acc * (2.0 * rl.astype(F32)),)


def _add(acc, res):
    return (acc + res,)


def _step(x, tgt, w, m, v):
    S, D = x.shape
    nw = len(_WEIGHTS)
    tab_mla, tab_ax, tab_swa = _tables(S)
    bf = lambda a: a.astype(BF16)

    w_up_s, w_dn_s = w['mlp_w_up'], w['mlp_w_down']
    gathered = _exchange("gather_first", [
        (bf(w['even_w_in'][0].T), False), (bf(w['mla_w_uq'][0].T), False), (bf(w['mla_w_ukv'][0].T), False)])
    zero = jnp.minimum(jnp.abs(gathered[2][0, 0:1, 0:1].astype(F32)), 0.0)
    later, tok = _exchange_start("gather_rest_start", [
        (bf(w['even_w_out'][0]), False), (bf(w['odd_w_qkv'][0].T), False), (bf(w['odd_w_out'][0]), False),
        (bf(w_up_s[0].T), False), (bf(w_up_s[1].T), False), (bf(w_dn_s[0]), False), (bf(w_dn_s[1]), False),
        (w['odd_norm'] + zero, False)])
    flat = lambda a: a.reshape((a.shape[0] * a.shape[1],) + a.shape[2:])
    win_t, wuq_g, wukv_t = [flat(a) for a in gathered]
    win_t = jnp.concatenate([win_t[:KR_END], jnp.zeros((LANES - MLA_ROPE, D), BF16), win_t[KR_END:]], 0)
    wuq_g = gathered[1]
    wuq_t = jnp.concatenate([wuq_g[:, :HD].reshape(HEADS * HD, Q_LORA),
                             wuq_g[:, HD:].reshape(HEADS * MLA_ROPE, Q_LORA)], 0)

    z64 = jnp.zeros((1, 64), F32)
    qn = w['mla_q_norm']
    g_even = [w['mla_q_lat_norm'], w['mla_kv_lat_norm'], jnp.concatenate([w['mla_k_rope_norm'], z64], 1),
              w['gqa_q_norm'], w['gqa_k_norm']]
    g_mla = [qn[:, :HD], jnp.concatenate([qn[:, HD:], qn[:, HD:]], 1), w['mla_k_nope_norm']]
    g_swa = [jnp.concatenate([w['swa_q_norm']] * 2, 1), jnp.concatenate([w['swa_k_norm']] * 2, 1)]

    def mlp_fwd(l, xin):
        hn = _rmsnorm(f"mlp{l}_norm", xin, w['mlp_norm'][l:l + 1])
        a, rl = _mm(f"mlp{l}_up", hn, wup_t[l], "nt", (BF16, BF16), epilogue=_relu2)
        xout = _mm(f"mlp{l}_down", a, wdn[l], "nn", (F32,), epilogue=_add, extras=(xin,))
        return xout, (hn, a, rl)

    h0 = _rmsnorm("even_norm", x, w['even_norm'] + tok[0:1, 0:1])
    proj = _mm("even_in", h0, win_t, "nt", (F32,), tm=512, tn=P_END)
    cqn, ckvn, krd, qg, kg, vg = _even_prep("even_prep", proj, tab_mla + tab_ax, g_even)
    qa = _mm("mla_uq", cqn, wuq_t, "nt", (F32,))
    kv = _mm("mla_ukv", ckvn, wukv_t, "nt", (F32,))
    q_a, k_a, v_a = _mla_prep("mla_prep", qa, kv, krd, tab_mla, g_mla)
    o_a, lse_a = _flash_fwd("mla_attn", q_a, k_a, v_a)
    o_g, lse_g = _flash_fwd("gqa_attn", qg, kg, vg)
    merged = jnp.concatenate([o_a, o_g], 1)
    rest = _exchange_wait("gather_rest_wait", later, merged)
    wout_e, wqkv_t, wout_o, wup0_t, wup1_t, wdn0, wdn1 = [flat(a) for a in rest[:7]]
    odd_norm = rest[7].reshape(1, D)
    wup_t, wdn = (wup0_t, wup1_t), (wdn0, wdn1)
    x1 = _mm("even_out", merged, wout_e, "nn", (F32,), epilogue=_add, extras=(x,))
    x2, mlp0 = mlp_fwd(0, x1)

    h1 = _rmsnorm("odd_norm", x2, odd_norm)
    qkv = _mm("odd_qkv", h1, wqkv_t, "nt", (F32,))
    q_s, k_s, v_s = _swa_prep("swa_prep", qkv, tab_swa, g_swa)
    o_s, lse_s = _swa_fwd("swa_attn", q_s, k_s, v_s, w['swa_sink'])
    x3 = _mm("odd_out", o_s, wout_o, "nn", (F32,), epilogue=_add, extras=(x2,))
    x4, mlp1 = mlp_fwd(1, x3)

    dy, loss_acc = _loss("loss", x4, tgt)
    loss = lax.psum(0.5 / D * loss_acc[0, 0], ("x", "y", "c"))

    gsm = {}

    def mlp_bwd(l, dout, xin, saved):
        hn, a, rl = saved
        du = _mm(f"mlp{l}_dact", dout, wdn[l], "nt", (BF16,), epilogue=_mul2, extras=(rl,))
        g_dn = _mm(f"mlp{l}_gdown", a, dout, "tn", (BF16,))
        g_up = _mm(f"mlp{l}_gup", du, hn, "tn", (BF16,))
        dhn = _mm(f"mlp{l}_dnorm", du, wup_t[l], "nn", (F32,))
        din, g_n = _rmsnorm_bwd(f"mlp{l}_norm_bwd", dout, xin, dhn, w['mlp_norm'][l:l + 1])
        return din, g_dn, g_up, g_n

    dx3, g_dn1, g_up1, g_mn1 = mlp_bwd(1, dy, x3, mlp1)
    split = lambda a: a.reshape((N_DEV, a.shape[0] // N_DEV) + a.shape[1:])
    sc1, tok1 = _exchange_start("scatter_mlp1_start", [(split(g_up1), True), (split(g_dn1), True)])
    sink = w['swa_sink'] + tok1[0:1, 0:SWA_HEADS]

    g_wout_o = _mm("odd_gout", o_s, dx3, "tn", (BF16,))
    do_s = _mm("odd_dattn", dx3, wout_o, "nt", (BF16,))
    dl_s = _delta("swa_delta", do_s, o_s, SWA_D).reshape(SWA_HEADS, 1, S)
    dq_s, dsink = _swa_bwd_q("swa_bwd_q", q_s, k_s, v_s, do_s, lse_s, dl_s, sink)
    dk_s, dv_s = _swa_bwd_kv("swa_bwd_kv", q_s, k_s, v_s, do_s, lse_s, dl_s)
    dqkv, g_sq, g_sk = _swa_prep_bwd("swa_prep_bwd", dq_s, dk_s, dv_s, qkv, tab_swa, g_swa)
    g_wqkv = _mm("odd_gqkv", dqkv, h1, "tn", (BF16,))
    dh1 = _mm("odd_dnorm", dqkv, wqkv_t, "nn", (F32,), tk=dqkv.shape[1])
    dx2, g_on = _rmsnorm_bwd("odd_norm_bwd", dx3, x2, dh1, odd_norm)
    gsm['swa_q_norm'] = g_sq[:, :64] + g_sq[:, 64:]
    gsm['swa_k_norm'] = g_sk[:, :64] + g_sk[:, 64:]
    gsm['swa_sink'] = dsink[:, 0].reshape(1, SWA_HEADS)

    dx1, g_dn0, g_up0, g_mn0 = mlp_bwd(0, dx2, x1, mlp0)
    sc2, tok2 = _exchange_start("scatter_mid_start", [(split(g_wout_o), True), (split(g_wqkv), True),
                                                      (split(g_up0), True), (split(g_dn0), True)])
    gsm['mlp_norm'] = jnp.concatenate([g_mn0, g_mn1], 0)

    g_wout_e = _mm("even_gout", merged, dx1, "tn", (BF16,))
    dmerged = _mm("even_dattn", dx1, wout_e, "nt", (BF16,))
    dl_e = _delta("even_delta", dmerged, merged, HD)
    lse_e = jnp.concatenate([lse_a, lse_g], 0) + tok2[0, 0]
    dq_a, dk_a, dv_a = _flash_bwd("mla_attn_bwd", q_a, k_a, v_a, dmerged, 0, lse_e, dl_e)
    dq_g, dk_g, dv_g = _flash_bwd("gqa_attn_bwd", qg, kg, vg, dmerged, HEADS, lse_e, dl_e)
    dqa, dkv, dkrd, g_qnn, g_qnr, g_kn = _mla_prep_bwd("mla_prep_bwd", dq_a, dk_a, dv_a, qa, kv, tab_mla, g_mla)
    g_wuq = _mm("mla_guq", dqa, cqn, "tn", (BF16,))
    dcqn = _mm("mla_dq_lat", dqa, wuq_t, "nn", (F32,))
    g_wukv = _mm("mla_gukv", dkv, ckvn, "tn", (BF16,))
    dckvn = _mm("mla_dkv_lat", dkv, wukv_t, "nn", (F32,))
    dproj, g_ql, g_kvl, g_kr, g_gq, g_gk = _even_prep_bwd("even_prep_bwd", dcqn, dckvn, dkrd, dq_g, dk_g, dv_g,
                                                          proj, tab_mla + tab_ax, g_even)
    g_win = _mm("even_gin", dproj, h0, "tn", (BF16,), tm=P_END, tk=1024)
    dh0 = _mm("even_dnorm", dproj, win_t, "nn", (F32,), tk=P_END)
    grad_x, g_en = _rmsnorm_bwd("even_norm_bwd", dx1, x, dh0, w['even_norm'])
    gsm.update(even_norm=g_en, mla_q_lat_norm=g_ql, mla_kv_lat_norm=g_kvl,
               mla_q_norm=jnp.concatenate([g_qnn, g_qnr[:, :64] + g_qnr[:, 64:]], 1), mla_k_nope_norm=g_kn,
               mla_k_rope_norm=g_kr[:, :64], gqa_q_norm=g_gq, gqa_k_norm=g_gk)

    g_win = jnp.concatenate([g_win[:KR_END], g_win[P_QG:]], 0)
    g_wuq = jnp.concatenate([g_wuq[:HEADS * HD].reshape(HEADS, HD, Q_LORA),
                             g_wuq[HEADS * HD:].reshape(HEADS, MLA_ROPE, Q_LORA)], 1)
    small_sizes = [w[n].size for n in _SMALL] + [D]
    small_vec = jnp.concatenate([gsm[n].reshape(1, -1) for n in _SMALL] + [g_on], 1)
    pad = (-small_vec.shape[1]) % LANES
    small_vec = jnp.pad(small_vec, ((0, 0), (0, pad)))
    last_st, tok3 = _exchange_start("scatter_last_start", [(split(g_win), True), (g_wuq, True), (split(g_wukv), True),
                                                           (split(g_wout_e), True), (small_vec, False)])
    p_up1, p_dn1 = _exchange_wait("scatter_mlp1_wait", sc1, tok3)
    p_wout_o, p_wqkv, p_up0, p_dn0 = _exchange_wait("scatter_mid_wait", sc2, tok3)
    parts = dict(odd_w_qkv=p_wqkv, odd_w_out=p_wout_o, up0=p_up0, up1=p_up1, dn0=p_dn0, dn1=p_dn1)
    red = {n: _sum8("sum_" + n, p) for n, p in parts.items()}
    grads = {
        'odd_w_qkv': red['odd_w_qkv'].T[None], 'odd_w_out': red['odd_w_out'][None],
        'mlp_w_up': jnp.stack([red['up0'].T, red['up1'].T]), 'mlp_w_down': jnp.stack([red['dn0'], red['dn1']]),
    }
    delta, new_m, new_v = {}, {}, {}

    def apply(names):
        for n in names:
            shp = w[n].shape
            two = lambda a: a.reshape(shp[0] * shp[1], shp[2])
            d_, m_, v_ = _adamw("adamw_" + n, two(w[n]), two(grads[n]), two(m[n]), two(v[n]))
            delta[n], new_m[n], new_v[n] = d_.reshape(shp), m_.reshape(shp), v_.reshape(shp)

    apply(['odd_w_qkv', 'odd_w_out', 'mlp_w_up', 'mlp_w_down'])
    last = _exchange_wait("scatter_last_wait", last_st, new_v['mlp_w_down'])
    for n, p in zip(['even_w_in', 'mla_w_uq', 'mla_w_ukv', 'even_w_out'], last[:4]):
        red[n] = _sum8("sum_" + n, p)
    small_g = _sum8("sum_small", last[4])
    grads.update({'even_w_in': red['even_w_in'].T[None], 'mla_w_uq': red['mla_w_uq'].T[None],
                  'mla_w_ukv': red['mla_w_ukv'].T[None], 'even_w_out': red['even_w_out'][None]})
    apply(['even_w_in', 'mla_w_uq', 'mla_w_ukv', 'even_w_out'])
    off = 0
    for n, sz in zip(_SMALL + ['odd_norm_full'], small_sizes):
        seg = small_g[:, off:off + sz]
        off += sz
        if n == 'odd_norm_full':
            me = 4 * lax.axis_index("x") + 2 * lax.axis_index("y") + lax.axis_index("c")
            grads['odd_norm'] = lax.dynamic_slice(seg, (0, me * (D // N_DEV)), (1, D // N_DEV))
        else:
            grads[n] = seg.reshape(w[n].shape)

    sm_names = _SMALL + ['odd_norm']
    pack = lambda d: jnp.concatenate([d[n].reshape(1, -1) for n in sm_names], 1)
    pw, pg, pm, pv = pack(w), pack(grads), pack(m), pack(v)
    padw = (-pw.shape[1]) % LANES
    padf = lambda a: jnp.pad(a, ((0, 0), (0, padw)))
    d_, m_, v_ = _adamw("adamw_small", padf(pw), padf(pg), padf(pm), jnp.pad(pv, ((0, 0), (0, padw)), constant_values=1.0))
    off = 0
    for n in sm_names:
        sz = w[n].size
        delta[n] = d_[:, off:off + sz].reshape(w[n].shape)
        new_m[n] = m_[:, off:off + sz].reshape(w[n].shape)
        new_v[n] = v_[:, off:off + sz].reshape(w[n].shape)
        off += sz

    return (loss, grad_x[None], *[grads[n] for n in _WEIGHTS], *[delta[n] for n in _WEIGHTS],
            *[new_m[n] for n in _WEIGHTS], *[new_v[n] for n in _WEIGHTS])


def kernel(x, even_norm, even_w_in, mla_q_lat_norm, mla_kv_lat_norm, mla_w_uq, mla_w_ukv, mla_q_norm, mla_k_nope_norm, mla_k_rope_norm, gqa_q_norm, gqa_k_norm, even_w_out, odd_norm, odd_w_qkv, swa_q_norm, swa_k_norm, swa_sink, odd_w_out, mlp_norm, mlp_w_up, mlp_w_down, loss_target, m_even_norm, m_even_w_in, m_mla_q_lat_norm, m_mla_kv_lat_norm, m_mla_w_uq, m_mla_w_ukv, m_mla_q_norm, m_mla_k_nope_norm, m_mla_k_rope_norm, m_gqa_q_norm, m_gqa_k_norm, m_even_w_out, m_odd_norm, m_odd_w_qkv, m_swa_q_norm, m_swa_k_norm, m_swa_sink, m_odd_w_out, m_mlp_norm, m_mlp_w_up, m_mlp_w_down, v_even_norm, v_even_w_in, v_mla_q_lat_norm, v_mla_kv_lat_norm, v_mla_w_uq, v_mla_w_ukv, v_mla_q_norm, v_mla_k_nope_norm, v_mla_k_rope_norm, v_gqa_q_norm, v_gqa_k_norm, v_even_w_out, v_odd_norm, v_odd_w_qkv, v_swa_q_norm, v_swa_k_norm, v_swa_sink, v_odd_w_out, v_mlp_norm, v_mlp_w_up, v_mlp_w_down):
    ws = (even_norm, even_w_in, mla_q_lat_norm, mla_kv_lat_norm, mla_w_uq, mla_w_ukv, mla_q_norm, mla_k_nope_norm, mla_k_rope_norm, gqa_q_norm, gqa_k_norm, even_w_out, odd_norm, odd_w_qkv, swa_q_norm, swa_k_norm, swa_sink, odd_w_out, mlp_norm, mlp_w_up, mlp_w_down)
    ms = (m_even_norm, m_even_w_in, m_mla_q_lat_norm, m_mla_kv_lat_norm, m_mla_w_uq, m_mla_w_ukv, m_mla_q_norm, m_mla_k_nope_norm, m_mla_k_rope_norm, m_gqa_q_norm, m_gqa_k_norm, m_even_w_out, m_odd_norm, m_odd_w_qkv, m_swa_q_norm, m_swa_k_norm, m_swa_sink, m_odd_w_out, m_mlp_norm, m_mlp_w_up, m_mlp_w_down)
    vs = (v_even_norm, v_even_w_in, v_mla_q_lat_norm, v_mla_kv_lat_norm, v_mla_w_uq, v_mla_w_ukv, v_mla_q_norm, v_mla_k_nope_norm, v_mla_k_rope_norm, v_gqa_q_norm, v_gqa_k_norm, v_even_w_out, v_odd_norm, v_odd_w_qkv, v_swa_q_norm, v_swa_k_norm, v_swa_sink, v_odd_w_out, v_mlp_norm, v_mlp_w_up, v_mlp_w_down)
    return _step(x[0], loss_target[0], dict(zip(_WEIGHTS, ws)), dict(zip(_WEIGHTS, ms)), dict(zip(_WEIGHTS, vs)))
```

```python
import functools

import jax
import jax.numpy as jnp
from jax import lax
from jax.experimental import pallas as pl
from jax.experimental.pallas import tpu as pltpu

F32 = jnp.float32
BF16 = jnp.bfloat16

N_DEV = 8
NORM_EPS = 1e-6
ROPE_THETA = 500000.0
AXIAL_THETA = 10000.0
GRID_W = 64
HEADS = 8
GQA_KV = 2
HD = 128
MLA_ROPE = 64
MLA_QK = HD + MLA_ROPE
Q_LORA = 512
KV_LORA = 256
SWA_HEADS = 32
SWA_KV = 4
SWA_D = 64
SWA_ROT = 16
SWA_WINDOW = 128
SWA_BLOCK = 128
MLA_SCALE, GQA_SCALE, SWA_SCALE = MLA_QK ** -0.5, HD ** -0.5, SWA_D ** -0.5
LANES = 128
ADAM_LR, ADAM_B1, ADAM_B2, ADAM_EPS, ADAM_WD, ADAM_STEP = 0.001, 0.9, 0.999, 1e-08, 0.01, 10
VMEM_LIMIT = 56 * 1024 * 1024

P_CQ, P_CKV, P_KR, P_QG = 0, Q_LORA, Q_LORA + KV_LORA, Q_LORA + KV_LORA + LANES
P_KG = P_QG + HEADS * HD
P_VG = P_KG + GQA_KV * HD
P_END = P_VG + GQA_KV * HD
KR_END = Q_LORA + KV_LORA + MLA_ROPE


def _tile(n, prefs):
    for t in prefs:
        if n % t == 0 and t <= n:
            return t
    return n


def _params(sem):
    return pltpu.CompilerParams(dimension_semantics=sem, vmem_limit_bytes=VMEM_LIMIT)


_DIMS = {"nn": ((1,), (0,)), "nt": ((1,), (1,)), "tn": ((0,), (0,))}


def _mm(name, a, b, mode, out_dtypes, epilogue=None, extras=(), tm=1024, tn=1024, tk=2048):
    if mode == "nn":
        (M, K), (_, N) = a.shape, b.shape
    elif mode == "nt":
        (M, K), (N, _) = a.shape, b.shape
    else:
        (K, M), (_, N) = a.shape, b.shape
    tm = _tile(M, (tm, 512, 256, 128))
    tn = _tile(N, (tn, 512, 256, 128))
    tk = _tile(K, (tk, 1024, 512, 256, 128))
    nk = K // tk
    ne, no = len(extras), len(out_dtypes)
    if mode == "tn":
        a_spec = pl.BlockSpec((tk, tm), lambda i, j, k: (k, i))
    else:
        a_spec = pl.BlockSpec((tm, tk), lambda i, j, k: (i, k))
    if mode == "nt":
        b_spec = pl.BlockSpec((tn, tk), lambda i, j, k: (j, k))
    else:
        b_spec = pl.BlockSpec((tk, tn), lambda i, j, k: (k, j))
    o_spec = pl.BlockSpec((tm, tn), lambda i, j, k: (i, j))
    dims = (_DIMS[mode], ((), ()))

    def body(a_ref, b_ref, *rest):
        ex, outs = rest[:ne], rest[ne:ne + no]
        k = pl.program_id(2)
        part = lax.dot_general(a_ref[...].astype(BF16), b_ref[...].astype(BF16), dims, preferred_element_type=F32)

        def finish(total):
            res = epilogue(total, *[e[...] for e in ex]) if epilogue else (total,)
            for o, r in zip(outs, res):
                o[...] = r.astype(o.dtype)

        if nk == 1:
            finish(part)
            return
        acc = rest[ne + no]

        @pl.when(k == 0)
        def _():
            acc[...] = part

        @pl.when((k > 0) & (k < nk - 1))
        def _():
            acc[...] += part

        @pl.when(k == nk - 1)
        def _():
            finish(acc[...] + part)

    out = pl.pallas_call(
        body, name=name, grid=(M // tm, N // tn, nk),
        in_specs=[a_spec, b_spec] + [o_spec] * ne,
        out_specs=[o_spec] * no,
        out_shape=[jax.ShapeDtypeStruct((M, N), d) for d in out_dtypes],
        scratch_shapes=[pltpu.VMEM((tm, tn), F32)] if nk > 1 else [],
        compiler_params=_params(("parallel", "parallel", "arbitrary")),
    )(a, b, *extras)
    return out[0] if no == 1 else out


def _rowwise(name, fn, rows, consts, outs, accs=(), tm=256):
    S = rows[0].shape[-2]
    tm = _tile(S, (tm, 128, 64, 32, 16, 8))
    nr, nc, no, na = len(rows), len(consts), len(outs), len(accs)

    def rspec(shape):
        if len(shape) == 2:
            return pl.BlockSpec((tm, shape[1]), lambda i: (i, 0))
        return pl.BlockSpec((shape[0], tm, shape[2]), lambda i: (0, i, 0))

    def cspec(shape):
        return pl.BlockSpec(tuple(shape), lambda i: (0,) * len(shape))

    def ospec(o):
        if isinstance(o, tuple):
            return pl.BlockSpec((o[1].shape[0], o[1].shape[1], tm), lambda i: (0, 0, i))
        return rspec(o.shape)

    out_sds = [o[1] if isinstance(o, tuple) else o for o in outs]

    def body(*refs):
        r, c = refs[:nr], refs[nr:nr + nc]
        o, a = refs[nr + nc:nr + nc + no], refs[nr + nc + no:]
        vals = fn(r, c, o)
        if na:
            @pl.when(pl.program_id(0) == 0)
            def _():
                for ar in a:
                    ar[...] = jnp.zeros_like(ar)

            for ar, v in zip(a, vals):
                ar[...] += v

    res = pl.pallas_call(
        body, name=name, grid=(S // tm,),
        in_specs=[rspec(x.shape) for x in rows] + [cspec(x.shape) for x in consts],
        out_specs=[ospec(x) for x in outs] + [cspec(x.shape) for x in accs],
        out_shape=out_sds + list(accs),
        compiler_params=_params(("arbitrary",) if na else ("parallel",)),
    )(*rows, *consts)
    return res


def _sds(shape, dtype):
    return jax.ShapeDtypeStruct(tuple(shape), dtype)


def _rsum(x):
    return jnp.sum(x, axis=-1, keepdims=True)


def _csum(x):
    return jnp.sum(x, axis=0, keepdims=True)


def _lane(shape):
    return lax.broadcasted_iota(jnp.int32, shape, 1)


def _partner(x, seg, half):
    lane = _lane(x.shape) % seg
    return jnp.where(lane < half, pltpu.roll(x, LANES - half, 1), pltpu.roll(x, half, 1))


def _rope(x, c, s, seg, half):
    return x * c + _partner(x, seg, half) * s


def _rope_bwd(dy, c, s, seg, half):
    t = _partner(dy * s, seg, half)
    if seg != 2 * half:
        t = jnp.where(_lane(dy.shape) % seg < 2 * half, t, 0.0)
    return dy * c + t


def _rms_r(ss, n):
    return lax.rsqrt(ss * (1.0 / n) + NORM_EPS)


def _rms_bwd(x, r, g, dy, dot_scale):
    dyg = dy * g
    return r * dyg - x * (r * r * r) * (_rsum(dyg * x) * dot_scale)


def _rmsnorm(name, x, g):
    D = x.shape[1]

    def fn(r, c, o):
        xv = r[0][...]
        o[0][...] = (xv * _rms_r(_rsum(xv * xv), D) * c[0][...]).astype(BF16)
        return ()

    return _rowwise(name, fn, [x], [g], [_sds(x.shape, BF16)])[0]


def _rmsnorm_bwd(name, dres, x, dh, g):
    D = x.shape[1]

    def fn(r, c, o):
        xv, dhv, gv = r[1][...], r[2][...], c[0][...]
        rr = _rms_r(_rsum(xv * xv), D)
        o[0][...] = r[0][...] + _rms_bwd(xv, rr, gv, dhv, 1.0 / D)
        return (_csum(dhv * xv * rr),)

    return _rowwise(name, fn, [dres, x, dh], [g], [_sds(x.shape, F32)], [_sds((1, D), F32)])


def _loss(name, y, t):
    D = y.shape[1]

    def fn(r, c, o):
        e = r[0][...] - r[1][...]
        o[0][...] = e * (1.0 / D)
        return (_csum(_rsum(e * e)),)

    return _rowwise(name, fn, [y, t], [], [_sds(y.shape, F32)], [_sds((1, 1), F32)])


def _even_prep(name, proj, tabs, gains):
    S = proj.shape[0]
    cm, sm, ca, sa = tabs

    def fn(r, c, o):
        p, cmv, smv, cav, sav = r[0], r[1][...], r[2][...], r[3][...], r[4][...]
        g_ql, g_kvl, g_kr, g_q, g_k = (x[...] for x in c)
        cq = p[:, P_CQ:P_CKV]
        o[0][...] = (cq * _rms_r(_rsum(cq * cq), Q_LORA) * g_ql).astype(BF16)
        ckv = p[:, P_CKV:P_KR]
        o[1][...] = (ckv * _rms_r(_rsum(ckv * ckv), KV_LORA) * g_kvl).astype(BF16)
        kr = p[:, P_KR:P_QG]
        y = _rope(kr * _rms_r(_rsum(kr * kr), MLA_ROPE) * g_kr, cmv, smv, 64, 32)
        o[2][...] = (y + pltpu.roll(y, 64, 1)).astype(BF16)
        for h in range(HEADS):
            xh = p[:, P_QG + HD * h:P_QG + HD * (h + 1)]
            o[3][h] = (_rope(xh * _rms_r(_rsum(xh * xh), HD) * g_q, cav, sav, 64, 32) * GQA_SCALE).astype(BF16)
        for h in range(GQA_KV):
            xh = p[:, P_KG + HD * h:P_KG + HD * (h + 1)]
            o[4][h] = _rope(xh * _rms_r(_rsum(xh * xh), HD) * g_k, cav, sav, 64, 32).astype(BF16)
            o[5][h] = p[:, P_VG + HD * h:P_VG + HD * (h + 1)].astype(BF16)
        return ()

    outs = [_sds((S, Q_LORA), BF16), _sds((S, KV_LORA), BF16), _sds((S, LANES), BF16),
            _sds((HEADS, S, HD), BF16), _sds((GQA_KV, S, HD), BF16), _sds((GQA_KV, S, HD), BF16)]
    return _rowwise(name, fn, [proj, cm, sm, ca, sa], gains, outs)


def _even_prep_bwd(name, dcqn, dckvn, dkrd, dqg, dkg, dvg, proj, tabs, gains):
    S = proj.shape[0]
    cm, sm, ca, sa = tabs

    def fn(r, c, o):
        p, cmv, smv, cav, sav = r[6], r[7][...], r[8][...], r[9][...], r[10][...]
        g_ql, g_kvl, g_kr, g_q, g_k = (x[...] for x in c)
        out = o[0]
        cq = p[:, P_CQ:P_CKV]
        rr = _rms_r(_rsum(cq * cq), Q_LORA)
        d = r[0][...]
        out[:, P_CQ:P_CKV] = _rms_bwd(cq, rr, g_ql, d, 1.0 / Q_LORA).astype(BF16)
        a_ql = _csum(d * cq * rr)
        ckv = p[:, P_CKV:P_KR]
        rr = _rms_r(_rsum(ckv * ckv), KV_LORA)
        d = r[1][...]
        out[:, P_CKV:P_KR] = _rms_bwd(ckv, rr, g_kvl, d, 1.0 / KV_LORA).astype(BF16)
        a_kvl = _csum(d * ckv * rr)
        kr = p[:, P_KR:P_QG]
        rr = _rms_r(_rsum(kr * kr), MLA_ROPE)
        d = r[2][...]
        d = d + pltpu.roll(d, 64, 1)
        d = _rope_bwd(d, cmv, smv, 64, 32)
        low = _lane(d.shape) < 64
        out[:, P_KR:P_QG] = jnp.where(low, _rms_bwd(kr, rr, g_kr, d, 1.0 / MLA_ROPE), 0.0).astype(BF16)
        a_kr = _csum(d * kr * rr)
        a_q = jnp.zeros((1, HD), F32)
        for h in range(HEADS):
            xh = p[:, P_QG + HD * h:P_QG + HD * (h + 1)]
            rr = _rms_r(_rsum(xh * xh), HD)
            d = _rope_bwd(r[3][h] * GQA_SCALE, cav, sav, 64, 32)
            out[:, P_QG + HD * h:P_QG + HD * (h + 1)] = _rms_bwd(xh, rr, g_q, d, 1.0 / HD).astype(BF16)
            a_q = a_q + _csum(d * xh * rr)
        a_k = jnp.zeros((1, HD), F32)
        grp = HEADS // GQA_KV
        for h in range(GQA_KV):
            xh = p[:, P_KG + HD * h:P_KG + HD * (h + 1)]
            rr = _rms_r(_rsum(xh * xh), HD)
            d = r[4][grp * h]
            dv = r[5][grp * h]
            for e in range(1, grp):
                d = d + r[4][grp * h + e]
                dv = dv + r[5][grp * h + e]
            d = _rope_bwd(d, cav, sav, 64, 32)
            out[:, P_KG + HD * h:P_KG + HD * (h + 1)] = _rms_bwd(xh, rr, g_k, d, 1.0 / HD).astype(BF16)
            out[:, P_VG + HD * h:P_VG + HD * (h + 1)] = dv.astype(BF16)
            a_k = a_k + _csum(d * xh * rr)
        return (a_ql, a_kvl, a_kr, a_q, a_k)

    accs = [_sds((1, Q_LORA), F32), _sds((1, KV_LORA), F32), _sds((1, LANES), F32),
            _sds((1, HD), F32), _sds((1, HD), F32)]
    return _rowwise(name, fn, [dcqn, dckvn, dkrd, dqg, dkg, dvg, proj, cm, sm, ca, sa], gains,
                    [_sds((S, P_END), BF16)], accs, tm=128)


def _mla_q_parts(qa, h, rp, g_n, g_r):
    nope = qa[:, HD * h:HD * (h + 1)]
    mine = (_lane(rp.shape) >= 64) == bool(h % 2)
    rpm = jnp.where(mine, rp, 0.0)
    rr = _rms_r(_rsum(nope * nope) + _rsum(rpm * rpm), MLA_QK)
    return nope, rpm, mine, rr


def _mla_prep(name, qa, kv, krd, tabs, gains):
    S = qa.shape[0]
    cm, sm = tabs

    def fn(r, c, o):
        qa_r, kv_r, krd_v, cmv, smv = r[0], r[1], r[2][...], r[3][...], r[4][...]
        g_n, g_r, g_kn = (x[...] for x in c)
        for h in range(HEADS):
            rp = qa_r[:, HEADS * HD + LANES * (h // 2):HEADS * HD + LANES * (h // 2 + 1)]
            nope, rpm, mine, rr = _mla_q_parts(qa_r, h, rp, g_n, g_r)
            o[0][h, :, 0:HD] = (nope * (rr * MLA_SCALE) * g_n).astype(BF16)
            o[0][h, :, HD:2 * HD] = _rope(rpm * (rr * MLA_SCALE) * g_r, cmv, smv, 64, 32).astype(BF16)
            kn = kv_r[:, 2 * HD * h:2 * HD * h + HD]
            o[1][h, :, 0:HD] = (kn * _rms_r(_rsum(kn * kn), HD) * g_kn).astype(BF16)
            o[1][h, :, HD:2 * HD] = krd_v
            o[2][h] = kv_r[:, 2 * HD * h + HD:2 * HD * (h + 1)].astype(BF16)
        return ()

    outs = [_sds((HEADS, S, 2 * HD), BF16), _sds((HEADS, S, 2 * HD), BF16), _sds((HEADS, S, HD), BF16)]
    return _rowwise(name, fn, [qa, kv, krd, cm, sm], gains, outs)


def _mla_prep_bwd(name, dq, dk, dv, qa, kv, tabs, gains):
    S = qa.shape[0]
    cm, sm = tabs

    def fn(r, c, o):
        dq_r, dk_r, dv_r, qa_r, kv_r, cmv, smv = r[0], r[1], r[2], r[3], r[4], r[5][...], r[6][...]
        g_n, g_r, g_kn = (x[...] for x in c)
        a_n = jnp.zeros((1, HD), F32)
        a_r = jnp.zeros((1, LANES), F32)
        a_kn = jnp.zeros((1, HD), F32)
        dkrd = jnp.zeros(cmv.shape, F32)
        drp = None
        for h in range(HEADS):
            rp = qa_r[:, HEADS * HD + LANES * (h // 2):HEADS * HD + LANES * (h // 2 + 1)]
            nope, rpm, mine, rr = _mla_q_parts(qa_r, h, rp, g_n, g_r)
            dn = dq_r[h, :, 0:HD] * MLA_SCALE
            dr = _rope_bwd(jnp.where(mine, dq_r[h, :, HD:2 * HD] * MLA_SCALE, 0.0), cmv, smv, 64, 32)
            dot = (_rsum(dn * g_n * nope) + _rsum(dr * g_r * rpm)) * (1.0 / MLA_QK)
            r3 = rr * rr * rr
            o[0][:, HD * h:HD * (h + 1)] = (rr * dn * g_n - nope * r3 * dot).astype(BF16)
            part = jnp.where(mine, rr * dr * g_r - rpm * r3 * dot, 0.0)
            drp = part if h % 2 == 0 else drp + part
            if h % 2 == 1:
                o[0][:, HEADS * HD + LANES * (h // 2):HEADS * HD + LANES * (h // 2 + 1)] = drp.astype(BF16)
            a_n = a_n + _csum(dn * nope * rr)
            a_r = a_r + _csum(dr * rpm * rr)
            kn = kv_r[:, 2 * HD * h:2 * HD * h + HD]
            rk = _rms_r(_rsum(kn * kn), HD)
            dkn = dk_r[h, :, 0:HD]
            o[1][:, 2 * HD * h:2 * HD * h + HD] = _rms_bwd(kn, rk, g_kn, dkn, 1.0 / HD).astype(BF16)
            o[1][:, 2 * HD * h + HD:2 * HD * (h + 1)] = dv_r[h].astype(BF16)
            a_kn = a_kn + _csum(dkn * kn * rk)
            dkrd = dkrd + dk_r[h, :, HD:2 * HD]
        o[2][...] = dkrd
        return (a_n, a_r, a_kn)

    outs = [_sds(qa.shape, BF16), _sds(kv.shape, BF16), _sds((S, LANES), F32)]
    accs = [_sds((1, HD), F32), _sds((1, LANES), F32), _sds((1, HD), F32)]
    return _rowwise(name, fn, [dq, dk, dv, qa, kv, cm, sm], gains, outs, accs, tm=128)


def _seg_ones(seg):
    r = lax.broadcasted_iota(jnp.int32, (LANES, LANES), 0) // seg
    c = lax.broadcasted_iota(jnp.int32, (LANES, LANES), 1) // seg
    return (r == c).astype(F32)


def _seg_sum(x, ones):
    return lax.dot_general(x, ones, _NN, preferred_element_type=F32, precision=lax.Precision.HIGHEST)


def _seg64_r(x, ones):
    return _rms_r(_seg_sum(x * x, ones), SWA_D)


def _swa_prep(name, qkv, tabs, gains):
    S = qkv.shape[0]
    nq, nk = SWA_HEADS * SWA_D, SWA_KV * SWA_D
    cs, ss = tabs

    def fn(r, c, o):
        x_r, csv, ssv = r[0], r[1][...], r[2][...]
        g_q, g_k = c[0][...], c[1][...]
        ones = _seg_ones(SWA_D)
        for g in range((nq + nk) // LANES):
            x = x_r[:, LANES * g:LANES * (g + 1)]
            rr = _seg64_r(x, ones)
            y = _rope(x * rr * (g_q if g < nq // LANES else g_k), csv, ssv, SWA_D, SWA_ROT // 2)
            if g < nq // LANES:
                o[0][:, LANES * g:LANES * (g + 1)] = (y * SWA_SCALE).astype(BF16)
            else:
                o[1][:, LANES * g - nq:LANES * (g + 1) - nq] = y.astype(BF16)
        o[2][...] = x_r[:, nq + nk:nq + 2 * nk].astype(BF16)
        return ()

    outs = [_sds((S, nq), BF16), _sds((S, nk), BF16), _sds((S, nk), BF16)]
    return _rowwise(name, fn, [qkv, cs, ss], gains, outs)


def _swa_prep_bwd(name, dq, dk, dv, qkv, tabs, gains):
    nq, nk = SWA_HEADS * SWA_D, SWA_KV * SWA_D
    cs, ss = tabs

    def fn(r, c, o):
        dq_r, dk_r, x_r, csv, ssv = r[0], r[1], r[3], r[4][...], r[5][...]
        g_q, g_k = c[0][...], c[1][...]
        acc = [jnp.zeros((1, LANES), F32), jnp.zeros((1, LANES), F32)]
        ones = _seg_ones(SWA_D)
        for g in range((nq + nk) // LANES):
            isq = g < nq // LANES
            x = x_r[:, LANES * g:LANES * (g + 1)]
            rr = _seg64_r(x, ones)
            d = dq_r[:, LANES * g:LANES * (g + 1)] * SWA_SCALE if isq else dk_r[:, LANES * g - nq:LANES * (g + 1) - nq]
            d = _rope_bwd(d, csv, ssv, SWA_D, SWA_ROT // 2)
            dyg = d * (g_q if isq else g_k)
            dot = _seg_sum(dyg * x, ones) * (1.0 / SWA_D)
            o[0][:, LANES * g:LANES * (g + 1)] = (rr * dyg - x * (rr * rr * rr) * dot).astype(BF16)
            acc[0 if isq else 1] = acc[0 if isq else 1] + _csum(d * x * rr)
        o[0][:, nq + nk:nq + 2 * nk] = r[2][...].astype(BF16)
        return tuple(acc)

    return _rowwise(name, fn, [dq, dk, dv, qkv, cs, ss], gains, [_sds(qkv.shape, BF16)],
                    [_sds((1, LANES), F32), _sds((1, LANES), F32)], tm=128)


def _delta(name, do, o_):
    S, C = do.shape

    def fn(r, c, o):
        for g in range(C // LANES):
            t = r[0][:, LANES * g:LANES * (g + 1)].astype(F32) * r[1][:, LANES * g:LANES * (g + 1)].astype(F32)
            o[0][g] = _rsum(t)
        return ()

    return _rowwise(name, fn, [do, o_], [], [_sds((C // LANES, S, 1), F32)])[0]


def _delta_t(name, do, o_):
    S, C = do.shape

    def fn(r, c, o):
        row = lax.broadcasted_iota(jnp.int32, (8, LANES), 0)
        sel = (row == _lane((8, LANES)) // SWA_D).astype(F32)
        for g in range(C // LANES):
            t = r[0][:, LANES * g:LANES * (g + 1)].astype(F32) * r[1][:, LANES * g:LANES * (g + 1)].astype(F32)
            both = lax.dot_general(sel, t, _NT, preferred_element_type=F32, precision=lax.Precision.HIGHEST)
            o[0][2 * g] = both[0:1, :]
            o[0][2 * g + 1] = both[1:2, :]
        return ()

    return _rowwise(name, fn, [do, o_], [], [("T", _sds((2 * C // LANES, 1, S), F32))])[0]


_NT = (((1,), (1,)), ((), ()))
_NN = (((1,), (0,)), ((), ()))
_TN = (((0,), (0,)), ((), ()))


def _flash_fwd(name, q, k, v, tq=1024, tk=4096):
    H, S, dk = q.shape
    G = H // k.shape[0]
    dv = v.shape[2]
    tq, tk = _tile(S, (tq, 256, 128)), _tile(S, (tk, 256, 128))
    nk = S // tk

    def body(q_ref, k_ref, v_ref, o_ref, lse_ref, m_s, l_s, acc_s):
        j = pl.program_id(2)

        @pl.when(j == 0)
        def _():
            m_s[...] = jnp.full_like(m_s, -jnp.inf)
            l_s[...] = jnp.zeros_like(l_s)
            acc_s[...] = jnp.zeros_like(acc_s)

        s = lax.dot_general(q_ref[...], k_ref[...], _NT, preferred_element_type=F32)
        m_new = jnp.maximum(m_s[...], jnp.max(s, axis=-1, keepdims=True))
        alpha = jnp.exp(m_s[...] - m_new)
        p = jnp.exp(s - m_new)
        l_s[...] = alpha * l_s[...] + _rsum(p)
        acc_s[...] = alpha * acc_s[...] + lax.dot_general(p.astype(BF16), v_ref[...], _NN,
                                                          preferred_element_type=F32)
        m_s[...] = m_new

        @pl.when(j == nk - 1)
        def _():
            o_ref[...] = (acc_s[...] / l_s[...]).astype(o_ref.dtype)
            lse_ref[...] = m_s[...] + jnp.log(l_s[...])

    return pl.pallas_call(
        body, name=name, grid=(H, S // tq, nk),
        in_specs=[pl.BlockSpec((None, tq, dk), lambda h, i, j: (h, i, 0)),
                  pl.BlockSpec((None, tk, dk), lambda h, i, j: (h // G, j, 0)),
                  pl.BlockSpec((None, tk, dv), lambda h, i, j: (h // G, j, 0))],
        out_specs=[pl.BlockSpec((tq, dv), lambda h, i, j: (i, h)),
                   pl.BlockSpec((None, tq, 1), lambda h, i, j: (h, i, 0))],
        out_shape=[_sds((S, H * dv), BF16), _sds((H, S, 1), F32)],
        scratch_shapes=[pltpu.VMEM((tq, 1), F32), pltpu.VMEM((tq, 1), F32), pltpu.VMEM((tq, dv), F32)],
        compiler_params=_params(("parallel", "parallel", "arbitrary")),
    )(q, k, v)


def _flash_bwd(name, q, k, v, do, head0, lse, delta, tq=1024, tk=1024):
    H, S, dk = q.shape
    G = H // k.shape[0]
    dv = v.shape[2]
    tq, tk = _tile(S, (tq, 256, 128)), _tile(S, (tk, 256, 128))
    nq = S // tq

    def body(q_ref, k_ref, v_ref, do_ref, lse_ref, dl_ref, dq_ref, dk_ref, dv_ref, dk_s, dv_s):
        j, i = pl.program_id(1), pl.program_id(2)

        @pl.when(i == 0)
        def _():
            dk_s[...] = jnp.zeros_like(dk_s)
            dv_s[...] = jnp.zeros_like(dv_s)

        qv, kv_, dov = q_ref[...], k_ref[...], do_ref[...]
        s = lax.dot_general(qv, kv_, _NT, preferred_element_type=F32)
        p = jnp.exp(s - lse_ref[...])
        dp = lax.dot_general(dov, v_ref[...], _NT, preferred_element_type=F32)
        ds = (p * (dp - dl_ref[...])).astype(BF16)
        dv_s[...] += lax.dot_general(p.astype(BF16), dov, _TN, preferred_element_type=F32)
        dk_s[...] += lax.dot_general(ds, qv, _TN, preferred_element_type=F32)
        dqi = lax.dot_general(ds, kv_, _NN, preferred_element_type=F32)
        rows = pl.ds(pl.multiple_of(i * tq, tq), tq)

        @pl.when(j == 0)
        def _():
            dq_ref[rows, :] = dqi

        @pl.when(j > 0)
        def _():
            dq_ref[rows, :] += dqi

        @pl.when(i == nq - 1)
        def _():
            dk_ref[...] = dk_s[...]
            dv_ref[...] = dv_s[...]

    return pl.pallas_call(
        body, name=name, grid=(H, S // tk, nq),
        in_specs=[pl.BlockSpec((None, tq, dk), lambda h, j, i: (h, i, 0)),
                  pl.BlockSpec((None, tk, dk), lambda h, j, i: (h // G, j, 0)),
                  pl.BlockSpec((None, tk, dv), lambda h, j, i: (h // G, j, 0)),
                  pl.BlockSpec((tq, dv), lambda h, j, i: (i, head0 + h)),
                  pl.BlockSpec((None, tq, 1), lambda h, j, i: (head0 + h, i, 0)),
                  pl.BlockSpec((None, tq, 1), lambda h, j, i: (head0 + h, i, 0))],
        out_specs=[pl.BlockSpec((None, S, dk), lambda h, j, i: (h, 0, 0)),
                   pl.BlockSpec((None, tk, dk), lambda h, j, i: (h, j, 0)),
                   pl.BlockSpec((None, tk, dv), lambda h, j, i: (h, j, 0))],
        out_shape=[_sds((H, S, dk), F32), _sds((H, S, dk), F32), _sds((H, S, dv), F32)],
        scratch_shapes=[pltpu.VMEM((tk, dk), F32), pltpu.VMEM((tk, dv), F32)],
        compiler_params=_params(("parallel", "arbitrary", "arbitrary")),
    )(q, k, v, do, lse, delta)


def _swa_place(ref128, h):
    e, t = h % 2, (h // (SWA_HEADS // SWA_KV)) % 2
    x = ref128.astype(F32)
    if e != t:
        x = pltpu.roll(x, 64, 1)
    return jnp.where((_lane(x.shape) >= 64) == bool(t), x, 0.0).astype(BF16)


def _swa_unplace(y, h):
    e, t = h % 2, (h // (SWA_HEADS // SWA_KV)) % 2
    return pltpu.roll(y, 64, 1) if e != t else y


def _swa_specs(width, nb):
    prev = pl.BlockSpec((SWA_BLOCK, width), lambda i: (jnp.maximum(i - 1, 0), 0))
    cur = pl.BlockSpec((SWA_BLOCK, width), lambda i: (i, 0))
    nxt = pl.BlockSpec((SWA_BLOCK, width), lambda i: (jnp.minimum(i + 1, nb - 1), 0))
    return [prev, cur, nxt]


def _swa_valid_t(i, S, reps):
    shape = (3 * SWA_BLOCK, reps * SWA_BLOCK)
    kpos = (i - 1) * SWA_BLOCK + lax.broadcasted_iota(jnp.int32, shape, 0)
    qpos = i * SWA_BLOCK + lax.broadcasted_iota(jnp.int32, shape, 1) % SWA_BLOCK
    return (jnp.abs(qpos - kpos) <= SWA_WINDOW) & (kpos >= 0) & (kpos < S)


def _swa_stack(refs, heads):
    return jnp.concatenate([_swa_place(r[:, LANES * (h // 2):LANES * (h // 2 + 1)], h) for h in heads for r in refs],
                           axis=0)


def _swa_unstack_t(ot, j, out_ref):
    grp = SWA_HEADS // SWA_KV
    for pair in range(grp // 2):
        h = grp * j + 2 * pair
        a = _swa_unplace(ot[:, SWA_BLOCK * 2 * pair:SWA_BLOCK * (2 * pair + 1)].T, h)
        b = _swa_unplace(ot[:, SWA_BLOCK * (2 * pair + 1):SWA_BLOCK * (2 * pair + 2)].T, h + 1)
        out_ref[:, LANES * (h // 2):LANES * (h // 2 + 1)] = jnp.where(_lane(a.shape) < 64, a, b).astype(out_ref.dtype)


def _swa_fwd(name, q, k, v, sink):
    S = q.shape[0]
    nb = S // SWA_BLOCK
    grp = SWA_HEADS // SWA_KV
    smem = pl.BlockSpec(memory_space=pltpu.SMEM)

    def body(sink_ref, q_ref, kp, kc, kn, vp, vc, vn, o_ref, lse_ref):
        i = pl.program_id(0)
        valid = _swa_valid_t(i, S, grp)
        kcat = [jnp.concatenate([r[:, LANES * u:LANES * (u + 1)] for r in (kp, kc, kn)], axis=0) for u in range(2)]
        vcat = [jnp.concatenate([r[:, LANES * u:LANES * (u + 1)] for r in (vp, vc, vn)], axis=0) for u in range(2)]
        for j in range(SWA_KV):
            heads = range(grp * j, grp * (j + 1))
            xq = _swa_stack([q_ref], heads)
            sk = jnp.concatenate([jnp.full((1, SWA_BLOCK), sink_ref[0, h], F32) for h in heads], axis=1)
            st = lax.dot_general(kcat[j // 2], xq, _NT, preferred_element_type=F32)
            st = jnp.where(valid, st, -jnp.inf)
            m = jnp.maximum(jnp.max(st, axis=0, keepdims=True), sk)
            pt = jnp.exp(st - m)
            den = jnp.sum(pt, axis=0, keepdims=True) + jnp.exp(sk - m)
            ot = lax.dot_general(vcat[j // 2], pt.astype(BF16), _TN, preferred_element_type=F32) * (1.0 / den)
            lse = m + jnp.log(den)
            for e, h in enumerate(heads):
                lse_ref[h] = lse[:, SWA_BLOCK * e:SWA_BLOCK * (e + 1)]
            _swa_unstack_t(ot, j, o_ref)

    return pl.pallas_call(
        body, name=name, grid=(nb,),
        in_specs=[smem, pl.BlockSpec((SWA_BLOCK, q.shape[1]), lambda i: (i, 0))]
        + _swa_specs(k.shape[1], nb) + _swa_specs(v.shape[1], nb),
        out_specs=[pl.BlockSpec((SWA_BLOCK, q.shape[1]), lambda i: (i, 0)),
                   pl.BlockSpec((SWA_HEADS, 1, SWA_BLOCK), lambda i: (0, 0, i))],
        out_shape=[_sds(q.shape, BF16), _sds((SWA_HEADS, 1, S), F32)],
        compiler_params=_params(("parallel",)),
    )(sink, q, k, k, k, v, v, v)


def _swa_bwd_q(name, q, k, v, do, lse_t, delta_t, sink):
    S = q.shape[0]
    nb = S // SWA_BLOCK
    grp = SWA_HEADS // SWA_KV
    smem = pl.BlockSpec(memory_space=pltpu.SMEM)
    row = pl.BlockSpec((SWA_BLOCK, q.shape[1]), lambda i: (i, 0))
    hrow = pl.BlockSpec((SWA_HEADS, 1, SWA_BLOCK), lambda i: (0, 0, i))

    def body(sink_ref, q_ref, do_ref, lse_ref, dl_ref, kp, kc, kn, vp, vc, vn, dq_ref, ds_ref):
        i = pl.program_id(0)

        @pl.when(i == 0)
        def _():
            ds_ref[...] = jnp.zeros_like(ds_ref)

        valid = _swa_valid_t(i, S, grp)
        kcat = [jnp.concatenate([r[:, LANES * u:LANES * (u + 1)] for r in (kp, kc, kn)], axis=0) for u in range(2)]
        vcat = [jnp.concatenate([r[:, LANES * u:LANES * (u + 1)] for r in (vp, vc, vn)], axis=0) for u in range(2)]
        for j in range(SWA_KV):
            heads = range(grp * j, grp * (j + 1))
            xq = _swa_stack([q_ref], heads)
            xdo = _swa_stack([do_ref], heads)
            lse_r = jnp.concatenate([lse_ref[h] for h in heads], axis=1)
            dl_r = jnp.concatenate([dl_ref[h] for h in heads], axis=1)
            st = lax.dot_general(kcat[j // 2], xq, _NT, preferred_element_type=F32)
            pt = jnp.exp(jnp.where(valid, st, -jnp.inf) - lse_r)
            dpt = lax.dot_general(vcat[j // 2], xdo, _NT, preferred_element_type=F32)
            dst = (pt * (dpt - dl_r)).astype(BF16)
            _swa_unstack_t(lax.dot_general(kcat[j // 2], dst, _TN, preferred_element_type=F32), j, dq_ref)
            for h in heads:
                dsink = -_rsum(jnp.exp(sink_ref[0, h] - lse_ref[h]) * dl_ref[h])
                ds_ref[h:h + 1, :] += jnp.broadcast_to(dsink, (1, LANES))

    return pl.pallas_call(
        body, name=name, grid=(nb,),
        in_specs=[smem, row, row, hrow, hrow] + _swa_specs(k.shape[1], nb) + _swa_specs(v.shape[1], nb),
        out_specs=[row, pl.BlockSpec((SWA_HEADS, LANES), lambda i: (0, 0))],
        out_shape=[_sds(q.shape, F32), _sds((SWA_HEADS, LANES), F32)],
        compiler_params=_params(("arbitrary",)),
    )(sink, q, do, lse_t, delta_t, k, k, k, v, v, v)


def _swa_bwd_kv(name, q, k, v, do, lse_t, delta_t):
    S = q.shape[0]
    nb = S // SWA_BLOCK
    grp = SWA_HEADS // SWA_KV
    krow = pl.BlockSpec((SWA_BLOCK, k.shape[1]), lambda i: (i, 0))

    def stat3():
        return [pl.BlockSpec((SWA_HEADS, 1, SWA_BLOCK), lambda i: (0, 0, jnp.maximum(i - 1, 0))),
                pl.BlockSpec((SWA_HEADS, 1, SWA_BLOCK), lambda i: (0, 0, i)),
                pl.BlockSpec((SWA_HEADS, 1, SWA_BLOCK), lambda i: (0, 0, jnp.minimum(i + 1, nb - 1)))]

    def body(qp, qc, qn, dop, doc, don, lp, lc, ln, dp_, dc_, dn_, k_ref, v_ref, dk_ref, dv_ref):
        j = pl.program_id(0)
        nh = 2 * grp
        shape = (SWA_BLOCK, nh * 3 * SWA_BLOCK)
        qpos = (j - 1) * SWA_BLOCK + lax.broadcasted_iota(jnp.int32, shape, 1) % (3 * SWA_BLOCK)
        kpos = j * SWA_BLOCK + lax.broadcasted_iota(jnp.int32, shape, 0)
        valid = (jnp.abs(qpos - kpos) <= SWA_WINDOW) & (qpos >= 0) & (qpos < S)
        for u in range(SWA_KV // 2):
            heads = range(nh * u, nh * (u + 1))
            kc = k_ref[:, LANES * u:LANES * (u + 1)]
            vc = v_ref[:, LANES * u:LANES * (u + 1)]
            xq = _swa_stack([qp, qc, qn], heads)
            xdo = _swa_stack([dop, doc, don], heads)
            lse_r = jnp.concatenate([r[h] for h in heads for r in (lp, lc, ln)], axis=1)
            dl_r = jnp.concatenate([r[h] for h in heads for r in (dp_, dc_, dn_)], axis=1)
            st = lax.dot_general(kc, xq, _NT, preferred_element_type=F32)
            pt = jnp.where(valid, jnp.exp(st - lse_r), 0.0)
            dpt = lax.dot_general(vc, xdo, _NT, preferred_element_type=F32)
            dst = (pt * (dpt - dl_r)).astype(BF16)
            dv_ref[:, LANES * u:LANES * (u + 1)] = lax.dot_general(pt.astype(BF16), xdo, _NN, preferred_element_type=F32)
            dk_ref[:, LANES * u:LANES * (u + 1)] = lax.dot_general(dst, xq, _NN, preferred_element_type=F32)

    return pl.pallas_call(
        body, name=name, grid=(nb,),
        in_specs=_swa_specs(q.shape[1], nb) + _swa_specs(do.shape[1], nb) + stat3() + stat3() + [krow, krow],
        out_specs=[krow, krow],
        out_shape=[_sds(k.shape, F32), _sds(v.shape, F32)],
        compiler_params=_params(("parallel",)),
    )(q, q, q, do, do, do, lse_t, lse_t, lse_t, delta_t, delta_t, delta_t, k, v)


def _peer_of(x, y, c, kk):
    return (x ^ ((kk >> 2) & 1), y ^ ((kk >> 1) & 1), c ^ (kk & 1))


def _own_slot(a, scatter):
    me = 4 * lax.axis_index("x") + 2 * lax.axis_index("y") + lax.axis_index("c")
    shape = a.shape if scatter else (N_DEV,) + a.shape
    own = lax.dynamic_slice_in_dim(a, me, 1, 0) if scatter else a[None]
    return lax.dynamic_update_slice_in_dim(lax.empty(shape, a.dtype), own, me, 0)


def _exchange_start(name, tensors):
    n = len(tensors)
    hbm = pl.BlockSpec(memory_space=pltpu.HBM)
    sem = pl.BlockSpec(memory_space=pltpu.SEMAPHORE)
    srcs = [pltpu.with_memory_space_constraint(a, pltpu.HBM) for a, _ in tensors]
    lands = [pltpu.with_memory_space_constraint(_own_slot(a, sc), pltpu.HBM) for a, sc in tensors]

    def body(*refs):
        ins, dst = refs[:n], refs[n:2 * n]
        send, recv, token = refs[2 * n], refs[2 * n + 1], refs[4 * n + 2]
        x, y, c = lax.axis_index("x"), lax.axis_index("y"), lax.axis_index("c")
        me = 4 * x + 2 * y + c
        for t in range(n):
            for kk in range(1, N_DEV):
                px, py, pc = _peer_of(x, y, c, kk)
                src = ins[t].at[4 * px + 2 * py + pc] if tensors[t][1] else ins[t]
                k1 = t * (N_DEV - 1) + kk - 1
                pltpu.make_async_remote_copy(src_ref=src, dst_ref=dst[t].at[me], send_sem=send.at[k1],
                                             recv_sem=recv.at[k1], device_id=(px, py, pc),
                                             device_id_type=pl.DeviceIdType.MESH).start()
        token[...] = jnp.zeros_like(token)

    out = pl.pallas_call(
        body, name=name,
        in_specs=[hbm] * (2 * n),
        out_specs=[sem, sem] + [hbm] * (2 * n) + [pl.BlockSpec(memory_space=pltpu.VMEM)],
        out_shape=[pltpu.SemaphoreType.DMA((n * (N_DEV - 1),)), pltpu.SemaphoreType.DMA((n * (N_DEV - 1),))]
        + [pltpu.HBM(a.shape, a.dtype) for a in srcs] + [pltpu.HBM(a.shape, a.dtype) for a in lands]
        + [_sds((8, LANES), F32)],
        input_output_aliases={i: i + 2 for i in range(2 * n)},
        compiler_params=pltpu.CompilerParams(has_side_effects=pltpu.SideEffectType.DATAFLOW_SIDE_EFFECTING),
    )(*srcs, *lands)
    return (out[0], out[1], out[2:2 + n], out[2 + n:2 + 2 * n], [sc for _, sc in tensors]), out[2 + 2 * n]


def _exchange_wait(name, started, after):
    send_s, recv_s, srcs, lands, flags = started
    n = len(srcs)
    hbm = pl.BlockSpec(memory_space=pltpu.HBM)
    sem = pl.BlockSpec(memory_space=pltpu.SEMAPHORE)

    def body(*refs):
        ins, dst, send, recv = refs[:n], refs[n:2 * n], refs[2 * n], refs[2 * n + 1]
        x, y, c = lax.axis_index("x"), lax.axis_index("y"), lax.axis_index("c")
        me = 4 * x + 2 * y + c
        for t in range(n):
            for kk in range(1, N_DEV):
                px, py, pc = _peer_of(x, y, c, kk)
                src = ins[t].at[me] if flags[t] else ins[t]
                k1 = t * (N_DEV - 1) + kk - 1
                cp = pltpu.make_async_remote_copy(src_ref=src, dst_ref=dst[t].at[4 * px + 2 * py + pc],
                                                  send_sem=send.at[k1], recv_sem=recv.at[k1],
                                                  device_id=(px, py, pc), device_id_type=pl.DeviceIdType.MESH)
                cp.wait_send()
                cp.wait_recv()

    out = pl.pallas_call(
        body, name=name,
        in_specs=[hbm] * (2 * n) + [sem, sem, pl.BlockSpec(memory_space=pl.ANY)],
        out_specs=[hbm] * (2 * n),
        out_shape=[pltpu.HBM(a.shape, a.dtype) for a in srcs] + [pltpu.HBM(a.shape, a.dtype) for a in lands],
        input_output_aliases={i: i for i in range(2 * n)},
        compiler_params=pltpu.CompilerParams(has_side_effects=pltpu.SideEffectType.DATAFLOW_SIDE_EFFECTING),
    )(*srcs, *lands, send_s, recv_s, after)
    return out[n:]


def _sum8(name, parts):
    _, R, C = parts.shape

    def fn(r, c, o):
        acc = r[0][0].astype(F32)
        for s in range(1, N_DEV):
            acc = acc + r[0][s].astype(F32)
        o[0][...] = acc
        return ()

    tm = R if R % 256 else 256
    return _rowwise(name, fn, [parts], [], [_sds((R, C), F32)], tm=tm)[0]


def _adamw(name, w, g, m, v):
    bc1 = 1.0 - ADAM_B1 ** ADAM_STEP
    bc2 = 1.0 - ADAM_B2 ** ADAM_STEP

    def fn(r, c, o):
        wv, gv, mv, vv = (x[...] for x in r)
        mn = ADAM_B1 * mv + (1.0 - ADAM_B1) * gv
        vn = ADAM_B2 * vv + (1.0 - ADAM_B2) * (gv * gv)
        o[0][...] = -ADAM_LR * ((mn / bc1) / (jnp.sqrt(vn / bc2) + ADAM_EPS) + ADAM_WD * wv)
        o[1][...] = mn
        o[2][...] = vn
        return ()

    tm = max(8, min(512, 1 << ((16 << 20) // (56 * w.shape[1])).bit_length() - 1))
    return _rowwise(name, fn, [w, g, m, v], [], [_sds(w.shape, F32)] * 3, tm=tm)


def _rope_cs(pos, dim, theta):
    inv = jnp.float32(theta) ** (-jnp.arange(0, dim, 2, dtype=jnp.float32) / dim)
    ang = pos.astype(jnp.float32)[:, None] * inv[None, :]
    return jnp.cos(ang), jnp.sin(ang)


def _tables(S):
    pos = jnp.arange(S)
    c, s = _rope_cs(pos, MLA_ROPE, ROPE_THETA)
    mla = (jnp.concatenate([c, c, c, c], 1), jnp.concatenate([-s, s, -s, s], 1))
    rc, rs = _rope_cs(pos // GRID_W, HD // 2, AXIAL_THETA)
    cc, cs = _rope_cs(pos % GRID_W, HD // 2, AXIAL_THETA)
    axial = (jnp.concatenate([rc, rc, cc, cc], 1), jnp.concatenate([-rs, rs, -cs, cs], 1))
    c, s = _rope_cs(pos, SWA_ROT, ROPE_THETA)
    one, zero = jnp.ones((S, SWA_D - SWA_ROT), F32), jnp.zeros((S, SWA_D - SWA_ROT), F32)
    swa = (jnp.concatenate([c, c, one, c, c, one], 1), jnp.concatenate([-s, s, zero, -s, s, zero], 1))
    return mla, axial, swa


_WEIGHTS = ['even_norm', 'even_w_in', 'mla_q_lat_norm', 'mla_kv_lat_norm', 'mla_w_uq', 'mla_w_ukv', 'mla_q_norm',
            'mla_k_nope_norm', 'mla_k_rope_norm', 'gqa_q_norm', 'gqa_k_norm', 'even_w_out', 'odd_norm', 'odd_w_qkv',
            'swa_q_norm', 'swa_k_norm', 'swa_sink', 'odd_w_out', 'mlp_norm', 'mlp_w_up', 'mlp_w_down']
_SMALL = ['even_norm', 'mla_q_lat_norm', 'mla_kv_lat_norm', 'mla_q_norm', 'mla_k_nope_norm', 'mla_k_rope_norm',
          'gqa_q_norm', 'gqa_k_norm', 'swa_q_norm', 'swa_k_norm', 'swa_sink', 'mlp_norm']


def _relu2(acc):
    rl = jnp.maximum(acc, 0.0)
    return rl * rl, rl


def _mul2(acc, rl):
    return (acc * (2.0 * rl.astype(F32)),)


def _add(acc, res):
    return (acc + res,)


def _step(x, tgt, w, m, v):
    S, D = x.shape
    nw = len(_WEIGHTS)
    tab_mla, tab_ax, tab_swa = _tables(S)
    bf = lambda a: a.astype(BF16)

    w_up_s, w_dn_s = w['mlp_w_up'], w['mlp_w_down']
    first, tok0 = _exchange_start("gather_first_start", [
        (bf(w['even_w_in'][0].T), False), (bf(w['mla_w_uq'][0].T), False), (bf(w['mla_w_ukv'][0].T), False)])
    h0 = _rmsnorm("even_norm", x, w['even_norm'] + tok0[0:1, 0:1])
    gathered = _exchange_wait("gather_first_wait", first, h0)
    zero = jnp.minimum(jnp.abs(gathered[2][0, 0:1, 0:1].astype(F32)), 0.0)
    later, tok = _exchange_start("gather_rest_start", [
        (bf(w['even_w_out'][0]), False), (bf(w['odd_w_qkv'][0].T), False), (bf(w['odd_w_out'][0]), False),
        (bf(w_up_s[0].T), False), (bf(w_up_s[1].T), False), (bf(w_dn_s[0]), False), (bf(w_dn_s[1]), False),
        (w['odd_norm'] + zero, False)])
    flat = lambda a: a.reshape((a.shape[0] * a.shape[1],) + a.shape[2:])
    win_t, wuq_g, wukv_t = [flat(a) for a in gathered]
    win_t = jnp.concatenate([win_t[:KR_END], jnp.zeros((LANES - MLA_ROPE, D), BF16), win_t[KR_END:]], 0)
    wuq_g = gathered[1]
    wuq_t = jnp.concatenate([wuq_g[:, :HD].reshape(HEADS * HD, Q_LORA),
                             wuq_g[:, HD:].reshape(HEADS * MLA_ROPE, Q_LORA)], 0)

    z64 = jnp.zeros((1, 64), F32)
    qn = w['mla_q_norm']
    g_even = [w['mla_q_lat_norm'] + tok[0:1, 0:1], w['mla_kv_lat_norm'], jnp.concatenate([w['mla_k_rope_norm'], z64], 1),
              w['gqa_q_norm'], w['gqa_k_norm']]
    g_mla = [qn[:, :HD], jnp.concatenate([qn[:, HD:], qn[:, HD:]], 1), w['mla_k_nope_norm']]
    g_swa = [jnp.concatenate([w['swa_q_norm']] * 2, 1), jnp.concatenate([w['swa_k_norm']] * 2, 1)]

    def mlp_fwd(l, xin):
        hn = _rmsnorm(f"mlp{l}_norm", xin, w['mlp_norm'][l:l + 1])
        a, rl = _mm(f"mlp{l}_up", hn, wup_t[l], "nt", (BF16, BF16), epilogue=_relu2)
        xout = _mm(f"mlp{l}_down", a, wdn[l], "nn", (F32,), epilogue=_add, extras=(xin,))
        return xout, (hn, a, rl)

    proj = _mm("even_in", h0, win_t, "nt", (F32,), tm=512, tn=P_END)
    cqn, ckvn, krd, qg, kg, vg = _even_prep("even_prep", proj, tab_mla + tab_ax, g_even)
    qa = _mm("mla_uq", cqn, wuq_t, "nt", (F32,))
    kv = _mm("mla_ukv", ckvn, wukv_t, "nt", (F32,))
    q_a, k_a, v_a = _mla_prep("mla_prep", qa, kv, krd, tab_mla, g_mla)
    o_a, lse_a = _flash_fwd("mla_attn", q_a, k_a, v_a)
    o_g, lse_g = _flash_fwd("gqa_attn", qg, kg, vg)
    merged = jnp.concatenate([o_a, o_g], 1)
    rest = _exchange_wait("gather_rest_wait", later, merged)
    wout_e, wqkv_t, wout_o, wup0_t, wup1_t, wdn0, wdn1 = [flat(a) for a in rest[:7]]
    odd_norm = rest[7].reshape(1, D)
    wup_t, wdn = (wup0_t, wup1_t), (wdn0, wdn1)
    x1 = _mm("even_out", merged, wout_e, "nn", (F32,), epilogue=_add, extras=(x,))
    x2, mlp0 = mlp_fwd(0, x1)

    h1 = _rmsnorm("odd_norm", x2, odd_norm)
    qkv = _mm("odd_qkv", h1, wqkv_t, "nt", (F32,))
    q_s, k_s, v_s = _swa_prep("swa_prep", qkv, tab_swa, g_swa)
    o_s, lse_s = _swa_fwd("swa_attn", q_s, k_s, v_s, w['swa_sink'])
    x3 = _mm("odd_out", o_s, wout_o, "nn", (F32,), epilogue=_add, extras=(x2,))
    x4, mlp1 = mlp_fwd(1, x3)

    dy, loss_acc = _loss("loss", x4, tgt)
    loss = lax.psum(0.5 / D * loss_acc[0, 0], ("x", "y", "c"))

    gsm = {}

    def mlp_bwd(l, dout, xin, saved):
        hn, a, rl = saved
        du = _mm(f"mlp{l}_dact", dout, wdn[l], "nt", (BF16,), epilogue=_mul2, extras=(rl,))
        g_dn = _mm(f"mlp{l}_gdown", a, dout, "tn", (BF16,))
        g_up = _mm(f"mlp{l}_gup", du, hn, "tn", (BF16,))
        dhn = _mm(f"mlp{l}_dnorm", du, wup_t[l], "nn", (F32,))
        din, g_n = _rmsnorm_bwd(f"mlp{l}_norm_bwd", dout, xin, dhn, w['mlp_norm'][l:l + 1])
        return din, g_dn, g_up, g_n

    dx3, g_dn1, g_up1, g_mn1 = mlp_bwd(1, dy, x3, mlp1)
    split = lambda a: a.reshape((N_DEV, a.shape[0] // N_DEV) + a.shape[1:])
    sc1, tok1 = _exchange_start("scatter_mlp1_start", [(split(g_up1), True), (split(g_dn1), True)])
    sink = w['swa_sink'] + tok1[0:1, 0:SWA_HEADS]

    g_wout_o = _mm("odd_gout", o_s, dx3, "tn", (BF16,))
    do_s = _mm("odd_dattn", dx3, wout_o, "nt", (BF16,))
    dl_s = _delta_t("swa_delta", do_s, o_s)
    dq_s, dsink = _swa_bwd_q("swa_bwd_q", q_s, k_s, v_s, do_s, lse_s, dl_s, sink)
    dk_s, dv_s = _swa_bwd_kv("swa_bwd_kv", q_s, k_s, v_s, do_s, lse_s, dl_s)
    dqkv, g_sq, g_sk = _swa_prep_bwd("swa_prep_bwd", dq_s, dk_s, dv_s, qkv, tab_swa, g_swa)
    g_wqkv = _mm("odd_gqkv", dqkv, h1, "tn", (BF16,))
    dh1 = _mm("odd_dnorm", dqkv, wqkv_t, "nn", (F32,), tk=dqkv.shape[1])
    dx2, g_on = _rmsnorm_bwd("odd_norm_bwd", dx3, x2, dh1, odd_norm)
    gsm['swa_q_norm'] = g_sq[:, :64] + g_sq[:, 64:]
    gsm['swa_k_norm'] = g_sk[:, :64] + g_sk[:, 64:]
    gsm['swa_sink'] = dsink[:, 0].reshape(1, SWA_HEADS)

    dx1, g_dn0, g_up0, g_mn0 = mlp_bwd(0, dx2, x1, mlp0)
    sc2, tok2 = _exchange_start("scatter_mid_start", [(split(g_wout_o), True), (split(g_wqkv), True),
                                                      (split(g_up0), True), (split(g_dn0), True)])
    gsm['mlp_norm'] = jnp.concatenate([g_mn0, g_mn1], 0)

    g_wout_e = _mm("even_gout", merged, dx1, "tn", (BF16,))
    dmerged = _mm("even_dattn", dx1, wout_e, "nt", (BF16,))
    dl_e = _delta("even_delta", dmerged, merged)
    lse_e = jnp.concatenate([lse_a, lse_g], 0) + tok2[0, 0]
    dq_a, dk_a, dv_a = _flash_bwd("mla_attn_bwd", q_a, k_a, v_a, dmerged, 0, lse_e, dl_e)
    dq_g, dk_g, dv_g = _flash_bwd("gqa_attn_bwd", qg, kg, vg, dmerged, HEADS, lse_e, dl_e)
    dqa, dkv, dkrd, g_qnn, g_qnr, g_kn = _mla_prep_bwd("mla_prep_bwd", dq_a, dk_a, dv_a, qa, kv, tab_mla, g_mla)
    g_wuq = _mm("mla_guq", dqa, cqn, "tn", (BF16,))
    dcqn = _mm("mla_dq_lat", dqa, wuq_t, "nn", (F32,))
    g_wukv = _mm("mla_gukv", dkv, ckvn, "tn", (BF16,))
    dckvn = _mm("mla_dkv_lat", dkv, wukv_t, "nn", (F32,))
    dproj, g_ql, g_kvl, g_kr, g_gq, g_gk = _even_prep_bwd("even_prep_bwd", dcqn, dckvn, dkrd, dq_g, dk_g, dv_g,
                                                          proj, tab_mla + tab_ax, g_even)
    g_win = _mm("even_gin", dproj, h0, "tn", (BF16,), tm=P_END, tk=1024)
    dh0 = _mm("even_dnorm", dproj, win_t, "nn", (F32,), tk=P_END)
    grad_x, g_en = _rmsnorm_bwd("even_norm_bwd", dx1, x, dh0, w['even_norm'])
    gsm.update(even_norm=g_en, mla_q_lat_norm=g_ql, mla_kv_lat_norm=g_kvl,
               mla_q_norm=jnp.concatenate([g_qnn, g_qnr[:, :64] + g_qnr[:, 64:]], 1), mla_k_nope_norm=g_kn,
               mla_k_rope_norm=g_kr[:, :64], gqa_q_norm=g_gq, gqa_k_norm=g_gk)

    g_win = jnp.concatenate([g_win[:KR_END], g_win[P_QG:]], 0)
    g_wuq = jnp.concatenate([g_wuq[:HEADS * HD].reshape(HEADS, HD, Q_LORA),
                             g_wuq[HEADS * HD:].reshape(HEADS, MLA_ROPE, Q_LORA)], 1)
    small_sizes = [w[n].size for n in _SMALL] + [D]
    small_vec = jnp.concatenate([gsm[n].reshape(1, -1) for n in _SMALL] + [g_on], 1)
    pad = (-small_vec.shape[1]) % LANES
    small_vec = jnp.pad(small_vec, ((0, 0), (0, pad)))
    last_st, tok3 = _exchange_start("scatter_last_start", [(split(g_win), True), (g_wuq, True), (split(g_wukv), True),
                                                           (split(g_wout_e), True), (small_vec, False)])
    p_up1, p_dn1 = _exchange_wait("scatter_mlp1_wait", sc1, tok3)
    p_wout_o, p_wqkv, p_up0, p_dn0 = _exchange_wait("scatter_mid_wait", sc2, tok3)
    parts = dict(odd_w_qkv=p_wqkv, odd_w_out=p_wout_o, up0=p_up0, up1=p_up1, dn0=p_dn0, dn1=p_dn1)
    red = {n: _sum8("sum_" + n, p) for n, p in parts.items()}
    grads = {
        'odd_w_qkv': red['odd_w_qkv'].T[None], 'odd_w_out': red['odd_w_out'][None],
        'mlp_w_up': jnp.stack([red['up0'].T, red['up1'].T]), 'mlp_w_down': jnp.stack([red['dn0'], red['dn1']]),
    }
    delta, new_m, new_v = {}, {}, {}

    def apply(names):
        for n in names:
            shp = w[n].shape
            two = lambda a: a.reshape(shp[0] * shp[1], shp[2])
            d_, m_, v_ = _adamw("adamw_" + n, two(w[n]), two(grads[n]), two(m[n]), two(v[n]))
            delta[n], new_m[n], new_v[n] = d_.reshape(shp), m_.reshape(shp), v_.reshape(shp)

    apply(['odd_w_qkv', 'odd_w_out', 'mlp_w_up', 'mlp_w_down'])
    last = _exchange_wait("scatter_last_wait", last_st, new_v['mlp_w_down'])
    for n, p in zip(['even_w_in', 'mla_w_uq', 'mla_w_ukv', 'even_w_out'], last[:4]):
        red[n] = _sum8("sum_" + n, p)
    small_g = _sum8("sum_small", last[4])
    grads.update({'even_w_in': red['even_w_in'].T[None], 'mla_w_uq': red['mla_w_uq'].T[None],
                  'mla_w_ukv': red['mla_w_ukv'].T[None], 'even_w_out': red['even_w_out'][None]})
    apply(['even_w_in', 'mla_w_uq', 'mla_w_ukv', 'even_w_out'])
    off = 0
    for n, sz in zip(_SMALL + ['odd_norm_full'], small_sizes):
        seg = small_g[:, off:off + sz]
        off += sz
        if n == 'odd_norm_full':
            me = 4 * lax.axis_index("x") + 2 * lax.axis_index("y") + lax.axis_index("c")
            grads['odd_norm'] = lax.dynamic_slice(seg, (0, me * (D // N_DEV)), (1, D // N_DEV))
        else:
            grads[n] = seg.reshape(w[n].shape)

    sm_names = _SMALL + ['odd_norm']
    pack = lambda d: jnp.concatenate([d[n].reshape(1, -1) for n in sm_names], 1)
    pw, pg, pm, pv = pack(w), pack(grads), pack(m), pack(v)
    padw = (-pw.shape[1]) % LANES
    padf = lambda a: jnp.pad(a, ((0, 0), (0, padw)))
    d_, m_, v_ = _adamw("adamw_small", padf(pw), padf(pg), padf(pm), jnp.pad(pv, ((0, 0), (0, padw)), constant_values=1.0))
    off = 0
    for n in sm_names:
        sz = w[n].size
        delta[n] = d_[:, off:off + sz].reshape(w[n].shape)
        new_m[n] = m_[:, off:off + sz].reshape(w[n].shape)
        new_v[n] = v_[:, off:off + sz].reshape(w[n].shape)
        off += sz

    return (loss, grad_x[None], *[grads[n] for n in _WEIGHTS], *[delta[n] for n in _WEIGHTS],
            *[new_m[n] for n in _WEIGHTS], *[new_v[n] for n in _WEIGHTS])


def kernel(x, even_norm, even_w_in, mla_q_lat_norm, mla_kv_lat_norm, mla_w_uq, mla_w_ukv, mla_q_norm, mla_k_nope_norm, mla_k_rope_norm, gqa_q_norm, gqa_k_norm, even_w_out, odd_norm, odd_w_qkv, swa_q_norm, swa_k_norm, swa_sink, odd_w_out, mlp_norm, mlp_w_up, mlp_w_down, loss_target, m_even_norm, m_even_w_in, m_mla_q_lat_norm, m_mla_kv_lat_norm, m_mla_w_uq, m_mla_w_ukv, m_mla_q_norm, m_mla_k_nope_norm, m_mla_k_rope_norm, m_gqa_q_norm, m_gqa_k_norm, m_even_w_out, m_odd_norm, m_odd_w_qkv, m_swa_q_norm, m_swa_k_norm, m_swa_sink, m_odd_w_out, m_mlp_norm, m_mlp_w_up, m_mlp_w_down, v_even_norm, v_even_w_in, v_mla_q_lat_norm, v_mla_kv_lat_norm, v_mla_w_uq, v_mla_w_ukv, v_mla_q_norm, v_mla_k_nope_norm, v_mla_k_rope_norm, v_gqa_q_norm, v_gqa_k_norm, v_even_w_out, v_odd_norm, v_odd_w_qkv, v_swa_q_norm, v_swa_k_norm, v_swa_sink, v_odd_w_out, v_mlp_norm, v_mlp_w_up, v_mlp_w_down):
    ws = (even_norm, even_w_in, mla_q_lat_norm, mla_kv_lat_norm, mla_w_uq, mla_w_ukv, mla_q_norm, mla_k_nope_norm, mla_k_rope_norm, gqa_q_norm, gqa_k_norm, even_w_out, odd_norm, odd_w_qkv, swa_q_norm, swa_k_norm, swa_sink, odd_w_out, mlp_norm, mlp_w_up, mlp_w_down)
    ms = (m_even_norm, m_even_w_in, m_mla_q_lat_norm, m_mla_kv_lat_norm, m_mla_w_uq, m_mla_w_ukv, m_mla_q_norm, m_mla_k_nope_norm, m_mla_k_rope_norm, m_gqa_q_norm, m_gqa_k_norm, m_even_w_out, m_odd_norm, m_odd_w_qkv, m_swa_q_norm, m_swa_k_norm, m_swa_sink, m_odd_w_out, m_mlp_norm, m_mlp_w_up, m_mlp_w_down)
    vs = (v_even_norm, v_even_w_in, v_mla_q_lat_norm, v_mla_kv_lat_norm, v_mla_w_uq, v_mla_w_ukv, v_mla_q_norm, v_mla_k_nope_norm, v_mla_k_rope_norm, v_gqa_q_norm, v_gqa_k_norm, v_even_w_out, v_odd_norm, v_odd_w_qkv, v_swa_q_norm, v_swa_k_norm, v_swa_sink, v_odd_w_out, v_mlp_norm, v_mlp_w_up, v_mlp_w_down)
    return _step(x[0], loss_target[0], dict(zip(_WEIGHTS, ws)), dict(zip(_WEIGHTS, ms)), dict(zip(_WEIGHTS, vs)))
```

```python
import functools

import jax
import jax.numpy as jnp
from jax import lax
from jax.experimental import pallas as pl
from jax.experimental.pallas import tpu as pltpu

F32 = jnp.float32
BF16 = jnp.bfloat16

N_DEV = 8
NORM_EPS = 1e-6
ROPE_THETA = 500000.0
AXIAL_THETA = 10000.0
GRID_W = 64
HEADS = 8
GQA_KV = 2
HD = 128
MLA_ROPE = 64
MLA_QK = HD + MLA_ROPE
Q_LORA = 512
KV_LORA = 256
SWA_HEADS = 32
SWA_KV = 4
SWA_D = 64
SWA_ROT = 16
SWA_WINDOW = 128
SWA_BLOCK = 128
MLA_SCALE, GQA_SCALE, SWA_SCALE = MLA_QK ** -0.5, HD ** -0.5, SWA_D ** -0.5
LANES = 128
ADAM_LR, ADAM_B1, ADAM_B2, ADAM_EPS, ADAM_WD, ADAM_STEP = 0.001, 0.9, 0.999, 1e-08, 0.01, 10
VMEM_LIMIT = 56 * 1024 * 1024

P_CQ, P_CKV, P_KR, P_QG = 0, Q_LORA, Q_LORA + KV_LORA, Q_LORA + KV_LORA + LANES
P_KG = P_QG + HEADS * HD
P_VG = P_KG + GQA_KV * HD
P_END = P_VG + GQA_KV * HD
KR_END = Q_LORA + KV_LORA + MLA_ROPE


def _tile(n, prefs):
    for t in prefs:
        if n % t == 0 and t <= n:
            return t
    return n


def _params(sem):
    return pltpu.CompilerParams(dimension_semantics=sem, vmem_limit_bytes=VMEM_LIMIT)


_DIMS = {"nn": ((1,), (0,)), "nt": ((1,), (1,)), "tn": ((0,), (0,))}


def _mm(name, a, b, mode, out_dtypes, epilogue=None, extras=(), sums=(), tm=1024, tn=1024, tk=2048):
    if mode == "nn":
        (M, K), (_, N) = a.shape, b.shape
    elif mode == "nt":
        (M, K), (N, _) = a.shape, b.shape
    else:
        (K, M), (_, N) = a.shape, b.shape
    tm = _tile(M, (tm, 512, 256, 128))
    tn = _tile(N, (tn, 512, 256, 128))
    tk = _tile(K, (tk, 1024, 512, 256, 128))
    nk = K // tk
    ne, no, ns = len(extras), len(out_dtypes), len(sums)
    if mode == "tn":
        a_spec = pl.BlockSpec((tk, tm), lambda i, j, k: (k, i))
    else:
        a_spec = pl.BlockSpec((tm, tk), lambda i, j, k: (i, k))
    if mode == "nt":
        b_spec = pl.BlockSpec((tn, tk), lambda i, j, k: (j, k))
    else:
        b_spec = pl.BlockSpec((tk, tn), lambda i, j, k: (k, j))
    o_spec = pl.BlockSpec((tm, tn), lambda i, j, k: (i, j))
    dims = (_DIMS[mode], ((), ()))

    def body(a_ref, b_ref, *rest):
        ex, outs = rest[:ne], rest[ne:ne + no]
        k = pl.program_id(2)
        part = lax.dot_general(a_ref[...].astype(BF16), b_ref[...].astype(BF16), dims, preferred_element_type=F32)

        def finish(total):
            res = epilogue(total, *[e[...] for e in ex]) if epilogue else (total,)
            for o, r in zip(outs, res[:no]):
                o[...] = r.astype(o.dtype)
            if ns:
                first = (pl.program_id(0) == 0) & (pl.program_id(1) == 0)

                @pl.when(first)
                def _():
                    for s_ref, r in zip(rest[ne + no:ne + no + ns], res[no:]):
                        s_ref[...] = r

                @pl.when(jnp.logical_not(first))
                def _():
                    for s_ref, r in zip(rest[ne + no:ne + no + ns], res[no:]):
                        s_ref[...] += r

        if nk == 1:
            finish(part)
            return
        acc = rest[ne + no + ns]

        @pl.when(k == 0)
        def _():
            acc[...] = part

        @pl.when((k > 0) & (k < nk - 1))
        def _():
            acc[...] += part

        @pl.when(k == nk - 1)
        def _():
            finish(acc[...] + part)

    out = pl.pallas_call(
        body, name=name, grid=(M // tm, N // tn, nk),
        in_specs=[a_spec, b_spec] + [o_spec] * ne,
        out_specs=[o_spec] * no + [pl.BlockSpec(tuple(s.shape), lambda i, j, k: (0,) * len(s.shape)) for s in sums],
        out_shape=[jax.ShapeDtypeStruct((M, N), d) for d in out_dtypes] + list(sums),
        scratch_shapes=[pltpu.VMEM((tm, tn), F32)] if nk > 1 else [],
        compiler_params=_params(("arbitrary",) * 3 if ns else ("parallel", "parallel", "arbitrary")),
    )(a, b, *extras)
    return out[0] if no + ns == 1 else out


def _rowwise(name, fn, rows, consts, outs, accs=(), tm=256, after=()):
    S = rows[0].shape[-2]
    tm = _tile(S, (tm, 128, 64, 32, 16, 8))
    nr, nc, nd, no, na = len(rows), len(consts), len(after), len(outs), len(accs)

    def rspec(shape):
        if len(shape) == 2:
            return pl.BlockSpec((tm, shape[1]), lambda i: (i, 0))
        return pl.BlockSpec((shape[0], tm, shape[2]), lambda i: (0, i, 0))

    def cspec(shape):
        return pl.BlockSpec(tuple(shape), lambda i: (0,) * len(shape))

    def ospec(o):
        if isinstance(o, tuple):
            return pl.BlockSpec((o[1].shape[0], o[1].shape[1], tm), lambda i: (0, 0, i))
        return rspec(o.shape)

    out_sds = [o[1] if isinstance(o, tuple) else o for o in outs]

    def body(*refs):
        r, c = refs[:nr], refs[nr:nr + nc]
        o, a = refs[nr + nc + nd:nr + nc + nd + no], refs[nr + nc + nd + no:]
        vals = fn(r, c, o)
        if na:
            @pl.when(pl.program_id(0) == 0)
            def _():
                for ar in a:
                    ar[...] = jnp.zeros_like(ar)

            for ar, v in zip(a, vals):
                ar[...] += v

    res = pl.pallas_call(
        body, name=name, grid=(S // tm,),
        in_specs=[rspec(x.shape) for x in rows] + [cspec(x.shape) for x in consts]
        + [pl.BlockSpec(memory_space=pl.ANY)] * len(after),
        out_specs=[ospec(x) for x in outs] + [cspec(x.shape) for x in accs],
        out_shape=out_sds + list(accs),
        compiler_params=_params(("arbitrary",) if na else ("parallel",)),
    )(*rows, *consts, *after)
    return res


def _sds(shape, dtype):
    return jax.ShapeDtypeStruct(tuple(shape), dtype)


def _rsum(x):
    return jnp.sum(x, axis=-1, keepdims=True)


def _csum(x):
    return jnp.sum(x, axis=0, keepdims=True)


def _lane(shape):
    return lax.broadcasted_iota(jnp.int32, shape, 1)


def _partner(x, seg, half):
    lane = _lane(x.shape) % seg
    return jnp.where(lane < half, pltpu.roll(x, LANES - half, 1), pltpu.roll(x, half, 1))


def _rope(x, c, s, seg, half):
    return x * c + _partner(x, seg, half) * s


def _rope_bwd(dy, c, s, seg, half):
    t = _partner(dy * s, seg, half)
    if seg != 2 * half:
        t = jnp.where(_lane(dy.shape) % seg < 2 * half, t, 0.0)
    return dy * c + t


def _rms_r(ss, n):
    return lax.rsqrt(ss * (1.0 / n) + NORM_EPS)


def _rms_bwd(x, r, g, dy, dot_scale):
    dyg = dy * g
    return r * dyg - x * (r * r * r) * (_rsum(dyg * x) * dot_scale)


def _rmsnorm(name, x, g, after=()):
    D = x.shape[1]

    def fn(r, c, o):
        xv = r[0][...]
        o[0][...] = (xv * _rms_r(_rsum(xv * xv), D) * c[0][...]).astype(BF16)
        return ()

    return _rowwise(name, fn, [x], [g], [_sds(x.shape, BF16)], after=after)[0]


def _rmsnorm_bwd(name, dres, x, dh, g):
    D = x.shape[1]

    def fn(r, c, o):
        xv, dhv, gv = r[1][...], r[2][...], c[0][...]
        rr = _rms_r(_rsum(xv * xv), D)
        o[0][...] = r[0][...] + _rms_bwd(xv, rr, gv, dhv, 1.0 / D)
        return (_csum(dhv * xv * rr),)

    return _rowwise(name, fn, [dres, x, dh], [g], [_sds(x.shape, F32)], [_sds((1, D), F32)])


def _even_prep(name, proj, tabs, gains, after=()):
    S = proj.shape[0]
    cm, sm, ca, sa = tabs

    def fn(r, c, o):
        p, cmv, smv, cav, sav = r[0], r[1][...], r[2][...], r[3][...], r[4][...]
        g_ql, g_kvl, g_kr, g_q, g_k = (x[...] for x in c)
        cq = p[:, P_CQ:P_CKV]
        o[0][...] = (cq * _rms_r(_rsum(cq * cq), Q_LORA) * g_ql).astype(BF16)
        ckv = p[:, P_CKV:P_KR]
        o[1][...] = (ckv * _rms_r(_rsum(ckv * ckv), KV_LORA) * g_kvl).astype(BF16)
        kr = p[:, P_KR:P_QG]
        y = _rope(kr * _rms_r(_rsum(kr * kr), MLA_ROPE) * g_kr, cmv, smv, 64, 32)
        o[2][...] = (y + pltpu.roll(y, 64, 1)).astype(BF16)
        for h in range(HEADS):
            xh = p[:, P_QG + HD * h:P_QG + HD * (h + 1)]
            o[3][h] = (_rope(xh * _rms_r(_rsum(xh * xh), HD) * g_q, cav, sav, 64, 32) * GQA_SCALE).astype(BF16)
        for h in range(GQA_KV):
            xh = p[:, P_KG + HD * h:P_KG + HD * (h + 1)]
            o[4][h] = _rope(xh * _rms_r(_rsum(xh * xh), HD) * g_k, cav, sav, 64, 32).astype(BF16)
            o[5][h] = p[:, P_VG + HD * h:P_VG + HD * (h + 1)].astype(BF16)
        return ()

    outs = [_sds((S, Q_LORA), BF16), _sds((S, KV_LORA), BF16), _sds((S, LANES), BF16),
            _sds((HEADS, S, HD), BF16), _sds((GQA_KV, S, HD), BF16), _sds((GQA_KV, S, HD), BF16)]
    return _rowwise(name, fn, [proj, cm, sm, ca, sa], gains, outs, after=after)


def _even_prep_bwd(name, dcqn, dckvn, dkrd, dqg, dkg, dvg, proj, tabs, gains):
    S = proj.shape[0]
    cm, sm, ca, sa = tabs

    def fn(r, c, o):
        p, cmv, smv, cav, sav = r[6], r[7][...], r[8][...], r[9][...], r[10][...]
        g_ql, g_kvl, g_kr, g_q, g_k = (x[...] for x in c)
        out = o[0]
        cq = p[:, P_CQ:P_CKV]
        rr = _rms_r(_rsum(cq * cq), Q_LORA)
        d = r[0][...]
        out[:, P_CQ:P_CKV] = _rms_bwd(cq, rr, g_ql, d, 1.0 / Q_LORA).astype(BF16)
        a_ql = _csum(d * cq * rr)
        ckv = p[:, P_CKV:P_KR]
        rr = _rms_r(_rsum(ckv * ckv), KV_LORA)
        d = r[1][...]
        out[:, P_CKV:P_KR] = _rms_bwd(ckv, rr, g_kvl, d, 1.0 / KV_LORA).astype(BF16)
        a_kvl = _csum(d * ckv * rr)
        kr = p[:, P_KR:P_QG]
        rr = _rms_r(_rsum(kr * kr), MLA_ROPE)
        d = r[2][...]
        d = d + pltpu.roll(d, 64, 1)
        d = _rope_bwd(d, cmv, smv, 64, 32)
        low = _lane(d.shape) < 64
        out[:, P_KR:P_QG] = jnp.where(low, _rms_bwd(kr, rr, g_kr, d, 1.0 / MLA_ROPE), 0.0).astype(BF16)
        a_kr = _csum(d * kr * rr)
        a_q = jnp.zeros((1, HD), F32)
        for h in range(HEADS):
            xh = p[:, P_QG + HD * h:P_QG + HD * (h + 1)]
            rr = _rms_r(_rsum(xh * xh), HD)
            d = _rope_bwd(r[3][h] * GQA_SCALE, cav, sav, 64, 32)
            out[:, P_QG + HD * h:P_QG + HD * (h + 1)] = _rms_bwd(xh, rr, g_q, d, 1.0 / HD).astype(BF16)
            a_q = a_q + _csum(d * xh * rr)
        a_k = jnp.zeros((1, HD), F32)
        grp = HEADS // GQA_KV
        for h in range(GQA_KV):
            xh = p[:, P_KG + HD * h:P_KG + HD * (h + 1)]
            rr = _rms_r(_rsum(xh * xh), HD)
            d = r[4][grp * h]
            dv = r[5][grp * h]
            for e in range(1, grp):
                d = d + r[4][grp * h + e]
                dv = dv + r[5][grp * h + e]
            d = _rope_bwd(d, cav, sav, 64, 32)
            out[:, P_KG + HD * h:P_KG + HD * (h + 1)] = _rms_bwd(xh, rr, g_k, d, 1.0 / HD).astype(BF16)
            out[:, P_VG + HD * h:P_VG + HD * (h + 1)] = dv.astype(BF16)
            a_k = a_k + _csum(d * xh * rr)
        return (a_ql, a_kvl, a_kr, a_q, a_k)

    accs = [_sds((1, Q_LORA), F32), _sds((1, KV_LORA), F32), _sds((1, LANES), F32),
            _sds((1, HD), F32), _sds((1, HD), F32)]
    return _rowwise(name, fn, [dcqn, dckvn, dkrd, dqg, dkg, dvg, proj, cm, sm, ca, sa], gains,
                    [_sds((S, P_END), BF16)], accs, tm=128)


def _mla_q_parts(qa, h, rp, g_n, g_r):
    nope = qa[:, HD * h:HD * (h + 1)]
    mine = (_lane(rp.shape) >= 64) == bool(h % 2)
    rpm = jnp.where(mine, rp, 0.0)
    rr = _rms_r(_rsum(nope * nope) + _rsum(rpm * rpm), MLA_QK)
    return nope, rpm, mine, rr


def _mla_prep(name, qa, kv, krd, tabs, gains):
    S = qa.shape[0]
    cm, sm = tabs

    def fn(r, c, o):
        qa_r, kv_r, krd_v, cmv, smv = r[0], r[1], r[2][...], r[3][...], r[4][...]
        g_n, g_r, g_kn = (x[...] for x in c)
        for h in range(HEADS):
            rp = qa_r[:, HEADS * HD + LANES * (h // 2):HEADS * HD + LANES * (h // 2 + 1)]
            nope, rpm, mine, rr = _mla_q_parts(qa_r, h, rp, g_n, g_r)
            o[0][h, :, 0:HD] = (nope * (rr * MLA_SCALE) * g_n).astype(BF16)
            o[0][h, :, HD:2 * HD] = _rope(rpm * (rr * MLA_SCALE) * g_r, cmv, smv, 64, 32).astype(BF16)
            kn = kv_r[:, 2 * HD * h:2 * HD * h + HD]
            o[1][h, :, 0:HD] = (kn * _rms_r(_rsum(kn * kn), HD) * g_kn).astype(BF16)
            o[1][h, :, HD:2 * HD] = krd_v
            o[2][h] = kv_r[:, 2 * HD * h + HD:2 * HD * (h + 1)].astype(BF16)
        return ()

    outs = [_sds((HEADS, S, 2 * HD), BF16), _sds((HEADS, S, 2 * HD), BF16), _sds((HEADS, S, HD), BF16)]
    return _rowwise(name, fn, [qa, kv, krd, cm, sm], gains, outs)


def _mla_prep_bwd(name, dq, dk, dv, qa, kv, tabs, gains):
    S = qa.shape[0]
    cm, sm = tabs

    def fn(r, c, o):
        dq_r, dk_r, dv_r, qa_r, kv_r, cmv, smv = r[0], r[1], r[2], r[3], r[4], r[5][...], r[6][...]
        g_n, g_r, g_kn = (x[...] for x in c)
        a_n = jnp.zeros((1, HD), F32)
        a_r = jnp.zeros((1, LANES), F32)
        a_kn = jnp.zeros((1, HD), F32)
        dkrd = jnp.zeros(cmv.shape, F32)
        drp = None
        for h in range(HEADS):
            rp = qa_r[:, HEADS * HD + LANES * (h // 2):HEADS * HD + LANES * (h // 2 + 1)]
            nope, rpm, mine, rr = _mla_q_parts(qa_r, h, rp, g_n, g_r)
            dn = dq_r[h, :, 0:HD] * MLA_SCALE
            dr = _rope_bwd(jnp.where(mine, dq_r[h, :, HD:2 * HD] * MLA_SCALE, 0.0), cmv, smv, 64, 32)
            dot = (_rsum(dn * g_n * nope) + _rsum(dr * g_r * rpm)) * (1.0 / MLA_QK)
            r3 = rr * rr * rr
            o[0][:, HD * h:HD * (h + 1)] = (rr * dn * g_n - nope * r3 * dot).astype(BF16)
            part = jnp.where(mine, rr * dr * g_r - rpm * r3 * dot, 0.0)
            drp = part if h % 2 == 0 else drp + part
            if h % 2 == 1:
                o[0][:, HEADS * HD + LANES * (h // 2):HEADS * HD + LANES * (h // 2 + 1)] = drp.astype(BF16)
            a_n = a_n + _csum(dn * nope * rr)
            a_r = a_r + _csum(dr * rpm * rr)
            kn = kv_r[:, 2 * HD * h:2 * HD * h + HD]
            rk = _rms_r(_rsum(kn * kn), HD)
            dkn = dk_r[h, :, 0:HD]
            o[1][:, 2 * HD * h:2 * HD * h + HD] = _rms_bwd(kn, rk, g_kn, dkn, 1.0 / HD).astype(BF16)
            o[1][:, 2 * HD * h + HD:2 * HD * (h + 1)] = dv_r[h].astype(BF16)
            a_kn = a_kn + _csum(dkn * kn * rk)
            dkrd = dkrd + dk_r[h, :, HD:2 * HD]
        o[2][...] = dkrd
        return (a_n, a_r, a_kn)

    outs = [_sds(qa.shape, BF16), _sds(kv.shape, BF16), _sds((S, LANES), F32)]
    accs = [_sds((1, HD), F32), _sds((1, LANES), F32), _sds((1, HD), F32)]
    return _rowwise(name, fn, [dq, dk, dv, qa, kv, cm, sm], gains, outs, accs, tm=128)


def _seg_ones(seg):
    r = lax.broadcasted_iota(jnp.int32, (LANES, LANES), 0) // seg
    c = lax.broadcasted_iota(jnp.int32, (LANES, LANES), 1) // seg
    return (r == c).astype(F32)


def _seg_sum(x, ones):
    return lax.dot_general(x, ones, _NN, preferred_element_type=F32, precision=lax.Precision.HIGHEST)


def _seg64_r(x, ones):
    return _rms_r(_seg_sum(x * x, ones), SWA_D)


def _swa_prep(name, qkv, tabs, gains):
    S = qkv.shape[0]
    nq, nk = SWA_HEADS * SWA_D, SWA_KV * SWA_D
    cs, ss = tabs

    def fn(r, c, o):
        x_r, csv, ssv = r[0], r[1][...], r[2][...]
        g_q, g_k = c[0][...], c[1][...]
        ones = _seg_ones(SWA_D)
        for g in range((nq + nk) // LANES):
            x = x_r[:, LANES * g:LANES * (g + 1)]
            rr = _seg64_r(x, ones)
            y = _rope(x * rr * (g_q if g < nq // LANES else g_k), csv, ssv, SWA_D, SWA_ROT // 2)
            if g < nq // LANES:
                o[0][:, LANES * g:LANES * (g + 1)] = (y * SWA_SCALE).astype(BF16)
            else:
                o[1][:, LANES * g - nq:LANES * (g + 1) - nq] = y.astype(BF16)
        o[2][...] = x_r[:, nq + nk:nq + 2 * nk].astype(BF16)
        return ()

    outs = [_sds((S, nq), BF16), _sds((S, nk), BF16), _sds((S, nk), BF16)]
    return _rowwise(name, fn, [qkv, cs, ss], gains, outs)


def _swa_prep_bwd(name, dq, dk, dv, qkv, tabs, gains):
    nq, nk = SWA_HEADS * SWA_D, SWA_KV * SWA_D
    cs, ss = tabs

    def fn(r, c, o):
        dq_r, dk_r, x_r, csv, ssv = r[0], r[1], r[3], r[4][...], r[5][...]
        g_q, g_k = c[0][...], c[1][...]
        acc = [jnp.zeros((1, LANES), F32), jnp.zeros((1, LANES), F32)]
        ones = _seg_ones(SWA_D)
        for g in range((nq + nk) // LANES):
            isq = g < nq // LANES
            x = x_r[:, LANES * g:LANES * (g + 1)]
            rr = _seg64_r(x, ones)
            d = dq_r[:, LANES * g:LANES * (g + 1)] * SWA_SCALE if isq else dk_r[:, LANES * g - nq:LANES * (g + 1) - nq]
            d = _rope_bwd(d, csv, ssv, SWA_D, SWA_ROT // 2)
            dyg = d * (g_q if isq else g_k)
            dot = _seg_sum(dyg * x, ones) * (1.0 / SWA_D)
            o[0][:, LANES * g:LANES * (g + 1)] = (rr * dyg - x * (rr * rr * rr) * dot).astype(BF16)
            acc[0 if isq else 1] = acc[0 if isq else 1] + _csum(d * x * rr)
        o[0][:, nq + nk:nq + 2 * nk] = r[2][...].astype(BF16)
        return tuple(acc)

    return _rowwise(name, fn, [dq, dk, dv, qkv, cs, ss], gains, [_sds(qkv.shape, BF16)],
                    [_sds((1, LANES), F32), _sds((1, LANES), F32)], tm=128)


def _delta(name, do, o_, after=()):
    S, C = do.shape

    def fn(r, c, o):
        for g in range(C // LANES):
            t = r[0][:, LANES * g:LANES * (g + 1)].astype(F32) * r[1][:, LANES * g:LANES * (g + 1)].astype(F32)
            o[0][g] = _rsum(t)
        return ()

    return _rowwise(name, fn, [do, o_], [], [_sds((C // LANES, S, 1), F32)], after=after)[0]


def _delta_t(name, do, o_, after=()):
    S, C = do.shape

    def fn(r, c, o):
        row = lax.broadcasted_iota(jnp.int32, (8, LANES), 0)
        sel = (row == _lane((8, LANES)) // SWA_D).astype(F32)
        for g in range(C // LANES):
            t = r[0][:, LANES * g:LANES * (g + 1)].astype(F32) * r[1][:, LANES * g:LANES * (g + 1)].astype(F32)
            both = lax.dot_general(sel, t, _NT, preferred_element_type=F32, precision=lax.Precision.HIGHEST)
            o[0][2 * g] = both[0:1, :]
            o[0][2 * g + 1] = both[1:2, :]
        return ()

    return _rowwise(name, fn, [do, o_], [], [("T", _sds((2 * C // LANES, 1, S), F32))], after=after)[0]


_NT = (((1,), (1,)), ((), ()))
_NN = (((1,), (0,)), ((), ()))
_TN = (((0,), (0,)), ((), ()))


def _flash_fwd(name, q, k, v, tq=1024, tk=4096):
    H, S, dk = q.shape
    G = H // k.shape[0]
    dv = v.shape[2]
    tq, tk = _tile(S, (tq, 256, 128)), _tile(S, (tk, 256, 128))
    nk = S // tk

    def body(q_ref, k_ref, v_ref, o_ref, lse_ref, m_s, l_s, acc_s):
        j = pl.program_id(2)

        @pl.when(j == 0)
        def _():
            m_s[...] = jnp.full_like(m_s, -jnp.inf)
            l_s[...] = jnp.zeros_like(l_s)
            acc_s[...] = jnp.zeros_like(acc_s)

        s = lax.dot_general(q_ref[...], k_ref[...], _NT, preferred_element_type=F32)
        m_new = jnp.maximum(m_s[...], jnp.max(s, axis=-1, keepdims=True))
        alpha = jnp.exp(m_s[...] - m_new)
        p = jnp.exp(s - m_new)
        l_s[...] = alpha * l_s[...] + _rsum(p)
        acc_s[...] = alpha * acc_s[...] + lax.dot_general(p.astype(BF16), v_ref[...], _NN,
                                                          preferred_element_type=F32)
        m_s[...] = m_new

        @pl.when(j == nk - 1)
        def _():
            o_ref[...] = (acc_s[...] / l_s[...]).astype(o_ref.dtype)
            lse_ref[...] = m_s[...] + jnp.log(l_s[...])

    return pl.pallas_call(
        body, name=name, grid=(H, S // tq, nk),
        in_specs=[pl.BlockSpec((None, tq, dk), lambda h, i, j: (h, i, 0)),
                  pl.BlockSpec((None, tk, dk), lambda h, i, j: (h // G, j, 0)),
                  pl.BlockSpec((None, tk, dv), lambda h, i, j: (h // G, j, 0))],
        out_specs=[pl.BlockSpec((tq, dv), lambda h, i, j: (i, h)),
                   pl.BlockSpec((None, tq, 1), lambda h, i, j: (h, i, 0))],
        out_shape=[_sds((S, H * dv), BF16), _sds((H, S, 1), F32)],
        scratch_shapes=[pltpu.VMEM((tq, 1), F32), pltpu.VMEM((tq, 1), F32), pltpu.VMEM((tq, dv), F32)],
        compiler_params=_params(("parallel", "parallel", "arbitrary")),
    )(q, k, v)


def _flash_bwd(name, q, k, v, do, head0, lse, delta, tq=1024, tk=1024):
    H, S, dk = q.shape
    G = H // k.shape[0]
    dv = v.shape[2]
    tq, tk = _tile(S, (tq, 256, 128)), _tile(S, (tk, 256, 128))
    nq = S // tq

    def body(q_ref, k_ref, v_ref, do_ref, lse_ref, dl_ref, dq_ref, dk_ref, dv_ref, dk_s, dv_s):
        j, i = pl.program_id(1), pl.program_id(2)

        @pl.when(i == 0)
        def _():
            dk_s[...] = jnp.zeros_like(dk_s)
            dv_s[...] = jnp.zeros_like(dv_s)

        qv, kv_, dov = q_ref[...], k_ref[...], do_ref[...]
        s = lax.dot_general(qv, kv_, _NT, preferred_element_type=F32)
        p = jnp.exp(s - lse_ref[...])
        dp = lax.dot_general(dov, v_ref[...], _NT, preferred_element_type=F32)
        ds = (p * (dp - dl_ref[...])).astype(BF16)
        dv_s[...] += lax.dot_general(p.astype(BF16), dov, _TN, preferred_element_type=F32)
        dk_s[...] += lax.dot_general(ds, qv, _TN, preferred_element_type=F32)
        dqi = lax.dot_general(ds, kv_, _NN, preferred_element_type=F32)
        rows = pl.ds(pl.multiple_of(i * tq, tq), tq)

        @pl.when(j == 0)
        def _():
            dq_ref[rows, :] = dqi

        @pl.when(j > 0)
        def _():
            dq_ref[rows, :] += dqi

        @pl.when(i == nq - 1)
        def _():
            dk_ref[...] = dk_s[...]
            dv_ref[...] = dv_s[...]

    return pl.pallas_call(
        body, name=name, grid=(H, S // tk, nq),
        in_specs=[pl.BlockSpec((None, tq, dk), lambda h, j, i: (h, i, 0)),
                  pl.BlockSpec((None, tk, dk), lambda h, j, i: (h // G, j, 0)),
                  pl.BlockSpec((None, tk, dv), lambda h, j, i: (h // G, j, 0)),
                  pl.BlockSpec((tq, dv), lambda h, j, i: (i, head0 + h)),
                  pl.BlockSpec((None, tq, 1), lambda h, j, i: (h, i, 0)),
                  pl.BlockSpec((None, tq, 1), lambda h, j, i: (head0 + h, i, 0))],
        out_specs=[pl.BlockSpec((None, S, dk), lambda h, j, i: (h, 0, 0)),
                   pl.BlockSpec((None, tk, dk), lambda h, j, i: (h, j, 0)),
                   pl.BlockSpec((None, tk, dv), lambda h, j, i: (h, j, 0))],
        out_shape=[_sds((H, S, dk), F32), _sds((H, S, dk), F32), _sds((H, S, dv), F32)],
        scratch_shapes=[pltpu.VMEM((tk, dk), F32), pltpu.VMEM((tk, dv), F32)],
        compiler_params=_params(("parallel", "arbitrary", "arbitrary")),
    )(q, k, v, do, lse, delta)


def _swa_place(ref128, h):
    e, t = h % 2, (h // (SWA_HEADS // SWA_KV)) % 2
    x = ref128.astype(F32)
    if e != t:
        x = pltpu.roll(x, 64, 1)
    return jnp.where((_lane(x.shape) >= 64) == bool(t), x, 0.0).astype(BF16)


def _swa_unplace(y, h):
    e, t = h % 2, (h // (SWA_HEADS // SWA_KV)) % 2
    return pltpu.roll(y, 64, 1) if e != t else y


def _swa_specs(width, nb):
    prev = pl.BlockSpec((SWA_BLOCK, width), lambda i: (jnp.maximum(i - 1, 0), 0))
    cur = pl.BlockSpec((SWA_BLOCK, width), lambda i: (i, 0))
    nxt = pl.BlockSpec((SWA_BLOCK, width), lambda i: (jnp.minimum(i + 1, nb - 1), 0))
    return [prev, cur, nxt]


def _swa_valid_t(i, S, reps):
    shape = (3 * SWA_BLOCK, reps * SWA_BLOCK)
    kpos = (i - 1) * SWA_BLOCK + lax.broadcasted_iota(jnp.int32, shape, 0)
    qpos = i * SWA_BLOCK + lax.broadcasted_iota(jnp.int32, shape, 1) % SWA_BLOCK
    return (jnp.abs(qpos - kpos) <= SWA_WINDOW) & (kpos >= 0) & (kpos < S)


def _swa_stack(refs, heads):
    return jnp.concatenate([_swa_place(r[:, LANES * (h // 2):LANES * (h // 2 + 1)], h) for h in heads for r in refs],
                           axis=0)


def _swa_unstack_t(ot, j, out_ref):
    grp = SWA_HEADS // SWA_KV
    for pair in range(grp // 2):
        h = grp * j + 2 * pair
        a = _swa_unplace(ot[:, SWA_BLOCK * 2 * pair:SWA_BLOCK * (2 * pair + 1)].T, h)
        b = _swa_unplace(ot[:, SWA_BLOCK * (2 * pair + 1):SWA_BLOCK * (2 * pair + 2)].T, h + 1)
        out_ref[:, LANES * (h // 2):LANES * (h // 2 + 1)] = jnp.where(_lane(a.shape) < 64, a, b).astype(out_ref.dtype)


def _swa_fwd(name, q, k, v, sink):
    S = q.shape[0]
    nb = S // SWA_BLOCK
    grp = SWA_HEADS // SWA_KV
    smem = pl.BlockSpec(memory_space=pltpu.SMEM)

    def body(sink_ref, q_ref, kp, kc, kn, vp, vc, vn, o_ref, lse_ref):
        i = pl.program_id(0)
        valid = _swa_valid_t(i, S, grp)
        kcat = [jnp.concatenate([r[:, LANES * u:LANES * (u + 1)] for r in (kp, kc, kn)], axis=0) for u in range(2)]
        vcat = [jnp.concatenate([r[:, LANES * u:LANES * (u + 1)] for r in (vp, vc, vn)], axis=0) for u in range(2)]
        for j in range(SWA_KV):
            heads = range(grp * j, grp * (j + 1))
            xq = _swa_stack([q_ref], heads)
            sk = jnp.concatenate([jnp.full((1, SWA_BLOCK), sink_ref[0, h], F32) for h in heads], axis=1)
            st = lax.dot_general(kcat[j // 2], xq, _NT, preferred_element_type=F32)
            st = jnp.where(valid, st, -jnp.inf)
            m = jnp.maximum(jnp.max(st, axis=0, keepdims=True), sk)
            pt = jnp.exp(st - m)
            den = jnp.sum(pt, axis=0, keepdims=True) + jnp.exp(sk - m)
            ot = lax.dot_general(vcat[j // 2], pt.astype(BF16), _TN, preferred_element_type=F32) * (1.0 / den)
            lse = m + jnp.log(den)
            for e, h in enumerate(heads):
                lse_ref[h] = lse[:, SWA_BLOCK * e:SWA_BLOCK * (e + 1)]
            _swa_unstack_t(ot, j, o_ref)

    return pl.pallas_call(
        body, name=name, grid=(nb,),
        in_specs=[smem, pl.BlockSpec((SWA_BLOCK, q.shape[1]), lambda i: (i, 0))]
        + _swa_specs(k.shape[1], nb) + _swa_specs(v.shape[1], nb),
        out_specs=[pl.BlockSpec((SWA_BLOCK, q.shape[1]), lambda i: (i, 0)),
                   pl.BlockSpec((SWA_HEADS, 1, SWA_BLOCK), lambda i: (0, 0, i))],
        out_shape=[_sds(q.shape, BF16), _sds((SWA_HEADS, 1, S), F32)],
        compiler_params=_params(("parallel",)),
    )(sink, q, k, k, k, v, v, v)


def _swa_bwd_q(name, q, k, v, do, lse_t, delta_t, sink):
    S = q.shape[0]
    nb = S // SWA_BLOCK
    grp = SWA_HEADS // SWA_KV
    smem = pl.BlockSpec(memory_space=pltpu.SMEM)
    row = pl.BlockSpec((SWA_BLOCK, q.shape[1]), lambda i: (i, 0))
    hrow = pl.BlockSpec((SWA_HEADS, 1, SWA_BLOCK), lambda i: (0, 0, i))

    def body(sink_ref, q_ref, do_ref, lse_ref, dl_ref, kp, kc, kn, vp, vc, vn, dq_ref, ds_ref):
        i = pl.program_id(0)

        @pl.when(i == 0)
        def _():
            ds_ref[...] = jnp.zeros_like(ds_ref)

        valid = _swa_valid_t(i, S, grp)
        kcat = [jnp.concatenate([r[:, LANES * u:LANES * (u + 1)] for r in (kp, kc, kn)], axis=0) for u in range(2)]
        vcat = [jnp.concatenate([r[:, LANES * u:LANES * (u + 1)] for r in (vp, vc, vn)], axis=0) for u in range(2)]
        for j in range(SWA_KV):
            heads = range(grp * j, grp * (j + 1))
            xq = _swa_stack([q_ref], heads)
            xdo = _swa_stack([do_ref], heads)
            lse_r = jnp.concatenate([lse_ref[h] for h in heads], axis=1)
            dl_r = jnp.concatenate([dl_ref[h] for h in heads], axis=1)
            st = lax.dot_general(kcat[j // 2], xq, _NT, preferred_element_type=F32)
            pt = jnp.exp(jnp.where(valid, st, -jnp.inf) - lse_r)
            dpt = lax.dot_general(vcat[j // 2], xdo, _NT, preferred_element_type=F32)
            dst = (pt * (dpt - dl_r)).astype(BF16)
            _swa_unstack_t(lax.dot_general(kcat[j // 2], dst, _TN, preferred_element_type=F32), j, dq_ref)
            for h in heads:
                dsink = -_rsum(jnp.exp(sink_ref[0, h] - lse_ref[h]) * dl_ref[h])
                ds_ref[h:h + 1, :] += jnp.broadcast_to(dsink, (1, LANES))

    return pl.pallas_call(
        body, name=name, grid=(nb,),
        in_specs=[smem, row, row, hrow, hrow] + _swa_specs(k.shape[1], nb) + _swa_specs(v.shape[1], nb),
        out_specs=[row, pl.BlockSpec((SWA_HEADS, LANES), lambda i: (0, 0))],
        out_shape=[_sds(q.shape, F32), _sds((SWA_HEADS, LANES), F32)],
        compiler_params=_params(("arbitrary",)),
    )(sink, q, do, lse_t, delta_t, k, k, k, v, v, v)


def _swa_bwd_kv(name, q, k, v, do, lse_t, delta_t):
    S = q.shape[0]
    nb = S // SWA_BLOCK
    grp = SWA_HEADS // SWA_KV
    krow = pl.BlockSpec((SWA_BLOCK, k.shape[1]), lambda i: (i, 0))

    def stat3():
        return [pl.BlockSpec((SWA_HEADS, 1, SWA_BLOCK), lambda i: (0, 0, jnp.maximum(i - 1, 0))),
                pl.BlockSpec((SWA_HEADS, 1, SWA_BLOCK), lambda i: (0, 0, i)),
                pl.BlockSpec((SWA_HEADS, 1, SWA_BLOCK), lambda i: (0, 0, jnp.minimum(i + 1, nb - 1)))]

    def body(qp, qc, qn, dop, doc, don, lp, lc, ln, dp_, dc_, dn_, k_ref, v_ref, dk_ref, dv_ref):
        j = pl.program_id(0)
        nh = 2 * grp
        shape = (SWA_BLOCK, nh * 3 * SWA_BLOCK)
        qpos = (j - 1) * SWA_BLOCK + lax.broadcasted_iota(jnp.int32, shape, 1) % (3 * SWA_BLOCK)
        kpos = j * SWA_BLOCK + lax.broadcasted_iota(jnp.int32, shape, 0)
        valid = (jnp.abs(qpos - kpos) <= SWA_WINDOW) & (qpos >= 0) & (qpos < S)
        for u in range(SWA_KV // 2):
            heads = range(nh * u, nh * (u + 1))
            kc = k_ref[:, LANES * u:LANES * (u + 1)]
            vc = v_ref[:, LANES * u:LANES * (u + 1)]
            xq = _swa_stack([qp, qc, qn], heads)
            xdo = _swa_stack([dop, doc, don], heads)
            lse_r = jnp.concatenate([r[h] for h in heads for r in (lp, lc, ln)], axis=1)
            dl_r = jnp.concatenate([r[h] for h in heads for r in (dp_, dc_, dn_)], axis=1)
            st = lax.dot_general(kc, xq, _NT, preferred_element_type=F32)
            pt = jnp.where(valid, jnp.exp(st - lse_r), 0.0)
            dpt = lax.dot_general(vc, xdo, _NT, preferred_element_type=F32)
            dst = (pt * (dpt - dl_r)).astype(BF16)
            dv_ref[:, LANES * u:LANES * (u + 1)] = lax.dot_general(pt.astype(BF16), xdo, _NN, preferred_element_type=F32)
            dk_ref[:, LANES * u:LANES * (u + 1)] = lax.dot_general(dst, xq, _NN, preferred_element_type=F32)

    return pl.pallas_call(
        body, name=name, grid=(nb,),
        in_specs=_swa_specs(q.shape[1], nb) + _swa_specs(do.shape[1], nb) + stat3() + stat3() + [krow, krow],
        out_specs=[krow, krow],
        out_shape=[_sds(k.shape, F32), _sds(v.shape, F32)],
        compiler_params=_params(("parallel",)),
    )(q, q, q, do, do, do, lse_t, lse_t, lse_t, delta_t, delta_t, delta_t, k, v)


def _peer_of(x, y, c, kk):
    return (x ^ ((kk >> 2) & 1), y ^ ((kk >> 1) & 1), c ^ (kk & 1))


def _own_slot(a, scatter):
    me = 4 * lax.axis_index("x") + 2 * lax.axis_index("y") + lax.axis_index("c")
    shape = a.shape if scatter else (N_DEV,) + a.shape
    own = lax.dynamic_slice_in_dim(a, me, 1, 0) if scatter else a[None]
    return lax.dynamic_update_slice_in_dim(lax.empty(shape, a.dtype), own, me, 0)


def _exchange_start(name, tensors):
    n = len(tensors)
    hbm = pl.BlockSpec(memory_space=pltpu.HBM)
    sem = pl.BlockSpec(memory_space=pltpu.SEMAPHORE)
    srcs = [pltpu.with_memory_space_constraint(a, pltpu.HBM) for a, _ in tensors]
    lands = [pltpu.with_memory_space_constraint(_own_slot(a, sc), pltpu.HBM) for a, sc in tensors]

    def body(*refs):
        ins, dst = refs[:n], refs[n:2 * n]
        send, recv, token = refs[2 * n], refs[2 * n + 1], refs[4 * n + 2]
        x, y, c = lax.axis_index("x"), lax.axis_index("y"), lax.axis_index("c")
        me = 4 * x + 2 * y + c
        for t in range(n):
            for kk in range(1, N_DEV):
                px, py, pc = _peer_of(x, y, c, kk)
                src = ins[t].at[4 * px + 2 * py + pc] if tensors[t][1] else ins[t]
                k1 = t * (N_DEV - 1) + kk - 1
                pltpu.make_async_remote_copy(src_ref=src, dst_ref=dst[t].at[me], send_sem=send.at[k1],
                                             recv_sem=recv.at[k1], device_id=(px, py, pc),
                                             device_id_type=pl.DeviceIdType.MESH).start()
        token[...] = jnp.zeros_like(token)

    out = pl.pallas_call(
        body, name=name,
        in_specs=[hbm] * (2 * n),
        out_specs=[sem, sem] + [hbm] * (2 * n) + [pl.BlockSpec(memory_space=pltpu.VMEM)],
        out_shape=[pltpu.SemaphoreType.DMA((n * (N_DEV - 1),)), pltpu.SemaphoreType.DMA((n * (N_DEV - 1),))]
        + [pltpu.HBM(a.shape, a.dtype) for a in srcs] + [pltpu.HBM(a.shape, a.dtype) for a in lands]
        + [_sds((8, LANES), F32)],
        input_output_aliases={i: i + 2 for i in range(2 * n)},
        compiler_params=pltpu.CompilerParams(has_side_effects=pltpu.SideEffectType.DATAFLOW_SIDE_EFFECTING),
    )(*srcs, *lands)
    return (out[0], out[1], out[2:2 + n], out[2 + n:2 + 2 * n], [sc for _, sc in tensors]), out[2 + 2 * n]


def _exchange_wait(name, started, after):
    send_s, recv_s, srcs, lands, flags = started
    n = len(srcs)
    hbm = pl.BlockSpec(memory_space=pltpu.HBM)
    sem = pl.BlockSpec(memory_space=pltpu.SEMAPHORE)

    def body(*refs):
        ins, dst, send, recv = refs[:n], refs[n:2 * n], refs[2 * n], refs[2 * n + 1]
        x, y, c = lax.axis_index("x"), lax.axis_index("y"), lax.axis_index("c")
        me = 4 * x + 2 * y + c
        for t in range(n):
            for kk in range(1, N_DEV):
                px, py, pc = _peer_of(x, y, c, kk)
                src = ins[t].at[me] if flags[t] else ins[t]
                k1 = t * (N_DEV - 1) + kk - 1
                cp = pltpu.make_async_remote_copy(src_ref=src, dst_ref=dst[t].at[4 * px + 2 * py + pc],
                                                  send_sem=send.at[k1], recv_sem=recv.at[k1],
                                                  device_id=(px, py, pc), device_id_type=pl.DeviceIdType.MESH)
                cp.wait_send()
                cp.wait_recv()

    out = pl.pallas_call(
        body, name=name,
        in_specs=[hbm] * (2 * n) + [sem, sem, pl.BlockSpec(memory_space=pl.ANY)],
        out_specs=[hbm] * (2 * n),
        out_shape=[pltpu.HBM(a.shape, a.dtype) for a in srcs] + [pltpu.HBM(a.shape, a.dtype) for a in lands],
        input_output_aliases={i: i for i in range(2 * n)},
        compiler_params=pltpu.CompilerParams(has_side_effects=pltpu.SideEffectType.DATAFLOW_SIDE_EFFECTING),
    )(*srcs, *lands, send_s, recv_s, after)
    return out[n:]


def _sum8(name, parts):
    _, R, C = parts.shape

    def fn(r, c, o):
        acc = r[0][0].astype(F32)
        for s in range(1, N_DEV):
            acc = acc + r[0][s].astype(F32)
        o[0][...] = acc
        return ()

    tm = R if R % 256 else 256
    return _rowwise(name, fn, [parts], [], [_sds((R, C), F32)], tm=tm)[0]


def _adamw(name, w, g, m, v):
    bc1 = 1.0 - ADAM_B1 ** ADAM_STEP
    bc2 = 1.0 - ADAM_B2 ** ADAM_STEP

    def fn(r, c, o):
        wv, gv, mv, vv = (x[...] for x in r)
        mn = ADAM_B1 * mv + (1.0 - ADAM_B1) * gv
        vn = ADAM_B2 * vv + (1.0 - ADAM_B2) * (gv * gv)
        o[0][...] = -ADAM_LR * ((mn / bc1) / (jnp.sqrt(vn / bc2) + ADAM_EPS) + ADAM_WD * wv)
        o[1][...] = mn
        o[2][...] = vn
        return ()

    tm = max(8, min(512, 1 << ((16 << 20) // (56 * w.shape[1])).bit_length() - 1))
    return _rowwise(name, fn, [w, g, m, v], [], [_sds(w.shape, F32)] * 3, tm=tm)


def _rope_cs(pos, dim, theta):
    inv = jnp.float32(theta) ** (-jnp.arange(0, dim, 2, dtype=jnp.float32) / dim)
    ang = pos.astype(jnp.float32)[:, None] * inv[None, :]
    return jnp.cos(ang), jnp.sin(ang)


def _tables(S):
    pos = jnp.arange(S)
    c, s = _rope_cs(pos, MLA_ROPE, ROPE_THETA)
    mla = (jnp.concatenate([c, c, c, c], 1), jnp.concatenate([-s, s, -s, s], 1))
    rc, rs = _rope_cs(pos // GRID_W, HD // 2, AXIAL_THETA)
    cc, cs = _rope_cs(pos % GRID_W, HD // 2, AXIAL_THETA)
    axial = (jnp.concatenate([rc, rc, cc, cc], 1), jnp.concatenate([-rs, rs, -cs, cs], 1))
    c, s = _rope_cs(pos, SWA_ROT, ROPE_THETA)
    one, zero = jnp.ones((S, SWA_D - SWA_ROT), F32), jnp.zeros((S, SWA_D - SWA_ROT), F32)
    swa = (jnp.concatenate([c, c, one, c, c, one], 1), jnp.concatenate([-s, s, zero, -s, s, zero], 1))
    return mla, axial, swa


_WEIGHTS = ['even_norm', 'even_w_in', 'mla_q_lat_norm', 'mla_kv_lat_norm', 'mla_w_uq', 'mla_w_ukv', 'mla_q_norm',
            'mla_k_nope_norm', 'mla_k_rope_norm', 'gqa_q_norm', 'gqa_k_norm', 'even_w_out', 'odd_norm', 'odd_w_qkv',
            'swa_q_norm', 'swa_k_norm', 'swa_sink', 'odd_w_out', 'mlp_norm', 'mlp_w_up', 'mlp_w_down']
_SMALL = ['even_norm', 'mla_q_lat_norm', 'mla_kv_lat_norm', 'mla_q_norm', 'mla_k_nope_norm', 'mla_k_rope_norm',
          'gqa_q_norm', 'gqa_k_norm', 'swa_q_norm', 'swa_k_norm', 'swa_sink', 'mlp_norm']


def _relu2(acc):
    rl = jnp.maximum(acc, 0.0)
    return rl * rl, rl


def _mul2(acc, rl):
    return (acc * (2.0 * rl.astype(F32)),)


def _add(acc, res):
    return (acc + res,)


def _step(x, tgt, w, m, v):
    S, D = x.shape
    nw = len(_WEIGHTS)
    tab_mla, tab_ax, tab_swa = _tables(S)
    bf = lambda a: a.astype(BF16)

    w_up_s, w_dn_s = w['mlp_w_up'], w['mlp_w_down']
    first, tok0 = _exchange_start("gather_first_start", [
        (bf(w['even_w_in'][0].T), False), (bf(w['mla_w_uq'][0].T), False), (bf(w['mla_w_ukv'][0].T), False)])
    h0 = _rmsnorm("even_norm", x, w['even_norm'], after=[tok0])
    gathered = _exchange_wait("gather_first_wait", first, h0)
    zero = jnp.minimum(jnp.abs(gathered[2][0, 0:1, 0:1].astype(F32)), 0.0)
    later, tok = _exchange_start("gather_rest_start", [
        (bf(w['even_w_out'][0]), False), (bf(w['odd_w_qkv'][0].T), False), (bf(w['odd_w_out'][0]), False),
        (bf(w_up_s[0].T), False), (bf(w_up_s[1].T), False), (bf(w_dn_s[0]), False), (bf(w_dn_s[1]), False),
        (w['odd_norm'] + zero, False)])
    flat = lambda a: a.reshape((a.shape[0] * a.shape[1],) + a.shape[2:])
    win_t, wuq_g, wukv_t = [flat(a) for a in gathered]
    win_t = jnp.concatenate([win_t[:KR_END], jnp.zeros((LANES - MLA_ROPE, D), BF16), win_t[KR_END:]], 0)
    wuq_g = gathered[1]
    wuq_t = jnp.concatenate([wuq_g[:, :HD].reshape(HEADS * HD, Q_LORA),
                             wuq_g[:, HD:].reshape(HEADS * MLA_ROPE, Q_LORA)], 0)

    z64 = jnp.zeros((1, 64), F32)
    qn = w['mla_q_norm']
    g_even = [w['mla_q_lat_norm'], w['mla_kv_lat_norm'], jnp.concatenate([w['mla_k_rope_norm'], z64], 1),
              w['gqa_q_norm'], w['gqa_k_norm']]
    g_mla = [qn[:, :HD], jnp.concatenate([qn[:, HD:], qn[:, HD:]], 1), w['mla_k_nope_norm']]
    g_swa = [jnp.concatenate([w['swa_q_norm']] * 2, 1), jnp.concatenate([w['swa_k_norm']] * 2, 1)]

    def loss_head(acc, res, t):
        e = acc + res - t
        return e * (1.0 / D), _csum(_rsum(e * e))

    def mlp_fwd(l, xin, last=False):
        hn = _rmsnorm(f"mlp{l}_norm", xin, w['mlp_norm'][l:l + 1])
        a, rl = _mm(f"mlp{l}_up", hn, wup_t[l], "nt", (BF16, BF16), epilogue=_relu2)
        if last:
            xout = _mm(f"mlp{l}_down", a, wdn[l], "nn", (F32,), epilogue=loss_head, extras=(xin, tgt),
                       sums=[_sds((1, 1), F32)])
        else:
            xout = _mm(f"mlp{l}_down", a, wdn[l], "nn", (F32,), epilogue=_add, extras=(xin,))
        return xout, (hn, a, rl)

    proj = _mm("even_in", h0, win_t, "nt", (F32,), tm=512, tn=P_END)
    cqn, ckvn, krd, qg, kg, vg = _even_prep("even_prep", proj, tab_mla + tab_ax, g_even, after=[tok])
    qa = _mm("mla_uq", cqn, wuq_t, "nt", (F32,))
    kv = _mm("mla_ukv", ckvn, wukv_t, "nt", (F32,))
    q_a, k_a, v_a = _mla_prep("mla_prep", qa, kv, krd, tab_mla, g_mla)
    o_a, lse_a = _flash_fwd("mla_attn", q_a, k_a, v_a)
    o_g, lse_g = _flash_fwd("gqa_attn", qg, kg, vg)
    merged = jnp.concatenate([o_a, o_g], 1)
    rest = _exchange_wait("gather_rest_wait", later, merged)
    wout_e, wqkv_t, wout_o, wup0_t, wup1_t, wdn0, wdn1 = [flat(a) for a in rest[:7]]
    odd_norm = rest[7].reshape(1, D)
    wup_t, wdn = (wup0_t, wup1_t), (wdn0, wdn1)
    x1 = _mm("even_out", merged, wout_e, "nn", (F32,), epilogue=_add, extras=(x,))
    x2, mlp0 = mlp_fwd(0, x1)

    h1 = _rmsnorm("odd_norm", x2, odd_norm)
    qkv = _mm("odd_qkv", h1, wqkv_t, "nt", (F32,))
    q_s, k_s, v_s = _swa_prep("swa_prep", qkv, tab_swa, g_swa)
    o_s, lse_s = _swa_fwd("swa_attn", q_s, k_s, v_s, w['swa_sink'])
    x3 = _mm("odd_out", o_s, wout_o, "nn", (F32,), epilogue=_add, extras=(x2,))
    (dy, loss_acc), mlp1 = mlp_fwd(1, x3, last=True)
    loss = lax.psum(0.5 / D * loss_acc[0, 0], ("x", "y", "c"))

    gsm = {}

    def mlp_bwd(l, dout, xin, saved):
        hn, a, rl = saved
        du = _mm(f"mlp{l}_dact", dout, wdn[l], "nt", (BF16,), epilogue=_mul2, extras=(rl,))
        g_dn = _mm(f"mlp{l}_gdown", a, dout, "tn", (BF16,))
        g_up = _mm(f"mlp{l}_gup", du, hn, "tn", (BF16,))
        dhn = _mm(f"mlp{l}_dnorm", du, wup_t[l], "nn", (F32,))
        din, g_n = _rmsnorm_bwd(f"mlp{l}_norm_bwd", dout, xin, dhn, w['mlp_norm'][l:l + 1])
        return din, g_dn, g_up, g_n

    dx3, g_dn1, g_up1, g_mn1 = mlp_bwd(1, dy, x3, mlp1)
    split = lambda a: a.reshape((N_DEV, a.shape[0] // N_DEV) + a.shape[1:])
    sc1, tok1 = _exchange_start("scatter_mlp1_start", [(split(g_up1), True), (split(g_dn1), True)])

    g_wout_o = _mm("odd_gout", o_s, dx3, "tn", (BF16,))
    do_s = _mm("odd_dattn", dx3, wout_o, "nt", (BF16,))
    dl_s = _delta_t("swa_delta", do_s, o_s, after=[tok1])
    dq_s, dsink = _swa_bwd_q("swa_bwd_q", q_s, k_s, v_s, do_s, lse_s, dl_s, w['swa_sink'])
    dk_s, dv_s = _swa_bwd_kv("swa_bwd_kv", q_s, k_s, v_s, do_s, lse_s, dl_s)
    dqkv, g_sq, g_sk = _swa_prep_bwd("swa_prep_bwd", dq_s, dk_s, dv_s, qkv, tab_swa, g_swa)
    g_wqkv = _mm("odd_gqkv", dqkv, h1, "tn", (BF16,))
    dh1 = _mm("odd_dnorm", dqkv, wqkv_t, "nn", (F32,), tk=dqkv.shape[1])
    dx2, g_on = _rmsnorm_bwd("odd_norm_bwd", dx3, x2, dh1, odd_norm)
    gsm['swa_q_norm'] = g_sq[:, :64] + g_sq[:, 64:]
    gsm['swa_k_norm'] = g_sk[:, :64] + g_sk[:, 64:]
    gsm['swa_sink'] = dsink[:, 0].reshape(1, SWA_HEADS)

    dx1, g_dn0, g_up0, g_mn0 = mlp_bwd(0, dx2, x1, mlp0)
    sc2, tok2 = _exchange_start("scatter_mid_start", [(split(g_wout_o), True), (split(g_wqkv), True),
                                                      (split(g_up0), True), (split(g_dn0), True)])
    gsm['mlp_norm'] = jnp.concatenate([g_mn0, g_mn1], 0)

    g_wout_e = _mm("even_gout", merged, dx1, "tn", (BF16,))
    dmerged = _mm("even_dattn", dx1, wout_e, "nt", (BF16,))
    dl_e = _delta("even_delta", dmerged, merged, after=[tok2])
    dq_a, dk_a, dv_a = _flash_bwd("mla_attn_bwd", q_a, k_a, v_a, dmerged, 0, lse_a, dl_e)
    dq_g, dk_g, dv_g = _flash_bwd("gqa_attn_bwd", qg, kg, vg, dmerged, HEADS, lse_g, dl_e)
    dqa, dkv, dkrd, g_qnn, g_qnr, g_kn = _mla_prep_bwd("mla_prep_bwd", dq_a, dk_a, dv_a, qa, kv, tab_mla, g_mla)
    g_wuq = _mm("mla_guq", dqa, cqn, "tn", (BF16,))
    dcqn = _mm("mla_dq_lat", dqa, wuq_t, "nn", (F32,))
    g_wukv = _mm("mla_gukv", dkv, ckvn, "tn", (BF16,))
    dckvn = _mm("mla_dkv_lat", dkv, wukv_t, "nn", (F32,))
    dproj, g_ql, g_kvl, g_kr, g_gq, g_gk = _even_prep_bwd("even_prep_bwd", dcqn, dckvn, dkrd, dq_g, dk_g, dv_g,
                                                          proj, tab_mla + tab_ax, g_even)
    g_win = _mm("even_gin", dproj, h0, "tn", (BF16,), tm=P_END, tk=1024)
    dh0 = _mm("even_dnorm", dproj, win_t, "nn", (F32,), tk=P_END)
    grad_x, g_en = _rmsnorm_bwd("even_norm_bwd", dx1, x, dh0, w['even_norm'])
    gsm.update(even_norm=g_en, mla_q_lat_norm=g_ql, mla_kv_lat_norm=g_kvl,
               mla_q_norm=jnp.concatenate([g_qnn, g_qnr[:, :64] + g_qnr[:, 64:]], 1), mla_k_nope_norm=g_kn,
               mla_k_rope_norm=g_kr[:, :64], gqa_q_norm=g_gq, gqa_k_norm=g_gk)

    g_win = jnp.concatenate([g_win[:KR_END], g_win[P_QG:]], 0)
    g_wuq = jnp.concatenate([g_wuq[:HEADS * HD].reshape(HEADS, HD, Q_LORA),
                             g_wuq[HEADS * HD:].reshape(HEADS, MLA_ROPE, Q_LORA)], 1)
    small_sizes = [w[n].size for n in _SMALL] + [D]
    small_vec = jnp.concatenate([gsm[n].reshape(1, -1) for n in _SMALL] + [g_on], 1)
    pad = (-small_vec.shape[1]) % LANES
    small_vec = jnp.pad(small_vec, ((0, 0), (0, pad)))
    last_st, tok3 = _exchange_start("scatter_last_start", [(split(g_win), True), (g_wuq, True), (split(g_wukv), True),
                                                           (split(g_wout_e), True), (small_vec, False)])
    p_up1, p_dn1 = _exchange_wait("scatter_mlp1_wait", sc1, tok3)
    p_wout_o, p_wqkv, p_up0, p_dn0 = _exchange_wait("scatter_mid_wait", sc2, tok3)
    parts = dict(odd_w_qkv=p_wqkv, odd_w_out=p_wout_o, up0=p_up0, up1=p_up1, dn0=p_dn0, dn1=p_dn1)
    red = {n: _sum8("sum_" + n, p) for n, p in parts.items()}
    grads = {
        'odd_w_qkv': red['odd_w_qkv'].T[None], 'odd_w_out': red['odd_w_out'][None],
        'mlp_w_up': jnp.stack([red['up0'].T, red['up1'].T]), 'mlp_w_down': jnp.stack([red['dn0'], red['dn1']]),
    }
    delta, new_m, new_v = {}, {}, {}

    def apply(names):
        for n in names:
            shp = w[n].shape
            two = lambda a: a.reshape(shp[0] * shp[1], shp[2])
            d_, m_, v_ = _adamw("adamw_" + n, two(w[n]), two(grads[n]), two(m[n]), two(v[n]))
            delta[n], new_m[n], new_v[n] = d_.reshape(shp), m_.reshape(shp), v_.reshape(shp)

    apply(['odd_w_qkv', 'odd_w_out', 'mlp_w_up', 'mlp_w_down'])
    last = _exchange_wait("scatter_last_wait", last_st, new_v['mlp_w_down'])
    for n, p in zip(['even_w_in', 'mla_w_uq', 'mla_w_ukv', 'even_w_out'], last[:4]):
        red[n] = _sum8("sum_" + n, p)
    small_g = _sum8("sum_small", last[4])
    grads.update({'even_w_in': red['even_w_in'].T[None], 'mla_w_uq': red['mla_w_uq'].T[None],
                  'mla_w_ukv': red['mla_w_ukv'].T[None], 'even_w_out': red['even_w_out'][None]})
    apply(['even_w_in', 'mla_w_uq', 'mla_w_ukv', 'even_w_out'])
    off = 0
    for n, sz in zip(_SMALL + ['odd_norm_full'], small_sizes):
        seg = small_g[:, off:off + sz]
        off += sz
        if n == 'odd_norm_full':
            me = 4 * lax.axis_index("x") + 2 * lax.axis_index("y") + lax.axis_index("c")
            grads['odd_norm'] = lax.dynamic_slice(seg, (0, me * (D // N_DEV)), (1, D // N_DEV))
        else:
            grads[n] = seg.reshape(w[n].shape)

    sm_names = _SMALL + ['odd_norm']
    pack = lambda d: jnp.concatenate([d[n].reshape(1, -1) for n in sm_names], 1)
    pw, pg, pm, pv = pack(w), pack(grads), pack(m), pack(v)
    padw = (-pw.shape[1]) % LANES
    padf = lambda a: jnp.pad(a, ((0, 0), (0, padw)))
    d_, m_, v_ = _adamw("adamw_small", padf(pw), padf(pg), padf(pm), jnp.pad(pv, ((0, 0), (0, padw)), constant_values=1.0))
    off = 0
    for n in sm_names:
        sz = w[n].size
        delta[n] = d_[:, off:off + sz].reshape(w[n].shape)
        new_m[n] = m_[:, off:off + sz].reshape(w[n].shape)
        new_v[n] = v_[:, off:off + sz].reshape(w[n].shape)
        off += sz

    return (loss, grad_x[None], *[grads[n] for n in _WEIGHTS], *[delta[n] for n in _WEIGHTS],
            *[new_m[n] for n in _WEIGHTS], *[new_v[n] for n in _WEIGHTS])


def kernel(x, even_norm, even_w_in, mla_q_lat_norm, mla_kv_lat_norm, mla_w_uq, mla_w_ukv, mla_q_norm, mla_k_nope_norm, mla_k_rope_norm, gqa_q_norm, gqa_k_norm, even_w_out, odd_norm, odd_w_qkv, swa_q_norm, swa_k_norm, swa_sink, odd_w_out, mlp_norm, mlp_w_up, mlp_w_down, loss_target, m_even_norm, m_even_w_in, m_mla_q_lat_norm, m_mla_kv_lat_norm, m_mla_w_uq, m_mla_w_ukv, m_mla_q_norm, m_mla_k_nope_norm, m_mla_k_rope_norm, m_gqa_q_norm, m_gqa_k_norm, m_even_w_out, m_odd_norm, m_odd_w_qkv, m_swa_q_norm, m_swa_k_norm, m_swa_sink, m_odd_w_out, m_mlp_norm, m_mlp_w_up, m_mlp_w_down, v_even_norm, v_even_w_in, v_mla_q_lat_norm, v_mla_kv_lat_norm, v_mla_w_uq, v_mla_w_ukv, v_mla_q_norm, v_mla_k_nope_norm, v_mla_k_rope_norm, v_gqa_q_norm, v_gqa_k_norm, v_even_w_out, v_odd_norm, v_odd_w_qkv, v_swa_q_norm, v_swa_k_norm, v_swa_sink, v_odd_w_out, v_mlp_norm, v_mlp_w_up, v_mlp_w_down):
    ws = (even_norm, even_w_in, mla_q_lat_norm, mla_kv_lat_norm, mla_w_uq, mla_w_ukv, mla_q_norm, mla_k_nope_norm, mla_k_rope_norm, gqa_q_norm, gqa_k_norm, even_w_out, odd_norm, odd_w_qkv, swa_q_norm, swa_k_norm, swa_sink, odd_w_out, mlp_norm, mlp_w_up, mlp_w_down)
    ms = (m_even_norm, m_even_w_in, m_mla_q_lat_norm, m_mla_kv_lat_norm, m_mla_w_uq, m_mla_w_ukv, m_mla_q_norm, m_mla_k_nope_norm, m_mla_k_rope_norm, m_gqa_q_norm, m_gqa_k_norm, m_even_w_out, m_odd_norm, m_odd_w_qkv, m_swa_q_norm, m_swa_k_norm, m_swa_sink, m_odd_w_out, m_mlp_norm, m_mlp_w_up, m_mlp_w_down)
    vs = (v_even_norm, v_even_w_in, v_mla_q_lat_norm, v_mla_kv_lat_norm, v_mla_w_uq, v_mla_w_ukv, v_mla_q_norm, v_mla_k_nope_norm, v_mla_k_rope_norm, v_gqa_q_norm, v_gqa_k_norm, v_even_w_out, v_odd_norm, v_odd_w_qkv, v_swa_q_norm, v_swa_k_norm, v_swa_sink, v_odd_w_out, v_mlp_norm, v_mlp_w_up, v_mlp_w_down)
    return _step(x[0], loss_target[0], dict(zip(_WEIGHTS, ws)), dict(zip(_WEIGHTS, ms)), dict(zip(_WEIGHTS, vs)))
```

```python
import functools

import jax
import jax.numpy as jnp
from jax import lax
from jax.experimental import pallas as pl
from jax.experimental.pallas import tpu as pltpu

F32 = jnp.float32
BF16 = jnp.bfloat16

N_DEV = 8
NORM_EPS = 1e-6
ROPE_THETA = 500000.0
AXIAL_THETA = 10000.0
GRID_W = 64
HEADS = 8
GQA_KV = 2
HD = 128
MLA_ROPE = 64
MLA_QK = HD + MLA_ROPE
Q_LORA = 512
KV_LORA = 256
SWA_HEADS = 32
SWA_KV = 4
SWA_D = 64
SWA_ROT = 16
SWA_WINDOW = 128
SWA_BLOCK = 128
MLA_SCALE, GQA_SCALE, SWA_SCALE = MLA_QK ** -0.5, HD ** -0.5, SWA_D ** -0.5
LANES = 128
ADAM_LR, ADAM_B1, ADAM_B2, ADAM_EPS, ADAM_WD, ADAM_STEP = 0.001, 0.9, 0.999, 1e-08, 0.01, 10
VMEM_LIMIT = 56 * 1024 * 1024

P_CQ, P_CKV, P_KR, P_QG = 0, Q_LORA, Q_LORA + KV_LORA, Q_LORA + KV_LORA + LANES
P_KG = P_QG + HEADS * HD
P_VG = P_KG + GQA_KV * HD
P_END = P_VG + GQA_KV * HD
KR_END = Q_LORA + KV_LORA + MLA_ROPE


def _tile(n, prefs):
    for t in prefs:
        if n % t == 0 and t <= n:
            return t
    return n


def _params(sem):
    return pltpu.CompilerParams(dimension_semantics=sem, vmem_limit_bytes=VMEM_LIMIT)


_DIMS = {"nn": ((1,), (0,)), "nt": ((1,), (1,)), "tn": ((0,), (0,))}


def _mm(name, a, b, mode, out_dtypes, epilogue=None, extras=(), sums=(), tm=1024, tn=1024, tk=2048):
    if mode == "nn":
        (M, K), (_, N) = a.shape, b.shape
    elif mode == "nt":
        (M, K), (N, _) = a.shape, b.shape
    else:
        (K, M), (_, N) = a.shape, b.shape
    tm = _tile(M, (tm, 512, 256, 128))
    tn = _tile(N, (tn, 512, 256, 128))
    tk = _tile(K, (tk, 1024, 512, 256, 128))
    nk = K // tk
    ne, no, ns = len(extras), len(out_dtypes), len(sums)
    if mode == "tn":
        a_spec = pl.BlockSpec((tk, tm), lambda i, j, k: (k, i))
    else:
        a_spec = pl.BlockSpec((tm, tk), lambda i, j, k: (i, k))
    if mode == "nt":
        b_spec = pl.BlockSpec((tn, tk), lambda i, j, k: (j, k))
    else:
        b_spec = pl.BlockSpec((tk, tn), lambda i, j, k: (k, j))
    o_spec = pl.BlockSpec((tm, tn), lambda i, j, k: (i, j))
    dims = (_DIMS[mode], ((), ()))

    def body(a_ref, b_ref, *rest):
        ex, outs = rest[:ne], rest[ne:ne + no]
        k = pl.program_id(2)
        part = lax.dot_general(a_ref[...].astype(BF16), b_ref[...].astype(BF16), dims, preferred_element_type=F32)

        def finish(total):
            res = epilogue(total, *[e[...] for e in ex]) if epilogue else (total,)
            for o, r in zip(outs, res[:no]):
                o[...] = r.astype(o.dtype)
            if ns:
                first = (pl.program_id(0) == 0) & (pl.program_id(1) == 0)

                @pl.when(first)
                def _():
                    for s_ref, r in zip(rest[ne + no:ne + no + ns], res[no:]):
                        s_ref[...] = r

                @pl.when(jnp.logical_not(first))
                def _():
                    for s_ref, r in zip(rest[ne + no:ne + no + ns], res[no:]):
                        s_ref[...] += r

        if nk == 1:
            finish(part)
            return
        acc = rest[ne + no + ns]

        @pl.when(k == 0)
        def _():
            acc[...] = part

        @pl.when((k > 0) & (k < nk - 1))
        def _():
            acc[...] += part

        @pl.when(k == nk - 1)
        def _():
            finish(acc[...] + part)

    out = pl.pallas_call(
        body, name=name, grid=(M // tm, N // tn, nk),
        in_specs=[a_spec, b_spec] + [o_spec] * ne,
        out_specs=[o_spec] * no + [pl.BlockSpec(tuple(s.shape), lambda i, j, k: (0,) * len(s.shape)) for s in sums],
        out_shape=[jax.ShapeDtypeStruct((M, N), d) for d in out_dtypes] + list(sums),
        scratch_shapes=[pltpu.VMEM((tm, tn), F32)] if nk > 1 else [],
        compiler_params=_params(("arbitrary",) * 3 if ns else ("parallel", "parallel", "arbitrary")),
    )(a, b, *extras)
    return out[0] if no + ns == 1 else out


def _rowwise(name, fn, rows, consts, outs, accs=(), tm=256, after=()):
    S = rows[0].shape[-2]
    tm = _tile(S, (tm, 128, 64, 32, 16, 8))
    nr, nc, nd, no, na = len(rows), len(consts), len(after), len(outs), len(accs)

    def rspec(shape):
        if len(shape) == 2:
            return pl.BlockSpec((tm, shape[1]), lambda i: (i, 0))
        return pl.BlockSpec((shape[0], tm, shape[2]), lambda i: (0, i, 0))

    def cspec(shape):
        return pl.BlockSpec(tuple(shape), lambda i: (0,) * len(shape))

    def ospec(o):
        if isinstance(o, tuple):
            return pl.BlockSpec((o[1].shape[0], o[1].shape[1], tm), lambda i: (0, 0, i))
        return rspec(o.shape)

    out_sds = [o[1] if isinstance(o, tuple) else o for o in outs]

    def body(*refs):
        r, c = refs[:nr], refs[nr:nr + nc]
        o, a = refs[nr + nc + nd:nr + nc + nd + no], refs[nr + nc + nd + no:]
        vals = fn(r, c, o)
        if na:
            @pl.when(pl.program_id(0) == 0)
            def _():
                for ar in a:
                    ar[...] = jnp.zeros_like(ar)

            for ar, v in zip(a, vals):
                ar[...] += v

    res = pl.pallas_call(
        body, name=name, grid=(S // tm,),
        in_specs=[rspec(x.shape) for x in rows] + [cspec(x.shape) for x in consts]
        + [pl.BlockSpec(memory_space=pl.ANY)] * len(after),
        out_specs=[ospec(x) for x in outs] + [cspec(x.shape) for x in accs],
        out_shape=out_sds + list(accs),
        compiler_params=_params(("arbitrary",) if na else ("parallel",)),
    )(*rows, *consts, *after)
    return res


def _sds(shape, dtype):
    return jax.ShapeDtypeStruct(tuple(shape), dtype)


def _rsum(x):
    return jnp.sum(x, axis=-1, keepdims=True)


def _csum(x):
    return jnp.sum(x, axis=0, keepdims=True)


def _lane(shape):
    return lax.broadcasted_iota(jnp.int32, shape, 1)


def _partner(x, seg, half):
    lane = _lane(x.shape) % seg
    return jnp.where(lane < half, pltpu.roll(x, LANES - half, 1), pltpu.roll(x, half, 1))


def _rope(x, c, s, seg, half):
    return x * c + _partner(x, seg, half) * s


def _rope_bwd(dy, c, s, seg, half):
    t = _partner(dy * s, seg, half)
    if seg != 2 * half:
        t = jnp.where(_lane(dy.shape) % seg < 2 * half, t, 0.0)
    return dy * c + t


def _rms_r(ss, n):
    return lax.rsqrt(ss * (1.0 / n) + NORM_EPS)


def _rms_bwd(x, r, g, dy, dot_scale):
    dyg = dy * g
    return r * dyg - x * (r * r * r) * (_rsum(dyg * x) * dot_scale)


def _rmsnorm(name, x, g, after=()):
    D = x.shape[1]

    def fn(r, c, o):
        xv = r[0][...]
        o[0][...] = (xv * _rms_r(_rsum(xv * xv), D) * c[0][...]).astype(BF16)
        return ()

    return _rowwise(name, fn, [x], [g], [_sds(x.shape, BF16)], after=after)[0]


def _rmsnorm_bwd(name, dres, x, dh, g):
    D = x.shape[1]

    def fn(r, c, o):
        xv, dhv, gv = r[1][...], r[2][...], c[0][...]
        rr = _rms_r(_rsum(xv * xv), D)
        o[0][...] = r[0][...] + _rms_bwd(xv, rr, gv, dhv, 1.0 / D)
        return (_csum(dhv * xv * rr),)

    return _rowwise(name, fn, [dres, x, dh], [g], [_sds(x.shape, F32)], [_sds((1, D), F32)])


def _even_prep(name, proj, tabs, gains, after=()):
    S = proj.shape[0]
    cm, sm, ca, sa = tabs

    def fn(r, c, o):
        p, cmv, smv, cav, sav = r[0], r[1][...], r[2][...], r[3][...], r[4][...]
        g_ql, g_kvl, g_kr, g_q, g_k = (x[...] for x in c)
        cq = p[:, P_CQ:P_CKV]
        o[0][...] = (cq * _rms_r(_rsum(cq * cq), Q_LORA) * g_ql).astype(BF16)
        ckv = p[:, P_CKV:P_KR]
        o[1][...] = (ckv * _rms_r(_rsum(ckv * ckv), KV_LORA) * g_kvl).astype(BF16)
        kr = p[:, P_KR:P_QG]
        y = _rope(kr * _rms_r(_rsum(kr * kr), MLA_ROPE) * g_kr, cmv, smv, 64, 32)
        o[2][...] = (y + pltpu.roll(y, 64, 1)).astype(BF16)
        for h in range(HEADS):
            xh = p[:, P_QG + HD * h:P_QG + HD * (h + 1)]
            o[3][h] = (_rope(xh * _rms_r(_rsum(xh * xh), HD) * g_q, cav, sav, 64, 32) * GQA_SCALE).astype(BF16)
        for h in range(GQA_KV):
            xh = p[:, P_KG + HD * h:P_KG + HD * (h + 1)]
            o[4][h] = _rope(xh * _rms_r(_rsum(xh * xh), HD) * g_k, cav, sav, 64, 32).astype(BF16)
            o[5][h] = p[:, P_VG + HD * h:P_VG + HD * (h + 1)].astype(BF16)
        return ()

    outs = [_sds((S, Q_LORA), BF16), _sds((S, KV_LORA), BF16), _sds((S, LANES), BF16),
            _sds((HEADS, S, HD), BF16), _sds((GQA_KV, S, HD), BF16), _sds((GQA_KV, S, HD), BF16)]
    return _rowwise(name, fn, [proj, cm, sm, ca, sa], gains, outs, after=after)


def _even_prep_bwd(name, dcqn, dckvn, dkrd, dqg, dkg, dvg, proj, tabs, gains):
    S = proj.shape[0]
    cm, sm, ca, sa = tabs

    def fn(r, c, o):
        p, cmv, smv, cav, sav = r[6], r[7][...], r[8][...], r[9][...], r[10][...]
        g_ql, g_kvl, g_kr, g_q, g_k = (x[...] for x in c)
        out = o[0]
        cq = p[:, P_CQ:P_CKV]
        rr = _rms_r(_rsum(cq * cq), Q_LORA)
        d = r[0][...]
        out[:, P_CQ:P_CKV] = _rms_bwd(cq, rr, g_ql, d, 1.0 / Q_LORA).astype(BF16)
        a_ql = _csum(d * cq * rr)
        ckv = p[:, P_CKV:P_KR]
        rr = _rms_r(_rsum(ckv * ckv), KV_LORA)
        d = r[1][...]
        out[:, P_CKV:P_KR] = _rms_bwd(ckv, rr, g_kvl, d, 1.0 / KV_LORA).astype(BF16)
        a_kvl = _csum(d * ckv * rr)
        kr = p[:, P_KR:P_QG]
        rr = _rms_r(_rsum(kr * kr), MLA_ROPE)
        d = r[2][...]
        d = d + pltpu.roll(d, 64, 1)
        d = _rope_bwd(d, cmv, smv, 64, 32)
        low = _lane(d.shape) < 64
        out[:, P_KR:P_QG] = jnp.where(low, _rms_bwd(kr, rr, g_kr, d, 1.0 / MLA_ROPE), 0.0).astype(BF16)
        a_kr = _csum(d * kr * rr)
        a_q = jnp.zeros((1, HD), F32)
        for h in range(HEADS):
            xh = p[:, P_QG + HD * h:P_QG + HD * (h + 1)]
            rr = _rms_r(_rsum(xh * xh), HD)
            d = _rope_bwd(r[3][h] * GQA_SCALE, cav, sav, 64, 32)
            out[:, P_QG + HD * h:P_QG + HD * (h + 1)] = _rms_bwd(xh, rr, g_q, d, 1.0 / HD).astype(BF16)
            a_q = a_q + _csum(d * xh * rr)
        a_k = jnp.zeros((1, HD), F32)
        grp = HEADS // GQA_KV
        for h in range(GQA_KV):
            xh = p[:, P_KG + HD * h:P_KG + HD * (h + 1)]
            rr = _rms_r(_rsum(xh * xh), HD)
            d = r[4][grp * h]
            dv = r[5][grp * h]
            for e in range(1, grp):
                d = d + r[4][grp * h + e]
                dv = dv + r[5][grp * h + e]
            d = _rope_bwd(d, cav, sav, 64, 32)
            out[:, P_KG + HD * h:P_KG + HD * (h + 1)] = _rms_bwd(xh, rr, g_k, d, 1.0 / HD).astype(BF16)
            out[:, P_VG + HD * h:P_VG + HD * (h + 1)] = dv.astype(BF16)
            a_k = a_k + _csum(d * xh * rr)
        return (a_ql, a_kvl, a_kr, a_q, a_k)

    accs = [_sds((1, Q_LORA), F32), _sds((1, KV_LORA), F32), _sds((1, LANES), F32),
            _sds((1, HD), F32), _sds((1, HD), F32)]
    return _rowwise(name, fn, [dcqn, dckvn, dkrd, dqg, dkg, dvg, proj, cm, sm, ca, sa], gains,
                    [_sds((S, P_END), BF16)], accs, tm=128)


def _mla_q_parts(qa, h, rp, g_n, g_r):
    nope = qa[:, HD * h:HD * (h + 1)]
    mine = (_lane(rp.shape) >= 64) == bool(h % 2)
    rpm = jnp.where(mine, rp, 0.0)
    rr = _rms_r(_rsum(nope * nope) + _rsum(rpm * rpm), MLA_QK)
    return nope, rpm, mine, rr


def _mla_prep(name, qa, kv, krd, tabs, gains):
    S = qa.shape[0]
    cm, sm = tabs

    def fn(r, c, o):
        qa_r, kv_r, krd_v, cmv, smv = r[0], r[1], r[2][...], r[3][...], r[4][...]
        g_n, g_r, g_kn = (x[...] for x in c)
        for h in range(HEADS):
            rp = qa_r[:, HEADS * HD + LANES * (h // 2):HEADS * HD + LANES * (h // 2 + 1)]
            nope, rpm, mine, rr = _mla_q_parts(qa_r, h, rp, g_n, g_r)
            o[0][h, :, 0:HD] = (nope * (rr * MLA_SCALE) * g_n).astype(BF16)
            o[0][h, :, HD:2 * HD] = _rope(rpm * (rr * MLA_SCALE) * g_r, cmv, smv, 64, 32).astype(BF16)
            kn = kv_r[:, 2 * HD * h:2 * HD * h + HD]
            o[1][h, :, 0:HD] = (kn * _rms_r(_rsum(kn * kn), HD) * g_kn).astype(BF16)
            o[1][h, :, HD:2 * HD] = krd_v
            o[2][h] = kv_r[:, 2 * HD * h + HD:2 * HD * (h + 1)].astype(BF16)
        return ()

    outs = [_sds((HEADS, S, 2 * HD), BF16), _sds((HEADS, S, 2 * HD), BF16), _sds((HEADS, S, HD), BF16)]
    return _rowwise(name, fn, [qa, kv, krd, cm, sm], gains, outs)


def _mla_prep_bwd(name, dq, dk, dv, qa, kv, tabs, gains):
    S = qa.shape[0]
    cm, sm = tabs

    def fn(r, c, o):
        dq_r, dk_r, dv_r, qa_r, kv_r, cmv, smv = r[0], r[1], r[2], r[3], r[4], r[5][...], r[6][...]
        g_n, g_r, g_kn = (x[...] for x in c)
        a_n = jnp.zeros((1, HD), F32)
        a_r = jnp.zeros((1, LANES), F32)
        a_kn = jnp.zeros((1, HD), F32)
        dkrd = jnp.zeros(cmv.shape, F32)
        drp = None
        for h in range(HEADS):
            rp = qa_r[:, HEADS * HD + LANES * (h // 2):HEADS * HD + LANES * (h // 2 + 1)]
            nope, rpm, mine, rr = _mla_q_parts(qa_r, h, rp, g_n, g_r)
            dn = dq_r[h, :, 0:HD] * MLA_SCALE
            dr = _rope_bwd(jnp.where(mine, dq_r[h, :, HD:2 * HD] * MLA_SCALE, 0.0), cmv, smv, 64, 32)
            dot = (_rsum(dn * g_n * nope) + _rsum(dr * g_r * rpm)) * (1.0 / MLA_QK)
            r3 = rr * rr * rr
            o[0][:, HD * h:HD * (h + 1)] = (rr * dn * g_n - nope * r3 * dot).astype(BF16)
            part = jnp.where(mine, rr * dr * g_r - rpm * r3 * dot, 0.0)
            drp = part if h % 2 == 0 else drp + part
            if h % 2 == 1:
                o[0][:, HEADS * HD + LANES * (h // 2):HEADS * HD + LANES * (h // 2 + 1)] = drp.astype(BF16)
            a_n = a_n + _csum(dn * nope * rr)
            a_r = a_r + _csum(dr * rpm * rr)
            kn = kv_r[:, 2 * HD * h:2 * HD * h + HD]
            rk = _rms_r(_rsum(kn * kn), HD)
            dkn = dk_r[h, :, 0:HD]
            o[1][:, 2 * HD * h:2 * HD * h + HD] = _rms_bwd(kn, rk, g_kn, dkn, 1.0 / HD).astype(BF16)
            o[1][:, 2 * HD * h + HD:2 * HD * (h + 1)] = dv_r[h].astype(BF16)
            a_kn = a_kn + _csum(dkn * kn * rk)
            dkrd = dkrd + dk_r[h, :, HD:2 * HD]
        o[2][...] = dkrd
        return (a_n, a_r, a_kn)

    outs = [_sds(qa.shape, BF16), _sds(kv.shape, BF16), _sds((S, LANES), F32)]
    accs = [_sds((1, HD), F32), _sds((1, LANES), F32), _sds((1, HD), F32)]
    return _rowwise(name, fn, [dq, dk, dv, qa, kv, cm, sm], gains, outs, accs, tm=128)


def _seg_ones(seg):
    r = lax.broadcasted_iota(jnp.int32, (LANES, LANES), 0) // seg
    c = lax.broadcasted_iota(jnp.int32, (LANES, LANES), 1) // seg
    return (r == c).astype(F32)


def _seg_sum(x, ones):
    return lax.dot_general(x, ones, _NN, preferred_element_type=F32, precision=lax.Precision.HIGHEST)


def _seg64_r(x, ones):
    return _rms_r(_seg_sum(x * x, ones), SWA_D)


def _swa_prep(name, qkv, tabs, gains):
    S = qkv.shape[0]
    nq, nk = SWA_HEADS * SWA_D, SWA_KV * SWA_D
    cs, ss = tabs

    def fn(r, c, o):
        x_r, csv, ssv = r[0], r[1][...], r[2][...]
        g_q, g_k = c[0][...], c[1][...]
        ones = _seg_ones(SWA_D)
        for g in range((nq + nk) // LANES):
            x = x_r[:, LANES * g:LANES * (g + 1)]
            rr = _seg64_r(x, ones)
            y = _rope(x * rr * (g_q if g < nq // LANES else g_k), csv, ssv, SWA_D, SWA_ROT // 2)
            if g < nq // LANES:
                o[0][:, LANES * g:LANES * (g + 1)] = (y * SWA_SCALE).astype(BF16)
            else:
                o[1][:, LANES * g - nq:LANES * (g + 1) - nq] = y.astype(BF16)
        o[2][...] = x_r[:, nq + nk:nq + 2 * nk].astype(BF16)
        return ()

    outs = [_sds((S, nq), BF16), _sds((S, nk), BF16), _sds((S, nk), BF16)]
    return _rowwise(name, fn, [qkv, cs, ss], gains, outs)


def _swa_prep_bwd(name, dq, dk, dv, qkv, tabs, gains):
    nq, nk = SWA_HEADS * SWA_D, SWA_KV * SWA_D
    cs, ss = tabs

    def fn(r, c, o):
        dq_r, dk_r, x_r, csv, ssv = r[0], r[1], r[3], r[4][...], r[5][...]
        g_q, g_k = c[0][...], c[1][...]
        acc = [jnp.zeros((1, LANES), F32), jnp.zeros((1, LANES), F32)]
        ones = _seg_ones(SWA_D)
        for g in range((nq + nk) // LANES):
            isq = g < nq // LANES
            x = x_r[:, LANES * g:LANES * (g + 1)]
            rr = _seg64_r(x, ones)
            d = dq_r[:, LANES * g:LANES * (g + 1)] * SWA_SCALE if isq else dk_r[:, LANES * g - nq:LANES * (g + 1) - nq]
            d = _rope_bwd(d, csv, ssv, SWA_D, SWA_ROT // 2)
            dyg = d * (g_q if isq else g_k)
            dot = _seg_sum(dyg * x, ones) * (1.0 / SWA_D)
            o[0][:, LANES * g:LANES * (g + 1)] = (rr * dyg - x * (rr * rr * rr) * dot).astype(BF16)
            acc[0 if isq else 1] = acc[0 if isq else 1] + _csum(d * x * rr)
        o[0][:, nq + nk:nq + 2 * nk] = r[2][...].astype(BF16)
        return tuple(acc)

    return _rowwise(name, fn, [dq, dk, dv, qkv, cs, ss], gains, [_sds(qkv.shape, BF16)],
                    [_sds((1, LANES), F32), _sds((1, LANES), F32)], tm=128)


def _delta(name, do, o_, after=()):
    S, C = do.shape

    def fn(r, c, o):
        for g in range(C // LANES):
            t = r[0][:, LANES * g:LANES * (g + 1)].astype(F32) * r[1][:, LANES * g:LANES * (g + 1)].astype(F32)
            o[0][g] = _rsum(t)
        return ()

    return _rowwise(name, fn, [do, o_], [], [_sds((C // LANES, S, 1), F32)], after=after)[0]


def _delta_t(name, do, o_, after=()):
    S, C = do.shape

    def fn(r, c, o):
        row = lax.broadcasted_iota(jnp.int32, (8, LANES), 0)
        sel = (row == _lane((8, LANES)) // SWA_D).astype(F32)
        for g in range(C // LANES):
            t = r[0][:, LANES * g:LANES * (g + 1)].astype(F32) * r[1][:, LANES * g:LANES * (g + 1)].astype(F32)
            both = lax.dot_general(sel, t, _NT, preferred_element_type=F32, precision=lax.Precision.HIGHEST)
            o[0][2 * g] = both[0:1, :]
            o[0][2 * g + 1] = both[1:2, :]
        return ()

    return _rowwise(name, fn, [do, o_], [], [("T", _sds((2 * C // LANES, 1, S), F32))], after=after)[0]


_NT = (((1,), (1,)), ((), ()))
_NN = (((1,), (0,)), ((), ()))
_TN = (((0,), (0,)), ((), ()))


def _flash_fwd(name, q, k, v, tq=1024, tk=4096):
    H, S, dk = q.shape
    G = H // k.shape[0]
    dv = v.shape[2]
    tq, tk = _tile(S, (tq, 256, 128)), _tile(S, (tk, 256, 128))
    nk = S // tk

    def body(q_ref, k_ref, v_ref, o_ref, lse_ref, m_s, l_s, acc_s):
        j = pl.program_id(2)

        @pl.when(j == 0)
        def _():
            m_s[...] = jnp.full_like(m_s, -jnp.inf)
            l_s[...] = jnp.zeros_like(l_s)
            acc_s[...] = jnp.zeros_like(acc_s)

        s = lax.dot_general(q_ref[...], k_ref[...], _NT, preferred_element_type=F32)
        m_new = jnp.maximum(m_s[...], jnp.max(s, axis=-1, keepdims=True))
        alpha = jnp.exp(m_s[...] - m_new)
        p = jnp.exp(s - m_new)
        l_s[...] = alpha * l_s[...] + _rsum(p)
        acc_s[...] = alpha * acc_s[...] + lax.dot_general(p.astype(BF16), v_ref[...], _NN,
                                                          preferred_element_type=F32)
        m_s[...] = m_new

        @pl.when(j == nk - 1)
        def _():
            o_ref[...] = (acc_s[...] / l_s[...]).astype(o_ref.dtype)
            lse_ref[...] = m_s[...] + jnp.log(l_s[...])

    return pl.pallas_call(
        body, name=name, grid=(H, S // tq, nk),
        in_specs=[pl.BlockSpec((None, tq, dk), lambda h, i, j: (h, i, 0)),
                  pl.BlockSpec((None, tk, dk), lambda h, i, j: (h // G, j, 0)),
                  pl.BlockSpec((None, tk, dv), lambda h, i, j: (h // G, j, 0))],
        out_specs=[pl.BlockSpec((tq, dv), lambda h, i, j: (i, h)),
                   pl.BlockSpec((None, tq, 1), lambda h, i, j: (h, i, 0))],
        out_shape=[_sds((S, H * dv), BF16), _sds((H, S, 1), F32)],
        scratch_shapes=[pltpu.VMEM((tq, 1), F32), pltpu.VMEM((tq, 1), F32), pltpu.VMEM((tq, dv), F32)],
        compiler_params=_params(("parallel", "parallel", "arbitrary")),
    )(q, k, v)


def _flash_bwd(name, q, k, v, do, head0, lse, delta, tq=1024, tk=1024):
    H, S, dk = q.shape
    G = H // k.shape[0]
    dv = v.shape[2]
    tq, tk = _tile(S, (tq, 256, 128)), _tile(S, (tk, 256, 128))
    nq = S // tq

    def body(q_ref, k_ref, v_ref, do_ref, lse_ref, dl_ref, dq_ref, dk_ref, dv_ref, dk_s, dv_s):
        j, i = pl.program_id(1), pl.program_id(2)

        @pl.when(i == 0)
        def _():
            dk_s[...] = jnp.zeros_like(dk_s)
            dv_s[...] = jnp.zeros_like(dv_s)

        qv, kv_, dov = q_ref[...], k_ref[...], do_ref[...]
        s = lax.dot_general(qv, kv_, _NT, preferred_element_type=F32)
        p = jnp.exp(s - lse_ref[...])
        dp = lax.dot_general(dov, v_ref[...], _NT, preferred_element_type=F32)
        ds = (p * (dp - dl_ref[...])).astype(BF16)
        dv_s[...] += lax.dot_general(p.astype(BF16), dov, _TN, preferred_element_type=F32)
        dk_s[...] += lax.dot_general(ds, qv, _TN, preferred_element_type=F32)
        dqi = lax.dot_general(ds, kv_, _NN, preferred_element_type=F32)
        rows = pl.ds(pl.multiple_of(i * tq, tq), tq)

        @pl.when(j == 0)
        def _():
            dq_ref[rows, :] = dqi

        @pl.when(j > 0)
        def _():
            dq_ref[rows, :] += dqi

        @pl.when(i == nq - 1)
        def _():
            dk_ref[...] = dk_s[...]
            dv_ref[...] = dv_s[...]

    return pl.pallas_call(
        body, name=name, grid=(H, S // tk, nq),
        in_specs=[pl.BlockSpec((None, tq, dk), lambda h, j, i: (h, i, 0)),
                  pl.BlockSpec((None, tk, dk), lambda h, j, i: (h // G, j, 0)),
                  pl.BlockSpec((None, tk, dv), lambda h, j, i: (h // G, j, 0)),
                  pl.BlockSpec((tq, dv), lambda h, j, i: (i, head0 + h)),
                  pl.BlockSpec((None, tq, 1), lambda h, j, i: (h, i, 0)),
                  pl.BlockSpec((None, tq, 1), lambda h, j, i: (head0 + h, i, 0))],
        out_specs=[pl.BlockSpec((None, S, dk), lambda h, j, i: (h, 0, 0)),
                   pl.BlockSpec((None, tk, dk), lambda h, j, i: (h, j, 0)),
                   pl.BlockSpec((None, tk, dv), lambda h, j, i: (h, j, 0))],
        out_shape=[_sds((H, S, dk), F32), _sds((H, S, dk), F32), _sds((H, S, dv), F32)],
        scratch_shapes=[pltpu.VMEM((tk, dk), F32), pltpu.VMEM((tk, dv), F32)],
        compiler_params=_params(("parallel", "arbitrary", "arbitrary")),
    )(q, k, v, do, lse, delta)


def _swa_place(ref128, h):
    e, t = h % 2, (h // (SWA_HEADS // SWA_KV)) % 2
    x = ref128.astype(F32)
    if e != t:
        x = pltpu.roll(x, 64, 1)
    return jnp.where((_lane(x.shape) >= 64) == bool(t), x, 0.0).astype(BF16)


def _swa_unplace(y, h):
    e, t = h % 2, (h // (SWA_HEADS // SWA_KV)) % 2
    return pltpu.roll(y, 64, 1) if e != t else y


def _swa_specs(width, nb):
    prev = pl.BlockSpec((SWA_BLOCK, width), lambda i: (jnp.maximum(i - 1, 0), 0))
    cur = pl.BlockSpec((SWA_BLOCK, width), lambda i: (i, 0))
    nxt = pl.BlockSpec((SWA_BLOCK, width), lambda i: (jnp.minimum(i + 1, nb - 1), 0))
    return [prev, cur, nxt]


def _swa_bias_t(i, S):
    shape = (3 * SWA_BLOCK, SWA_BLOCK)
    kpos = (i - 1) * SWA_BLOCK + lax.broadcasted_iota(jnp.int32, shape, 0)
    qpos = i * SWA_BLOCK + lax.broadcasted_iota(jnp.int32, shape, 1)
    ok = (jnp.abs(qpos - kpos) <= SWA_WINDOW) & (kpos >= 0) & (kpos < S)
    return jnp.where(ok, 0.0, -jnp.inf)


def _add_blocks(x, blocks):
    n = x.shape[1] // LANES
    return jnp.concatenate([x[:, LANES * b:LANES * (b + 1)] + blocks[b % len(blocks)] for b in range(n)], axis=1)


def _swa_stack(refs, heads):
    return jnp.concatenate([_swa_place(r[:, LANES * (h // 2):LANES * (h // 2 + 1)], h) for h in heads for r in refs],
                           axis=0)


def _swa_unstack_t(ot, j, out_ref):
    grp = SWA_HEADS // SWA_KV
    for pair in range(grp // 2):
        h = grp * j + 2 * pair
        a = _swa_unplace(ot[:, SWA_BLOCK * 2 * pair:SWA_BLOCK * (2 * pair + 1)].T, h)
        b = _swa_unplace(ot[:, SWA_BLOCK * (2 * pair + 1):SWA_BLOCK * (2 * pair + 2)].T, h + 1)
        out_ref[:, LANES * (h // 2):LANES * (h // 2 + 1)] = jnp.where(_lane(a.shape) < 64, a, b).astype(out_ref.dtype)


def _swa_fwd(name, q, k, v, sink):
    S = q.shape[0]
    nb = S // SWA_BLOCK
    grp = SWA_HEADS // SWA_KV
    smem = pl.BlockSpec(memory_space=pltpu.SMEM)

    def body(sink_ref, q_ref, kp, kc, kn, vp, vc, vn, o_ref, lse_ref):
        i = pl.program_id(0)
        bias = [_swa_bias_t(i, S)]
        kcat = [jnp.concatenate([r[:, LANES * u:LANES * (u + 1)] for r in (kp, kc, kn)], axis=0) for u in range(2)]
        vcat = [jnp.concatenate([r[:, LANES * u:LANES * (u + 1)] for r in (vp, vc, vn)], axis=0) for u in range(2)]
        for j in range(SWA_KV):
            heads = range(grp * j, grp * (j + 1))
            xq = _swa_stack([q_ref], heads)
            sk = jnp.concatenate([jnp.full((1, SWA_BLOCK), sink_ref[0, h], F32) for h in heads], axis=1)
            st = lax.dot_general(kcat[j // 2], xq, _NT, preferred_element_type=F32)
            st = _add_blocks(st, bias)
            m = jnp.maximum(jnp.max(st, axis=0, keepdims=True), sk)
            pt = jnp.exp(st - m)
            den = jnp.sum(pt, axis=0, keepdims=True) + jnp.exp(sk - m)
            ot = lax.dot_general(vcat[j // 2], pt.astype(BF16), _TN, preferred_element_type=F32) * (1.0 / den)
            lse = m + jnp.log(den)
            for e, h in enumerate(heads):
                lse_ref[h] = lse[:, SWA_BLOCK * e:SWA_BLOCK * (e + 1)]
            _swa_unstack_t(ot, j, o_ref)

    return pl.pallas_call(
        body, name=name, grid=(nb,),
        in_specs=[smem, pl.BlockSpec((SWA_BLOCK, q.shape[1]), lambda i: (i, 0))]
        + _swa_specs(k.shape[1], nb) + _swa_specs(v.shape[1], nb),
        out_specs=[pl.BlockSpec((SWA_BLOCK, q.shape[1]), lambda i: (i, 0)),
                   pl.BlockSpec((SWA_HEADS, 1, SWA_BLOCK), lambda i: (0, 0, i))],
        out_shape=[_sds(q.shape, BF16), _sds((SWA_HEADS, 1, S), F32)],
        compiler_params=_params(("parallel",)),
    )(sink, q, k, k, k, v, v, v)


def _swa_bwd_q(name, q, k, v, do, lse_t, delta_t, sink):
    S = q.shape[0]
    nb = S // SWA_BLOCK
    grp = SWA_HEADS // SWA_KV
    smem = pl.BlockSpec(memory_space=pltpu.SMEM)
    row = pl.BlockSpec((SWA_BLOCK, q.shape[1]), lambda i: (i, 0))
    hrow = pl.BlockSpec((SWA_HEADS, 1, SWA_BLOCK), lambda i: (0, 0, i))

    def body(sink_ref, q_ref, do_ref, lse_ref, dl_ref, kp, kc, kn, vp, vc, vn, dq_ref, ds_ref):
        i = pl.program_id(0)

        @pl.when(i == 0)
        def _():
            ds_ref[...] = jnp.zeros_like(ds_ref)

        bias = [_swa_bias_t(i, S)]
        kcat = [jnp.concatenate([r[:, LANES * u:LANES * (u + 1)] for r in (kp, kc, kn)], axis=0) for u in range(2)]
        vcat = [jnp.concatenate([r[:, LANES * u:LANES * (u + 1)] for r in (vp, vc, vn)], axis=0) for u in range(2)]
        for j in range(SWA_KV):
            heads = range(grp * j, grp * (j + 1))
            xq = _swa_stack([q_ref], heads)
            xdo = _swa_stack([do_ref], heads)
            lse_r = jnp.concatenate([lse_ref[h] for h in heads], axis=1)
            dl_r = jnp.concatenate([dl_ref[h] for h in heads], axis=1)
            st = lax.dot_general(kcat[j // 2], xq, _NT, preferred_element_type=F32)
            pt = jnp.exp(_add_blocks(st, bias) - lse_r)
            dpt = lax.dot_general(vcat[j // 2], xdo, _NT, preferred_element_type=F32)
            dst = (pt * (dpt - dl_r)).astype(BF16)
            _swa_unstack_t(lax.dot_general(kcat[j // 2], dst, _TN, preferred_element_type=F32), j, dq_ref)
            for h in heads:
                dsink = -_rsum(jnp.exp(sink_ref[0, h] - lse_ref[h]) * dl_ref[h])
                ds_ref[h:h + 1, :] += jnp.broadcast_to(dsink, (1, LANES))

    return pl.pallas_call(
        body, name=name, grid=(nb,),
        in_specs=[smem, row, row, hrow, hrow] + _swa_specs(k.shape[1], nb) + _swa_specs(v.shape[1], nb),
        out_specs=[row, pl.BlockSpec((SWA_HEADS, LANES), lambda i: (0, 0))],
        out_shape=[_sds(q.shape, F32), _sds((SWA_HEADS, LANES), F32)],
        compiler_params=_params(("arbitrary",)),
    )(sink, q, do, lse_t, delta_t, k, k, k, v, v, v)


def _swa_bwd_kv(name, q, k, v, do, lse_t, delta_t):
    S = q.shape[0]
    nb = S // SWA_BLOCK
    grp = SWA_HEADS // SWA_KV
    krow = pl.BlockSpec((SWA_BLOCK, k.shape[1]), lambda i: (i, 0))

    def stat3():
        return [pl.BlockSpec((SWA_HEADS, 1, SWA_BLOCK), lambda i: (0, 0, jnp.maximum(i - 1, 0))),
                pl.BlockSpec((SWA_HEADS, 1, SWA_BLOCK), lambda i: (0, 0, i)),
                pl.BlockSpec((SWA_HEADS, 1, SWA_BLOCK), lambda i: (0, 0, jnp.minimum(i + 1, nb - 1)))]

    def body(qp, qc, qn, dop, doc, don, lp, lc, ln, dp_, dc_, dn_, k_ref, v_ref, dk_ref, dv_ref):
        j = pl.program_id(0)
        nh = 2 * grp
        kpos = j * SWA_BLOCK + lax.broadcasted_iota(jnp.int32, (SWA_BLOCK, SWA_BLOCK), 0)
        bias = []
        for b in range(3):
            qpos = (j - 1 + b) * SWA_BLOCK + lax.broadcasted_iota(jnp.int32, (SWA_BLOCK, SWA_BLOCK), 1)
            ok = (jnp.abs(qpos - kpos) <= SWA_WINDOW) & (qpos >= 0) & (qpos < S)
            bias.append(jnp.where(ok, 0.0, -jnp.inf))
        for u in range(SWA_KV // 2):
            heads = range(nh * u, nh * (u + 1))
            kc = k_ref[:, LANES * u:LANES * (u + 1)]
            vc = v_ref[:, LANES * u:LANES * (u + 1)]
            xq = _swa_stack([qp, qc, qn], heads)
            xdo = _swa_stack([dop, doc, don], heads)
            lse_r = jnp.concatenate([r[h] for h in heads for r in (lp, lc, ln)], axis=1)
            dl_r = jnp.concatenate([r[h] for h in heads for r in (dp_, dc_, dn_)], axis=1)
            st = lax.dot_general(kc, xq, _NT, preferred_element_type=F32)
            pt = jnp.exp(_add_blocks(st, bias) - lse_r)
            dpt = lax.dot_general(vc, xdo, _NT, preferred_element_type=F32)
            dst = (pt * (dpt - dl_r)).astype(BF16)
            dv_ref[:, LANES * u:LANES * (u + 1)] = lax.dot_general(pt.astype(BF16), xdo, _NN, preferred_element_type=F32)
            dk_ref[:, LANES * u:LANES * (u + 1)] = lax.dot_general(dst, xq, _NN, preferred_element_type=F32)

    return pl.pallas_call(
        body, name=name, grid=(nb,),
        in_specs=_swa_specs(q.shape[1], nb) + _swa_specs(do.shape[1], nb) + stat3() + stat3() + [krow, krow],
        out_specs=[krow, krow],
        out_shape=[_sds(k.shape, F32), _sds(v.shape, F32)],
        compiler_params=_params(("parallel",)),
    )(q, q, q, do, do, do, lse_t, lse_t, lse_t, delta_t, delta_t, delta_t, k, v)


def _peer_of(x, y, c, kk):
    return (x ^ ((kk >> 2) & 1), y ^ ((kk >> 1) & 1), c ^ (kk & 1))


def _own_slot(a, scatter):
    me = 4 * lax.axis_index("x") + 2 * lax.axis_index("y") + lax.axis_index("c")
    shape = a.shape if scatter else (N_DEV,) + a.shape
    own = lax.dynamic_slice_in_dim(a, me, 1, 0) if scatter else a[None]
    return lax.dynamic_update_slice_in_dim(lax.empty(shape, a.dtype), own, me, 0)


def _exchange_start(name, tensors):
    n = len(tensors)
    hbm = pl.BlockSpec(memory_space=pltpu.HBM)
    sem = pl.BlockSpec(memory_space=pltpu.SEMAPHORE)
    srcs = [pltpu.with_memory_space_constraint(a, pltpu.HBM) for a, _ in tensors]
    lands = [pltpu.with_memory_space_constraint(_own_slot(a, sc), pltpu.HBM) for a, sc in tensors]

    def body(*refs):
        ins, dst = refs[:n], refs[n:2 * n]
        send, recv, token = refs[2 * n], refs[2 * n + 1], refs[4 * n + 2]
        x, y, c = lax.axis_index("x"), lax.axis_index("y"), lax.axis_index("c")
        me = 4 * x + 2 * y + c
        for t in range(n):
            for kk in range(1, N_DEV):
                px, py, pc = _peer_of(x, y, c, kk)
                src = ins[t].at[4 * px + 2 * py + pc] if tensors[t][1] else ins[t]
                k1 = t * (N_DEV - 1) + kk - 1
                pltpu.make_async_remote_copy(src_ref=src, dst_ref=dst[t].at[me], send_sem=send.at[k1],
                                             recv_sem=recv.at[k1], device_id=(px, py, pc),
                                             device_id_type=pl.DeviceIdType.MESH).start()
        token[...] = jnp.zeros_like(token)

    out = pl.pallas_call(
        body, name=name,
        in_specs=[hbm] * (2 * n),
        out_specs=[sem, sem] + [hbm] * (2 * n) + [pl.BlockSpec(memory_space=pltpu.VMEM)],
        out_shape=[pltpu.SemaphoreType.DMA((n * (N_DEV - 1),)), pltpu.SemaphoreType.DMA((n * (N_DEV - 1),))]
        + [pltpu.HBM(a.shape, a.dtype) for a in srcs] + [pltpu.HBM(a.shape, a.dtype) for a in lands]
        + [_sds((8, LANES), F32)],
        input_output_aliases={i: i + 2 for i in range(2 * n)},
        compiler_params=pltpu.CompilerParams(has_side_effects=pltpu.SideEffectType.DATAFLOW_SIDE_EFFECTING),
    )(*srcs, *lands)
    return (out[0], out[1], out[2:2 + n], out[2 + n:2 + 2 * n], [sc for _, sc in tensors]), out[2 + 2 * n]


def _exchange_wait(name, started, after):
    send_s, recv_s, srcs, lands, flags = started
    n = len(srcs)
    hbm = pl.BlockSpec(memory_space=pltpu.HBM)
    sem = pl.BlockSpec(memory_space=pltpu.SEMAPHORE)

    def body(*refs):
        ins, dst, send, recv = refs[:n], refs[n:2 * n], refs[2 * n], refs[2 * n + 1]
        x, y, c = lax.axis_index("x"), lax.axis_index("y"), lax.axis_index("c")
        me = 4 * x + 2 * y + c
        for t in range(n):
            for kk in range(1, N_DEV):
                px, py, pc = _peer_of(x, y, c, kk)
                src = ins[t].at[me] if flags[t] else ins[t]
                k1 = t * (N_DEV - 1) + kk - 1
                cp = pltpu.make_async_remote_copy(src_ref=src, dst_ref=dst[t].at[4 * px + 2 * py + pc],
                                                  send_sem=send.at[k1], recv_sem=recv.at[k1],
                                                  device_id=(px, py, pc), device_id_type=pl.DeviceIdType.MESH)
                cp.wait_send()
                cp.wait_recv()

    out = pl.pallas_call(
        body, name=name,
        in_specs=[hbm] * (2 * n) + [sem, sem, pl.BlockSpec(memory_space=pl.ANY)],
        out_specs=[hbm] * (2 * n),
        out_shape=[pltpu.HBM(a.shape, a.dtype) for a in srcs] + [pltpu.HBM(a.shape, a.dtype) for a in lands],
        input_output_aliases={i: i for i in range(2 * n)},
        compiler_params=pltpu.CompilerParams(has_side_effects=pltpu.SideEffectType.DATAFLOW_SIDE_EFFECTING),
    )(*srcs, *lands, send_s, recv_s, after)
    return out[n:]


def _sum8(name, parts):
    _, R, C = parts.shape

    def fn(r, c, o):
        acc = r[0][0].astype(F32)
        for s in range(1, N_DEV):
            acc = acc + r[0][s].astype(F32)
        o[0][...] = acc
        return ()

    tm = R if R % 256 else 256
    return _rowwise(name, fn, [parts], [], [_sds((R, C), F32)], tm=tm)[0]


def _adamw(name, w, g, m, v):
    bc1 = 1.0 - ADAM_B1 ** ADAM_STEP
    bc2 = 1.0 - ADAM_B2 ** ADAM_STEP

    def fn(r, c, o):
        wv, gv, mv, vv = (x[...] for x in r)
        mn = ADAM_B1 * mv + (1.0 - ADAM_B1) * gv
        vn = ADAM_B2 * vv + (1.0 - ADAM_B2) * (gv * gv)
        o[0][...] = -ADAM_LR * ((mn / bc1) / (jnp.sqrt(vn / bc2) + ADAM_EPS) + ADAM_WD * wv)
        o[1][...] = mn
        o[2][...] = vn
        return ()

    tm = max(8, min(512, 1 << ((16 << 20) // (56 * w.shape[1])).bit_length() - 1))
    return _rowwise(name, fn, [w, g, m, v], [], [_sds(w.shape, F32)] * 3, tm=tm)


def _rope_cs(pos, dim, theta):
    inv = jnp.float32(theta) ** (-jnp.arange(0, dim, 2, dtype=jnp.float32) / dim)
    ang = pos.astype(jnp.float32)[:, None] * inv[None, :]
    return jnp.cos(ang), jnp.sin(ang)


def _tables(S):
    pos = jnp.arange(S)
    c, s = _rope_cs(pos, MLA_ROPE, ROPE_THETA)
    mla = (jnp.concatenate([c, c, c, c], 1), jnp.concatenate([-s, s, -s, s], 1))
    rc, rs = _rope_cs(pos // GRID_W, HD // 2, AXIAL_THETA)
    cc, cs = _rope_cs(pos % GRID_W, HD // 2, AXIAL_THETA)
    axial = (jnp.concatenate([rc, rc, cc, cc], 1), jnp.concatenate([-rs, rs, -cs, cs], 1))
    c, s = _rope_cs(pos, SWA_ROT, ROPE_THETA)
    one, zero = jnp.ones((S, SWA_D - SWA_ROT), F32), jnp.zeros((S, SWA_D - SWA_ROT), F32)
    swa = (jnp.concatenate([c, c, one, c, c, one], 1), jnp.concatenate([-s, s, zero, -s, s, zero], 1))
    return mla, axial, swa


_WEIGHTS = ['even_norm', 'even_w_in', 'mla_q_lat_norm', 'mla_kv_lat_norm', 'mla_w_uq', 'mla_w_ukv', 'mla_q_norm',
            'mla_k_nope_norm', 'mla_k_rope_norm', 'gqa_q_norm', 'gqa_k_norm', 'even_w_out', 'odd_norm', 'odd_w_qkv',
            'swa_q_norm', 'swa_k_norm', 'swa_sink', 'odd_w_out', 'mlp_norm', 'mlp_w_up', 'mlp_w_down']
_SMALL = ['even_norm', 'mla_q_lat_norm', 'mla_kv_lat_norm', 'mla_q_norm', 'mla_k_nope_norm', 'mla_k_rope_norm',
          'gqa_q_norm', 'gqa_k_norm', 'swa_q_norm', 'swa_k_norm', 'swa_sink', 'mlp_norm']


def _relu2(acc):
    rl = jnp.maximum(acc, 0.0)
    return rl * rl, rl


def _mul2(acc, rl):
    return (acc * (2.0 * rl.astype(F32)),)


def _add(acc, res):
    return (acc + res,)


def _step(x, tgt, w, m, v):
    S, D = x.shape
    nw = len(_WEIGHTS)
    tab_mla, tab_ax, tab_swa = _tables(S)
    bf = lambda a: a.astype(BF16)

    w_up_s, w_dn_s = w['mlp_w_up'], w['mlp_w_down']
    first, tok0 = _exchange_start("gather_first_start", [
        (bf(w['even_w_in'][0].T), False), (bf(w['mla_w_uq'][0].T), False), (bf(w['mla_w_ukv'][0].T), False)])
    h0 = _rmsnorm("even_norm", x, w['even_norm'], after=[tok0])
    gathered = _exchange_wait("gather_first_wait", first, h0)
    zero = jnp.minimum(jnp.abs(gathered[2][0, 0:1, 0:1].astype(F32)), 0.0)
    later, tok = _exchange_start("gather_rest_start", [
        (bf(w['even_w_out'][0]), False), (bf(w['odd_w_qkv'][0].T), False), (bf(w['odd_w_out'][0]), False),
        (bf(w_up_s[0].T), False), (bf(w_up_s[1].T), False), (bf(w_dn_s[0]), False), (bf(w_dn_s[1]), False),
        (w['odd_norm'] + zero, False)])
    flat = lambda a: a.reshape((a.shape[0] * a.shape[1],) + a.shape[2:])
    win_t, wuq_g, wukv_t = [flat(a) for a in gathered]
    win_t = jnp.concatenate([win_t[:KR_END], jnp.zeros((LANES - MLA_ROPE, D), BF16), win_t[KR_END:]], 0)
    wuq_g = gathered[1]
    wuq_t = jnp.concatenate([wuq_g[:, :HD].reshape(HEADS * HD, Q_LORA),
                             wuq_g[:, HD:].reshape(HEADS * MLA_ROPE, Q_LORA)], 0)

    z64 = jnp.zeros((1, 64), F32)
    qn = w['mla_q_norm']
    g_even = [w['mla_q_lat_norm'], w['mla_kv_lat_norm'], jnp.concatenate([w['mla_k_rope_norm'], z64], 1),
              w['gqa_q_norm'], w['gqa_k_norm']]
    g_mla = [qn[:, :HD], jnp.concatenate([qn[:, HD:], qn[:, HD:]], 1), w['mla_k_nope_norm']]
    g_swa = [jnp.concatenate([w['swa_q_norm']] * 2, 1), jnp.concatenate([w['swa_k_norm']] * 2, 1)]

    def loss_head(acc, res, t):
        e = acc + res - t
        return e * (1.0 / D), _csum(_rsum(e * e))

    def mlp_fwd(l, xin, last=False):
        hn = _rmsnorm(f"mlp{l}_norm", xin, w['mlp_norm'][l:l + 1])
        a, rl = _mm(f"mlp{l}_up", hn, wup_t[l], "nt", (BF16, BF16), epilogue=_relu2)
        if last:
            xout = _mm(f"mlp{l}_down", a, wdn[l], "nn", (F32,), epilogue=loss_head, extras=(xin, tgt),
                       sums=[_sds((1, 1), F32)])
        else:
            xout = _mm(f"mlp{l}_down", a, wdn[l], "nn", (F32,), epilogue=_add, extras=(xin,))
        return xout, (hn, a, rl)

    proj = _mm("even_in", h0, win_t, "nt", (F32,), tm=512, tn=P_END)
    cqn, ckvn, krd, qg, kg, vg = _even_prep("even_prep", proj, tab_mla + tab_ax, g_even, after=[tok])
    qa = _mm("mla_uq", cqn, wuq_t, "nt", (F32,))
    kv = _mm("mla_ukv", ckvn, wukv_t, "nt", (F32,))
    q_a, k_a, v_a = _mla_prep("mla_prep", qa, kv, krd, tab_mla, g_mla)
    o_a, lse_a = _flash_fwd("mla_attn", q_a, k_a, v_a)
    o_g, lse_g = _flash_fwd("gqa_attn", qg, kg, vg)
    merged = jnp.concatenate([o_a, o_g], 1)
    rest = _exchange_wait("gather_rest_wait", later, merged)
    wout_e, wqkv_t, wout_o, wup0_t, wup1_t, wdn0, wdn1 = [flat(a) for a in rest[:7]]
    odd_norm = rest[7].reshape(1, D)
    wup_t, wdn = (wup0_t, wup1_t), (wdn0, wdn1)
    x1 = _mm("even_out", merged, wout_e, "nn", (F32,), epilogue=_add, extras=(x,))
    x2, mlp0 = mlp_fwd(0, x1)

    h1 = _rmsnorm("odd_norm", x2, odd_norm)
    qkv = _mm("odd_qkv", h1, wqkv_t, "nt", (F32,))
    q_s, k_s, v_s = _swa_prep("swa_prep", qkv, tab_swa, g_swa)
    o_s, lse_s = _swa_fwd("swa_attn", q_s, k_s, v_s, w['swa_sink'])
    x3 = _mm("odd_out", o_s, wout_o, "nn", (F32,), epilogue=_add, extras=(x2,))
    (dy, loss_acc), mlp1 = mlp_fwd(1, x3, last=True)
    loss = lax.psum(0.5 / D * loss_acc[0, 0], ("x", "y", "c"))

    gsm = {}

    def mlp_bwd(l, dout, xin, saved):
        hn, a, rl = saved
        du = _mm(f"mlp{l}_dact", dout, wdn[l], "nt", (BF16,), epilogue=_mul2, extras=(rl,))
        g_dn = _mm(f"mlp{l}_gdown", a, dout, "tn", (BF16,))
        g_up = _mm(f"mlp{l}_gup", du, hn, "tn", (BF16,))
        dhn = _mm(f"mlp{l}_dnorm", du, wup_t[l], "nn", (F32,))
        din, g_n = _rmsnorm_bwd(f"mlp{l}_norm_bwd", dout, xin, dhn, w['mlp_norm'][l:l + 1])
        return din, g_dn, g_up, g_n

    dx3, g_dn1, g_up1, g_mn1 = mlp_bwd(1, dy, x3, mlp1)
    split = lambda a: a.reshape((N_DEV, a.shape[0] // N_DEV) + a.shape[1:])
    sc1, tok1 = _exchange_start("scatter_mlp1_start", [(split(g_up1), True), (split(g_dn1), True)])

    g_wout_o = _mm("odd_gout", o_s, dx3, "tn", (BF16,))
    do_s = _mm("odd_dattn", dx3, wout_o, "nt", (BF16,))
    dl_s = _delta_t("swa_delta", do_s, o_s, after=[tok1])
    dq_s, dsink = _swa_bwd_q("swa_bwd_q", q_s, k_s, v_s, do_s, lse_s, dl_s, w['swa_sink'])
    dk_s, dv_s = _swa_bwd_kv("swa_bwd_kv", q_s, k_s, v_s, do_s, lse_s, dl_s)
    dqkv, g_sq, g_sk = _swa_prep_bwd("swa_prep_bwd", dq_s, dk_s, dv_s, qkv, tab_swa, g_swa)
    g_wqkv = _mm("odd_gqkv", dqkv, h1, "tn", (BF16,))
    dh1 = _mm("odd_dnorm", dqkv, wqkv_t, "nn", (F32,), tk=dqkv.shape[1])
    dx2, g_on = _rmsnorm_bwd("odd_norm_bwd", dx3, x2, dh1, odd_norm)
    gsm['swa_q_norm'] = g_sq[:, :64] + g_sq[:, 64:]
    gsm['swa_k_norm'] = g_sk[:, :64] + g_sk[:, 64:]
    gsm['swa_sink'] = dsink[:, 0].reshape(1, SWA_HEADS)

    dx1, g_dn0, g_up0, g_mn0 = mlp_bwd(0, dx2, x1, mlp0)
    sc2, tok2 = _exchange_start("scatter_mid_start", [(split(g_wout_o), True), (split(g_wqkv), True),
                                                      (split(g_up0), True), (split(g_dn0), True)])
    gsm['mlp_norm'] = jnp.concatenate([g_mn0, g_mn1], 0)

    g_wout_e = _mm("even_gout", merged, dx1, "tn", (BF16,))
    dmerged = _mm("even_dattn", dx1, wout_e, "nt", (BF16,))
    dl_e = _delta("even_delta", dmerged, merged, after=[tok2])
    dq_a, dk_a, dv_a = _flash_bwd("mla_attn_bwd", q_a, k_a, v_a, dmerged, 0, lse_a, dl_e)
    dq_g, dk_g, dv_g = _flash_bwd("gqa_attn_bwd", qg, kg, vg, dmerged, HEADS, lse_g, dl_e)
    dqa, dkv, dkrd, g_qnn, g_qnr, g_kn = _mla_prep_bwd("mla_prep_bwd", dq_a, dk_a, dv_a, qa, kv, tab_mla, g_mla)
    g_wuq = _mm("mla_guq", dqa, cqn, "tn", (BF16,))
    dcqn = _mm("mla_dq_lat", dqa, wuq_t, "nn", (F32,))
    g_wukv = _mm("mla_gukv", dkv, ckvn, "tn", (BF16,))
    dckvn = _mm("mla_dkv_lat", dkv, wukv_t, "nn", (F32,))
    dproj, g_ql, g_kvl, g_kr, g_gq, g_gk = _even_prep_bwd("even_prep_bwd", dcqn, dckvn, dkrd, dq_g, dk_g, dv_g,
                                                          proj, tab_mla + tab_ax, g_even)
    g_win = _mm("even_gin", dproj, h0, "tn", (BF16,), tm=P_END, tk=1024)
    dh0 = _mm("even_dnorm", dproj, win_t, "nn", (F32,), tk=P_END)
    grad_x, g_en = _rmsnorm_bwd("even_norm_bwd", dx1, x, dh0, w['even_norm'])
    gsm.update(even_norm=g_en, mla_q_lat_norm=g_ql, mla_kv_lat_norm=g_kvl,
               mla_q_norm=jnp.concatenate([g_qnn, g_qnr[:, :64] + g_qnr[:, 64:]], 1), mla_k_nope_norm=g_kn,
               mla_k_rope_norm=g_kr[:, :64], gqa_q_norm=g_gq, gqa_k_norm=g_gk)

    g_win = jnp.concatenate([g_win[:KR_END], g_win[P_QG:]], 0)
    g_wuq = jnp.concatenate([g_wuq[:HEADS * HD].reshape(HEADS, HD, Q_LORA),
                             g_wuq[HEADS * HD:].reshape(HEADS, MLA_ROPE, Q_LORA)], 1)
    small_sizes = [w[n].size for n in _SMALL] + [D]
    small_vec = jnp.concatenate([gsm[n].reshape(1, -1) for n in _SMALL] + [g_on], 1)
    pad = (-small_vec.shape[1]) % LANES
    small_vec = jnp.pad(small_vec, ((0, 0), (0, pad)))
    last_st, tok3 = _exchange_start("scatter_last_start", [(split(g_win), True), (g_wuq, True), (split(g_wukv), True),
                                                           (split(g_wout_e), True), (small_vec, False)])
    p_up1, p_dn1 = _exchange_wait("scatter_mlp1_wait", sc1, tok3)
    p_wout_o, p_wqkv, p_up0, p_dn0 = _exchange_wait("scatter_mid_wait", sc2, tok3)
    parts = dict(odd_w_qkv=p_wqkv, odd_w_out=p_wout_o, up0=p_up0, up1=p_up1, dn0=p_dn0, dn1=p_dn1)
    red = {n: _sum8("sum_" + n, p) for n, p in parts.items()}
    grads = {
        'odd_w_qkv': red['odd_w_qkv'].T[None], 'odd_w_out': red['odd_w_out'][None],
        'mlp_w_up': jnp.stack([red['up0'].T, red['up1'].T]), 'mlp_w_down': jnp.stack([red['dn0'], red['dn1']]),
    }
    delta, new_m, new_v = {}, {}, {}

    def apply(names):
        for n in names:
            shp = w[n].shape
            two = lambda a: a.reshape(shp[0] * shp[1], shp[2])
            d_, m_, v_ = _adamw("adamw_" + n, two(w[n]), two(grads[n]), two(m[n]), two(v[n]))
            delta[n], new_m[n], new_v[n] = d_.reshape(shp), m_.reshape(shp), v_.reshape(shp)

    apply(['odd_w_qkv', 'odd_w_out', 'mlp_w_up', 'mlp_w_down'])
    last = _exchange_wait("scatter_last_wait", last_st, new_v['mlp_w_down'])
    for n, p in zip(['even_w_in', 'mla_w_uq', 'mla_w_ukv', 'even_w_out'], last[:4]):
        red[n] = _sum8("sum_" + n, p)
    small_g = _sum8("sum_small", last[4])
    grads.update({'even_w_in': red['even_w_in'].T[None], 'mla_w_uq': red['mla_w_uq'].T[None],
                  'mla_w_ukv': red['mla_w_ukv'].T[None], 'even_w_out': red['even_w_out'][None]})
    apply(['even_w_in', 'mla_w_uq', 'mla_w_ukv', 'even_w_out'])
    off = 0
    for n, sz in zip(_SMALL + ['odd_norm_full'], small_sizes):
        seg = small_g[:, off:off + sz]
        off += sz
        if n == 'odd_norm_full':
            me = 4 * lax.axis_index("x") + 2 * lax.axis_index("y") + lax.axis_index("c")
            grads['odd_norm'] = lax.dynamic_slice(seg, (0, me * (D // N_DEV)), (1, D // N_DEV))
        else:
            grads[n] = seg.reshape(w[n].shape)

    sm_names = _SMALL + ['odd_norm']
    pack = lambda d: jnp.concatenate([d[n].reshape(1, -1) for n in sm_names], 1)
    pw, pg, pm, pv = pack(w), pack(grads), pack(m), pack(v)
    padw = (-pw.shape[1]) % LANES
    padf = lambda a: jnp.pad(a, ((0, 0), (0, padw)))
    d_, m_, v_ = _adamw("adamw_small", padf(pw), padf(pg), padf(pm), jnp.pad(pv, ((0, 0), (0, padw)), constant_values=1.0))
    off = 0
    for n in sm_names:
        sz = w[n].size
        delta[n] = d_[:, off:off + sz].reshape(w[n].shape)
        new_m[n] = m_[:, off:off + sz].reshape(w[n].shape)
        new_v[n] = v_[:, off:off + sz].reshape(w[n].shape)
        off += sz

    return (loss, grad_x[None], *[grads[n] for n in _WEIGHTS], *[delta[n] for n in _WEIGHTS],
            *[new_m[n] for n in _WEIGHTS], *[new_v[n] for n in _WEIGHTS])


def kernel(x, even_norm, even_w_in, mla_q_lat_norm, mla_kv_lat_norm, mla_w_uq, mla_w_ukv, mla_q_norm, mla_k_nope_norm, mla_k_rope_norm, gqa_q_norm, gqa_k_norm, even_w_out, odd_norm, odd_w_qkv, swa_q_norm, swa_k_norm, swa_sink, odd_w_out, mlp_norm, mlp_w_up, mlp_w_down, loss_target, m_even_norm, m_even_w_in, m_mla_q_lat_norm, m_mla_kv_lat_norm, m_mla_w_uq, m_mla_w_ukv, m_mla_q_norm, m_mla_k_nope_norm, m_mla_k_rope_norm, m_gqa_q_norm, m_gqa_k_norm, m_even_w_out, m_odd_norm, m_odd_w_qkv, m_swa_q_norm, m_swa_k_norm, m_swa_sink, m_odd_w_out, m_mlp_norm, m_mlp_w_up, m_mlp_w_down, v_even_norm, v_even_w_in, v_mla_q_lat_norm, v_mla_kv_lat_norm, v_mla_w_uq, v_mla_w_ukv, v_mla_q_norm, v_mla_k_nope_norm, v_mla_k_rope_norm, v_gqa_q_norm, v_gqa_k_norm, v_even_w_out, v_odd_norm, v_odd_w_qkv, v_swa_q_norm, v_swa_k_norm, v_swa_sink, v_odd_w_out, v_mlp_norm, v_mlp_w_up, v_mlp_w_down):
    ws = (even_norm, even_w_in, mla_q_lat_norm, mla_kv_lat_norm, mla_w_uq, mla_w_ukv, mla_q_norm, mla_k_nope_norm, mla_k_rope_norm, gqa_q_norm, gqa_k_norm, even_w_out, odd_norm, odd_w_qkv, swa_q_norm, swa_k_norm, swa_sink, odd_w_out, mlp_norm, mlp_w_up, mlp_w_down)
    ms = (m_even_norm, m_even_w_in, m_mla_q_lat_norm, m_mla_kv_lat_norm, m_mla_w_uq, m_mla_w_ukv, m_mla_q_norm, m_mla_k_nope_norm, m_mla_k_rope_norm, m_gqa_q_norm, m_gqa_k_norm, m_even_w_out, m_odd_norm, m_odd_w_qkv, m_swa_q_norm, m_swa_k_norm, m_swa_sink, m_odd_w_out, m_mlp_norm, m_mlp_w_up, m_mlp_w_down)
    vs = (v_even_norm, v_even_w_in, v_mla_q_lat_norm, v_mla_kv_lat_norm, v_mla_w_uq, v_mla_w_ukv, v_mla_q_norm, v_mla_k_nope_norm, v_mla_k_rope_norm, v_gqa_q_norm, v_gqa_k_norm, v_even_w_out, v_odd_norm, v_odd_w_qkv, v_swa_q_norm, v_swa_k_norm, v_swa_sink, v_odd_w_out, v_mlp_norm, v_mlp_w_up, v_mlp_w_down)
    return _step(x[0], loss_target[0], dict(zip(_WEIGHTS, ws)), dict(zip(_WEIGHTS, ms)), dict(zip(_WEIGHTS, vs)))
```

```python
import functools

import jax
import jax.numpy as jnp
from jax import lax
from jax.experimental import pallas as pl
from jax.experimental.pallas import tpu as pltpu

F32 = jnp.float32
BF16 = jnp.bfloat16

N_DEV = 8
NORM_EPS = 1e-6
ROPE_THETA = 500000.0
AXIAL_THETA = 10000.0
GRID_W = 64
HEADS = 8
GQA_KV = 2
HD = 128
MLA_ROPE = 64
MLA_QK = HD + MLA_ROPE
Q_LORA = 512
KV_LORA = 256
SWA_HEADS = 32
SWA_KV = 4
SWA_D = 64
SWA_ROT = 16
SWA_WINDOW = 128
SWA_BLOCK = 128
MLA_SCALE, GQA_SCALE, SWA_SCALE = MLA_QK ** -0.5, HD ** -0.5, SWA_D ** -0.5
LANES = 128
ADAM_LR, ADAM_B1, ADAM_B2, ADAM_EPS, ADAM_WD, ADAM_STEP = 0.001, 0.9, 0.999, 1e-08, 0.01, 10
VMEM_LIMIT = 56 * 1024 * 1024
MM_TILE = (1024, 1024, 2048)
FWD_TILE = (1024, 4096)
BWD_TILE = (1024, 1024)
ROW_TILE = 256

P_CQ, P_CKV, P_KR, P_QG = 0, Q_LORA, Q_LORA + KV_LORA, Q_LORA + KV_LORA + LANES
P_KG = P_QG + HEADS * HD
P_VG = P_KG + GQA_KV * HD
P_END = P_VG + GQA_KV * HD
KR_END = Q_LORA + KV_LORA + MLA_ROPE


def _tile(n, prefs):
    for t in prefs:
        if n % t == 0 and t <= n:
            return t
    return n


def _params(sem):
    return pltpu.CompilerParams(dimension_semantics=sem, vmem_limit_bytes=VMEM_LIMIT)


_DIMS = {"nn": ((1,), (0,)), "nt": ((1,), (1,)), "tn": ((0,), (0,))}


def _mm(name, a, b, mode, out_dtypes, epilogue=None, extras=(), sums=(), tm=MM_TILE[0], tn=MM_TILE[1], tk=MM_TILE[2]):
    if mode == "nn":
        (M, K), (_, N) = a.shape, b.shape
    elif mode == "nt":
        (M, K), (N, _) = a.shape, b.shape
    else:
        (K, M), (_, N) = a.shape, b.shape
    tm = _tile(M, (tm, 512, 256, 128))
    tn = _tile(N, (tn, 512, 256, 128))
    tk = _tile(K, (tk, 1024, 512, 256, 128))
    nk = K // tk
    ne, no, ns = len(extras), len(out_dtypes), len(sums)
    if mode == "tn":
        a_spec = pl.BlockSpec((tk, tm), lambda i, j, k: (k, i))
    else:
        a_spec = pl.BlockSpec((tm, tk), lambda i, j, k: (i, k))
    if mode == "nt":
        b_spec = pl.BlockSpec((tn, tk), lambda i, j, k: (j, k))
    else:
        b_spec = pl.BlockSpec((tk, tn), lambda i, j, k: (k, j))
    o_spec = pl.BlockSpec((tm, tn), lambda i, j, k: (i, j))
    dims = (_DIMS[mode], ((), ()))

    def body(a_ref, b_ref, *rest):
        ex, outs = rest[:ne], rest[ne:ne + no]
        k = pl.program_id(2)
        part = lax.dot_general(a_ref[...].astype(BF16), b_ref[...].astype(BF16), dims, preferred_element_type=F32)

        def finish(total):
            res = epilogue(total, *[e[...] for e in ex]) if epilogue else (total,)
            for o, r in zip(outs, res[:no]):
                o[...] = r.astype(o.dtype)
            if ns:
                first = (pl.program_id(0) == 0) & (pl.program_id(1) == 0)

                @pl.when(first)
                def _():
                    for s_ref, r in zip(rest[ne + no:ne + no + ns], res[no:]):
                        s_ref[...] = r

                @pl.when(jnp.logical_not(first))
                def _():
                    for s_ref, r in zip(rest[ne + no:ne + no + ns], res[no:]):
                        s_ref[...] += r

        if nk == 1:
            finish(part)
            return
        acc = rest[ne + no + ns]

        @pl.when(k == 0)
        def _():
            acc[...] = part

        @pl.when((k > 0) & (k < nk - 1))
        def _():
            acc[...] += part

        @pl.when(k == nk - 1)
        def _():
            finish(acc[...] + part)

    out = pl.pallas_call(
        body, name=name, grid=(M // tm, N // tn, nk),
        in_specs=[a_spec, b_spec] + [o_spec] * ne,
        out_specs=[o_spec] * no + [pl.BlockSpec(tuple(s.shape), lambda i, j, k: (0,) * len(s.shape)) for s in sums],
        out_shape=[jax.ShapeDtypeStruct((M, N), d) for d in out_dtypes] + list(sums),
        scratch_shapes=[pltpu.VMEM((tm, tn), F32)] if nk > 1 else [],
        compiler_params=_params(("arbitrary",) * 3 if ns else ("parallel", "parallel", "arbitrary")),
    )(a, b, *extras)
    return out[0] if no + ns == 1 else out


def _rowwise(name, fn, rows, consts, outs, accs=(), tm=ROW_TILE, after=()):
    S = rows[0].shape[-2]
    tm = _tile(S, (tm, 128, 64, 32, 16, 8))
    nr, nc, nd, no, na = len(rows), len(consts), len(after), len(outs), len(accs)

    def rspec(shape):
        if len(shape) == 2:
            return pl.BlockSpec((tm, shape[1]), lambda i: (i, 0))
        return pl.BlockSpec((shape[0], tm, shape[2]), lambda i: (0, i, 0))

    def cspec(shape):
        return pl.BlockSpec(tuple(shape), lambda i: (0,) * len(shape))

    def ospec(o):
        if isinstance(o, tuple):
            return pl.BlockSpec((o[1].shape[0], o[1].shape[1], tm), lambda i: (0, 0, i))
        return rspec(o.shape)

    out_sds = [o[1] if isinstance(o, tuple) else o for o in outs]

    def body(*refs):
        r, c = refs[:nr], refs[nr:nr + nc]
        o, a = refs[nr + nc + nd:nr + nc + nd + no], refs[nr + nc + nd + no:]
        vals = fn(r, c, o)
        if na:
            @pl.when(pl.program_id(0) == 0)
            def _():
                for ar in a:
                    ar[...] = jnp.zeros_like(ar)

            for ar, v in zip(a, vals):
                ar[...] += v

    res = pl.pallas_call(
        body, name=name, grid=(S // tm,),
        in_specs=[rspec(x.shape) for x in rows] + [cspec(x.shape) for x in consts]
        + [pl.BlockSpec(memory_space=pl.ANY)] * len(after),
        out_specs=[ospec(x) for x in outs] + [cspec(x.shape) for x in accs],
        out_shape=out_sds + list(accs),
        compiler_params=_params(("arbitrary",) if na else ("parallel",)),
    )(*rows, *consts, *after)
    return res


def _sds(shape, dtype):
    return jax.ShapeDtypeStruct(tuple(shape), dtype)


def _rsum(x):
    return jnp.sum(x, axis=-1, keepdims=True)


def _csum(x):
    return jnp.sum(x, axis=0, keepdims=True)


def _lane(shape):
    return lax.broadcasted_iota(jnp.int32, shape, 1)


def _partner(x, seg, half):
    lane = _lane(x.shape) % seg
    return jnp.where(lane < half, pltpu.roll(x, LANES - half, 1), pltpu.roll(x, half, 1))


def _rope(x, c, s, seg, half):
    return x * c + _partner(x, seg, half) * s


def _rope_bwd(dy, c, s, seg, half):
    t = _partner(dy * s, seg, half)
    if seg != 2 * half:
        t = jnp.where(_lane(dy.shape) % seg < 2 * half, t, 0.0)
    return dy * c + t


def _rms_r(ss, n):
    return lax.rsqrt(ss * (1.0 / n) + NORM_EPS)


def _rms_bwd(x, r, g, dy, dot_scale):
    dyg = dy * g
    return r * dyg - x * (r * r * r) * (_rsum(dyg * x) * dot_scale)


def _rmsnorm(name, x, g, after=()):
    D = x.shape[1]

    def fn(r, c, o):
        xv = r[0][...]
        o[0][...] = (xv * _rms_r(_rsum(xv * xv), D) * c[0][...]).astype(BF16)
        return ()

    return _rowwise(name, fn, [x], [g], [_sds(x.shape, BF16)], after=after)[0]


def _rmsnorm_bwd(name, dres, x, dh, g):
    D = x.shape[1]

    def fn(r, c, o):
        xv, dhv, gv = r[1][...], r[2][...], c[0][...]
        rr = _rms_r(_rsum(xv * xv), D)
        o[0][...] = r[0][...] + _rms_bwd(xv, rr, gv, dhv, 1.0 / D)
        return (_csum(dhv * xv * rr),)

    return _rowwise(name, fn, [dres, x, dh], [g], [_sds(x.shape, F32)], [_sds((1, D), F32)])


def _even_prep(name, proj, tabs, gains, after=()):
    S = proj.shape[0]
    cm, sm, ca, sa = tabs

    def fn(r, c, o):
        p, cmv, smv, cav, sav = r[0], r[1][...], r[2][...], r[3][...], r[4][...]
        g_ql, g_kvl, g_kr, g_q, g_k = (x[...] for x in c)
        cq = p[:, P_CQ:P_CKV]
        o[0][...] = (cq * _rms_r(_rsum(cq * cq), Q_LORA) * g_ql).astype(BF16)
        ckv = p[:, P_CKV:P_KR]
        o[1][...] = (ckv * _rms_r(_rsum(ckv * ckv), KV_LORA) * g_kvl).astype(BF16)
        kr = p[:, P_KR:P_QG]
        y = _rope(kr * _rms_r(_rsum(kr * kr), MLA_ROPE) * g_kr, cmv, smv, 64, 32)
        o[2][...] = (y + pltpu.roll(y, 64, 1)).astype(BF16)
        for h in range(HEADS):
            xh = p[:, P_QG + HD * h:P_QG + HD * (h + 1)]
            o[3][h] = (_rope(xh * _rms_r(_rsum(xh * xh), HD) * g_q, cav, sav, 64, 32) * GQA_SCALE).astype(BF16)
        for h in range(GQA_KV):
            xh = p[:, P_KG + HD * h:P_KG + HD * (h + 1)]
            o[4][h] = _rope(xh * _rms_r(_rsum(xh * xh), HD) * g_k, cav, sav, 64, 32).astype(BF16)
            o[5][h] = p[:, P_VG + HD * h:P_VG + HD * (h + 1)].astype(BF16)
        return ()

    outs = [_sds((S, Q_LORA), BF16), _sds((S, KV_LORA), BF16), _sds((S, LANES), BF16),
            _sds((HEADS, S, HD), BF16), _sds((GQA_KV, S, HD), BF16), _sds((GQA_KV, S, HD), BF16)]
    return _rowwise(name, fn, [proj, cm, sm, ca, sa], gains, outs, after=after)


def _even_prep_bwd(name, dcqn, dckvn, dkrd, dqg, dkg, dvg, proj, tabs, gains, after=()):
    S = proj.shape[0]
    cm, sm, ca, sa = tabs

    def fn(r, c, o):
        p, cmv, smv, cav, sav = r[6], r[7][...], r[8][...], r[9][...], r[10][...]
        g_ql, g_kvl, g_kr, g_q, g_k = (x[...] for x in c)
        out = o[0]
        cq = p[:, P_CQ:P_CKV]
        rr = _rms_r(_rsum(cq * cq), Q_LORA)
        d = r[0][...]
        out[:, P_CQ:P_CKV] = _rms_bwd(cq, rr, g_ql, d, 1.0 / Q_LORA).astype(BF16)
        a_ql = _csum(d * cq * rr)
        ckv = p[:, P_CKV:P_KR]
        rr = _rms_r(_rsum(ckv * ckv), KV_LORA)
        d = r[1][...]
        out[:, P_CKV:P_KR] = _rms_bwd(ckv, rr, g_kvl, d, 1.0 / KV_LORA).astype(BF16)
        a_kvl = _csum(d * ckv * rr)
        kr = p[:, P_KR:P_QG]
        rr = _rms_r(_rsum(kr * kr), MLA_ROPE)
        d = r[2][...]
        d = d + pltpu.roll(d, 64, 1)
        d = _rope_bwd(d, cmv, smv, 64, 32)
        low = _lane(d.shape) < 64
        out[:, P_KR:P_QG] = jnp.where(low, _rms_bwd(kr, rr, g_kr, d, 1.0 / MLA_ROPE), 0.0).astype(BF16)
        a_kr = _csum(d * kr * rr)
        a_q = jnp.zeros((1, HD), F32)
        for h in range(HEADS):
            xh = p[:, P_QG + HD * h:P_QG + HD * (h + 1)]
            rr = _rms_r(_rsum(xh * xh), HD)
            d = _rope_bwd(r[3][h] * GQA_SCALE, cav, sav, 64, 32)
            out[:, P_QG + HD * h:P_QG + HD * (h + 1)] = _rms_bwd(xh, rr, g_q, d, 1.0 / HD).astype(BF16)
            a_q = a_q + _csum(d * xh * rr)
        a_k = jnp.zeros((1, HD), F32)
        grp = HEADS // GQA_KV
        for h in range(GQA_KV):
            xh = p[:, P_KG + HD * h:P_KG + HD * (h + 1)]
            rr = _rms_r(_rsum(xh * xh), HD)
            d = r[4][grp * h]
            dv = r[5][grp * h]
            for e in range(1, grp):
                d = d + r[4][grp * h + e]
                dv = dv + r[5][grp * h + e]
            d = _rope_bwd(d, cav, sav, 64, 32)
            out[:, P_KG + HD * h:P_KG + HD * (h + 1)] = _rms_bwd(xh, rr, g_k, d, 1.0 / HD).astype(BF16)
            out[:, P_VG + HD * h:P_VG + HD * (h + 1)] = dv.astype(BF16)
            a_k = a_k + _csum(d * xh * rr)
        return (a_ql, a_kvl, a_kr, a_q, a_k)

    accs = [_sds((1, Q_LORA), F32), _sds((1, KV_LORA), F32), _sds((1, LANES), F32),
            _sds((1, HD), F32), _sds((1, HD), F32)]
    return _rowwise(name, fn, [dcqn, dckvn, dkrd, dqg, dkg, dvg, proj, cm, sm, ca, sa], gains,
                    [_sds((S, P_END), BF16)], accs, tm=128, after=after)


def _mla_q_parts(qa, h, rp, g_n, g_r):
    nope = qa[:, HD * h:HD * (h + 1)]
    mine = (_lane(rp.shape) >= 64) == bool(h % 2)
    rpm = jnp.where(mine, rp, 0.0)
    rr = _rms_r(_rsum(nope * nope) + _rsum(rpm * rpm), MLA_QK)
    return nope, rpm, mine, rr


def _mla_prep(name, qa, kv, krd, tabs, gains):
    S = qa.shape[0]
    cm, sm = tabs

    def fn(r, c, o):
        qa_r, kv_r, krd_v, cmv, smv = r[0], r[1], r[2][...], r[3][...], r[4][...]
        g_n, g_r, g_kn = (x[...] for x in c)
        for h in range(HEADS):
            rp = qa_r[:, HEADS * HD + LANES * (h // 2):HEADS * HD + LANES * (h // 2 + 1)]
            nope, rpm, mine, rr = _mla_q_parts(qa_r, h, rp, g_n, g_r)
            o[0][h, :, 0:HD] = (nope * (rr * MLA_SCALE) * g_n).astype(BF16)
            o[0][h, :, HD:2 * HD] = _rope(rpm * (rr * MLA_SCALE) * g_r, cmv, smv, 64, 32).astype(BF16)
            kn = kv_r[:, 2 * HD * h:2 * HD * h + HD]
            o[1][h, :, 0:HD] = (kn * _rms_r(_rsum(kn * kn), HD) * g_kn).astype(BF16)
            o[1][h, :, HD:2 * HD] = krd_v
            o[2][h] = kv_r[:, 2 * HD * h + HD:2 * HD * (h + 1)].astype(BF16)
        return ()

    outs = [_sds((HEADS, S, 2 * HD), BF16), _sds((HEADS, S, 2 * HD), BF16), _sds((HEADS, S, HD), BF16)]
    return _rowwise(name, fn, [qa, kv, krd, cm, sm], gains, outs)


def _mla_prep_bwd(name, dq, dk, dv, qa, kv, tabs, gains):
    S = qa.shape[0]
    cm, sm = tabs

    def fn(r, c, o):
        dq_r, dk_r, dv_r, qa_r, kv_r, cmv, smv = r[0], r[1], r[2], r[3], r[4], r[5][...], r[6][...]
        g_n, g_r, g_kn = (x[...] for x in c)
        a_n = jnp.zeros((1, HD), F32)
        a_r = jnp.zeros((1, LANES), F32)
        a_kn = jnp.zeros((1, HD), F32)
        dkrd = jnp.zeros(cmv.shape, F32)
        drp = None
        for h in range(HEADS):
            rp = qa_r[:, HEADS * HD + LANES * (h // 2):HEADS * HD + LANES * (h // 2 + 1)]
            nope, rpm, mine, rr = _mla_q_parts(qa_r, h, rp, g_n, g_r)
            dn = dq_r[h, :, 0:HD] * MLA_SCALE
            dr = _rope_bwd(jnp.where(mine, dq_r[h, :, HD:2 * HD] * MLA_SCALE, 0.0), cmv, smv, 64, 32)
            dot = (_rsum(dn * g_n * nope) + _rsum(dr * g_r * rpm)) * (1.0 / MLA_QK)
            r3 = rr * rr * rr
            o[0][:, HD * h:HD * (h + 1)] = (rr * dn * g_n - nope * r3 * dot).astype(BF16)
            part = jnp.where(mine, rr * dr * g_r - rpm * r3 * dot, 0.0)
            drp = part if h % 2 == 0 else drp + part
            if h % 2 == 1:
                o[0][:, HEADS * HD + LANES * (h // 2):HEADS * HD + LANES * (h // 2 + 1)] = drp.astype(BF16)
            a_n = a_n + _csum(dn * nope * rr)
            a_r = a_r + _csum(dr * rpm * rr)
            kn = kv_r[:, 2 * HD * h:2 * HD * h + HD]
            rk = _rms_r(_rsum(kn * kn), HD)
            dkn = dk_r[h, :, 0:HD]
            o[1][:, 2 * HD * h:2 * HD * h + HD] = _rms_bwd(kn, rk, g_kn, dkn, 1.0 / HD).astype(BF16)
            o[1][:, 2 * HD * h + HD:2 * HD * (h + 1)] = dv_r[h].astype(BF16)
            a_kn = a_kn + _csum(dkn * kn * rk)
            dkrd = dkrd + dk_r[h, :, HD:2 * HD]
        o[2][...] = dkrd
        return (a_n, a_r, a_kn)

    outs = [_sds(qa.shape, BF16), _sds(kv.shape, BF16), _sds((S, LANES), F32)]
    accs = [_sds((1, HD), F32), _sds((1, LANES), F32), _sds((1, HD), F32)]
    return _rowwise(name, fn, [dq, dk, dv, qa, kv, cm, sm], gains, outs, accs, tm=128)


def _seg_ones(seg):
    r = lax.broadcasted_iota(jnp.int32, (LANES, LANES), 0) // seg
    c = lax.broadcasted_iota(jnp.int32, (LANES, LANES), 1) // seg
    return (r == c).astype(F32)


def _dot01(x, sel, dims):
    hi = x.astype(BF16)
    lo = (x - hi.astype(F32)).astype(BF16)
    sb = sel.astype(BF16)
    if dims is _NN:
        return (lax.dot_general(hi, sb, dims, preferred_element_type=F32)
                + lax.dot_general(lo, sb, dims, preferred_element_type=F32))
    return (lax.dot_general(sb, hi, dims, preferred_element_type=F32)
            + lax.dot_general(sb, lo, dims, preferred_element_type=F32))


def _seg_sum(x, ones):
    return _dot01(x, ones, _NN)


def _seg64_r(x, ones):
    return _rms_r(_seg_sum(x * x, ones), SWA_D)


def _swa_prep(name, qkv, tabs, gains):
    S = qkv.shape[0]
    nq, nk = SWA_HEADS * SWA_D, SWA_KV * SWA_D
    cs, ss = tabs

    def fn(r, c, o):
        x_r, csv, ssv = r[0], r[1][...], r[2][...]
        g_q, g_k = c[0][...], c[1][...]
        ones = _seg_ones(SWA_D)
        for g in range((nq + nk) // LANES):
            x = x_r[:, LANES * g:LANES * (g + 1)]
            rr = _seg64_r(x, ones)
            y = _rope(x * rr * (g_q if g < nq // LANES else g_k), csv, ssv, SWA_D, SWA_ROT // 2)
            if g < nq // LANES:
                o[0][:, LANES * g:LANES * (g + 1)] = (y * SWA_SCALE).astype(BF16)
            else:
                o[1][:, LANES * g - nq:LANES * (g + 1) - nq] = y.astype(BF16)
        o[2][...] = x_r[:, nq + nk:nq + 2 * nk].astype(BF16)
        return ()

    outs = [_sds((S, nq), BF16), _sds((S, nk), BF16), _sds((S, nk), BF16)]
    return _rowwise(name, fn, [qkv, cs, ss], gains, outs)


def _swa_prep_bwd(name, dq, dk, dv, qkv, tabs, gains):
    nq, nk = SWA_HEADS * SWA_D, SWA_KV * SWA_D
    cs, ss = tabs

    def fn(r, c, o):
        dq_r, dk_r, x_r, csv, ssv = r[0], r[1], r[3], r[4][...], r[5][...]
        g_q, g_k = c[0][...], c[1][...]
        acc = [jnp.zeros((1, LANES), F32), jnp.zeros((1, LANES), F32)]
        ones = _seg_ones(SWA_D)
        for g in range((nq + nk) // LANES):
            isq = g < nq // LANES
            x = x_r[:, LANES * g:LANES * (g + 1)]
            rr = _seg64_r(x, ones)
            d = dq_r[:, LANES * g:LANES * (g + 1)] * SWA_SCALE if isq else dk_r[:, LANES * g - nq:LANES * (g + 1) - nq]
            d = _rope_bwd(d, csv, ssv, SWA_D, SWA_ROT // 2)
            dyg = d * (g_q if isq else g_k)
            dot = _seg_sum(dyg * x, ones) * (1.0 / SWA_D)
            o[0][:, LANES * g:LANES * (g + 1)] = (rr * dyg - x * (rr * rr * rr) * dot).astype(BF16)
            acc[0 if isq else 1] = acc[0 if isq else 1] + _csum(d * x * rr)
        o[0][:, nq + nk:nq + 2 * nk] = r[2][...].astype(BF16)
        return tuple(acc)

    return _rowwise(name, fn, [dq, dk, dv, qkv, cs, ss], gains, [_sds(qkv.shape, BF16)],
                    [_sds((1, LANES), F32), _sds((1, LANES), F32)], tm=128)


def _delta(name, do, o_, after=()):
    S, C = do.shape

    def fn(r, c, o):
        for g in range(C // LANES):
            t = r[0][:, LANES * g:LANES * (g + 1)].astype(F32) * r[1][:, LANES * g:LANES * (g + 1)].astype(F32)
            o[0][g] = _rsum(t)
        return ()

    return _rowwise(name, fn, [do, o_], [], [_sds((C // LANES, S, 1), F32)], after=after)[0]


def _delta_t(name, do, o_, after=()):
    S, C = do.shape

    def fn(r, c, o):
        row = lax.broadcasted_iota(jnp.int32, (8, LANES), 0)
        sel = (row == _lane((8, LANES)) // SWA_D).astype(F32)
        for g in range(C // LANES):
            t = r[0][:, LANES * g:LANES * (g + 1)].astype(F32) * r[1][:, LANES * g:LANES * (g + 1)].astype(F32)
            both = _dot01(t, sel, _NT)
            o[0][2 * g] = both[0:1, :]
            o[0][2 * g + 1] = both[1:2, :]
        return ()

    return _rowwise(name, fn, [do, o_], [], [("T", _sds((2 * C // LANES, 1, S), F32))], after=after)[0]


_NT = (((1,), (1,)), ((), ()))
_NN = (((1,), (0,)), ((), ()))
_TN = (((0,), (0,)), ((), ()))


def _flash_fwd(name, q, k, v, tq=FWD_TILE[0], tk=FWD_TILE[1]):
    H, S, dk = q.shape
    G = H // k.shape[0]
    dv = v.shape[2]
    tq, tk = _tile(S, (tq, 256, 128)), _tile(S, (tk, 256, 128))
    nk = S // tk

    def body(q_ref, k_ref, v_ref, o_ref, lse_ref, m_s, l_s, acc_s):
        j = pl.program_id(2)

        @pl.when(j == 0)
        def _():
            m_s[...] = jnp.full_like(m_s, -jnp.inf)
            l_s[...] = jnp.zeros_like(l_s)
            acc_s[...] = jnp.zeros_like(acc_s)

        s = lax.dot_general(q_ref[...], k_ref[...], _NT, preferred_element_type=F32)
        m_new = jnp.maximum(m_s[...], jnp.max(s, axis=-1, keepdims=True))
        alpha = jnp.exp(m_s[...] - m_new)
        p = jnp.exp(s - m_new)
        l_s[...] = alpha * l_s[...] + _rsum(p)
        acc_s[...] = alpha * acc_s[...] + lax.dot_general(p.astype(BF16), v_ref[...], _NN,
                                                          preferred_element_type=F32)
        m_s[...] = m_new

        @pl.when(j == nk - 1)
        def _():
            o_ref[...] = (acc_s[...] / l_s[...]).astype(o_ref.dtype)
            lse_ref[...] = m_s[...] + jnp.log(l_s[...])

    return pl.pallas_call(
        body, name=name, grid=(H, S // tq, nk),
        in_specs=[pl.BlockSpec((None, tq, dk), lambda h, i, j: (h, i, 0)),
                  pl.BlockSpec((None, tk, dk), lambda h, i, j: (h // G, j, 0)),
                  pl.BlockSpec((None, tk, dv), lambda h, i, j: (h // G, j, 0))],
        out_specs=[pl.BlockSpec((tq, dv), lambda h, i, j: (i, h)),
                   pl.BlockSpec((None, tq, 1), lambda h, i, j: (h, i, 0))],
        out_shape=[_sds((S, H * dv), BF16), _sds((H, S, 1), F32)],
        scratch_shapes=[pltpu.VMEM((tq, 1), F32), pltpu.VMEM((tq, 1), F32), pltpu.VMEM((tq, dv), F32)],
        compiler_params=_params(("parallel", "parallel", "arbitrary")),
    )(q, k, v)


def _flash_bwd(name, q, k, v, do, head0, lse, delta, tq=BWD_TILE[0], tk=BWD_TILE[1]):
    H, S, dk = q.shape
    G = H // k.shape[0]
    dv = v.shape[2]
    tq, tk = _tile(S, (tq, 256, 128)), _tile(S, (tk, 256, 128))
    nq = S // tq

    def body(q_ref, k_ref, v_ref, do_ref, lse_ref, dl_ref, dq_ref, dk_ref, dv_ref, dk_s, dv_s):
        j, i = pl.program_id(1), pl.program_id(2)

        @pl.when(i == 0)
        def _():
            dk_s[...] = jnp.zeros_like(dk_s)
            dv_s[...] = jnp.zeros_like(dv_s)

        qv, kv_, dov = q_ref[...], k_ref[...], do_ref[...]
        s = lax.dot_general(qv, kv_, _NT, preferred_element_type=F32)
        p = jnp.exp(s - lse_ref[...])
        dp = lax.dot_general(dov, v_ref[...], _NT, preferred_element_type=F32)
        ds = (p * (dp - dl_ref[...])).astype(BF16)
        dv_s[...] += lax.dot_general(p.astype(BF16), dov, _TN, preferred_element_type=F32)
        dk_s[...] += lax.dot_general(ds, qv, _TN, preferred_element_type=F32)
        dqi = lax.dot_general(ds, kv_, _NN, preferred_element_type=F32)
        rows = pl.ds(pl.multiple_of(i * tq, tq), tq)

        @pl.when(j == 0)
        def _():
            dq_ref[rows, :] = dqi

        @pl.when(j > 0)
        def _():
            dq_ref[rows, :] += dqi

        @pl.when(i == nq - 1)
        def _():
            dk_ref[...] = dk_s[...]
            dv_ref[...] = dv_s[...]

    return pl.pallas_call(
        body, name=name, grid=(H, S // tk, nq),
        in_specs=[pl.BlockSpec((None, tq, dk), lambda h, j, i: (h, i, 0)),
                  pl.BlockSpec((None, tk, dk), lambda h, j, i: (h // G, j, 0)),
                  pl.BlockSpec((None, tk, dv), lambda h, j, i: (h // G, j, 0)),
                  pl.BlockSpec((tq, dv), lambda h, j, i: (i, head0 + h)),
                  pl.BlockSpec((None, tq, 1), lambda h, j, i: (h, i, 0)),
                  pl.BlockSpec((None, tq, 1), lambda h, j, i: (head0 + h, i, 0))],
        out_specs=[pl.BlockSpec((None, S, dk), lambda h, j, i: (h, 0, 0)),
                   pl.BlockSpec((None, tk, dk), lambda h, j, i: (h, j, 0)),
                   pl.BlockSpec((None, tk, dv), lambda h, j, i: (h, j, 0))],
        out_shape=[_sds((H, S, dk), F32), _sds((H, S, dk), F32), _sds((H, S, dv), F32)],
        scratch_shapes=[pltpu.VMEM((tk, dk), F32), pltpu.VMEM((tk, dv), F32)],
        compiler_params=_params(("parallel", "arbitrary", "arbitrary")),
    )(q, k, v, do, lse, delta)


def _swa_place(ref128, h):
    e, t = h % 2, (h // (SWA_HEADS // SWA_KV)) % 2
    x = ref128.astype(F32)
    if e != t:
        x = pltpu.roll(x, 64, 1)
    return jnp.where((_lane(x.shape) >= 64) == bool(t), x, 0.0).astype(BF16)


def _swa_unplace(y, h):
    e, t = h % 2, (h // (SWA_HEADS // SWA_KV)) % 2
    return pltpu.roll(y, 64, 1) if e != t else y


def _swa_specs(width, nb):
    prev = pl.BlockSpec((SWA_BLOCK, width), lambda i: (jnp.maximum(i - 1, 0), 0))
    cur = pl.BlockSpec((SWA_BLOCK, width), lambda i: (i, 0))
    nxt = pl.BlockSpec((SWA_BLOCK, width), lambda i: (jnp.minimum(i + 1, nb - 1), 0))
    return [prev, cur, nxt]


def _swa_bias_t(i, S):
    shape = (3 * SWA_BLOCK, SWA_BLOCK)
    kpos = (i - 1) * SWA_BLOCK + lax.broadcasted_iota(jnp.int32, shape, 0)
    qpos = i * SWA_BLOCK + lax.broadcasted_iota(jnp.int32, shape, 1)
    ok = (jnp.abs(qpos - kpos) <= SWA_WINDOW) & (kpos >= 0) & (kpos < S)
    return jnp.where(ok, 0.0, -jnp.inf)


def _add_blocks(x, blocks):
    n = x.shape[1] // LANES
    return jnp.concatenate([x[:, LANES * b:LANES * (b + 1)] + blocks[b % len(blocks)] for b in range(n)], axis=1)


def _swa_stack(refs, heads):
    return jnp.concatenate([_swa_place(r[:, LANES * (h // 2):LANES * (h // 2 + 1)], h) for h in heads for r in refs],
                           axis=0)


def _swa_unstack_t(ot, j, out_ref):
    grp = SWA_HEADS // SWA_KV
    for pair in range(grp // 2):
        h = grp * j + 2 * pair
        a = _swa_unplace(ot[:, SWA_BLOCK * 2 * pair:SWA_BLOCK * (2 * pair + 1)].T, h)
        b = _swa_unplace(ot[:, SWA_BLOCK * (2 * pair + 1):SWA_BLOCK * (2 * pair + 2)].T, h + 1)
        out_ref[:, LANES * (h // 2):LANES * (h // 2 + 1)] = jnp.where(_lane(a.shape) < 64, a, b).astype(out_ref.dtype)


def _swa_fwd(name, q, k, v, sink):
    S = q.shape[0]
    nb = S // SWA_BLOCK
    grp = SWA_HEADS // SWA_KV
    smem = pl.BlockSpec(memory_space=pltpu.SMEM)

    def body(sink_ref, q_ref, kp, kc, kn, vp, vc, vn, o_ref, lse_ref):
        i = pl.program_id(0)
        bias = [_swa_bias_t(i, S)]
        kcat = [jnp.concatenate([r[:, LANES * u:LANES * (u + 1)] for r in (kp, kc, kn)], axis=0) for u in range(2)]
        vcat = [jnp.concatenate([r[:, LANES * u:LANES * (u + 1)] for r in (vp, vc, vn)], axis=0) for u in range(2)]
        for j in range(SWA_KV):
            heads = range(grp * j, grp * (j + 1))
            xq = _swa_stack([q_ref], heads)
            sk = jnp.concatenate([jnp.full((1, SWA_BLOCK), sink_ref[0, h], F32) for h in heads], axis=1)
            st = lax.dot_general(kcat[j // 2], xq, _NT, preferred_element_type=F32)
            st = _add_blocks(st, bias)
            m = jnp.maximum(jnp.max(st, axis=0, keepdims=True), sk)
            pt = jnp.exp(st - m)
            den = jnp.sum(pt, axis=0, keepdims=True) + jnp.exp(sk - m)
            ot = lax.dot_general(vcat[j // 2], pt.astype(BF16), _TN, preferred_element_type=F32) * (1.0 / den)
            lse = m + jnp.log(den)
            for e, h in enumerate(heads):
                lse_ref[h] = lse[:, SWA_BLOCK * e:SWA_BLOCK * (e + 1)]
            _swa_unstack_t(ot, j, o_ref)

    return pl.pallas_call(
        body, name=name, grid=(nb,),
        in_specs=[smem, pl.BlockSpec((SWA_BLOCK, q.shape[1]), lambda i: (i, 0))]
        + _swa_specs(k.shape[1], nb) + _swa_specs(v.shape[1], nb),
        out_specs=[pl.BlockSpec((SWA_BLOCK, q.shape[1]), lambda i: (i, 0)),
                   pl.BlockSpec((SWA_HEADS, 1, SWA_BLOCK), lambda i: (0, 0, i))],
        out_shape=[_sds(q.shape, BF16), _sds((SWA_HEADS, 1, S), F32)],
        compiler_params=_params(("parallel",)),
    )(sink, q, k, k, k, v, v, v)


def _swa_bwd_q(name, q, k, v, do, lse_t, delta_t, sink):
    S = q.shape[0]
    nb = S // SWA_BLOCK
    grp = SWA_HEADS // SWA_KV
    smem = pl.BlockSpec(memory_space=pltpu.SMEM)
    row = pl.BlockSpec((SWA_BLOCK, q.shape[1]), lambda i: (i, 0))
    hrow = pl.BlockSpec((SWA_HEADS, 1, SWA_BLOCK), lambda i: (0, 0, i))

    def body(sink_ref, q_ref, do_ref, lse_ref, dl_ref, kp, kc, kn, vp, vc, vn, dq_ref, ds_ref):
        i = pl.program_id(0)

        @pl.when(i == 0)
        def _():
            ds_ref[...] = jnp.zeros_like(ds_ref)

        bias = [_swa_bias_t(i, S)]
        kcat = [jnp.concatenate([r[:, LANES * u:LANES * (u + 1)] for r in (kp, kc, kn)], axis=0) for u in range(2)]
        vcat = [jnp.concatenate([r[:, LANES * u:LANES * (u + 1)] for r in (vp, vc, vn)], axis=0) for u in range(2)]
        for j in range(SWA_KV):
            heads = range(grp * j, grp * (j + 1))
            xq = _swa_stack([q_ref], heads)
            xdo = _swa_stack([do_ref], heads)
            lse_r = jnp.concatenate([lse_ref[h] for h in heads], axis=1)
            dl_r = jnp.concatenate([dl_ref[h] for h in heads], axis=1)
            st = lax.dot_general(kcat[j // 2], xq, _NT, preferred_element_type=F32)
            pt = jnp.exp(_add_blocks(st, bias) - lse_r)
            dpt = lax.dot_general(vcat[j // 2], xdo, _NT, preferred_element_type=F32)
            dst = (pt * (dpt - dl_r)).astype(BF16)
            _swa_unstack_t(lax.dot_general(kcat[j // 2], dst, _TN, preferred_element_type=F32), j, dq_ref)
            for h in heads:
                dsink = -_rsum(jnp.exp(sink_ref[0, h] - lse_ref[h]) * dl_ref[h])
                ds_ref[h:h + 1, :] += jnp.broadcast_to(dsink, (1, LANES))

    return pl.pallas_call(
        body, name=name, grid=(nb,),
        in_specs=[smem, row, row, hrow, hrow] + _swa_specs(k.shape[1], nb) + _swa_specs(v.shape[1], nb),
        out_specs=[row, pl.BlockSpec((SWA_HEADS, LANES), lambda i: (0, 0))],
        out_shape=[_sds(q.shape, F32), _sds((SWA_HEADS, LANES), F32)],
        compiler_params=_params(("arbitrary",)),
    )(sink, q, do, lse_t, delta_t, k, k, k, v, v, v)


def _swa_bwd_kv(name, q, k, v, do, lse_t, delta_t):
    S = q.shape[0]
    nb = S // SWA_BLOCK
    grp = SWA_HEADS // SWA_KV
    krow = pl.BlockSpec((SWA_BLOCK, k.shape[1]), lambda i: (i, 0))

    def stat3():
        return [pl.BlockSpec((SWA_HEADS, 1, SWA_BLOCK), lambda i: (0, 0, jnp.maximum(i - 1, 0))),
                pl.BlockSpec((SWA_HEADS, 1, SWA_BLOCK), lambda i: (0, 0, i)),
                pl.BlockSpec((SWA_HEADS, 1, SWA_BLOCK), lambda i: (0, 0, jnp.minimum(i + 1, nb - 1)))]

    def body(qp, qc, qn, dop, doc, don, lp, lc, ln, dp_, dc_, dn_, k_ref, v_ref, dk_ref, dv_ref):
        j = pl.program_id(0)
        nh = 2 * grp
        kpos = j * SWA_BLOCK + lax.broadcasted_iota(jnp.int32, (SWA_BLOCK, SWA_BLOCK), 0)
        bias = []
        for b in range(3):
            qpos = (j - 1 + b) * SWA_BLOCK + lax.broadcasted_iota(jnp.int32, (SWA_BLOCK, SWA_BLOCK), 1)
            ok = (jnp.abs(qpos - kpos) <= SWA_WINDOW) & (qpos >= 0) & (qpos < S)
            bias.append(jnp.where(ok, 0.0, -jnp.inf))
        for u in range(SWA_KV // 2):
            heads = range(nh * u, nh * (u + 1))
            kc = k_ref[:, LANES * u:LANES * (u + 1)]
            vc = v_ref[:, LANES * u:LANES * (u + 1)]
            xq = _swa_stack([qp, qc, qn], heads)
            xdo = _swa_stack([dop, doc, don], heads)
            lse_r = jnp.concatenate([r[h] for h in heads for r in (lp, lc, ln)], axis=1)
            dl_r = jnp.concatenate([r[h] for h in heads for r in (dp_, dc_, dn_)], axis=1)
            st = lax.dot_general(kc, xq, _NT, preferred_element_type=F32)
            pt = jnp.exp(_add_blocks(st, bias) - lse_r)
            dpt = lax.dot_general(vc, xdo, _NT, preferred_element_type=F32)
            dst = (pt * (dpt - dl_r)).astype(BF16)
            dv_ref[:, LANES * u:LANES * (u + 1)] = lax.dot_general(pt.astype(BF16), xdo, _NN, preferred_element_type=F32)
            dk_ref[:, LANES * u:LANES * (u + 1)] = lax.dot_general(dst, xq, _NN, preferred_element_type=F32)

    return pl.pallas_call(
        body, name=name, grid=(nb,),
        in_specs=_swa_specs(q.shape[1], nb) + _swa_specs(do.shape[1], nb) + stat3() + stat3() + [krow, krow],
        out_specs=[krow, krow],
        out_shape=[_sds(k.shape, F32), _sds(v.shape, F32)],
        compiler_params=_params(("parallel",)),
    )(q, q, q, do, do, do, lse_t, lse_t, lse_t, delta_t, delta_t, delta_t, k, v)


def _peer_of(x, y, c, kk):
    return (x ^ ((kk >> 2) & 1), y ^ ((kk >> 1) & 1), c ^ (kk & 1))


def _own_slot(a, scatter):
    me = 4 * lax.axis_index("x") + 2 * lax.axis_index("y") + lax.axis_index("c")
    shape = a.shape if scatter else (N_DEV,) + a.shape
    own = lax.dynamic_slice_in_dim(a, me, 1, 0) if scatter else a[None]
    return lax.dynamic_update_slice_in_dim(lax.empty(shape, a.dtype), own, me, 0)


def _exchange_start(name, tensors):
    n = len(tensors)
    hbm = pl.BlockSpec(memory_space=pltpu.HBM)
    sem = pl.BlockSpec(memory_space=pltpu.SEMAPHORE)
    srcs = [pltpu.with_memory_space_constraint(a, pltpu.HBM) for a, _ in tensors]
    lands = [pltpu.with_memory_space_constraint(_own_slot(a, sc), pltpu.HBM) for a, sc in tensors]

    def body(*refs):
        ins, dst = refs[:n], refs[n:2 * n]
        send, recv, token = refs[2 * n], refs[2 * n + 1], refs[4 * n + 2]
        x, y, c = lax.axis_index("x"), lax.axis_index("y"), lax.axis_index("c")
        me = 4 * x + 2 * y + c
        for t in range(n):
            for kk in range(1, N_DEV):
                px, py, pc = _peer_of(x, y, c, kk)
                src = ins[t].at[4 * px + 2 * py + pc] if tensors[t][1] else ins[t]
                k1 = t * (N_DEV - 1) + kk - 1
                pltpu.make_async_remote_copy(src_ref=src, dst_ref=dst[t].at[me], send_sem=send.at[k1],
                                             recv_sem=recv.at[k1], device_id=(px, py, pc),
                                             device_id_type=pl.DeviceIdType.MESH).start()
        token[...] = jnp.zeros_like(token)

    out = pl.pallas_call(
        body, name=name,
        in_specs=[hbm] * (2 * n),
        out_specs=[sem, sem] + [hbm] * (2 * n) + [pl.BlockSpec(memory_space=pltpu.VMEM)],
        out_shape=[pltpu.SemaphoreType.DMA((n * (N_DEV - 1),)), pltpu.SemaphoreType.DMA((n * (N_DEV - 1),))]
        + [pltpu.HBM(a.shape, a.dtype) for a in srcs] + [pltpu.HBM(a.shape, a.dtype) for a in lands]
        + [_sds((8, LANES), F32)],
        input_output_aliases={i: i + 2 for i in range(2 * n)},
        compiler_params=pltpu.CompilerParams(has_side_effects=pltpu.SideEffectType.DATAFLOW_SIDE_EFFECTING),
    )(*srcs, *lands)
    return (out[0], out[1], out[2:2 + n], out[2 + n:2 + 2 * n], [sc for _, sc in tensors]), out[2 + 2 * n]


def _exchange_wait(name, started, after):
    send_s, recv_s, srcs, lands, flags = started
    n = len(srcs)
    hbm = pl.BlockSpec(memory_space=pltpu.HBM)
    sem = pl.BlockSpec(memory_space=pltpu.SEMAPHORE)

    def body(*refs):
        ins, dst, send, recv = refs[:n], refs[n:2 * n], refs[2 * n], refs[2 * n + 1]
        x, y, c = lax.axis_index("x"), lax.axis_index("y"), lax.axis_index("c")
        me = 4 * x + 2 * y + c
        for t in range(n):
            for kk in range(1, N_DEV):
                px, py, pc = _peer_of(x, y, c, kk)
                src = ins[t].at[me] if flags[t] else ins[t]
                k1 = t * (N_DEV - 1) + kk - 1
                cp = pltpu.make_async_remote_copy(src_ref=src, dst_ref=dst[t].at[4 * px + 2 * py + pc],
                                                  send_sem=send.at[k1], recv_sem=recv.at[k1],
                                                  device_id=(px, py, pc), device_id_type=pl.DeviceIdType.MESH)
                cp.wait_send()
                cp.wait_recv()

    out = pl.pallas_call(
        body, name=name,
        in_specs=[hbm] * (2 * n) + [sem, sem, pl.BlockSpec(memory_space=pl.ANY)],
        out_specs=[hbm] * (2 * n),
        out_shape=[pltpu.HBM(a.shape, a.dtype) for a in srcs] + [pltpu.HBM(a.shape, a.dtype) for a in lands],
        input_output_aliases={i: i for i in range(2 * n)},
        compiler_params=pltpu.CompilerParams(has_side_effects=pltpu.SideEffectType.DATAFLOW_SIDE_EFFECTING),
    )(*srcs, *lands, send_s, recv_s, after)
    return out[n:]


def _sum8(name, parts):
    _, R, C = parts.shape

    def fn(r, c, o):
        acc = r[0][0].astype(F32)
        for s in range(1, N_DEV):
            acc = acc + r[0][s].astype(F32)
        o[0][...] = acc
        return ()

    tm = R if R % 256 else 256
    return _rowwise(name, fn, [parts], [], [_sds((R, C), F32)], tm=tm)[0]


def _adamw(name, w, g, m, v):
    bc1 = 1.0 - ADAM_B1 ** ADAM_STEP
    bc2 = 1.0 - ADAM_B2 ** ADAM_STEP

    def fn(r, c, o):
        wv, gv, mv, vv = (x[...] for x in r)
        mn = ADAM_B1 * mv + (1.0 - ADAM_B1) * gv
        vn = ADAM_B2 * vv + (1.0 - ADAM_B2) * (gv * gv)
        o[0][...] = -ADAM_LR * ((mn / bc1) / (jnp.sqrt(vn / bc2) + ADAM_EPS) + ADAM_WD * wv)
        o[1][...] = mn
        o[2][...] = vn
        return ()

    tm = max(8, min(512, 1 << ((16 << 20) // (56 * w.shape[1])).bit_length() - 1))
    return _rowwise(name, fn, [w, g, m, v], [], [_sds(w.shape, F32)] * 3, tm=tm)


def _rope_cs(pos, dim, theta):
    inv = jnp.float32(theta) ** (-jnp.arange(0, dim, 2, dtype=jnp.float32) / dim)
    ang = pos.astype(jnp.float32)[:, None] * inv[None, :]
    return jnp.cos(ang), jnp.sin(ang)


def _tables(S):
    pos = jnp.arange(S)
    c, s = _rope_cs(pos, MLA_ROPE, ROPE_THETA)
    mla = (jnp.concatenate([c, c, c, c], 1), jnp.concatenate([-s, s, -s, s], 1))
    rc, rs = _rope_cs(pos // GRID_W, HD // 2, AXIAL_THETA)
    cc, cs = _rope_cs(pos % GRID_W, HD // 2, AXIAL_THETA)
    axial = (jnp.concatenate([rc, rc, cc, cc], 1), jnp.concatenate([-rs, rs, -cs, cs], 1))
    c, s = _rope_cs(pos, SWA_ROT, ROPE_THETA)
    one, zero = jnp.ones((S, SWA_D - SWA_ROT), F32), jnp.zeros((S, SWA_D - SWA_ROT), F32)
    swa = (jnp.concatenate([c, c, one, c, c, one], 1), jnp.concatenate([-s, s, zero, -s, s, zero], 1))
    return mla, axial, swa


_WEIGHTS = ['even_norm', 'even_w_in', 'mla_q_lat_norm', 'mla_kv_lat_norm', 'mla_w_uq', 'mla_w_ukv', 'mla_q_norm',
            'mla_k_nope_norm', 'mla_k_rope_norm', 'gqa_q_norm', 'gqa_k_norm', 'even_w_out', 'odd_norm', 'odd_w_qkv',
            'swa_q_norm', 'swa_k_norm', 'swa_sink', 'odd_w_out', 'mlp_norm', 'mlp_w_up', 'mlp_w_down']
_SMALL = ['even_norm', 'mla_q_lat_norm', 'mla_kv_lat_norm', 'mla_q_norm', 'mla_k_nope_norm', 'mla_k_rope_norm',
          'gqa_q_norm', 'gqa_k_norm', 'swa_q_norm', 'swa_k_norm', 'swa_sink', 'mlp_norm']


def _relu2(acc):
    rl = jnp.maximum(acc, 0.0)
    return rl * rl, rl


def _mul2(acc, rl):
    return (acc * (2.0 * rl.astype(F32)),)


def _add(acc, res):
    return (acc + res,)


def _step(x, tgt, w, m, v):
    S, D = x.shape
    nw = len(_WEIGHTS)
    tab_mla, tab_ax, tab_swa = _tables(S)
    bf = lambda a: a.astype(BF16)

    w_up_s, w_dn_s = w['mlp_w_up'], w['mlp_w_down']
    first, tok0 = _exchange_start("gather_first_start", [
        (bf(w['even_w_in'][0].T), False), (bf(w['mla_w_uq'][0].T), False), (bf(w['mla_w_ukv'][0].T), False)])
    h0 = _rmsnorm("even_norm", x, w['even_norm'], after=[tok0])
    gathered = _exchange_wait("gather_first_wait", first, h0)
    zero = jnp.minimum(jnp.abs(gathered[2][0, 0:1, 0:1].astype(F32)), 0.0)
    later, tok = _exchange_start("gather_rest_start", [
        (bf(w['even_w_out'][0]), False), (bf(w['odd_w_qkv'][0].T), False), (bf(w['odd_w_out'][0]), False),
        (bf(w_up_s[0].T), False), (bf(w_up_s[1].T), False), (bf(w_dn_s[0]), False), (bf(w_dn_s[1]), False),
        (w['odd_norm'] + zero, False)])
    flat = lambda a: a.reshape((a.shape[0] * a.shape[1],) + a.shape[2:])
    win_t, wuq_g, wukv_t = [flat(a) for a in gathered]
    win_t = jnp.concatenate([win_t[:KR_END], jnp.zeros((LANES - MLA_ROPE, D), BF16), win_t[KR_END:]], 0)
    wuq_g = gathered[1]
    wuq_t = jnp.concatenate([wuq_g[:, :HD].reshape(HEADS * HD, Q_LORA),
                             wuq_g[:, HD:].reshape(HEADS * MLA_ROPE, Q_LORA)], 0)

    z64 = jnp.zeros((1, 64), F32)
    qn = w['mla_q_norm']
    g_even = [w['mla_q_lat_norm'], w['mla_kv_lat_norm'], jnp.concatenate([w['mla_k_rope_norm'], z64], 1),
              w['gqa_q_norm'], w['gqa_k_norm']]
    g_mla = [qn[:, :HD], jnp.concatenate([qn[:, HD:], qn[:, HD:]], 1), w['mla_k_nope_norm']]
    g_swa = [jnp.concatenate([w['swa_q_norm']] * 2, 1), jnp.concatenate([w['swa_k_norm']] * 2, 1)]

    def loss_head(acc, res, t):
        e = acc + res - t
        return e * (1.0 / D), _csum(_rsum(e * e))

    def mlp_fwd(l, xin, last=False):
        hn = _rmsnorm(f"mlp{l}_norm", xin, w['mlp_norm'][l:l + 1])
        a, rl = _mm(f"mlp{l}_up", hn, wup_t[l], "nt", (BF16, BF16), epilogue=_relu2)
        if last:
            xout = _mm(f"mlp{l}_down", a, wdn[l], "nn", (F32,), epilogue=loss_head, extras=(xin, tgt),
                       sums=[_sds((1, 1), F32)])
        else:
            xout = _mm(f"mlp{l}_down", a, wdn[l], "nn", (F32,), epilogue=_add, extras=(xin,))
        return xout, (hn, a, rl)

    proj = _mm("even_in", h0, win_t, "nt", (F32,), tm=512, tn=P_END)
    cqn, ckvn, krd, qg, kg, vg = _even_prep("even_prep", proj, tab_mla + tab_ax, g_even, after=[tok])
    qa = _mm("mla_uq", cqn, wuq_t, "nt", (F32,))
    kv = _mm("mla_ukv", ckvn, wukv_t, "nt", (F32,))
    q_a, k_a, v_a = _mla_prep("mla_prep", qa, kv, krd, tab_mla, g_mla)
    o_a, lse_a = _flash_fwd("mla_attn", q_a, k_a, v_a)
    o_g, lse_g = _flash_fwd("gqa_attn", qg, kg, vg)
    merged = jnp.concatenate([o_a, o_g], 1)
    rest = _exchange_wait("gather_rest_wait", later, merged)
    wout_e, wqkv_t, wout_o, wup0_t, wup1_t, wdn0, wdn1 = [flat(a) for a in rest[:7]]
    odd_norm = rest[7].reshape(1, D)
    wup_t, wdn = (wup0_t, wup1_t), (wdn0, wdn1)
    x1 = _mm("even_out", merged, wout_e, "nn", (F32,), epilogue=_add, extras=(x,))
    x2, mlp0 = mlp_fwd(0, x1)

    h1 = _rmsnorm("odd_norm", x2, odd_norm)
    qkv = _mm("odd_qkv", h1, wqkv_t, "nt", (F32,))
    q_s, k_s, v_s = _swa_prep("swa_prep", qkv, tab_swa, g_swa)
    o_s, lse_s = _swa_fwd("swa_attn", q_s, k_s, v_s, w['swa_sink'])
    x3 = _mm("odd_out", o_s, wout_o, "nn", (F32,), epilogue=_add, extras=(x2,))
    (dy, loss_acc), mlp1 = mlp_fwd(1, x3, last=True)
    loss = lax.psum(0.5 / D * loss_acc[0, 0], ("x", "y", "c"))

    gsm = {}

    def mlp_bwd(l, dout, xin, saved):
        hn, a, rl = saved
        du = _mm(f"mlp{l}_dact", dout, wdn[l], "nt", (BF16,), epilogue=_mul2, extras=(rl,))
        g_dn = _mm(f"mlp{l}_gdown", a, dout, "tn", (BF16,))
        g_up = _mm(f"mlp{l}_gup", du, hn, "tn", (BF16,))
        dhn = _mm(f"mlp{l}_dnorm", du, wup_t[l], "nn", (F32,))
        din, g_n = _rmsnorm_bwd(f"mlp{l}_norm_bwd", dout, xin, dhn, w['mlp_norm'][l:l + 1])
        return din, g_dn, g_up, g_n

    dx3, g_dn1, g_up1, g_mn1 = mlp_bwd(1, dy, x3, mlp1)
    split = lambda a: a.reshape((N_DEV, a.shape[0] // N_DEV) + a.shape[1:])
    sc1, tok1 = _exchange_start("scatter_mlp1_start", [(split(g_up1), True), (split(g_dn1), True)])

    g_wout_o = _mm("odd_gout", o_s, dx3, "tn", (BF16,))
    do_s = _mm("odd_dattn", dx3, wout_o, "nt", (BF16,))
    dl_s = _delta_t("swa_delta", do_s, o_s, after=[tok1])
    dq_s, dsink = _swa_bwd_q("swa_bwd_q", q_s, k_s, v_s, do_s, lse_s, dl_s, w['swa_sink'])
    dk_s, dv_s = _swa_bwd_kv("swa_bwd_kv", q_s, k_s, v_s, do_s, lse_s, dl_s)
    dqkv, g_sq, g_sk = _swa_prep_bwd("swa_prep_bwd", dq_s, dk_s, dv_s, qkv, tab_swa, g_swa)
    g_wqkv = _mm("odd_gqkv", dqkv, h1, "tn", (BF16,))
    dh1 = _mm("odd_dnorm", dqkv, wqkv_t, "nn", (F32,), tk=dqkv.shape[1])
    dx2, g_on = _rmsnorm_bwd("odd_norm_bwd", dx3, x2, dh1, odd_norm)
    gsm['swa_q_norm'] = g_sq[:, :64] + g_sq[:, 64:]
    gsm['swa_k_norm'] = g_sk[:, :64] + g_sk[:, 64:]
    gsm['swa_sink'] = dsink[:, 0].reshape(1, SWA_HEADS)

    dx1, g_dn0, g_up0, g_mn0 = mlp_bwd(0, dx2, x1, mlp0)
    sc2, tok2 = _exchange_start("scatter_mid_start", [(split(g_wout_o), True), (split(g_wqkv), True),
                                                      (split(g_up0), True), (split(g_dn0), True)])
    gsm['mlp_norm'] = jnp.concatenate([g_mn0, g_mn1], 0)

    g_wout_e = _mm("even_gout", merged, dx1, "tn", (BF16,))
    dmerged = _mm("even_dattn", dx1, wout_e, "nt", (BF16,))
    dl_e = _delta("even_delta", dmerged, merged, after=[tok2])
    dq_a, dk_a, dv_a = _flash_bwd("mla_attn_bwd", q_a, k_a, v_a, dmerged, 0, lse_a, dl_e)
    dq_g, dk_g, dv_g = _flash_bwd("gqa_attn_bwd", qg, kg, vg, dmerged, HEADS, lse_g, dl_e)
    dqa, dkv, dkrd, g_qnn, g_qnr, g_kn = _mla_prep_bwd("mla_prep_bwd", dq_a, dk_a, dv_a, qa, kv, tab_mla, g_mla)
    g_wuq = _mm("mla_guq", dqa, cqn, "tn", (BF16,))
    dcqn = _mm("mla_dq_lat", dqa, wuq_t, "nn", (F32,))
    g_wukv = _mm("mla_gukv", dkv, ckvn, "tn", (BF16,))
    dckvn = _mm("mla_dkv_lat", dkv, wukv_t, "nn", (F32,))
    g_wuq = jnp.concatenate([g_wuq[:HEADS * HD].reshape(HEADS, HD, Q_LORA),
                             g_wuq[HEADS * HD:].reshape(HEADS, MLA_ROPE, Q_LORA)], 1)
    sc3, tok_e = _exchange_start("scatter_even_start", [(g_wuq, True), (split(g_wukv), True), (split(g_wout_e), True)])
    dproj, g_ql, g_kvl, g_kr, g_gq, g_gk = _even_prep_bwd("even_prep_bwd", dcqn, dckvn, dkrd, dq_g, dk_g, dv_g,
                                                          proj, tab_mla + tab_ax, g_even, after=[tok_e])
    g_win = _mm("even_gin", dproj, h0, "tn", (BF16,), tm=P_END, tk=1024)
    dh0 = _mm("even_dnorm", dproj, win_t, "nn", (F32,), tk=P_END)
    grad_x, g_en = _rmsnorm_bwd("even_norm_bwd", dx1, x, dh0, w['even_norm'])
    gsm.update(even_norm=g_en, mla_q_lat_norm=g_ql, mla_kv_lat_norm=g_kvl,
               mla_q_norm=jnp.concatenate([g_qnn, g_qnr[:, :64] + g_qnr[:, 64:]], 1), mla_k_nope_norm=g_kn,
               mla_k_rope_norm=g_kr[:, :64], gqa_q_norm=g_gq, gqa_k_norm=g_gk)

    g_win = jnp.concatenate([g_win[:KR_END], g_win[P_QG:]], 0)
    small_sizes = [w[n].size for n in _SMALL] + [D]
    small_vec = jnp.concatenate([gsm[n].reshape(1, -1) for n in _SMALL] + [g_on], 1)
    pad = (-small_vec.shape[1]) % LANES
    small_vec = jnp.pad(small_vec, ((0, 0), (0, pad)))
    last_st, tok3 = _exchange_start("scatter_last_start", [(split(g_win), True), (small_vec, False)])
    p_up1, p_dn1 = _exchange_wait("scatter_mlp1_wait", sc1, tok3)
    p_wout_o, p_wqkv, p_up0, p_dn0 = _exchange_wait("scatter_mid_wait", sc2, tok3)
    p_wuq, p_wukv, p_wout_e = _exchange_wait("scatter_even_wait", sc3, tok3)
    parts = dict(odd_w_qkv=p_wqkv, odd_w_out=p_wout_o, up0=p_up0, up1=p_up1, dn0=p_dn0, dn1=p_dn1,
                 mla_w_uq=p_wuq, mla_w_ukv=p_wukv, even_w_out=p_wout_e)
    red = {n: _sum8("sum_" + n, p) for n, p in parts.items()}
    grads = {
        'odd_w_qkv': red['odd_w_qkv'].T[None], 'odd_w_out': red['odd_w_out'][None],
        'mlp_w_up': jnp.stack([red['up0'].T, red['up1'].T]), 'mlp_w_down': jnp.stack([red['dn0'], red['dn1']]),
        'mla_w_uq': red['mla_w_uq'].T[None], 'mla_w_ukv': red['mla_w_ukv'].T[None],
        'even_w_out': red['even_w_out'][None],
    }
    delta, new_m, new_v = {}, {}, {}

    def apply(names):
        for n in names:
            shp = w[n].shape
            two = lambda a: a.reshape(shp[0] * shp[1], shp[2])
            d_, m_, v_ = _adamw("adamw_" + n, two(w[n]), two(grads[n]), two(m[n]), two(v[n]))
            delta[n], new_m[n], new_v[n] = d_.reshape(shp), m_.reshape(shp), v_.reshape(shp)

    apply(['mla_w_uq', 'mla_w_ukv', 'even_w_out', 'odd_w_qkv', 'odd_w_out', 'mlp_w_up', 'mlp_w_down'])
    last = _exchange_wait("scatter_last_wait", last_st, new_v['mlp_w_down'])
    small_g = _sum8("sum_small", last[1])
    grads['even_w_in'] = _sum8("sum_even_w_in", last[0]).T[None]
    apply(['even_w_in'])
    off = 0
    for n, sz in zip(_SMALL + ['odd_norm_full'], small_sizes):
        seg = small_g[:, off:off + sz]
        off += sz
        if n == 'odd_norm_full':
            me = 4 * lax.axis_index("x") + 2 * lax.axis_index("y") + lax.axis_index("c")
            grads['odd_norm'] = lax.dynamic_slice(seg, (0, me * (D // N_DEV)), (1, D // N_DEV))
        else:
            grads[n] = seg.reshape(w[n].shape)

    sm_names = _SMALL + ['odd_norm']
    pack = lambda d: jnp.concatenate([d[n].reshape(1, -1) for n in sm_names], 1)
    pw, pg, pm, pv = pack(w), pack(grads), pack(m), pack(v)
    padw = (-pw.shape[1]) % LANES
    padf = lambda a: jnp.pad(a, ((0, 0), (0, padw)))
    d_, m_, v_ = _adamw("adamw_small", padf(pw), padf(pg), padf(pm), jnp.pad(pv, ((0, 0), (0, padw)), constant_values=1.0))
    off = 0
    for n in sm_names:
        sz = w[n].size
        delta[n] = d_[:, off:off + sz].reshape(w[n].shape)
        new_m[n] = m_[:, off:off + sz].reshape(w[n].shape)
        new_v[n] = v_[:, off:off + sz].reshape(w[n].shape)
        off += sz

    return (loss, grad_x[None], *[grads[n] for n in _WEIGHTS], *[delta[n] for n in _WEIGHTS],
            *[new_m[n] for n in _WEIGHTS], *[new_v[n] for n in _WEIGHTS])


def kernel(x, even_norm, even_w_in, mla_q_lat_norm, mla_kv_lat_norm, mla_w_uq, mla_w_ukv, mla_q_norm, mla_k_nope_norm, mla_k_rope_norm, gqa_q_norm, gqa_k_norm, even_w_out, odd_norm, odd_w_qkv, swa_q_norm, swa_k_norm, swa_sink, odd_w_out, mlp_norm, mlp_w_up, mlp_w_down, loss_target, m_even_norm, m_even_w_in, m_mla_q_lat_norm, m_mla_kv_lat_norm, m_mla_w_uq, m_mla_w_ukv, m_mla_q_norm, m_mla_k_nope_norm, m_mla_k_rope_norm, m_gqa_q_norm, m_gqa_k_norm, m_even_w_out, m_odd_norm, m_odd_w_qkv, m_swa_q_norm, m_swa_k_norm, m_swa_sink, m_odd_w_out, m_mlp_norm, m_mlp_w_up, m_mlp_w_down, v_even_norm, v_even_w_in, v_mla_q_lat_norm, v_mla_kv_lat_norm, v_mla_w_uq, v_mla_w_ukv, v_mla_q_norm, v_mla_k_nope_norm, v_mla_k_rope_norm, v_gqa_q_norm, v_gqa_k_norm, v_even_w_out, v_odd_norm, v_odd_w_qkv, v_swa_q_norm, v_swa_k_norm, v_swa_sink, v_odd_w_out, v_mlp_norm, v_mlp_w_up, v_mlp_w_down):
    ws = (even_norm, even_w_in, mla_q_lat_norm, mla_kv_lat_norm, mla_w_uq, mla_w_ukv, mla_q_norm, mla_k_nope_norm, mla_k_rope_norm, gqa_q_norm, gqa_k_norm, even_w_out, odd_norm, odd_w_qkv, swa_q_norm, swa_k_norm, swa_sink, odd_w_out, mlp_norm, mlp_w_up, mlp_w_down)
    ms = (m_even_norm, m_even_w_in, m_mla_q_lat_norm, m_mla_kv_lat_norm, m_mla_w_uq, m_mla_w_ukv, m_mla_q_norm, m_mla_k_nope_norm, m_mla_k_rope_norm, m_gqa_q_norm, m_gqa_k_norm, m_even_w_out, m_odd_norm, m_odd_w_qkv, m_swa_q_norm, m_swa_k_norm, m_swa_sink, m_odd_w_out, m_mlp_norm, m_mlp_w_up, m_mlp_w_down)
    vs = (v_even_norm, v_even_w_in, v_mla_q_lat_norm, v_mla_kv_lat_norm, v_mla_w_uq, v_mla_w_ukv, v_mla_q_norm, v_mla_k_nope_norm, v_mla_k_rope_norm, v_gqa_q_norm, v_gqa_k_norm, v_even_w_out, v_odd_norm, v_odd_w_qkv, v_swa_q_norm, v_swa_k_norm, v_swa_sink, v_odd_w_out, v_mlp_norm, v_mlp_w_up, v_mlp_w_down)
    return _step(x[0], loss_target[0], dict(zip(_WEIGHTS, ws)), dict(zip(_WEIGHTS, ms)), dict(zip(_WEIGHTS, vs)))
```

```python
import functools

import jax
import jax.numpy as jnp
from jax import lax
from jax.experimental import pallas as pl
from jax.experimental.pallas import tpu as pltpu

F32 = jnp.float32
BF16 = jnp.bfloat16

N_DEV = 8
NORM_EPS = 1e-6
ROPE_THETA = 500000.0
AXIAL_THETA = 10000.0
GRID_W = 64
HEADS = 8
GQA_KV = 2
HD = 128
MLA_ROPE = 64
MLA_QK = HD + MLA_ROPE
Q_LORA = 512
KV_LORA = 256
SWA_HEADS = 32
SWA_KV = 4
SWA_D = 64
SWA_ROT = 16
SWA_WINDOW = 128
SWA_BLOCK = 128
MLA_SCALE, GQA_SCALE, SWA_SCALE = MLA_QK ** -0.5, HD ** -0.5, SWA_D ** -0.5
LANES = 128
ADAM_LR, ADAM_B1, ADAM_B2, ADAM_EPS, ADAM_WD, ADAM_STEP = 0.001, 0.9, 0.999, 1e-08, 0.01, 10
VMEM_LIMIT = 56 * 1024 * 1024
MM_TILE = (1024, 1024, 2048)
FWD_TILE = (1024, 4096)
BWD_TILE = (1024, 1024)
ROW_TILE = 256

P_CQ, P_CKV, P_KR, P_QG = 0, Q_LORA, Q_LORA + KV_LORA, Q_LORA + KV_LORA + LANES
P_KG = P_QG + HEADS * HD
P_VG = P_KG + GQA_KV * HD
P_END = P_VG + GQA_KV * HD
KR_END = Q_LORA + KV_LORA + MLA_ROPE


def _tile(n, prefs):
    for t in prefs:
        if n % t == 0 and t <= n:
            return t
    return n


def _params(sem):
    return pltpu.CompilerParams(dimension_semantics=sem, vmem_limit_bytes=VMEM_LIMIT)


_DIMS = {"nn": ((1,), (0,)), "nt": ((1,), (1,)), "tn": ((0,), (0,))}


def _mm(name, a, b, mode, out_dtypes, epilogue=None, extras=(), sums=(), tm=MM_TILE[0], tn=MM_TILE[1], tk=MM_TILE[2]):
    if mode == "nn":
        (M, K), (_, N) = a.shape, b.shape
    elif mode == "nt":
        (M, K), (N, _) = a.shape, b.shape
    else:
        (K, M), (_, N) = a.shape, b.shape
    tm = _tile(M, (tm, 512, 256, 128))
    tn = _tile(N, (tn, 512, 256, 128))
    tk = _tile(K, (tk, 1024, 512, 256, 128))
    nk = K // tk
    ne, no, ns = len(extras), len(out_dtypes), len(sums)
    if mode == "tn":
        a_spec = pl.BlockSpec((tk, tm), lambda i, j, k: (k, i))
    else:
        a_spec = pl.BlockSpec((tm, tk), lambda i, j, k: (i, k))
    if mode == "nt":
        b_spec = pl.BlockSpec((tn, tk), lambda i, j, k: (j, k))
    else:
        b_spec = pl.BlockSpec((tk, tn), lambda i, j, k: (k, j))
    o_spec = pl.BlockSpec((tm, tn), lambda i, j, k: (i, j))
    dims = (_DIMS[mode], ((), ()))

    def body(a_ref, b_ref, *rest):
        ex, outs = rest[:ne], rest[ne:ne + no]
        k = pl.program_id(2)
        part = lax.dot_general(a_ref[...].astype(BF16), b_ref[...].astype(BF16), dims, preferred_element_type=F32)

        def finish(total):
            res = epilogue(total, *[e[...] for e in ex]) if epilogue else (total,)
            for o, r in zip(outs, res[:no]):
                o[...] = r.astype(o.dtype)
            if ns:
                first = (pl.program_id(0) == 0) & (pl.program_id(1) == 0)

                @pl.when(first)
                def _():
                    for s_ref, r in zip(rest[ne + no:ne + no + ns], res[no:]):
                        s_ref[...] = r

                @pl.when(jnp.logical_not(first))
                def _():
                    for s_ref, r in zip(rest[ne + no:ne + no + ns], res[no:]):
                        s_ref[...] += r

        if nk == 1:
            finish(part)
            return
        acc = rest[ne + no + ns]

        @pl.when(k == 0)
        def _():
            acc[...] = part

        @pl.when((k > 0) & (k < nk - 1))
        def _():
            acc[...] += part

        @pl.when(k == nk - 1)
        def _():
            finish(acc[...] + part)

    out = pl.pallas_call(
        body, name=name, grid=(M // tm, N // tn, nk),
        in_specs=[a_spec, b_spec] + [o_spec] * ne,
        out_specs=[o_spec] * no + [pl.BlockSpec(tuple(s.shape), lambda i, j, k: (0,) * len(s.shape)) for s in sums],
        out_shape=[jax.ShapeDtypeStruct((M, N), d) for d in out_dtypes] + list(sums),
        scratch_shapes=[pltpu.VMEM((tm, tn), F32)] if nk > 1 else [],
        compiler_params=_params(("arbitrary",) * 3 if ns else ("parallel", "parallel", "arbitrary")),
    )(a, b, *extras)
    return out[0] if no + ns == 1 else out


def _rowwise(name, fn, rows, consts, outs, accs=(), tm=ROW_TILE, after=()):
    S = rows[0].shape[-2]
    tm = _tile(S, (tm, 128, 64, 32, 16, 8))
    nr, nc, nd, no, na = len(rows), len(consts), len(after), len(outs), len(accs)

    def rspec(shape):
        if len(shape) == 2:
            return pl.BlockSpec((tm, shape[1]), lambda i: (i, 0))
        return pl.BlockSpec((shape[0], tm, shape[2]), lambda i: (0, i, 0))

    def cspec(shape):
        return pl.BlockSpec(tuple(shape), lambda i: (0,) * len(shape))

    def ospec(o):
        if isinstance(o, tuple):
            return pl.BlockSpec((o[1].shape[0], o[1].shape[1], tm), lambda i: (0, 0, i))
        return rspec(o.shape)

    out_sds = [o[1] if isinstance(o, tuple) else o for o in outs]

    def body(*refs):
        r, c = refs[:nr], refs[nr:nr + nc]
        o, a = refs[nr + nc + nd:nr + nc + nd + no], refs[nr + nc + nd + no:]
        vals = fn(r, c, o)
        if na:
            @pl.when(pl.program_id(0) == 0)
            def _():
                for ar in a:
                    ar[...] = jnp.zeros_like(ar)

            for ar, v in zip(a, vals):
                ar[...] += v

    res = pl.pallas_call(
        body, name=name, grid=(S // tm,),
        in_specs=[rspec(x.shape) for x in rows] + [cspec(x.shape) for x in consts]
        + [pl.BlockSpec(memory_space=pl.ANY)] * len(after),
        out_specs=[ospec(x) for x in outs] + [cspec(x.shape) for x in accs],
        out_shape=out_sds + list(accs),
        compiler_params=_params(("arbitrary",) if na else ("parallel",)),
    )(*rows, *consts, *after)
    return res


def _sds(shape, dtype):
    return jax.ShapeDtypeStruct(tuple(shape), dtype)


def _rsum(x):
    return jnp.sum(x, axis=-1, keepdims=True)


def _csum(x):
    return jnp.sum(x, axis=0, keepdims=True)


def _lane(shape):
    return lax.broadcasted_iota(jnp.int32, shape, 1)


def _partner(x, seg, half):
    lane = _lane(x.shape) % seg
    return jnp.where(lane < half, pltpu.roll(x, LANES - half, 1), pltpu.roll(x, half, 1))


def _rope(x, c, s, seg, half):
    return x * c + _partner(x, seg, half) * s


def _rope_bwd(dy, c, s, seg, half):
    t = _partner(dy * s, seg, half)
    if seg != 2 * half:
        t = jnp.where(_lane(dy.shape) % seg < 2 * half, t, 0.0)
    return dy * c + t


def _rms_r(ss, n):
    return lax.rsqrt(ss * (1.0 / n) + NORM_EPS)


def _rms_bwd(x, r, g, dy, dot_scale):
    dyg = dy * g
    return r * dyg - x * (r * r * r) * (_rsum(dyg * x) * dot_scale)


def _rmsnorm(name, x, g, after=()):
    D = x.shape[1]

    def fn(r, c, o):
        xv = r[0][...]
        o[0][...] = (xv * _rms_r(_rsum(xv * xv), D) * c[0][...]).astype(BF16)
        return ()

    return _rowwise(name, fn, [x], [g], [_sds(x.shape, BF16)], after=after)[0]


def _rmsnorm_bwd(name, dres, x, dh, g):
    D = x.shape[1]

    def fn(r, c, o):
        xv, dhv, gv = r[1][...], r[2][...], c[0][...]
        rr = _rms_r(_rsum(xv * xv), D)
        o[0][...] = r[0][...] + _rms_bwd(xv, rr, gv, dhv, 1.0 / D)
        return (_csum(dhv * xv * rr),)

    return _rowwise(name, fn, [dres, x, dh], [g], [_sds(x.shape, F32)], [_sds((1, D), F32)])


def _even_prep(name, proj, tabs, gains, after=()):
    S = proj.shape[0]
    cm, sm, ca, sa = tabs

    def fn(r, c, o):
        p, cmv, smv, cav, sav = r[0], r[1][...], r[2][...], r[3][...], r[4][...]
        g_ql, g_kvl, g_kr, g_q, g_k = (x[...] for x in c)
        ones = _seg_ones(LANES)
        cq = p[:, P_CQ:P_CKV]
        o[0][...] = (cq * _wide(_rms_r(_seg_sum(_fold(cq * cq), ones), Q_LORA), Q_LORA // LANES) * g_ql).astype(BF16)
        ckv = p[:, P_CKV:P_KR]
        o[1][...] = (ckv * _wide(_rms_r(_seg_sum(_fold(ckv * ckv), ones), KV_LORA), KV_LORA // LANES) * g_kvl).astype(BF16)
        kr = p[:, P_KR:P_QG]
        y = _rope(kr * _rms_r(_seg_sum(kr * kr, ones), MLA_ROPE) * g_kr, cmv, smv, 64, 32)
        o[2][...] = (y + pltpu.roll(y, 64, 1)).astype(BF16)
        for h in range(HEADS):
            xh = p[:, P_QG + HD * h:P_QG + HD * (h + 1)]
            o[3][h] = (_rope(xh * _rms_r(_seg_sum(xh * xh, ones), HD) * g_q, cav, sav, 64, 32) * GQA_SCALE).astype(BF16)
        for h in range(GQA_KV):
            xh = p[:, P_KG + HD * h:P_KG + HD * (h + 1)]
            o[4][h] = _rope(xh * _rms_r(_seg_sum(xh * xh, ones), HD) * g_k, cav, sav, 64, 32).astype(BF16)
            o[5][h] = p[:, P_VG + HD * h:P_VG + HD * (h + 1)].astype(BF16)
        return ()

    outs = [_sds((S, Q_LORA), BF16), _sds((S, KV_LORA), BF16), _sds((S, LANES), BF16),
            _sds((HEADS, S, HD), BF16), _sds((GQA_KV, S, HD), BF16), _sds((GQA_KV, S, HD), BF16)]
    return _rowwise(name, fn, [proj, cm, sm, ca, sa], gains, outs, after=after)


def _even_prep_bwd(name, dcqn, dckvn, dkrd, dqg, dkg, dvg, proj, tabs, gains, after=()):
    S = proj.shape[0]
    cm, sm, ca, sa = tabs

    def fn(r, c, o):
        p, cmv, smv, cav, sav = r[6], r[7][...], r[8][...], r[9][...], r[10][...]
        g_ql, g_kvl, g_kr, g_q, g_k = (x[...] for x in c)
        out = o[0]
        ones = _seg_ones(LANES)

        def wide_bwd(x, g, d, n):
            w_ = n // LANES
            rr = _wide(_rms_r(_seg_sum(_fold(x * x), ones), n), w_)
            dot = _wide(_seg_sum(_fold(d * g * x), ones) * (1.0 / n), w_)
            return _rms_bwd_b(x, rr, g, d, dot).astype(BF16), _csum(d * x * rr)

        cq = p[:, P_CQ:P_CKV]
        out[:, P_CQ:P_CKV], a_ql = wide_bwd(cq, g_ql, r[0][...], Q_LORA)
        ckv = p[:, P_CKV:P_KR]
        out[:, P_CKV:P_KR], a_kvl = wide_bwd(ckv, g_kvl, r[1][...], KV_LORA)
        kr = p[:, P_KR:P_QG]
        rr = _rms_r(_seg_sum(kr * kr, ones), MLA_ROPE)
        d = r[2][...]
        d = d + pltpu.roll(d, 64, 1)
        d = _rope_bwd(d, cmv, smv, 64, 32)
        low = _lane(d.shape) < 64
        dot = _seg_sum(jnp.where(low, d * g_kr * kr, 0.0), ones) * (1.0 / MLA_ROPE)
        out[:, P_KR:P_QG] = jnp.where(low, _rms_bwd_b(kr, rr, g_kr, d, dot), 0.0).astype(BF16)
        a_kr = _csum(d * kr * rr)
        a_q = jnp.zeros((1, HD), F32)

        def head_bwd(xh, g, d):
            rr = _rms_r(_seg_sum(xh * xh, ones), HD)
            dot = _seg_sum(d * g * xh, ones) * (1.0 / HD)
            return _rms_bwd_b(xh, rr, g, d, dot).astype(BF16), _csum(d * xh * rr)

        for h in range(HEADS):
            xh = p[:, P_QG + HD * h:P_QG + HD * (h + 1)]
            d = _rope_bwd(r[3][h] * GQA_SCALE, cav, sav, 64, 32)
            out[:, P_QG + HD * h:P_QG + HD * (h + 1)], inc = head_bwd(xh, g_q, d)
            a_q = a_q + inc
        a_k = jnp.zeros((1, HD), F32)
        grp = HEADS // GQA_KV
        for h in range(GQA_KV):
            xh = p[:, P_KG + HD * h:P_KG + HD * (h + 1)]
            d = r[4][grp * h]
            dv = r[5][grp * h]
            for e in range(1, grp):
                d = d + r[4][grp * h + e]
                dv = dv + r[5][grp * h + e]
            d = _rope_bwd(d, cav, sav, 64, 32)
            out[:, P_KG + HD * h:P_KG + HD * (h + 1)], inc = head_bwd(xh, g_k, d)
            out[:, P_VG + HD * h:P_VG + HD * (h + 1)] = dv.astype(BF16)
            a_k = a_k + inc
        return (a_ql, a_kvl, a_kr, a_q, a_k)

    accs = [_sds((1, Q_LORA), F32), _sds((1, KV_LORA), F32), _sds((1, LANES), F32),
            _sds((1, HD), F32), _sds((1, HD), F32)]
    return _rowwise(name, fn, [dcqn, dckvn, dkrd, dqg, dkg, dvg, proj, cm, sm, ca, sa], gains,
                    [_sds((S, P_END), BF16)], accs, tm=128, after=after)


def _mla_q_parts(qa, h, rp, ones):
    nope = qa[:, HD * h:HD * (h + 1)]
    mine = (_lane(rp.shape) >= 64) == bool(h % 2)
    rpm = jnp.where(mine, rp, 0.0)
    rr = _rms_r(_seg_sum(nope * nope + rpm * rpm, ones), MLA_QK)
    return nope, rpm, mine, rr


def _mla_prep(name, qa, kv, krd, tabs, gains):
    S = qa.shape[0]
    cm, sm = tabs

    def fn(r, c, o):
        qa_r, kv_r, krd_v, cmv, smv = r[0], r[1], r[2][...], r[3][...], r[4][...]
        g_n, g_r, g_kn = (x[...] for x in c)
        ones = _seg_ones(LANES)
        for h in range(HEADS):
            rp = qa_r[:, HEADS * HD + LANES * (h // 2):HEADS * HD + LANES * (h // 2 + 1)]
            nope, rpm, mine, rr = _mla_q_parts(qa_r, h, rp, ones)
            o[0][h, :, 0:HD] = (nope * (rr * MLA_SCALE) * g_n).astype(BF16)
            o[0][h, :, HD:2 * HD] = _rope(rpm * (rr * MLA_SCALE) * g_r, cmv, smv, 64, 32).astype(BF16)
            kn = kv_r[:, 2 * HD * h:2 * HD * h + HD]
            o[1][h, :, 0:HD] = (kn * _rms_r(_seg_sum(kn * kn, ones), HD) * g_kn).astype(BF16)
            o[1][h, :, HD:2 * HD] = krd_v
            o[2][h] = kv_r[:, 2 * HD * h + HD:2 * HD * (h + 1)].astype(BF16)
        return ()

    outs = [_sds((HEADS, S, 2 * HD), BF16), _sds((HEADS, S, 2 * HD), BF16), _sds((HEADS, S, HD), BF16)]
    return _rowwise(name, fn, [qa, kv, krd, cm, sm], gains, outs)


def _mla_prep_bwd(name, dq, dk, dv, qa, kv, tabs, gains):
    S = qa.shape[0]
    cm, sm = tabs

    def fn(r, c, o):
        dq_r, dk_r, dv_r, qa_r, kv_r, cmv, smv = r[0], r[1], r[2], r[3], r[4], r[5][...], r[6][...]
        g_n, g_r, g_kn = (x[...] for x in c)
        a_n = jnp.zeros((1, HD), F32)
        a_r = jnp.zeros((1, LANES), F32)
        a_kn = jnp.zeros((1, HD), F32)
        dkrd = jnp.zeros(cmv.shape, F32)
        drp = None
        ones = _seg_ones(LANES)
        for h in range(HEADS):
            rp = qa_r[:, HEADS * HD + LANES * (h // 2):HEADS * HD + LANES * (h // 2 + 1)]
            nope, rpm, mine, rr = _mla_q_parts(qa_r, h, rp, ones)
            dn = dq_r[h, :, 0:HD] * MLA_SCALE
            dr = _rope_bwd(jnp.where(mine, dq_r[h, :, HD:2 * HD] * MLA_SCALE, 0.0), cmv, smv, 64, 32)
            dot = _seg_sum(dn * g_n * nope + dr * g_r * rpm, ones) * (1.0 / MLA_QK)
            r3 = rr * rr * rr
            o[0][:, HD * h:HD * (h + 1)] = (rr * dn * g_n - nope * r3 * dot).astype(BF16)
            part = jnp.where(mine, rr * dr * g_r - rpm * r3 * dot, 0.0)
            drp = part if h % 2 == 0 else drp + part
            if h % 2 == 1:
                o[0][:, HEADS * HD + LANES * (h // 2):HEADS * HD + LANES * (h // 2 + 1)] = drp.astype(BF16)
            a_n = a_n + _csum(dn * nope * rr)
            a_r = a_r + _csum(dr * rpm * rr)
            kn = kv_r[:, 2 * HD * h:2 * HD * h + HD]
            rk = _rms_r(_seg_sum(kn * kn, ones), HD)
            dkn = dk_r[h, :, 0:HD]
            dotk = _seg_sum(dkn * g_kn * kn, ones) * (1.0 / HD)
            o[1][:, 2 * HD * h:2 * HD * h + HD] = _rms_bwd_b(kn, rk, g_kn, dkn, dotk).astype(BF16)
            o[1][:, 2 * HD * h + HD:2 * HD * (h + 1)] = dv_r[h].astype(BF16)
            a_kn = a_kn + _csum(dkn * kn * rk)
            dkrd = dkrd + dk_r[h, :, HD:2 * HD]
        o[2][...] = dkrd
        return (a_n, a_r, a_kn)

    outs = [_sds(qa.shape, BF16), _sds(kv.shape, BF16), _sds((S, LANES), F32)]
    accs = [_sds((1, HD), F32), _sds((1, LANES), F32), _sds((1, HD), F32)]
    return _rowwise(name, fn, [dq, dk, dv, qa, kv, cm, sm], gains, outs, accs, tm=128)


def _seg_ones(seg):
    r = lax.broadcasted_iota(jnp.int32, (LANES, LANES), 0) // seg
    c = lax.broadcasted_iota(jnp.int32, (LANES, LANES), 1) // seg
    return (r == c).astype(F32)


def _dot01(x, sel, dims):
    hi = x.astype(BF16)
    lo = (x - hi.astype(F32)).astype(BF16)
    sb = sel.astype(BF16)
    if dims is _NN:
        return (lax.dot_general(hi, sb, dims, preferred_element_type=F32)
                + lax.dot_general(lo, sb, dims, preferred_element_type=F32))
    return (lax.dot_general(sb, hi, dims, preferred_element_type=F32)
            + lax.dot_general(sb, lo, dims, preferred_element_type=F32))


def _seg_sum(x, ones):
    return _dot01(x, ones, _NN)


def _fold(x):
    acc = x[:, 0:LANES]
    for g in range(1, x.shape[1] // LANES):
        acc = acc + x[:, LANES * g:LANES * (g + 1)]
    return acc


def _wide(v, n):
    return jnp.concatenate([v] * n, axis=1) if n > 1 else v


def _rms_bwd_b(x, r, g, dy, dot_b):
    dyg = dy * g
    return r * dyg - x * (r * r * r) * dot_b


def _seg64_r(x, ones):
    return _rms_r(_seg_sum(x * x, ones), SWA_D)


def _swa_prep(name, qkv, tabs, gains):
    S = qkv.shape[0]
    nq, nk = SWA_HEADS * SWA_D, SWA_KV * SWA_D
    cs, ss = tabs

    def fn(r, c, o):
        x_r, csv, ssv = r[0], r[1][...], r[2][...]
        g_q, g_k = c[0][...], c[1][...]
        ones = _seg_ones(SWA_D)
        for g in range((nq + nk) // LANES):
            x = x_r[:, LANES * g:LANES * (g + 1)]
            rr = _seg64_r(x, ones)
            y = _rope(x * rr * (g_q if g < nq // LANES else g_k), csv, ssv, SWA_D, SWA_ROT // 2)
            if g < nq // LANES:
                o[0][:, LANES * g:LANES * (g + 1)] = (y * SWA_SCALE).astype(BF16)
            else:
                o[1][:, LANES * g - nq:LANES * (g + 1) - nq] = y.astype(BF16)
        o[2][...] = x_r[:, nq + nk:nq + 2 * nk].astype(BF16)
        return ()

    outs = [_sds((S, nq), BF16), _sds((S, nk), BF16), _sds((S, nk), BF16)]
    return _rowwise(name, fn, [qkv, cs, ss], gains, outs)


def _swa_prep_bwd(name, dq, dk, dv, qkv, tabs, gains):
    nq, nk = SWA_HEADS * SWA_D, SWA_KV * SWA_D
    cs, ss = tabs

    def fn(r, c, o):
        dq_r, dk_r, x_r, csv, ssv = r[0], r[1], r[3], r[4][...], r[5][...]
        g_q, g_k = c[0][...], c[1][...]
        acc = [jnp.zeros((1, LANES), F32), jnp.zeros((1, LANES), F32)]
        ones = _seg_ones(SWA_D)
        for g in range((nq + nk) // LANES):
            isq = g < nq // LANES
            x = x_r[:, LANES * g:LANES * (g + 1)]
            rr = _seg64_r(x, ones)
            d = dq_r[:, LANES * g:LANES * (g + 1)] * SWA_SCALE if isq else dk_r[:, LANES * g - nq:LANES * (g + 1) - nq]
            d = _rope_bwd(d, csv, ssv, SWA_D, SWA_ROT // 2)
            dyg = d * (g_q if isq else g_k)
            dot = _seg_sum(dyg * x, ones) * (1.0 / SWA_D)
            o[0][:, LANES * g:LANES * (g + 1)] = (rr * dyg - x * (rr * rr * rr) * dot).astype(BF16)
            acc[0 if isq else 1] = acc[0 if isq else 1] + _csum(d * x * rr)
        o[0][:, nq + nk:nq + 2 * nk] = r[2][...].astype(BF16)
        return tuple(acc)

    return _rowwise(name, fn, [dq, dk, dv, qkv, cs, ss], gains, [_sds(qkv.shape, BF16)],
                    [_sds((1, LANES), F32), _sds((1, LANES), F32)], tm=128)


def _delta(name, do, o_, after=()):
    S, C = do.shape

    def fn(r, c, o):
        for g in range(C // LANES):
            t = r[0][:, LANES * g:LANES * (g + 1)].astype(F32) * r[1][:, LANES * g:LANES * (g + 1)].astype(F32)
            o[0][g] = _rsum(t)
        return ()

    return _rowwise(name, fn, [do, o_], [], [_sds((C // LANES, S, 1), F32)], after=after)[0]


def _delta_t(name, do, o_, after=()):
    S, C = do.shape

    def fn(r, c, o):
        row = lax.broadcasted_iota(jnp.int32, (8, LANES), 0)
        sel = (row == _lane((8, LANES)) // SWA_D).astype(F32)
        for g in range(C // LANES):
            t = r[0][:, LANES * g:LANES * (g + 1)].astype(F32) * r[1][:, LANES * g:LANES * (g + 1)].astype(F32)
            both = _dot01(t, sel, _NT)
            o[0][2 * g] = both[0:1, :]
            o[0][2 * g + 1] = both[1:2, :]
        return ()

    return _rowwise(name, fn, [do, o_], [], [("T", _sds((2 * C // LANES, 1, S), F32))], after=after)[0]


_NT = (((1,), (1,)), ((), ()))
_NN = (((1,), (0,)), ((), ()))
_TN = (((0,), (0,)), ((), ()))


def _flash_fwd(name, q, k, v, tq=FWD_TILE[0], tk=FWD_TILE[1]):
    H, S, dk = q.shape
    G = H // k.shape[0]
    dv = v.shape[2]
    tq, tk = _tile(S, (tq, 256, 128)), _tile(S, (tk, 256, 128))
    nk = S // tk

    def body(q_ref, k_ref, v_ref, o_ref, lse_ref, m_s, l_s, acc_s):
        j = pl.program_id(2)

        @pl.when(j == 0)
        def _():
            m_s[...] = jnp.full_like(m_s, -jnp.inf)
            l_s[...] = jnp.zeros_like(l_s)
            acc_s[...] = jnp.zeros_like(acc_s)

        s = lax.dot_general(q_ref[...], k_ref[...], _NT, preferred_element_type=F32)
        m_new = jnp.maximum(m_s[...], jnp.max(s, axis=-1, keepdims=True))
        alpha = jnp.exp(m_s[...] - m_new)
        p = jnp.exp(s - m_new)
        l_s[...] = alpha * l_s[...] + _rsum(p)
        acc_s[...] = alpha * acc_s[...] + lax.dot_general(p.astype(BF16), v_ref[...], _NN,
                                                          preferred_element_type=F32)
        m_s[...] = m_new

        @pl.when(j == nk - 1)
        def _():
            o_ref[...] = (acc_s[...] / l_s[...]).astype(o_ref.dtype)
            lse_ref[...] = m_s[...] + jnp.log(l_s[...])

    return pl.pallas_call(
        body, name=name, grid=(H, S // tq, nk),
        in_specs=[pl.BlockSpec((None, tq, dk), lambda h, i, j: (h, i, 0)),
                  pl.BlockSpec((None, tk, dk), lambda h, i, j: (h // G, j, 0)),
                  pl.BlockSpec((None, tk, dv), lambda h, i, j: (h // G, j, 0))],
        out_specs=[pl.BlockSpec((tq, dv), lambda h, i, j: (i, h)),
                   pl.BlockSpec((None, tq, 1), lambda h, i, j: (h, i, 0))],
        out_shape=[_sds((S, H * dv), BF16), _sds((H, S, 1), F32)],
        scratch_shapes=[pltpu.VMEM((tq, 1), F32), pltpu.VMEM((tq, 1), F32), pltpu.VMEM((tq, dv), F32)],
        compiler_params=_params(("parallel", "parallel", "arbitrary")),
    )(q, k, v)


def _flash_bwd(name, q, k, v, do, head0, lse, delta, tq=BWD_TILE[0], tk=BWD_TILE[1]):
    H, S, dk = q.shape
    G = H // k.shape[0]
    dv = v.shape[2]
    tq, tk = _tile(S, (tq, 256, 128)), _tile(S, (tk, 256, 128))
    nq = S // tq

    def body(q_ref, k_ref, v_ref, do_ref, lse_ref, dl_ref, dq_ref, dk_ref, dv_ref, dk_s, dv_s):
        j, i = pl.program_id(1), pl.program_id(2)

        @pl.when(i == 0)
        def _():
            dk_s[...] = jnp.zeros_like(dk_s)
            dv_s[...] = jnp.zeros_like(dv_s)

        qv, kv_, dov = q_ref[...], k_ref[...], do_ref[...]
        s = lax.dot_general(qv, kv_, _NT, preferred_element_type=F32)
        p = jnp.exp(s - lse_ref[...])
        dp = lax.dot_general(dov, v_ref[...], _NT, preferred_element_type=F32)
        ds = (p * (dp - dl_ref[...])).astype(BF16)
        dv_s[...] += lax.dot_general(p.astype(BF16), dov, _TN, preferred_element_type=F32)
        dk_s[...] += lax.dot_general(ds, qv, _TN, preferred_element_type=F32)
        dqi = lax.dot_general(ds, kv_, _NN, preferred_element_type=F32)
        rows = pl.ds(pl.multiple_of(i * tq, tq), tq)

        @pl.when(j == 0)
        def _():
            dq_ref[rows, :] = dqi

        @pl.when(j > 0)
        def _():
            dq_ref[rows, :] += dqi

        @pl.when(i == nq - 1)
        def _():
            dk_ref[...] = dk_s[...]
            dv_ref[...] = dv_s[...]

    return pl.pallas_call(
        body, name=name, grid=(H, S // tk, nq),
        in_specs=[pl.BlockSpec((None, tq, dk), lambda h, j, i: (h, i, 0)),
                  pl.BlockSpec((None, tk, dk), lambda h, j, i: (h // G, j, 0)),
                  pl.BlockSpec((None, tk, dv), lambda h, j, i: (h // G, j, 0)),
                  pl.BlockSpec((tq, dv), lambda h, j, i: (i, head0 + h)),
                  pl.BlockSpec((None, tq, 1), lambda h, j, i: (h, i, 0)),
                  pl.BlockSpec((None, tq, 1), lambda h, j, i: (head0 + h, i, 0))],
        out_specs=[pl.BlockSpec((None, S, dk), lambda h, j, i: (h, 0, 0)),
                   pl.BlockSpec((None, tk, dk), lambda h, j, i: (h, j, 0)),
                   pl.BlockSpec((None, tk, dv), lambda h, j, i: (h, j, 0))],
        out_shape=[_sds((H, S, dk), F32), _sds((H, S, dk), F32), _sds((H, S, dv), F32)],
        scratch_shapes=[pltpu.VMEM((tk, dk), F32), pltpu.VMEM((tk, dv), F32)],
        compiler_params=_params(("parallel", "arbitrary", "arbitrary")),
    )(q, k, v, do, lse, delta)


def _swa_place(ref128, h):
    e, t = h % 2, (h // (SWA_HEADS // SWA_KV)) % 2
    x = ref128.astype(F32)
    if e != t:
        x = pltpu.roll(x, 64, 1)
    return jnp.where((_lane(x.shape) >= 64) == bool(t), x, 0.0).astype(BF16)


def _swa_unplace(y, h):
    e, t = h % 2, (h // (SWA_HEADS // SWA_KV)) % 2
    return pltpu.roll(y, 64, 1) if e != t else y


def _swa_specs(width, nb):
    prev = pl.BlockSpec((SWA_BLOCK, width), lambda i: (jnp.maximum(i - 1, 0), 0))
    cur = pl.BlockSpec((SWA_BLOCK, width), lambda i: (i, 0))
    nxt = pl.BlockSpec((SWA_BLOCK, width), lambda i: (jnp.minimum(i + 1, nb - 1), 0))
    return [prev, cur, nxt]


def _swa_bias_t(i, S):
    shape = (3 * SWA_BLOCK, SWA_BLOCK)
    kpos = (i - 1) * SWA_BLOCK + lax.broadcasted_iota(jnp.int32, shape, 0)
    qpos = i * SWA_BLOCK + lax.broadcasted_iota(jnp.int32, shape, 1)
    ok = (jnp.abs(qpos - kpos) <= SWA_WINDOW) & (kpos >= 0) & (kpos < S)
    return jnp.where(ok, 0.0, -jnp.inf)


def _add_blocks(x, blocks):
    n = x.shape[1] // LANES
    return jnp.concatenate([x[:, LANES * b:LANES * (b + 1)] + blocks[b % len(blocks)] for b in range(n)], axis=1)


def _swa_stack(refs, heads):
    return jnp.concatenate([_swa_place(r[:, LANES * (h // 2):LANES * (h // 2 + 1)], h) for h in heads for r in refs],
                           axis=0)


def _swa_unstack_t(ot, j, out_ref):
    grp = SWA_HEADS // SWA_KV
    for pair in range(grp // 2):
        h = grp * j + 2 * pair
        a = _swa_unplace(ot[:, SWA_BLOCK * 2 * pair:SWA_BLOCK * (2 * pair + 1)].T, h)
        b = _swa_unplace(ot[:, SWA_BLOCK * (2 * pair + 1):SWA_BLOCK * (2 * pair + 2)].T, h + 1)
        out_ref[:, LANES * (h // 2):LANES * (h // 2 + 1)] = jnp.where(_lane(a.shape) < 64, a, b).astype(out_ref.dtype)


def _swa_fwd(name, q, k, v, sink):
    S = q.shape[0]
    nb = S // SWA_BLOCK
    grp = SWA_HEADS // SWA_KV
    smem = pl.BlockSpec(memory_space=pltpu.SMEM)

    def body(sink_ref, q_ref, kp, kc, kn, vp, vc, vn, o_ref, lse_ref):
        i = pl.program_id(0)
        bias = [_swa_bias_t(i, S)]
        kcat = [jnp.concatenate([r[:, LANES * u:LANES * (u + 1)] for r in (kp, kc, kn)], axis=0) for u in range(2)]
        vcat = [jnp.concatenate([r[:, LANES * u:LANES * (u + 1)] for r in (vp, vc, vn)], axis=0) for u in range(2)]
        for j in range(SWA_KV):
            heads = range(grp * j, grp * (j + 1))
            xq = _swa_stack([q_ref], heads)
            sk = jnp.concatenate([jnp.full((1, SWA_BLOCK), sink_ref[0, h], F32) for h in heads], axis=1)
            st = lax.dot_general(kcat[j // 2], xq, _NT, preferred_element_type=F32)
            st = _add_blocks(st, bias)
            m = jnp.maximum(jnp.max(st, axis=0, keepdims=True), sk)
            pt = jnp.exp(st - m)
            den = jnp.sum(pt, axis=0, keepdims=True) + jnp.exp(sk - m)
            ot = lax.dot_general(vcat[j // 2], pt.astype(BF16), _TN, preferred_element_type=F32) * (1.0 / den)
            lse = m + jnp.log(den)
            for e, h in enumerate(heads):
                lse_ref[h] = lse[:, SWA_BLOCK * e:SWA_BLOCK * (e + 1)]
            _swa_unstack_t(ot, j, o_ref)

    return pl.pallas_call(
        body, name=name, grid=(nb,),
        in_specs=[smem, pl.BlockSpec((SWA_BLOCK, q.shape[1]), lambda i: (i, 0))]
        + _swa_specs(k.shape[1], nb) + _swa_specs(v.shape[1], nb),
        out_specs=[pl.BlockSpec((SWA_BLOCK, q.shape[1]), lambda i: (i, 0)),
                   pl.BlockSpec((SWA_HEADS, 1, SWA_BLOCK), lambda i: (0, 0, i))],
        out_shape=[_sds(q.shape, BF16), _sds((SWA_HEADS, 1, S), F32)],
        compiler_params=_params(("parallel",)),
    )(sink, q, k, k, k, v, v, v)


def _swa_bwd_q(name, q, k, v, do, lse_t, delta_t, sink):
    S = q.shape[0]
    nb = S // SWA_BLOCK
    grp = SWA_HEADS // SWA_KV
    smem = pl.BlockSpec(memory_space=pltpu.SMEM)
    row = pl.BlockSpec((SWA_BLOCK, q.shape[1]), lambda i: (i, 0))
    hrow = pl.BlockSpec((SWA_HEADS, 1, SWA_BLOCK), lambda i: (0, 0, i))

    def body(sink_ref, q_ref, do_ref, lse_ref, dl_ref, kp, kc, kn, vp, vc, vn, dq_ref, ds_ref):
        i = pl.program_id(0)

        @pl.when(i == 0)
        def _():
            ds_ref[...] = jnp.zeros_like(ds_ref)

        bias = [_swa_bias_t(i, S)]
        kcat = [jnp.concatenate([r[:, LANES * u:LANES * (u + 1)] for r in (kp, kc, kn)], axis=0) for u in range(2)]
        vcat = [jnp.concatenate([r[:, LANES * u:LANES * (u + 1)] for r in (vp, vc, vn)], axis=0) for u in range(2)]
        for j in range(SWA_KV):
            heads = range(grp * j, grp * (j + 1))
            xq = _swa_stack([q_ref], heads)
            xdo = _swa_stack([do_ref], heads)
            lse_r = jnp.concatenate([lse_ref[h] for h in heads], axis=1)
            dl_r = jnp.concatenate([dl_ref[h] for h in heads], axis=1)
            st = lax.dot_general(kcat[j // 2], xq, _NT, preferred_element_type=F32)
            pt = jnp.exp(_add_blocks(st, bias) - lse_r)
            dpt = lax.dot_general(vcat[j // 2], xdo, _NT, preferred_element_type=F32)
            dst = (pt * (dpt - dl_r)).astype(BF16)
            _swa_unstack_t(lax.dot_general(kcat[j // 2], dst, _TN, preferred_element_type=F32), j, dq_ref)
            for h in heads:
                dsink = -_rsum(jnp.exp(sink_ref[0, h] - lse_ref[h]) * dl_ref[h])
                ds_ref[h:h + 1, :] += jnp.broadcast_to(dsink, (1, LANES))

    return pl.pallas_call(
        body, name=name, grid=(nb,),
        in_specs=[smem, row, row, hrow, hrow] + _swa_specs(k.shape[1], nb) + _swa_specs(v.shape[1], nb),
        out_specs=[row, pl.BlockSpec((SWA_HEADS, LANES), lambda i: (0, 0))],
        out_shape=[_sds(q.shape, F32), _sds((SWA_HEADS, LANES), F32)],
        compiler_params=_params(("arbitrary",)),
    )(sink, q, do, lse_t, delta_t, k, k, k, v, v, v)


def _swa_bwd_kv(name, q, k, v, do, lse_t, delta_t):
    S = q.shape[0]
    nb = S // SWA_BLOCK
    grp = SWA_HEADS // SWA_KV
    krow = pl.BlockSpec((SWA_BLOCK, k.shape[1]), lambda i: (i, 0))

    def stat3():
        return [pl.BlockSpec((SWA_HEADS, 1, SWA_BLOCK), lambda i: (0, 0, jnp.maximum(i - 1, 0))),
                pl.BlockSpec((SWA_HEADS, 1, SWA_BLOCK), lambda i: (0, 0, i)),
                pl.BlockSpec((SWA_HEADS, 1, SWA_BLOCK), lambda i: (0, 0, jnp.minimum(i + 1, nb - 1)))]

    def body(qp, qc, qn, dop, doc, don, lp, lc, ln, dp_, dc_, dn_, k_ref, v_ref, dk_ref, dv_ref):
        j = pl.program_id(0)
        nh = 2 * grp
        kpos = j * SWA_BLOCK + lax.broadcasted_iota(jnp.int32, (SWA_BLOCK, SWA_BLOCK), 0)
        bias = []
        for b in range(3):
            qpos = (j - 1 + b) * SWA_BLOCK + lax.broadcasted_iota(jnp.int32, (SWA_BLOCK, SWA_BLOCK), 1)
            ok = (jnp.abs(qpos - kpos) <= SWA_WINDOW) & (qpos >= 0) & (qpos < S)
            bias.append(jnp.where(ok, 0.0, -jnp.inf))
        for u in range(SWA_KV // 2):
            heads = range(nh * u, nh * (u + 1))
            kc = k_ref[:, LANES * u:LANES * (u + 1)]
            vc = v_ref[:, LANES * u:LANES * (u + 1)]
            xq = _swa_stack([qp, qc, qn], heads)
            xdo = _swa_stack([dop, doc, don], heads)
            lse_r = jnp.concatenate([r[h] for h in heads for r in (lp, lc, ln)], axis=1)
            dl_r = jnp.concatenate([r[h] for h in heads for r in (dp_, dc_, dn_)], axis=1)
            st = lax.dot_general(kc, xq, _NT, preferred_element_type=F32)
            pt = jnp.exp(_add_blocks(st, bias) - lse_r)
            dpt = lax.dot_general(vc, xdo, _NT, preferred_element_type=F32)
            dst = (pt * (dpt - dl_r)).astype(BF16)
            dv_ref[:, LANES * u:LANES * (u + 1)] = lax.dot_general(pt.astype(BF16), xdo, _NN, preferred_element_type=F32)
            dk_ref[:, LANES * u:LANES * (u + 1)] = lax.dot_general(dst, xq, _NN, preferred_element_type=F32)

    return pl.pallas_call(
        body, name=name, grid=(nb,),
        in_specs=_swa_specs(q.shape[1], nb) + _swa_specs(do.shape[1], nb) + stat3() + stat3() + [krow, krow],
        out_specs=[krow, krow],
        out_shape=[_sds(k.shape, F32), _sds(v.shape, F32)],
        compiler_params=_params(("parallel",)),
    )(q, q, q, do, do, do, lse_t, lse_t, lse_t, delta_t, delta_t, delta_t, k, v)


def _peer_of(x, y, c, kk):
    return (x ^ ((kk >> 2) & 1), y ^ ((kk >> 1) & 1), c ^ (kk & 1))


def _own_slot(a, scatter):
    me = 4 * lax.axis_index("x") + 2 * lax.axis_index("y") + lax.axis_index("c")
    shape = a.shape if scatter else (N_DEV,) + a.shape
    own = lax.dynamic_slice_in_dim(a, me, 1, 0) if scatter else a[None]
    return lax.dynamic_update_slice_in_dim(lax.empty(shape, a.dtype), own, me, 0)


def _exchange_start(name, tensors):
    n = len(tensors)
    hbm = pl.BlockSpec(memory_space=pltpu.HBM)
    sem = pl.BlockSpec(memory_space=pltpu.SEMAPHORE)
    srcs = [pltpu.with_memory_space_constraint(a, pltpu.HBM) for a, _ in tensors]
    lands = [pltpu.with_memory_space_constraint(_own_slot(a, sc), pltpu.HBM) for a, sc in tensors]

    def body(*refs):
        ins, dst = refs[:n], refs[n:2 * n]
        send, recv, token = refs[2 * n], refs[2 * n + 1], refs[4 * n + 2]
        x, y, c = lax.axis_index("x"), lax.axis_index("y"), lax.axis_index("c")
        me = 4 * x + 2 * y + c
        for t in range(n):
            for kk in range(1, N_DEV):
                px, py, pc = _peer_of(x, y, c, kk)
                src = ins[t].at[4 * px + 2 * py + pc] if tensors[t][1] else ins[t]
                k1 = t * (N_DEV - 1) + kk - 1
                pltpu.make_async_remote_copy(src_ref=src, dst_ref=dst[t].at[me], send_sem=send.at[k1],
                                             recv_sem=recv.at[k1], device_id=(px, py, pc),
                                             device_id_type=pl.DeviceIdType.MESH).start()
        token[...] = jnp.zeros_like(token)

    out = pl.pallas_call(
        body, name=name,
        in_specs=[hbm] * (2 * n),
        out_specs=[sem, sem] + [hbm] * (2 * n) + [pl.BlockSpec(memory_space=pltpu.VMEM)],
        out_shape=[pltpu.SemaphoreType.DMA((n * (N_DEV - 1),)), pltpu.SemaphoreType.DMA((n * (N_DEV - 1),))]
        + [pltpu.HBM(a.shape, a.dtype) for a in srcs] + [pltpu.HBM(a.shape, a.dtype) for a in lands]
        + [_sds((8, LANES), F32)],
        input_output_aliases={i: i + 2 for i in range(2 * n)},
        compiler_params=pltpu.CompilerParams(has_side_effects=pltpu.SideEffectType.DATAFLOW_SIDE_EFFECTING),
    )(*srcs, *lands)
    return (out[0], out[1], out[2:2 + n], out[2 + n:2 + 2 * n], [sc for _, sc in tensors]), out[2 + 2 * n]


def _exchange_wait(name, started, after):
    send_s, recv_s, srcs, lands, flags = started
    n = len(srcs)
    hbm = pl.BlockSpec(memory_space=pltpu.HBM)
    sem = pl.BlockSpec(memory_space=pltpu.SEMAPHORE)

    def body(*refs):
        ins, dst, send, recv = refs[:n], refs[n:2 * n], refs[2 * n], refs[2 * n + 1]
        x, y, c = lax.axis_index("x"), lax.axis_index("y"), lax.axis_index("c")
        me = 4 * x + 2 * y + c
        for t in range(n):
            for kk in range(1, N_DEV):
                px, py, pc = _peer_of(x, y, c, kk)
                src = ins[t].at[me] if flags[t] else ins[t]
                k1 = t * (N_DEV - 1) + kk - 1
                cp = pltpu.make_async_remote_copy(src_ref=src, dst_ref=dst[t].at[4 * px + 2 * py + pc],
                                                  send_sem=send.at[k1], recv_sem=recv.at[k1],
                                                  device_id=(px, py, pc), device_id_type=pl.DeviceIdType.MESH)
                cp.wait_send()
                cp.wait_recv()

    out = pl.pallas_call(
        body, name=name,
        in_specs=[hbm] * (2 * n) + [sem, sem, pl.BlockSpec(memory_space=pl.ANY)],
        out_specs=[hbm] * (2 * n),
        out_shape=[pltpu.HBM(a.shape, a.dtype) for a in srcs] + [pltpu.HBM(a.shape, a.dtype) for a in lands],
        input_output_aliases={i: i for i in range(2 * n)},
        compiler_params=pltpu.CompilerParams(has_side_effects=pltpu.SideEffectType.DATAFLOW_SIDE_EFFECTING),
    )(*srcs, *lands, send_s, recv_s, after)
    return out[n:]


def _sum8(name, parts):
    _, R, C = parts.shape

    def fn(r, c, o):
        acc = r[0][0].astype(F32)
        for s in range(1, N_DEV):
            acc = acc + r[0][s].astype(F32)
        o[0][...] = acc
        return ()

    tm = R if R % 256 else 256
    return _rowwise(name, fn, [parts], [], [_sds((R, C), F32)], tm=tm)[0]


def _adamw(name, w, g, m, v):
    bc1 = 1.0 - ADAM_B1 ** ADAM_STEP
    bc2 = 1.0 - ADAM_B2 ** ADAM_STEP

    def fn(r, c, o):
        wv, gv, mv, vv = (x[...] for x in r)
        mn = ADAM_B1 * mv + (1.0 - ADAM_B1) * gv
        vn = ADAM_B2 * vv + (1.0 - ADAM_B2) * (gv * gv)
        o[0][...] = -ADAM_LR * ((mn / bc1) / (jnp.sqrt(vn / bc2) + ADAM_EPS) + ADAM_WD * wv)
        o[1][...] = mn
        o[2][...] = vn
        return ()

    tm = max(8, min(512, 1 << ((16 << 20) // (56 * w.shape[1])).bit_length() - 1))
    return _rowwise(name, fn, [w, g, m, v], [], [_sds(w.shape, F32)] * 3, tm=tm)


def _rope_cs(pos, dim, theta):
    inv = jnp.float32(theta) ** (-jnp.arange(0, dim, 2, dtype=jnp.float32) / dim)
    ang = pos.astype(jnp.float32)[:, None] * inv[None, :]
    return jnp.cos(ang), jnp.sin(ang)


def _tables(S):
    pos = jnp.arange(S)
    c, s = _rope_cs(pos, MLA_ROPE, ROPE_THETA)
    mla = (jnp.concatenate([c, c, c, c], 1), jnp.concatenate([-s, s, -s, s], 1))
    rc, rs = _rope_cs(pos // GRID_W, HD // 2, AXIAL_THETA)
    cc, cs = _rope_cs(pos % GRID_W, HD // 2, AXIAL_THETA)
    axial = (jnp.concatenate([rc, rc, cc, cc], 1), jnp.concatenate([-rs, rs, -cs, cs], 1))
    c, s = _rope_cs(pos, SWA_ROT, ROPE_THETA)
    one, zero = jnp.ones((S, SWA_D - SWA_ROT), F32), jnp.zeros((S, SWA_D - SWA_ROT), F32)
    swa = (jnp.concatenate([c, c, one, c, c, one], 1), jnp.concatenate([-s, s, zero, -s, s, zero], 1))
    return mla, axial, swa


_WEIGHTS = ['even_norm', 'even_w_in', 'mla_q_lat_norm', 'mla_kv_lat_norm', 'mla_w_uq', 'mla_w_ukv', 'mla_q_norm',
            'mla_k_nope_norm', 'mla_k_rope_norm', 'gqa_q_norm', 'gqa_k_norm', 'even_w_out', 'odd_norm', 'odd_w_qkv',
            'swa_q_norm', 'swa_k_norm', 'swa_sink', 'odd_w_out', 'mlp_norm', 'mlp_w_up', 'mlp_w_down']
_SMALL = ['even_norm', 'mla_q_lat_norm', 'mla_kv_lat_norm', 'mla_q_norm', 'mla_k_nope_norm', 'mla_k_rope_norm',
          'gqa_q_norm', 'gqa_k_norm', 'swa_q_norm', 'swa_k_norm', 'swa_sink', 'mlp_norm']


def _relu2(acc):
    rl = jnp.maximum(acc, 0.0)
    return rl * rl, rl


def _mul2(acc, rl):
    return (acc * (2.0 * rl.astype(F32)),)


def _add(acc, res):
    return (acc + res,)


def _step(x, tgt, w, m, v):
    S, D = x.shape
    nw = len(_WEIGHTS)
    tab_mla, tab_ax, tab_swa = _tables(S)
    bf = lambda a: a.astype(BF16)

    w_up_s, w_dn_s = w['mlp_w_up'], w['mlp_w_down']
    first, tok0 = _exchange_start("gather_first_start", [
        (bf(w['even_w_in'][0].T), False), (bf(w['mla_w_uq'][0].T), False), (bf(w['mla_w_ukv'][0].T), False)])
    h0 = _rmsnorm("even_norm", x, w['even_norm'], after=[tok0])
    gathered = _exchange_wait("gather_first_wait", first, h0)
    zero = jnp.minimum(jnp.abs(gathered[2][0, 0:1, 0:1].astype(F32)), 0.0)
    later, tok = _exchange_start("gather_rest_start", [
        (bf(w['even_w_out'][0]), False), (bf(w['odd_w_qkv'][0].T), False), (bf(w['odd_w_out'][0]), False),
        (bf(w_up_s[0].T), False), (bf(w_up_s[1].T), False), (bf(w_dn_s[0]), False), (bf(w_dn_s[1]), False),
        (w['odd_norm'] + zero, False)])
    flat = lambda a: a.reshape((a.shape[0] * a.shape[1],) + a.shape[2:])
    win_t, wuq_g, wukv_t = [flat(a) for a in gathered]
    win_t = jnp.concatenate([win_t[:KR_END], jnp.zeros((LANES - MLA_ROPE, D), BF16), win_t[KR_END:]], 0)
    wuq_g = gathered[1]
    wuq_t = jnp.concatenate([wuq_g[:, :HD].reshape(HEADS * HD, Q_LORA),
                             wuq_g[:, HD:].reshape(HEADS * MLA_ROPE, Q_LORA)], 0)

    z64 = jnp.zeros((1, 64), F32)
    qn = w['mla_q_norm']
    g_even = [w['mla_q_lat_norm'], w['mla_kv_lat_norm'], jnp.concatenate([w['mla_k_rope_norm'], z64], 1),
              w['gqa_q_norm'], w['gqa_k_norm']]
    g_mla = [qn[:, :HD], jnp.concatenate([qn[:, HD:], qn[:, HD:]], 1), w['mla_k_nope_norm']]
    g_swa = [jnp.concatenate([w['swa_q_norm']] * 2, 1), jnp.concatenate([w['swa_k_norm']] * 2, 1)]

    def loss_head(acc, res, t):
        e = acc + res - t
        return e * (1.0 / D), _csum(_rsum(e * e))

    def mlp_fwd(l, xin, last=False):
        hn = _rmsnorm(f"mlp{l}_norm", xin, w['mlp_norm'][l:l + 1])
        a, rl = _mm(f"mlp{l}_up", hn, wup_t[l], "nt", (BF16, BF16), epilogue=_relu2)
        if last:
            xout = _mm(f"mlp{l}_down", a, wdn[l], "nn", (F32,), epilogue=loss_head, extras=(xin, tgt),
                       sums=[_sds((1, 1), F32)])
        else:
            xout = _mm(f"mlp{l}_down", a, wdn[l], "nn", (F32,), epilogue=_add, extras=(xin,))
        return xout, (hn, a, rl)

    proj = _mm("even_in", h0, win_t, "nt", (F32,), tm=512, tn=P_END)
    cqn, ckvn, krd, qg, kg, vg = _even_prep("even_prep", proj, tab_mla + tab_ax, g_even, after=[tok])
    qa = _mm("mla_uq", cqn, wuq_t, "nt", (F32,))
    kv = _mm("mla_ukv", ckvn, wukv_t, "nt", (F32,))
    q_a, k_a, v_a = _mla_prep("mla_prep", qa, kv, krd, tab_mla, g_mla)
    o_a, lse_a = _flash_fwd("mla_attn", q_a, k_a, v_a)
    o_g, lse_g = _flash_fwd("gqa_attn", qg, kg, vg)
    merged = jnp.concatenate([o_a, o_g], 1)
    rest = _exchange_wait("gather_rest_wait", later, merged)
    wout_e, wqkv_t, wout_o, wup0_t, wup1_t, wdn0, wdn1 = [flat(a) for a in rest[:7]]
    odd_norm = rest[7].reshape(1, D)
    wup_t, wdn = (wup0_t, wup1_t), (wdn0, wdn1)
    x1 = _mm("even_out", merged, wout_e, "nn", (F32,), epilogue=_add, extras=(x,))
    x2, mlp0 = mlp_fwd(0, x1)

    h1 = _rmsnorm("odd_norm", x2, odd_norm)
    qkv = _mm("odd_qkv", h1, wqkv_t, "nt", (F32,))
    q_s, k_s, v_s = _swa_prep("swa_prep", qkv, tab_swa, g_swa)
    o_s, lse_s = _swa_fwd("swa_attn", q_s, k_s, v_s, w['swa_sink'])
    x3 = _mm("odd_out", o_s, wout_o, "nn", (F32,), epilogue=_add, extras=(x2,))
    (dy, loss_acc), mlp1 = mlp_fwd(1, x3, last=True)
    loss = lax.psum(0.5 / D * loss_acc[0, 0], ("x", "y", "c"))

    gsm = {}

    def mlp_bwd(l, dout, xin, saved):
        hn, a, rl = saved
        du = _mm(f"mlp{l}_dact", dout, wdn[l], "nt", (BF16,), epilogue=_mul2, extras=(rl,))
        g_dn = _mm(f"mlp{l}_gdown", a, dout, "tn", (BF16,))
        g_up = _mm(f"mlp{l}_gup", du, hn, "tn", (BF16,))
        dhn = _mm(f"mlp{l}_dnorm", du, wup_t[l], "nn", (F32,))
        din, g_n = _rmsnorm_bwd(f"mlp{l}_norm_bwd", dout, xin, dhn, w['mlp_norm'][l:l + 1])
        return din, g_dn, g_up, g_n

    dx3, g_dn1, g_up1, g_mn1 = mlp_bwd(1, dy, x3, mlp1)
    split = lambda a: a.reshape((N_DEV, a.shape[0] // N_DEV) + a.shape[1:])
    sc1, tok1 = _exchange_start("scatter_mlp1_start", [(split(g_up1), True), (split(g_dn1), True)])

    g_wout_o = _mm("odd_gout", o_s, dx3, "tn", (BF16,))
    do_s = _mm("odd_dattn", dx3, wout_o, "nt", (BF16,))
    dl_s = _delta_t("swa_delta", do_s, o_s, after=[tok1])
    dq_s, dsink = _swa_bwd_q("swa_bwd_q", q_s, k_s, v_s, do_s, lse_s, dl_s, w['swa_sink'])
    dk_s, dv_s = _swa_bwd_kv("swa_bwd_kv", q_s, k_s, v_s, do_s, lse_s, dl_s)
    dqkv, g_sq, g_sk = _swa_prep_bwd("swa_prep_bwd", dq_s, dk_s, dv_s, qkv, tab_swa, g_swa)
    g_wqkv = _mm("odd_gqkv", dqkv, h1, "tn", (BF16,))
    dh1 = _mm("odd_dnorm", dqkv, wqkv_t, "nn", (F32,), tk=dqkv.shape[1])
    dx2, g_on = _rmsnorm_bwd("odd_norm_bwd", dx3, x2, dh1, odd_norm)
    gsm['swa_q_norm'] = g_sq[:, :64] + g_sq[:, 64:]
    gsm['swa_k_norm'] = g_sk[:, :64] + g_sk[:, 64:]
    gsm['swa_sink'] = dsink[:, 0].reshape(1, SWA_HEADS)

    dx1, g_dn0, g_up0, g_mn0 = mlp_bwd(0, dx2, x1, mlp0)
    sc2, tok2 = _exchange_start("scatter_mid_start", [(split(g_wout_o), True), (split(g_wqkv), True),
                                                      (split(g_up0), True), (split(g_dn0), True)])
    gsm['mlp_norm'] = jnp.concatenate([g_mn0, g_mn1], 0)

    g_wout_e = _mm("even_gout", merged, dx1, "tn", (BF16,))
    dmerged = _mm("even_dattn", dx1, wout_e, "nt", (BF16,))
    dl_e = _delta("even_delta", dmerged, merged, after=[tok2])
    dq_a, dk_a, dv_a = _flash_bwd("mla_attn_bwd", q_a, k_a, v_a, dmerged, 0, lse_a, dl_e)
    dq_g, dk_g, dv_g = _flash_bwd("gqa_attn_bwd", qg, kg, vg, dmerged, HEADS, lse_g, dl_e)
    dqa, dkv, dkrd, g_qnn, g_qnr, g_kn = _mla_prep_bwd("mla_prep_bwd", dq_a, dk_a, dv_a, qa, kv, tab_mla, g_mla)
    g_wuq = _mm("mla_guq", dqa, cqn, "tn", (BF16,))
    dcqn = _mm("mla_dq_lat", dqa, wuq_t, "nn", (F32,))
    g_wukv = _mm("mla_gukv", dkv, ckvn, "tn", (BF16,))
    dckvn = _mm("mla_dkv_lat", dkv, wukv_t, "nn", (F32,))
    g_wuq = jnp.concatenate([g_wuq[:HEADS * HD].reshape(HEADS, HD, Q_LORA),
                             g_wuq[HEADS * HD:].reshape(HEADS, MLA_ROPE, Q_LORA)], 1)
    sc3, tok_e = _exchange_start("scatter_even_start", [(g_wuq, True), (split(g_wukv), True), (split(g_wout_e), True)])
    dproj, g_ql, g_kvl, g_kr, g_gq, g_gk = _even_prep_bwd("even_prep_bwd", dcqn, dckvn, dkrd, dq_g, dk_g, dv_g,
                                                          proj, tab_mla + tab_ax, g_even, after=[tok_e])
    g_win = _mm("even_gin", dproj, h0, "tn", (BF16,), tm=P_END, tk=1024)
    dh0 = _mm("even_dnorm", dproj, win_t, "nn", (F32,), tk=P_END)
    grad_x, g_en = _rmsnorm_bwd("even_norm_bwd", dx1, x, dh0, w['even_norm'])
    gsm.update(even_norm=g_en, mla_q_lat_norm=g_ql, mla_kv_lat_norm=g_kvl,
               mla_q_norm=jnp.concatenate([g_qnn, g_qnr[:, :64] + g_qnr[:, 64:]], 1), mla_k_nope_norm=g_kn,
               mla_k_rope_norm=g_kr[:, :64], gqa_q_norm=g_gq, gqa_k_norm=g_gk)

    g_win = jnp.concatenate([g_win[:KR_END], g_win[P_QG:]], 0)
    small_sizes = [w[n].size for n in _SMALL] + [D]
    small_vec = jnp.concatenate([gsm[n].reshape(1, -1) for n in _SMALL] + [g_on], 1)
    pad = (-small_vec.shape[1]) % LANES
    small_vec = jnp.pad(small_vec, ((0, 0), (0, pad)))
    last_st, tok3 = _exchange_start("scatter_last_start", [(split(g_win), True), (small_vec, False)])
    p_up1, p_dn1 = _exchange_wait("scatter_mlp1_wait", sc1, tok3)
    p_wout_o, p_wqkv, p_up0, p_dn0 = _exchange_wait("scatter_mid_wait", sc2, tok3)
    p_wuq, p_wukv, p_wout_e = _exchange_wait("scatter_even_wait", sc3, tok3)
    parts = dict(odd_w_qkv=p_wqkv, odd_w_out=p_wout_o, up0=p_up0, up1=p_up1, dn0=p_dn0, dn1=p_dn1,
                 mla_w_uq=p_wuq, mla_w_ukv=p_wukv, even_w_out=p_wout_e)
    red = {n: _sum8("sum_" + n, p) for n, p in parts.items()}
    grads = {
        'odd_w_qkv': red['odd_w_qkv'].T[None], 'odd_w_out': red['odd_w_out'][None],
        'mlp_w_up': jnp.stack([red['up0'].T, red['up1'].T]), 'mlp_w_down': jnp.stack([red['dn0'], red['dn1']]),
        'mla_w_uq': red['mla_w_uq'].T[None], 'mla_w_ukv': red['mla_w_ukv'].T[None],
        'even_w_out': red['even_w_out'][None],
    }
    delta, new_m, new_v = {}, {}, {}

    def apply(names):
        for n in names:
            shp = w[n].shape
            two = lambda a: a.reshape(shp[0] * shp[1], shp[2])
            d_, m_, v_ = _adamw("adamw_" + n, two(w[n]), two(grads[n]), two(m[n]), two(v[n]))
            delta[n], new_m[n], new_v[n] = d_.reshape(shp), m_.reshape(shp), v_.reshape(shp)

    apply(['mla_w_uq', 'mla_w_ukv', 'even_w_out', 'odd_w_qkv', 'odd_w_out', 'mlp_w_up', 'mlp_w_down'])
    last = _exchange_wait("scatter_last_wait", last_st, new_v['mlp_w_down'])
    small_g = _sum8("sum_small", last[1])
    grads['even_w_in'] = _sum8("sum_even_w_in", last[0]).T[None]
    apply(['even_w_in'])
    off = 0
    for n, sz in zip(_SMALL + ['odd_norm_full'], small_sizes):
        seg = small_g[:, off:off + sz]
        off += sz
        if n == 'odd_norm_full':
            me = 4 * lax.axis_index("x") + 2 * lax.axis_index("y") + lax.axis_index("c")
            grads['odd_norm'] = lax.dynamic_slice(seg, (0, me * (D // N_DEV)), (1, D // N_DEV))
        else:
            grads[n] = seg.reshape(w[n].shape)

    sm_names = _SMALL + ['odd_norm']
    pack = lambda d: jnp.concatenate([d[n].reshape(1, -1) for n in sm_names], 1)
    pw, pg, pm, pv = pack(w), pack(grads), pack(m), pack(v)
    padw = (-pw.shape[1]) % LANES
    padf = lambda a: jnp.pad(a, ((0, 0), (0, padw)))
    d_, m_, v_ = _adamw("adamw_small", padf(pw), padf(pg), padf(pm), jnp.pad(pv, ((0, 0), (0, padw)), constant_values=1.0))
    off = 0
    for n in sm_names:
        sz = w[n].size
        delta[n] = d_[:, off:off + sz].reshape(w[n].shape)
        new_m[n] = m_[:, off:off + sz].reshape(w[n].shape)
        new_v[n] = v_[:, off:off + sz].reshape(w[n].shape)
        off += sz

    return (loss, grad_x[None], *[grads[n] for n in _WEIGHTS], *[delta[n] for n in _WEIGHTS],
            *[new_m[n] for n in _WEIGHTS], *[new_v[n] for n in _WEIGHTS])


def kernel(x, even_norm, even_w_in, mla_q_lat_norm, mla_kv_lat_norm, mla_w_uq, mla_w_ukv, mla_q_norm, mla_k_nope_norm, mla_k_rope_norm, gqa_q_norm, gqa_k_norm, even_w_out, odd_norm, odd_w_qkv, swa_q_norm, swa_k_norm, swa_sink, odd_w_out, mlp_norm, mlp_w_up, mlp_w_down, loss_target, m_even_norm, m_even_w_in, m_mla_q_lat_norm, m_mla_kv_lat_norm, m_mla_w_uq, m_mla_w_ukv, m_mla_q_norm, m_mla_k_nope_norm, m_mla_k_rope_norm, m_gqa_q_norm, m_gqa_k_norm, m_even_w_out, m_odd_norm, m_odd_w_qkv, m_swa_q_norm, m_swa_k_norm, m_swa_sink, m_odd_w_out, m_mlp_norm, m_mlp_w_up, m_mlp_w_down, v_even_norm, v_even_w_in, v_mla_q_lat_norm, v_mla_kv_lat_norm, v_mla_w_uq, v_mla_w_ukv, v_mla_q_norm, v_mla_k_nope_norm, v_mla_k_rope_norm, v_gqa_q_norm, v_gqa_k_norm, v_even_w_out, v_odd_norm, v_odd_w_qkv, v_swa_q_norm, v_swa_k_norm, v_swa_sink, v_odd_w_out, v_mlp_norm, v_mlp_w_up, v_mlp_w_down):
    ws = (even_norm, even_w_in, mla_q_lat_norm, mla_kv_lat_norm, mla_w_uq, mla_w_ukv, mla_q_norm, mla_k_nope_norm, mla_k_rope_norm, gqa_q_norm, gqa_k_norm, even_w_out, odd_norm, odd_w_qkv, swa_q_norm, swa_k_norm, swa_sink, odd_w_out, mlp_norm, mlp_w_up, mlp_w_down)
    ms = (m_even_norm, m_even_w_in, m_mla_q_lat_norm, m_mla_kv_lat_norm, m_mla_w_uq, m_mla_w_ukv, m_mla_q_norm, m_mla_k_nope_norm, m_mla_k_rope_norm, m_gqa_q_norm, m_gqa_k_norm, m_even_w_out, m_odd_norm, m_odd_w_qkv, m_swa_q_norm, m_swa_k_norm, m_swa_sink, m_odd_w_out, m_mlp_norm, m_mlp_w_up, m_mlp_w_down)
    vs = (v_even_norm, v_even_w_in, v_mla_q_lat_norm, v_mla_kv_lat_norm, v_mla_w_uq, v_mla_w_ukv, v_mla_q_norm, v_mla_k_nope_norm, v_mla_k_rope_norm, v_gqa_q_norm, v_gqa_k_norm, v_even_w_out, v_odd_norm, v_odd_w_qkv, v_swa_q_norm, v_swa_k_norm, v_swa_sink, v_odd_w_out, v_mlp_norm, v_mlp_w_up, v_mlp_w_down)
    return _step(x[0], loss_target[0], dict(zip(_WEIGHTS, ws)), dict(zip(_WEIGHTS, ms)), dict(zip(_WEIGHTS, vs)))
```

```python
import jax
import jax.numpy as jnp
from jax import lax
from jax.experimental import pallas as pl
from jax.experimental.pallas import tpu as pltpu

F32 = jnp.float32
BF16 = jnp.bfloat16

N_DEV = 8
NORM_EPS = 1e-6
ROPE_THETA = 500000.0
AXIAL_THETA = 10000.0
GRID_W = 64
HEADS = 8
GQA_KV = 2
HD = 128
MLA_ROPE = 64
MLA_QK = HD + MLA_ROPE
Q_LORA = 512
KV_LORA = 256
SWA_HEADS = 32
SWA_KV = 4
SWA_D = 64
SWA_ROT = 16
SWA_WINDOW = 128
SWA_BLOCK = 128
MLA_SCALE, GQA_SCALE, SWA_SCALE = MLA_QK ** -0.5, HD ** -0.5, SWA_D ** -0.5
LANES = 128
ADAM_LR, ADAM_B1, ADAM_B2, ADAM_EPS, ADAM_WD, ADAM_STEP = 0.001, 0.9, 0.999, 1e-08, 0.01, 10
VMEM_LIMIT = 56 * 1024 * 1024
MM_TILE = (1024, 1024, 2048)
FWD_TILE = (1024, 4096)
BWD_TILE = (1024, 1024)
ROW_TILE = 256

P_CQ, P_CKV, P_KR, P_QG = 0, Q_LORA, Q_LORA + KV_LORA, Q_LORA + KV_LORA + LANES
P_KG = P_QG + HEADS * HD
P_VG = P_KG + GQA_KV * HD
P_END = P_VG + GQA_KV * HD
KR_END = Q_LORA + KV_LORA + MLA_ROPE


def _tile(n, prefs):
    for t in prefs:
        if n % t == 0 and t <= n:
            return t
    return n


def _params(sem):
    return pltpu.CompilerParams(dimension_semantics=sem, vmem_limit_bytes=VMEM_LIMIT)


_DIMS = {"nn": ((1,), (0,)), "nt": ((1,), (1,)), "tn": ((0,), (0,))}


def _mm(name, a, b, mode, out_dtypes, epilogue=None, extras=(), sums=(), tm=MM_TILE[0], tn=MM_TILE[1], tk=MM_TILE[2]):
    if mode == "nn":
        (M, K), (_, N) = a.shape, b.shape
    elif mode == "nt":
        (M, K), (N, _) = a.shape, b.shape
    else:
        (K, M), (_, N) = a.shape, b.shape
    tm = _tile(M, (tm, 512, 256, 128))
    tn = _tile(N, (tn, 512, 256, 128))
    tk = _tile(K, (tk, 1024, 512, 256, 128))
    nk = K // tk
    ne, no, ns = len(extras), len(out_dtypes), len(sums)
    if mode == "tn":
        a_spec = pl.BlockSpec((tk, tm), lambda i, j, k: (k, i))
    else:
        a_spec = pl.BlockSpec((tm, tk), lambda i, j, k: (i, k))
    if mode == "nt":
        b_spec = pl.BlockSpec((tn, tk), lambda i, j, k: (j, k))
    else:
        b_spec = pl.BlockSpec((tk, tn), lambda i, j, k: (k, j))
    o_spec = pl.BlockSpec((tm, tn), lambda i, j, k: (i, j))
    dims = (_DIMS[mode], ((), ()))

    def body(a_ref, b_ref, *rest):
        ex, outs = rest[:ne], rest[ne:ne + no]
        k = pl.program_id(2)
        part = lax.dot_general(a_ref[...].astype(BF16), b_ref[...].astype(BF16), dims, preferred_element_type=F32)

        def finish(total):
            res = epilogue(total, *[e[...] for e in ex]) if epilogue else (total,)
            for o, r in zip(outs, res[:no]):
                o[...] = r.astype(o.dtype)
            if ns:
                first = (pl.program_id(0) == 0) & (pl.program_id(1) == 0)

                @pl.when(first)
                def _():
                    for s_ref, r in zip(rest[ne + no:ne + no + ns], res[no:]):
                        s_ref[...] = r

                @pl.when(jnp.logical_not(first))
                def _():
                    for s_ref, r in zip(rest[ne + no:ne + no + ns], res[no:]):
                        s_ref[...] += r

        if nk == 1:
            finish(part)
            return
        acc = rest[ne + no + ns]

        @pl.when(k == 0)
        def _():
            acc[...] = part

        @pl.when((k > 0) & (k < nk - 1))
        def _():
            acc[...] += part

        @pl.when(k == nk - 1)
        def _():
            finish(acc[...] + part)

    out = pl.pallas_call(
        body, name=name, grid=(M // tm, N // tn, nk),
        in_specs=[a_spec, b_spec] + [o_spec] * ne,
        out_specs=[o_spec] * no + [pl.BlockSpec(tuple(s.shape), lambda i, j, k: (0,) * len(s.shape)) for s in sums],
        out_shape=[jax.ShapeDtypeStruct((M, N), d) for d in out_dtypes] + list(sums),
        scratch_shapes=[pltpu.VMEM((tm, tn), F32)] if nk > 1 else [],
        compiler_params=_params(("arbitrary",) * 3 if ns else ("parallel", "parallel", "arbitrary")),
    )(a, b, *extras)
    return out[0] if no + ns == 1 else out


def _rowwise(name, fn, rows, consts, outs, accs=(), tm=ROW_TILE, after=()):
    S = rows[0].shape[-2]
    tm = _tile(S, (tm, 128, 64, 32, 16, 8))
    nr, nc, nd, no, na = len(rows), len(consts), len(after), len(outs), len(accs)

    def rspec(shape):
        if len(shape) == 2:
            return pl.BlockSpec((tm, shape[1]), lambda i: (i, 0))
        return pl.BlockSpec((shape[0], tm, shape[2]), lambda i: (0, i, 0))

    def cspec(shape):
        return pl.BlockSpec(tuple(shape), lambda i: (0,) * len(shape))

    def ospec(o):
        if isinstance(o, tuple):
            return pl.BlockSpec((o[1].shape[0], o[1].shape[1], tm), lambda i: (0, 0, i))
        return rspec(o.shape)

    out_sds = [o[1] if isinstance(o, tuple) else o for o in outs]

    def body(*refs):
        r, c = refs[:nr], refs[nr:nr + nc]
        o, a = refs[nr + nc + nd:nr + nc + nd + no], refs[nr + nc + nd + no:]
        vals = fn(r, c, o)
        if na:
            @pl.when(pl.program_id(0) == 0)
            def _():
                for ar in a:
                    ar[...] = jnp.zeros_like(ar)

            for ar, v in zip(a, vals):
                ar[...] += v

    res = pl.pallas_call(
        body, name=name, grid=(S // tm,),
        in_specs=[rspec(x.shape) for x in rows] + [cspec(x.shape) for x in consts]
        + [pl.BlockSpec(memory_space=pl.ANY)] * len(after),
        out_specs=[ospec(x) for x in outs] + [cspec(x.shape) for x in accs],
        out_shape=out_sds + list(accs),
        compiler_params=_params(("arbitrary",) if na else ("parallel",)),
    )(*rows, *consts, *after)
    return res


def _sds(shape, dtype):
    return jax.ShapeDtypeStruct(tuple(shape), dtype)


def _rsum(x):
    return jnp.sum(x, axis=-1, keepdims=True)


def _csum(x):
    return jnp.sum(x, axis=0, keepdims=True)


def _lane(shape):
    return lax.broadcasted_iota(jnp.int32, shape, 1)


def _partner(x, seg, half):
    lane = _lane(x.shape) % seg
    return jnp.where(lane < half, pltpu.roll(x, LANES - half, 1), pltpu.roll(x, half, 1))


def _rope(x, c, s, seg, half):
    return x * c + _partner(x, seg, half) * s


def _rope_bwd(dy, c, s, seg, half):
    t = _partner(dy * s, seg, half)
    if seg != 2 * half:
        t = jnp.where(_lane(dy.shape) % seg < 2 * half, t, 0.0)
    return dy * c + t


def _rms_r(ss, n):
    return lax.rsqrt(ss * (1.0 / n) + NORM_EPS)


def _rms_bwd(x, r, g, dy, dot_scale):
    dyg = dy * g
    return r * dyg - x * (r * r * r) * (_rsum(dyg * x) * dot_scale)


def _rmsnorm(name, x, g, after=()):
    D = x.shape[1]

    def fn(r, c, o):
        xv = r[0][...]
        o[0][...] = (xv * _rms_r(_rsum(xv * xv), D) * c[0][...]).astype(BF16)
        return ()

    return _rowwise(name, fn, [x], [g], [_sds(x.shape, BF16)], tm=4 * ROW_TILE, after=after)[0]


def _rmsnorm_bwd(name, dres, x, dh, g):
    D = x.shape[1]

    def fn(r, c, o):
        xv, dhv, gv = r[1][...], r[2][...], c[0][...]
        rr = _rms_r(_rsum(xv * xv), D)
        o[0][...] = r[0][...] + _rms_bwd(xv, rr, gv, dhv, 1.0 / D)
        return (_csum(dhv * xv * rr),)

    return _rowwise(name, fn, [dres, x, dh], [g], [_sds(x.shape, F32)], [_sds((1, D), F32)], tm=2 * ROW_TILE)


def _even_prep(name, proj, tabs, gains, after=()):
    S = proj.shape[0]
    cm, sm, ca, sa = tabs

    def fn(r, c, o):
        p, cmv, smv, cav, sav = r[0], r[1][...], r[2][...], r[3][...], r[4][...]
        g_ql, g_kvl, g_kr, g_q, g_k = (x[...] for x in c)
        ones = _seg_ones(LANES)
        cq = p[:, P_CQ:P_CKV]
        o[0][...] = (cq * _wide(_rms_r(_seg_sum(_fold(cq * cq), ones), Q_LORA), Q_LORA // LANES) * g_ql).astype(BF16)
        ckv = p[:, P_CKV:P_KR]
        o[1][...] = (ckv * _wide(_rms_r(_seg_sum(_fold(ckv * ckv), ones), KV_LORA), KV_LORA // LANES) * g_kvl).astype(BF16)
        kr = p[:, P_KR:P_QG]
        y = _rope(kr * _rms_r(_seg_sum(kr * kr, ones), MLA_ROPE) * g_kr, cmv, smv, 64, 32)
        o[2][...] = (y + pltpu.roll(y, 64, 1)).astype(BF16)
        for h in range(HEADS):
            xh = p[:, P_QG + HD * h:P_QG + HD * (h + 1)]
            o[3][h] = (_rope(xh * _rms_r(_seg_sum(xh * xh, ones), HD) * g_q, cav, sav, 64, 32) * GQA_SCALE).astype(BF16)
        for h in range(GQA_KV):
            xh = p[:, P_KG + HD * h:P_KG + HD * (h + 1)]
            o[4][h] = _rope(xh * _rms_r(_seg_sum(xh * xh, ones), HD) * g_k, cav, sav, 64, 32).astype(BF16)
            o[5][h] = p[:, P_VG + HD * h:P_VG + HD * (h + 1)].astype(BF16)
        return ()

    outs = [_sds((S, Q_LORA), BF16), _sds((S, KV_LORA), BF16), _sds((S, LANES), BF16),
            _sds((HEADS, S, HD), BF16), _sds((GQA_KV, S, HD), BF16), _sds((GQA_KV, S, HD), BF16)]
    return _rowwise(name, fn, [proj, cm, sm, ca, sa], gains, outs, tm=2 * ROW_TILE, after=after)


def _even_prep_bwd(name, dcqn, dckvn, dkrd, dqg, dkg, dvg, proj, tabs, gains, after=()):
    S = proj.shape[0]
    cm, sm, ca, sa = tabs

    def fn(r, c, o):
        p, cmv, smv, cav, sav = r[6], r[7][...], r[8][...], r[9][...], r[10][...]
        g_ql, g_kvl, g_kr, g_q, g_k = (x[...] for x in c)
        out = o[0]
        ones = _seg_ones(LANES)

        def wide_bwd(x, g, d, n):
            w_ = n // LANES
            rr = _wide(_rms_r(_seg_sum(_fold(x * x), ones), n), w_)
            dot = _wide(_seg_sum(_fold(d * g * x), ones) * (1.0 / n), w_)
            return _rms_bwd_b(x, rr, g, d, dot).astype(BF16), _csum(d * x * rr)

        cq = p[:, P_CQ:P_CKV]
        out[:, P_CQ:P_CKV], a_ql = wide_bwd(cq, g_ql, r[0][...], Q_LORA)
        ckv = p[:, P_CKV:P_KR]
        out[:, P_CKV:P_KR], a_kvl = wide_bwd(ckv, g_kvl, r[1][...], KV_LORA)
        kr = p[:, P_KR:P_QG]
        rr = _rms_r(_seg_sum(kr * kr, ones), MLA_ROPE)
        d = r[2][...]
        d = d + pltpu.roll(d, 64, 1)
        d = _rope_bwd(d, cmv, smv, 64, 32)
        low = _lane(d.shape) < 64
        dot = _seg_sum(jnp.where(low, d * g_kr * kr, 0.0), ones) * (1.0 / MLA_ROPE)
        out[:, P_KR:P_QG] = jnp.where(low, _rms_bwd_b(kr, rr, g_kr, d, dot), 0.0).astype(BF16)
        a_kr = _csum(d * kr * rr)
        a_q = jnp.zeros((1, HD), F32)

        def head_bwd(xh, g, d):
            rr = _rms_r(_seg_sum(xh * xh, ones), HD)
            dot = _seg_sum(d * g * xh, ones) * (1.0 / HD)
            return _rms_bwd_b(xh, rr, g, d, dot).astype(BF16), _csum(d * xh * rr)

        for h in range(HEADS):
            xh = p[:, P_QG + HD * h:P_QG + HD * (h + 1)]
            d = _rope_bwd(r[3][h] * GQA_SCALE, cav, sav, 64, 32)
            out[:, P_QG + HD * h:P_QG + HD * (h + 1)], inc = head_bwd(xh, g_q, d)
            a_q = a_q + inc
        a_k = jnp.zeros((1, HD), F32)
        grp = HEADS // GQA_KV
        for h in range(GQA_KV):
            xh = p[:, P_KG + HD * h:P_KG + HD * (h + 1)]
            d = r[4][grp * h]
            dv = r[5][grp * h]
            for e in range(1, grp):
                d = d + r[4][grp * h + e]
                dv = dv + r[5][grp * h + e]
            d = _rope_bwd(d, cav, sav, 64, 32)
            out[:, P_KG + HD * h:P_KG + HD * (h + 1)], inc = head_bwd(xh, g_k, d)
            out[:, P_VG + HD * h:P_VG + HD * (h + 1)] = dv.astype(BF16)
            a_k = a_k + inc
        return (a_ql, a_kvl, a_kr, a_q, a_k)

    accs = [_sds((1, Q_LORA), F32), _sds((1, KV_LORA), F32), _sds((1, LANES), F32),
            _sds((1, HD), F32), _sds((1, HD), F32)]
    return _rowwise(name, fn, [dcqn, dckvn, dkrd, dqg, dkg, dvg, proj, cm, sm, ca, sa], gains,
                    [_sds((S, P_END), BF16)], accs, after=after)


def _mla_q_parts(qa, h, rp, ones):
    nope = qa[:, HD * h:HD * (h + 1)]
    mine = (_lane(rp.shape) >= 64) == bool(h % 2)
    rpm = jnp.where(mine, rp, 0.0)
    rr = _rms_r(_seg_sum(nope * nope + rpm * rpm, ones), MLA_QK)
    return nope, rpm, mine, rr


def _mla_prep(name, qa, kv, krd, tabs, gains):
    S = qa.shape[0]
    cm, sm = tabs

    def fn(r, c, o):
        qa_r, kv_r, krd_v, cmv, smv = r[0], r[1], r[2][...], r[3][...], r[4][...]
        g_n, g_r, g_kn = (x[...] for x in c)
        ones = _seg_ones(LANES)
        for h in range(HEADS):
            rp = qa_r[:, HEADS * HD + LANES * (h // 2):HEADS * HD + LANES * (h // 2 + 1)]
            nope, rpm, mine, rr = _mla_q_parts(qa_r, h, rp, ones)
            o[0][h, :, 0:HD] = (nope * (rr * MLA_SCALE) * g_n).astype(BF16)
            o[0][h, :, HD:2 * HD] = _rope(rpm * (rr * MLA_SCALE) * g_r, cmv, smv, 64, 32).astype(BF16)
            kn = kv_r[:, 2 * HD * h:2 * HD * h + HD]
            o[1][h, :, 0:HD] = (kn * _rms_r(_seg_sum(kn * kn, ones), HD) * g_kn).astype(BF16)
            o[1][h, :, HD:2 * HD] = krd_v
            o[2][h] = kv_r[:, 2 * HD * h + HD:2 * HD * (h + 1)].astype(BF16)
        return ()

    outs = [_sds((HEADS, S, 2 * HD), BF16), _sds((HEADS, S, 2 * HD), BF16), _sds((HEADS, S, HD), BF16)]
    return _rowwise(name, fn, [qa, kv, krd, cm, sm], gains, outs, tm=2 * ROW_TILE)


def _mla_prep_bwd(name, dq, dk, dv, qa, kv, tabs, gains):
    S = qa.shape[0]
    cm, sm = tabs

    def fn(r, c, o):
        dq_r, dk_r, dv_r, qa_r, kv_r, cmv, smv = r[0], r[1], r[2], r[3], r[4], r[5][...], r[6][...]
        g_n, g_r, g_kn = (x[...] for x in c)
        a_n = jnp.zeros((1, HD), F32)
        a_r = jnp.zeros((1, LANES), F32)
        a_kn = jnp.zeros((1, HD), F32)
        dkrd = jnp.zeros(cmv.shape, F32)
        drp = None
        ones = _seg_ones(LANES)
        for h in range(HEADS):
            rp = qa_r[:, HEADS * HD + LANES * (h // 2):HEADS * HD + LANES * (h // 2 + 1)]
            nope, rpm, mine, rr = _mla_q_parts(qa_r, h, rp, ones)
            dn = dq_r[h, :, 0:HD] * MLA_SCALE
            dr = _rope_bwd(jnp.where(mine, dq_r[h, :, HD:2 * HD] * MLA_SCALE, 0.0), cmv, smv, 64, 32)
            dot = _seg_sum(dn * g_n * nope + dr * g_r * rpm, ones) * (1.0 / MLA_QK)
            r3 = rr * rr * rr
            o[0][:, HD * h:HD * (h + 1)] = (rr * dn * g_n - nope * r3 * dot).astype(BF16)
            part = jnp.where(mine, rr * dr * g_r - rpm * r3 * dot, 0.0)
            drp = part if h % 2 == 0 else drp + part
            if h % 2 == 1:
                o[0][:, HEADS * HD + LANES * (h // 2):HEADS * HD + LANES * (h // 2 + 1)] = drp.astype(BF16)
            a_n = a_n + _csum(dn * nope * rr)
            a_r = a_r + _csum(dr * rpm * rr)
            kn = kv_r[:, 2 * HD * h:2 * HD * h + HD]
            rk = _rms_r(_seg_sum(kn * kn, ones), HD)
            dkn = dk_r[h, :, 0:HD]
            dotk = _seg_sum(dkn * g_kn * kn, ones) * (1.0 / HD)
            o[1][:, 2 * HD * h:2 * HD * h + HD] = _rms_bwd_b(kn, rk, g_kn, dkn, dotk).astype(BF16)
            o[1][:, 2 * HD * h + HD:2 * HD * (h + 1)] = dv_r[h].astype(BF16)
            a_kn = a_kn + _csum(dkn * kn * rk)
            dkrd = dkrd + dk_r[h, :, HD:2 * HD]
        o[2][...] = dkrd
        return (a_n, a_r, a_kn)

    outs = [_sds(qa.shape, BF16), _sds(kv.shape, BF16), _sds((S, LANES), F32)]
    accs = [_sds((1, HD), F32), _sds((1, LANES), F32), _sds((1, HD), F32)]
    return _rowwise(name, fn, [dq, dk, dv, qa, kv, cm, sm], gains, outs, accs)


def _seg_ones(seg):
    r = lax.broadcasted_iota(jnp.int32, (LANES, LANES), 0) // seg
    c = lax.broadcasted_iota(jnp.int32, (LANES, LANES), 1) // seg
    return (r == c).astype(F32)


def _dot01(x, sel, dims):
    hi = x.astype(BF16)
    lo = (x - hi.astype(F32)).astype(BF16)
    sb = sel.astype(BF16)
    if dims is _NN:
        return (lax.dot_general(hi, sb, dims, preferred_element_type=F32)
                + lax.dot_general(lo, sb, dims, preferred_element_type=F32))
    return (lax.dot_general(sb, hi, dims, preferred_element_type=F32)
            + lax.dot_general(sb, lo, dims, preferred_element_type=F32))


def _seg_sum(x, ones):
    return _dot01(x, ones, _NN)


def _fold(x):
    acc = x[:, 0:LANES]
    for g in range(1, x.shape[1] // LANES):
        acc = acc + x[:, LANES * g:LANES * (g + 1)]
    return acc


def _wide(v, n):
    return jnp.concatenate([v] * n, axis=1) if n > 1 else v


def _rms_bwd_b(x, r, g, dy, dot_b):
    dyg = dy * g
    return r * dyg - x * (r * r * r) * dot_b


def _seg64_r(x, ones):
    return _rms_r(_seg_sum(x * x, ones), SWA_D)


def _swa_prep(name, qkv, tabs, gains):
    S = qkv.shape[0]
    nq, nk = SWA_HEADS * SWA_D, SWA_KV * SWA_D
    cs, ss = tabs

    def fn(r, c, o):
        x_r, csv, ssv = r[0], r[1][...], r[2][...]
        g_q, g_k = c[0][...], c[1][...]
        ones = _seg_ones(SWA_D)
        for g in range((nq + nk) // LANES):
            x = x_r[:, LANES * g:LANES * (g + 1)]
            rr = _seg64_r(x, ones)
            y = _rope(x * rr * (g_q if g < nq // LANES else g_k), csv, ssv, SWA_D, SWA_ROT // 2)
            if g < nq // LANES:
                o[0][:, LANES * g:LANES * (g + 1)] = (y * SWA_SCALE).astype(BF16)
            else:
                o[1][:, LANES * g - nq:LANES * (g + 1) - nq] = y.astype(BF16)
        o[2][...] = x_r[:, nq + nk:nq + 2 * nk].astype(BF16)
        return ()

    outs = [_sds((S, nq), BF16), _sds((S, nk), BF16), _sds((S, nk), BF16)]
    return _rowwise(name, fn, [qkv, cs, ss], gains, outs, tm=2 * ROW_TILE)


def _swa_prep_bwd(name, dq, dk, dv, qkv, tabs, gains):
    nq, nk = SWA_HEADS * SWA_D, SWA_KV * SWA_D
    cs, ss = tabs

    def fn(r, c, o):
        dq_r, dk_r, x_r, csv, ssv = r[0], r[1], r[3], r[4][...], r[5][...]
        g_q, g_k = c[0][...], c[1][...]
        acc = [jnp.zeros((1, LANES), F32), jnp.zeros((1, LANES), F32)]
        ones = _seg_ones(SWA_D)
        for g in range((nq + nk) // LANES):
            isq = g < nq // LANES
            x = x_r[:, LANES * g:LANES * (g + 1)]
            rr = _seg64_r(x, ones)
            d = dq_r[:, LANES * g:LANES * (g + 1)] * SWA_SCALE if isq else dk_r[:, LANES * g - nq:LANES * (g + 1) - nq]
            d = _rope_bwd(d, csv, ssv, SWA_D, SWA_ROT // 2)
            dyg = d * (g_q if isq else g_k)
            dot = _seg_sum(dyg * x, ones) * (1.0 / SWA_D)
            o[0][:, LANES * g:LANES * (g + 1)] = (rr * dyg - x * (rr * rr * rr) * dot).astype(BF16)
            acc[0 if isq else 1] = acc[0 if isq else 1] + _csum(d * x * rr)
        o[0][:, nq + nk:nq + 2 * nk] = r[2][...].astype(BF16)
        return tuple(acc)

    return _rowwise(name, fn, [dq, dk, dv, qkv, cs, ss], gains, [_sds(qkv.shape, BF16)],
                    [_sds((1, LANES), F32), _sds((1, LANES), F32)])


def _delta(name, do, o_, after=()):
    S, C = do.shape

    def fn(r, c, o):
        for g in range(C // LANES):
            t = r[0][:, LANES * g:LANES * (g + 1)].astype(F32) * r[1][:, LANES * g:LANES * (g + 1)].astype(F32)
            o[0][g] = _rsum(t)
        return ()

    return _rowwise(name, fn, [do, o_], [], [_sds((C // LANES, S, 1), F32)], after=after)[0]


def _delta_t(name, do, o_, after=()):
    S, C = do.shape

    def fn(r, c, o):
        row = lax.broadcasted_iota(jnp.int32, (8, LANES), 0)
        sel = (row == _lane((8, LANES)) // SWA_D).astype(F32)
        for g in range(C // LANES):
            t = r[0][:, LANES * g:LANES * (g + 1)].astype(F32) * r[1][:, LANES * g:LANES * (g + 1)].astype(F32)
            both = _dot01(t, sel, _NT)
            o[0][2 * g] = both[0:1, :]
            o[0][2 * g + 1] = both[1:2, :]
        return ()

    return _rowwise(name, fn, [do, o_], [], [("T", _sds((2 * C // LANES, 1, S), F32))], after=after)[0]


_NT = (((1,), (1,)), ((), ()))
_NN = (((1,), (0,)), ((), ()))
_TN = (((0,), (0,)), ((), ()))


def _flash_fwd(name, q, k, v, tq=FWD_TILE[0], tk=FWD_TILE[1]):
    H, S, dk = q.shape
    G = H // k.shape[0]
    dv = v.shape[2]
    tq, tk = _tile(S, (tq, 256, 128)), _tile(S, (tk, 256, 128))
    nk = S // tk

    def body(q_ref, k_ref, v_ref, o_ref, lse_ref, m_s, l_s, acc_s):
        j = pl.program_id(2)

        @pl.when(j == 0)
        def _():
            m_s[...] = jnp.full_like(m_s, -jnp.inf)
            l_s[...] = jnp.zeros_like(l_s)
            acc_s[...] = jnp.zeros_like(acc_s)

        s = lax.dot_general(q_ref[...], k_ref[...], _NT, preferred_element_type=F32)
        m_new = jnp.maximum(m_s[...], jnp.max(s, axis=-1, keepdims=True))
        alpha = jnp.exp(m_s[...] - m_new)
        p = jnp.exp(s - m_new)
        l_s[...] = alpha * l_s[...] + _rsum(p)
        acc_s[...] = alpha * acc_s[...] + lax.dot_general(p.astype(BF16), v_ref[...], _NN,
                                                          preferred_element_type=F32)
        m_s[...] = m_new

        @pl.when(j == nk - 1)
        def _():
            o_ref[...] = (acc_s[...] / l_s[...]).astype(o_ref.dtype)
            lse_ref[...] = m_s[...] + jnp.log(l_s[...])

    return pl.pallas_call(
        body, name=name, grid=(H, S // tq, nk),
        in_specs=[pl.BlockSpec((None, tq, dk), lambda h, i, j: (h, i, 0)),
                  pl.BlockSpec((None, tk, dk), lambda h, i, j: (h // G, j, 0)),
                  pl.BlockSpec((None, tk, dv), lambda h, i, j: (h // G, j, 0))],
        out_specs=[pl.BlockSpec((tq, dv), lambda h, i, j: (i, h)),
                   pl.BlockSpec((None, tq, 1), lambda h, i, j: (h, i, 0))],
        out_shape=[_sds((S, H * dv), BF16), _sds((H, S, 1), F32)],
        scratch_shapes=[pltpu.VMEM((tq, 1), F32), pltpu.VMEM((tq, 1), F32), pltpu.VMEM((tq, dv), F32)],
        compiler_params=_params(("parallel", "parallel", "arbitrary")),
    )(q, k, v)


def _flash_bwd(name, q, k, v, do, head0, lse, delta, tq=BWD_TILE[0], tk=BWD_TILE[1]):
    H, S, dk = q.shape
    G = H // k.shape[0]
    dv = v.shape[2]
    tq, tk = _tile(S, (tq, 256, 128)), _tile(S, (tk, 256, 128))
    nq = S // tq

    def body(q_ref, k_ref, v_ref, do_ref, lse_ref, dl_ref, dq_ref, dk_ref, dv_ref, dk_s, dv_s):
        j, i = pl.program_id(1), pl.program_id(2)

        @pl.when(i == 0)
        def _():
            dk_s[...] = jnp.zeros_like(dk_s)
            dv_s[...] = jnp.zeros_like(dv_s)

        qv, kv_, dov = q_ref[...], k_ref[...], do_ref[...]
        s = lax.dot_general(qv, kv_, _NT, preferred_element_type=F32)
        p = jnp.exp(s - lse_ref[...])
        dp = lax.dot_general(dov, v_ref[...], _NT, preferred_element_type=F32)
        ds = (p * (dp - dl_ref[...])).astype(BF16)
        dv_s[...] += lax.dot_general(p.astype(BF16), dov, _TN, preferred_element_type=F32)
        dk_s[...] += lax.dot_general(ds, qv, _TN, preferred_element_type=F32)
        dqi = lax.dot_general(ds, kv_, _NN, preferred_element_type=F32)
        rows = pl.ds(pl.multiple_of(i * tq, tq), tq)

        @pl.when(j == 0)
        def _():
            dq_ref[rows, :] = dqi

        @pl.when(j > 0)
        def _():
            dq_ref[rows, :] += dqi

        @pl.when(i == nq - 1)
        def _():
            dk_ref[...] = dk_s[...]
            dv_ref[...] = dv_s[...]

    return pl.pallas_call(
        body, name=name, grid=(H, S // tk, nq),
        in_specs=[pl.BlockSpec((None, tq, dk), lambda h, j, i: (h, i, 0)),
                  pl.BlockSpec((None, tk, dk), lambda h, j, i: (h // G, j, 0)),
                  pl.BlockSpec((None, tk, dv), lambda h, j, i: (h // G, j, 0)),
                  pl.BlockSpec((tq, dv), lambda h, j, i: (i, head0 + h)),
                  pl.BlockSpec((None, tq, 1), lambda h, j, i: (h, i, 0)),
                  pl.BlockSpec((None, tq, 1), lambda h, j, i: (head0 + h, i, 0))],
        out_specs=[pl.BlockSpec((None, S, dk), lambda h, j, i: (h, 0, 0)),
                   pl.BlockSpec((None, tk, dk), lambda h, j, i: (h, j, 0)),
                   pl.BlockSpec((None, tk, dv), lambda h, j, i: (h, j, 0))],
        out_shape=[_sds((H, S, dk), F32), _sds((H, S, dk), F32), _sds((H, S, dv), F32)],
        scratch_shapes=[pltpu.VMEM((tk, dk), F32), pltpu.VMEM((tk, dv), F32)],
        compiler_params=_params(("parallel", "arbitrary", "arbitrary")),
    )(q, k, v, do, lse, delta)


def _swa_place(ref128, h):
    e, t = h % 2, (h // (SWA_HEADS // SWA_KV)) % 2
    x = ref128.astype(F32)
    if e != t:
        x = pltpu.roll(x, 64, 1)
    return jnp.where((_lane(x.shape) >= 64) == bool(t), x, 0.0).astype(BF16)


def _swa_unplace(y, h):
    e, t = h % 2, (h // (SWA_HEADS // SWA_KV)) % 2
    return pltpu.roll(y, 64, 1) if e != t else y


def _swa_specs(width, nb):
    prev = pl.BlockSpec((SWA_BLOCK, width), lambda i: (jnp.maximum(i - 1, 0), 0))
    cur = pl.BlockSpec((SWA_BLOCK, width), lambda i: (i, 0))
    nxt = pl.BlockSpec((SWA_BLOCK, width), lambda i: (jnp.minimum(i + 1, nb - 1), 0))
    return [prev, cur, nxt]


def _swa_bias_t(i, S):
    shape = (3 * SWA_BLOCK, SWA_BLOCK)
    kpos = (i - 1) * SWA_BLOCK + lax.broadcasted_iota(jnp.int32, shape, 0)
    qpos = i * SWA_BLOCK + lax.broadcasted_iota(jnp.int32, shape, 1)
    ok = (jnp.abs(qpos - kpos) <= SWA_WINDOW) & (kpos >= 0) & (kpos < S)
    return jnp.where(ok, 0.0, -jnp.inf)


def _add_blocks(x, blocks):
    n = x.shape[1] // LANES
    return jnp.concatenate([x[:, LANES * b:LANES * (b + 1)] + blocks[b % len(blocks)] for b in range(n)], axis=1)


def _swa_stack(refs, heads):
    return jnp.concatenate([_swa_place(r[:, LANES * (h // 2):LANES * (h // 2 + 1)], h) for h in heads for r in refs],
                           axis=0)


def _swa_unstack_t(ot, j, out_ref):
    grp = SWA_HEADS // SWA_KV
    for pair in range(grp // 2):
        h = grp * j + 2 * pair
        a = _swa_unplace(ot[:, SWA_BLOCK * 2 * pair:SWA_BLOCK * (2 * pair + 1)].T, h)
        b = _swa_unplace(ot[:, SWA_BLOCK * (2 * pair + 1):SWA_BLOCK * (2 * pair + 2)].T, h + 1)
        out_ref[:, LANES * (h // 2):LANES * (h // 2 + 1)] = jnp.where(_lane(a.shape) < 64, a, b).astype(out_ref.dtype)


def _swa_fwd(name, q, k, v, sink):
    S = q.shape[0]
    nb = S // SWA_BLOCK
    grp = SWA_HEADS // SWA_KV
    smem = pl.BlockSpec(memory_space=pltpu.SMEM)

    def body(sink_ref, q_ref, kp, kc, kn, vp, vc, vn, o_ref, lse_ref):
        i = pl.program_id(0)
        bias = [_swa_bias_t(i, S)]
        kcat = [jnp.concatenate([r[:, LANES * u:LANES * (u + 1)] for r in (kp, kc, kn)], axis=0) for u in range(2)]
        vcat = [jnp.concatenate([r[:, LANES * u:LANES * (u + 1)] for r in (vp, vc, vn)], axis=0) for u in range(2)]
        for j in range(SWA_KV):
            heads = range(grp * j, grp * (j + 1))
            xq = _swa_stack([q_ref], heads)
            sk = jnp.concatenate([jnp.full((1, SWA_BLOCK), sink_ref[0, h], F32) for h in heads], axis=1)
            st = lax.dot_general(kcat[j // 2], xq, _NT, preferred_element_type=F32)
            st = _add_blocks(st, bias)
            m = jnp.maximum(jnp.max(st, axis=0, keepdims=True), sk)
            pt = jnp.exp(st - m)
            den = jnp.sum(pt, axis=0, keepdims=True) + jnp.exp(sk - m)
            ot = lax.dot_general(vcat[j // 2], pt.astype(BF16), _TN, preferred_element_type=F32) * (1.0 / den)
            lse = m + jnp.log(den)
            for e, h in enumerate(heads):
                lse_ref[h] = lse[:, SWA_BLOCK * e:SWA_BLOCK * (e + 1)]
            _swa_unstack_t(ot, j, o_ref)

    return pl.pallas_call(
        body, name=name, grid=(nb,),
        in_specs=[smem, pl.BlockSpec((SWA_BLOCK, q.shape[1]), lambda i: (i, 0))]
        + _swa_specs(k.shape[1], nb) + _swa_specs(v.shape[1], nb),
        out_specs=[pl.BlockSpec((SWA_BLOCK, q.shape[1]), lambda i: (i, 0)),
                   pl.BlockSpec((SWA_HEADS, 1, SWA_BLOCK), lambda i: (0, 0, i))],
        out_shape=[_sds(q.shape, BF16), _sds((SWA_HEADS, 1, S), F32)],
        compiler_params=_params(("parallel",)),
    )(sink, q, k, k, k, v, v, v)


def _swa_bwd_q(name, q, k, v, do, lse_t, delta_t, sink):
    S = q.shape[0]
    nb = S // SWA_BLOCK
    grp = SWA_HEADS // SWA_KV
    smem = pl.BlockSpec(memory_space=pltpu.SMEM)
    row = pl.BlockSpec((SWA_BLOCK, q.shape[1]), lambda i: (i, 0))
    hrow = pl.BlockSpec((SWA_HEADS, 1, SWA_BLOCK), lambda i: (0, 0, i))

    def body(sink_ref, q_ref, do_ref, lse_ref, dl_ref, kp, kc, kn, vp, vc, vn, dq_ref, ds_ref):
        i = pl.program_id(0)

        @pl.when(i == 0)
        def _():
            ds_ref[...] = jnp.zeros_like(ds_ref)

        bias = [_swa_bias_t(i, S)]
        kcat = [jnp.concatenate([r[:, LANES * u:LANES * (u + 1)] for r in (kp, kc, kn)], axis=0) for u in range(2)]
        vcat = [jnp.concatenate([r[:, LANES * u:LANES * (u + 1)] for r in (vp, vc, vn)], axis=0) for u in range(2)]
        for j in range(SWA_KV):
            heads = range(grp * j, grp * (j + 1))
            xq = _swa_stack([q_ref], heads)
            xdo = _swa_stack([do_ref], heads)
            lse_r = jnp.concatenate([lse_ref[h] for h in heads], axis=1)
            dl_r = jnp.concatenate([dl_ref[h] for h in heads], axis=1)
            st = lax.dot_general(kcat[j // 2], xq, _NT, preferred_element_type=F32)
            pt = jnp.exp(_add_blocks(st, bias) - lse_r)
            dpt = lax.dot_general(vcat[j // 2], xdo, _NT, preferred_element_type=F32)
            dst = (pt * (dpt - dl_r)).astype(BF16)
            _swa_unstack_t(lax.dot_general(kcat[j // 2], dst, _TN, preferred_element_type=F32), j, dq_ref)
            for h in heads:
                dsink = -_rsum(jnp.exp(sink_ref[0, h] - lse_ref[h]) * dl_ref[h])
                ds_ref[h:h + 1, :] += jnp.broadcast_to(dsink, (1, LANES))

    return pl.pallas_call(
        body, name=name, grid=(nb,),
        in_specs=[smem, row, row, hrow, hrow] + _swa_specs(k.shape[1], nb) + _swa_specs(v.shape[1], nb),
        out_specs=[row, pl.BlockSpec((SWA_HEADS, LANES), lambda i: (0, 0))],
        out_shape=[_sds(q.shape, F32), _sds((SWA_HEADS, LANES), F32)],
        compiler_params=_params(("arbitrary",)),
    )(sink, q, do, lse_t, delta_t, k, k, k, v, v, v)


def _swa_bwd_kv(name, q, k, v, do, lse_t, delta_t):
    S = q.shape[0]
    nb = S // SWA_BLOCK
    grp = SWA_HEADS // SWA_KV
    krow = pl.BlockSpec((SWA_BLOCK, k.shape[1]), lambda i: (i, 0))

    def stat3():
        return [pl.BlockSpec((SWA_HEADS, 1, SWA_BLOCK), lambda i: (0, 0, jnp.maximum(i - 1, 0))),
                pl.BlockSpec((SWA_HEADS, 1, SWA_BLOCK), lambda i: (0, 0, i)),
                pl.BlockSpec((SWA_HEADS, 1, SWA_BLOCK), lambda i: (0, 0, jnp.minimum(i + 1, nb - 1)))]

    def body(qp, qc, qn, dop, doc, don, lp, lc, ln, dp_, dc_, dn_, k_ref, v_ref, dk_ref, dv_ref):
        j = pl.program_id(0)
        nh = 2 * grp
        kpos = j * SWA_BLOCK + lax.broadcasted_iota(jnp.int32, (SWA_BLOCK, SWA_BLOCK), 0)
        bias = []
        for b in range(3):
            qpos = (j - 1 + b) * SWA_BLOCK + lax.broadcasted_iota(jnp.int32, (SWA_BLOCK, SWA_BLOCK), 1)
            ok = (jnp.abs(qpos - kpos) <= SWA_WINDOW) & (qpos >= 0) & (qpos < S)
            bias.append(jnp.where(ok, 0.0, -jnp.inf))
        for u in range(SWA_KV // 2):
            heads = range(nh * u, nh * (u + 1))
            kc = k_ref[:, LANES * u:LANES * (u + 1)]
            vc = v_ref[:, LANES * u:LANES * (u + 1)]
            xq = _swa_stack([qp, qc, qn], heads)
            xdo = _swa_stack([dop, doc, don], heads)
            lse_r = jnp.concatenate([r[h] for h in heads for r in (lp, lc, ln)], axis=1)
            dl_r = jnp.concatenate([r[h] for h in heads for r in (dp_, dc_, dn_)], axis=1)
            st = lax.dot_general(kc, xq, _NT, preferred_element_type=F32)
            pt = jnp.exp(_add_blocks(st, bias) - lse_r)
            dpt = lax.dot_general(vc, xdo, _NT, preferred_element_type=F32)
            dst = (pt * (dpt - dl_r)).astype(BF16)
            dv_ref[:, LANES * u:LANES * (u + 1)] = lax.dot_general(pt.astype(BF16), xdo, _NN, preferred_element_type=F32)
            dk_ref[:, LANES * u:LANES * (u + 1)] = lax.dot_general(dst, xq, _NN, preferred_element_type=F32)

    return pl.pallas_call(
        body, name=name, grid=(nb,),
        in_specs=_swa_specs(q.shape[1], nb) + _swa_specs(do.shape[1], nb) + stat3() + stat3() + [krow, krow],
        out_specs=[krow, krow],
        out_shape=[_sds(k.shape, F32), _sds(v.shape, F32)],
        compiler_params=_params(("parallel",)),
    )(q, q, q, do, do, do, lse_t, lse_t, lse_t, delta_t, delta_t, delta_t, k, v)


def _peer_of(x, y, c, kk):
    return (x ^ ((kk >> 2) & 1), y ^ ((kk >> 1) & 1), c ^ (kk & 1))


def _own_slot(a, scatter):
    me = 4 * lax.axis_index("x") + 2 * lax.axis_index("y") + lax.axis_index("c")
    shape = a.shape if scatter else (N_DEV,) + a.shape
    own = lax.dynamic_slice_in_dim(a, me, 1, 0) if scatter else a[None]
    return lax.dynamic_update_slice_in_dim(lax.empty(shape, a.dtype), own, me, 0)


def _exchange_start(name, tensors):
    n = len(tensors)
    hbm = pl.BlockSpec(memory_space=pltpu.HBM)
    sem = pl.BlockSpec(memory_space=pltpu.SEMAPHORE)
    srcs = [pltpu.with_memory_space_constraint(a, pltpu.HBM) for a, _ in tensors]
    lands = [pltpu.with_memory_space_constraint(_own_slot(a, sc), pltpu.HBM) for a, sc in tensors]

    def body(*refs):
        ins, dst = refs[:n], refs[n:2 * n]
        send, recv, token = refs[2 * n], refs[2 * n + 1], refs[4 * n + 2]
        x, y, c = lax.axis_index("x"), lax.axis_index("y"), lax.axis_index("c")
        me = 4 * x + 2 * y + c
        for t in range(n):
            for kk in range(1, N_DEV):
                px, py, pc = _peer_of(x, y, c, kk)
                src = ins[t].at[4 * px + 2 * py + pc] if tensors[t][1] else ins[t]
                k1 = t * (N_DEV - 1) + kk - 1
                pltpu.make_async_remote_copy(src_ref=src, dst_ref=dst[t].at[me], send_sem=send.at[k1],
                                             recv_sem=recv.at[k1], device_id=(px, py, pc),
                                             device_id_type=pl.DeviceIdType.MESH).start()
        token[...] = jnp.zeros_like(token)

    out = pl.pallas_call(
        body, name=name,
        in_specs=[hbm] * (2 * n),
        out_specs=[sem, sem] + [hbm] * (2 * n) + [pl.BlockSpec(memory_space=pltpu.VMEM)],
        out_shape=[pltpu.SemaphoreType.DMA((n * (N_DEV - 1),)), pltpu.SemaphoreType.DMA((n * (N_DEV - 1),))]
        + [pltpu.HBM(a.shape, a.dtype) for a in srcs] + [pltpu.HBM(a.shape, a.dtype) for a in lands]
        + [_sds((8, LANES), F32)],
        input_output_aliases={i: i + 2 for i in range(2 * n)},
        compiler_params=pltpu.CompilerParams(has_side_effects=pltpu.SideEffectType.DATAFLOW_SIDE_EFFECTING),
    )(*srcs, *lands)
    return (out[0], out[1], out[2:2 + n], out[2 + n:2 + 2 * n], [sc for _, sc in tensors]), out[2 + 2 * n]


def _exchange_wait(name, started, after):
    send_s, recv_s, srcs, lands, flags = started
    n = len(srcs)
    hbm = pl.BlockSpec(memory_space=pltpu.HBM)
    sem = pl.BlockSpec(memory_space=pltpu.SEMAPHORE)

    def body(*refs):
        ins, dst, send, recv = refs[:n], refs[n:2 * n], refs[2 * n], refs[2 * n + 1]
        x, y, c = lax.axis_index("x"), lax.axis_index("y"), lax.axis_index("c")
        me = 4 * x + 2 * y + c
        for t in range(n):
            for kk in range(1, N_DEV):
                px, py, pc = _peer_of(x, y, c, kk)
                src = ins[t].at[me] if flags[t] else ins[t]
                k1 = t * (N_DEV - 1) + kk - 1
                cp = pltpu.make_async_remote_copy(src_ref=src, dst_ref=dst[t].at[4 * px + 2 * py + pc],
                                                  send_sem=send.at[k1], recv_sem=recv.at[k1],
                                                  device_id=(px, py, pc), device_id_type=pl.DeviceIdType.MESH)
                cp.wait_send()
                cp.wait_recv()

    out = pl.pallas_call(
        body, name=name,
        in_specs=[hbm] * (2 * n) + [sem, sem, pl.BlockSpec(memory_space=pl.ANY)],
        out_specs=[hbm] * (2 * n),
        out_shape=[pltpu.HBM(a.shape, a.dtype) for a in srcs] + [pltpu.HBM(a.shape, a.dtype) for a in lands],
        input_output_aliases={i: i for i in range(2 * n)},
        compiler_params=pltpu.CompilerParams(has_side_effects=pltpu.SideEffectType.DATAFLOW_SIDE_EFFECTING),
    )(*srcs, *lands, send_s, recv_s, after)
    return out[n:]


def _sum8(name, parts):
    _, R, C = parts.shape

    def fn(r, c, o):
        acc = r[0][0].astype(F32)
        for s in range(1, N_DEV):
            acc = acc + r[0][s].astype(F32)
        o[0][...] = acc
        return ()

    tm = R if R % 256 else 256
    return _rowwise(name, fn, [parts], [], [_sds((R, C), F32)], tm=tm)[0]


def _adamw(name, w, g, m, v):
    bc1 = 1.0 - ADAM_B1 ** ADAM_STEP
    bc2 = 1.0 - ADAM_B2 ** ADAM_STEP

    def fn(r, c, o):
        wv, gv, mv, vv = (x[...] for x in r)
        mn = ADAM_B1 * mv + (1.0 - ADAM_B1) * gv
        vn = ADAM_B2 * vv + (1.0 - ADAM_B2) * (gv * gv)
        o[0][...] = -ADAM_LR * ((mn / bc1) / (jnp.sqrt(vn / bc2) + ADAM_EPS) + ADAM_WD * wv)
        o[1][...] = mn
        o[2][...] = vn
        return ()

    tm = max(8, min(512, 1 << ((16 << 20) // (56 * w.shape[1])).bit_length() - 1))
    return _rowwise(name, fn, [w, g, m, v], [], [_sds(w.shape, F32)] * 3, tm=tm)


def _rope_cs(pos, dim, theta):
    inv = jnp.float32(theta) ** (-jnp.arange(0, dim, 2, dtype=jnp.float32) / dim)
    ang = pos.astype(jnp.float32)[:, None] * inv[None, :]
    return jnp.cos(ang), jnp.sin(ang)


def _tables(S):
    pos = jnp.arange(S)
    c, s = _rope_cs(pos, MLA_ROPE, ROPE_THETA)
    mla = (jnp.concatenate([c, c, c, c], 1), jnp.concatenate([-s, s, -s, s], 1))
    rc, rs = _rope_cs(pos // GRID_W, HD // 2, AXIAL_THETA)
    cc, cs = _rope_cs(pos % GRID_W, HD // 2, AXIAL_THETA)
    axial = (jnp.concatenate([rc, rc, cc, cc], 1), jnp.concatenate([-rs, rs, -cs, cs], 1))
    c, s = _rope_cs(pos, SWA_ROT, ROPE_THETA)
    one, zero = jnp.ones((S, SWA_D - SWA_ROT), F32), jnp.zeros((S, SWA_D - SWA_ROT), F32)
    swa = (jnp.concatenate([c, c, one, c, c, one], 1), jnp.concatenate([-s, s, zero, -s, s, zero], 1))
    return mla, axial, swa


_WEIGHTS = ['even_norm', 'even_w_in', 'mla_q_lat_norm', 'mla_kv_lat_norm', 'mla_w_uq', 'mla_w_ukv', 'mla_q_norm',
            'mla_k_nope_norm', 'mla_k_rope_norm', 'gqa_q_norm', 'gqa_k_norm', 'even_w_out', 'odd_norm', 'odd_w_qkv',
            'swa_q_norm', 'swa_k_norm', 'swa_sink', 'odd_w_out', 'mlp_norm', 'mlp_w_up', 'mlp_w_down']
_SMALL = ['even_norm', 'mla_q_lat_norm', 'mla_kv_lat_norm', 'mla_q_norm', 'mla_k_nope_norm', 'mla_k_rope_norm',
          'gqa_q_norm', 'gqa_k_norm', 'swa_q_norm', 'swa_k_norm', 'swa_sink', 'mlp_norm']


def _relu2(acc):
    rl = jnp.maximum(acc, 0.0)
    return rl * rl, rl


def _mul2(acc, rl):
    return (acc * (2.0 * rl.astype(F32)),)


def _add(acc, res):
    return (acc + res,)


def _step(x, tgt, w, m, v):
    S, D = x.shape
    nw = len(_WEIGHTS)
    tab_mla, tab_ax, tab_swa = _tables(S)
    bf = lambda a: a.astype(BF16)

    w_up_s, w_dn_s = w['mlp_w_up'], w['mlp_w_down']
    first, tok0 = _exchange_start("gather_first_start", [
        (bf(w['even_w_in'][0].T), False), (bf(w['mla_w_uq'][0].T), False), (bf(w['mla_w_ukv'][0].T), False)])
    h0 = _rmsnorm("even_norm", x, w['even_norm'], after=[tok0])
    gathered = _exchange_wait("gather_first_wait", first, h0)
    zero = jnp.minimum(jnp.abs(gathered[2][0, 0:1, 0:1].astype(F32)), 0.0)
    later, tok = _exchange_start("gather_rest_start", [
        (bf(w['even_w_out'][0]), False), (bf(w['odd_w_qkv'][0].T), False), (bf(w['odd_w_out'][0]), False),
        (bf(w_up_s[0].T), False), (bf(w_up_s[1].T), False), (bf(w_dn_s[0]), False), (bf(w_dn_s[1]), False),
        (w['odd_norm'] + zero, False)])
    flat = lambda a: a.reshape((a.shape[0] * a.shape[1],) + a.shape[2:])
    win_t, wuq_g, wukv_t = [flat(a) for a in gathered]
    win_t = jnp.concatenate([win_t[:KR_END], jnp.zeros((LANES - MLA_ROPE, D), BF16), win_t[KR_END:]], 0)
    wuq_g = gathered[1]
    wuq_t = jnp.concatenate([wuq_g[:, :HD].reshape(HEADS * HD, Q_LORA),
                             wuq_g[:, HD:].reshape(HEADS * MLA_ROPE, Q_LORA)], 0)

    z64 = jnp.zeros((1, 64), F32)
    qn = w['mla_q_norm']
    g_even = [w['mla_q_lat_norm'], w['mla_kv_lat_norm'], jnp.concatenate([w['mla_k_rope_norm'], z64], 1),
              w['gqa_q_norm'], w['gqa_k_norm']]
    g_mla = [qn[:, :HD], jnp.concatenate([qn[:, HD:], qn[:, HD:]], 1), w['mla_k_nope_norm']]
    g_swa = [jnp.concatenate([w['swa_q_norm']] * 2, 1), jnp.concatenate([w['swa_k_norm']] * 2, 1)]

    def loss_head(acc, res, t):
        e = acc + res - t
        return e * (1.0 / D), _csum(_rsum(e * e))

    def mlp_fwd(l, xin, last=False):
        hn = _rmsnorm(f"mlp{l}_norm", xin, w['mlp_norm'][l:l + 1])
        a, rl = _mm(f"mlp{l}_up", hn, wup_t[l], "nt", (BF16, BF16), epilogue=_relu2)
        if last:
            xout = _mm(f"mlp{l}_down", a, wdn[l], "nn", (F32,), epilogue=loss_head, extras=(xin, tgt),
                       sums=[_sds((1, 1), F32)])
        else:
            xout = _mm(f"mlp{l}_down", a, wdn[l], "nn", (F32,), epilogue=_add, extras=(xin,))
        return xout, (hn, a, rl)

    proj = _mm("even_in", h0, win_t, "nt", (F32,), tm=512, tn=P_END)
    cqn, ckvn, krd, qg, kg, vg = _even_prep("even_prep", proj, tab_mla + tab_ax, g_even, after=[tok])
    qa = _mm("mla_uq", cqn, wuq_t, "nt", (F32,))
    kv = _mm("mla_ukv", ckvn, wukv_t, "nt", (F32,))
    q_a, k_a, v_a = _mla_prep("mla_prep", qa, kv, krd, tab_mla, g_mla)
    o_a, lse_a = _flash_fwd("mla_attn", q_a, k_a, v_a)
    o_g, lse_g = _flash_fwd("gqa_attn", qg, kg, vg)
    merged = jnp.concatenate([o_a, o_g], 1)
    rest = _exchange_wait("gather_rest_wait", later, merged)
    wout_e, wqkv_t, wout_o, wup0_t, wup1_t, wdn0, wdn1 = [flat(a) for a in rest[:7]]
    odd_norm = rest[7].reshape(1, D)
    wup_t, wdn = (wup0_t, wup1_t), (wdn0, wdn1)
    x1 = _mm("even_out", merged, wout_e, "nn", (F32,), epilogue=_add, extras=(x,))
    x2, mlp0 = mlp_fwd(0, x1)

    h1 = _rmsnorm("odd_norm", x2, odd_norm)
    qkv = _mm("odd_qkv", h1, wqkv_t, "nt", (F32,))
    q_s, k_s, v_s = _swa_prep("swa_prep", qkv, tab_swa, g_swa)
    o_s, lse_s = _swa_fwd("swa_attn", q_s, k_s, v_s, w['swa_sink'])
    x3 = _mm("odd_out", o_s, wout_o, "nn", (F32,), epilogue=_add, extras=(x2,))
    (dy, loss_acc), mlp1 = mlp_fwd(1, x3, last=True)
    loss = lax.psum(0.5 / D * loss_acc[0, 0], ("x", "y", "c"))

    gsm = {}

    def mlp_bwd(l, dout, xin, saved):
        hn, a, rl = saved
        du = _mm(f"mlp{l}_dact", dout, wdn[l], "nt", (BF16,), epilogue=_mul2, extras=(rl,))
        g_dn = _mm(f"mlp{l}_gdown", a, dout, "tn", (BF16,))
        g_up = _mm(f"mlp{l}_gup", du, hn, "tn", (BF16,))
        dhn = _mm(f"mlp{l}_dnorm", du, wup_t[l], "nn", (F32,))
        din, g_n = _rmsnorm_bwd(f"mlp{l}_norm_bwd", dout, xin, dhn, w['mlp_norm'][l:l + 1])
        return din, g_dn, g_up, g_n

    dx3, g_dn1, g_up1, g_mn1 = mlp_bwd(1, dy, x3, mlp1)
    split = lambda a: a.reshape((N_DEV, a.shape[0] // N_DEV) + a.shape[1:])
    sc1, tok1 = _exchange_start("scatter_mlp1_start", [(split(g_up1), True), (split(g_dn1), True)])

    g_wout_o = _mm("odd_gout", o_s, dx3, "tn", (BF16,))
    do_s = _mm("odd_dattn", dx3, wout_o, "nt", (BF16,))
    dl_s = _delta_t("swa_delta", do_s, o_s, after=[tok1])
    dq_s, dsink = _swa_bwd_q("swa_bwd_q", q_s, k_s, v_s, do_s, lse_s, dl_s, w['swa_sink'])
    dk_s, dv_s = _swa_bwd_kv("swa_bwd_kv", q_s, k_s, v_s, do_s, lse_s, dl_s)
    dqkv, g_sq, g_sk = _swa_prep_bwd("swa_prep_bwd", dq_s, dk_s, dv_s, qkv, tab_swa, g_swa)
    g_wqkv = _mm("odd_gqkv", dqkv, h1, "tn", (BF16,))
    dh1 = _mm("odd_dnorm", dqkv, wqkv_t, "nn", (F32,), tk=dqkv.shape[1])
    dx2, g_on = _rmsnorm_bwd("odd_norm_bwd", dx3, x2, dh1, odd_norm)
    gsm['swa_q_norm'] = g_sq[:, :64] + g_sq[:, 64:]
    gsm['swa_k_norm'] = g_sk[:, :64] + g_sk[:, 64:]
    gsm['swa_sink'] = dsink[:, 0].reshape(1, SWA_HEADS)

    dx1, g_dn0, g_up0, g_mn0 = mlp_bwd(0, dx2, x1, mlp0)
    sc2, tok2 = _exchange_start("scatter_mid_start", [(split(g_wout_o), True), (split(g_wqkv), True),
                                                      (split(g_up0), True), (split(g_dn0), True)])
    gsm['mlp_norm'] = jnp.concatenate([g_mn0, g_mn1], 0)

    g_wout_e = _mm("even_gout", merged, dx1, "tn", (BF16,))
    dmerged = _mm("even_dattn", dx1, wout_e, "nt", (BF16,))
    dl_e = _delta("even_delta", dmerged, merged, after=[tok2])
    dq_a, dk_a, dv_a = _flash_bwd("mla_attn_bwd", q_a, k_a, v_a, dmerged, 0, lse_a, dl_e)
    dq_g, dk_g, dv_g = _flash_bwd("gqa_attn_bwd", qg, kg, vg, dmerged, HEADS, lse_g, dl_e)
    dqa, dkv, dkrd, g_qnn, g_qnr, g_kn = _mla_prep_bwd("mla_prep_bwd", dq_a, dk_a, dv_a, qa, kv, tab_mla, g_mla)
    g_wuq = _mm("mla_guq", dqa, cqn, "tn", (BF16,))
    dcqn = _mm("mla_dq_lat", dqa, wuq_t, "nn", (F32,))
    g_wukv = _mm("mla_gukv", dkv, ckvn, "tn", (BF16,))
    dckvn = _mm("mla_dkv_lat", dkv, wukv_t, "nn", (F32,))
    g_wuq = jnp.concatenate([g_wuq[:HEADS * HD].reshape(HEADS, HD, Q_LORA),
                             g_wuq[HEADS * HD:].reshape(HEADS, MLA_ROPE, Q_LORA)], 1)
    sc3, tok_e = _exchange_start("scatter_even_start", [(g_wuq, True), (split(g_wukv), True), (split(g_wout_e), True)])
    dproj, g_ql, g_kvl, g_kr, g_gq, g_gk = _even_prep_bwd("even_prep_bwd", dcqn, dckvn, dkrd, dq_g, dk_g, dv_g,
                                                          proj, tab_mla + tab_ax, g_even, after=[tok_e])
    g_win = _mm("even_gin", dproj, h0, "tn", (BF16,), tm=P_END, tk=1024)
    dh0 = _mm("even_dnorm", dproj, win_t, "nn", (F32,), tk=P_END)
    grad_x, g_en = _rmsnorm_bwd("even_norm_bwd", dx1, x, dh0, w['even_norm'])
    gsm.update(even_norm=g_en, mla_q_lat_norm=g_ql, mla_kv_lat_norm=g_kvl,
               mla_q_norm=jnp.concatenate([g_qnn, g_qnr[:, :64] + g_qnr[:, 64:]], 1), mla_k_nope_norm=g_kn,
               mla_k_rope_norm=g_kr[:, :64], gqa_q_norm=g_gq, gqa_k_norm=g_gk)

    g_win = jnp.concatenate([g_win[:KR_END], g_win[P_QG:]], 0)
    small_sizes = [w[n].size for n in _SMALL] + [D]
    small_vec = jnp.concatenate([gsm[n].reshape(1, -1) for n in _SMALL] + [g_on], 1)
    pad = (-small_vec.shape[1]) % LANES
    small_vec = jnp.pad(small_vec, ((0, 0), (0, pad)))
    last_st, tok3 = _exchange_start("scatter_last_start", [(split(g_win), True), (small_vec, False)])
    p_up1, p_dn1 = _exchange_wait("scatter_mlp1_wait", sc1, tok3)
    p_wout_o, p_wqkv, p_up0, p_dn0 = _exchange_wait("scatter_mid_wait", sc2, tok3)
    p_wuq, p_wukv, p_wout_e = _exchange_wait("scatter_even_wait", sc3, tok3)
    parts = dict(odd_w_qkv=p_wqkv, odd_w_out=p_wout_o, up0=p_up0, up1=p_up1, dn0=p_dn0, dn1=p_dn1,
                 mla_w_uq=p_wuq, mla_w_ukv=p_wukv, even_w_out=p_wout_e)
    red = {n: _sum8("sum_" + n, p) for n, p in parts.items()}
    grads = {
        'odd_w_qkv': red['odd_w_qkv'].T[None], 'odd_w_out': red['odd_w_out'][None],
        'mlp_w_up': jnp.stack([red['up0'].T, red['up1'].T]), 'mlp_w_down': jnp.stack([red['dn0'], red['dn1']]),
        'mla_w_uq': red['mla_w_uq'].T[None], 'mla_w_ukv': red['mla_w_ukv'].T[None],
        'even_w_out': red['even_w_out'][None],
    }
    delta, new_m, new_v = {}, {}, {}

    def apply(names):
        for n in names:
            shp = w[n].shape
            two = lambda a: a.reshape(shp[0] * shp[1], shp[2])
            d_, m_, v_ = _adamw("adamw_" + n, two(w[n]), two(grads[n]), two(m[n]), two(v[n]))
            delta[n], new_m[n], new_v[n] = d_.reshape(shp), m_.reshape(shp), v_.reshape(shp)

    apply(['mla_w_uq', 'mla_w_ukv', 'even_w_out', 'odd_w_qkv', 'odd_w_out', 'mlp_w_up', 'mlp_w_down'])
    last = _exchange_wait("scatter_last_wait", last_st, new_v['mlp_w_down'])
    small_g = _sum8("sum_small", last[1])
    grads['even_w_in'] = _sum8("sum_even_w_in", last[0]).T[None]
    apply(['even_w_in'])
    off = 0
    for n, sz in zip(_SMALL + ['odd_norm_full'], small_sizes):
        seg = small_g[:, off:off + sz]
        off += sz
        if n == 'odd_norm_full':
            me = 4 * lax.axis_index("x") + 2 * lax.axis_index("y") + lax.axis_index("c")
            grads['odd_norm'] = lax.dynamic_slice(seg, (0, me * (D // N_DEV)), (1, D // N_DEV))
        else:
            grads[n] = seg.reshape(w[n].shape)

    sm_names = _SMALL + ['odd_norm']
    pack = lambda d: jnp.concatenate([d[n].reshape(1, -1) for n in sm_names], 1)
    pw, pg, pm, pv = pack(w), pack(grads), pack(m), pack(v)
    padw = (-pw.shape[1]) % LANES
    padf = lambda a: jnp.pad(a, ((0, 0), (0, padw)))
    d_, m_, v_ = _adamw("adamw_small", padf(pw), padf(pg), padf(pm), jnp.pad(pv, ((0, 0), (0, padw)), constant_values=1.0))
    off = 0
    for n in sm_names:
        sz = w[n].size
        delta[n] = d_[:, off:off + sz].reshape(w[n].shape)
        new_m[n] = m_[:, off:off + sz].reshape(w[n].shape)
        new_v[n] = v_[:, off:off + sz].reshape(w[n].shape)
        off += sz

    return (loss, grad_x[None], *[grads[n] for n in _WEIGHTS], *[delta[n] for n in _WEIGHTS],
            *[new_m[n] for n in _WEIGHTS], *[new_v[n] for n in _WEIGHTS])


def kernel(x, even_norm, even_w_in, mla_q_lat_norm, mla_kv_lat_norm, mla_w_uq, mla_w_ukv, mla_q_norm, mla_k_nope_norm, mla_k_rope_norm, gqa_q_norm, gqa_k_norm, even_w_out, odd_norm, odd_w_qkv, swa_q_norm, swa_k_norm, swa_sink, odd_w_out, mlp_norm, mlp_w_up, mlp_w_down, loss_target, m_even_norm, m_even_w_in, m_mla_q_lat_norm, m_mla_kv_lat_norm, m_mla_w_uq, m_mla_w_ukv, m_mla_q_norm, m_mla_k_nope_norm, m_mla_k_rope_norm, m_gqa_q_norm, m_gqa_k_norm, m_even_w_out, m_odd_norm, m_odd_w_qkv, m_swa_q_norm, m_swa_k_norm, m_swa_sink, m_odd_w_out, m_mlp_norm, m_mlp_w_up, m_mlp_w_down, v_even_norm, v_even_w_in, v_mla_q_lat_norm, v_mla_kv_lat_norm, v_mla_w_uq, v_mla_w_ukv, v_mla_q_norm, v_mla_k_nope_norm, v_mla_k_rope_norm, v_gqa_q_norm, v_gqa_k_norm, v_even_w_out, v_odd_norm, v_odd_w_qkv, v_swa_q_norm, v_swa_k_norm, v_swa_sink, v_odd_w_out, v_mlp_norm, v_mlp_w_up, v_mlp_w_down):
    ws = (even_norm, even_w_in, mla_q_lat_norm, mla_kv_lat_norm, mla_w_uq, mla_w_ukv, mla_q_norm, mla_k_nope_norm, mla_k_rope_norm, gqa_q_norm, gqa_k_norm, even_w_out, odd_norm, odd_w_qkv, swa_q_norm, swa_k_norm, swa_sink, odd_w_out, mlp_norm, mlp_w_up, mlp_w_down)
    ms = (m_even_norm, m_even_w_in, m_mla_q_lat_norm, m_mla_kv_lat_norm, m_mla_w_uq, m_mla_w_ukv, m_mla_q_norm, m_mla_k_nope_norm, m_mla_k_rope_norm, m_gqa_q_norm, m_gqa_k_norm, m_even_w_out, m_odd_norm, m_odd_w_qkv, m_swa_q_norm, m_swa_k_norm, m_swa_sink, m_odd_w_out, m_mlp_norm, m_mlp_w_up, m_mlp_w_down)
    vs = (v_even_norm, v_even_w_in, v_mla_q_lat_norm, v_mla_kv_lat_norm, v_mla_w_uq, v_mla_w_ukv, v_mla_q_norm, v_mla_k_nope_norm, v_mla_k_rope_norm, v_gqa_q_norm, v_gqa_k_norm, v_even_w_out, v_odd_norm, v_odd_w_qkv, v_swa_q_norm, v_swa_k_norm, v_swa_sink, v_odd_w_out, v_mlp_norm, v_mlp_w_up, v_mlp_w_down)
    return _step(x[0], loss_target[0], dict(zip(_WEIGHTS, ws)), dict(zip(_WEIGHTS, ms)), dict(zip(_WEIGHTS, vs)))
```

```python
import jax
import jax.numpy as jnp
from jax import lax
from jax.experimental import pallas as pl
from jax.experimental.pallas import tpu as pltpu

F32 = jnp.float32
BF16 = jnp.bfloat16

N_DEV = 8
NORM_EPS = 1e-6
ROPE_THETA = 500000.0
AXIAL_THETA = 10000.0
GRID_W = 64
HEADS = 8
GQA_KV = 2
HD = 128
MLA_ROPE = 64
MLA_QK = HD + MLA_ROPE
Q_LORA = 512
KV_LORA = 256
SWA_HEADS = 32
SWA_KV = 4
SWA_D = 64
SWA_ROT = 16
SWA_WINDOW = 128
SWA_BLOCK = 128
MLA_SCALE, GQA_SCALE, SWA_SCALE = MLA_QK ** -0.5, HD ** -0.5, SWA_D ** -0.5
LANES = 128
ADAM_LR, ADAM_B1, ADAM_B2, ADAM_EPS, ADAM_WD, ADAM_STEP = 0.001, 0.9, 0.999, 1e-08, 0.01, 10
VMEM_LIMIT = 56 * 1024 * 1024
MM_TILE = (1024, 1024, 2048)
FWD_TILE = (1024, 8192)
BWD_TILE = (1024, 1024)
ROW_TILE = 256

P_CQ, P_CKV, P_KR, P_QG = 0, Q_LORA, Q_LORA + KV_LORA, Q_LORA + KV_LORA + LANES
P_KG = P_QG + HEADS * HD
P_VG = P_KG + GQA_KV * HD
P_END = P_VG + GQA_KV * HD
KR_END = Q_LORA + KV_LORA + MLA_ROPE


def _tile(n, prefs):
    for t in prefs:
        if n % t == 0 and t <= n:
            return t
    return n


def _params(sem):
    return pltpu.CompilerParams(dimension_semantics=sem, vmem_limit_bytes=VMEM_LIMIT)


_DIMS = {"nn": ((1,), (0,)), "nt": ((1,), (1,)), "tn": ((0,), (0,))}


def _mm(name, a, b, mode, out_dtypes, epilogue=None, extras=(), sums=(), tm=MM_TILE[0], tn=MM_TILE[1], tk=MM_TILE[2]):
    if mode == "nn":
        (M, K), (_, N) = a.shape, b.shape
    elif mode == "nt":
        (M, K), (N, _) = a.shape, b.shape
    else:
        (K, M), (_, N) = a.shape, b.shape
    tm = _tile(M, (tm, 512, 256, 128))
    tn = _tile(N, (tn, 512, 256, 128))
    tk = _tile(K, (tk, 1024, 512, 256, 128))
    nk = K // tk
    ne, no, ns = len(extras), len(out_dtypes), len(sums)
    if mode == "tn":
        a_spec = pl.BlockSpec((tk, tm), lambda i, j, k: (k, i))
    else:
        a_spec = pl.BlockSpec((tm, tk), lambda i, j, k: (i, k))
    if mode == "nt":
        b_spec = pl.BlockSpec((tn, tk), lambda i, j, k: (j, k))
    else:
        b_spec = pl.BlockSpec((tk, tn), lambda i, j, k: (k, j))
    o_spec = pl.BlockSpec((tm, tn), lambda i, j, k: (i, j))
    dims = (_DIMS[mode], ((), ()))

    def body(a_ref, b_ref, *rest):
        ex, outs = rest[:ne], rest[ne:ne + no]
        k = pl.program_id(2)
        part = lax.dot_general(a_ref[...].astype(BF16), b_ref[...].astype(BF16), dims, preferred_element_type=F32)

        def finish(total):
            res = epilogue(total, *[e[...] for e in ex]) if epilogue else (total,)
            for o, r in zip(outs, res[:no]):
                o[...] = r.astype(o.dtype)
            if ns:
                first = (pl.program_id(0) == 0) & (pl.program_id(1) == 0)

                @pl.when(first)
                def _():
                    for s_ref, r in zip(rest[ne + no:ne + no + ns], res[no:]):
                        s_ref[...] = r

                @pl.when(jnp.logical_not(first))
                def _():
                    for s_ref, r in zip(rest[ne + no:ne + no + ns], res[no:]):
                        s_ref[...] += r

        if nk == 1:
            finish(part)
            return
        acc = rest[ne + no + ns]

        @pl.when(k == 0)
        def _():
            acc[...] = part

        @pl.when((k > 0) & (k < nk - 1))
        def _():
            acc[...] += part

        @pl.when(k == nk - 1)
        def _():
            finish(acc[...] + part)

    out = pl.pallas_call(
        body, name=name, grid=(M // tm, N // tn, nk),
        in_specs=[a_spec, b_spec] + [o_spec] * ne,
        out_specs=[o_spec] * no + [pl.BlockSpec(tuple(s.shape), lambda i, j, k: (0,) * len(s.shape)) for s in sums],
        out_shape=[jax.ShapeDtypeStruct((M, N), d) for d in out_dtypes] + list(sums),
        scratch_shapes=[pltpu.VMEM((tm, tn), F32)] if nk > 1 else [],
        compiler_params=_params(("arbitrary",) * 3 if ns else ("parallel", "parallel", "arbitrary")),
    )(a, b, *extras)
    return out[0] if no + ns == 1 else out


def _rowwise(name, fn, rows, consts, outs, accs=(), tm=ROW_TILE, after=()):
    S = rows[0].shape[-2]
    tm = _tile(S, (tm, 128, 64, 32, 16, 8))
    nr, nc, nd, no, na = len(rows), len(consts), len(after), len(outs), len(accs)

    def rspec(shape):
        if len(shape) == 2:
            return pl.BlockSpec((tm, shape[1]), lambda i: (i, 0))
        return pl.BlockSpec((shape[0], tm, shape[2]), lambda i: (0, i, 0))

    def cspec(shape):
        return pl.BlockSpec(tuple(shape), lambda i: (0,) * len(shape))

    def ospec(o):
        if isinstance(o, tuple):
            return pl.BlockSpec((o[1].shape[0], o[1].shape[1], tm), lambda i: (0, 0, i))
        return rspec(o.shape)

    out_sds = [o[1] if isinstance(o, tuple) else o for o in outs]

    def body(*refs):
        r, c = refs[:nr], refs[nr:nr + nc]
        o, a = refs[nr + nc + nd:nr + nc + nd + no], refs[nr + nc + nd + no:]
        vals = fn(r, c, o)
        if na:
            @pl.when(pl.program_id(0) == 0)
            def _():
                for ar in a:
                    ar[...] = jnp.zeros_like(ar)

            for ar, v in zip(a, vals):
                ar[...] += v

    res = pl.pallas_call(
        body, name=name, grid=(S // tm,),
        in_specs=[rspec(x.shape) for x in rows] + [cspec(x.shape) for x in consts]
        + [pl.BlockSpec(memory_space=pl.ANY)] * len(after),
        out_specs=[ospec(x) for x in outs] + [cspec(x.shape) for x in accs],
        out_shape=out_sds + list(accs),
        compiler_params=_params(("arbitrary",) if na else ("parallel",)),
    )(*rows, *consts, *after)
    return res


def _sds(shape, dtype):
    return jax.ShapeDtypeStruct(tuple(shape), dtype)


def _rsum(x):
    return jnp.sum(x, axis=-1, keepdims=True)


def _csum(x):
    return jnp.sum(x, axis=0, keepdims=True)


def _lane(shape):
    return lax.broadcasted_iota(jnp.int32, shape, 1)


def _partner(x, seg, half):
    lane = _lane(x.shape) % seg
    return jnp.where(lane < half, pltpu.roll(x, LANES - half, 1), pltpu.roll(x, half, 1))


def _rope(x, c, s, seg, half):
    return x * c + _partner(x, seg, half) * s


def _rope_bwd(dy, c, s, seg, half):
    t = _partner(dy * s, seg, half)
    if seg != 2 * half:
        t = jnp.where(_lane(dy.shape) % seg < 2 * half, t, 0.0)
    return dy * c + t


def _rms_r(ss, n):
    return lax.rsqrt(ss * (1.0 / n) + NORM_EPS)


def _rms_bwd(x, r, g, dy, dot_scale):
    dyg = dy * g
    return r * dyg - x * (r * r * r) * (_rsum(dyg * x) * dot_scale)


def _rmsnorm(name, x, g, after=()):
    D = x.shape[1]

    def fn(r, c, o):
        xv = r[0][...]
        o[0][...] = (xv * _rms_r(_rsum(xv * xv), D) * c[0][...]).astype(BF16)
        return ()

    return _rowwise(name, fn, [x], [g], [_sds(x.shape, BF16)], tm=4 * ROW_TILE, after=after)[0]


def _rmsnorm_bwd(name, dres, x, dh, g):
    D = x.shape[1]

    def fn(r, c, o):
        xv, dhv, gv = r[1][...], r[2][...], c[0][...]
        rr = _rms_r(_rsum(xv * xv), D)
        o[0][...] = r[0][...] + _rms_bwd(xv, rr, gv, dhv, 1.0 / D)
        return (_csum(dhv * xv * rr),)

    return _rowwise(name, fn, [dres, x, dh], [g], [_sds(x.shape, F32)], [_sds((1, D), F32)], tm=2 * ROW_TILE)


def _even_prep(name, proj, tabs, gains, after=()):
    S = proj.shape[0]
    cm, sm, ca, sa = tabs

    def fn(r, c, o):
        p, cmv, smv, cav, sav = r[0], r[1][...], r[2][...], r[3][...], r[4][...]
        g_ql, g_kvl, g_kr, g_q, g_k = (x[...] for x in c)
        ones = _seg_ones(LANES)
        cq = p[:, P_CQ:P_CKV]
        o[0][...] = (cq * _wide(_rms_r(_seg_sum(_fold(cq * cq), ones), Q_LORA), Q_LORA // LANES) * g_ql).astype(BF16)
        ckv = p[:, P_CKV:P_KR]
        o[1][...] = (ckv * _wide(_rms_r(_seg_sum(_fold(ckv * ckv), ones), KV_LORA), KV_LORA // LANES) * g_kvl).astype(BF16)
        kr = p[:, P_KR:P_QG]
        y = _rope(kr * _rms_r(_seg_sum(kr * kr, ones), MLA_ROPE) * g_kr, cmv, smv, 64, 32)
        o[2][...] = (y + pltpu.roll(y, 64, 1)).astype(BF16)
        for h in range(HEADS):
            xh = p[:, P_QG + HD * h:P_QG + HD * (h + 1)]
            o[3][h] = (_rope(xh * _rms_r(_seg_sum(xh * xh, ones), HD) * g_q, cav, sav, 64, 32) * GQA_SCALE).astype(BF16)
        for h in range(GQA_KV):
            xh = p[:, P_KG + HD * h:P_KG + HD * (h + 1)]
            o[4][h] = _rope(xh * _rms_r(_seg_sum(xh * xh, ones), HD) * g_k, cav, sav, 64, 32).astype(BF16)
            o[5][h] = p[:, P_VG + HD * h:P_VG + HD * (h + 1)].astype(BF16)
        return ()

    outs = [_sds((S, Q_LORA), BF16), _sds((S, KV_LORA), BF16), _sds((S, LANES), BF16),
            _sds((HEADS, S, HD), BF16), _sds((GQA_KV, S, HD), BF16), _sds((GQA_KV, S, HD), BF16)]
    return _rowwise(name, fn, [proj, cm, sm, ca, sa], gains, outs, tm=2 * ROW_TILE, after=after)


def _even_prep_bwd(name, dcqn, dckvn, dkrd, dqg, dkg, dvg, proj, tabs, gains, after=()):
    S = proj.shape[0]
    cm, sm, ca, sa = tabs

    def fn(r, c, o):
        p, cmv, smv, cav, sav = r[6], r[7][...], r[8][...], r[9][...], r[10][...]
        g_ql, g_kvl, g_kr, g_q, g_k = (x[...] for x in c)
        out = o[0]
        ones = _seg_ones(LANES)

        def wide_bwd(x, g, d, n):
            w_ = n // LANES
            rr = _wide(_rms_r(_seg_sum(_fold(x * x), ones), n), w_)
            dot = _wide(_seg_sum(_fold(d * g * x), ones) * (1.0 / n), w_)
            return _rms_bwd_b(x, rr, g, d, dot).astype(BF16), _csum(d * x * rr)

        cq = p[:, P_CQ:P_CKV]
        out[:, P_CQ:P_CKV], a_ql = wide_bwd(cq, g_ql, r[0][...], Q_LORA)
        ckv = p[:, P_CKV:P_KR]
        out[:, P_CKV:P_KR], a_kvl = wide_bwd(ckv, g_kvl, r[1][...], KV_LORA)
        kr = p[:, P_KR:P_QG]
        rr = _rms_r(_seg_sum(kr * kr, ones), MLA_ROPE)
        d = r[2][...]
        d = d + pltpu.roll(d, 64, 1)
        d = _rope_bwd(d, cmv, smv, 64, 32)
        low = _lane(d.shape) < 64
        dot = _seg_sum(jnp.where(low, d * g_kr * kr, 0.0), ones) * (1.0 / MLA_ROPE)
        out[:, P_KR:P_QG] = jnp.where(low, _rms_bwd_b(kr, rr, g_kr, d, dot), 0.0).astype(BF16)
        a_kr = _csum(d * kr * rr)
        a_q = jnp.zeros((1, HD), F32)

        def head_bwd(xh, g, d):
            rr = _rms_r(_seg_sum(xh * xh, ones), HD)
            dot = _seg_sum(d * g * xh, ones) * (1.0 / HD)
            return _rms_bwd_b(xh, rr, g, d, dot).astype(BF16), _csum(d * xh * rr)

        for h in range(HEADS):
            xh = p[:, P_QG + HD * h:P_QG + HD * (h + 1)]
            d = _rope_bwd(r[3][h] * GQA_SCALE, cav, sav, 64, 32)
            out[:, P_QG + HD * h:P_QG + HD * (h + 1)], inc = head_bwd(xh, g_q, d)
            a_q = a_q + inc
        a_k = jnp.zeros((1, HD), F32)
        grp = HEADS // GQA_KV
        for h in range(GQA_KV):
            xh = p[:, P_KG + HD * h:P_KG + HD * (h + 1)]
            d = r[4][grp * h]
            dv = r[5][grp * h]
            for e in range(1, grp):
                d = d + r[4][grp * h + e]
                dv = dv + r[5][grp * h + e]
            d = _rope_bwd(d, cav, sav, 64, 32)
            out[:, P_KG + HD * h:P_KG + HD * (h + 1)], inc = head_bwd(xh, g_k, d)
            out[:, P_VG + HD * h:P_VG + HD * (h + 1)] = dv.astype(BF16)
            a_k = a_k + inc
        return (a_ql, a_kvl, a_kr, a_q, a_k)

    accs = [_sds((1, Q_LORA), F32), _sds((1, KV_LORA), F32), _sds((1, LANES), F32),
            _sds((1, HD), F32), _sds((1, HD), F32)]
    return _rowwise(name, fn, [dcqn, dckvn, dkrd, dqg, dkg, dvg, proj, cm, sm, ca, sa], gains,
                    [_sds((S, P_END), BF16)], accs, after=after)


def _mla_q_parts(qa, h, rp, ones):
    nope = qa[:, HD * h:HD * (h + 1)]
    mine = (_lane(rp.shape) >= 64) == bool(h % 2)
    rpm = jnp.where(mine, rp, 0.0)
    rr = _rms_r(_seg_sum(nope * nope + rpm * rpm, ones), MLA_QK)
    return nope, rpm, mine, rr


def _mla_prep(name, qa, kv, krd, tabs, gains):
    S = qa.shape[0]
    cm, sm = tabs

    def fn(r, c, o):
        qa_r, kv_r, krd_v, cmv, smv = r[0], r[1], r[2][...], r[3][...], r[4][...]
        g_n, g_r, g_kn = (x[...] for x in c)
        ones = _seg_ones(LANES)
        for h in range(HEADS):
            rp = qa_r[:, HEADS * HD + LANES * (h // 2):HEADS * HD + LANES * (h // 2 + 1)]
            nope, rpm, mine, rr = _mla_q_parts(qa_r, h, rp, ones)
            o[0][h, :, 0:HD] = (nope * (rr * MLA_SCALE) * g_n).astype(BF16)
            o[0][h, :, HD:2 * HD] = _rope(rpm * (rr * MLA_SCALE) * g_r, cmv, smv, 64, 32).astype(BF16)
            kn = kv_r[:, 2 * HD * h:2 * HD * h + HD]
            o[1][h, :, 0:HD] = (kn * _rms_r(_seg_sum(kn * kn, ones), HD) * g_kn).astype(BF16)
            o[1][h, :, HD:2 * HD] = krd_v
            o[2][h] = kv_r[:, 2 * HD * h + HD:2 * HD * (h + 1)].astype(BF16)
        return ()

    outs = [_sds((HEADS, S, 2 * HD), BF16), _sds((HEADS, S, 2 * HD), BF16), _sds((HEADS, S, HD), BF16)]
    return _rowwise(name, fn, [qa, kv, krd, cm, sm], gains, outs, tm=2 * ROW_TILE)


def _mla_prep_bwd(name, dq, dk, dv, qa, kv, tabs, gains):
    S = qa.shape[0]
    cm, sm = tabs

    def fn(r, c, o):
        dq_r, dk_r, dv_r, qa_r, kv_r, cmv, smv = r[0], r[1], r[2], r[3], r[4], r[5][...], r[6][...]
        g_n, g_r, g_kn = (x[...] for x in c)
        a_n = jnp.zeros((1, HD), F32)
        a_r = jnp.zeros((1, LANES), F32)
        a_kn = jnp.zeros((1, HD), F32)
        dkrd = jnp.zeros(cmv.shape, F32)
        drp = None
        ones = _seg_ones(LANES)
        for h in range(HEADS):
            rp = qa_r[:, HEADS * HD + LANES * (h // 2):HEADS * HD + LANES * (h // 2 + 1)]
            nope, rpm, mine, rr = _mla_q_parts(qa_r, h, rp, ones)
            dn = dq_r[h, :, 0:HD] * MLA_SCALE
            dr = _rope_bwd(jnp.where(mine, dq_r[h, :, HD:2 * HD] * MLA_SCALE, 0.0), cmv, smv, 64, 32)
            dot = _seg_sum(dn * g_n * nope + dr * g_r * rpm, ones) * (1.0 / MLA_QK)
            r3 = rr * rr * rr
            o[0][:, HD * h:HD * (h + 1)] = (rr * dn * g_n - nope * r3 * dot).astype(BF16)
            part = jnp.where(mine, rr * dr * g_r - rpm * r3 * dot, 0.0)
            drp = part if h % 2 == 0 else drp + part
            if h % 2 == 1:
                o[0][:, HEADS * HD + LANES * (h // 2):HEADS * HD + LANES * (h // 2 + 1)] = drp.astype(BF16)
            a_n = a_n + _csum(dn * nope * rr)
            a_r = a_r + _csum(dr * rpm * rr)
            kn = kv_r[:, 2 * HD * h:2 * HD * h + HD]
            rk = _rms_r(_seg_sum(kn * kn, ones), HD)
            dkn = dk_r[h, :, 0:HD]
            dotk = _seg_sum(dkn * g_kn * kn, ones) * (1.0 / HD)
            o[1][:, 2 * HD * h:2 * HD * h + HD] = _rms_bwd_b(kn, rk, g_kn, dkn, dotk).astype(BF16)
            o[1][:, 2 * HD * h + HD:2 * HD * (h + 1)] = dv_r[h].astype(BF16)
            a_kn = a_kn + _csum(dkn * kn * rk)
            dkrd = dkrd + dk_r[h, :, HD:2 * HD]
        o[2][...] = dkrd
        return (a_n, a_r, a_kn)

    outs = [_sds(qa.shape, BF16), _sds(kv.shape, BF16), _sds((S, LANES), F32)]
    accs = [_sds((1, HD), F32), _sds((1, LANES), F32), _sds((1, HD), F32)]
    return _rowwise(name, fn, [dq, dk, dv, qa, kv, cm, sm], gains, outs, accs)


def _seg_ones(seg):
    r = lax.broadcasted_iota(jnp.int32, (LANES, LANES), 0) // seg
    c = lax.broadcasted_iota(jnp.int32, (LANES, LANES), 1) // seg
    return (r == c).astype(F32)


def _dot01(x, sel, dims):
    hi = x.astype(BF16)
    lo = (x - hi.astype(F32)).astype(BF16)
    sb = sel.astype(BF16)
    if dims is _NN:
        return (lax.dot_general(hi, sb, dims, preferred_element_type=F32)
                + lax.dot_general(lo, sb, dims, preferred_element_type=F32))
    return (lax.dot_general(sb, hi, dims, preferred_element_type=F32)
            + lax.dot_general(sb, lo, dims, preferred_element_type=F32))


def _seg_sum(x, ones):
    return _dot01(x, ones, _NN)


def _fold(x):
    acc = x[:, 0:LANES]
    for g in range(1, x.shape[1] // LANES):
        acc = acc + x[:, LANES * g:LANES * (g + 1)]
    return acc


def _wide(v, n):
    return jnp.concatenate([v] * n, axis=1) if n > 1 else v


def _rms_bwd_b(x, r, g, dy, dot_b):
    dyg = dy * g
    return r * dyg - x * (r * r * r) * dot_b


def _seg64_r(x, ones):
    return _rms_r(_seg_sum(x * x, ones), SWA_D)


def _swa_prep(name, qkv, tabs, gains):
    S = qkv.shape[0]
    nq, nk = SWA_HEADS * SWA_D, SWA_KV * SWA_D
    cs, ss = tabs

    def fn(r, c, o):
        x_r, csv, ssv = r[0], r[1][...], r[2][...]
        g_q, g_k = c[0][...], c[1][...]
        ones = _seg_ones(SWA_D)
        for g in range((nq + nk) // LANES):
            x = x_r[:, LANES * g:LANES * (g + 1)]
            rr = _seg64_r(x, ones)
            y = _rope(x * rr * (g_q if g < nq // LANES else g_k), csv, ssv, SWA_D, SWA_ROT // 2)
            if g < nq // LANES:
                o[0][:, LANES * g:LANES * (g + 1)] = (y * SWA_SCALE).astype(BF16)
            else:
                o[1][:, LANES * g - nq:LANES * (g + 1) - nq] = y.astype(BF16)
        o[2][...] = x_r[:, nq + nk:nq + 2 * nk].astype(BF16)
        return ()

    outs = [_sds((S, nq), BF16), _sds((S, nk), BF16), _sds((S, nk), BF16)]
    return _rowwise(name, fn, [qkv, cs, ss], gains, outs, tm=2 * ROW_TILE)


def _swa_prep_bwd(name, dq, dk, dv, qkv, tabs, gains):
    nq, nk = SWA_HEADS * SWA_D, SWA_KV * SWA_D
    cs, ss = tabs

    def fn(r, c, o):
        dq_r, dk_r, x_r, csv, ssv = r[0], r[1], r[3], r[4][...], r[5][...]
        g_q, g_k = c[0][...], c[1][...]
        acc = [jnp.zeros((1, LANES), F32), jnp.zeros((1, LANES), F32)]
        ones = _seg_ones(SWA_D)
        for g in range((nq + nk) // LANES):
            isq = g < nq // LANES
            x = x_r[:, LANES * g:LANES * (g + 1)]
            rr = _seg64_r(x, ones)
            d = dq_r[:, LANES * g:LANES * (g + 1)] * SWA_SCALE if isq else dk_r[:, LANES * g - nq:LANES * (g + 1) - nq]
            d = _rope_bwd(d, csv, ssv, SWA_D, SWA_ROT // 2)
            dyg = d * (g_q if isq else g_k)
            dot = _seg_sum(dyg * x, ones) * (1.0 / SWA_D)
            o[0][:, LANES * g:LANES * (g + 1)] = (rr * dyg - x * (rr * rr * rr) * dot).astype(BF16)
            acc[0 if isq else 1] = acc[0 if isq else 1] + _csum(d * x * rr)
        o[0][:, nq + nk:nq + 2 * nk] = r[2][...].astype(BF16)
        return tuple(acc)

    return _rowwise(name, fn, [dq, dk, dv, qkv, cs, ss], gains, [_sds(qkv.shape, BF16)],
                    [_sds((1, LANES), F32), _sds((1, LANES), F32)])


def _delta(name, do, o_, after=()):
    S, C = do.shape

    def fn(r, c, o):
        for g in range(C // LANES):
            t = r[0][:, LANES * g:LANES * (g + 1)].astype(F32) * r[1][:, LANES * g:LANES * (g + 1)].astype(F32)
            o[0][g] = _rsum(t)
        return ()

    return _rowwise(name, fn, [do, o_], [], [_sds((C // LANES, S, 1), F32)], after=after)[0]


def _delta_t(name, do, o_, after=()):
    S, C = do.shape

    def fn(r, c, o):
        row = lax.broadcasted_iota(jnp.int32, (8, LANES), 0)
        sel = (row == _lane((8, LANES)) // SWA_D).astype(F32)
        for g in range(C // LANES):
            t = r[0][:, LANES * g:LANES * (g + 1)].astype(F32) * r[1][:, LANES * g:LANES * (g + 1)].astype(F32)
            both = _dot01(t, sel, _NT)
            o[0][2 * g] = both[0:1, :]
            o[0][2 * g + 1] = both[1:2, :]
        return ()

    return _rowwise(name, fn, [do, o_], [], [("T", _sds((2 * C // LANES, 1, S), F32))], after=after)[0]


_NT = (((1,), (1,)), ((), ()))
_NN = (((1,), (0,)), ((), ()))
_TN = (((0,), (0,)), ((), ()))


def _flash_fwd(name, q, k, v, tq=FWD_TILE[0], tk=FWD_TILE[1]):
    H, S, dk = q.shape
    G = H // k.shape[0]
    dv = v.shape[2]
    tq, tk = _tile(S, (tq, 256, 128)), _tile(S, (tk, 256, 128))
    nk = S // tk

    def body(q_ref, k_ref, v_ref, o_ref, lse_ref, m_s, l_s, acc_s):
        j = pl.program_id(2)

        @pl.when(j == 0)
        def _():
            m_s[...] = jnp.full_like(m_s, -jnp.inf)
            l_s[...] = jnp.zeros_like(l_s)
            acc_s[...] = jnp.zeros_like(acc_s)

        s = lax.dot_general(q_ref[...], k_ref[...], _NT, preferred_element_type=F32)
        m_new = jnp.maximum(m_s[...], jnp.max(s, axis=-1, keepdims=True))
        alpha = jnp.exp(m_s[...] - m_new)
        p = jnp.exp(s - m_new)
        l_s[...] = alpha * l_s[...] + _rsum(p)
        acc_s[...] = alpha * acc_s[...] + lax.dot_general(p.astype(BF16), v_ref[...], _NN,
                                                          preferred_element_type=F32)
        m_s[...] = m_new

        @pl.when(j == nk - 1)
        def _():
            o_ref[...] = (acc_s[...] / l_s[...]).astype(o_ref.dtype)
            lse_ref[...] = m_s[...] + jnp.log(l_s[...])

    return pl.pallas_call(
        body, name=name, grid=(H, S // tq, nk),
        in_specs=[pl.BlockSpec((None, tq, dk), lambda h, i, j: (h, i, 0)),
                  pl.BlockSpec((None, tk, dk), lambda h, i, j: (h // G, j, 0)),
                  pl.BlockSpec((None, tk, dv), lambda h, i, j: (h // G, j, 0))],
        out_specs=[pl.BlockSpec((tq, dv), lambda h, i, j: (i, h)),
                   pl.BlockSpec((None, tq, 1), lambda h, i, j: (h, i, 0))],
        out_shape=[_sds((S, H * dv), BF16), _sds((H, S, 1), F32)],
        scratch_shapes=[pltpu.VMEM((tq, 1), F32), pltpu.VMEM((tq, 1), F32), pltpu.VMEM((tq, dv), F32)],
        compiler_params=_params(("parallel", "parallel", "arbitrary")),
    )(q, k, v)


def _flash_bwd(name, q, k, v, do, head0, lse, delta, tq=BWD_TILE[0], tk=BWD_TILE[1]):
    H, S, dk = q.shape
    G = H // k.shape[0]
    dv = v.shape[2]
    tq, tk = _tile(S, (tq, 256, 128)), _tile(S, (tk, 256, 128))
    nq = S // tq

    def body(q_ref, k_ref, v_ref, do_ref, lse_ref, dl_ref, dq_ref, dk_ref, dv_ref, dk_s, dv_s):
        j, i = pl.program_id(1), pl.program_id(2)

        @pl.when(i == 0)
        def _():
            dk_s[...] = jnp.zeros_like(dk_s)
            dv_s[...] = jnp.zeros_like(dv_s)

        qv, kv_, dov = q_ref[...], k_ref[...], do_ref[...]
        s = lax.dot_general(qv, kv_, _NT, preferred_element_type=F32)
        p = jnp.exp(s - lse_ref[...])
        dp = lax.dot_general(dov, v_ref[...], _NT, preferred_element_type=F32)
        ds = (p * (dp - dl_ref[...])).astype(BF16)
        dv_s[...] += lax.dot_general(p.astype(BF16), dov, _TN, preferred_element_type=F32)
        dk_s[...] += lax.dot_general(ds, qv, _TN, preferred_element_type=F32)
        dqi = lax.dot_general(ds, kv_, _NN, preferred_element_type=F32)
        rows = pl.ds(pl.multiple_of(i * tq, tq), tq)

        @pl.when(j == 0)
        def _():
            dq_ref[rows, :] = dqi

        @pl.when(j > 0)
        def _():
            dq_ref[rows, :] += dqi

        @pl.when(i == nq - 1)
        def _():
            dk_ref[...] = dk_s[...]
            dv_ref[...] = dv_s[...]

    return pl.pallas_call(
        body, name=name, grid=(H, S // tk, nq),
        in_specs=[pl.BlockSpec((None, tq, dk), lambda h, j, i: (h, i, 0)),
                  pl.BlockSpec((None, tk, dk), lambda h, j, i: (h // G, j, 0)),
                  pl.BlockSpec((None, tk, dv), lambda h, j, i: (h // G, j, 0)),
                  pl.BlockSpec((tq, dv), lambda h, j, i: (i, head0 + h)),
                  pl.BlockSpec((None, tq, 1), lambda h, j, i: (h, i, 0)),
                  pl.BlockSpec((None, tq, 1), lambda h, j, i: (head0 + h, i, 0))],
        out_specs=[pl.BlockSpec((None, S, dk), lambda h, j, i: (h, 0, 0)),
                   pl.BlockSpec((None, tk, dk), lambda h, j, i: (h, j, 0)),
                   pl.BlockSpec((None, tk, dv), lambda h, j, i: (h, j, 0))],
        out_shape=[_sds((H, S, dk), F32), _sds((H, S, dk), F32), _sds((H, S, dv), F32)],
        scratch_shapes=[pltpu.VMEM((tk, dk), F32), pltpu.VMEM((tk, dv), F32)],
        compiler_params=_params(("parallel", "arbitrary", "arbitrary")),
    )(q, k, v, do, lse, delta)


def _swa_place(ref128, h):
    e, t = h % 2, (h // (SWA_HEADS // SWA_KV)) % 2
    x = ref128.astype(F32)
    if e != t:
        x = pltpu.roll(x, 64, 1)
    return jnp.where((_lane(x.shape) >= 64) == bool(t), x, 0.0).astype(BF16)


def _swa_unplace(y, h):
    e, t = h % 2, (h // (SWA_HEADS // SWA_KV)) % 2
    return pltpu.roll(y, 64, 1) if e != t else y


def _swa_specs(width, nb):
    prev = pl.BlockSpec((SWA_BLOCK, width), lambda i: (jnp.maximum(i - 1, 0), 0))
    cur = pl.BlockSpec((SWA_BLOCK, width), lambda i: (i, 0))
    nxt = pl.BlockSpec((SWA_BLOCK, width), lambda i: (jnp.minimum(i + 1, nb - 1), 0))
    return [prev, cur, nxt]


def _swa_bias_t(i, S):
    shape = (3 * SWA_BLOCK, SWA_BLOCK)
    kpos = (i - 1) * SWA_BLOCK + lax.broadcasted_iota(jnp.int32, shape, 0)
    qpos = i * SWA_BLOCK + lax.broadcasted_iota(jnp.int32, shape, 1)
    ok = (jnp.abs(qpos - kpos) <= SWA_WINDOW) & (kpos >= 0) & (kpos < S)
    return jnp.where(ok, 0.0, -jnp.inf)


def _add_blocks(x, blocks):
    n = x.shape[1] // LANES
    return jnp.concatenate([x[:, LANES * b:LANES * (b + 1)] + blocks[b % len(blocks)] for b in range(n)], axis=1)


def _swa_stack(refs, heads):
    return jnp.concatenate([_swa_place(r[:, LANES * (h // 2):LANES * (h // 2 + 1)], h) for h in heads for r in refs],
                           axis=0)


def _swa_unstack_t(ot, j, out_ref):
    grp = SWA_HEADS // SWA_KV
    for pair in range(grp // 2):
        h = grp * j + 2 * pair
        a = _swa_unplace(ot[:, SWA_BLOCK * 2 * pair:SWA_BLOCK * (2 * pair + 1)].T, h)
        b = _swa_unplace(ot[:, SWA_BLOCK * (2 * pair + 1):SWA_BLOCK * (2 * pair + 2)].T, h + 1)
        out_ref[:, LANES * (h // 2):LANES * (h // 2 + 1)] = jnp.where(_lane(a.shape) < 64, a, b).astype(out_ref.dtype)


def _swa_fwd(name, q, k, v, sink):
    S = q.shape[0]
    nb = S // SWA_BLOCK
    grp = SWA_HEADS // SWA_KV
    smem = pl.BlockSpec(memory_space=pltpu.SMEM)

    def body(sink_ref, q_ref, kp, kc, kn, vp, vc, vn, o_ref, lse_ref):
        i = pl.program_id(0)
        bias = [_swa_bias_t(i, S)]
        kcat = [jnp.concatenate([r[:, LANES * u:LANES * (u + 1)] for r in (kp, kc, kn)], axis=0) for u in range(2)]
        vcat = [jnp.concatenate([r[:, LANES * u:LANES * (u + 1)] for r in (vp, vc, vn)], axis=0) for u in range(2)]
        for j in range(SWA_KV):
            heads = range(grp * j, grp * (j + 1))
            xq = _swa_stack([q_ref], heads)
            sk = jnp.concatenate([jnp.full((1, SWA_BLOCK), sink_ref[0, h], F32) for h in heads], axis=1)
            st = lax.dot_general(kcat[j // 2], xq, _NT, preferred_element_type=F32)
            st = _add_blocks(st, bias)
            m = jnp.maximum(jnp.max(st, axis=0, keepdims=True), sk)
            pt = jnp.exp(st - m)
            den = jnp.sum(pt, axis=0, keepdims=True) + jnp.exp(sk - m)
            ot = lax.dot_general(vcat[j // 2], pt.astype(BF16), _TN, preferred_element_type=F32) * (1.0 / den)
            lse = m + jnp.log(den)
            for e, h in enumerate(heads):
                lse_ref[h] = lse[:, SWA_BLOCK * e:SWA_BLOCK * (e + 1)]
            _swa_unstack_t(ot, j, o_ref)

    return pl.pallas_call(
        body, name=name, grid=(nb,),
        in_specs=[smem, pl.BlockSpec((SWA_BLOCK, q.shape[1]), lambda i: (i, 0))]
        + _swa_specs(k.shape[1], nb) + _swa_specs(v.shape[1], nb),
        out_specs=[pl.BlockSpec((SWA_BLOCK, q.shape[1]), lambda i: (i, 0)),
                   pl.BlockSpec((SWA_HEADS, 1, SWA_BLOCK), lambda i: (0, 0, i))],
        out_shape=[_sds(q.shape, BF16), _sds((SWA_HEADS, 1, S), F32)],
        compiler_params=_params(("parallel",)),
    )(sink, q, k, k, k, v, v, v)


def _swa_bwd_q(name, q, k, v, do, lse_t, delta_t, sink):
    S = q.shape[0]
    nb = S // SWA_BLOCK
    grp = SWA_HEADS // SWA_KV
    smem = pl.BlockSpec(memory_space=pltpu.SMEM)
    row = pl.BlockSpec((SWA_BLOCK, q.shape[1]), lambda i: (i, 0))
    hrow = pl.BlockSpec((SWA_HEADS, 1, SWA_BLOCK), lambda i: (0, 0, i))

    def body(sink_ref, q_ref, do_ref, lse_ref, dl_ref, kp, kc, kn, vp, vc, vn, dq_ref, ds_ref):
        i = pl.program_id(0)

        @pl.when(i == 0)
        def _():
            ds_ref[...] = jnp.zeros_like(ds_ref)

        bias = [_swa_bias_t(i, S)]
        kcat = [jnp.concatenate([r[:, LANES * u:LANES * (u + 1)] for r in (kp, kc, kn)], axis=0) for u in range(2)]
        vcat = [jnp.concatenate([r[:, LANES * u:LANES * (u + 1)] for r in (vp, vc, vn)], axis=0) for u in range(2)]
        for j in range(SWA_KV):
            heads = range(grp * j, grp * (j + 1))
            xq = _swa_stack([q_ref], heads)
            xdo = _swa_stack([do_ref], heads)
            lse_r = jnp.concatenate([lse_ref[h] for h in heads], axis=1)
            dl_r = jnp.concatenate([dl_ref[h] for h in heads], axis=1)
            st = lax.dot_general(kcat[j // 2], xq, _NT, preferred_element_type=F32)
            pt = jnp.exp(_add_blocks(st, bias) - lse_r)
            dpt = lax.dot_general(vcat[j // 2], xdo, _NT, preferred_element_type=F32)
            dst = (pt * (dpt - dl_r)).astype(BF16)
            _swa_unstack_t(lax.dot_general(kcat[j // 2], dst, _TN, preferred_element_type=F32), j, dq_ref)
            for h in heads:
                dsink = -_rsum(jnp.exp(sink_ref[0, h] - lse_ref[h]) * dl_ref[h])
                ds_ref[h:h + 1, :] += jnp.broadcast_to(dsink, (1, LANES))

    return pl.pallas_call(
        body, name=name, grid=(nb,),
        in_specs=[smem, row, row, hrow, hrow] + _swa_specs(k.shape[1], nb) + _swa_specs(v.shape[1], nb),
        out_specs=[row, pl.BlockSpec((SWA_HEADS, LANES), lambda i: (0, 0))],
        out_shape=[_sds(q.shape, F32), _sds((SWA_HEADS, LANES), F32)],
        compiler_params=_params(("arbitrary",)),
    )(sink, q, do, lse_t, delta_t, k, k, k, v, v, v)


def _swa_bwd_kv(name, q, k, v, do, lse_t, delta_t):
    S = q.shape[0]
    nb = S // SWA_BLOCK
    grp = SWA_HEADS // SWA_KV
    krow = pl.BlockSpec((SWA_BLOCK, k.shape[1]), lambda i: (i, 0))

    def stat3():
        return [pl.BlockSpec((SWA_HEADS, 1, SWA_BLOCK), lambda i: (0, 0, jnp.maximum(i - 1, 0))),
                pl.BlockSpec((SWA_HEADS, 1, SWA_BLOCK), lambda i: (0, 0, i)),
                pl.BlockSpec((SWA_HEADS, 1, SWA_BLOCK), lambda i: (0, 0, jnp.minimum(i + 1, nb - 1)))]

    def body(qp, qc, qn, dop, doc, don, lp, lc, ln, dp_, dc_, dn_, k_ref, v_ref, dk_ref, dv_ref):
        j = pl.program_id(0)
        nh = 2 * grp
        kpos = j * SWA_BLOCK + lax.broadcasted_iota(jnp.int32, (SWA_BLOCK, SWA_BLOCK), 0)
        bias = []
        for b in range(3):
            qpos = (j - 1 + b) * SWA_BLOCK + lax.broadcasted_iota(jnp.int32, (SWA_BLOCK, SWA_BLOCK), 1)
            ok = (jnp.abs(qpos - kpos) <= SWA_WINDOW) & (qpos >= 0) & (qpos < S)
            bias.append(jnp.where(ok, 0.0, -jnp.inf))
        for u in range(SWA_KV // 2):
            heads = range(nh * u, nh * (u + 1))
            kc = k_ref[:, LANES * u:LANES * (u + 1)]
            vc = v_ref[:, LANES * u:LANES * (u + 1)]
            xq = _swa_stack([qp, qc, qn], heads)
            xdo = _swa_stack([dop, doc, don], heads)
            lse_r = jnp.concatenate([r[h] for h in heads for r in (lp, lc, ln)], axis=1)
            dl_r = jnp.concatenate([r[h] for h in heads for r in (dp_, dc_, dn_)], axis=1)
            st = lax.dot_general(kc, xq, _NT, preferred_element_type=F32)
            pt = jnp.exp(_add_blocks(st, bias) - lse_r)
            dpt = lax.dot_general(vc, xdo, _NT, preferred_element_type=F32)
            dst = (pt * (dpt - dl_r)).astype(BF16)
            dv_ref[:, LANES * u:LANES * (u + 1)] = lax.dot_general(pt.astype(BF16), xdo, _NN, preferred_element_type=F32)
            dk_ref[:, LANES * u:LANES * (u + 1)] = lax.dot_general(dst, xq, _NN, preferred_element_type=F32)

    return pl.pallas_call(
        body, name=name, grid=(nb,),
        in_specs=_swa_specs(q.shape[1], nb) + _swa_specs(do.shape[1], nb) + stat3() + stat3() + [krow, krow],
        out_specs=[krow, krow],
        out_shape=[_sds(k.shape, F32), _sds(v.shape, F32)],
        compiler_params=_params(("parallel",)),
    )(q, q, q, do, do, do, lse_t, lse_t, lse_t, delta_t, delta_t, delta_t, k, v)


def _peer_of(x, y, c, kk):
    return (x ^ ((kk >> 2) & 1), y ^ ((kk >> 1) & 1), c ^ (kk & 1))


def _own_slot(a, scatter):
    me = 4 * lax.axis_index("x") + 2 * lax.axis_index("y") + lax.axis_index("c")
    shape = a.shape if scatter else (N_DEV,) + a.shape
    own = lax.dynamic_slice_in_dim(a, me, 1, 0) if scatter else a[None]
    return lax.dynamic_update_slice_in_dim(lax.empty(shape, a.dtype), own, me, 0)


def _exchange_start(name, tensors):
    n = len(tensors)
    hbm = pl.BlockSpec(memory_space=pltpu.HBM)
    sem = pl.BlockSpec(memory_space=pltpu.SEMAPHORE)
    srcs = [pltpu.with_memory_space_constraint(a, pltpu.HBM) for a, _ in tensors]
    lands = [pltpu.with_memory_space_constraint(_own_slot(a, sc), pltpu.HBM) for a, sc in tensors]

    def body(*refs):
        ins, dst = refs[:n], refs[n:2 * n]
        send, recv, token = refs[2 * n], refs[2 * n + 1], refs[4 * n + 2]
        x, y, c = lax.axis_index("x"), lax.axis_index("y"), lax.axis_index("c")
        me = 4 * x + 2 * y + c
        for t in range(n):
            for kk in range(1, N_DEV):
                px, py, pc = _peer_of(x, y, c, kk)
                src = ins[t].at[4 * px + 2 * py + pc] if tensors[t][1] else ins[t]
                k1 = t * (N_DEV - 1) + kk - 1
                pltpu.make_async_remote_copy(src_ref=src, dst_ref=dst[t].at[me], send_sem=send.at[k1],
                                             recv_sem=recv.at[k1], device_id=(px, py, pc),
                                             device_id_type=pl.DeviceIdType.MESH).start()
        token[...] = jnp.zeros_like(token)

    out = pl.pallas_call(
        body, name=name,
        in_specs=[hbm] * (2 * n),
        out_specs=[sem, sem] + [hbm] * (2 * n) + [pl.BlockSpec(memory_space=pltpu.VMEM)],
        out_shape=[pltpu.SemaphoreType.DMA((n * (N_DEV - 1),)), pltpu.SemaphoreType.DMA((n * (N_DEV - 1),))]
        + [pltpu.HBM(a.shape, a.dtype) for a in srcs] + [pltpu.HBM(a.shape, a.dtype) for a in lands]
        + [_sds((8, LANES), F32)],
        input_output_aliases={i: i + 2 for i in range(2 * n)},
        compiler_params=pltpu.CompilerParams(has_side_effects=pltpu.SideEffectType.DATAFLOW_SIDE_EFFECTING),
    )(*srcs, *lands)
    return (out[0], out[1], out[2:2 + n], out[2 + n:2 + 2 * n], [sc for _, sc in tensors]), out[2 + 2 * n]


def _exchange_wait(name, started, after):
    send_s, recv_s, srcs, lands, flags = started
    n = len(srcs)
    hbm = pl.BlockSpec(memory_space=pltpu.HBM)
    sem = pl.BlockSpec(memory_space=pltpu.SEMAPHORE)

    def body(*refs):
        ins, dst, send, recv = refs[:n], refs[n:2 * n], refs[2 * n], refs[2 * n + 1]
        x, y, c = lax.axis_index("x"), lax.axis_index("y"), lax.axis_index("c")
        me = 4 * x + 2 * y + c
        for t in range(n):
            for kk in range(1, N_DEV):
                px, py, pc = _peer_of(x, y, c, kk)
                src = ins[t].at[me] if flags[t] else ins[t]
                k1 = t * (N_DEV - 1) + kk - 1
                cp = pltpu.make_async_remote_copy(src_ref=src, dst_ref=dst[t].at[4 * px + 2 * py + pc],
                                                  send_sem=send.at[k1], recv_sem=recv.at[k1],
                                                  device_id=(px, py, pc), device_id_type=pl.DeviceIdType.MESH)
                cp.wait_send()
                cp.wait_recv()

    out = pl.pallas_call(
        body, name=name,
        in_specs=[hbm] * (2 * n) + [sem, sem, pl.BlockSpec(memory_space=pl.ANY)],
        out_specs=[hbm] * (2 * n),
        out_shape=[pltpu.HBM(a.shape, a.dtype) for a in srcs] + [pltpu.HBM(a.shape, a.dtype) for a in lands],
        input_output_aliases={i: i for i in range(2 * n)},
        compiler_params=pltpu.CompilerParams(has_side_effects=pltpu.SideEffectType.DATAFLOW_SIDE_EFFECTING),
    )(*srcs, *lands, send_s, recv_s, after)
    return out[n:]


def _sum8(name, parts):
    _, R, C = parts.shape

    def fn(r, c, o):
        acc = r[0][0].astype(F32)
        for s in range(1, N_DEV):
            acc = acc + r[0][s].astype(F32)
        o[0][...] = acc
        return ()

    tm = R if R % 256 else 256
    return _rowwise(name, fn, [parts], [], [_sds((R, C), F32)], tm=tm)[0]


def _adamw(name, w, g, m, v):
    bc1 = 1.0 - ADAM_B1 ** ADAM_STEP
    bc2 = 1.0 - ADAM_B2 ** ADAM_STEP

    def fn(r, c, o):
        wv, gv, mv, vv = (x[...] for x in r)
        mn = ADAM_B1 * mv + (1.0 - ADAM_B1) * gv
        vn = ADAM_B2 * vv + (1.0 - ADAM_B2) * (gv * gv)
        o[0][...] = -ADAM_LR * ((mn / bc1) / (jnp.sqrt(vn / bc2) + ADAM_EPS) + ADAM_WD * wv)
        o[1][...] = mn
        o[2][...] = vn
        return ()

    tm = max(8, min(512, 1 << ((16 << 20) // (56 * w.shape[1])).bit_length() - 1))
    return _rowwise(name, fn, [w, g, m, v], [], [_sds(w.shape, F32)] * 3, tm=tm)


def _rope_cs(pos, dim, theta):
    inv = jnp.float32(theta) ** (-jnp.arange(0, dim, 2, dtype=jnp.float32) / dim)
    ang = pos.astype(jnp.float32)[:, None] * inv[None, :]
    return jnp.cos(ang), jnp.sin(ang)


def _tables(S):
    pos = jnp.arange(S)
    c, s = _rope_cs(pos, MLA_ROPE, ROPE_THETA)
    mla = (jnp.concatenate([c, c, c, c], 1), jnp.concatenate([-s, s, -s, s], 1))
    rc, rs = _rope_cs(pos // GRID_W, HD // 2, AXIAL_THETA)
    cc, cs = _rope_cs(pos % GRID_W, HD // 2, AXIAL_THETA)
    axial = (jnp.concatenate([rc, rc, cc, cc], 1), jnp.concatenate([-rs, rs, -cs, cs], 1))
    c, s = _rope_cs(pos, SWA_ROT, ROPE_THETA)
    one, zero = jnp.ones((S, SWA_D - SWA_ROT), F32), jnp.zeros((S, SWA_D - SWA_ROT), F32)
    swa = (jnp.concatenate([c, c, one, c, c, one], 1), jnp.concatenate([-s, s, zero, -s, s, zero], 1))
    return mla, axial, swa


_WEIGHTS = ['even_norm', 'even_w_in', 'mla_q_lat_norm', 'mla_kv_lat_norm', 'mla_w_uq', 'mla_w_ukv', 'mla_q_norm',
            'mla_k_nope_norm', 'mla_k_rope_norm', 'gqa_q_norm', 'gqa_k_norm', 'even_w_out', 'odd_norm', 'odd_w_qkv',
            'swa_q_norm', 'swa_k_norm', 'swa_sink', 'odd_w_out', 'mlp_norm', 'mlp_w_up', 'mlp_w_down']
_SMALL = ['even_norm', 'mla_q_lat_norm', 'mla_kv_lat_norm', 'mla_q_norm', 'mla_k_nope_norm', 'mla_k_rope_norm',
          'gqa_q_norm', 'gqa_k_norm', 'swa_q_norm', 'swa_k_norm', 'swa_sink', 'mlp_norm']


def _relu2(acc):
    rl = jnp.maximum(acc, 0.0)
    return rl * rl, rl


def _mul2(acc, rl):
    return (acc * (2.0 * rl.astype(F32)),)


def _add(acc, res):
    return (acc + res,)


def _step(x, tgt, w, m, v):
    S, D = x.shape
    nw = len(_WEIGHTS)
    tab_mla, tab_ax, tab_swa = _tables(S)
    bf = lambda a: a.astype(BF16)

    w_up_s, w_dn_s = w['mlp_w_up'], w['mlp_w_down']
    first, tok0 = _exchange_start("gather_first_start", [
        (bf(w['even_w_in'][0].T), False), (bf(w['mla_w_uq'][0].T), False), (bf(w['mla_w_ukv'][0].T), False)])
    h0 = _rmsnorm("even_norm", x, w['even_norm'], after=[tok0])
    gathered = _exchange_wait("gather_first_wait", first, h0)
    zero = jnp.minimum(jnp.abs(gathered[2][0, 0:1, 0:1].astype(F32)), 0.0)
    later, tok = _exchange_start("gather_rest_start", [
        (bf(w['even_w_out'][0]), False), (bf(w['odd_w_qkv'][0].T), False), (bf(w['odd_w_out'][0]), False),
        (bf(w_up_s[0].T), False), (bf(w_up_s[1].T), False), (bf(w_dn_s[0]), False), (bf(w_dn_s[1]), False),
        (w['odd_norm'] + zero, False)])
    flat = lambda a: a.reshape((a.shape[0] * a.shape[1],) + a.shape[2:])
    win_t, wuq_g, wukv_t = [flat(a) for a in gathered]
    win_t = jnp.concatenate([win_t[:KR_END], jnp.zeros((LANES - MLA_ROPE, D), BF16), win_t[KR_END:]], 0)
    wuq_g = gathered[1]
    wuq_t = jnp.concatenate([wuq_g[:, :HD].reshape(HEADS * HD, Q_LORA),
                             wuq_g[:, HD:].reshape(HEADS * MLA_ROPE, Q_LORA)], 0)

    z64 = jnp.zeros((1, 64), F32)
    qn = w['mla_q_norm']
    g_even = [w['mla_q_lat_norm'], w['mla_kv_lat_norm'], jnp.concatenate([w['mla_k_rope_norm'], z64], 1),
              w['gqa_q_norm'], w['gqa_k_norm']]
    g_mla = [qn[:, :HD], jnp.concatenate([qn[:, HD:], qn[:, HD:]], 1), w['mla_k_nope_norm']]
    g_swa = [jnp.concatenate([w['swa_q_norm']] * 2, 1), jnp.concatenate([w['swa_k_norm']] * 2, 1)]

    def loss_head(acc, res, t):
        e = acc + res - t
        return e * (1.0 / D), _csum(_rsum(e * e))

    def mlp_fwd(l, xin, last=False):
        hn = _rmsnorm(f"mlp{l}_norm", xin, w['mlp_norm'][l:l + 1])
        a, rl = _mm(f"mlp{l}_up", hn, wup_t[l], "nt", (BF16, BF16), epilogue=_relu2)
        if last:
            xout = _mm(f"mlp{l}_down", a, wdn[l], "nn", (F32,), epilogue=loss_head, extras=(xin, tgt),
                       sums=[_sds((1, 1), F32)])
        else:
            xout = _mm(f"mlp{l}_down", a, wdn[l], "nn", (F32,), epilogue=_add, extras=(xin,))
        return xout, (hn, a, rl)

    proj = _mm("even_in", h0, win_t, "nt", (F32,), tm=512, tn=P_END)
    cqn, ckvn, krd, qg, kg, vg = _even_prep("even_prep", proj, tab_mla + tab_ax, g_even, after=[tok])
    qa = _mm("mla_uq", cqn, wuq_t, "nt", (F32,))
    kv = _mm("mla_ukv", ckvn, wukv_t, "nt", (F32,))
    q_a, k_a, v_a = _mla_prep("mla_prep", qa, kv, krd, tab_mla, g_mla)
    o_a, lse_a = _flash_fwd("mla_attn", q_a, k_a, v_a)
    o_g, lse_g = _flash_fwd("gqa_attn", qg, kg, vg)
    merged = jnp.concatenate([o_a, o_g], 1)
    rest = _exchange_wait("gather_rest_wait", later, merged)
    wout_e, wqkv_t, wout_o, wup0_t, wup1_t, wdn0, wdn1 = [flat(a) for a in rest[:7]]
    odd_norm = rest[7].reshape(1, D)
    wup_t, wdn = (wup0_t, wup1_t), (wdn0, wdn1)
    x1 = _mm("even_out", merged, wout_e, "nn", (F32,), epilogue=_add, extras=(x,))
    x2, mlp0 = mlp_fwd(0, x1)

    h1 = _rmsnorm("odd_norm", x2, odd_norm)
    qkv = _mm("odd_qkv", h1, wqkv_t, "nt", (F32,))
    q_s, k_s, v_s = _swa_prep("swa_prep", qkv, tab_swa, g_swa)
    o_s, lse_s = _swa_fwd("swa_attn", q_s, k_s, v_s, w['swa_sink'])
    x3 = _mm("odd_out", o_s, wout_o, "nn", (F32,), epilogue=_add, extras=(x2,))
    (dy, loss_acc), mlp1 = mlp_fwd(1, x3, last=True)
    loss = lax.psum(0.5 / D * loss_acc[0, 0], ("x", "y", "c"))

    gsm = {}

    def mlp_bwd(l, dout, xin, saved):
        hn, a, rl = saved
        du = _mm(f"mlp{l}_dact", dout, wdn[l], "nt", (BF16,), epilogue=_mul2, extras=(rl,))
        g_dn = _mm(f"mlp{l}_gdown", a, dout, "tn", (BF16,))
        g_up = _mm(f"mlp{l}_gup", du, hn, "tn", (BF16,))
        dhn = _mm(f"mlp{l}_dnorm", du, wup_t[l], "nn", (F32,))
        din, g_n = _rmsnorm_bwd(f"mlp{l}_norm_bwd", dout, xin, dhn, w['mlp_norm'][l:l + 1])
        return din, g_dn, g_up, g_n

    dx3, g_dn1, g_up1, g_mn1 = mlp_bwd(1, dy, x3, mlp1)
    split = lambda a: a.reshape((N_DEV, a.shape[0] // N_DEV) + a.shape[1:])
    sc1, tok1 = _exchange_start("scatter_mlp1_start", [(split(g_up1), True), (split(g_dn1), True)])

    g_wout_o = _mm("odd_gout", o_s, dx3, "tn", (BF16,))
    do_s = _mm("odd_dattn", dx3, wout_o, "nt", (BF16,))
    dl_s = _delta_t("swa_delta", do_s, o_s, after=[tok1])
    dq_s, dsink = _swa_bwd_q("swa_bwd_q", q_s, k_s, v_s, do_s, lse_s, dl_s, w['swa_sink'])
    dk_s, dv_s = _swa_bwd_kv("swa_bwd_kv", q_s, k_s, v_s, do_s, lse_s, dl_s)
    dqkv, g_sq, g_sk = _swa_prep_bwd("swa_prep_bwd", dq_s, dk_s, dv_s, qkv, tab_swa, g_swa)
    g_wqkv = _mm("odd_gqkv", dqkv, h1, "tn", (BF16,))
    dh1 = _mm("odd_dnorm", dqkv, wqkv_t, "nn", (F32,), tk=dqkv.shape[1])
    dx2, g_on = _rmsnorm_bwd("odd_norm_bwd", dx3, x2, dh1, odd_norm)
    gsm['swa_q_norm'] = g_sq[:, :64] + g_sq[:, 64:]
    gsm['swa_k_norm'] = g_sk[:, :64] + g_sk[:, 64:]
    gsm['swa_sink'] = dsink[:, 0].reshape(1, SWA_HEADS)

    dx1, g_dn0, g_up0, g_mn0 = mlp_bwd(0, dx2, x1, mlp0)
    sc2, tok2 = _exchange_start("scatter_mid_start", [(split(g_wout_o), True), (split(g_wqkv), True),
                                                      (split(g_up0), True), (split(g_dn0), True)])
    gsm['mlp_norm'] = jnp.concatenate([g_mn0, g_mn1], 0)

    g_wout_e = _mm("even_gout", merged, dx1, "tn", (BF16,))
    dmerged = _mm("even_dattn", dx1, wout_e, "nt", (BF16,))
    dl_e = _delta("even_delta", dmerged, merged, after=[tok2])
    dq_a, dk_a, dv_a = _flash_bwd("mla_attn_bwd", q_a, k_a, v_a, dmerged, 0, lse_a, dl_e)
    dq_g, dk_g, dv_g = _flash_bwd("gqa_attn_bwd", qg, kg, vg, dmerged, HEADS, lse_g, dl_e)
    dqa, dkv, dkrd, g_qnn, g_qnr, g_kn = _mla_prep_bwd("mla_prep_bwd", dq_a, dk_a, dv_a, qa, kv, tab_mla, g_mla)
    g_wuq = _mm("mla_guq", dqa, cqn, "tn", (BF16,))
    dcqn = _mm("mla_dq_lat", dqa, wuq_t, "nn", (F32,))
    g_wukv = _mm("mla_gukv", dkv, ckvn, "tn", (BF16,))
    dckvn = _mm("mla_dkv_lat", dkv, wukv_t, "nn", (F32,))
    g_wuq = jnp.concatenate([g_wuq[:HEADS * HD].reshape(HEADS, HD, Q_LORA),
                             g_wuq[HEADS * HD:].reshape(HEADS, MLA_ROPE, Q_LORA)], 1)
    sc3, tok_e = _exchange_start("scatter_even_start", [(g_wuq, True), (split(g_wukv), True), (split(g_wout_e), True)])
    dproj, g_ql, g_kvl, g_kr, g_gq, g_gk = _even_prep_bwd("even_prep_bwd", dcqn, dckvn, dkrd, dq_g, dk_g, dv_g,
                                                          proj, tab_mla + tab_ax, g_even, after=[tok_e])
    g_win = _mm("even_gin", dproj, h0, "tn", (BF16,), tm=P_END, tk=1024)
    dh0 = _mm("even_dnorm", dproj, win_t, "nn", (F32,), tk=P_END)
    grad_x, g_en = _rmsnorm_bwd("even_norm_bwd", dx1, x, dh0, w['even_norm'])
    gsm.update(even_norm=g_en, mla_q_lat_norm=g_ql, mla_kv_lat_norm=g_kvl,
               mla_q_norm=jnp.concatenate([g_qnn, g_qnr[:, :64] + g_qnr[:, 64:]], 1), mla_k_nope_norm=g_kn,
               mla_k_rope_norm=g_kr[:, :64], gqa_q_norm=g_gq, gqa_k_norm=g_gk)

    g_win = jnp.concatenate([g_win[:KR_END], g_win[P_QG:]], 0)
    small_sizes = [w[n].size for n in _SMALL] + [D]
    small_vec = jnp.concatenate([gsm[n].reshape(1, -1) for n in _SMALL] + [g_on], 1)
    pad = (-small_vec.shape[1]) % LANES
    small_vec = jnp.pad(small_vec, ((0, 0), (0, pad)))
    last_st, tok3 = _exchange_start("scatter_last_start", [(split(g_win), True), (small_vec, False)])
    p_up1, p_dn1 = _exchange_wait("scatter_mlp1_wait", sc1, tok3)
    p_wout_o, p_wqkv, p_up0, p_dn0 = _exchange_wait("scatter_mid_wait", sc2, tok3)
    p_wuq, p_wukv, p_wout_e = _exchange_wait("scatter_even_wait", sc3, tok3)
    parts = dict(odd_w_qkv=p_wqkv, odd_w_out=p_wout_o, up0=p_up0, up1=p_up1, dn0=p_dn0, dn1=p_dn1,
                 mla_w_uq=p_wuq, mla_w_ukv=p_wukv, even_w_out=p_wout_e)
    red = {n: _sum8("sum_" + n, p) for n, p in parts.items()}
    grads = {
        'odd_w_qkv': red['odd_w_qkv'].T[None], 'odd_w_out': red['odd_w_out'][None],
        'mlp_w_up': jnp.stack([red['up0'].T, red['up1'].T]), 'mlp_w_down': jnp.stack([red['dn0'], red['dn1']]),
        'mla_w_uq': red['mla_w_uq'].T[None], 'mla_w_ukv': red['mla_w_ukv'].T[None],
        'even_w_out': red['even_w_out'][None],
    }
    delta, new_m, new_v = {}, {}, {}

    def apply(names):
        for n in names:
            shp = w[n].shape
            two = lambda a: a.reshape(shp[0] * shp[1], shp[2])
            d_, m_, v_ = _adamw("adamw_" + n, two(w[n]), two(grads[n]), two(m[n]), two(v[n]))
            delta[n], new_m[n], new_v[n] = d_.reshape(shp), m_.reshape(shp), v_.reshape(shp)

    apply(['mla_w_uq', 'mla_w_ukv', 'even_w_out', 'odd_w_qkv', 'odd_w_out', 'mlp_w_up', 'mlp_w_down'])
    last = _exchange_wait("scatter_last_wait", last_st, new_v['mlp_w_down'])
    small_g = _sum8("sum_small", last[1])
    grads['even_w_in'] = _sum8("sum_even_w_in", last[0]).T[None]
    apply(['even_w_in'])
    off = 0
    for n, sz in zip(_SMALL + ['odd_norm_full'], small_sizes):
        seg = small_g[:, off:off + sz]
        off += sz
        if n == 'odd_norm_full':
            me = 4 * lax.axis_index("x") + 2 * lax.axis_index("y") + lax.axis_index("c")
            grads['odd_norm'] = lax.dynamic_slice(seg, (0, me * (D // N_DEV)), (1, D // N_DEV))
        else:
            grads[n] = seg.reshape(w[n].shape)

    sm_names = _SMALL + ['odd_norm']
    pack = lambda d: jnp.concatenate([d[n].reshape(1, -1) for n in sm_names], 1)
    pw, pg, pm, pv = pack(w), pack(grads), pack(m), pack(v)
    padw = (-pw.shape[1]) % LANES
    padf = lambda a: jnp.pad(a, ((0, 0), (0, padw)))
    d_, m_, v_ = _adamw("adamw_small", padf(pw), padf(pg), padf(pm), jnp.pad(pv, ((0, 0), (0, padw)), constant_values=1.0))
    off = 0
    for n in sm_names:
        sz = w[n].size
        delta[n] = d_[:, off:off + sz].reshape(w[n].shape)
        new_m[n] = m_[:, off:off + sz].reshape(w[n].shape)
        new_v[n] = v_[:, off:off + sz].reshape(w[n].shape)
        off += sz

    return (loss, grad_x[None], *[grads[n] for n in _WEIGHTS], *[delta[n] for n in _WEIGHTS],
            *[new_m[n] for n in _WEIGHTS], *[new_v[n] for n in _WEIGHTS])


def kernel(x, even_norm, even_w_in, mla_q_lat_norm, mla_kv_lat_norm, mla_w_uq, mla_w_ukv, mla_q_norm, mla_k_nope_norm, mla_k_rope_norm, gqa_q_norm, gqa_k_norm, even_w_out, odd_norm, odd_w_qkv, swa_q_norm, swa_k_norm, swa_sink, odd_w_out, mlp_norm, mlp_w_up, mlp_w_down, loss_target, m_even_norm, m_even_w_in, m_mla_q_lat_norm, m_mla_kv_lat_norm, m_mla_w_uq, m_mla_w_ukv, m_mla_q_norm, m_mla_k_nope_norm, m_mla_k_rope_norm, m_gqa_q_norm, m_gqa_k_norm, m_even_w_out, m_odd_norm, m_odd_w_qkv, m_swa_q_norm, m_swa_k_norm, m_swa_sink, m_odd_w_out, m_mlp_norm, m_mlp_w_up, m_mlp_w_down, v_even_norm, v_even_w_in, v_mla_q_lat_norm, v_mla_kv_lat_norm, v_mla_w_uq, v_mla_w_ukv, v_mla_q_norm, v_mla_k_nope_norm, v_mla_k_rope_norm, v_gqa_q_norm, v_gqa_k_norm, v_even_w_out, v_odd_norm, v_odd_w_qkv, v_swa_q_norm, v_swa_k_norm, v_swa_sink, v_odd_w_out, v_mlp_norm, v_mlp_w_up, v_mlp_w_down):
    ws = (even_norm, even_w_in, mla_q_lat_norm, mla_kv_lat_norm, mla_w_uq, mla_w_ukv, mla_q_norm, mla_k_nope_norm, mla_k_rope_norm, gqa_q_norm, gqa_k_norm, even_w_out, odd_norm, odd_w_qkv, swa_q_norm, swa_k_norm, swa_sink, odd_w_out, mlp_norm, mlp_w_up, mlp_w_down)
    ms = (m_even_norm, m_even_w_in, m_mla_q_lat_norm, m_mla_kv_lat_norm, m_mla_w_uq, m_mla_w_ukv, m_mla_q_norm, m_mla_k_nope_norm, m_mla_k_rope_norm, m_gqa_q_norm, m_gqa_k_norm, m_even_w_out, m_odd_norm, m_odd_w_qkv, m_swa_q_norm, m_swa_k_norm, m_swa_sink, m_odd_w_out, m_mlp_norm, m_mlp_w_up, m_mlp_w_down)
    vs = (v_even_norm, v_even_w_in, v_mla_q_lat_norm, v_mla_kv_lat_norm, v_mla_w_uq, v_mla_w_ukv, v_mla_q_norm, v_mla_k_nope_norm, v_mla_k_rope_norm, v_gqa_q_norm, v_gqa_k_norm, v_even_w_out, v_odd_norm, v_odd_w_qkv, v_swa_q_norm, v_swa_k_norm, v_swa_sink, v_odd_w_out, v_mlp_norm, v_mlp_w_up, v_mlp_w_down)
    return _step(x[0], loss_target[0], dict(zip(_WEIGHTS, ws)), dict(zip(_WEIGHTS, ms)), dict(zip(_WEIGHTS, vs)))
```

```python
import jax
import jax.numpy as jnp
from jax import lax
from jax.experimental import pallas as pl
from jax.experimental.pallas import tpu as pltpu

F32 = jnp.float32
BF16 = jnp.bfloat16

N_DEV = 8
NORM_EPS = 1e-6
ROPE_THETA = 500000.0
AXIAL_THETA = 10000.0
GRID_W = 64
HEADS = 8
GQA_KV = 2
HD = 128
MLA_ROPE = 64
MLA_QK = HD + MLA_ROPE
Q_LORA = 512
KV_LORA = 256
SWA_HEADS = 32
SWA_KV = 4
SWA_D = 64
SWA_ROT = 16
SWA_WINDOW = 128
SWA_BLOCK = 128
MLA_SCALE, GQA_SCALE, SWA_SCALE = MLA_QK ** -0.5, HD ** -0.5, SWA_D ** -0.5
LANES = 128
ADAM_LR, ADAM_B1, ADAM_B2, ADAM_EPS, ADAM_WD, ADAM_STEP = 0.001, 0.9, 0.999, 1e-08, 0.01, 10
VMEM_LIMIT = 56 * 1024 * 1024
MM_TILE = (1024, 1024, 2048)
FWD_TILE = (1024, 8192)
BWD_TILE = (2048, 1024)
ROW_TILE = 256

P_CQ, P_CKV, P_KR, P_QG = 0, Q_LORA, Q_LORA + KV_LORA, Q_LORA + KV_LORA + LANES
P_KG = P_QG + HEADS * HD
P_VG = P_KG + GQA_KV * HD
P_END = P_VG + GQA_KV * HD
KR_END = Q_LORA + KV_LORA + MLA_ROPE


def _tile(n, prefs):
    for t in prefs:
        if n % t == 0 and t <= n:
            return t
    return n


def _params(sem):
    return pltpu.CompilerParams(dimension_semantics=sem, vmem_limit_bytes=VMEM_LIMIT)


_DIMS = {"nn": ((1,), (0,)), "nt": ((1,), (1,)), "tn": ((0,), (0,))}


def _mm(name, a, b, mode, out_dtypes, epilogue=None, extras=(), sums=(), tm=MM_TILE[0], tn=MM_TILE[1], tk=MM_TILE[2]):
    if mode == "nn":
        (M, K), (_, N) = a.shape, b.shape
    elif mode == "nt":
        (M, K), (N, _) = a.shape, b.shape
    else:
        (K, M), (_, N) = a.shape, b.shape
    tm = _tile(M, (tm, 512, 256, 128))
    tn = _tile(N, (tn, 512, 256, 128))
    tk = _tile(K, (tk, 1024, 512, 256, 128))
    nk = K // tk
    ne, no, ns = len(extras), len(out_dtypes), len(sums)
    if mode == "tn":
        a_spec = pl.BlockSpec((tk, tm), lambda i, j, k: (k, i))
    else:
        a_spec = pl.BlockSpec((tm, tk), lambda i, j, k: (i, k))
    if mode == "nt":
        b_spec = pl.BlockSpec((tn, tk), lambda i, j, k: (j, k))
    else:
        b_spec = pl.BlockSpec((tk, tn), lambda i, j, k: (k, j))
    o_spec = pl.BlockSpec((tm, tn), lambda i, j, k: (i, j))
    dims = (_DIMS[mode], ((), ()))

    def body(a_ref, b_ref, *rest):
        ex, outs = rest[:ne], rest[ne:ne + no]
        k = pl.program_id(2)
        part = lax.dot_general(a_ref[...].astype(BF16), b_ref[...].astype(BF16), dims, preferred_element_type=F32)

        def finish(total):
            res = epilogue(total, *[e[...] for e in ex]) if epilogue else (total,)
            for o, r in zip(outs, res[:no]):
                o[...] = r.astype(o.dtype)
            if ns:
                first = (pl.program_id(0) == 0) & (pl.program_id(1) == 0)

                @pl.when(first)
                def _():
                    for s_ref, r in zip(rest[ne + no:ne + no + ns], res[no:]):
                        s_ref[...] = r

                @pl.when(jnp.logical_not(first))
                def _():
                    for s_ref, r in zip(rest[ne + no:ne + no + ns], res[no:]):
                        s_ref[...] += r

        if nk == 1:
            finish(part)
            return
        acc = rest[ne + no + ns]

        @pl.when(k == 0)
        def _():
            acc[...] = part

        @pl.when((k > 0) & (k < nk - 1))
        def _():
            acc[...] += part

        @pl.when(k == nk - 1)
        def _():
            finish(acc[...] + part)

    out = pl.pallas_call(
        body, name=name, grid=(M // tm, N // tn, nk),
        in_specs=[a_spec, b_spec] + [o_spec] * ne,
        out_specs=[o_spec] * no + [pl.BlockSpec(tuple(s.shape), lambda i, j, k: (0,) * len(s.shape)) for s in sums],
        out_shape=[jax.ShapeDtypeStruct((M, N), d) for d in out_dtypes] + list(sums),
        scratch_shapes=[pltpu.VMEM((tm, tn), F32)] if nk > 1 else [],
        compiler_params=_params(("arbitrary",) * 3 if ns else ("parallel", "parallel", "arbitrary")),
    )(a, b, *extras)
    return out[0] if no + ns == 1 else out


def _rowwise(name, fn, rows, consts, outs, accs=(), tm=ROW_TILE, after=()):
    S = rows[0].shape[-2]
    tm = _tile(S, (tm, 128, 64, 32, 16, 8))
    nr, nc, nd, no, na = len(rows), len(consts), len(after), len(outs), len(accs)

    def rspec(shape):
        if len(shape) == 2:
            return pl.BlockSpec((tm, shape[1]), lambda i: (i, 0))
        return pl.BlockSpec((shape[0], tm, shape[2]), lambda i: (0, i, 0))

    def cspec(shape):
        return pl.BlockSpec(tuple(shape), lambda i: (0,) * len(shape))

    def ospec(o):
        if isinstance(o, tuple):
            return pl.BlockSpec((o[1].shape[0], o[1].shape[1], tm), lambda i: (0, 0, i))
        return rspec(o.shape)

    out_sds = [o[1] if isinstance(o, tuple) else o for o in outs]

    def body(*refs):
        r, c = refs[:nr], refs[nr:nr + nc]
        o, a = refs[nr + nc + nd:nr + nc + nd + no], refs[nr + nc + nd + no:]
        vals = fn(r, c, o)
        if na:
            @pl.when(pl.program_id(0) == 0)
            def _():
                for ar in a:
                    ar[...] = jnp.zeros_like(ar)

            for ar, v in zip(a, vals):
                ar[...] += v

    res = pl.pallas_call(
        body, name=name, grid=(S // tm,),
        in_specs=[rspec(x.shape) for x in rows] + [cspec(x.shape) for x in consts]
        + [pl.BlockSpec(memory_space=pl.ANY)] * len(after),
        out_specs=[ospec(x) for x in outs] + [cspec(x.shape) for x in accs],
        out_shape=out_sds + list(accs),
        compiler_params=_params(("arbitrary",) if na else ("parallel",)),
    )(*rows, *consts, *after)
    return res


def _sds(shape, dtype):
    return jax.ShapeDtypeStruct(tuple(shape), dtype)


def _rsum(x):
    return jnp.sum(x, axis=-1, keepdims=True)


def _csum(x):
    return jnp.sum(x, axis=0, keepdims=True)


def _lane(shape):
    return lax.broadcasted_iota(jnp.int32, shape, 1)


def _partner(x, seg, half):
    lane = _lane(x.shape) % seg
    return jnp.where(lane < half, pltpu.roll(x, LANES - half, 1), pltpu.roll(x, half, 1))


def _rope(x, c, s, seg, half):
    return x * c + _partner(x, seg, half) * s


def _rope_bwd(dy, c, s, seg, half):
    t = _partner(dy * s, seg, half)
    if seg != 2 * half:
        t = jnp.where(_lane(dy.shape) % seg < 2 * half, t, 0.0)
    return dy * c + t


def _rms_r(ss, n):
    return lax.rsqrt(ss * (1.0 / n) + NORM_EPS)


def _rms_bwd(x, r, g, dy, dot_scale):
    dyg = dy * g
    return r * dyg - x * (r * r * r) * (_rsum(dyg * x) * dot_scale)


def _rmsnorm(name, x, g, after=()):
    D = x.shape[1]

    def fn(r, c, o):
        xv = r[0][...]
        o[0][...] = (xv * _rms_r(_rsum(xv * xv), D) * c[0][...]).astype(BF16)
        return ()

    return _rowwise(name, fn, [x], [g], [_sds(x.shape, BF16)], tm=4 * ROW_TILE, after=after)[0]


def _rmsnorm_bwd(name, dres, x, dh, g):
    D = x.shape[1]

    def fn(r, c, o):
        xv, dhv, gv = r[1][...], r[2][...], c[0][...]
        rr = _rms_r(_rsum(xv * xv), D)
        o[0][...] = r[0][...] + _rms_bwd(xv, rr, gv, dhv, 1.0 / D)
        return (_csum(dhv * xv * rr),)

    return _rowwise(name, fn, [dres, x, dh], [g], [_sds(x.shape, F32)], [_sds((1, D), F32)], tm=2 * ROW_TILE)


def _even_prep(name, proj, tabs, gains, after=()):
    S = proj.shape[0]
    cm, sm, ca, sa = tabs

    def fn(r, c, o):
        p, cmv, smv, cav, sav = r[0], r[1][...], r[2][...], r[3][...], r[4][...]
        g_ql, g_kvl, g_kr, g_q, g_k = (x[...] for x in c)
        ones = _seg_ones(LANES)
        cq = p[:, P_CQ:P_CKV]
        o[0][...] = (cq * _wide(_rms_r(_seg_sum(_fold(cq * cq), ones), Q_LORA), Q_LORA // LANES) * g_ql).astype(BF16)
        ckv = p[:, P_CKV:P_KR]
        o[1][...] = (ckv * _wide(_rms_r(_seg_sum(_fold(ckv * ckv), ones), KV_LORA), KV_LORA // LANES) * g_kvl).astype(BF16)
        kr = p[:, P_KR:P_QG]
        y = _rope(kr * _rms_r(_seg_sum(kr * kr, ones), MLA_ROPE) * g_kr, cmv, smv, 64, 32)
        o[2][...] = (y + pltpu.roll(y, 64, 1)).astype(BF16)
        for h in range(HEADS):
            xh = p[:, P_QG + HD * h:P_QG + HD * (h + 1)]
            o[3][h] = (_rope(xh * _rms_r(_seg_sum(xh * xh, ones), HD) * g_q, cav, sav, 64, 32) * GQA_SCALE).astype(BF16)
        for h in range(GQA_KV):
            xh = p[:, P_KG + HD * h:P_KG + HD * (h + 1)]
            o[4][h] = _rope(xh * _rms_r(_seg_sum(xh * xh, ones), HD) * g_k, cav, sav, 64, 32).astype(BF16)
            o[5][h] = p[:, P_VG + HD * h:P_VG + HD * (h + 1)].astype(BF16)
        return ()

    outs = [_sds((S, Q_LORA), BF16), _sds((S, KV_LORA), BF16), _sds((S, LANES), BF16),
            _sds((HEADS, S, HD), BF16), _sds((GQA_KV, S, HD), BF16), _sds((GQA_KV, S, HD), BF16)]
    return _rowwise(name, fn, [proj, cm, sm, ca, sa], gains, outs, tm=2 * ROW_TILE, after=after)


def _even_prep_bwd(name, dcqn, dckvn, dkrd, dqg, dkg, dvg, proj, tabs, gains, after=()):
    S = proj.shape[0]
    cm, sm, ca, sa = tabs

    def fn(r, c, o):
        p, cmv, smv, cav, sav = r[6], r[7][...], r[8][...], r[9][...], r[10][...]
        g_ql, g_kvl, g_kr, g_q, g_k = (x[...] for x in c)
        out = o[0]
        ones = _seg_ones(LANES)

        def wide_bwd(x, g, d, n):
            w_ = n // LANES
            rr = _wide(_rms_r(_seg_sum(_fold(x * x), ones), n), w_)
            dot = _wide(_seg_sum(_fold(d * g * x), ones) * (1.0 / n), w_)
            return _rms_bwd_b(x, rr, g, d, dot).astype(BF16), _csum(d * x * rr)

        cq = p[:, P_CQ:P_CKV]
        out[:, P_CQ:P_CKV], a_ql = wide_bwd(cq, g_ql, r[0][...], Q_LORA)
        ckv = p[:, P_CKV:P_KR]
        out[:, P_CKV:P_KR], a_kvl = wide_bwd(ckv, g_kvl, r[1][...], KV_LORA)
        kr = p[:, P_KR:P_QG]
        rr = _rms_r(_seg_sum(kr * kr, ones), MLA_ROPE)
        d = r[2][...]
        d = d + pltpu.roll(d, 64, 1)
        d = _rope_bwd(d, cmv, smv, 64, 32)
        low = _lane(d.shape) < 64
        dot = _seg_sum(jnp.where(low, d * g_kr * kr, 0.0), ones) * (1.0 / MLA_ROPE)
        out[:, P_KR:P_QG] = jnp.where(low, _rms_bwd_b(kr, rr, g_kr, d, dot), 0.0).astype(BF16)
        a_kr = _csum(d * kr * rr)
        a_q = jnp.zeros((1, HD), F32)

        def head_bwd(xh, g, d):
            rr = _rms_r(_seg_sum(xh * xh, ones), HD)
            dot = _seg_sum(d * g * xh, ones) * (1.0 / HD)
            return _rms_bwd_b(xh, rr, g, d, dot).astype(BF16), _csum(d * xh * rr)

        for h in range(HEADS):
            xh = p[:, P_QG + HD * h:P_QG + HD * (h + 1)]
            d = _rope_bwd(r[3][h] * GQA_SCALE, cav, sav, 64, 32)
            out[:, P_QG + HD * h:P_QG + HD * (h + 1)], inc = head_bwd(xh, g_q, d)
            a_q = a_q + inc
        a_k = jnp.zeros((1, HD), F32)
        grp = HEADS // GQA_KV
        for h in range(GQA_KV):
            xh = p[:, P_KG + HD * h:P_KG + HD * (h + 1)]
            d = r[4][grp * h]
            dv = r[5][grp * h]
            for e in range(1, grp):
                d = d + r[4][grp * h + e]
                dv = dv + r[5][grp * h + e]
            d = _rope_bwd(d, cav, sav, 64, 32)
            out[:, P_KG + HD * h:P_KG + HD * (h + 1)], inc = head_bwd(xh, g_k, d)
            out[:, P_VG + HD * h:P_VG + HD * (h + 1)] = dv.astype(BF16)
            a_k = a_k + inc
        return (a_ql, a_kvl, a_kr, a_q, a_k)

    accs = [_sds((1, Q_LORA), F32), _sds((1, KV_LORA), F32), _sds((1, LANES), F32),
            _sds((1, HD), F32), _sds((1, HD), F32)]
    return _rowwise(name, fn, [dcqn, dckvn, dkrd, dqg, dkg, dvg, proj, cm, sm, ca, sa], gains,
                    [_sds((S, P_END), BF16)], accs, after=after)


def _mla_q_parts(qa, h, rp, ones):
    nope = qa[:, HD * h:HD * (h + 1)]
    mine = (_lane(rp.shape) >= 64) == bool(h % 2)
    rpm = jnp.where(mine, rp, 0.0)
    rr = _rms_r(_seg_sum(nope * nope + rpm * rpm, ones), MLA_QK)
    return nope, rpm, mine, rr


def _mla_prep(name, qa, kv, krd, tabs, gains):
    S = qa.shape[0]
    cm, sm = tabs

    def fn(r, c, o):
        qa_r, kv_r, krd_v, cmv, smv = r[0], r[1], r[2][...], r[3][...], r[4][...]
        g_n, g_r, g_kn = (x[...] for x in c)
        ones = _seg_ones(LANES)
        for h in range(HEADS):
            rp = qa_r[:, HEADS * HD + LANES * (h // 2):HEADS * HD + LANES * (h // 2 + 1)]
            nope, rpm, mine, rr = _mla_q_parts(qa_r, h, rp, ones)
            o[0][h, :, 0:HD] = (nope * (rr * MLA_SCALE) * g_n).astype(BF16)
            o[0][h, :, HD:2 * HD] = _rope(rpm * (rr * MLA_SCALE) * g_r, cmv, smv, 64, 32).astype(BF16)
            kn = kv_r[:, 2 * HD * h:2 * HD * h + HD]
            o[1][h, :, 0:HD] = (kn * _rms_r(_seg_sum(kn * kn, ones), HD) * g_kn).astype(BF16)
            o[1][h, :, HD:2 * HD] = krd_v
            o[2][h] = kv_r[:, 2 * HD * h + HD:2 * HD * (h + 1)].astype(BF16)
        return ()

    outs = [_sds((HEADS, S, 2 * HD), BF16), _sds((HEADS, S, 2 * HD), BF16), _sds((HEADS, S, HD), BF16)]
    return _rowwise(name, fn, [qa, kv, krd, cm, sm], gains, outs, tm=2 * ROW_TILE)


def _mla_prep_bwd(name, dq, dk, dv, qa, kv, tabs, gains):
    S = qa.shape[0]
    cm, sm = tabs

    def fn(r, c, o):
        dq_r, dk_r, dv_r, qa_r, kv_r, cmv, smv = r[0], r[1], r[2], r[3], r[4], r[5][...], r[6][...]
        g_n, g_r, g_kn = (x[...] for x in c)
        a_n = jnp.zeros((1, HD), F32)
        a_r = jnp.zeros((1, LANES), F32)
        a_kn = jnp.zeros((1, HD), F32)
        dkrd = jnp.zeros(cmv.shape, F32)
        drp = None
        ones = _seg_ones(LANES)
        for h in range(HEADS):
            rp = qa_r[:, HEADS * HD + LANES * (h // 2):HEADS * HD + LANES * (h // 2 + 1)]
            nope, rpm, mine, rr = _mla_q_parts(qa_r, h, rp, ones)
            dn = dq_r[h, :, 0:HD] * MLA_SCALE
            dr = _rope_bwd(jnp.where(mine, dq_r[h, :, HD:2 * HD] * MLA_SCALE, 0.0), cmv, smv, 64, 32)
            dot = _seg_sum(dn * g_n * nope + dr * g_r * rpm, ones) * (1.0 / MLA_QK)
            r3 = rr * rr * rr
            o[0][:, HD * h:HD * (h + 1)] = (rr * dn * g_n - nope * r3 * dot).astype(BF16)
            part = jnp.where(mine, rr * dr * g_r - rpm * r3 * dot, 0.0)
            drp = part if h % 2 == 0 else drp + part
            if h % 2 == 1:
                o[0][:, HEADS * HD + LANES * (h // 2):HEADS * HD + LANES * (h // 2 + 1)] = drp.astype(BF16)
            a_n = a_n + _csum(dn * nope * rr)
            a_r = a_r + _csum(dr * rpm * rr)
            kn = kv_r[:, 2 * HD * h:2 * HD * h + HD]
            rk = _rms_r(_seg_sum(kn * kn, ones), HD)
            dkn = dk_r[h, :, 0:HD]
            dotk = _seg_sum(dkn * g_kn * kn, ones) * (1.0 / HD)
            o[1][:, 2 * HD * h:2 * HD * h + HD] = _rms_bwd_b(kn, rk, g_kn, dkn, dotk).astype(BF16)
            o[1][:, 2 * HD * h + HD:2 * HD * (h + 1)] = dv_r[h].astype(BF16)
            a_kn = a_kn + _csum(dkn * kn * rk)
            dkrd = dkrd + dk_r[h, :, HD:2 * HD]
        o[2][...] = dkrd
        return (a_n, a_r, a_kn)

    outs = [_sds(qa.shape, BF16), _sds(kv.shape, BF16), _sds((S, LANES), F32)]
    accs = [_sds((1, HD), F32), _sds((1, LANES), F32), _sds((1, HD), F32)]
    return _rowwise(name, fn, [dq, dk, dv, qa, kv, cm, sm], gains, outs, accs)


def _seg_ones(seg):
    r = lax.broadcasted_iota(jnp.int32, (LANES, LANES), 0) // seg
    c = lax.broadcasted_iota(jnp.int32, (LANES, LANES), 1) // seg
    return (r == c).astype(F32)


def _dot01(x, sel, dims):
    hi = x.astype(BF16)
    lo = (x - hi.astype(F32)).astype(BF16)
    sb = sel.astype(BF16)
    if dims is _NN:
        return (lax.dot_general(hi, sb, dims, preferred_element_type=F32)
                + lax.dot_general(lo, sb, dims, preferred_element_type=F32))
    return (lax.dot_general(sb, hi, dims, preferred_element_type=F32)
            + lax.dot_general(sb, lo, dims, preferred_element_type=F32))


def _seg_sum(x, ones):
    return _dot01(x, ones, _NN)


def _fold(x):
    acc = x[:, 0:LANES]
    for g in range(1, x.shape[1] // LANES):
        acc = acc + x[:, LANES * g:LANES * (g + 1)]
    return acc


def _wide(v, n):
    return jnp.concatenate([v] * n, axis=1) if n > 1 else v


def _rms_bwd_b(x, r, g, dy, dot_b):
    dyg = dy * g
    return r * dyg - x * (r * r * r) * dot_b


def _seg64_r(x, ones):
    return _rms_r(_seg_sum(x * x, ones), SWA_D)


def _swa_prep(name, qkv, tabs, gains):
    S = qkv.shape[0]
    nq, nk = SWA_HEADS * SWA_D, SWA_KV * SWA_D
    cs, ss = tabs

    def fn(r, c, o):
        x_r, csv, ssv = r[0], r[1][...], r[2][...]
        g_q, g_k = c[0][...], c[1][...]
        ones = _seg_ones(SWA_D)
        for g in range((nq + nk) // LANES):
            x = x_r[:, LANES * g:LANES * (g + 1)]
            rr = _seg64_r(x, ones)
            y = _rope(x * rr * (g_q if g < nq // LANES else g_k), csv, ssv, SWA_D, SWA_ROT // 2)
            if g < nq // LANES:
                o[0][:, LANES * g:LANES * (g + 1)] = (y * SWA_SCALE).astype(BF16)
            else:
                o[1][:, LANES * g - nq:LANES * (g + 1) - nq] = y.astype(BF16)
        o[2][...] = x_r[:, nq + nk:nq + 2 * nk].astype(BF16)
        return ()

    outs = [_sds((S, nq), BF16), _sds((S, nk), BF16), _sds((S, nk), BF16)]
    return _rowwise(name, fn, [qkv, cs, ss], gains, outs, tm=2 * ROW_TILE)


def _swa_prep_bwd(name, dq, dk, dv, qkv, tabs, gains):
    nq, nk = SWA_HEADS * SWA_D, SWA_KV * SWA_D
    cs, ss = tabs

    def fn(r, c, o):
        dq_r, dk_r, x_r, csv, ssv = r[0], r[1], r[3], r[4][...], r[5][...]
        g_q, g_k = c[0][...], c[1][...]
        acc = [jnp.zeros((1, LANES), F32), jnp.zeros((1, LANES), F32)]
        ones = _seg_ones(SWA_D)
        for g in range((nq + nk) // LANES):
            isq = g < nq // LANES
            x = x_r[:, LANES * g:LANES * (g + 1)]
            rr = _seg64_r(x, ones)
            d = dq_r[:, LANES * g:LANES * (g + 1)] * SWA_SCALE if isq else dk_r[:, LANES * g - nq:LANES * (g + 1) - nq]
            d = _rope_bwd(d, csv, ssv, SWA_D, SWA_ROT // 2)
            dyg = d * (g_q if isq else g_k)
            dot = _seg_sum(dyg * x, ones) * (1.0 / SWA_D)
            o[0][:, LANES * g:LANES * (g + 1)] = (rr * dyg - x * (rr * rr * rr) * dot).astype(BF16)
            acc[0 if isq else 1] = acc[0 if isq else 1] + _csum(d * x * rr)
        o[0][:, nq + nk:nq + 2 * nk] = r[2][...].astype(BF16)
        return tuple(acc)

    return _rowwise(name, fn, [dq, dk, dv, qkv, cs, ss], gains, [_sds(qkv.shape, BF16)],
                    [_sds((1, LANES), F32), _sds((1, LANES), F32)])


def _delta(name, do, o_, after=()):
    S, C = do.shape

    def fn(r, c, o):
        for g in range(C // LANES):
            t = r[0][:, LANES * g:LANES * (g + 1)].astype(F32) * r[1][:, LANES * g:LANES * (g + 1)].astype(F32)
            o[0][g] = _rsum(t)
        return ()

    return _rowwise(name, fn, [do, o_], [], [_sds((C // LANES, S, 1), F32)], after=after)[0]


def _delta_t(name, do, o_, after=()):
    S, C = do.shape

    def fn(r, c, o):
        row = lax.broadcasted_iota(jnp.int32, (8, LANES), 0)
        sel = (row == _lane((8, LANES)) // SWA_D).astype(F32)
        for g in range(C // LANES):
            t = r[0][:, LANES * g:LANES * (g + 1)].astype(F32) * r[1][:, LANES * g:LANES * (g + 1)].astype(F32)
            both = _dot01(t, sel, _NT)
            o[0][2 * g] = both[0:1, :]
            o[0][2 * g + 1] = both[1:2, :]
        return ()

    return _rowwise(name, fn, [do, o_], [], [("T", _sds((2 * C // LANES, 1, S), F32))], after=after)[0]


_NT = (((1,), (1,)), ((), ()))
_NN = (((1,), (0,)), ((), ()))
_TN = (((0,), (0,)), ((), ()))


def _flash_fwd(name, q, k, v, tq=FWD_TILE[0], tk=FWD_TILE[1]):
    H, S, dk = q.shape
    G = H // k.shape[0]
    dv = v.shape[2]
    tq, tk = _tile(S, (tq, 256, 128)), _tile(S, (tk, 256, 128))
    nk = S // tk

    def body(q_ref, k_ref, v_ref, o_ref, lse_ref, m_s, l_s, acc_s):
        j = pl.program_id(2)

        @pl.when(j == 0)
        def _():
            m_s[...] = jnp.full_like(m_s, -jnp.inf)
            l_s[...] = jnp.zeros_like(l_s)
            acc_s[...] = jnp.zeros_like(acc_s)

        s = lax.dot_general(q_ref[...], k_ref[...], _NT, preferred_element_type=F32)
        m_new = jnp.maximum(m_s[...], jnp.max(s, axis=-1, keepdims=True))
        alpha = jnp.exp(m_s[...] - m_new)
        p = jnp.exp(s - m_new)
        l_s[...] = alpha * l_s[...] + _rsum(p)
        acc_s[...] = alpha * acc_s[...] + lax.dot_general(p.astype(BF16), v_ref[...], _NN,
                                                          preferred_element_type=F32)
        m_s[...] = m_new

        @pl.when(j == nk - 1)
        def _():
            o_ref[...] = (acc_s[...] / l_s[...]).astype(o_ref.dtype)
            lse_ref[...] = m_s[...] + jnp.log(l_s[...])

    return pl.pallas_call(
        body, name=name, grid=(H, S // tq, nk),
        in_specs=[pl.BlockSpec((None, tq, dk), lambda h, i, j: (h, i, 0)),
                  pl.BlockSpec((None, tk, dk), lambda h, i, j: (h // G, j, 0)),
                  pl.BlockSpec((None, tk, dv), lambda h, i, j: (h // G, j, 0))],
        out_specs=[pl.BlockSpec((tq, dv), lambda h, i, j: (i, h)),
                   pl.BlockSpec((None, tq, 1), lambda h, i, j: (h, i, 0))],
        out_shape=[_sds((S, H * dv), BF16), _sds((H, S, 1), F32)],
        scratch_shapes=[pltpu.VMEM((tq, 1), F32), pltpu.VMEM((tq, 1), F32), pltpu.VMEM((tq, dv), F32)],
        compiler_params=_params(("parallel", "parallel", "arbitrary")),
    )(q, k, v)


def _flash_bwd(name, q, k, v, do, head0, lse, delta, tq=BWD_TILE[0], tk=BWD_TILE[1]):
    H, S, dk = q.shape
    G = H // k.shape[0]
    dv = v.shape[2]
    tq, tk = _tile(S, (tq, 256, 128)), _tile(S, (tk, 256, 128))
    nq = S // tq

    def body(q_ref, k_ref, v_ref, do_ref, lse_ref, dl_ref, dq_ref, dk_ref, dv_ref, dk_s, dv_s):
        j, i = pl.program_id(1), pl.program_id(2)

        @pl.when(i == 0)
        def _():
            dk_s[...] = jnp.zeros_like(dk_s)
            dv_s[...] = jnp.zeros_like(dv_s)

        qv, kv_, dov = q_ref[...], k_ref[...], do_ref[...]
        s = lax.dot_general(qv, kv_, _NT, preferred_element_type=F32)
        p = jnp.exp(s - lse_ref[...])
        dp = lax.dot_general(dov, v_ref[...], _NT, preferred_element_type=F32)
        ds = (p * (dp - dl_ref[...])).astype(BF16)
        dv_s[...] += lax.dot_general(p.astype(BF16), dov, _TN, preferred_element_type=F32)
        dk_s[...] += lax.dot_general(ds, qv, _TN, preferred_element_type=F32)
        dqi = lax.dot_general(ds, kv_, _NN, preferred_element_type=F32)
        rows = pl.ds(pl.multiple_of(i * tq, tq), tq)

        @pl.when(j == 0)
        def _():
            dq_ref[rows, :] = dqi

        @pl.when(j > 0)
        def _():
            dq_ref[rows, :] += dqi

        @pl.when(i == nq - 1)
        def _():
            dk_ref[...] = dk_s[...]
            dv_ref[...] = dv_s[...]

    return pl.pallas_call(
        body, name=name, grid=(H, S // tk, nq),
        in_specs=[pl.BlockSpec((None, tq, dk), lambda h, j, i: (h, i, 0)),
                  pl.BlockSpec((None, tk, dk), lambda h, j, i: (h // G, j, 0)),
                  pl.BlockSpec((None, tk, dv), lambda h, j, i: (h // G, j, 0)),
                  pl.BlockSpec((tq, dv), lambda h, j, i: (i, head0 + h)),
                  pl.BlockSpec((None, tq, 1), lambda h, j, i: (h, i, 0)),
                  pl.BlockSpec((None, tq, 1), lambda h, j, i: (head0 + h, i, 0))],
        out_specs=[pl.BlockSpec((None, S, dk), lambda h, j, i: (h, 0, 0)),
                   pl.BlockSpec((None, tk, dk), lambda h, j, i: (h, j, 0)),
                   pl.BlockSpec((None, tk, dv), lambda h, j, i: (h, j, 0))],
        out_shape=[_sds((H, S, dk), F32), _sds((H, S, dk), F32), _sds((H, S, dv), F32)],
        scratch_shapes=[pltpu.VMEM((tk, dk), F32), pltpu.VMEM((tk, dv), F32)],
        compiler_params=_params(("parallel", "arbitrary", "arbitrary")),
    )(q, k, v, do, lse, delta)


def _swa_place(ref128, h):
    e, t = h % 2, (h // (SWA_HEADS // SWA_KV)) % 2
    x = ref128.astype(F32)
    if e != t:
        x = pltpu.roll(x, 64, 1)
    return jnp.where((_lane(x.shape) >= 64) == bool(t), x, 0.0).astype(BF16)


def _swa_unplace(y, h):
    e, t = h % 2, (h // (SWA_HEADS // SWA_KV)) % 2
    return pltpu.roll(y, 64, 1) if e != t else y


def _swa_specs(width, nb):
    prev = pl.BlockSpec((SWA_BLOCK, width), lambda i: (jnp.maximum(i - 1, 0), 0))
    cur = pl.BlockSpec((SWA_BLOCK, width), lambda i: (i, 0))
    nxt = pl.BlockSpec((SWA_BLOCK, width), lambda i: (jnp.minimum(i + 1, nb - 1), 0))
    return [prev, cur, nxt]


def _swa_bias_t(i, S):
    shape = (3 * SWA_BLOCK, SWA_BLOCK)
    kpos = (i - 1) * SWA_BLOCK + lax.broadcasted_iota(jnp.int32, shape, 0)
    qpos = i * SWA_BLOCK + lax.broadcasted_iota(jnp.int32, shape, 1)
    ok = (jnp.abs(qpos - kpos) <= SWA_WINDOW) & (kpos >= 0) & (kpos < S)
    return jnp.where(ok, 0.0, -jnp.inf)


def _add_blocks(x, blocks):
    n = x.shape[1] // LANES
    return jnp.concatenate([x[:, LANES * b:LANES * (b + 1)] + blocks[b % len(blocks)] for b in range(n)], axis=1)


def _swa_stack(refs, heads):
    return jnp.concatenate([_swa_place(r[:, LANES * (h // 2):LANES * (h // 2 + 1)], h) for h in heads for r in refs],
                           axis=0)


def _swa_unstack_t(ot, j, out_ref):
    grp = SWA_HEADS // SWA_KV
    for pair in range(grp // 2):
        h = grp * j + 2 * pair
        a = _swa_unplace(ot[:, SWA_BLOCK * 2 * pair:SWA_BLOCK * (2 * pair + 1)].T, h)
        b = _swa_unplace(ot[:, SWA_BLOCK * (2 * pair + 1):SWA_BLOCK * (2 * pair + 2)].T, h + 1)
        out_ref[:, LANES * (h // 2):LANES * (h // 2 + 1)] = jnp.where(_lane(a.shape) < 64, a, b).astype(out_ref.dtype)


def _swa_fwd(name, q, k, v, sink):
    S = q.shape[0]
    nb = S // SWA_BLOCK
    grp = SWA_HEADS // SWA_KV
    smem = pl.BlockSpec(memory_space=pltpu.SMEM)

    def body(sink_ref, q_ref, kp, kc, kn, vp, vc, vn, o_ref, lse_ref):
        i = pl.program_id(0)
        bias = [_swa_bias_t(i, S)]
        kcat = [jnp.concatenate([r[:, LANES * u:LANES * (u + 1)] for r in (kp, kc, kn)], axis=0) for u in range(2)]
        vcat = [jnp.concatenate([r[:, LANES * u:LANES * (u + 1)] for r in (vp, vc, vn)], axis=0) for u in range(2)]
        for j in range(SWA_KV):
            heads = range(grp * j, grp * (j + 1))
            xq = _swa_stack([q_ref], heads)
            sk = jnp.concatenate([jnp.full((1, SWA_BLOCK), sink_ref[0, h], F32) for h in heads], axis=1)
            st = lax.dot_general(kcat[j // 2], xq, _NT, preferred_element_type=F32)
            st = _add_blocks(st, bias)
            m = jnp.maximum(jnp.max(st, axis=0, keepdims=True), sk)
            pt = jnp.exp(st - m)
            den = jnp.sum(pt, axis=0, keepdims=True) + jnp.exp(sk - m)
            ot = lax.dot_general(vcat[j // 2], pt.astype(BF16), _TN, preferred_element_type=F32) * (1.0 / den)
            lse = m + jnp.log(den)
            for e, h in enumerate(heads):
                lse_ref[h] = lse[:, SWA_BLOCK * e:SWA_BLOCK * (e + 1)]
            _swa_unstack_t(ot, j, o_ref)

    return pl.pallas_call(
        body, name=name, grid=(nb,),
        in_specs=[smem, pl.BlockSpec((SWA_BLOCK, q.shape[1]), lambda i: (i, 0))]
        + _swa_specs(k.shape[1], nb) + _swa_specs(v.shape[1], nb),
        out_specs=[pl.BlockSpec((SWA_BLOCK, q.shape[1]), lambda i: (i, 0)),
                   pl.BlockSpec((SWA_HEADS, 1, SWA_BLOCK), lambda i: (0, 0, i))],
        out_shape=[_sds(q.shape, BF16), _sds((SWA_HEADS, 1, S), F32)],
        compiler_params=_params(("parallel",)),
    )(sink, q, k, k, k, v, v, v)


def _swa_bwd_q(name, q, k, v, do, lse_t, delta_t, sink):
    S = q.shape[0]
    nb = S // SWA_BLOCK
    grp = SWA_HEADS // SWA_KV
    smem = pl.BlockSpec(memory_space=pltpu.SMEM)
    row = pl.BlockSpec((SWA_BLOCK, q.shape[1]), lambda i: (i, 0))
    hrow = pl.BlockSpec((SWA_HEADS, 1, SWA_BLOCK), lambda i: (0, 0, i))

    def body(sink_ref, q_ref, do_ref, lse_ref, dl_ref, kp, kc, kn, vp, vc, vn, dq_ref, ds_ref):
        i = pl.program_id(0)

        @pl.when(i == 0)
        def _():
            ds_ref[...] = jnp.zeros_like(ds_ref)

        bias = [_swa_bias_t(i, S)]
        kcat = [jnp.concatenate([r[:, LANES * u:LANES * (u + 1)] for r in (kp, kc, kn)], axis=0) for u in range(2)]
        vcat = [jnp.concatenate([r[:, LANES * u:LANES * (u + 1)] for r in (vp, vc, vn)], axis=0) for u in range(2)]
        for j in range(SWA_KV):
            heads = range(grp * j, grp * (j + 1))
            xq = _swa_stack([q_ref], heads)
            xdo = _swa_stack([do_ref], heads)
            lse_r = jnp.concatenate([lse_ref[h] for h in heads], axis=1)
            dl_r = jnp.concatenate([dl_ref[h] for h in heads], axis=1)
            st = lax.dot_general(kcat[j // 2], xq, _NT, preferred_element_type=F32)
            pt = jnp.exp(_add_blocks(st, bias) - lse_r)
            dpt = lax.dot_general(vcat[j // 2], xdo, _NT, preferred_element_type=F32)
            dst = (pt * (dpt - dl_r)).astype(BF16)
            _swa_unstack_t(lax.dot_general(kcat[j // 2], dst, _TN, preferred_element_type=F32), j, dq_ref)
            for h in heads:
                dsink = -_rsum(jnp.exp(sink_ref[0, h] - lse_ref[h]) * dl_ref[h])
                ds_ref[h:h + 1, :] += jnp.broadcast_to(dsink, (1, LANES))

    return pl.pallas_call(
        body, name=name, grid=(nb,),
        in_specs=[smem, row, row, hrow, hrow] + _swa_specs(k.shape[1], nb) + _swa_specs(v.shape[1], nb),
        out_specs=[row, pl.BlockSpec((SWA_HEADS, LANES), lambda i: (0, 0))],
        out_shape=[_sds(q.shape, F32), _sds((SWA_HEADS, LANES), F32)],
        compiler_params=_params(("arbitrary",)),
    )(sink, q, do, lse_t, delta_t, k, k, k, v, v, v)


def _swa_bwd_kv(name, q, k, v, do, lse_t, delta_t):
    S = q.shape[0]
    nb = S // SWA_BLOCK
    grp = SWA_HEADS // SWA_KV
    krow = pl.BlockSpec((SWA_BLOCK, k.shape[1]), lambda i: (i, 0))

    def stat3():
        return [pl.BlockSpec((SWA_HEADS, 1, SWA_BLOCK), lambda i: (0, 0, jnp.maximum(i - 1, 0))),
                pl.BlockSpec((SWA_HEADS, 1, SWA_BLOCK), lambda i: (0, 0, i)),
                pl.BlockSpec((SWA_HEADS, 1, SWA_BLOCK), lambda i: (0, 0, jnp.minimum(i + 1, nb - 1)))]

    def body(qp, qc, qn, dop, doc, don, lp, lc, ln, dp_, dc_, dn_, k_ref, v_ref, dk_ref, dv_ref):
        j = pl.program_id(0)
        nh = 2 * grp
        kpos = j * SWA_BLOCK + lax.broadcasted_iota(jnp.int32, (SWA_BLOCK, SWA_BLOCK), 0)
        bias = []
        for b in range(3):
            qpos = (j - 1 + b) * SWA_BLOCK + lax.broadcasted_iota(jnp.int32, (SWA_BLOCK, SWA_BLOCK), 1)
            ok = (jnp.abs(qpos - kpos) <= SWA_WINDOW) & (qpos >= 0) & (qpos < S)
            bias.append(jnp.where(ok, 0.0, -jnp.inf))
        for u in range(SWA_KV // 2):
            heads = range(nh * u, nh * (u + 1))
            kc = k_ref[:, LANES * u:LANES * (u + 1)]
            vc = v_ref[:, LANES * u:LANES * (u + 1)]
            xq = _swa_stack([qp, qc, qn], heads)
            xdo = _swa_stack([dop, doc, don], heads)
            lse_r = jnp.concatenate([r[h] for h in heads for r in (lp, lc, ln)], axis=1)
            dl_r = jnp.concatenate([r[h] for h in heads for r in (dp_, dc_, dn_)], axis=1)
            st = lax.dot_general(kc, xq, _NT, preferred_element_type=F32)
            pt = jnp.exp(_add_blocks(st, bias) - lse_r)
            dpt = lax.dot_general(vc, xdo, _NT, preferred_element_type=F32)
            dst = (pt * (dpt - dl_r)).astype(BF16)
            dv_ref[:, LANES * u:LANES * (u + 1)] = lax.dot_general(pt.astype(BF16), xdo, _NN, preferred_element_type=F32)
            dk_ref[:, LANES * u:LANES * (u + 1)] = lax.dot_general(dst, xq, _NN, preferred_element_type=F32)

    return pl.pallas_call(
        body, name=name, grid=(nb,),
        in_specs=_swa_specs(q.shape[1], nb) + _swa_specs(do.shape[1], nb) + stat3() + stat3() + [krow, krow],
        out_specs=[krow, krow],
        out_shape=[_sds(k.shape, F32), _sds(v.shape, F32)],
        compiler_params=_params(("parallel",)),
    )(q, q, q, do, do, do, lse_t, lse_t, lse_t, delta_t, delta_t, delta_t, k, v)


def _peer_of(x, y, c, kk):
    return (x ^ ((kk >> 2) & 1), y ^ ((kk >> 1) & 1), c ^ (kk & 1))


def _own_slot(a, scatter):
    me = 4 * lax.axis_index("x") + 2 * lax.axis_index("y") + lax.axis_index("c")
    shape = a.shape if scatter else (N_DEV,) + a.shape
    own = lax.dynamic_slice_in_dim(a, me, 1, 0) if scatter else a[None]
    return lax.dynamic_update_slice_in_dim(lax.empty(shape, a.dtype), own, me, 0)


def _exchange_start(name, tensors):
    n = len(tensors)
    hbm = pl.BlockSpec(memory_space=pltpu.HBM)
    sem = pl.BlockSpec(memory_space=pltpu.SEMAPHORE)
    srcs = [pltpu.with_memory_space_constraint(a, pltpu.HBM) for a, _ in tensors]
    lands = [pltpu.with_memory_space_constraint(_own_slot(a, sc), pltpu.HBM) for a, sc in tensors]

    def body(*refs):
        ins, dst = refs[:n], refs[n:2 * n]
        send, recv, token = refs[2 * n], refs[2 * n + 1], refs[4 * n + 2]
        x, y, c = lax.axis_index("x"), lax.axis_index("y"), lax.axis_index("c")
        me = 4 * x + 2 * y + c
        for t in range(n):
            for kk in range(1, N_DEV):
                px, py, pc = _peer_of(x, y, c, kk)
                src = ins[t].at[4 * px + 2 * py + pc] if tensors[t][1] else ins[t]
                k1 = t * (N_DEV - 1) + kk - 1
                pltpu.make_async_remote_copy(src_ref=src, dst_ref=dst[t].at[me], send_sem=send.at[k1],
                                             recv_sem=recv.at[k1], device_id=(px, py, pc),
                                             device_id_type=pl.DeviceIdType.MESH).start()
        token[...] = jnp.zeros_like(token)

    out = pl.pallas_call(
        body, name=name,
        in_specs=[hbm] * (2 * n),
        out_specs=[sem, sem] + [hbm] * (2 * n) + [pl.BlockSpec(memory_space=pltpu.VMEM)],
        out_shape=[pltpu.SemaphoreType.DMA((n * (N_DEV - 1),)), pltpu.SemaphoreType.DMA((n * (N_DEV - 1),))]
        + [pltpu.HBM(a.shape, a.dtype) for a in srcs] + [pltpu.HBM(a.shape, a.dtype) for a in lands]
        + [_sds((8, LANES), F32)],
        input_output_aliases={i: i + 2 for i in range(2 * n)},
        compiler_params=pltpu.CompilerParams(has_side_effects=pltpu.SideEffectType.DATAFLOW_SIDE_EFFECTING),
    )(*srcs, *lands)
    return (out[0], out[1], out[2:2 + n], out[2 + n:2 + 2 * n], [sc for _, sc in tensors]), out[2 + 2 * n]


def _exchange_wait(name, started, after):
    send_s, recv_s, srcs, lands, flags = started
    n = len(srcs)
    hbm = pl.BlockSpec(memory_space=pltpu.HBM)
    sem = pl.BlockSpec(memory_space=pltpu.SEMAPHORE)

    def body(*refs):
        ins, dst, send, recv = refs[:n], refs[n:2 * n], refs[2 * n], refs[2 * n + 1]
        x, y, c = lax.axis_index("x"), lax.axis_index("y"), lax.axis_index("c")
        me = 4 * x + 2 * y + c
        for t in range(n):
            for kk in range(1, N_DEV):
                px, py, pc = _peer_of(x, y, c, kk)
                src = ins[t].at[me] if flags[t] else ins[t]
                k1 = t * (N_DEV - 1) + kk - 1
                cp = pltpu.make_async_remote_copy(src_ref=src, dst_ref=dst[t].at[4 * px + 2 * py + pc],
                                                  send_sem=send.at[k1], recv_sem=recv.at[k1],
                                                  device_id=(px, py, pc), device_id_type=pl.DeviceIdType.MESH)
                cp.wait_send()
                cp.wait_recv()

    out = pl.pallas_call(
        body, name=name,
        in_specs=[hbm] * (2 * n) + [sem, sem, pl.BlockSpec(memory_space=pl.ANY)],
        out_specs=[hbm] * (2 * n),
        out_shape=[pltpu.HBM(a.shape, a.dtype) for a in srcs] + [pltpu.HBM(a.shape, a.dtype) for a in lands],
        input_output_aliases={i: i for i in range(2 * n)},
        compiler_params=pltpu.CompilerParams(has_side_effects=pltpu.SideEffectType.DATAFLOW_SIDE_EFFECTING),
    )(*srcs, *lands, send_s, recv_s, after)
    return out[n:]


def _sum8(name, parts):
    _, R, C = parts.shape

    def fn(r, c, o):
        acc = r[0][0].astype(F32)
        for s in range(1, N_DEV):
            acc = acc + r[0][s].astype(F32)
        o[0][...] = acc
        return ()

    tm = R if R % 256 else 256
    return _rowwise(name, fn, [parts], [], [_sds((R, C), F32)], tm=tm)[0]


def _adamw(name, w, g, m, v):
    bc1 = 1.0 - ADAM_B1 ** ADAM_STEP
    bc2 = 1.0 - ADAM_B2 ** ADAM_STEP

    def fn(r, c, o):
        wv, gv, mv, vv = (x[...] for x in r)
        mn = ADAM_B1 * mv + (1.0 - ADAM_B1) * gv
        vn = ADAM_B2 * vv + (1.0 - ADAM_B2) * (gv * gv)
        o[0][...] = -ADAM_LR * ((mn / bc1) / (jnp.sqrt(vn / bc2) + ADAM_EPS) + ADAM_WD * wv)
        o[1][...] = mn
        o[2][...] = vn
        return ()

    tm = max(8, min(512, 1 << ((16 << 20) // (56 * w.shape[1])).bit_length() - 1))
    return _rowwise(name, fn, [w, g, m, v], [], [_sds(w.shape, F32)] * 3, tm=tm)


def _rope_cs(pos, dim, theta):
    inv = jnp.float32(theta) ** (-jnp.arange(0, dim, 2, dtype=jnp.float32) / dim)
    ang = pos.astype(jnp.float32)[:, None] * inv[None, :]
    return jnp.cos(ang), jnp.sin(ang)


def _tables(S):
    pos = jnp.arange(S)
    c, s = _rope_cs(pos, MLA_ROPE, ROPE_THETA)
    mla = (jnp.concatenate([c, c, c, c], 1), jnp.concatenate([-s, s, -s, s], 1))
    rc, rs = _rope_cs(pos // GRID_W, HD // 2, AXIAL_THETA)
    cc, cs = _rope_cs(pos % GRID_W, HD // 2, AXIAL_THETA)
    axial = (jnp.concatenate([rc, rc, cc, cc], 1), jnp.concatenate([-rs, rs, -cs, cs], 1))
    c, s = _rope_cs(pos, SWA_ROT, ROPE_THETA)
    one, zero = jnp.ones((S, SWA_D - SWA_ROT), F32), jnp.zeros((S, SWA_D - SWA_ROT), F32)
    swa = (jnp.concatenate([c, c, one, c, c, one], 1), jnp.concatenate([-s, s, zero, -s, s, zero], 1))
    return mla, axial, swa


_WEIGHTS = ['even_norm', 'even_w_in', 'mla_q_lat_norm', 'mla_kv_lat_norm', 'mla_w_uq', 'mla_w_ukv', 'mla_q_norm',
            'mla_k_nope_norm', 'mla_k_rope_norm', 'gqa_q_norm', 'gqa_k_norm', 'even_w_out', 'odd_norm', 'odd_w_qkv',
            'swa_q_norm', 'swa_k_norm', 'swa_sink', 'odd_w_out', 'mlp_norm', 'mlp_w_up', 'mlp_w_down']
_SMALL = ['even_norm', 'mla_q_lat_norm', 'mla_kv_lat_norm', 'mla_q_norm', 'mla_k_nope_norm', 'mla_k_rope_norm',
          'gqa_q_norm', 'gqa_k_norm', 'swa_q_norm', 'swa_k_norm', 'swa_sink', 'mlp_norm']


def _relu2(acc):
    rl = jnp.maximum(acc, 0.0)
    return rl * rl, rl


def _mul2(acc, rl):
    return (acc * (2.0 * rl.astype(F32)),)


def _add(acc, res):
    return (acc + res,)


def _step(x, tgt, w, m, v):
    S, D = x.shape
    nw = len(_WEIGHTS)
    tab_mla, tab_ax, tab_swa = _tables(S)
    bf = lambda a: a.astype(BF16)

    w_up_s, w_dn_s = w['mlp_w_up'], w['mlp_w_down']
    first, tok0 = _exchange_start("gather_first_start", [
        (bf(w['even_w_in'][0].T), False), (bf(w['mla_w_uq'][0].T), False), (bf(w['mla_w_ukv'][0].T), False)])
    h0 = _rmsnorm("even_norm", x, w['even_norm'], after=[tok0])
    gathered = _exchange_wait("gather_first_wait", first, h0)
    zero = jnp.minimum(jnp.abs(gathered[2][0, 0:1, 0:1].astype(F32)), 0.0)
    later, tok = _exchange_start("gather_rest_start", [
        (bf(w['even_w_out'][0]), False), (bf(w['odd_w_qkv'][0].T), False), (bf(w['odd_w_out'][0]), False),
        (bf(w_up_s[0].T), False), (bf(w_up_s[1].T), False), (bf(w_dn_s[0]), False), (bf(w_dn_s[1]), False),
        (w['odd_norm'] + zero, False)])
    flat = lambda a: a.reshape((a.shape[0] * a.shape[1],) + a.shape[2:])
    win_t, wuq_g, wukv_t = [flat(a) for a in gathered]
    win_t = jnp.concatenate([win_t[:KR_END], jnp.zeros((LANES - MLA_ROPE, D), BF16), win_t[KR_END:]], 0)
    wuq_g = gathered[1]
    wuq_t = jnp.concatenate([wuq_g[:, :HD].reshape(HEADS * HD, Q_LORA),
                             wuq_g[:, HD:].reshape(HEADS * MLA_ROPE, Q_LORA)], 0)

    z64 = jnp.zeros((1, 64), F32)
    qn = w['mla_q_norm']
    g_even = [w['mla_q_lat_norm'], w['mla_kv_lat_norm'], jnp.concatenate([w['mla_k_rope_norm'], z64], 1),
              w['gqa_q_norm'], w['gqa_k_norm']]
    g_mla = [qn[:, :HD], jnp.concatenate([qn[:, HD:], qn[:, HD:]], 1), w['mla_k_nope_norm']]
    g_swa = [jnp.concatenate([w['swa_q_norm']] * 2, 1), jnp.concatenate([w['swa_k_norm']] * 2, 1)]

    def loss_head(acc, res, t):
        e = acc + res - t
        return e * (1.0 / D), _csum(_rsum(e * e))

    def mlp_fwd(l, xin, last=False):
        hn = _rmsnorm(f"mlp{l}_norm", xin, w['mlp_norm'][l:l + 1])
        a, rl = _mm(f"mlp{l}_up", hn, wup_t[l], "nt", (BF16, BF16), epilogue=_relu2)
        if last:
            xout = _mm(f"mlp{l}_down", a, wdn[l], "nn", (F32,), epilogue=loss_head, extras=(xin, tgt),
                       sums=[_sds((1, 1), F32)])
        else:
            xout = _mm(f"mlp{l}_down", a, wdn[l], "nn", (F32,), epilogue=_add, extras=(xin,))
        return xout, (hn, a, rl)

    proj = _mm("even_in", h0, win_t, "nt", (F32,), tm=512, tn=P_END)
    cqn, ckvn, krd, qg, kg, vg = _even_prep("even_prep", proj, tab_mla + tab_ax, g_even, after=[tok])
    qa = _mm("mla_uq", cqn, wuq_t, "nt", (F32,))
    kv = _mm("mla_ukv", ckvn, wukv_t, "nt", (F32,))
    q_a, k_a, v_a = _mla_prep("mla_prep", qa, kv, krd, tab_mla, g_mla)
    o_a, lse_a = _flash_fwd("mla_attn", q_a, k_a, v_a)
    o_g, lse_g = _flash_fwd("gqa_attn", qg, kg, vg)
    merged = jnp.concatenate([o_a, o_g], 1)
    rest = _exchange_wait("gather_rest_wait", later, merged)
    wout_e, wqkv_t, wout_o, wup0_t, wup1_t, wdn0, wdn1 = [flat(a) for a in rest[:7]]
    odd_norm = rest[7].reshape(1, D)
    wup_t, wdn = (wup0_t, wup1_t), (wdn0, wdn1)
    x1 = _mm("even_out", merged, wout_e, "nn", (F32,), epilogue=_add, extras=(x,))
    x2, mlp0 = mlp_fwd(0, x1)

    h1 = _rmsnorm("odd_norm", x2, odd_norm)
    qkv = _mm("odd_qkv", h1, wqkv_t, "nt", (F32,))
    q_s, k_s, v_s = _swa_prep("swa_prep", qkv, tab_swa, g_swa)
    o_s, lse_s = _swa_fwd("swa_attn", q_s, k_s, v_s, w['swa_sink'])
    x3 = _mm("odd_out", o_s, wout_o, "nn", (F32,), epilogue=_add, extras=(x2,))
    (dy, loss_acc), mlp1 = mlp_fwd(1, x3, last=True)
    loss = lax.psum(0.5 / D * loss_acc[0, 0], ("x", "y", "c"))

    gsm = {}

    def mlp_bwd(l, dout, xin, saved):
        hn, a, rl = saved
        du = _mm(f"mlp{l}_dact", dout, wdn[l], "nt", (BF16,), epilogue=_mul2, extras=(rl,))
        g_dn = _mm(f"mlp{l}_gdown", a, dout, "tn", (BF16,))
        g_up = _mm(f"mlp{l}_gup", du, hn, "tn", (BF16,))
        dhn = _mm(f"mlp{l}_dnorm", du, wup_t[l], "nn", (F32,))
        din, g_n = _rmsnorm_bwd(f"mlp{l}_norm_bwd", dout, xin, dhn, w['mlp_norm'][l:l + 1])
        return din, g_dn, g_up, g_n

    dx3, g_dn1, g_up1, g_mn1 = mlp_bwd(1, dy, x3, mlp1)
    split = lambda a: a.reshape((N_DEV, a.shape[0] // N_DEV) + a.shape[1:])
    sc1, tok1 = _exchange_start("scatter_mlp1_start", [(split(g_up1), True), (split(g_dn1), True)])

    g_wout_o = _mm("odd_gout", o_s, dx3, "tn", (BF16,))
    do_s = _mm("odd_dattn", dx3, wout_o, "nt", (BF16,))
    dl_s = _delta_t("swa_delta", do_s, o_s, after=[tok1])
    dq_s, dsink = _swa_bwd_q("swa_bwd_q", q_s, k_s, v_s, do_s, lse_s, dl_s, w['swa_sink'])
    dk_s, dv_s = _swa_bwd_kv("swa_bwd_kv", q_s, k_s, v_s, do_s, lse_s, dl_s)
    dqkv, g_sq, g_sk = _swa_prep_bwd("swa_prep_bwd", dq_s, dk_s, dv_s, qkv, tab_swa, g_swa)
    g_wqkv = _mm("odd_gqkv", dqkv, h1, "tn", (BF16,))
    dh1 = _mm("odd_dnorm", dqkv, wqkv_t, "nn", (F32,), tk=dqkv.shape[1])
    dx2, g_on = _rmsnorm_bwd("odd_norm_bwd", dx3, x2, dh1, odd_norm)
    gsm['swa_q_norm'] = g_sq[:, :64] + g_sq[:, 64:]
    gsm['swa_k_norm'] = g_sk[:, :64] + g_sk[:, 64:]
    gsm['swa_sink'] = dsink[:, 0].reshape(1, SWA_HEADS)

    dx1, g_dn0, g_up0, g_mn0 = mlp_bwd(0, dx2, x1, mlp0)
    sc2, tok2 = _exchange_start("scatter_mid_start", [(split(g_wout_o), True), (split(g_wqkv), True),
                                                      (split(g_up0), True), (split(g_dn0), True)])
    gsm['mlp_norm'] = jnp.concatenate([g_mn0, g_mn1], 0)

    g_wout_e = _mm("even_gout", merged, dx1, "tn", (BF16,))
    dmerged = _mm("even_dattn", dx1, wout_e, "nt", (BF16,))
    dl_e = _delta("even_delta", dmerged, merged, after=[tok2])
    dq_a, dk_a, dv_a = _flash_bwd("mla_attn_bwd", q_a, k_a, v_a, dmerged, 0, lse_a, dl_e)
    dq_g, dk_g, dv_g = _flash_bwd("gqa_attn_bwd", qg, kg, vg, dmerged, HEADS, lse_g, dl_e)
    dqa, dkv, dkrd, g_qnn, g_qnr, g_kn = _mla_prep_bwd("mla_prep_bwd", dq_a, dk_a, dv_a, qa, kv, tab_mla, g_mla)
    g_wuq = _mm("mla_guq", dqa, cqn, "tn", (BF16,))
    dcqn = _mm("mla_dq_lat", dqa, wuq_t, "nn", (F32,))
    g_wukv = _mm("mla_gukv", dkv, ckvn, "tn", (BF16,))
    dckvn = _mm("mla_dkv_lat", dkv, wukv_t, "nn", (F32,))
    g_wuq = jnp.concatenate([g_wuq[:HEADS * HD].reshape(HEADS, HD, Q_LORA),
                             g_wuq[HEADS * HD:].reshape(HEADS, MLA_ROPE, Q_LORA)], 1)
    sc3, tok_e = _exchange_start("scatter_even_start", [(g_wuq, True), (split(g_wukv), True), (split(g_wout_e), True)])
    dproj, g_ql, g_kvl, g_kr, g_gq, g_gk = _even_prep_bwd("even_prep_bwd", dcqn, dckvn, dkrd, dq_g, dk_g, dv_g,
                                                          proj, tab_mla + tab_ax, g_even, after=[tok_e])
    g_win = _mm("even_gin", dproj, h0, "tn", (BF16,), tm=P_END, tk=1024)
    dh0 = _mm("even_dnorm", dproj, win_t, "nn", (F32,), tk=P_END)
    grad_x, g_en = _rmsnorm_bwd("even_norm_bwd", dx1, x, dh0, w['even_norm'])
    gsm.update(even_norm=g_en, mla_q_lat_norm=g_ql, mla_kv_lat_norm=g_kvl,
               mla_q_norm=jnp.concatenate([g_qnn, g_qnr[:, :64] + g_qnr[:, 64:]], 1), mla_k_nope_norm=g_kn,
               mla_k_rope_norm=g_kr[:, :64], gqa_q_norm=g_gq, gqa_k_norm=g_gk)

    g_win = jnp.concatenate([g_win[:KR_END], g_win[P_QG:]], 0)
    small_sizes = [w[n].size for n in _SMALL] + [D]
    small_vec = jnp.concatenate([gsm[n].reshape(1, -1) for n in _SMALL] + [g_on], 1)
    pad = (-small_vec.shape[1]) % LANES
    small_vec = jnp.pad(small_vec, ((0, 0), (0, pad)))
    last_st, tok3 = _exchange_start("scatter_last_start", [(split(g_win), True), (small_vec, False)])
    p_up1, p_dn1 = _exchange_wait("scatter_mlp1_wait", sc1, tok3)
    p_wout_o, p_wqkv, p_up0, p_dn0 = _exchange_wait("scatter_mid_wait", sc2, tok3)
    p_wuq, p_wukv, p_wout_e = _exchange_wait("scatter_even_wait", sc3, tok3)
    parts = dict(odd_w_qkv=p_wqkv, odd_w_out=p_wout_o, up0=p_up0, up1=p_up1, dn0=p_dn0, dn1=p_dn1,
                 mla_w_uq=p_wuq, mla_w_ukv=p_wukv, even_w_out=p_wout_e)
    red = {n: _sum8("sum_" + n, p) for n, p in parts.items()}
    grads = {
        'odd_w_qkv': red['odd_w_qkv'].T[None], 'odd_w_out': red['odd_w_out'][None],
        'mlp_w_up': jnp.stack([red['up0'].T, red['up1'].T]), 'mlp_w_down': jnp.stack([red['dn0'], red['dn1']]),
        'mla_w_uq': red['mla_w_uq'].T[None], 'mla_w_ukv': red['mla_w_ukv'].T[None],
        'even_w_out': red['even_w_out'][None],
    }
    delta, new_m, new_v = {}, {}, {}

    def apply(names):
        for n in names:
            shp = w[n].shape
            two = lambda a: a.reshape(shp[0] * shp[1], shp[2])
            d_, m_, v_ = _adamw("adamw_" + n, two(w[n]), two(grads[n]), two(m[n]), two(v[n]))
            delta[n], new_m[n], new_v[n] = d_.reshape(shp), m_.reshape(shp), v_.reshape(shp)

    apply(['mla_w_uq', 'mla_w_ukv', 'even_w_out', 'odd_w_qkv', 'odd_w_out', 'mlp_w_up', 'mlp_w_down'])
    last = _exchange_wait("scatter_last_wait", last_st, new_v['mlp_w_down'])
    small_g = _sum8("sum_small", last[1])
    grads['even_w_in'] = _sum8("sum_even_w_in", last[0]).T[None]
    apply(['even_w_in'])
    off = 0
    for n, sz in zip(_SMALL + ['odd_norm_full'], small_sizes):
        seg = small_g[:, off:off + sz]
        off += sz
        if n == 'odd_norm_full':
            me = 4 * lax.axis_index("x") + 2 * lax.axis_index("y") + lax.axis_index("c")
            grads['odd_norm'] = lax.dynamic_slice(seg, (0, me * (D // N_DEV)), (1, D // N_DEV))
        else:
            grads[n] = seg.reshape(w[n].shape)

    sm_names = _SMALL + ['odd_norm']
    pack = lambda d: jnp.concatenate([d[n].reshape(1, -1) for n in sm_names], 1)
    pw, pg, pm, pv = pack(w), pack(grads), pack(m), pack(v)
    padw = (-pw.shape[1]) % LANES
    padf = lambda a: jnp.pad(a, ((0, 0), (0, padw)))
    d_, m_, v_ = _adamw("adamw_small", padf(pw), padf(pg), padf(pm), jnp.pad(pv, ((0, 0), (0, padw)), constant_values=1.0))
    off = 0
    for n in sm_names:
        sz = w[n].size
        delta[n] = d_[:, off:off + sz].reshape(w[n].shape)
        new_m[n] = m_[:, off:off + sz].reshape(w[n].shape)
        new_v[n] = v_[:, off:off + sz].reshape(w[n].shape)
        off += sz

    return (loss, grad_x[None], *[grads[n] for n in _WEIGHTS], *[delta[n] for n in _WEIGHTS],
            *[new_m[n] for n in _WEIGHTS], *[new_v[n] for n in _WEIGHTS])


def kernel(x, even_norm, even_w_in, mla_q_lat_norm, mla_kv_lat_norm, mla_w_uq, mla_w_ukv, mla_q_norm, mla_k_nope_norm, mla_k_rope_norm, gqa_q_norm, gqa_k_norm, even_w_out, odd_norm, odd_w_qkv, swa_q_norm, swa_k_norm, swa_sink, odd_w_out, mlp_norm, mlp_w_up, mlp_w_down, loss_target, m_even_norm, m_even_w_in, m_mla_q_lat_norm, m_mla_kv_lat_norm, m_mla_w_uq, m_mla_w_ukv, m_mla_q_norm, m_mla_k_nope_norm, m_mla_k_rope_norm, m_gqa_q_norm, m_gqa_k_norm, m_even_w_out, m_odd_norm, m_odd_w_qkv, m_swa_q_norm, m_swa_k_norm, m_swa_sink, m_odd_w_out, m_mlp_norm, m_mlp_w_up, m_mlp_w_down, v_even_norm, v_even_w_in, v_mla_q_lat_norm, v_mla_kv_lat_norm, v_mla_w_uq, v_mla_w_ukv, v_mla_q_norm, v_mla_k_nope_norm, v_mla_k_rope_norm, v_gqa_q_norm, v_gqa_k_norm, v_even_w_out, v_odd_norm, v_odd_w_qkv, v_swa_q_norm, v_swa_k_norm, v_swa_sink, v_odd_w_out, v_mlp_norm, v_mlp_w_up, v_mlp_w_down):
    ws = (even_norm, even_w_in, mla_q_lat_norm, mla_kv_lat_norm, mla_w_uq, mla_w_ukv, mla_q_norm, mla_k_nope_norm, mla_k_rope_norm, gqa_q_norm, gqa_k_norm, even_w_out, odd_norm, odd_w_qkv, swa_q_norm, swa_k_norm, swa_sink, odd_w_out, mlp_norm, mlp_w_up, mlp_w_down)
    ms = (m_even_norm, m_even_w_in, m_mla_q_lat_norm, m_mla_kv_lat_norm, m_mla_w_uq, m_mla_w_ukv, m_mla_q_norm, m_mla_k_nope_norm, m_mla_k_rope_norm, m_gqa_q_norm, m_gqa_k_norm, m_even_w_out, m_odd_norm, m_odd_w_qkv, m_swa_q_norm, m_swa_k_norm, m_swa_sink, m_odd_w_out, m_mlp_norm, m_mlp_w_up, m_mlp_w_down)
    vs = (v_even_norm, v_even_w_in, v_mla_q_lat_norm, v_mla_kv_lat_norm, v_mla_w_uq, v_mla_w_ukv, v_mla_q_norm, v_mla_k_nope_norm, v_mla_k_rope_norm, v_gqa_q_norm, v_gqa_k_norm, v_even_w_out, v_odd_norm, v_odd_w_qkv, v_swa_q_norm, v_swa_k_norm, v_swa_sink, v_odd_w_out, v_mlp_norm, v_mlp_w_up, v_mlp_w_down)
    return _step(x[0], loss_target[0], dict(zip(_WEIGHTS, ws)), dict(zip(_WEIGHTS, ms)), dict(zip(_WEIGHTS, vs)))
```

```python
import jax
import jax.numpy as jnp
from jax import lax
from jax.experimental import pallas as pl
from jax.experimental.pallas import tpu as pltpu

F32 = jnp.float32
BF16 = jnp.bfloat16

N_DEV = 8
NORM_EPS = 1e-6
ROPE_THETA = 500000.0
AXIAL_THETA = 10000.0
GRID_W = 64
HEADS = 8
GQA_KV = 2
HD = 128
MLA_ROPE = 64
MLA_QK = HD + MLA_ROPE
Q_LORA = 512
KV_LORA = 256
SWA_HEADS = 32
SWA_KV = 4
SWA_D = 64
SWA_ROT = 16
SWA_WINDOW = 128
SWA_BLOCK = 128
MLA_SCALE, GQA_SCALE, SWA_SCALE = MLA_QK ** -0.5, HD ** -0.5, SWA_D ** -0.5
LANES = 128
ADAM_LR, ADAM_B1, ADAM_B2, ADAM_EPS, ADAM_WD, ADAM_STEP = 0.001, 0.9, 0.999, 1e-08, 0.01, 10
VMEM_LIMIT = 56 * 1024 * 1024
MM_TILE = (1024, 1024, 2048)
FWD_TILE = (1024, 8192)
BWD_TILE = (2048, 1024)
ROW_TILE = 256

P_CQ, P_CKV, P_KR, P_QG = 0, Q_LORA, Q_LORA + KV_LORA, Q_LORA + KV_LORA + LANES
P_KG = P_QG + HEADS * HD
P_VG = P_KG + GQA_KV * HD
P_END = P_VG + GQA_KV * HD
KR_END = Q_LORA + KV_LORA + MLA_ROPE


def _tile(n, prefs):
    for t in prefs:
        if n % t == 0 and t <= n:
            return t
    return n


def _params(sem):
    return pltpu.CompilerParams(dimension_semantics=sem, vmem_limit_bytes=VMEM_LIMIT)


_DIMS = {"nn": ((1,), (0,)), "nt": ((1,), (1,)), "tn": ((0,), (0,))}


def _mm(name, a, b, mode, out_dtypes, epilogue=None, extras=(), sums=(), tm=MM_TILE[0], tn=MM_TILE[1], tk=MM_TILE[2]):
    if mode == "nn":
        (M, K), (_, N) = a.shape, b.shape
    elif mode == "nt":
        (M, K), (N, _) = a.shape, b.shape
    else:
        (K, M), (_, N) = a.shape, b.shape
    tm = _tile(M, (tm, 512, 256, 128))
    tn = _tile(N, (tn, 512, 256, 128))
    tk = _tile(K, (tk, 1024, 512, 256, 128))
    nk = K // tk
    ne, no, ns = len(extras), len(out_dtypes), len(sums)
    if mode == "tn":
        a_spec = pl.BlockSpec((tk, tm), lambda i, j, k: (k, i))
    else:
        a_spec = pl.BlockSpec((tm, tk), lambda i, j, k: (i, k))
    if mode == "nt":
        b_spec = pl.BlockSpec((tn, tk), lambda i, j, k: (j, k))
    else:
        b_spec = pl.BlockSpec((tk, tn), lambda i, j, k: (k, j))
    o_spec = pl.BlockSpec((tm, tn), lambda i, j, k: (i, j))
    dims = (_DIMS[mode], ((), ()))

    def body(a_ref, b_ref, *rest):
        ex, outs = rest[:ne], rest[ne:ne + no]
        k = pl.program_id(2)
        part = lax.dot_general(a_ref[...].astype(BF16), b_ref[...].astype(BF16), dims, preferred_element_type=F32)

        def finish(total):
            res = epilogue(total, *[e[...] for e in ex]) if epilogue else (total,)
            for o, r in zip(outs, res[:no]):
                o[...] = r.astype(o.dtype)
            if ns:
                first = (pl.program_id(0) == 0) & (pl.program_id(1) == 0)

                @pl.when(first)
                def _():
                    for s_ref, r in zip(rest[ne + no:ne + no + ns], res[no:]):
                        s_ref[...] = r

                @pl.when(jnp.logical_not(first))
                def _():
                    for s_ref, r in zip(rest[ne + no:ne + no + ns], res[no:]):
                        s_ref[...] += r

        if nk == 1:
            finish(part)
            return
        acc = rest[ne + no + ns]

        @pl.when(k == 0)
        def _():
            acc[...] = part

        @pl.when((k > 0) & (k < nk - 1))
        def _():
            acc[...] += part

        @pl.when(k == nk - 1)
        def _():
            finish(acc[...] + part)

    out = pl.pallas_call(
        body, name=name, grid=(M // tm, N // tn, nk),
        in_specs=[a_spec, b_spec] + [o_spec] * ne,
        out_specs=[o_spec] * no + [pl.BlockSpec(tuple(s.shape), lambda i, j, k: (0,) * len(s.shape)) for s in sums],
        out_shape=[jax.ShapeDtypeStruct((M, N), d) for d in out_dtypes] + list(sums),
        scratch_shapes=[pltpu.VMEM((tm, tn), F32)] if nk > 1 else [],
        compiler_params=_params(("arbitrary",) * 3 if ns else ("parallel", "parallel", "arbitrary")),
    )(a, b, *extras)
    return out[0] if no + ns == 1 else out


def _rowwise(name, fn, rows, consts, outs, accs=(), tm=ROW_TILE, after=()):
    S = rows[0].shape[-2]
    tm = _tile(S, (tm, 128, 64, 32, 16, 8))
    nr, nc, nd, no, na = len(rows), len(consts), len(after), len(outs), len(accs)

    def rspec(shape):
        if len(shape) == 2:
            return pl.BlockSpec((tm, shape[1]), lambda i: (i, 0))
        return pl.BlockSpec((shape[0], tm, shape[2]), lambda i: (0, i, 0))

    def cspec(shape):
        return pl.BlockSpec(tuple(shape), lambda i: (0,) * len(shape))

    def ospec(o):
        if isinstance(o, tuple):
            return pl.BlockSpec((o[1].shape[0], o[1].shape[1], tm), lambda i: (0, 0, i))
        return rspec(o.shape)

    out_sds = [o[1] if isinstance(o, tuple) else o for o in outs]

    def body(*refs):
        r, c = refs[:nr], refs[nr:nr + nc]
        o, a = refs[nr + nc + nd:nr + nc + nd + no], refs[nr + nc + nd + no:]
        vals = fn(r, c, o)
        if na:
            @pl.when(pl.program_id(0) == 0)
            def _():
                for ar in a:
                    ar[...] = jnp.zeros_like(ar)

            for ar, v in zip(a, vals):
                ar[...] += v

    res = pl.pallas_call(
        body, name=name, grid=(S // tm,),
        in_specs=[rspec(x.shape) for x in rows] + [cspec(x.shape) for x in consts]
        + [pl.BlockSpec(memory_space=pl.ANY)] * len(after),
        out_specs=[ospec(x) for x in outs] + [cspec(x.shape) for x in accs],
        out_shape=out_sds + list(accs),
        compiler_params=_params(("arbitrary",) if na else ("parallel",)),
    )(*rows, *consts, *after)
    return res


def _sds(shape, dtype):
    return jax.ShapeDtypeStruct(tuple(shape), dtype)


def _rsum(x):
    return jnp.sum(x, axis=-1, keepdims=True)


def _csum(x):
    return jnp.sum(x, axis=0, keepdims=True)


def _lane(shape):
    return lax.broadcasted_iota(jnp.int32, shape, 1)


def _partner(x, seg, half):
    lane = _lane(x.shape) % seg
    return jnp.where(lane < half, pltpu.roll(x, LANES - half, 1), pltpu.roll(x, half, 1))


def _rope(x, c, s, seg, half):
    return x * c + _partner(x, seg, half) * s


def _rope_bwd(dy, c, s, seg, half):
    t = _partner(dy * s, seg, half)
    if seg != 2 * half:
        t = jnp.where(_lane(dy.shape) % seg < 2 * half, t, 0.0)
    return dy * c + t


def _rms_r(ss, n):
    return lax.rsqrt(ss * (1.0 / n) + NORM_EPS)


def _rms_bwd(x, r, g, dy, dot_scale):
    dyg = dy * g
    return r * dyg - x * (r * r * r) * (_rsum(dyg * x) * dot_scale)


def _rmsnorm(name, x, g, after=()):
    D = x.shape[1]

    def fn(r, c, o):
        xv = r[0][...]
        o[0][...] = (xv * _rms_r(_rsum(xv * xv), D) * c[0][...]).astype(BF16)
        return ()

    return _rowwise(name, fn, [x], [g], [_sds(x.shape, BF16)], tm=4 * ROW_TILE, after=after)[0]


def _rmsnorm_bwd(name, dres, x, dh, g):
    D = x.shape[1]

    def fn(r, c, o):
        xv, dhv, gv = r[1][...], r[2][...], c[0][...]
        rr = _rms_r(_rsum(xv * xv), D)
        o[0][...] = r[0][...] + _rms_bwd(xv, rr, gv, dhv, 1.0 / D)
        return (_csum(dhv * xv * rr),)

    return _rowwise(name, fn, [dres, x, dh], [g], [_sds(x.shape, F32)], [_sds((1, D), F32)], tm=2 * ROW_TILE)


def _even_prep(name, proj, tabs, gains, after=()):
    S = proj.shape[0]
    cm, sm, ca, sa = tabs

    def fn(r, c, o):
        p, cmv, smv, cav, sav = r[0], r[1][...], r[2][...], r[3][...], r[4][...]
        g_ql, g_kvl, g_kr, g_q, g_k = (x[...] for x in c)
        ones = _seg_ones(LANES)
        cq = p[:, P_CQ:P_CKV]
        o[0][...] = (cq * _wide(_rms_r(_seg_sum(_fold(cq * cq), ones), Q_LORA), Q_LORA // LANES) * g_ql).astype(BF16)
        ckv = p[:, P_CKV:P_KR]
        o[1][...] = (ckv * _wide(_rms_r(_seg_sum(_fold(ckv * ckv), ones), KV_LORA), KV_LORA // LANES) * g_kvl).astype(BF16)
        kr = p[:, P_KR:P_QG]
        y = _rope(kr * _rms_r(_seg_sum(kr * kr, ones), MLA_ROPE) * g_kr, cmv, smv, 64, 32)
        o[2][...] = (y + pltpu.roll(y, 64, 1)).astype(BF16)
        for h in range(HEADS):
            xh = p[:, P_QG + HD * h:P_QG + HD * (h + 1)]
            o[3][h] = (_rope(xh * _rms_r(_seg_sum(xh * xh, ones), HD) * g_q, cav, sav, 64, 32) * GQA_SCALE).astype(BF16)
        for h in range(GQA_KV):
            xh = p[:, P_KG + HD * h:P_KG + HD * (h + 1)]
            o[4][h] = _rope(xh * _rms_r(_seg_sum(xh * xh, ones), HD) * g_k, cav, sav, 64, 32).astype(BF16)
            o[5][h] = p[:, P_VG + HD * h:P_VG + HD * (h + 1)].astype(BF16)
        return ()

    outs = [_sds((S, Q_LORA), BF16), _sds((S, KV_LORA), BF16), _sds((S, LANES), BF16),
            _sds((HEADS, S, HD), BF16), _sds((GQA_KV, S, HD), BF16), _sds((GQA_KV, S, HD), BF16)]
    return _rowwise(name, fn, [proj, cm, sm, ca, sa], gains, outs, tm=2 * ROW_TILE, after=after)


def _even_prep_bwd(name, dcqn, dckvn, dkrd, dqg, dkg, dvg, proj, tabs, gains, after=()):
    S = proj.shape[0]
    cm, sm, ca, sa = tabs

    def fn(r, c, o):
        p, cmv, smv, cav, sav = r[6], r[7][...], r[8][...], r[9][...], r[10][...]
        g_ql, g_kvl, g_kr, g_q, g_k = (x[...] for x in c)
        out = o[0]
        ones = _seg_ones(LANES)

        def wide_bwd(x, g, d, n):
            w_ = n // LANES
            rr = _wide(_rms_r(_seg_sum(_fold(x * x), ones), n), w_)
            dot = _wide(_seg_sum(_fold(d * g * x), ones) * (1.0 / n), w_)
            return _rms_bwd_b(x, rr, g, d, dot).astype(BF16), _csum(d * x * rr)

        cq = p[:, P_CQ:P_CKV]
        out[:, P_CQ:P_CKV], a_ql = wide_bwd(cq, g_ql, r[0][...], Q_LORA)
        ckv = p[:, P_CKV:P_KR]
        out[:, P_CKV:P_KR], a_kvl = wide_bwd(ckv, g_kvl, r[1][...], KV_LORA)
        kr = p[:, P_KR:P_QG]
        rr = _rms_r(_seg_sum(kr * kr, ones), MLA_ROPE)
        d = r[2][...]
        d = d + pltpu.roll(d, 64, 1)
        d = _rope_bwd(d, cmv, smv, 64, 32)
        low = _lane(d.shape) < 64
        dot = _seg_sum(jnp.where(low, d * g_kr * kr, 0.0), ones) * (1.0 / MLA_ROPE)
        out[:, P_KR:P_QG] = jnp.where(low, _rms_bwd_b(kr, rr, g_kr, d, dot), 0.0).astype(BF16)
        a_kr = _csum(d * kr * rr)
        a_q = jnp.zeros((1, HD), F32)

        def head_bwd(xh, g, d):
            rr = _rms_r(_seg_sum(xh * xh, ones), HD)
            dot = _seg_sum(d * g * xh, ones) * (1.0 / HD)
            return _rms_bwd_b(xh, rr, g, d, dot).astype(BF16), _csum(d * xh * rr)

        for h in range(HEADS):
            xh = p[:, P_QG + HD * h:P_QG + HD * (h + 1)]
            d = _rope_bwd(r[3][h] * GQA_SCALE, cav, sav, 64, 32)
            out[:, P_QG + HD * h:P_QG + HD * (h + 1)], inc = head_bwd(xh, g_q, d)
            a_q = a_q + inc
        a_k = jnp.zeros((1, HD), F32)
        grp = HEADS // GQA_KV
        for h in range(GQA_KV):
            xh = p[:, P_KG + HD * h:P_KG + HD * (h + 1)]
            d = r[4][grp * h]
            dv = r[5][grp * h]
            for e in range(1, grp):
                d = d + r[4][grp * h + e]
                dv = dv + r[5][grp * h + e]
            d = _rope_bwd(d, cav, sav, 64, 32)
            out[:, P_KG + HD * h:P_KG + HD * (h + 1)], inc = head_bwd(xh, g_k, d)
            out[:, P_VG + HD * h:P_VG + HD * (h + 1)] = dv.astype(BF16)
            a_k = a_k + inc
        return (a_ql, a_kvl, a_kr, a_q, a_k)

    accs = [_sds((1, Q_LORA), F32), _sds((1, KV_LORA), F32), _sds((1, LANES), F32),
            _sds((1, HD), F32), _sds((1, HD), F32)]
    return _rowwise(name, fn, [dcqn, dckvn, dkrd, dqg, dkg, dvg, proj, cm, sm, ca, sa], gains,
                    [_sds((S, P_END), BF16)], accs, after=after)


def _mla_q_parts(qa, h, rp, ones):
    nope = qa[:, HD * h:HD * (h + 1)]
    mine = (_lane(rp.shape) >= 64) == bool(h % 2)
    rpm = jnp.where(mine, rp, 0.0)
    rr = _rms_r(_seg_sum(nope * nope + rpm * rpm, ones), MLA_QK)
    return nope, rpm, mine, rr


def _mla_prep(name, qa, kv, krd, tabs, gains):
    S = qa.shape[0]
    cm, sm = tabs

    def fn(r, c, o):
        qa_r, kv_r, krd_v, cmv, smv = r[0], r[1], r[2][...], r[3][...], r[4][...]
        g_n, g_r, g_kn = (x[...] for x in c)
        ones = _seg_ones(LANES)
        for h in range(HEADS):
            rp = qa_r[:, HEADS * HD + LANES * (h // 2):HEADS * HD + LANES * (h // 2 + 1)]
            nope, rpm, mine, rr = _mla_q_parts(qa_r, h, rp, ones)
            o[0][h, :, 0:HD] = (nope * (rr * MLA_SCALE) * g_n).astype(BF16)
            o[0][h, :, HD:2 * HD] = _rope(rpm * (rr * MLA_SCALE) * g_r, cmv, smv, 64, 32).astype(BF16)
            kn = kv_r[:, 2 * HD * h:2 * HD * h + HD]
            o[1][h, :, 0:HD] = (kn * _rms_r(_seg_sum(kn * kn, ones), HD) * g_kn).astype(BF16)
            o[1][h, :, HD:2 * HD] = krd_v
            o[2][h] = kv_r[:, 2 * HD * h + HD:2 * HD * (h + 1)].astype(BF16)
        return ()

    outs = [_sds((HEADS, S, 2 * HD), BF16), _sds((HEADS, S, 2 * HD), BF16), _sds((HEADS, S, HD), BF16)]
    return _rowwise(name, fn, [qa, kv, krd, cm, sm], gains, outs, tm=2 * ROW_TILE)


def _mla_prep_bwd(name, dq, dk, dv, qa, kv, tabs, gains):
    S = qa.shape[0]
    cm, sm = tabs

    def fn(r, c, o):
        dq_r, dk_r, dv_r, qa_r, kv_r, cmv, smv = r[0], r[1], r[2], r[3], r[4], r[5][...], r[6][...]
        g_n, g_r, g_kn = (x[...] for x in c)
        a_n = jnp.zeros((1, HD), F32)
        a_r = jnp.zeros((1, LANES), F32)
        a_kn = jnp.zeros((1, HD), F32)
        dkrd = jnp.zeros(cmv.shape, F32)
        drp = None
        ones = _seg_ones(LANES)
        for h in range(HEADS):
            rp = qa_r[:, HEADS * HD + LANES * (h // 2):HEADS * HD + LANES * (h // 2 + 1)]
            nope, rpm, mine, rr = _mla_q_parts(qa_r, h, rp, ones)
            dn = dq_r[h, :, 0:HD] * MLA_SCALE
            dr = _rope_bwd(jnp.where(mine, dq_r[h, :, HD:2 * HD] * MLA_SCALE, 0.0), cmv, smv, 64, 32)
            dot = _seg_sum(dn * g_n * nope + dr * g_r * rpm, ones) * (1.0 / MLA_QK)
            r3 = rr * rr * rr
            o[0][:, HD * h:HD * (h + 1)] = (rr * dn * g_n - nope * r3 * dot).astype(BF16)
            part = jnp.where(mine, rr * dr * g_r - rpm * r3 * dot, 0.0)
            drp = part if h % 2 == 0 else drp + part
            if h % 2 == 1:
                o[0][:, HEADS * HD + LANES * (h // 2):HEADS * HD + LANES * (h // 2 + 1)] = drp.astype(BF16)
            a_n = a_n + _csum(dn * nope * rr)
            a_r = a_r + _csum(dr * rpm * rr)
            kn = kv_r[:, 2 * HD * h:2 * HD * h + HD]
            rk = _rms_r(_seg_sum(kn * kn, ones), HD)
            dkn = dk_r[h, :, 0:HD]
            dotk = _seg_sum(dkn * g_kn * kn, ones) * (1.0 / HD)
            o[1][:, 2 * HD * h:2 * HD * h + HD] = _rms_bwd_b(kn, rk, g_kn, dkn, dotk).astype(BF16)
            o[1][:, 2 * HD * h + HD:2 * HD * (h + 1)] = dv_r[h].astype(BF16)
            a_kn = a_kn + _csum(dkn * kn * rk)
            dkrd = dkrd + dk_r[h, :, HD:2 * HD]
        o[2][...] = dkrd
        return (a_n, a_r, a_kn)

    outs = [_sds(qa.shape, BF16), _sds(kv.shape, BF16), _sds((S, LANES), F32)]
    accs = [_sds((1, HD), F32), _sds((1, LANES), F32), _sds((1, HD), F32)]
    return _rowwise(name, fn, [dq, dk, dv, qa, kv, cm, sm], gains, outs, accs)


def _seg_ones(seg):
    r = lax.broadcasted_iota(jnp.int32, (LANES, LANES), 0) // seg
    c = lax.broadcasted_iota(jnp.int32, (LANES, LANES), 1) // seg
    return (r == c).astype(F32)


def _dot01(x, sel, dims):
    hi = x.astype(BF16)
    lo = (x - hi.astype(F32)).astype(BF16)
    sb = sel.astype(BF16)
    if dims is _NN:
        return (lax.dot_general(hi, sb, dims, preferred_element_type=F32)
                + lax.dot_general(lo, sb, dims, preferred_element_type=F32))
    return (lax.dot_general(sb, hi, dims, preferred_element_type=F32)
            + lax.dot_general(sb, lo, dims, preferred_element_type=F32))


def _seg_sum(x, ones):
    return _dot01(x, ones, _NN)


def _fold(x):
    acc = x[:, 0:LANES]
    for g in range(1, x.shape[1] // LANES):
        acc = acc + x[:, LANES * g:LANES * (g + 1)]
    return acc


def _wide(v, n):
    return jnp.concatenate([v] * n, axis=1) if n > 1 else v


def _rms_bwd_b(x, r, g, dy, dot_b):
    dyg = dy * g
    return r * dyg - x * (r * r * r) * dot_b


def _seg64_r(x, ones):
    return _rms_r(_seg_sum(x * x, ones), SWA_D)


def _swa_prep(name, qkv, tabs, gains):
    S = qkv.shape[0]
    nq, nk = SWA_HEADS * SWA_D, SWA_KV * SWA_D
    cs, ss = tabs

    def fn(r, c, o):
        x_r, csv, ssv = r[0], r[1][...], r[2][...]
        g_q, g_k = c[0][...], c[1][...]
        ones = _seg_ones(SWA_D)
        for g in range((nq + nk) // LANES):
            x = x_r[:, LANES * g:LANES * (g + 1)]
            rr = _seg64_r(x, ones)
            y = _rope(x * rr * (g_q if g < nq // LANES else g_k), csv, ssv, SWA_D, SWA_ROT // 2)
            if g < nq // LANES:
                o[0][:, LANES * g:LANES * (g + 1)] = (y * SWA_SCALE).astype(BF16)
            else:
                o[1][:, LANES * g - nq:LANES * (g + 1) - nq] = y.astype(BF16)
        o[2][...] = x_r[:, nq + nk:nq + 2 * nk].astype(BF16)
        return ()

    outs = [_sds((S, nq), BF16), _sds((S, nk), BF16), _sds((S, nk), BF16)]
    return _rowwise(name, fn, [qkv, cs, ss], gains, outs, tm=2 * ROW_TILE)


def _swa_prep_bwd(name, dq, dk, dv, qkv, tabs, gains):
    nq, nk = SWA_HEADS * SWA_D, SWA_KV * SWA_D
    cs, ss = tabs

    def fn(r, c, o):
        dq_r, dk_r, x_r, csv, ssv = r[0], r[1], r[3], r[4][...], r[5][...]
        g_q, g_k = c[0][...], c[1][...]
        acc = [jnp.zeros((1, LANES), F32), jnp.zeros((1, LANES), F32)]
        ones = _seg_ones(SWA_D)
        for g in range((nq + nk) // LANES):
            isq = g < nq // LANES
            x = x_r[:, LANES * g:LANES * (g + 1)]
            rr = _seg64_r(x, ones)
            d = dq_r[:, LANES * g:LANES * (g + 1)] * SWA_SCALE if isq else dk_r[:, LANES * g - nq:LANES * (g + 1) - nq]
            d = _rope_bwd(d, csv, ssv, SWA_D, SWA_ROT // 2)
            dyg = d * (g_q if isq else g_k)
            dot = _seg_sum(dyg * x, ones) * (1.0 / SWA_D)
            o[0][:, LANES * g:LANES * (g + 1)] = (rr * dyg - x * (rr * rr * rr) * dot).astype(BF16)
            acc[0 if isq else 1] = acc[0 if isq else 1] + _csum(d * x * rr)
        o[0][:, nq + nk:nq + 2 * nk] = r[2][...].astype(BF16)
        return tuple(acc)

    return _rowwise(name, fn, [dq, dk, dv, qkv, cs, ss], gains, [_sds(qkv.shape, BF16)],
                    [_sds((1, LANES), F32), _sds((1, LANES), F32)])


def _delta(name, do, o_, after=()):
    S, C = do.shape

    def fn(r, c, o):
        for g in range(C // LANES):
            t = r[0][:, LANES * g:LANES * (g + 1)].astype(F32) * r[1][:, LANES * g:LANES * (g + 1)].astype(F32)
            o[0][g] = _rsum(t)
        return ()

    return _rowwise(name, fn, [do, o_], [], [_sds((C // LANES, S, 1), F32)], tm=4 * ROW_TILE, after=after)[0]


def _delta_t(name, do, o_, after=()):
    S, C = do.shape

    def fn(r, c, o):
        row = lax.broadcasted_iota(jnp.int32, (8, LANES), 0)
        sel = (row == _lane((8, LANES)) // SWA_D).astype(F32)
        for g in range(C // LANES):
            t = r[0][:, LANES * g:LANES * (g + 1)].astype(F32) * r[1][:, LANES * g:LANES * (g + 1)].astype(F32)
            both = _dot01(t, sel, _NT)
            o[0][2 * g] = both[0:1, :]
            o[0][2 * g + 1] = both[1:2, :]
        return ()

    return _rowwise(name, fn, [do, o_], [], [("T", _sds((2 * C // LANES, 1, S), F32))], tm=4 * ROW_TILE,
                    after=after)[0]


_NT = (((1,), (1,)), ((), ()))
_NN = (((1,), (0,)), ((), ()))
_TN = (((0,), (0,)), ((), ()))


def _flash_fwd(name, q, k, v, tq=FWD_TILE[0], tk=FWD_TILE[1]):
    H, S, dk = q.shape
    G = H // k.shape[0]
    dv = v.shape[2]
    tq, tk = _tile(S, (tq, 256, 128)), _tile(S, (tk, 256, 128))
    nk = S // tk

    def body(q_ref, k_ref, v_ref, o_ref, lse_ref, m_s, l_s, acc_s):
        j = pl.program_id(2)

        @pl.when(j == 0)
        def _():
            m_s[...] = jnp.full_like(m_s, -jnp.inf)
            l_s[...] = jnp.zeros_like(l_s)
            acc_s[...] = jnp.zeros_like(acc_s)

        s = lax.dot_general(q_ref[...], k_ref[...], _NT, preferred_element_type=F32)
        m_new = jnp.maximum(m_s[...], jnp.max(s, axis=-1, keepdims=True))
        alpha = jnp.exp(m_s[...] - m_new)
        p = jnp.exp(s - m_new)
        l_s[...] = alpha * l_s[...] + _rsum(p)
        acc_s[...] = alpha * acc_s[...] + lax.dot_general(p.astype(BF16), v_ref[...], _NN,
                                                          preferred_element_type=F32)
        m_s[...] = m_new

        @pl.when(j == nk - 1)
        def _():
            o_ref[...] = (acc_s[...] / l_s[...]).astype(o_ref.dtype)
            lse_ref[...] = m_s[...] + jnp.log(l_s[...])

    return pl.pallas_call(
        body, name=name, grid=(H, S // tq, nk),
        in_specs=[pl.BlockSpec((None, tq, dk), lambda h, i, j: (h, i, 0)),
                  pl.BlockSpec((None, tk, dk), lambda h, i, j: (h // G, j, 0)),
                  pl.BlockSpec((None, tk, dv), lambda h, i, j: (h // G, j, 0))],
        out_specs=[pl.BlockSpec((tq, dv), lambda h, i, j: (i, h)),
                   pl.BlockSpec((None, tq, 1), lambda h, i, j: (h, i, 0))],
        out_shape=[_sds((S, H * dv), BF16), _sds((H, S, 1), F32)],
        scratch_shapes=[pltpu.VMEM((tq, 1), F32), pltpu.VMEM((tq, 1), F32), pltpu.VMEM((tq, dv), F32)],
        compiler_params=_params(("parallel", "parallel", "arbitrary")),
    )(q, k, v)


def _flash_bwd(name, q, k, v, do, head0, lse, delta, tq=BWD_TILE[0], tk=BWD_TILE[1]):
    H, S, dk = q.shape
    G = H // k.shape[0]
    dv = v.shape[2]
    tq, tk = _tile(S, (tq, 256, 128)), _tile(S, (tk, 256, 128))
    nq = S // tq

    def body(q_ref, k_ref, v_ref, do_ref, lse_ref, dl_ref, dq_ref, dk_ref, dv_ref, dk_s, dv_s):
        j, i = pl.program_id(1), pl.program_id(2)

        @pl.when(i == 0)
        def _():
            dk_s[...] = jnp.zeros_like(dk_s)
            dv_s[...] = jnp.zeros_like(dv_s)

        qv, kv_, dov = q_ref[...], k_ref[...], do_ref[...]
        s = lax.dot_general(qv, kv_, _NT, preferred_element_type=F32)
        p = jnp.exp(s - lse_ref[...])
        dp = lax.dot_general(dov, v_ref[...], _NT, preferred_element_type=F32)
        ds = (p * (dp - dl_ref[...])).astype(BF16)
        dv_s[...] += lax.dot_general(p.astype(BF16), dov, _TN, preferred_element_type=F32)
        dk_s[...] += lax.dot_general(ds, qv, _TN, preferred_element_type=F32)
        dqi = lax.dot_general(ds, kv_, _NN, preferred_element_type=F32)
        rows = pl.ds(pl.multiple_of(i * tq, tq), tq)

        @pl.when(j == 0)
        def _():
            dq_ref[rows, :] = dqi

        @pl.when(j > 0)
        def _():
            dq_ref[rows, :] += dqi

        @pl.when(i == nq - 1)
        def _():
            dk_ref[...] = dk_s[...]
            dv_ref[...] = dv_s[...]

    return pl.pallas_call(
        body, name=name, grid=(H, S // tk, nq),
        in_specs=[pl.BlockSpec((None, tq, dk), lambda h, j, i: (h, i, 0)),
                  pl.BlockSpec((None, tk, dk), lambda h, j, i: (h // G, j, 0)),
                  pl.BlockSpec((None, tk, dv), lambda h, j, i: (h // G, j, 0)),
                  pl.BlockSpec((tq, dv), lambda h, j, i: (i, head0 + h)),
                  pl.BlockSpec((None, tq, 1), lambda h, j, i: (h, i, 0)),
                  pl.BlockSpec((None, tq, 1), lambda h, j, i: (head0 + h, i, 0))],
        out_specs=[pl.BlockSpec((None, S, dk), lambda h, j, i: (h, 0, 0)),
                   pl.BlockSpec((None, tk, dk), lambda h, j, i: (h, j, 0)),
                   pl.BlockSpec((None, tk, dv), lambda h, j, i: (h, j, 0))],
        out_shape=[_sds((H, S, dk), F32), _sds((H, S, dk), F32), _sds((H, S, dv), F32)],
        scratch_shapes=[pltpu.VMEM((tk, dk), F32), pltpu.VMEM((tk, dv), F32)],
        compiler_params=_params(("parallel", "arbitrary", "arbitrary")),
    )(q, k, v, do, lse, delta)


def _swa_place(ref128, h):
    e, t = h % 2, (h // (SWA_HEADS // SWA_KV)) % 2
    x = ref128.astype(F32)
    if e != t:
        x = pltpu.roll(x, 64, 1)
    return jnp.where((_lane(x.shape) >= 64) == bool(t), x, 0.0).astype(BF16)


def _swa_unplace(y, h):
    e, t = h % 2, (h // (SWA_HEADS // SWA_KV)) % 2
    return pltpu.roll(y, 64, 1) if e != t else y


def _swa_specs(width, nb):
    prev = pl.BlockSpec((SWA_BLOCK, width), lambda i: (jnp.maximum(i - 1, 0), 0))
    cur = pl.BlockSpec((SWA_BLOCK, width), lambda i: (i, 0))
    nxt = pl.BlockSpec((SWA_BLOCK, width), lambda i: (jnp.minimum(i + 1, nb - 1), 0))
    return [prev, cur, nxt]


def _swa_bias_t(i, S):
    shape = (3 * SWA_BLOCK, SWA_BLOCK)
    kpos = (i - 1) * SWA_BLOCK + lax.broadcasted_iota(jnp.int32, shape, 0)
    qpos = i * SWA_BLOCK + lax.broadcasted_iota(jnp.int32, shape, 1)
    ok = (jnp.abs(qpos - kpos) <= SWA_WINDOW) & (kpos >= 0) & (kpos < S)
    return jnp.where(ok, 0.0, -jnp.inf)


def _add_blocks(x, blocks):
    n = x.shape[1] // LANES
    return jnp.concatenate([x[:, LANES * b:LANES * (b + 1)] + blocks[b % len(blocks)] for b in range(n)], axis=1)


def _swa_stack(refs, heads):
    return jnp.concatenate([_swa_place(r[:, LANES * (h // 2):LANES * (h // 2 + 1)], h) for h in heads for r in refs],
                           axis=0)


def _swa_unstack_t(ot, j, out_ref):
    grp = SWA_HEADS // SWA_KV
    for pair in range(grp // 2):
        h = grp * j + 2 * pair
        a = _swa_unplace(ot[:, SWA_BLOCK * 2 * pair:SWA_BLOCK * (2 * pair + 1)].T, h)
        b = _swa_unplace(ot[:, SWA_BLOCK * (2 * pair + 1):SWA_BLOCK * (2 * pair + 2)].T, h + 1)
        out_ref[:, LANES * (h // 2):LANES * (h // 2 + 1)] = jnp.where(_lane(a.shape) < 64, a, b).astype(out_ref.dtype)


def _swa_fwd(name, q, k, v, sink):
    S = q.shape[0]
    nb = S // SWA_BLOCK
    grp = SWA_HEADS // SWA_KV
    smem = pl.BlockSpec(memory_space=pltpu.SMEM)

    def body(sink_ref, q_ref, kp, kc, kn, vp, vc, vn, o_ref, lse_ref):
        i = pl.program_id(0)
        bias = [_swa_bias_t(i, S)]
        kcat = [jnp.concatenate([r[:, LANES * u:LANES * (u + 1)] for r in (kp, kc, kn)], axis=0) for u in range(2)]
        vcat = [jnp.concatenate([r[:, LANES * u:LANES * (u + 1)] for r in (vp, vc, vn)], axis=0) for u in range(2)]
        for j in range(SWA_KV):
            heads = range(grp * j, grp * (j + 1))
            xq = _swa_stack([q_ref], heads)
            sk = jnp.concatenate([jnp.full((1, SWA_BLOCK), sink_ref[0, h], F32) for h in heads], axis=1)
            st = lax.dot_general(kcat[j // 2], xq, _NT, preferred_element_type=F32)
            st = _add_blocks(st, bias)
            m = jnp.maximum(jnp.max(st, axis=0, keepdims=True), sk)
            pt = jnp.exp(st - m)
            den = jnp.sum(pt, axis=0, keepdims=True) + jnp.exp(sk - m)
            ot = lax.dot_general(vcat[j // 2], pt.astype(BF16), _TN, preferred_element_type=F32) * (1.0 / den)
            lse = m + jnp.log(den)
            for e, h in enumerate(heads):
                lse_ref[h] = lse[:, SWA_BLOCK * e:SWA_BLOCK * (e + 1)]
            _swa_unstack_t(ot, j, o_ref)

    return pl.pallas_call(
        body, name=name, grid=(nb,),
        in_specs=[smem, pl.BlockSpec((SWA_BLOCK, q.shape[1]), lambda i: (i, 0))]
        + _swa_specs(k.shape[1], nb) + _swa_specs(v.shape[1], nb),
        out_specs=[pl.BlockSpec((SWA_BLOCK, q.shape[1]), lambda i: (i, 0)),
                   pl.BlockSpec((SWA_HEADS, 1, SWA_BLOCK), lambda i: (0, 0, i))],
        out_shape=[_sds(q.shape, BF16), _sds((SWA_HEADS, 1, S), F32)],
        compiler_params=_params(("parallel",)),
    )(sink, q, k, k, k, v, v, v)


def _swa_bwd_q(name, q, k, v, do, lse_t, delta_t, sink):
    S = q.shape[0]
    nb = S // SWA_BLOCK
    grp = SWA_HEADS // SWA_KV
    smem = pl.BlockSpec(memory_space=pltpu.SMEM)
    row = pl.BlockSpec((SWA_BLOCK, q.shape[1]), lambda i: (i, 0))
    hrow = pl.BlockSpec((SWA_HEADS, 1, SWA_BLOCK), lambda i: (0, 0, i))

    def body(sink_ref, q_ref, do_ref, lse_ref, dl_ref, kp, kc, kn, vp, vc, vn, dq_ref, ds_ref):
        i = pl.program_id(0)

        @pl.when(i == 0)
        def _():
            ds_ref[...] = jnp.zeros_like(ds_ref)

        bias = [_swa_bias_t(i, S)]
        kcat = [jnp.concatenate([r[:, LANES * u:LANES * (u + 1)] for r in (kp, kc, kn)], axis=0) for u in range(2)]
        vcat = [jnp.concatenate([r[:, LANES * u:LANES * (u + 1)] for r in (vp, vc, vn)], axis=0) for u in range(2)]
        for j in range(SWA_KV):
            heads = range(grp * j, grp * (j + 1))
            xq = _swa_stack([q_ref], heads)
            xdo = _swa_stack([do_ref], heads)
            lse_r = jnp.concatenate([lse_ref[h] for h in heads], axis=1)
            dl_r = jnp.concatenate([dl_ref[h] for h in heads], axis=1)
            st = lax.dot_general(kcat[j // 2], xq, _NT, preferred_element_type=F32)
            pt = jnp.exp(_add_blocks(st, bias) - lse_r)
            dpt = lax.dot_general(vcat[j // 2], xdo, _NT, preferred_element_type=F32)
            dst = (pt * (dpt - dl_r)).astype(BF16)
            _swa_unstack_t(lax.dot_general(kcat[j // 2], dst, _TN, preferred_element_type=F32), j, dq_ref)
            for h in heads:
                dsink = -_rsum(jnp.exp(sink_ref[0, h] - lse_ref[h]) * dl_ref[h])
                ds_ref[h:h + 1, :] += jnp.broadcast_to(dsink, (1, LANES))

    return pl.pallas_call(
        body, name=name, grid=(nb,),
        in_specs=[smem, row, row, hrow, hrow] + _swa_specs(k.shape[1], nb) + _swa_specs(v.shape[1], nb),
        out_specs=[row, pl.BlockSpec((SWA_HEADS, LANES), lambda i: (0, 0))],
        out_shape=[_sds(q.shape, F32), _sds((SWA_HEADS, LANES), F32)],
        compiler_params=_params(("arbitrary",)),
    )(sink, q, do, lse_t, delta_t, k, k, k, v, v, v)


def _swa_bwd_kv(name, q, k, v, do, lse_t, delta_t):
    S = q.shape[0]
    nb = S // SWA_BLOCK
    grp = SWA_HEADS // SWA_KV
    krow = pl.BlockSpec((SWA_BLOCK, k.shape[1]), lambda i: (i, 0))

    def stat3():
        return [pl.BlockSpec((SWA_HEADS, 1, SWA_BLOCK), lambda i: (0, 0, jnp.maximum(i - 1, 0))),
                pl.BlockSpec((SWA_HEADS, 1, SWA_BLOCK), lambda i: (0, 0, i)),
                pl.BlockSpec((SWA_HEADS, 1, SWA_BLOCK), lambda i: (0, 0, jnp.minimum(i + 1, nb - 1)))]

    def body(qp, qc, qn, dop, doc, don, lp, lc, ln, dp_, dc_, dn_, k_ref, v_ref, dk_ref, dv_ref):
        j = pl.program_id(0)
        nh = 2 * grp
        kpos = j * SWA_BLOCK + lax.broadcasted_iota(jnp.int32, (SWA_BLOCK, SWA_BLOCK), 0)
        bias = []
        for b in range(3):
            qpos = (j - 1 + b) * SWA_BLOCK + lax.broadcasted_iota(jnp.int32, (SWA_BLOCK, SWA_BLOCK), 1)
            ok = (jnp.abs(qpos - kpos) <= SWA_WINDOW) & (qpos >= 0) & (qpos < S)
            bias.append(jnp.where(ok, 0.0, -jnp.inf))
        for u in range(SWA_KV // 2):
            heads = range(nh * u, nh * (u + 1))
            kc = k_ref[:, LANES * u:LANES * (u + 1)]
            vc = v_ref[:, LANES * u:LANES * (u + 1)]
            xq = _swa_stack([qp, qc, qn], heads)
            xdo = _swa_stack([dop, doc, don], heads)
            lse_r = jnp.concatenate([r[h] for h in heads for r in (lp, lc, ln)], axis=1)
            dl_r = jnp.concatenate([r[h] for h in heads for r in (dp_, dc_, dn_)], axis=1)
            st = lax.dot_general(kc, xq, _NT, preferred_element_type=F32)
            pt = jnp.exp(_add_blocks(st, bias) - lse_r)
            dpt = lax.dot_general(vc, xdo, _NT, preferred_element_type=F32)
            dst = (pt * (dpt - dl_r)).astype(BF16)
            dv_ref[:, LANES * u:LANES * (u + 1)] = lax.dot_general(pt.astype(BF16), xdo, _NN, preferred_element_type=F32)
            dk_ref[:, LANES * u:LANES * (u + 1)] = lax.dot_general(dst, xq, _NN, preferred_element_type=F32)

    return pl.pallas_call(
        body, name=name, grid=(nb,),
        in_specs=_swa_specs(q.shape[1], nb) + _swa_specs(do.shape[1], nb) + stat3() + stat3() + [krow, krow],
        out_specs=[krow, krow],
        out_shape=[_sds(k.shape, F32), _sds(v.shape, F32)],
        compiler_params=_params(("parallel",)),
    )(q, q, q, do, do, do, lse_t, lse_t, lse_t, delta_t, delta_t, delta_t, k, v)


def _peer_of(x, y, c, kk):
    return (x ^ ((kk >> 2) & 1), y ^ ((kk >> 1) & 1), c ^ (kk & 1))


def _own_slot(a, scatter):
    me = 4 * lax.axis_index("x") + 2 * lax.axis_index("y") + lax.axis_index("c")
    shape = a.shape if scatter else (N_DEV,) + a.shape
    own = lax.dynamic_slice_in_dim(a, me, 1, 0) if scatter else a[None]
    return lax.dynamic_update_slice_in_dim(lax.empty(shape, a.dtype), own, me, 0)


def _exchange_start(name, tensors):
    n = len(tensors)
    hbm = pl.BlockSpec(memory_space=pltpu.HBM)
    sem = pl.BlockSpec(memory_space=pltpu.SEMAPHORE)
    srcs = [pltpu.with_memory_space_constraint(a, pltpu.HBM) for a, _ in tensors]
    lands = [pltpu.with_memory_space_constraint(_own_slot(a, sc), pltpu.HBM) for a, sc in tensors]

    def body(*refs):
        ins, dst = refs[:n], refs[n:2 * n]
        send, recv, token = refs[2 * n], refs[2 * n + 1], refs[4 * n + 2]
        x, y, c = lax.axis_index("x"), lax.axis_index("y"), lax.axis_index("c")
        me = 4 * x + 2 * y + c
        for t in range(n):
            for kk in range(1, N_DEV):
                px, py, pc = _peer_of(x, y, c, kk)
                src = ins[t].at[4 * px + 2 * py + pc] if tensors[t][1] else ins[t]
                k1 = t * (N_DEV - 1) + kk - 1
                pltpu.make_async_remote_copy(src_ref=src, dst_ref=dst[t].at[me], send_sem=send.at[k1],
                                             recv_sem=recv.at[k1], device_id=(px, py, pc),
                                             device_id_type=pl.DeviceIdType.MESH).start()
        token[...] = jnp.zeros_like(token)

    out = pl.pallas_call(
        body, name=name,
        in_specs=[hbm] * (2 * n),
        out_specs=[sem, sem] + [hbm] * (2 * n) + [pl.BlockSpec(memory_space=pltpu.VMEM)],
        out_shape=[pltpu.SemaphoreType.DMA((n * (N_DEV - 1),)), pltpu.SemaphoreType.DMA((n * (N_DEV - 1),))]
        + [pltpu.HBM(a.shape, a.dtype) for a in srcs] + [pltpu.HBM(a.shape, a.dtype) for a in lands]
        + [_sds((8, LANES), F32)],
        input_output_aliases={i: i + 2 for i in range(2 * n)},
        compiler_params=pltpu.CompilerParams(has_side_effects=pltpu.SideEffectType.DATAFLOW_SIDE_EFFECTING),
    )(*srcs, *lands)
    return (out[0], out[1], out[2:2 + n], out[2 + n:2 + 2 * n], [sc for _, sc in tensors]), out[2 + 2 * n]


def _exchange_wait(name, started, after):
    send_s, recv_s, srcs, lands, flags = started
    n = len(srcs)
    hbm = pl.BlockSpec(memory_space=pltpu.HBM)
    sem = pl.BlockSpec(memory_space=pltpu.SEMAPHORE)

    def body(*refs):
        ins, dst, send, recv = refs[:n], refs[n:2 * n], refs[2 * n], refs[2 * n + 1]
        x, y, c = lax.axis_index("x"), lax.axis_index("y"), lax.axis_index("c")
        me = 4 * x + 2 * y + c
        for t in range(n):
            for kk in range(1, N_DEV):
                px, py, pc = _peer_of(x, y, c, kk)
                src = ins[t].at[me] if flags[t] else ins[t]
                k1 = t * (N_DEV - 1) + kk - 1
                cp = pltpu.make_async_remote_copy(src_ref=src, dst_ref=dst[t].at[4 * px + 2 * py + pc],
                                                  send_sem=send.at[k1], recv_sem=recv.at[k1],
                                                  device_id=(px, py, pc), device_id_type=pl.DeviceIdType.MESH)
                cp.wait_send()
                cp.wait_recv()

    out = pl.pallas_call(
        body, name=name,
        in_specs=[hbm] * (2 * n) + [sem, sem, pl.BlockSpec(memory_space=pl.ANY)],
        out_specs=[hbm] * (2 * n),
        out_shape=[pltpu.HBM(a.shape, a.dtype) for a in srcs] + [pltpu.HBM(a.shape, a.dtype) for a in lands],
        input_output_aliases={i: i for i in range(2 * n)},
        compiler_params=pltpu.CompilerParams(has_side_effects=pltpu.SideEffectType.DATAFLOW_SIDE_EFFECTING),
    )(*srcs, *lands, send_s, recv_s, after)
    return out[n:]


def _sum8(name, parts):
    _, R, C = parts.shape

    def fn(r, c, o):
        acc = r[0][0].astype(F32)
        for s in range(1, N_DEV):
            acc = acc + r[0][s].astype(F32)
        o[0][...] = acc
        return ()

    tm = R if R % 256 else 256
    return _rowwise(name, fn, [parts], [], [_sds((R, C), F32)], tm=tm)[0]


def _adamw(name, w, g, m, v):
    bc1 = 1.0 - ADAM_B1 ** ADAM_STEP
    bc2 = 1.0 - ADAM_B2 ** ADAM_STEP

    def fn(r, c, o):
        wv, gv, mv, vv = (x[...] for x in r)
        mn = ADAM_B1 * mv + (1.0 - ADAM_B1) * gv
        vn = ADAM_B2 * vv + (1.0 - ADAM_B2) * (gv * gv)
        o[0][...] = -ADAM_LR * ((mn / bc1) / (jnp.sqrt(vn / bc2) + ADAM_EPS) + ADAM_WD * wv)
        o[1][...] = mn
        o[2][...] = vn
        return ()

    tm = max(8, min(512, 1 << ((16 << 20) // (56 * w.shape[1])).bit_length() - 1))
    return _rowwise(name, fn, [w, g, m, v], [], [_sds(w.shape, F32)] * 3, tm=tm)


def _rope_cs(pos, dim, theta):
    inv = jnp.float32(theta) ** (-jnp.arange(0, dim, 2, dtype=jnp.float32) / dim)
    ang = pos.astype(jnp.float32)[:, None] * inv[None, :]
    return jnp.cos(ang), jnp.sin(ang)


def _tables(S):
    pos = jnp.arange(S)
    c, s = _rope_cs(pos, MLA_ROPE, ROPE_THETA)
    mla = (jnp.concatenate([c, c, c, c], 1), jnp.concatenate([-s, s, -s, s], 1))
    rc, rs = _rope_cs(pos // GRID_W, HD // 2, AXIAL_THETA)
    cc, cs = _rope_cs(pos % GRID_W, HD // 2, AXIAL_THETA)
    axial = (jnp.concatenate([rc, rc, cc, cc], 1), jnp.concatenate([-rs, rs, -cs, cs], 1))
    c, s = _rope_cs(pos, SWA_ROT, ROPE_THETA)
    one, zero = jnp.ones((S, SWA_D - SWA_ROT), F32), jnp.zeros((S, SWA_D - SWA_ROT), F32)
    swa = (jnp.concatenate([c, c, one, c, c, one], 1), jnp.concatenate([-s, s, zero, -s, s, zero], 1))
    return mla, axial, swa


_WEIGHTS = ['even_norm', 'even_w_in', 'mla_q_lat_norm', 'mla_kv_lat_norm', 'mla_w_uq', 'mla_w_ukv', 'mla_q_norm',
            'mla_k_nope_norm', 'mla_k_rope_norm', 'gqa_q_norm', 'gqa_k_norm', 'even_w_out', 'odd_norm', 'odd_w_qkv',
            'swa_q_norm', 'swa_k_norm', 'swa_sink', 'odd_w_out', 'mlp_norm', 'mlp_w_up', 'mlp_w_down']
_SMALL = ['even_norm', 'mla_q_lat_norm', 'mla_kv_lat_norm', 'mla_q_norm', 'mla_k_nope_norm', 'mla_k_rope_norm',
          'gqa_q_norm', 'gqa_k_norm', 'swa_q_norm', 'swa_k_norm', 'swa_sink', 'mlp_norm']


def _relu2(acc):
    rl = jnp.maximum(acc, 0.0)
    return rl * rl, rl


def _mul2(acc, rl):
    return (acc * (2.0 * rl.astype(F32)),)


def _add(acc, res):
    return (acc + res,)


def _step(x, tgt, w, m, v):
    S, D = x.shape
    nw = len(_WEIGHTS)
    tab_mla, tab_ax, tab_swa = _tables(S)
    bf = lambda a: a.astype(BF16)

    w_up_s, w_dn_s = w['mlp_w_up'], w['mlp_w_down']
    first, tok0 = _exchange_start("gather_first_start", [
        (bf(w['even_w_in'][0].T), False), (bf(w['mla_w_uq'][0].T), False), (bf(w['mla_w_ukv'][0].T), False)])
    h0 = _rmsnorm("even_norm", x, w['even_norm'], after=[tok0])
    gathered = _exchange_wait("gather_first_wait", first, h0)
    zero = jnp.minimum(jnp.abs(gathered[2][0, 0:1, 0:1].astype(F32)), 0.0)
    later, tok = _exchange_start("gather_rest_start", [
        (bf(w['even_w_out'][0]), False), (bf(w['odd_w_qkv'][0].T), False), (bf(w['odd_w_out'][0]), False),
        (bf(w_up_s[0].T), False), (bf(w_up_s[1].T), False), (bf(w_dn_s[0]), False), (bf(w_dn_s[1]), False),
        (w['odd_norm'] + zero, False)])
    flat = lambda a: a.reshape((a.shape[0] * a.shape[1],) + a.shape[2:])
    win_t, wuq_g, wukv_t = [flat(a) for a in gathered]
    win_t = jnp.concatenate([win_t[:KR_END], jnp.zeros((LANES - MLA_ROPE, D), BF16), win_t[KR_END:]], 0)
    wuq_g = gathered[1]
    wuq_t = jnp.concatenate([wuq_g[:, :HD].reshape(HEADS * HD, Q_LORA),
                             wuq_g[:, HD:].reshape(HEADS * MLA_ROPE, Q_LORA)], 0)

    z64 = jnp.zeros((1, 64), F32)
    qn = w['mla_q_norm']
    g_even = [w['mla_q_lat_norm'], w['mla_kv_lat_norm'], jnp.concatenate([w['mla_k_rope_norm'], z64], 1),
              w['gqa_q_norm'], w['gqa_k_norm']]
    g_mla = [qn[:, :HD], jnp.concatenate([qn[:, HD:], qn[:, HD:]], 1), w['mla_k_nope_norm']]
    g_swa = [jnp.concatenate([w['swa_q_norm']] * 2, 1), jnp.concatenate([w['swa_k_norm']] * 2, 1)]

    def loss_head(acc, res, t):
        e = acc + res - t
        return e * (1.0 / D), _csum(_rsum(e * e))

    def mlp_fwd(l, xin, last=False):
        hn = _rmsnorm(f"mlp{l}_norm", xin, w['mlp_norm'][l:l + 1])
        a, rl = _mm(f"mlp{l}_up", hn, wup_t[l], "nt", (BF16, BF16), epilogue=_relu2)
        if last:
            xout = _mm(f"mlp{l}_down", a, wdn[l], "nn", (F32,), epilogue=loss_head, extras=(xin, tgt),
                       sums=[_sds((1, 1), F32)])
        else:
            xout = _mm(f"mlp{l}_down", a, wdn[l], "nn", (F32,), epilogue=_add, extras=(xin,))
        return xout, (hn, a, rl)

    proj = _mm("even_in", h0, win_t, "nt", (F32,), tm=512, tn=P_END)
    cqn, ckvn, krd, qg, kg, vg = _even_prep("even_prep", proj, tab_mla + tab_ax, g_even, after=[tok])
    qa = _mm("mla_uq", cqn, wuq_t, "nt", (F32,), tn=wuq_t.shape[0])
    kv = _mm("mla_ukv", ckvn, wukv_t, "nt", (F32,))
    q_a, k_a, v_a = _mla_prep("mla_prep", qa, kv, krd, tab_mla, g_mla)
    o_a, lse_a = _flash_fwd("mla_attn", q_a, k_a, v_a)
    o_g, lse_g = _flash_fwd("gqa_attn", qg, kg, vg)
    merged = jnp.concatenate([o_a, o_g], 1)
    rest = _exchange_wait("gather_rest_wait", later, merged)
    wout_e, wqkv_t, wout_o, wup0_t, wup1_t, wdn0, wdn1 = [flat(a) for a in rest[:7]]
    odd_norm = rest[7].reshape(1, D)
    wup_t, wdn = (wup0_t, wup1_t), (wdn0, wdn1)
    x1 = _mm("even_out", merged, wout_e, "nn", (F32,), epilogue=_add, extras=(x,))
    x2, mlp0 = mlp_fwd(0, x1)

    h1 = _rmsnorm("odd_norm", x2, odd_norm)
    qkv = _mm("odd_qkv", h1, wqkv_t, "nt", (F32,), tn=wqkv_t.shape[0] // 2)
    q_s, k_s, v_s = _swa_prep("swa_prep", qkv, tab_swa, g_swa)
    o_s, lse_s = _swa_fwd("swa_attn", q_s, k_s, v_s, w['swa_sink'])
    x3 = _mm("odd_out", o_s, wout_o, "nn", (F32,), epilogue=_add, extras=(x2,))
    (dy, loss_acc), mlp1 = mlp_fwd(1, x3, last=True)
    loss = lax.psum(0.5 / D * loss_acc[0, 0], ("x", "y", "c"))

    gsm = {}

    def mlp_bwd(l, dout, xin, saved):
        hn, a, rl = saved
        du = _mm(f"mlp{l}_dact", dout, wdn[l], "nt", (BF16,), epilogue=_mul2, extras=(rl,))
        g_dn = _mm(f"mlp{l}_gdown", a, dout, "tn", (BF16,))
        g_up = _mm(f"mlp{l}_gup", du, hn, "tn", (BF16,))
        dhn = _mm(f"mlp{l}_dnorm", du, wup_t[l], "nn", (F32,))
        din, g_n = _rmsnorm_bwd(f"mlp{l}_norm_bwd", dout, xin, dhn, w['mlp_norm'][l:l + 1])
        return din, g_dn, g_up, g_n

    dx3, g_dn1, g_up1, g_mn1 = mlp_bwd(1, dy, x3, mlp1)
    split = lambda a: a.reshape((N_DEV, a.shape[0] // N_DEV) + a.shape[1:])
    sc1, tok1 = _exchange_start("scatter_mlp1_start", [(split(g_up1), True), (split(g_dn1), True)])

    g_wout_o = _mm("odd_gout", o_s, dx3, "tn", (BF16,))
    do_s = _mm("odd_dattn", dx3, wout_o, "nt", (BF16,))
    dl_s = _delta_t("swa_delta", do_s, o_s, after=[tok1])
    dq_s, dsink = _swa_bwd_q("swa_bwd_q", q_s, k_s, v_s, do_s, lse_s, dl_s, w['swa_sink'])
    dk_s, dv_s = _swa_bwd_kv("swa_bwd_kv", q_s, k_s, v_s, do_s, lse_s, dl_s)
    dqkv, g_sq, g_sk = _swa_prep_bwd("swa_prep_bwd", dq_s, dk_s, dv_s, qkv, tab_swa, g_swa)
    g_wqkv = _mm("odd_gqkv", dqkv, h1, "tn", (BF16,), tm=dqkv.shape[1] // 2)
    dh1 = _mm("odd_dnorm", dqkv, wqkv_t, "nn", (F32,), tk=dqkv.shape[1])
    dx2, g_on = _rmsnorm_bwd("odd_norm_bwd", dx3, x2, dh1, odd_norm)
    gsm['swa_q_norm'] = g_sq[:, :64] + g_sq[:, 64:]
    gsm['swa_k_norm'] = g_sk[:, :64] + g_sk[:, 64:]
    gsm['swa_sink'] = dsink[:, 0].reshape(1, SWA_HEADS)

    dx1, g_dn0, g_up0, g_mn0 = mlp_bwd(0, dx2, x1, mlp0)
    sc2, tok2 = _exchange_start("scatter_mid_start", [(split(g_wout_o), True), (split(g_wqkv), True),
                                                      (split(g_up0), True), (split(g_dn0), True)])
    gsm['mlp_norm'] = jnp.concatenate([g_mn0, g_mn1], 0)

    g_wout_e = _mm("even_gout", merged, dx1, "tn", (BF16,))
    dmerged = _mm("even_dattn", dx1, wout_e, "nt", (BF16,))
    dl_e = _delta("even_delta", dmerged, merged, after=[tok2])
    dq_a, dk_a, dv_a = _flash_bwd("mla_attn_bwd", q_a, k_a, v_a, dmerged, 0, lse_a, dl_e)
    dq_g, dk_g, dv_g = _flash_bwd("gqa_attn_bwd", qg, kg, vg, dmerged, HEADS, lse_g, dl_e)
    dqa, dkv, dkrd, g_qnn, g_qnr, g_kn = _mla_prep_bwd("mla_prep_bwd", dq_a, dk_a, dv_a, qa, kv, tab_mla, g_mla)
    g_wuq = _mm("mla_guq", dqa, cqn, "tn", (BF16,))
    dcqn = _mm("mla_dq_lat", dqa, wuq_t, "nn", (F32,))
    g_wukv = _mm("mla_gukv", dkv, ckvn, "tn", (BF16,))
    dckvn = _mm("mla_dkv_lat", dkv, wukv_t, "nn", (F32,))
    g_wuq = jnp.concatenate([g_wuq[:HEADS * HD].reshape(HEADS, HD, Q_LORA),
                             g_wuq[HEADS * HD:].reshape(HEADS, MLA_ROPE, Q_LORA)], 1)
    sc3, tok_e = _exchange_start("scatter_even_start", [(g_wuq, True), (split(g_wukv), True), (split(g_wout_e), True)])
    dproj, g_ql, g_kvl, g_kr, g_gq, g_gk = _even_prep_bwd("even_prep_bwd", dcqn, dckvn, dkrd, dq_g, dk_g, dv_g,
                                                          proj, tab_mla + tab_ax, g_even, after=[tok_e])
    g_win = _mm("even_gin", dproj, h0, "tn", (BF16,), tm=P_END, tk=1024)
    dh0 = _mm("even_dnorm", dproj, win_t, "nn", (F32,), tk=P_END)
    grad_x, g_en = _rmsnorm_bwd("even_norm_bwd", dx1, x, dh0, w['even_norm'])
    gsm.update(even_norm=g_en, mla_q_lat_norm=g_ql, mla_kv_lat_norm=g_kvl,
               mla_q_norm=jnp.concatenate([g_qnn, g_qnr[:, :64] + g_qnr[:, 64:]], 1), mla_k_nope_norm=g_kn,
               mla_k_rope_norm=g_kr[:, :64], gqa_q_norm=g_gq, gqa_k_norm=g_gk)

    g_win = jnp.concatenate([g_win[:KR_END], g_win[P_QG:]], 0)
    small_sizes = [w[n].size for n in _SMALL] + [D]
    small_vec = jnp.concatenate([gsm[n].reshape(1, -1) for n in _SMALL] + [g_on], 1)
    pad = (-small_vec.shape[1]) % LANES
    small_vec = jnp.pad(small_vec, ((0, 0), (0, pad)))
    last_st, tok3 = _exchange_start("scatter_last_start", [(split(g_win), True), (small_vec, False)])
    p_up1, p_dn1 = _exchange_wait("scatter_mlp1_wait", sc1, tok3)
    p_wout_o, p_wqkv, p_up0, p_dn0 = _exchange_wait("scatter_mid_wait", sc2, tok3)
    p_wuq, p_wukv, p_wout_e = _exchange_wait("scatter_even_wait", sc3, tok3)
    parts = dict(odd_w_qkv=p_wqkv, odd_w_out=p_wout_o, up0=p_up0, up1=p_up1, dn0=p_dn0, dn1=p_dn1,
                 mla_w_uq=p_wuq, mla_w_ukv=p_wukv, even_w_out=p_wout_e)
    red = {n: _sum8("sum_" + n, p) for n, p in parts.items()}
    grads = {
        'odd_w_qkv': red['odd_w_qkv'].T[None], 'odd_w_out': red['odd_w_out'][None],
        'mlp_w_up': jnp.stack([red['up0'].T, red['up1'].T]), 'mlp_w_down': jnp.stack([red['dn0'], red['dn1']]),
        'mla_w_uq': red['mla_w_uq'].T[None], 'mla_w_ukv': red['mla_w_ukv'].T[None],
        'even_w_out': red['even_w_out'][None],
    }
    delta, new_m, new_v = {}, {}, {}

    def apply(names):
        for n in names:
            shp = w[n].shape
            two = lambda a: a.reshape(shp[0] * shp[1], shp[2])
            d_, m_, v_ = _adamw("adamw_" + n, two(w[n]), two(grads[n]), two(m[n]), two(v[n]))
            delta[n], new_m[n], new_v[n] = d_.reshape(shp), m_.reshape(shp), v_.reshape(shp)

    apply(['mla_w_uq', 'mla_w_ukv', 'even_w_out', 'odd_w_qkv', 'odd_w_out', 'mlp_w_up', 'mlp_w_down'])
    last = _exchange_wait("scatter_last_wait", last_st, new_v['mlp_w_down'])
    small_g = _sum8("sum_small", last[1])
    grads['even_w_in'] = _sum8("sum_even_w_in", last[0]).T[None]
    apply(['even_w_in'])
    off = 0
    for n, sz in zip(_SMALL + ['odd_norm_full'], small_sizes):
        seg = small_g[:, off:off + sz]
        off += sz
        if n == 'odd_norm_full':
            me = 4 * lax.axis_index("x") + 2 * lax.axis_index("y") + lax.axis_index("c")
            grads['odd_norm'] = lax.dynamic_slice(seg, (0, me * (D // N_DEV)), (1, D // N_DEV))
        else:
            grads[n] = seg.reshape(w[n].shape)

    sm_names = _SMALL + ['odd_norm']
    pack = lambda d: jnp.concatenate([d[n].reshape(1, -1) for n in sm_names], 1)
    pw, pg, pm, pv = pack(w), pack(grads), pack(m), pack(v)
    padw = (-pw.shape[1]) % LANES
    padf = lambda a: jnp.pad(a, ((0, 0), (0, padw)))
    d_, m_, v_ = _adamw("adamw_small", padf(pw), padf(pg), padf(pm), jnp.pad(pv, ((0, 0), (0, padw)), constant_values=1.0))
    off = 0
    for n in sm_names:
        sz = w[n].size
        delta[n] = d_[:, off:off + sz].reshape(w[n].shape)
        new_m[n] = m_[:, off:off + sz].reshape(w[n].shape)
        new_v[n] = v_[:, off:off + sz].reshape(w[n].shape)
        off += sz

    return (loss, grad_x[None], *[grads[n] for n in _WEIGHTS], *[delta[n] for n in _WEIGHTS],
            *[new_m[n] for n in _WEIGHTS], *[new_v[n] for n in _WEIGHTS])


def kernel(x, even_norm, even_w_in, mla_q_lat_norm, mla_kv_lat_norm, mla_w_uq, mla_w_ukv, mla_q_norm, mla_k_nope_norm, mla_k_rope_norm, gqa_q_norm, gqa_k_norm, even_w_out, odd_norm, odd_w_qkv, swa_q_norm, swa_k_norm, swa_sink, odd_w_out, mlp_norm, mlp_w_up, mlp_w_down, loss_target, m_even_norm, m_even_w_in, m_mla_q_lat_norm, m_mla_kv_lat_norm, m_mla_w_uq, m_mla_w_ukv, m_mla_q_norm, m_mla_k_nope_norm, m_mla_k_rope_norm, m_gqa_q_norm, m_gqa_k_norm, m_even_w_out, m_odd_norm, m_odd_w_qkv, m_swa_q_norm, m_swa_k_norm, m_swa_sink, m_odd_w_out, m_mlp_norm, m_mlp_w_up, m_mlp_w_down, v_even_norm, v_even_w_in, v_mla_q_lat_norm, v_mla_kv_lat_norm, v_mla_w_uq, v_mla_w_ukv, v_mla_q_norm, v_mla_k_nope_norm, v_mla_k_rope_norm, v_gqa_q_norm, v_gqa_k_norm, v_even_w_out, v_odd_norm, v_odd_w_qkv, v_swa_q_norm, v_swa_k_norm, v_swa_sink, v_odd_w_out, v_mlp_norm, v_mlp_w_up, v_mlp_w_down):
    ws = (even_norm, even_w_in, mla_q_lat_norm, mla_kv_lat_norm, mla_w_uq, mla_w_ukv, mla_q_norm, mla_k_nope_norm, mla_k_rope_norm, gqa_q_norm, gqa_k_norm, even_w_out, odd_norm, odd_w_qkv, swa_q_norm, swa_k_norm, swa_sink, odd_w_out, mlp_norm, mlp_w_up, mlp_w_down)
    ms = (m_even_norm, m_even_w_in, m_mla_q_lat_norm, m_mla_kv_lat_norm, m_mla_w_uq, m_mla_w_ukv, m_mla_q_norm, m_mla_k_nope_norm, m_mla_k_rope_norm, m_gqa_q_norm, m_gqa_k_norm, m_even_w_out, m_odd_norm, m_odd_w_qkv, m_swa_q_norm, m_swa_k_norm, m_swa_sink, m_odd_w_out, m_mlp_norm, m_mlp_w_up, m_mlp_w_down)
    vs = (v_even_norm, v_even_w_in, v_mla_q_lat_norm, v_mla_kv_lat_norm, v_mla_w_uq, v_mla_w_ukv, v_mla_q_norm, v_mla_k_nope_norm, v_mla_k_rope_norm, v_gqa_q_norm, v_gqa_k_norm, v_even_w_out, v_odd_norm, v_odd_w_qkv, v_swa_q_norm, v_swa_k_norm, v_swa_sink, v_odd_w_out, v_mlp_norm, v_mlp_w_up, v_mlp_w_down)
    return _step(x[0], loss_target[0], dict(zip(_WEIGHTS, ws)), dict(zip(_WEIGHTS, ms)), dict(zip(_WEIGHTS, vs)))
```

```python
import jax
import jax.numpy as jnp
from jax import lax
from jax.experimental import pallas as pl
from jax.experimental.pallas import tpu as pltpu

F32 = jnp.float32
BF16 = jnp.bfloat16

N_DEV = 8
NORM_EPS = 1e-6
ROPE_THETA = 500000.0
AXIAL_THETA = 10000.0
GRID_W = 64
HEADS = 8
GQA_KV = 2
HD = 128
MLA_ROPE = 64
MLA_QK = HD + MLA_ROPE
Q_LORA = 512
KV_LORA = 256
SWA_HEADS = 32
SWA_KV = 4
SWA_D = 64
SWA_ROT = 16
SWA_WINDOW = 128
SWA_BLOCK = 128
MLA_SCALE, GQA_SCALE, SWA_SCALE = MLA_QK ** -0.5, HD ** -0.5, SWA_D ** -0.5
LANES = 128
ADAM_LR, ADAM_B1, ADAM_B2, ADAM_EPS, ADAM_WD, ADAM_STEP = 0.001, 0.9, 0.999, 1e-08, 0.01, 10
VMEM_LIMIT = 56 * 1024 * 1024
MM_TILE = (1024, 1024, 2048)
FWD_TILE = (1024, 8192)
BWD_TILE = (2048, 1024)
ROW_TILE = 256

P_CQ, P_CKV, P_KR, P_QG = 0, Q_LORA, Q_LORA + KV_LORA, Q_LORA + KV_LORA + LANES
P_KG = P_QG + HEADS * HD
P_VG = P_KG + GQA_KV * HD
P_END = P_VG + GQA_KV * HD
KR_END = Q_LORA + KV_LORA + MLA_ROPE


def _tile(n, prefs):
    for t in prefs:
        if n % t == 0 and t <= n:
            return t
    return n


def _params(sem):
    return pltpu.CompilerParams(dimension_semantics=sem, vmem_limit_bytes=VMEM_LIMIT)


_DIMS = {"nn": ((1,), (0,)), "nt": ((1,), (1,)), "tn": ((0,), (0,))}


def _mm(name, a, b, mode, out_dtypes, epilogue=None, extras=(), sums=(), tm=MM_TILE[0], tn=MM_TILE[1], tk=MM_TILE[2]):
    if mode == "nn":
        (M, K), (_, N) = a.shape, b.shape
    elif mode == "nt":
        (M, K), (N, _) = a.shape, b.shape
    else:
        (K, M), (_, N) = a.shape, b.shape
    tm = _tile(M, (tm, 512, 256, 128))
    tn = _tile(N, (tn, 512, 256, 128))
    tk = _tile(K, (tk, 1024, 512, 256, 128))
    nk = K // tk
    ne, no, ns = len(extras), len(out_dtypes), len(sums)
    if mode == "tn":
        a_spec = pl.BlockSpec((tk, tm), lambda i, j, k: (k, i))
    else:
        a_spec = pl.BlockSpec((tm, tk), lambda i, j, k: (i, k))
    if mode == "nt":
        b_spec = pl.BlockSpec((tn, tk), lambda i, j, k: (j, k))
    else:
        b_spec = pl.BlockSpec((tk, tn), lambda i, j, k: (k, j))
    o_spec = pl.BlockSpec((tm, tn), lambda i, j, k: (i, j))
    dims = (_DIMS[mode], ((), ()))

    def body(a_ref, b_ref, *rest):
        ex, outs = rest[:ne], rest[ne:ne + no]
        k = pl.program_id(2)
        part = lax.dot_general(a_ref[...].astype(BF16), b_ref[...].astype(BF16), dims, preferred_element_type=F32)

        def finish(total):
            res = epilogue(total, *[e[...] for e in ex]) if epilogue else (total,)
            for o, r in zip(outs, res[:no]):
                o[...] = r.astype(o.dtype)
            if ns:
                first = (pl.program_id(0) == 0) & (pl.program_id(1) == 0)

                @pl.when(first)
                def _():
                    for s_ref, r in zip(rest[ne + no:ne + no + ns], res[no:]):
                        s_ref[...] = r

                @pl.when(jnp.logical_not(first))
                def _():
                    for s_ref, r in zip(rest[ne + no:ne + no + ns], res[no:]):
                        s_ref[...] += r

        if nk == 1:
            finish(part)
            return
        acc = rest[ne + no + ns]

        @pl.when(k == 0)
        def _():
            acc[...] = part

        @pl.when((k > 0) & (k < nk - 1))
        def _():
            acc[...] += part

        @pl.when(k == nk - 1)
        def _():
            finish(acc[...] + part)

    out = pl.pallas_call(
        body, name=name, grid=(M // tm, N // tn, nk),
        in_specs=[a_spec, b_spec] + [o_spec] * ne,
        out_specs=[o_spec] * no + [pl.BlockSpec(tuple(s.shape), lambda i, j, k: (0,) * len(s.shape)) for s in sums],
        out_shape=[jax.ShapeDtypeStruct((M, N), d) for d in out_dtypes] + list(sums),
        scratch_shapes=[pltpu.VMEM((tm, tn), F32)] if nk > 1 else [],
        compiler_params=_params(("arbitrary",) * 3 if ns else ("parallel", "parallel", "arbitrary")),
    )(a, b, *extras)
    return out[0] if no + ns == 1 else out


def _rowwise(name, fn, rows, consts, outs, accs=(), tm=ROW_TILE, after=()):
    S = rows[0].shape[-2]
    tm = _tile(S, (tm, 128, 64, 32, 16, 8))
    nr, nc, nd, no, na = len(rows), len(consts), len(after), len(outs), len(accs)

    def rspec(shape):
        if len(shape) == 2:
            return pl.BlockSpec((tm, shape[1]), lambda i: (i, 0))
        return pl.BlockSpec((shape[0], tm, shape[2]), lambda i: (0, i, 0))

    def cspec(shape):
        return pl.BlockSpec(tuple(shape), lambda i: (0,) * len(shape))

    def ospec(o):
        if isinstance(o, tuple):
            return pl.BlockSpec((o[1].shape[0], o[1].shape[1], tm), lambda i: (0, 0, i))
        return rspec(o.shape)

    out_sds = [o[1] if isinstance(o, tuple) else o for o in outs]

    def body(*refs):
        r, c = refs[:nr], refs[nr:nr + nc]
        o, a = refs[nr + nc + nd:nr + nc + nd + no], refs[nr + nc + nd + no:]
        vals = fn(r, c, o)
        if na:
            @pl.when(pl.program_id(0) == 0)
            def _():
                for ar in a:
                    ar[...] = jnp.zeros_like(ar)

            for ar, v in zip(a, vals):
                ar[...] += v

    res = pl.pallas_call(
        body, name=name, grid=(S // tm,),
        in_specs=[rspec(x.shape) for x in rows] + [cspec(x.shape) for x in consts]
        + [pl.BlockSpec(memory_space=pl.ANY)] * len(after),
        out_specs=[ospec(x) for x in outs] + [cspec(x.shape) for x in accs],
        out_shape=out_sds + list(accs),
        compiler_params=_params(("arbitrary",) if na else ("parallel",)),
    )(*rows, *consts, *after)
    return res


def _sds(shape, dtype):
    return jax.ShapeDtypeStruct(tuple(shape), dtype)


def _rsum(x):
    return jnp.sum(x, axis=-1, keepdims=True)


def _csum(x):
    return jnp.sum(x, axis=0, keepdims=True)


def _lane(shape):
    return lax.broadcasted_iota(jnp.int32, shape, 1)


def _partner(x, seg, half):
    lane = _lane(x.shape) % seg
    return jnp.where(lane < half, pltpu.roll(x, LANES - half, 1), pltpu.roll(x, half, 1))


def _rope(x, c, s, seg, half):
    return x * c + _partner(x, seg, half) * s


def _rope_bwd(dy, c, s, seg, half):
    t = _partner(dy * s, seg, half)
    if seg != 2 * half:
        t = jnp.where(_lane(dy.shape) % seg < 2 * half, t, 0.0)
    return dy * c + t


def _rms_r(ss, n):
    return lax.rsqrt(ss * (1.0 / n) + NORM_EPS)


def _rms_bwd(x, r, g, dy, dot_scale):
    dyg = dy * g
    return r * dyg - x * (r * r * r) * (_rsum(dyg * x) * dot_scale)


def _rmsnorm(name, x, g, after=()):
    D = x.shape[1]

    def fn(r, c, o):
        xv = r[0][...]
        o[0][...] = (xv * _rms_r(_rsum(xv * xv), D) * c[0][...]).astype(BF16)
        return ()

    return _rowwise(name, fn, [x], [g], [_sds(x.shape, BF16)], tm=4 * ROW_TILE, after=after)[0]


def _rmsnorm_bwd(name, dres, x, dh, g):
    D = x.shape[1]

    def fn(r, c, o):
        xv, dhv, gv = r[1][...], r[2][...], c[0][...]
        rr = _rms_r(_rsum(xv * xv), D)
        o[0][...] = r[0][...] + _rms_bwd(xv, rr, gv, dhv, 1.0 / D)
        return (_csum(dhv * xv * rr),)

    return _rowwise(name, fn, [dres, x, dh], [g], [_sds(x.shape, F32)], [_sds((1, D), F32)], tm=2 * ROW_TILE)


def _even_prep(name, proj, tabs, gains, after=()):
    S = proj.shape[0]
    cm, sm, ca, sa = tabs

    def fn(r, c, o):
        p, cmv, smv, cav, sav = r[0], r[1][...], r[2][...], r[3][...], r[4][...]
        g_ql, g_kvl, g_kr, g_q, g_k = (x[...] for x in c)
        ones = _seg_ones(LANES)
        cq = p[:, P_CQ:P_CKV]
        o[0][...] = (cq * _wide(_rms_r(_seg_sum(_fold(cq * cq), ones), Q_LORA), Q_LORA // LANES) * g_ql).astype(BF16)
        ckv = p[:, P_CKV:P_KR]
        o[1][...] = (ckv * _wide(_rms_r(_seg_sum(_fold(ckv * ckv), ones), KV_LORA), KV_LORA // LANES) * g_kvl).astype(BF16)
        kr = p[:, P_KR:P_QG]
        y = _rope(kr * _rms_r(_seg_sum(kr * kr, ones), MLA_ROPE) * g_kr, cmv, smv, 64, 32)
        o[2][...] = (y + pltpu.roll(y, 64, 1)).astype(BF16)
        for h in range(HEADS):
            xh = p[:, P_QG + HD * h:P_QG + HD * (h + 1)]
            o[3][h] = (_rope(xh * _rms_r(_seg_sum(xh * xh, ones), HD) * g_q, cav, sav, 64, 32) * GQA_SCALE).astype(BF16)
        for h in range(GQA_KV):
            xh = p[:, P_KG + HD * h:P_KG + HD * (h + 1)]
            o[4][h] = _rope(xh * _rms_r(_seg_sum(xh * xh, ones), HD) * g_k, cav, sav, 64, 32).astype(BF16)
            o[5][h] = p[:, P_VG + HD * h:P_VG + HD * (h + 1)].astype(BF16)
        return ()

    outs = [_sds((S, Q_LORA), BF16), _sds((S, KV_LORA), BF16), _sds((S, LANES), BF16),
            _sds((HEADS, S, HD), BF16), _sds((GQA_KV, S, HD), BF16), _sds((GQA_KV, S, HD), BF16)]
    return _rowwise(name, fn, [proj, cm, sm, ca, sa], gains, outs, tm=2 * ROW_TILE, after=after)


def _even_prep_bwd(name, dcqn, dckvn, dkrd, dqg, dkg, dvg, proj, tabs, gains, after=()):
    S = proj.shape[0]
    cm, sm, ca, sa = tabs

    def fn(r, c, o):
        p, cmv, smv, cav, sav = r[6], r[7][...], r[8][...], r[9][...], r[10][...]
        g_ql, g_kvl, g_kr, g_q, g_k = (x[...] for x in c)
        out = o[0]
        ones = _seg_ones(LANES)

        def wide_bwd(x, g, d, n):
            w_ = n // LANES
            rr = _wide(_rms_r(_seg_sum(_fold(x * x), ones), n), w_)
            dot = _wide(_seg_sum(_fold(d * g * x), ones) * (1.0 / n), w_)
            return _rms_bwd_b(x, rr, g, d, dot).astype(BF16), _csum(d * x * rr)

        cq = p[:, P_CQ:P_CKV]
        out[:, P_CQ:P_CKV], a_ql = wide_bwd(cq, g_ql, r[0][...], Q_LORA)
        ckv = p[:, P_CKV:P_KR]
        out[:, P_CKV:P_KR], a_kvl = wide_bwd(ckv, g_kvl, r[1][...], KV_LORA)
        kr = p[:, P_KR:P_QG]
        rr = _rms_r(_seg_sum(kr * kr, ones), MLA_ROPE)
        d = r[2][...]
        d = d + pltpu.roll(d, 64, 1)
        d = _rope_bwd(d, cmv, smv, 64, 32)
        low = _lane(d.shape) < 64
        dot = _seg_sum(jnp.where(low, d * g_kr * kr, 0.0), ones) * (1.0 / MLA_ROPE)
        out[:, P_KR:P_QG] = jnp.where(low, _rms_bwd_b(kr, rr, g_kr, d, dot), 0.0).astype(BF16)
        a_kr = _csum(d * kr * rr)
        a_q = jnp.zeros((1, HD), F32)

        def head_bwd(xh, g, d):
            rr = _rms_r(_seg_sum(xh * xh, ones), HD)
            dot = _seg_sum(d * g * xh, ones) * (1.0 / HD)
            return _rms_bwd_b(xh, rr, g, d, dot).astype(BF16), _csum(d * xh * rr)

        for h in range(HEADS):
            xh = p[:, P_QG + HD * h:P_QG + HD * (h + 1)]
            d = _rope_bwd(r[3][h] * GQA_SCALE, cav, sav, 64, 32)
            out[:, P_QG + HD * h:P_QG + HD * (h + 1)], inc = head_bwd(xh, g_q, d)
            a_q = a_q + inc
        a_k = jnp.zeros((1, HD), F32)
        grp = HEADS // GQA_KV
        for h in range(GQA_KV):
            xh = p[:, P_KG + HD * h:P_KG + HD * (h + 1)]
            d = r[4][grp * h]
            dv = r[5][grp * h]
            for e in range(1, grp):
                d = d + r[4][grp * h + e]
                dv = dv + r[5][grp * h + e]
            d = _rope_bwd(d, cav, sav, 64, 32)
            out[:, P_KG + HD * h:P_KG + HD * (h + 1)], inc = head_bwd(xh, g_k, d)
            out[:, P_VG + HD * h:P_VG + HD * (h + 1)] = dv.astype(BF16)
            a_k = a_k + inc
        return (a_ql, a_kvl, a_kr, a_q, a_k)

    accs = [_sds((1, Q_LORA), F32), _sds((1, KV_LORA), F32), _sds((1, LANES), F32),
            _sds((1, HD), F32), _sds((1, HD), F32)]
    return _rowwise(name, fn, [dcqn, dckvn, dkrd, dqg, dkg, dvg, proj, cm, sm, ca, sa], gains,
                    [_sds((S, P_END), BF16)], accs, after=after)


def _mla_q_parts(qa, h, rp, ones):
    nope = qa[:, HD * h:HD * (h + 1)]
    mine = (_lane(rp.shape) >= 64) == bool(h % 2)
    rpm = jnp.where(mine, rp, 0.0)
    rr = _rms_r(_seg_sum(nope * nope + rpm * rpm, ones), MLA_QK)
    return nope, rpm, mine, rr


def _mla_prep(name, qa, kv, krd, tabs, gains):
    S = qa.shape[0]
    cm, sm = tabs

    def fn(r, c, o):
        qa_r, kv_r, krd_v, cmv, smv = r[0], r[1], r[2][...], r[3][...], r[4][...]
        g_n, g_r, g_kn = (x[...] for x in c)
        ones = _seg_ones(LANES)
        for h in range(HEADS):
            rp = qa_r[:, HEADS * HD + LANES * (h // 2):HEADS * HD + LANES * (h // 2 + 1)]
            nope, rpm, mine, rr = _mla_q_parts(qa_r, h, rp, ones)
            o[0][h, :, 0:HD] = (nope * (rr * MLA_SCALE) * g_n).astype(BF16)
            o[0][h, :, HD:2 * HD] = _rope(rpm * (rr * MLA_SCALE) * g_r, cmv, smv, 64, 32).astype(BF16)
            kn = kv_r[:, 2 * HD * h:2 * HD * h + HD]
            o[1][h, :, 0:HD] = (kn * _rms_r(_seg_sum(kn * kn, ones), HD) * g_kn).astype(BF16)
            o[1][h, :, HD:2 * HD] = krd_v
            o[2][h] = kv_r[:, 2 * HD * h + HD:2 * HD * (h + 1)].astype(BF16)
        return ()

    outs = [_sds((HEADS, S, 2 * HD), BF16), _sds((HEADS, S, 2 * HD), BF16), _sds((HEADS, S, HD), BF16)]
    return _rowwise(name, fn, [qa, kv, krd, cm, sm], gains, outs, tm=2 * ROW_TILE)


def _mla_prep_bwd(name, dq, dk, dv, qa, kv, tabs, gains):
    S = qa.shape[0]
    cm, sm = tabs

    def fn(r, c, o):
        dq_r, dk_r, dv_r, qa_r, kv_r, cmv, smv = r[0], r[1], r[2], r[3], r[4], r[5][...], r[6][...]
        g_n, g_r, g_kn = (x[...] for x in c)
        a_n = jnp.zeros((1, HD), F32)
        a_r = jnp.zeros((1, LANES), F32)
        a_kn = jnp.zeros((1, HD), F32)
        dkrd = jnp.zeros(cmv.shape, F32)
        drp = None
        ones = _seg_ones(LANES)
        for h in range(HEADS):
            rp = qa_r[:, HEADS * HD + LANES * (h // 2):HEADS * HD + LANES * (h // 2 + 1)]
            nope, rpm, mine, rr = _mla_q_parts(qa_r, h, rp, ones)
            dn = dq_r[h, :, 0:HD] * MLA_SCALE
            dr = _rope_bwd(jnp.where(mine, dq_r[h, :, HD:2 * HD] * MLA_SCALE, 0.0), cmv, smv, 64, 32)
            dot = _seg_sum(dn * g_n * nope + dr * g_r * rpm, ones) * (1.0 / MLA_QK)
            r3 = rr * rr * rr
            o[0][:, HD * h:HD * (h + 1)] = (rr * dn * g_n - nope * r3 * dot).astype(BF16)
            part = jnp.where(mine, rr * dr * g_r - rpm * r3 * dot, 0.0)
            drp = part if h % 2 == 0 else drp + part
            if h % 2 == 1:
                o[0][:, HEADS * HD + LANES * (h // 2):HEADS * HD + LANES * (h // 2 + 1)] = drp.astype(BF16)
            a_n = a_n + _csum(dn * nope * rr)
            a_r = a_r + _csum(dr * rpm * rr)
            kn = kv_r[:, 2 * HD * h:2 * HD * h + HD]
            rk = _rms_r(_seg_sum(kn * kn, ones), HD)
            dkn = dk_r[h, :, 0:HD]
            dotk = _seg_sum(dkn * g_kn * kn, ones) * (1.0 / HD)
            o[1][:, 2 * HD * h:2 * HD * h + HD] = _rms_bwd_b(kn, rk, g_kn, dkn, dotk).astype(BF16)
            o[1][:, 2 * HD * h + HD:2 * HD * (h + 1)] = dv_r[h].astype(BF16)
            a_kn = a_kn + _csum(dkn * kn * rk)
            dkrd = dkrd + dk_r[h, :, HD:2 * HD]
        o[2][...] = dkrd
        return (a_n, a_r, a_kn)

    outs = [_sds(qa.shape, BF16), _sds(kv.shape, BF16), _sds((S, LANES), F32)]
    accs = [_sds((1, HD), F32), _sds((1, LANES), F32), _sds((1, HD), F32)]
    return _rowwise(name, fn, [dq, dk, dv, qa, kv, cm, sm], gains, outs, accs)


def _seg_ones(seg):
    r = lax.broadcasted_iota(jnp.int32, (LANES, LANES), 0) // seg
    c = lax.broadcasted_iota(jnp.int32, (LANES, LANES), 1) // seg
    return (r == c).astype(F32)


def _dot01(x, sel, dims):
    hi = x.astype(BF16)
    lo = (x - hi.astype(F32)).astype(BF16)
    sb = sel.astype(BF16)
    if dims is _NN:
        return (lax.dot_general(hi, sb, dims, preferred_element_type=F32)
                + lax.dot_general(lo, sb, dims, preferred_element_type=F32))
    return (lax.dot_general(sb, hi, dims, preferred_element_type=F32)
            + lax.dot_general(sb, lo, dims, preferred_element_type=F32))


def _seg_sum(x, ones):
    return _dot01(x, ones, _NN)


def _fold(x):
    acc = x[:, 0:LANES]
    for g in range(1, x.shape[1] // LANES):
        acc = acc + x[:, LANES * g:LANES * (g + 1)]
    return acc


def _wide(v, n):
    return jnp.concatenate([v] * n, axis=1) if n > 1 else v


def _rms_bwd_b(x, r, g, dy, dot_b):
    dyg = dy * g
    return r * dyg - x * (r * r * r) * dot_b


def _seg64_r(x, ones):
    return _rms_r(_seg_sum(x * x, ones), SWA_D)


def _swa_prep(name, qkv, tabs, gains):
    S = qkv.shape[0]
    nq, nk = SWA_HEADS * SWA_D, SWA_KV * SWA_D
    cs, ss = tabs

    def fn(r, c, o):
        x_r, csv, ssv = r[0], r[1][...], r[2][...]
        g_q, g_k = c[0][...], c[1][...]
        ones = _seg_ones(SWA_D)
        for g in range((nq + nk) // LANES):
            x = x_r[:, LANES * g:LANES * (g + 1)]
            rr = _seg64_r(x, ones)
            y = _rope(x * rr * (g_q if g < nq // LANES else g_k), csv, ssv, SWA_D, SWA_ROT // 2)
            if g < nq // LANES:
                o[0][:, LANES * g:LANES * (g + 1)] = (y * SWA_SCALE).astype(BF16)
            else:
                o[1][:, LANES * g - nq:LANES * (g + 1) - nq] = y.astype(BF16)
        o[2][...] = x_r[:, nq + nk:nq + 2 * nk].astype(BF16)
        return ()

    outs = [_sds((S, nq), BF16), _sds((S, nk), BF16), _sds((S, nk), BF16)]
    return _rowwise(name, fn, [qkv, cs, ss], gains, outs, tm=2 * ROW_TILE)


def _swa_prep_bwd(name, dq, dk, dv, qkv, tabs, gains):
    nq, nk = SWA_HEADS * SWA_D, SWA_KV * SWA_D
    cs, ss = tabs

    def fn(r, c, o):
        dq_r, dk_r, x_r, csv, ssv = r[0], r[1], r[3], r[4][...], r[5][...]
        g_q, g_k = c[0][...], c[1][...]
        acc = [jnp.zeros((1, LANES), F32), jnp.zeros((1, LANES), F32)]
        ones = _seg_ones(SWA_D)
        for g in range((nq + nk) // LANES):
            isq = g < nq // LANES
            x = x_r[:, LANES * g:LANES * (g + 1)]
            rr = _seg64_r(x, ones)
            d = dq_r[:, LANES * g:LANES * (g + 1)] * SWA_SCALE if isq else dk_r[:, LANES * g - nq:LANES * (g + 1) - nq]
            d = _rope_bwd(d, csv, ssv, SWA_D, SWA_ROT // 2)
            dyg = d * (g_q if isq else g_k)
            dot = _seg_sum(dyg * x, ones) * (1.0 / SWA_D)
            o[0][:, LANES * g:LANES * (g + 1)] = (rr * dyg - x * (rr * rr * rr) * dot).astype(BF16)
            acc[0 if isq else 1] = acc[0 if isq else 1] + _csum(d * x * rr)
        o[0][:, nq + nk:nq + 2 * nk] = r[2][...].astype(BF16)
        return tuple(acc)

    return _rowwise(name, fn, [dq, dk, dv, qkv, cs, ss], gains, [_sds(qkv.shape, BF16)],
                    [_sds((1, LANES), F32), _sds((1, LANES), F32)])


def _delta(name, do, o_, after=()):
    S, C = do.shape

    def fn(r, c, o):
        for g in range(C // LANES):
            t = r[0][:, LANES * g:LANES * (g + 1)].astype(F32) * r[1][:, LANES * g:LANES * (g + 1)].astype(F32)
            o[0][g] = _rsum(t)
        return ()

    return _rowwise(name, fn, [do, o_], [], [_sds((C // LANES, S, 1), F32)], tm=4 * ROW_TILE, after=after)[0]


def _delta_t(name, do, o_, after=()):
    S, C = do.shape

    def fn(r, c, o):
        row = lax.broadcasted_iota(jnp.int32, (8, LANES), 0)
        sel = (row == _lane((8, LANES)) // SWA_D).astype(F32)
        for g in range(C // LANES):
            t = r[0][:, LANES * g:LANES * (g + 1)].astype(F32) * r[1][:, LANES * g:LANES * (g + 1)].astype(F32)
            both = _dot01(t, sel, _NT)
            o[0][2 * g] = both[0:1, :]
            o[0][2 * g + 1] = both[1:2, :]
        return ()

    return _rowwise(name, fn, [do, o_], [], [("T", _sds((2 * C // LANES, 1, S), F32))], tm=4 * ROW_TILE,
                    after=after)[0]


_NT = (((1,), (1,)), ((), ()))
_NN = (((1,), (0,)), ((), ()))
_TN = (((0,), (0,)), ((), ()))


def _flash_fwd(name, q, k, v, tq=FWD_TILE[0], tk=FWD_TILE[1]):
    H, S, dk = q.shape
    G = H // k.shape[0]
    dv = v.shape[2]
    tq, tk = _tile(S, (tq, 256, 128)), _tile(S, (tk, 256, 128))
    nk = S // tk

    def body(q_ref, k_ref, v_ref, o_ref, lse_ref, m_s, l_s, acc_s):
        j = pl.program_id(2)

        @pl.when(j == 0)
        def _():
            m_s[...] = jnp.full_like(m_s, -jnp.inf)
            l_s[...] = jnp.zeros_like(l_s)
            acc_s[...] = jnp.zeros_like(acc_s)

        s = lax.dot_general(q_ref[...], k_ref[...], _NT, preferred_element_type=F32)
        m_new = jnp.maximum(m_s[...], jnp.max(s, axis=-1, keepdims=True))
        alpha = jnp.exp(m_s[...] - m_new)
        p = jnp.exp(s - m_new)
        l_s[...] = alpha * l_s[...] + _rsum(p)
        acc_s[...] = alpha * acc_s[...] + lax.dot_general(p.astype(BF16), v_ref[...], _NN,
                                                          preferred_element_type=F32)
        m_s[...] = m_new

        @pl.when(j == nk - 1)
        def _():
            o_ref[...] = (acc_s[...] / l_s[...]).astype(o_ref.dtype)
            lse_ref[...] = m_s[...] + jnp.log(l_s[...])

    return pl.pallas_call(
        body, name=name, grid=(H, S // tq, nk),
        in_specs=[pl.BlockSpec((None, tq, dk), lambda h, i, j: (h, i, 0)),
                  pl.BlockSpec((None, tk, dk), lambda h, i, j: (h // G, j, 0)),
                  pl.BlockSpec((None, tk, dv), lambda h, i, j: (h // G, j, 0))],
        out_specs=[pl.BlockSpec((tq, dv), lambda h, i, j: (i, h)),
                   pl.BlockSpec((None, tq, 1), lambda h, i, j: (h, i, 0))],
        out_shape=[_sds((S, H * dv), BF16), _sds((H, S, 1), F32)],
        scratch_shapes=[pltpu.VMEM((tq, 1), F32), pltpu.VMEM((tq, 1), F32), pltpu.VMEM((tq, dv), F32)],
        compiler_params=_params(("parallel", "parallel", "arbitrary")),
    )(q, k, v)


def _flash_bwd(name, q, k, v, do, head0, lse, delta, tq=BWD_TILE[0], tk=BWD_TILE[1]):
    H, S, dk = q.shape
    G = H // k.shape[0]
    dv = v.shape[2]
    tq, tk = _tile(S, (tq, 256, 128)), _tile(S, (tk, 256, 128))
    nq = S // tq

    def body(q_ref, k_ref, v_ref, do_ref, lse_ref, dl_ref, dq_ref, dk_ref, dv_ref, dk_s, dv_s):
        j, i = pl.program_id(1), pl.program_id(2)

        @pl.when(i == 0)
        def _():
            dk_s[...] = jnp.zeros_like(dk_s)
            dv_s[...] = jnp.zeros_like(dv_s)

        qv, kv_, dov = q_ref[...], k_ref[...], do_ref[...]
        s = lax.dot_general(qv, kv_, _NT, preferred_element_type=F32)
        p = jnp.exp(s - lse_ref[...])
        dp = lax.dot_general(dov, v_ref[...], _NT, preferred_element_type=F32)
        ds = (p * (dp - dl_ref[...])).astype(BF16)
        dv_s[...] += lax.dot_general(p.astype(BF16), dov, _TN, preferred_element_type=F32)
        dk_s[...] += lax.dot_general(ds, qv, _TN, preferred_element_type=F32)
        dqi = lax.dot_general(ds, kv_, _NN, preferred_element_type=F32)
        rows = pl.ds(pl.multiple_of(i * tq, tq), tq)

        @pl.when(j == 0)
        def _():
            dq_ref[rows, :] = dqi

        @pl.when(j > 0)
        def _():
            dq_ref[rows, :] += dqi

        @pl.when(i == nq - 1)
        def _():
            dk_ref[...] = dk_s[...]
            dv_ref[...] = dv_s[...]

    return pl.pallas_call(
        body, name=name, grid=(H, S // tk, nq),
        in_specs=[pl.BlockSpec((None, tq, dk), lambda h, j, i: (h, i, 0)),
                  pl.BlockSpec((None, tk, dk), lambda h, j, i: (h // G, j, 0)),
                  pl.BlockSpec((None, tk, dv), lambda h, j, i: (h // G, j, 0)),
                  pl.BlockSpec((tq, dv), lambda h, j, i: (i, head0 + h)),
                  pl.BlockSpec((None, tq, 1), lambda h, j, i: (h, i, 0)),
                  pl.BlockSpec((None, tq, 1), lambda h, j, i: (head0 + h, i, 0))],
        out_specs=[pl.BlockSpec((None, S, dk), lambda h, j, i: (h, 0, 0)),
                   pl.BlockSpec((None, tk, dk), lambda h, j, i: (h, j, 0)),
                   pl.BlockSpec((None, tk, dv), lambda h, j, i: (h, j, 0))],
        out_shape=[_sds((H, S, dk), F32), _sds((H, S, dk), F32), _sds((H, S, dv), F32)],
        scratch_shapes=[pltpu.VMEM((tk, dk), F32), pltpu.VMEM((tk, dv), F32)],
        compiler_params=_params(("parallel", "arbitrary", "arbitrary")),
    )(q, k, v, do, lse, delta)


def _swa_place(ref128, h):
    e, t = h % 2, (h // (SWA_HEADS // SWA_KV)) % 2
    x = ref128.astype(F32)
    if e != t:
        x = pltpu.roll(x, 64, 1)
    return jnp.where((_lane(x.shape) >= 64) == bool(t), x, 0.0).astype(BF16)


def _swa_unplace(y, h):
    e, t = h % 2, (h // (SWA_HEADS // SWA_KV)) % 2
    return pltpu.roll(y, 64, 1) if e != t else y


def _swa_specs(width, nb):
    prev = pl.BlockSpec((SWA_BLOCK, width), lambda i: (jnp.maximum(i - 1, 0), 0))
    cur = pl.BlockSpec((SWA_BLOCK, width), lambda i: (i, 0))
    nxt = pl.BlockSpec((SWA_BLOCK, width), lambda i: (jnp.minimum(i + 1, nb - 1), 0))
    return [prev, cur, nxt]


def _swa_bias_t(i, S):
    shape = (3 * SWA_BLOCK, SWA_BLOCK)
    kpos = (i - 1) * SWA_BLOCK + lax.broadcasted_iota(jnp.int32, shape, 0)
    qpos = i * SWA_BLOCK + lax.broadcasted_iota(jnp.int32, shape, 1)
    ok = (jnp.abs(qpos - kpos) <= SWA_WINDOW) & (kpos >= 0) & (kpos < S)
    return jnp.where(ok, 0.0, -jnp.inf)


def _add_blocks(x, blocks):
    n = x.shape[1] // LANES
    return jnp.concatenate([x[:, LANES * b:LANES * (b + 1)] + blocks[b % len(blocks)] for b in range(n)], axis=1)


def _swa_stack(refs, heads):
    return jnp.concatenate([_swa_place(r[:, LANES * (h // 2):LANES * (h // 2 + 1)], h) for h in heads for r in refs],
                           axis=0)


def _swa_unstack_t(ot, j, out_ref):
    grp = SWA_HEADS // SWA_KV
    for pair in range(grp // 2):
        h = grp * j + 2 * pair
        a = _swa_unplace(ot[:, SWA_BLOCK * 2 * pair:SWA_BLOCK * (2 * pair + 1)].T, h)
        b = _swa_unplace(ot[:, SWA_BLOCK * (2 * pair + 1):SWA_BLOCK * (2 * pair + 2)].T, h + 1)
        out_ref[:, LANES * (h // 2):LANES * (h // 2 + 1)] = jnp.where(_lane(a.shape) < 64, a, b).astype(out_ref.dtype)


def _swa_fwd(name, q, k, v, sink):
    S = q.shape[0]
    nb = S // SWA_BLOCK
    grp = SWA_HEADS // SWA_KV
    smem = pl.BlockSpec(memory_space=pltpu.SMEM)

    def body(sink_ref, q_ref, kp, kc, kn, vp, vc, vn, o_ref, lse_ref):
        i = pl.program_id(0)
        bias = [_swa_bias_t(i, S)]
        kcat = [jnp.concatenate([r[:, LANES * u:LANES * (u + 1)] for r in (kp, kc, kn)], axis=0) for u in range(2)]
        vcat = [jnp.concatenate([r[:, LANES * u:LANES * (u + 1)] for r in (vp, vc, vn)], axis=0) for u in range(2)]
        for j in range(SWA_KV):
            heads = range(grp * j, grp * (j + 1))
            xq = _swa_stack([q_ref], heads)
            sk = jnp.concatenate([jnp.full((1, SWA_BLOCK), sink_ref[0, h], F32) for h in heads], axis=1)
            st = lax.dot_general(kcat[j // 2], xq, _NT, preferred_element_type=F32)
            st = _add_blocks(st, bias)
            m = jnp.maximum(jnp.max(st, axis=0, keepdims=True), sk)
            pt = jnp.exp(st - m)
            den = jnp.sum(pt, axis=0, keepdims=True) + jnp.exp(sk - m)
            ot = lax.dot_general(vcat[j // 2], pt.astype(BF16), _TN, preferred_element_type=F32) * (1.0 / den)
            lse = m + jnp.log(den)
            for e, h in enumerate(heads):
                lse_ref[h] = lse[:, SWA_BLOCK * e:SWA_BLOCK * (e + 1)]
            _swa_unstack_t(ot, j, o_ref)

    return pl.pallas_call(
        body, name=name, grid=(nb,),
        in_specs=[smem, pl.BlockSpec((SWA_BLOCK, q.shape[1]), lambda i: (i, 0))]
        + _swa_specs(k.shape[1], nb) + _swa_specs(v.shape[1], nb),
        out_specs=[pl.BlockSpec((SWA_BLOCK, q.shape[1]), lambda i: (i, 0)),
                   pl.BlockSpec((SWA_HEADS, 1, SWA_BLOCK), lambda i: (0, 0, i))],
        out_shape=[_sds(q.shape, BF16), _sds((SWA_HEADS, 1, S), F32)],
        compiler_params=_params(("parallel",)),
    )(sink, q, k, k, k, v, v, v)


def _swa_bwd_q(name, q, k, v, do, lse_t, delta_t, sink):
    S = q.shape[0]
    nb = S // SWA_BLOCK
    grp = SWA_HEADS // SWA_KV
    smem = pl.BlockSpec(memory_space=pltpu.SMEM)
    row = pl.BlockSpec((SWA_BLOCK, q.shape[1]), lambda i: (i, 0))
    hrow = pl.BlockSpec((SWA_HEADS, 1, SWA_BLOCK), lambda i: (0, 0, i))

    def body(sink_ref, q_ref, do_ref, lse_ref, dl_ref, kp, kc, kn, vp, vc, vn, dq_ref, ds_ref):
        i = pl.program_id(0)

        @pl.when(i == 0)
        def _():
            ds_ref[...] = jnp.zeros_like(ds_ref)

        bias = [_swa_bias_t(i, S)]
        kcat = [jnp.concatenate([r[:, LANES * u:LANES * (u + 1)] for r in (kp, kc, kn)], axis=0) for u in range(2)]
        vcat = [jnp.concatenate([r[:, LANES * u:LANES * (u + 1)] for r in (vp, vc, vn)], axis=0) for u in range(2)]
        for j in range(SWA_KV):
            heads = range(grp * j, grp * (j + 1))
            xq = _swa_stack([q_ref], heads)
            xdo = _swa_stack([do_ref], heads)
            lse_r = jnp.concatenate([lse_ref[h] for h in heads], axis=1)
            dl_r = jnp.concatenate([dl_ref[h] for h in heads], axis=1)
            st = lax.dot_general(kcat[j // 2], xq, _NT, preferred_element_type=F32)
            pt = jnp.exp(_add_blocks(st, bias) - lse_r)
            dpt = lax.dot_general(vcat[j // 2], xdo, _NT, preferred_element_type=F32)
            dst = (pt * (dpt - dl_r)).astype(BF16)
            _swa_unstack_t(lax.dot_general(kcat[j // 2], dst, _TN, preferred_element_type=F32), j, dq_ref)
            for h in heads:
                dsink = -_rsum(jnp.exp(sink_ref[0, h] - lse_ref[h]) * dl_ref[h])
                ds_ref[h:h + 1, :] += jnp.broadcast_to(dsink, (1, LANES))

    return pl.pallas_call(
        body, name=name, grid=(nb,),
        in_specs=[smem, row, row, hrow, hrow] + _swa_specs(k.shape[1], nb) + _swa_specs(v.shape[1], nb),
        out_specs=[row, pl.BlockSpec((SWA_HEADS, LANES), lambda i: (0, 0))],
        out_shape=[_sds(q.shape, F32), _sds((SWA_HEADS, LANES), F32)],
        compiler_params=_params(("arbitrary",)),
    )(sink, q, do, lse_t, delta_t, k, k, k, v, v, v)


def _swa_bwd_kv(name, q, k, v, do, lse_t, delta_t):
    S = q.shape[0]
    nb = S // SWA_BLOCK
    grp = SWA_HEADS // SWA_KV
    krow = pl.BlockSpec((SWA_BLOCK, k.shape[1]), lambda i: (i, 0))

    def stat3():
        return [pl.BlockSpec((SWA_HEADS, 1, SWA_BLOCK), lambda i: (0, 0, jnp.maximum(i - 1, 0))),
                pl.BlockSpec((SWA_HEADS, 1, SWA_BLOCK), lambda i: (0, 0, i)),
                pl.BlockSpec((SWA_HEADS, 1, SWA_BLOCK), lambda i: (0, 0, jnp.minimum(i + 1, nb - 1)))]

    def body(qp, qc, qn, dop, doc, don, lp, lc, ln, dp_, dc_, dn_, k_ref, v_ref, dk_ref, dv_ref):
        j = pl.program_id(0)
        nh = 2 * grp
        kpos = j * SWA_BLOCK + lax.broadcasted_iota(jnp.int32, (SWA_BLOCK, SWA_BLOCK), 0)
        bias = []
        for b in range(3):
            qpos = (j - 1 + b) * SWA_BLOCK + lax.broadcasted_iota(jnp.int32, (SWA_BLOCK, SWA_BLOCK), 1)
            ok = (jnp.abs(qpos - kpos) <= SWA_WINDOW) & (qpos >= 0) & (qpos < S)
            bias.append(jnp.where(ok, 0.0, -jnp.inf))
        for u in range(SWA_KV // 2):
            heads = range(nh * u, nh * (u + 1))
            kc = k_ref[:, LANES * u:LANES * (u + 1)]
            vc = v_ref[:, LANES * u:LANES * (u + 1)]
            xq = _swa_stack([qp, qc, qn], heads)
            xdo = _swa_stack([dop, doc, don], heads)
            lse_r = jnp.concatenate([r[h] for h in heads for r in (lp, lc, ln)], axis=1)
            dl_r = jnp.concatenate([r[h] for h in heads for r in (dp_, dc_, dn_)], axis=1)
            st = lax.dot_general(kc, xq, _NT, preferred_element_type=F32)
            pt = jnp.exp(_add_blocks(st, bias) - lse_r)
            dpt = lax.dot_general(vc, xdo, _NT, preferred_element_type=F32)
            dst = (pt * (dpt - dl_r)).astype(BF16)
            dv_ref[:, LANES * u:LANES * (u + 1)] = lax.dot_general(pt.astype(BF16), xdo, _NN, preferred_element_type=F32)
            dk_ref[:, LANES * u:LANES * (u + 1)] = lax.dot_general(dst, xq, _NN, preferred_element_type=F32)

    return pl.pallas_call(
        body, name=name, grid=(nb,),
        in_specs=_swa_specs(q.shape[1], nb) + _swa_specs(do.shape[1], nb) + stat3() + stat3() + [krow, krow],
        out_specs=[krow, krow],
        out_shape=[_sds(k.shape, F32), _sds(v.shape, F32)],
        compiler_params=_params(("parallel",)),
    )(q, q, q, do, do, do, lse_t, lse_t, lse_t, delta_t, delta_t, delta_t, k, v)


def _peer_of(x, y, c, kk):
    return (x ^ ((kk >> 2) & 1), y ^ ((kk >> 1) & 1), c ^ (kk & 1))


def _own_slot(a, scatter):
    me = 4 * lax.axis_index("x") + 2 * lax.axis_index("y") + lax.axis_index("c")
    shape = a.shape if scatter else (N_DEV,) + a.shape
    own = lax.dynamic_slice_in_dim(a, me, 1, 0) if scatter else a[None]
    return lax.dynamic_update_slice_in_dim(lax.empty(shape, a.dtype), own, me, 0)


def _exchange_start(name, tensors, lands=None):
    n = len(tensors)
    hbm = pl.BlockSpec(memory_space=pltpu.HBM)
    sem = pl.BlockSpec(memory_space=pltpu.SEMAPHORE)
    srcs = [pltpu.with_memory_space_constraint(a, pltpu.HBM) for a, _ in tensors]
    lands = [_own_slot(a, sc) if z is None else z for (a, sc), z in zip(tensors, lands or [None] * n)]
    lands = [pltpu.with_memory_space_constraint(z, pltpu.HBM) for z in lands]

    def body(*refs):
        ins, dst = refs[:n], refs[n:2 * n]
        send, recv, token = refs[2 * n], refs[2 * n + 1], refs[4 * n + 2]
        x, y, c = lax.axis_index("x"), lax.axis_index("y"), lax.axis_index("c")
        me = 4 * x + 2 * y + c
        for t in range(n):
            for kk in range(1, N_DEV):
                px, py, pc = _peer_of(x, y, c, kk)
                src = ins[t].at[4 * px + 2 * py + pc] if tensors[t][1] else ins[t]
                k1 = t * (N_DEV - 1) + kk - 1
                pltpu.make_async_remote_copy(src_ref=src, dst_ref=dst[t].at[me], send_sem=send.at[k1],
                                             recv_sem=recv.at[k1], device_id=(px, py, pc),
                                             device_id_type=pl.DeviceIdType.MESH).start()
        token[...] = jnp.zeros_like(token)

    out = pl.pallas_call(
        body, name=name,
        in_specs=[hbm] * (2 * n),
        out_specs=[sem, sem] + [hbm] * (2 * n) + [pl.BlockSpec(memory_space=pltpu.VMEM)],
        out_shape=[pltpu.SemaphoreType.DMA((n * (N_DEV - 1),)), pltpu.SemaphoreType.DMA((n * (N_DEV - 1),))]
        + [pltpu.HBM(a.shape, a.dtype) for a in srcs] + [pltpu.HBM(a.shape, a.dtype) for a in lands]
        + [_sds((8, LANES), F32)],
        input_output_aliases={i: i + 2 for i in range(2 * n)},
        compiler_params=pltpu.CompilerParams(has_side_effects=pltpu.SideEffectType.DATAFLOW_SIDE_EFFECTING),
    )(*srcs, *lands)
    return (out[0], out[1], out[2:2 + n], out[2 + n:2 + 2 * n], [sc for _, sc in tensors]), out[2 + 2 * n]


def _exchange_wait(name, started, after):
    send_s, recv_s, srcs, lands, flags = started
    n = len(srcs)
    after = list(after) if isinstance(after, (list, tuple)) else [after]
    hbm = pl.BlockSpec(memory_space=pltpu.HBM)
    sem = pl.BlockSpec(memory_space=pltpu.SEMAPHORE)

    def body(*refs):
        ins, dst, send, recv = refs[:n], refs[n:2 * n], refs[2 * n], refs[2 * n + 1]
        x, y, c = lax.axis_index("x"), lax.axis_index("y"), lax.axis_index("c")
        me = 4 * x + 2 * y + c
        for t in range(n):
            for kk in range(1, N_DEV):
                px, py, pc = _peer_of(x, y, c, kk)
                src = ins[t].at[me] if flags[t] else ins[t]
                k1 = t * (N_DEV - 1) + kk - 1
                cp = pltpu.make_async_remote_copy(src_ref=src, dst_ref=dst[t].at[4 * px + 2 * py + pc],
                                                  send_sem=send.at[k1], recv_sem=recv.at[k1],
                                                  device_id=(px, py, pc), device_id_type=pl.DeviceIdType.MESH)
                cp.wait_send()
                cp.wait_recv()

    out = pl.pallas_call(
        body, name=name,
        in_specs=[hbm] * (2 * n) + [sem, sem] + [pl.BlockSpec(memory_space=pl.ANY)] * len(after),
        out_specs=[hbm] * (2 * n),
        out_shape=[pltpu.HBM(a.shape, a.dtype) for a in srcs] + [pltpu.HBM(a.shape, a.dtype) for a in lands],
        input_output_aliases={i: i for i in range(2 * n)},
        compiler_params=pltpu.CompilerParams(has_side_effects=pltpu.SideEffectType.DATAFLOW_SIDE_EFFECTING),
    )(*srcs, *lands, send_s, recv_s, *after)
    return out[n:]


def _sum8(name, parts):
    _, R, C = parts.shape

    def fn(r, c, o):
        acc = r[0][0].astype(F32)
        for s in range(1, N_DEV):
            acc = acc + r[0][s].astype(F32)
        o[0][...] = acc
        return ()

    tm = R if R % 256 else 256
    return _rowwise(name, fn, [parts], [], [_sds((R, C), F32)], tm=tm)[0]


def _adamw(name, w, g, m, v):
    bc1 = 1.0 - ADAM_B1 ** ADAM_STEP
    bc2 = 1.0 - ADAM_B2 ** ADAM_STEP

    def fn(r, c, o):
        wv, gv, mv, vv = (x[...] for x in r)
        mn = ADAM_B1 * mv + (1.0 - ADAM_B1) * gv
        vn = ADAM_B2 * vv + (1.0 - ADAM_B2) * (gv * gv)
        o[0][...] = -ADAM_LR * ((mn / bc1) / (jnp.sqrt(vn / bc2) + ADAM_EPS) + ADAM_WD * wv)
        o[1][...] = mn
        o[2][...] = vn
        return ()

    tm = max(8, min(512, 1 << ((16 << 20) // (56 * w.shape[1])).bit_length() - 1))
    return _rowwise(name, fn, [w, g, m, v], [], [_sds(w.shape, F32)] * 3, tm=tm)


def _rope_cs(pos, dim, theta):
    inv = jnp.float32(theta) ** (-jnp.arange(0, dim, 2, dtype=jnp.float32) / dim)
    ang = pos.astype(jnp.float32)[:, None] * inv[None, :]
    return jnp.cos(ang), jnp.sin(ang)


def _tables(S):
    pos = jnp.arange(S)
    c, s = _rope_cs(pos, MLA_ROPE, ROPE_THETA)
    mla = (jnp.concatenate([c, c, c, c], 1), jnp.concatenate([-s, s, -s, s], 1))
    rc, rs = _rope_cs(pos // GRID_W, HD // 2, AXIAL_THETA)
    cc, cs = _rope_cs(pos % GRID_W, HD // 2, AXIAL_THETA)
    axial = (jnp.concatenate([rc, rc, cc, cc], 1), jnp.concatenate([-rs, rs, -cs, cs], 1))
    c, s = _rope_cs(pos, SWA_ROT, ROPE_THETA)
    one, zero = jnp.ones((S, SWA_D - SWA_ROT), F32), jnp.zeros((S, SWA_D - SWA_ROT), F32)
    swa = (jnp.concatenate([c, c, one, c, c, one], 1), jnp.concatenate([-s, s, zero, -s, s, zero], 1))
    return mla, axial, swa


_WEIGHTS = ['even_norm', 'even_w_in', 'mla_q_lat_norm', 'mla_kv_lat_norm', 'mla_w_uq', 'mla_w_ukv', 'mla_q_norm',
            'mla_k_nope_norm', 'mla_k_rope_norm', 'gqa_q_norm', 'gqa_k_norm', 'even_w_out', 'odd_norm', 'odd_w_qkv',
            'swa_q_norm', 'swa_k_norm', 'swa_sink', 'odd_w_out', 'mlp_norm', 'mlp_w_up', 'mlp_w_down']
_SMALL = ['even_norm', 'mla_q_lat_norm', 'mla_kv_lat_norm', 'mla_q_norm', 'mla_k_nope_norm', 'mla_k_rope_norm',
          'gqa_q_norm', 'gqa_k_norm', 'swa_q_norm', 'swa_k_norm', 'swa_sink', 'mlp_norm']


def _relu2(acc):
    rl = jnp.maximum(acc, 0.0)
    return rl * rl, rl


def _mul2(acc, rl):
    return (acc * (2.0 * rl.astype(F32)),)


def _add(acc, res):
    return (acc + res,)


def _step(x, tgt, w, m, v):
    S, D = x.shape
    nw = len(_WEIGHTS)
    tab_mla, tab_ax, tab_swa = _tables(S)
    bf = lambda a: a.astype(BF16)

    w_up_s, w_dn_s = w['mlp_w_up'], w['mlp_w_down']
    first, tok0 = _exchange_start("gather_first_start", [
        (bf(w['even_w_in'][0].T), False), (bf(w['mla_w_uq'][0].T), False), (bf(w['mla_w_ukv'][0].T), False)])
    h0 = _rmsnorm("even_norm", x, w['even_norm'], after=[tok0])
    rest = [(bf(w['even_w_out'][0]), False), (bf(w['odd_w_qkv'][0].T), False), (bf(w['odd_w_out'][0]), False),
            (bf(w_up_s[0].T), False), (bf(w_up_s[1].T), False), (bf(w_dn_s[0]), False), (bf(w_dn_s[1]), False)]
    rest_lands = [_own_slot(a, sc) for a, sc in rest]
    gathered = _exchange_wait("gather_first_wait", first, [h0, *tab_mla, *tab_ax, *tab_swa, *rest_lands])
    zero = jnp.minimum(jnp.abs(gathered[2][0, 0:1, 0:1].astype(F32)), 0.0)
    later, tok = _exchange_start("gather_rest_start", rest + [(w['odd_norm'] + zero, False)], rest_lands + [None])
    flat = lambda a: a.reshape((a.shape[0] * a.shape[1],) + a.shape[2:])
    win_t, wuq_g, wukv_t = [flat(a) for a in gathered]
    win_t = jnp.concatenate([win_t[:KR_END], jnp.zeros((LANES - MLA_ROPE, D), BF16), win_t[KR_END:]], 0)
    wuq_g = gathered[1]
    wuq_t = jnp.concatenate([wuq_g[:, :HD].reshape(HEADS * HD, Q_LORA),
                             wuq_g[:, HD:].reshape(HEADS * MLA_ROPE, Q_LORA)], 0)

    z64 = jnp.zeros((1, 64), F32)
    qn = w['mla_q_norm']
    g_even = [w['mla_q_lat_norm'], w['mla_kv_lat_norm'], jnp.concatenate([w['mla_k_rope_norm'], z64], 1),
              w['gqa_q_norm'], w['gqa_k_norm']]
    g_mla = [qn[:, :HD], jnp.concatenate([qn[:, HD:], qn[:, HD:]], 1), w['mla_k_nope_norm']]
    g_swa = [jnp.concatenate([w['swa_q_norm']] * 2, 1), jnp.concatenate([w['swa_k_norm']] * 2, 1)]

    def loss_head(acc, res, t):
        e = acc + res - t
        return e * (1.0 / D), _csum(_rsum(e * e))

    def mlp_fwd(l, xin, last=False):
        hn = _rmsnorm(f"mlp{l}_norm", xin, w['mlp_norm'][l:l + 1])
        a, rl = _mm(f"mlp{l}_up", hn, wup_t[l], "nt", (BF16, BF16), epilogue=_relu2)
        if last:
            xout = _mm(f"mlp{l}_down", a, wdn[l], "nn", (F32,), epilogue=loss_head, extras=(xin, tgt),
                       sums=[_sds((1, 1), F32)])
        else:
            xout = _mm(f"mlp{l}_down", a, wdn[l], "nn", (F32,), epilogue=_add, extras=(xin,))
        return xout, (hn, a, rl)

    proj = _mm("even_in", h0, win_t, "nt", (F32,), tm=512, tn=P_END)
    cqn, ckvn, krd, qg, kg, vg = _even_prep("even_prep", proj, tab_mla + tab_ax, g_even, after=[tok])
    qa = _mm("mla_uq", cqn, wuq_t, "nt", (F32,), tn=wuq_t.shape[0])
    kv = _mm("mla_ukv", ckvn, wukv_t, "nt", (F32,))
    q_a, k_a, v_a = _mla_prep("mla_prep", qa, kv, krd, tab_mla, g_mla)
    o_a, lse_a = _flash_fwd("mla_attn", q_a, k_a, v_a)
    o_g, lse_g = _flash_fwd("gqa_attn", qg, kg, vg)
    merged = jnp.concatenate([o_a, o_g], 1)
    rest = _exchange_wait("gather_rest_wait", later, merged)
    wout_e, wqkv_t, wout_o, wup0_t, wup1_t, wdn0, wdn1 = [flat(a) for a in rest[:7]]
    odd_norm = rest[7].reshape(1, D)
    wup_t, wdn = (wup0_t, wup1_t), (wdn0, wdn1)
    x1 = _mm("even_out", merged, wout_e, "nn", (F32,), epilogue=_add, extras=(x,))
    x2, mlp0 = mlp_fwd(0, x1)

    h1 = _rmsnorm("odd_norm", x2, odd_norm)
    qkv = _mm("odd_qkv", h1, wqkv_t, "nt", (F32,), tn=wqkv_t.shape[0] // 2)
    q_s, k_s, v_s = _swa_prep("swa_prep", qkv, tab_swa, g_swa)
    o_s, lse_s = _swa_fwd("swa_attn", q_s, k_s, v_s, w['swa_sink'])
    x3 = _mm("odd_out", o_s, wout_o, "nn", (F32,), epilogue=_add, extras=(x2,))
    (dy, loss_acc), mlp1 = mlp_fwd(1, x3, last=True)
    loss = lax.psum(0.5 / D * loss_acc[0, 0], ("x", "y", "c"))

    gsm = {}

    def mlp_bwd(l, dout, xin, saved):
        hn, a, rl = saved
        du = _mm(f"mlp{l}_dact", dout, wdn[l], "nt", (BF16,), epilogue=_mul2, extras=(rl,))
        g_dn = _mm(f"mlp{l}_gdown", a, dout, "tn", (BF16,))
        g_up = _mm(f"mlp{l}_gup", du, hn, "tn", (BF16,))
        dhn = _mm(f"mlp{l}_dnorm", du, wup_t[l], "nn", (F32,))
        din, g_n = _rmsnorm_bwd(f"mlp{l}_norm_bwd", dout, xin, dhn, w['mlp_norm'][l:l + 1])
        return din, g_dn, g_up, g_n

    dx3, g_dn1, g_up1, g_mn1 = mlp_bwd(1, dy, x3, mlp1)
    split = lambda a: a.reshape((N_DEV, a.shape[0] // N_DEV) + a.shape[1:])
    sc1, tok1 = _exchange_start("scatter_mlp1_start", [(split(g_up1), True), (split(g_dn1), True)])

    g_wout_o = _mm("odd_gout", o_s, dx3, "tn", (BF16,))
    do_s = _mm("odd_dattn", dx3, wout_o, "nt", (BF16,))
    dl_s = _delta_t("swa_delta", do_s, o_s, after=[tok1])
    dq_s, dsink = _swa_bwd_q("swa_bwd_q", q_s, k_s, v_s, do_s, lse_s, dl_s, w['swa_sink'])
    dk_s, dv_s = _swa_bwd_kv("swa_bwd_kv", q_s, k_s, v_s, do_s, lse_s, dl_s)
    dqkv, g_sq, g_sk = _swa_prep_bwd("swa_prep_bwd", dq_s, dk_s, dv_s, qkv, tab_swa, g_swa)
    g_wqkv = _mm("odd_gqkv", dqkv, h1, "tn", (BF16,), tm=dqkv.shape[1] // 2)
    dh1 = _mm("odd_dnorm", dqkv, wqkv_t, "nn", (F32,), tk=dqkv.shape[1])
    dx2, g_on = _rmsnorm_bwd("odd_norm_bwd", dx3, x2, dh1, odd_norm)
    gsm['swa_q_norm'] = g_sq[:, :64] + g_sq[:, 64:]
    gsm['swa_k_norm'] = g_sk[:, :64] + g_sk[:, 64:]
    gsm['swa_sink'] = dsink[:, 0].reshape(1, SWA_HEADS)

    dx1, g_dn0, g_up0, g_mn0 = mlp_bwd(0, dx2, x1, mlp0)
    sc2, tok2 = _exchange_start("scatter_mid_start", [(split(g_wout_o), True), (split(g_wqkv), True),
                                                      (split(g_up0), True), (split(g_dn0), True)])
    gsm['mlp_norm'] = jnp.concatenate([g_mn0, g_mn1], 0)

    g_wout_e = _mm("even_gout", merged, dx1, "tn", (BF16,))
    dmerged = _mm("even_dattn", dx1, wout_e, "nt", (BF16,))
    dl_e = _delta("even_delta", dmerged, merged, after=[tok2])
    dq_a, dk_a, dv_a = _flash_bwd("mla_attn_bwd", q_a, k_a, v_a, dmerged, 0, lse_a, dl_e)
    dq_g, dk_g, dv_g = _flash_bwd("gqa_attn_bwd", qg, kg, vg, dmerged, HEADS, lse_g, dl_e)
    dqa, dkv, dkrd, g_qnn, g_qnr, g_kn = _mla_prep_bwd("mla_prep_bwd", dq_a, dk_a, dv_a, qa, kv, tab_mla, g_mla)
    g_wuq = _mm("mla_guq", dqa, cqn, "tn", (BF16,))
    dcqn = _mm("mla_dq_lat", dqa, wuq_t, "nn", (F32,))
    g_wukv = _mm("mla_gukv", dkv, ckvn, "tn", (BF16,))
    dckvn = _mm("mla_dkv_lat", dkv, wukv_t, "nn", (F32,))
    g_wuq = jnp.concatenate([g_wuq[:HEADS * HD].reshape(HEADS, HD, Q_LORA),
                             g_wuq[HEADS * HD:].reshape(HEADS, MLA_ROPE, Q_LORA)], 1)
    sc3, tok_e = _exchange_start("scatter_even_start", [(g_wuq, True), (split(g_wukv), True), (split(g_wout_e), True)])
    dproj, g_ql, g_kvl, g_kr, g_gq, g_gk = _even_prep_bwd("even_prep_bwd", dcqn, dckvn, dkrd, dq_g, dk_g, dv_g,
                                                          proj, tab_mla + tab_ax, g_even, after=[tok_e])
    g_win = _mm("even_gin", dproj, h0, "tn", (BF16,), tm=P_END, tk=1024)
    dh0 = _mm("even_dnorm", dproj, win_t, "nn", (F32,), tk=P_END)
    grad_x, g_en = _rmsnorm_bwd("even_norm_bwd", dx1, x, dh0, w['even_norm'])
    gsm.update(even_norm=g_en, mla_q_lat_norm=g_ql, mla_kv_lat_norm=g_kvl,
               mla_q_norm=jnp.concatenate([g_qnn, g_qnr[:, :64] + g_qnr[:, 64:]], 1), mla_k_nope_norm=g_kn,
               mla_k_rope_norm=g_kr[:, :64], gqa_q_norm=g_gq, gqa_k_norm=g_gk)

    g_win = jnp.concatenate([g_win[:KR_END], g_win[P_QG:]], 0)
    small_sizes = [w[n].size for n in _SMALL] + [D]
    small_vec = jnp.concatenate([gsm[n].reshape(1, -1) for n in _SMALL] + [g_on], 1)
    pad = (-small_vec.shape[1]) % LANES
    small_vec = jnp.pad(small_vec, ((0, 0), (0, pad)))
    last_st, tok3 = _exchange_start("scatter_last_start", [(split(g_win), True), (small_vec, False)])
    p_up1, p_dn1 = _exchange_wait("scatter_mlp1_wait", sc1, tok3)
    p_wout_o, p_wqkv, p_up0, p_dn0 = _exchange_wait("scatter_mid_wait", sc2, tok3)
    p_wuq, p_wukv, p_wout_e = _exchange_wait("scatter_even_wait", sc3, tok3)
    parts = dict(odd_w_qkv=p_wqkv, odd_w_out=p_wout_o, up0=p_up0, up1=p_up1, dn0=p_dn0, dn1=p_dn1,
                 mla_w_uq=p_wuq, mla_w_ukv=p_wukv, even_w_out=p_wout_e)
    red = {n: _sum8("sum_" + n, p) for n, p in parts.items()}
    grads = {
        'odd_w_qkv': red['odd_w_qkv'].T[None], 'odd_w_out': red['odd_w_out'][None],
        'mlp_w_up': jnp.stack([red['up0'].T, red['up1'].T]), 'mlp_w_down': jnp.stack([red['dn0'], red['dn1']]),
        'mla_w_uq': red['mla_w_uq'].T[None], 'mla_w_ukv': red['mla_w_ukv'].T[None],
        'even_w_out': red['even_w_out'][None],
    }
    delta, new_m, new_v = {}, {}, {}

    def apply(names):
        for n in names:
            shp = w[n].shape
            two = lambda a: a.reshape(shp[0] * shp[1], shp[2])
            d_, m_, v_ = _adamw("adamw_" + n, two(w[n]), two(grads[n]), two(m[n]), two(v[n]))
            delta[n], new_m[n], new_v[n] = d_.reshape(shp), m_.reshape(shp), v_.reshape(shp)

    apply(['mla_w_uq', 'mla_w_ukv', 'even_w_out', 'odd_w_qkv', 'odd_w_out', 'mlp_w_up', 'mlp_w_down'])
    last = _exchange_wait("scatter_last_wait", last_st, new_v['mlp_w_down'])
    small_g = _sum8("sum_small", last[1])
    grads['even_w_in'] = _sum8("sum_even_w_in", last[0]).T[None]
    apply(['even_w_in'])
    off = 0
    for n, sz in zip(_SMALL + ['odd_norm_full'], small_sizes):
        seg = small_g[:, off:off + sz]
        off += sz
        if n == 'odd_norm_full':
            me = 4 * lax.axis_index("x") + 2 * lax.axis_index("y") + lax.axis_index("c")
            grads['odd_norm'] = lax.dynamic_slice(seg, (0, me * (D // N_DEV)), (1, D // N_DEV))
        else:
            grads[n] = seg.reshape(w[n].shape)

    sm_names = _SMALL + ['odd_norm']
    pack = lambda d: jnp.concatenate([d[n].reshape(1, -1) for n in sm_names], 1)
    pw, pg, pm, pv = pack(w), pack(grads), pack(m), pack(v)
    padw = (-pw.shape[1]) % LANES
    padf = lambda a: jnp.pad(a, ((0, 0), (0, padw)))
    d_, m_, v_ = _adamw("adamw_small", padf(pw), padf(pg), padf(pm), jnp.pad(pv, ((0, 0), (0, padw)), constant_values=1.0))
    off = 0
    for n in sm_names:
        sz = w[n].size
        delta[n] = d_[:, off:off + sz].reshape(w[n].shape)
        new_m[n] = m_[:, off:off + sz].reshape(w[n].shape)
        new_v[n] = v_[:, off:off + sz].reshape(w[n].shape)
        off += sz

    return (loss, grad_x[None], *[grads[n] for n in _WEIGHTS], *[delta[n] for n in _WEIGHTS],
            *[new_m[n] for n in _WEIGHTS], *[new_v[n] for n in _WEIGHTS])


def kernel(x, even_norm, even_w_in, mla_q_lat_norm, mla_kv_lat_norm, mla_w_uq, mla_w_ukv, mla_q_norm, mla_k_nope_norm, mla_k_rope_norm, gqa_q_norm, gqa_k_norm, even_w_out, odd_norm, odd_w_qkv, swa_q_norm, swa_k_norm, swa_sink, odd_w_out, mlp_norm, mlp_w_up, mlp_w_down, loss_target, m_even_norm, m_even_w_in, m_mla_q_lat_norm, m_mla_kv_lat_norm, m_mla_w_uq, m_mla_w_ukv, m_mla_q_norm, m_mla_k_nope_norm, m_mla_k_rope_norm, m_gqa_q_norm, m_gqa_k_norm, m_even_w_out, m_odd_norm, m_odd_w_qkv, m_swa_q_norm, m_swa_k_norm, m_swa_sink, m_odd_w_out, m_mlp_norm, m_mlp_w_up, m_mlp_w_down, v_even_norm, v_even_w_in, v_mla_q_lat_norm, v_mla_kv_lat_norm, v_mla_w_uq, v_mla_w_ukv, v_mla_q_norm, v_mla_k_nope_norm, v_mla_k_rope_norm, v_gqa_q_norm, v_gqa_k_norm, v_even_w_out, v_odd_norm, v_odd_w_qkv, v_swa_q_norm, v_swa_k_norm, v_swa_sink, v_odd_w_out, v_mlp_norm, v_mlp_w_up, v_mlp_w_down):
    ws = (even_norm, even_w_in, mla_q_lat_norm, mla_kv_lat_norm, mla_w_uq, mla_w_ukv, mla_q_norm, mla_k_nope_norm, mla_k_rope_norm, gqa_q_norm, gqa_k_norm, even_w_out, odd_norm, odd_w_qkv, swa_q_norm, swa_k_norm, swa_sink, odd_w_out, mlp_norm, mlp_w_up, mlp_w_down)
    ms = (m_even_norm, m_even_w_in, m_mla_q_lat_norm, m_mla_kv_lat_norm, m_mla_w_uq, m_mla_w_ukv, m_mla_q_norm, m_mla_k_nope_norm, m_mla_k_rope_norm, m_gqa_q_norm, m_gqa_k_norm, m_even_w_out, m_odd_norm, m_odd_w_qkv, m_swa_q_norm, m_swa_k_norm, m_swa_sink, m_odd_w_out, m_mlp_norm, m_mlp_w_up, m_mlp_w_down)
    vs = (v_even_norm, v_even_w_in, v_mla_q_lat_norm, v_mla_kv_lat_norm, v_mla_w_uq, v_mla_w_ukv, v_mla_q_norm, v_mla_k_nope_norm, v_mla_k_rope_norm, v_gqa_q_norm, v_gqa_k_norm, v_even_w_out, v_odd_norm, v_odd_w_qkv, v_swa_q_norm, v_swa_k_norm, v_swa_sink, v_odd_w_out, v_mlp_norm, v_mlp_w_up, v_mlp_w_down)
    return _step(x[0], loss_target[0], dict(zip(_WEIGHTS, ws)), dict(zip(_WEIGHTS, ms)), dict(zip(_WEIGHTS, vs)))
```

```python
import jax
import jax.numpy as jnp
from jax import lax
from jax.experimental import pallas as pl
from jax.experimental.pallas import tpu as pltpu

F32 = jnp.float32
BF16 = jnp.bfloat16

N_DEV = 8
NORM_EPS = 1e-6
ROPE_THETA = 500000.0
AXIAL_THETA = 10000.0
GRID_W = 64
HEADS = 8
GQA_KV = 2
HD = 128
MLA_ROPE = 64
MLA_QK = HD + MLA_ROPE
Q_LORA = 512
KV_LORA = 256
SWA_HEADS = 32
SWA_KV = 4
SWA_D = 64
SWA_ROT = 16
SWA_WINDOW = 128
SWA_BLOCK = 128
MLA_SCALE, GQA_SCALE, SWA_SCALE = MLA_QK ** -0.5, HD ** -0.5, SWA_D ** -0.5
LANES = 128
ADAM_LR, ADAM_B1, ADAM_B2, ADAM_EPS, ADAM_WD, ADAM_STEP = 0.001, 0.9, 0.999, 1e-08, 0.01, 10
VMEM_LIMIT = 56 * 1024 * 1024
MM_TILE = (1024, 1024, 2048)
FWD_TILE = (1024, 8192)
BWD_TILE = (2048, 1024)
ROW_TILE = 256

P_CQ, P_CKV, P_KR, P_QG = 0, Q_LORA, Q_LORA + KV_LORA, Q_LORA + KV_LORA + LANES
P_KG = P_QG + HEADS * HD
P_VG = P_KG + GQA_KV * HD
P_END = P_VG + GQA_KV * HD
KR_END = Q_LORA + KV_LORA + MLA_ROPE


def _tile(n, prefs):
    for t in prefs:
        if n % t == 0 and t <= n:
            return t
    return n


def _params(sem):
    return pltpu.CompilerParams(dimension_semantics=sem, vmem_limit_bytes=VMEM_LIMIT)


_DIMS = {"nn": ((1,), (0,)), "nt": ((1,), (1,)), "tn": ((0,), (0,))}


def _mm(name, a, b, mode, out_dtypes, epilogue=None, extras=(), sums=(), tm=MM_TILE[0], tn=MM_TILE[1], tk=MM_TILE[2]):
    if mode == "nn":
        (M, K), (_, N) = a.shape, b.shape
    elif mode == "nt":
        (M, K), (N, _) = a.shape, b.shape
    else:
        (K, M), (_, N) = a.shape, b.shape
    tm = _tile(M, (tm, 512, 256, 128))
    tn = _tile(N, (tn, 512, 256, 128))
    tk = _tile(K, (tk, 1024, 512, 256, 128))
    nk = K // tk
    ne, no, ns = len(extras), len(out_dtypes), len(sums)
    if mode == "tn":
        a_spec = pl.BlockSpec((tk, tm), lambda i, j, k: (k, i))
    else:
        a_spec = pl.BlockSpec((tm, tk), lambda i, j, k: (i, k))
    if mode == "nt":
        b_spec = pl.BlockSpec((tn, tk), lambda i, j, k: (j, k))
    else:
        b_spec = pl.BlockSpec((tk, tn), lambda i, j, k: (k, j))
    o_spec = pl.BlockSpec((tm, tn), lambda i, j, k: (i, j))
    dims = (_DIMS[mode], ((), ()))

    def body(a_ref, b_ref, *rest):
        ex, outs = rest[:ne], rest[ne:ne + no]
        k = pl.program_id(2)
        part = lax.dot_general(a_ref[...].astype(BF16), b_ref[...].astype(BF16), dims, preferred_element_type=F32)

        def finish(total):
            res = epilogue(total, *[e[...] for e in ex]) if epilogue else (total,)
            for o, r in zip(outs, res[:no]):
                o[...] = r.astype(o.dtype)
            if ns:
                first = (pl.program_id(0) == 0) & (pl.program_id(1) == 0)

                @pl.when(first)
                def _():
                    for s_ref, r in zip(rest[ne + no:ne + no + ns], res[no:]):
                        s_ref[...] = r

                @pl.when(jnp.logical_not(first))
                def _():
                    for s_ref, r in zip(rest[ne + no:ne + no + ns], res[no:]):
                        s_ref[...] += r

        if nk == 1:
            finish(part)
            return
        acc = rest[ne + no + ns]

        @pl.when(k == 0)
        def _():
            acc[...] = part

        @pl.when((k > 0) & (k < nk - 1))
        def _():
            acc[...] += part

        @pl.when(k == nk - 1)
        def _():
            finish(acc[...] + part)

    out = pl.pallas_call(
        body, name=name, grid=(M // tm, N // tn, nk),
        in_specs=[a_spec, b_spec] + [o_spec] * ne,
        out_specs=[o_spec] * no + [pl.BlockSpec(tuple(s.shape), lambda i, j, k: (0,) * len(s.shape)) for s in sums],
        out_shape=[jax.ShapeDtypeStruct((M, N), d) for d in out_dtypes] + list(sums),
        scratch_shapes=[pltpu.VMEM((tm, tn), F32)] if nk > 1 else [],
        compiler_params=_params(("arbitrary",) * 3 if ns else ("parallel", "parallel", "arbitrary")),
    )(a, b, *extras)
    return out[0] if no + ns == 1 else out


def _rowwise(name, fn, rows, consts, outs, accs=(), tm=ROW_TILE, after=()):
    S = rows[0].shape[-2]
    tm = _tile(S, (tm, 128, 64, 32, 16, 8))
    nr, nc, nd, no, na = len(rows), len(consts), len(after), len(outs), len(accs)

    def rspec(shape):
        if len(shape) == 2:
            return pl.BlockSpec((tm, shape[1]), lambda i: (i, 0))
        return pl.BlockSpec((shape[0], tm, shape[2]), lambda i: (0, i, 0))

    def cspec(shape):
        return pl.BlockSpec(tuple(shape), lambda i: (0,) * len(shape))

    def ospec(o):
        if isinstance(o, tuple):
            return pl.BlockSpec((o[1].shape[0], o[1].shape[1], tm), lambda i: (0, 0, i))
        return rspec(o.shape)

    out_sds = [o[1] if isinstance(o, tuple) else o for o in outs]

    def body(*refs):
        r, c = refs[:nr], refs[nr:nr + nc]
        o, a = refs[nr + nc + nd:nr + nc + nd + no], refs[nr + nc + nd + no:]
        vals = fn(r, c, o)
        if na:
            @pl.when(pl.program_id(0) == 0)
            def _():
                for ar in a:
                    ar[...] = jnp.zeros_like(ar)

            for ar, v in zip(a, vals):
                ar[...] += v

    res = pl.pallas_call(
        body, name=name, grid=(S // tm,),
        in_specs=[rspec(x.shape) for x in rows] + [cspec(x.shape) for x in consts]
        + [pl.BlockSpec(memory_space=pl.ANY)] * len(after),
        out_specs=[ospec(x) for x in outs] + [cspec(x.shape) for x in accs],
        out_shape=out_sds + list(accs),
        compiler_params=_params(("arbitrary",) if na else ("parallel",)),
    )(*rows, *consts, *after)
    return res


def _sds(shape, dtype):
    return jax.ShapeDtypeStruct(tuple(shape), dtype)


def _rsum(x):
    return jnp.sum(x, axis=-1, keepdims=True)


def _csum(x):
    return jnp.sum(x, axis=0, keepdims=True)


def _lane(shape):
    return lax.broadcasted_iota(jnp.int32, shape, 1)


def _partner(x, seg, half):
    lane = _lane(x.shape) % seg
    return jnp.where(lane < half, pltpu.roll(x, LANES - half, 1), pltpu.roll(x, half, 1))


def _rope(x, c, s, seg, half):
    return x * c + _partner(x, seg, half) * s


def _rope_bwd(dy, c, s, seg, half):
    t = _partner(dy * s, seg, half)
    if seg != 2 * half:
        t = jnp.where(_lane(dy.shape) % seg < 2 * half, t, 0.0)
    return dy * c + t


def _rms_r(ss, n):
    return lax.rsqrt(ss * (1.0 / n) + NORM_EPS)


def _rms_bwd(x, r, g, dy, dot_scale):
    dyg = dy * g
    return r * dyg - x * (r * r * r) * (_rsum(dyg * x) * dot_scale)


def _rmsnorm(name, x, g, after=()):
    D = x.shape[1]

    def fn(r, c, o):
        xv = r[0][...]
        o[0][...] = (xv * _rms_r(_rsum(xv * xv), D) * c[0][...]).astype(BF16)
        return ()

    return _rowwise(name, fn, [x], [g], [_sds(x.shape, BF16)], tm=4 * ROW_TILE, after=after)[0]


def _rmsnorm_bwd(name, dres, x, dh, g):
    D = x.shape[1]

    def fn(r, c, o):
        xv, dhv, gv = r[1][...], r[2][...], c[0][...]
        rr = _rms_r(_rsum(xv * xv), D)
        o[0][...] = r[0][...] + _rms_bwd(xv, rr, gv, dhv, 1.0 / D)
        return (_csum(dhv * xv * rr),)

    return _rowwise(name, fn, [dres, x, dh], [g], [_sds(x.shape, F32)], [_sds((1, D), F32)], tm=2 * ROW_TILE)


def _even_prep(name, proj, tabs, gains, after=()):
    S = proj.shape[0]
    cm, sm, ca, sa = tabs

    def fn(r, c, o):
        p, cmv, smv, cav, sav = r[0], r[1][...], r[2][...], r[3][...], r[4][...]
        g_ql, g_kvl, g_kr, g_q, g_k = (x[...] for x in c)
        ones = _seg_ones(LANES)
        cq = p[:, P_CQ:P_CKV]
        o[0][...] = (cq * _wide(_rms_r(_seg_sum(_fold(cq * cq), ones), Q_LORA), Q_LORA // LANES) * g_ql).astype(BF16)
        ckv = p[:, P_CKV:P_KR]
        o[1][...] = (ckv * _wide(_rms_r(_seg_sum(_fold(ckv * ckv), ones), KV_LORA), KV_LORA // LANES) * g_kvl).astype(BF16)
        kr = p[:, P_KR:P_QG]
        y = _rope(kr * _rms_r(_seg_sum(kr * kr, ones), MLA_ROPE) * g_kr, cmv, smv, 64, 32)
        o[2][...] = (y + pltpu.roll(y, 64, 1)).astype(BF16)
        for h in range(HEADS):
            xh = p[:, P_QG + HD * h:P_QG + HD * (h + 1)]
            o[3][h] = (_rope(xh * _rms_r(_seg_sum(xh * xh, ones), HD) * g_q, cav, sav, 64, 32) * GQA_SCALE).astype(BF16)
        for h in range(GQA_KV):
            xh = p[:, P_KG + HD * h:P_KG + HD * (h + 1)]
            o[4][h] = _rope(xh * _rms_r(_seg_sum(xh * xh, ones), HD) * g_k, cav, sav, 64, 32).astype(BF16)
            o[5][h] = p[:, P_VG + HD * h:P_VG + HD * (h + 1)].astype(BF16)
        return ()

    outs = [_sds((S, Q_LORA), BF16), _sds((S, KV_LORA), BF16), _sds((S, LANES), BF16),
            _sds((HEADS, S, HD), BF16), _sds((GQA_KV, S, HD), BF16), _sds((GQA_KV, S, HD), BF16)]
    return _rowwise(name, fn, [proj, cm, sm, ca, sa], gains, outs, tm=2 * ROW_TILE, after=after)


def _even_prep_bwd(name, dcqn, dckvn, dkrd, dqg, dkg, dvg, proj, tabs, gains, after=()):
    S = proj.shape[0]
    cm, sm, ca, sa = tabs

    def fn(r, c, o):
        p, cmv, smv, cav, sav = r[6], r[7][...], r[8][...], r[9][...], r[10][...]
        g_ql, g_kvl, g_kr, g_q, g_k = (x[...] for x in c)
        out = o[0]
        ones = _seg_ones(LANES)

        def wide_bwd(x, g, d, n):
            w_ = n // LANES
            rr = _wide(_rms_r(_seg_sum(_fold(x * x), ones), n), w_)
            dot = _wide(_seg_sum(_fold(d * g * x), ones) * (1.0 / n), w_)
            return _rms_bwd_b(x, rr, g, d, dot).astype(BF16), _csum(d * x * rr)

        cq = p[:, P_CQ:P_CKV]
        out[:, P_CQ:P_CKV], a_ql = wide_bwd(cq, g_ql, r[0][...], Q_LORA)
        ckv = p[:, P_CKV:P_KR]
        out[:, P_CKV:P_KR], a_kvl = wide_bwd(ckv, g_kvl, r[1][...], KV_LORA)
        kr = p[:, P_KR:P_QG]
        rr = _rms_r(_seg_sum(kr * kr, ones), MLA_ROPE)
        d = r[2][...]
        d = d + pltpu.roll(d, 64, 1)
        d = _rope_bwd(d, cmv, smv, 64, 32)
        low = _lane(d.shape) < 64
        dot = _seg_sum(jnp.where(low, d * g_kr * kr, 0.0), ones) * (1.0 / MLA_ROPE)
        out[:, P_KR:P_QG] = jnp.where(low, _rms_bwd_b(kr, rr, g_kr, d, dot), 0.0).astype(BF16)
        a_kr = _csum(d * kr * rr)
        a_q = jnp.zeros((1, HD), F32)

        def head_bwd(xh, g, d):
            rr = _rms_r(_seg_sum(xh * xh, ones), HD)
            dot = _seg_sum(d * g * xh, ones) * (1.0 / HD)
            return _rms_bwd_b(xh, rr, g, d, dot).astype(BF16), _csum(d * xh * rr)

        for h in range(HEADS):
            xh = p[:, P_QG + HD * h:P_QG + HD * (h + 1)]
            d = _rope_bwd(r[3][h] * GQA_SCALE, cav, sav, 64, 32)
            out[:, P_QG + HD * h:P_QG + HD * (h + 1)], inc = head_bwd(xh, g_q, d)
            a_q = a_q + inc
        a_k = jnp.zeros((1, HD), F32)
        grp = HEADS // GQA_KV
        for h in range(GQA_KV):
            xh = p[:, P_KG + HD * h:P_KG + HD * (h + 1)]
            d = r[4][grp * h]
            dv = r[5][grp * h]
            for e in range(1, grp):
                d = d + r[4][grp * h + e]
                dv = dv + r[5][grp * h + e]
            d = _rope_bwd(d, cav, sav, 64, 32)
            out[:, P_KG + HD * h:P_KG + HD * (h + 1)], inc = head_bwd(xh, g_k, d)
            out[:, P_VG + HD * h:P_VG + HD * (h + 1)] = dv.astype(BF16)
            a_k = a_k + inc
        return (a_ql, a_kvl, a_kr, a_q, a_k)

    accs = [_sds((1, Q_LORA), F32), _sds((1, KV_LORA), F32), _sds((1, LANES), F32),
            _sds((1, HD), F32), _sds((1, HD), F32)]
    return _rowwise(name, fn, [dcqn, dckvn, dkrd, dqg, dkg, dvg, proj, cm, sm, ca, sa], gains,
                    [_sds((S, P_END), BF16)], accs, after=after)


def _mla_q_parts(qa, h, rp, ones):
    nope = qa[:, HD * h:HD * (h + 1)]
    mine = (_lane(rp.shape) >= 64) == bool(h % 2)
    rpm = jnp.where(mine, rp, 0.0)
    rr = _rms_r(_seg_sum(nope * nope + rpm * rpm, ones), MLA_QK)
    return nope, rpm, mine, rr


def _mla_prep(name, qa, kv, krd, tabs, gains):
    S = qa.shape[0]
    cm, sm = tabs

    def fn(r, c, o):
        qa_r, kv_r, krd_v, cmv, smv = r[0], r[1], r[2][...], r[3][...], r[4][...]
        g_n, g_r, g_kn = (x[...] for x in c)
        ones = _seg_ones(LANES)
        for h in range(HEADS):
            rp = qa_r[:, HEADS * HD + LANES * (h // 2):HEADS * HD + LANES * (h // 2 + 1)]
            nope, rpm, mine, rr = _mla_q_parts(qa_r, h, rp, ones)
            o[0][h, :, 0:HD] = (nope * (rr * MLA_SCALE) * g_n).astype(BF16)
            o[0][h, :, HD:2 * HD] = _rope(rpm * (rr * MLA_SCALE) * g_r, cmv, smv, 64, 32).astype(BF16)
            kn = kv_r[:, 2 * HD * h:2 * HD * h + HD]
            o[1][h, :, 0:HD] = (kn * _rms_r(_seg_sum(kn * kn, ones), HD) * g_kn).astype(BF16)
            o[1][h, :, HD:2 * HD] = krd_v
            o[2][h] = kv_r[:, 2 * HD * h + HD:2 * HD * (h + 1)].astype(BF16)
        return ()

    outs = [_sds((HEADS, S, 2 * HD), BF16), _sds((HEADS, S, 2 * HD), BF16), _sds((HEADS, S, HD), BF16)]
    return _rowwise(name, fn, [qa, kv, krd, cm, sm], gains, outs, tm=2 * ROW_TILE)


def _mla_prep_bwd(name, dq, dk, dv, qa, kv, tabs, gains):
    S = qa.shape[0]
    cm, sm = tabs

    def fn(r, c, o):
        dq_r, dk_r, dv_r, qa_r, kv_r, cmv, smv = r[0], r[1], r[2], r[3], r[4], r[5][...], r[6][...]
        g_n, g_r, g_kn = (x[...] for x in c)
        a_n = jnp.zeros((1, HD), F32)
        a_r = jnp.zeros((1, LANES), F32)
        a_kn = jnp.zeros((1, HD), F32)
        dkrd = jnp.zeros(cmv.shape, F32)
        drp = None
        ones = _seg_ones(LANES)
        for h in range(HEADS):
            rp = qa_r[:, HEADS * HD + LANES * (h // 2):HEADS * HD + LANES * (h // 2 + 1)]
            nope, rpm, mine, rr = _mla_q_parts(qa_r, h, rp, ones)
            dn = dq_r[h, :, 0:HD] * MLA_SCALE
            dr = _rope_bwd(jnp.where(mine, dq_r[h, :, HD:2 * HD] * MLA_SCALE, 0.0), cmv, smv, 64, 32)
            dot = _seg_sum(dn * g_n * nope + dr * g_r * rpm, ones) * (1.0 / MLA_QK)
            r3 = rr * rr * rr
            o[0][:, HD * h:HD * (h + 1)] = (rr * dn * g_n - nope * r3 * dot).astype(BF16)
            part = jnp.where(mine, rr * dr * g_r - rpm * r3 * dot, 0.0)
            drp = part if h % 2 == 0 else drp + part
            if h % 2 == 1:
                o[0][:, HEADS * HD + LANES * (h // 2):HEADS * HD + LANES * (h // 2 + 1)] = drp.astype(BF16)
            a_n = a_n + _csum(dn * nope * rr)
            a_r = a_r + _csum(dr * rpm * rr)
            kn = kv_r[:, 2 * HD * h:2 * HD * h + HD]
            rk = _rms_r(_seg_sum(kn * kn, ones), HD)
            dkn = dk_r[h, :, 0:HD]
            dotk = _seg_sum(dkn * g_kn * kn, ones) * (1.0 / HD)
            o[1][:, 2 * HD * h:2 * HD * h + HD] = _rms_bwd_b(kn, rk, g_kn, dkn, dotk).astype(BF16)
            o[1][:, 2 * HD * h + HD:2 * HD * (h + 1)] = dv_r[h].astype(BF16)
            a_kn = a_kn + _csum(dkn * kn * rk)
            dkrd = dkrd + dk_r[h, :, HD:2 * HD]
        o[2][...] = dkrd
        return (a_n, a_r, a_kn)

    outs = [_sds(qa.shape, BF16), _sds(kv.shape, BF16), _sds((S, LANES), F32)]
    accs = [_sds((1, HD), F32), _sds((1, LANES), F32), _sds((1, HD), F32)]
    return _rowwise(name, fn, [dq, dk, dv, qa, kv, cm, sm], gains, outs, accs)


def _seg_ones(seg):
    r = lax.broadcasted_iota(jnp.int32, (LANES, LANES), 0) // seg
    c = lax.broadcasted_iota(jnp.int32, (LANES, LANES), 1) // seg
    return (r == c).astype(F32)


def _dot01(x, sel, dims):
    hi = x.astype(BF16)
    lo = (x - hi.astype(F32)).astype(BF16)
    sb = sel.astype(BF16)
    if dims is _NN:
        return (lax.dot_general(hi, sb, dims, preferred_element_type=F32)
                + lax.dot_general(lo, sb, dims, preferred_element_type=F32))
    return (lax.dot_general(sb, hi, dims, preferred_element_type=F32)
            + lax.dot_general(sb, lo, dims, preferred_element_type=F32))


def _seg_sum(x, ones):
    return _dot01(x, ones, _NN)


def _fold(x):
    acc = x[:, 0:LANES]
    for g in range(1, x.shape[1] // LANES):
        acc = acc + x[:, LANES * g:LANES * (g + 1)]
    return acc


def _wide(v, n):
    return jnp.concatenate([v] * n, axis=1) if n > 1 else v


def _rms_bwd_b(x, r, g, dy, dot_b):
    dyg = dy * g
    return r * dyg - x * (r * r * r) * dot_b


def _seg64_r(x, ones):
    return _rms_r(_seg_sum(x * x, ones), SWA_D)


def _swa_prep(name, qkv, tabs, gains):
    S = qkv.shape[0]
    nq, nk = SWA_HEADS * SWA_D, SWA_KV * SWA_D
    cs, ss = tabs

    def fn(r, c, o):
        x_r, csv, ssv = r[0], r[1][...], r[2][...]
        g_q, g_k = c[0][...], c[1][...]
        ones = _seg_ones(SWA_D)
        for g in range((nq + nk) // LANES):
            x = x_r[:, LANES * g:LANES * (g + 1)]
            rr = _seg64_r(x, ones)
            y = _rope(x * rr * (g_q if g < nq // LANES else g_k), csv, ssv, SWA_D, SWA_ROT // 2)
            if g < nq // LANES:
                o[0][:, LANES * g:LANES * (g + 1)] = (y * SWA_SCALE).astype(BF16)
            else:
                o[1][:, LANES * g - nq:LANES * (g + 1) - nq] = y.astype(BF16)
        o[2][...] = x_r[:, nq + nk:nq + 2 * nk].astype(BF16)
        return ()

    outs = [_sds((S, nq), BF16), _sds((S, nk), BF16), _sds((S, nk), BF16)]
    return _rowwise(name, fn, [qkv, cs, ss], gains, outs, tm=2 * ROW_TILE)


def _swa_prep_bwd(name, dq, dk, dv, qkv, tabs, gains):
    nq, nk = SWA_HEADS * SWA_D, SWA_KV * SWA_D
    cs, ss = tabs

    def fn(r, c, o):
        dq_r, dk_r, x_r, csv, ssv = r[0], r[1], r[3], r[4][...], r[5][...]
        g_q, g_k = c[0][...], c[1][...]
        acc = [jnp.zeros((1, LANES), F32), jnp.zeros((1, LANES), F32)]
        ones = _seg_ones(SWA_D)
        for g in range((nq + nk) // LANES):
            isq = g < nq // LANES
            x = x_r[:, LANES * g:LANES * (g + 1)]
            rr = _seg64_r(x, ones)
            d = dq_r[:, LANES * g:LANES * (g + 1)] * SWA_SCALE if isq else dk_r[:, LANES * g - nq:LANES * (g + 1) - nq]
            d = _rope_bwd(d, csv, ssv, SWA_D, SWA_ROT // 2)
            dyg = d * (g_q if isq else g_k)
            dot = _seg_sum(dyg * x, ones) * (1.0 / SWA_D)
            o[0][:, LANES * g:LANES * (g + 1)] = (rr * dyg - x * (rr * rr * rr) * dot).astype(BF16)
            acc[0 if isq else 1] = acc[0 if isq else 1] + _csum(d * x * rr)
        o[0][:, nq + nk:nq + 2 * nk] = r[2][...].astype(BF16)
        return tuple(acc)

    return _rowwise(name, fn, [dq, dk, dv, qkv, cs, ss], gains, [_sds(qkv.shape, BF16)],
                    [_sds((1, LANES), F32), _sds((1, LANES), F32)])


def _delta(name, do, o_, after=()):
    S, C = do.shape

    def fn(r, c, o):
        for g in range(C // LANES):
            t = r[0][:, LANES * g:LANES * (g + 1)].astype(F32) * r[1][:, LANES * g:LANES * (g + 1)].astype(F32)
            o[0][g] = _rsum(t)
        return ()

    return _rowwise(name, fn, [do, o_], [], [_sds((C // LANES, S, 1), F32)], tm=4 * ROW_TILE, after=after)[0]


def _delta_t(name, do, o_, after=()):
    S, C = do.shape

    def fn(r, c, o):
        row = lax.broadcasted_iota(jnp.int32, (8, LANES), 0)
        sel = (row == _lane((8, LANES)) // SWA_D).astype(F32)
        for g in range(C // LANES):
            t = r[0][:, LANES * g:LANES * (g + 1)].astype(F32) * r[1][:, LANES * g:LANES * (g + 1)].astype(F32)
            both = _dot01(t, sel, _NT)
            o[0][2 * g] = both[0:1, :]
            o[0][2 * g + 1] = both[1:2, :]
        return ()

    return _rowwise(name, fn, [do, o_], [], [("T", _sds((2 * C // LANES, 1, S), F32))], tm=4 * ROW_TILE,
                    after=after)[0]


_NT = (((1,), (1,)), ((), ()))
_NN = (((1,), (0,)), ((), ()))
_TN = (((0,), (0,)), ((), ()))


def _flash_fwd(name, q, k, v, tq=FWD_TILE[0], tk=FWD_TILE[1]):
    H, S, dk = q.shape
    G = H // k.shape[0]
    dv = v.shape[2]
    tq, tk = _tile(S, (tq, 256, 128)), _tile(S, (tk, 256, 128))
    nk = S // tk

    def body(q_ref, k_ref, v_ref, o_ref, lse_ref, m_s, l_s, acc_s):
        j = pl.program_id(2)

        @pl.when(j == 0)
        def _():
            m_s[...] = jnp.full_like(m_s, -jnp.inf)
            l_s[...] = jnp.zeros_like(l_s)
            acc_s[...] = jnp.zeros_like(acc_s)

        s = lax.dot_general(q_ref[...], k_ref[...], _NT, preferred_element_type=F32)
        m_new = jnp.maximum(m_s[...], jnp.max(s, axis=-1, keepdims=True))
        alpha = jnp.exp(m_s[...] - m_new)
        p = jnp.exp(s - m_new)
        l_s[...] = alpha * l_s[...] + _rsum(p)
        acc_s[...] = alpha * acc_s[...] + lax.dot_general(p.astype(BF16), v_ref[...], _NN,
                                                          preferred_element_type=F32)
        m_s[...] = m_new

        @pl.when(j == nk - 1)
        def _():
            o_ref[...] = (acc_s[...] / l_s[...]).astype(o_ref.dtype)
            lse_ref[...] = m_s[...] + jnp.log(l_s[...])

    return pl.pallas_call(
        body, name=name, grid=(H, S // tq, nk),
        in_specs=[pl.BlockSpec((None, tq, dk), lambda h, i, j: (h, i, 0)),
                  pl.BlockSpec((None, tk, dk), lambda h, i, j: (h // G, j, 0)),
                  pl.BlockSpec((None, tk, dv), lambda h, i, j: (h // G, j, 0))],
        out_specs=[pl.BlockSpec((tq, dv), lambda h, i, j: (i, h)),
                   pl.BlockSpec((None, tq, 1), lambda h, i, j: (h, i, 0))],
        out_shape=[_sds((S, H * dv), BF16), _sds((H, S, 1), F32)],
        scratch_shapes=[pltpu.VMEM((tq, 1), F32), pltpu.VMEM((tq, 1), F32), pltpu.VMEM((tq, dv), F32)],
        compiler_params=_params(("parallel", "parallel", "arbitrary")),
    )(q, k, v)


def _flash_bwd(name, q, k, v, do, head0, lse, delta, tq=BWD_TILE[0], tk=BWD_TILE[1]):
    H, S, dk = q.shape
    G = H // k.shape[0]
    dv = v.shape[2]
    tq, tk = _tile(S, (tq, 256, 128)), _tile(S, (tk, 256, 128))
    nq = S // tq

    def body(q_ref, k_ref, v_ref, do_ref, lse_ref, dl_ref, dq_ref, dk_ref, dv_ref, dk_s, dv_s):
        j, i = pl.program_id(1), pl.program_id(2)

        @pl.when(i == 0)
        def _():
            dk_s[...] = jnp.zeros_like(dk_s)
            dv_s[...] = jnp.zeros_like(dv_s)

        qv, kv_, dov = q_ref[...], k_ref[...], do_ref[...]
        s = lax.dot_general(qv, kv_, _NT, preferred_element_type=F32)
        p = jnp.exp(s - lse_ref[...])
        dp = lax.dot_general(dov, v_ref[...], _NT, preferred_element_type=F32)
        ds = (p * (dp - dl_ref[...])).astype(BF16)
        dv_s[...] += lax.dot_general(p.astype(BF16), dov, _TN, preferred_element_type=F32)
        dk_s[...] += lax.dot_general(ds, qv, _TN, preferred_element_type=F32)
        dqi = lax.dot_general(ds, kv_, _NN, preferred_element_type=F32)
        rows = pl.ds(pl.multiple_of(i * tq, tq), tq)

        @pl.when(j == 0)
        def _():
            dq_ref[rows, :] = dqi

        @pl.when(j > 0)
        def _():
            dq_ref[rows, :] += dqi

        @pl.when(i == nq - 1)
        def _():
            dk_ref[...] = dk_s[...]
            dv_ref[...] = dv_s[...]

    return pl.pallas_call(
        body, name=name, grid=(H, S // tk, nq),
        in_specs=[pl.BlockSpec((None, tq, dk), lambda h, j, i: (h, i, 0)),
                  pl.BlockSpec((None, tk, dk), lambda h, j, i: (h // G, j, 0)),
                  pl.BlockSpec((None, tk, dv), lambda h, j, i: (h // G, j, 0)),
                  pl.BlockSpec((tq, dv), lambda h, j, i: (i, head0 + h)),
                  pl.BlockSpec((None, tq, 1), lambda h, j, i: (h, i, 0)),
                  pl.BlockSpec((None, tq, 1), lambda h, j, i: (head0 + h, i, 0))],
        out_specs=[pl.BlockSpec((None, S, dk), lambda h, j, i: (h, 0, 0)),
                   pl.BlockSpec((None, tk, dk), lambda h, j, i: (h, j, 0)),
                   pl.BlockSpec((None, tk, dv), lambda h, j, i: (h, j, 0))],
        out_shape=[_sds((H, S, dk), F32), _sds((H, S, dk), F32), _sds((H, S, dv), F32)],
        scratch_shapes=[pltpu.VMEM((tk, dk), F32), pltpu.VMEM((tk, dv), F32)],
        compiler_params=_params(("parallel", "arbitrary", "arbitrary")),
    )(q, k, v, do, lse, delta)


def _swa_place(ref128, h):
    e, t = h % 2, (h // (SWA_HEADS // SWA_KV)) % 2
    x = ref128.astype(F32)
    if e != t:
        x = pltpu.roll(x, 64, 1)
    return jnp.where((_lane(x.shape) >= 64) == bool(t), x, 0.0).astype(BF16)


def _swa_unplace(y, h):
    e, t = h % 2, (h // (SWA_HEADS // SWA_KV)) % 2
    return pltpu.roll(y, 64, 1) if e != t else y


def _swa_specs(width, nb):
    prev = pl.BlockSpec((SWA_BLOCK, width), lambda i: (jnp.maximum(i - 1, 0), 0))
    cur = pl.BlockSpec((SWA_BLOCK, width), lambda i: (i, 0))
    nxt = pl.BlockSpec((SWA_BLOCK, width), lambda i: (jnp.minimum(i + 1, nb - 1), 0))
    return [prev, cur, nxt]


def _swa_bias_t(i, S):
    shape = (3 * SWA_BLOCK, SWA_BLOCK)
    kpos = (i - 1) * SWA_BLOCK + lax.broadcasted_iota(jnp.int32, shape, 0)
    qpos = i * SWA_BLOCK + lax.broadcasted_iota(jnp.int32, shape, 1)
    ok = (jnp.abs(qpos - kpos) <= SWA_WINDOW) & (kpos >= 0) & (kpos < S)
    return jnp.where(ok, 0.0, -jnp.inf)


def _add_blocks(x, blocks):
    n = x.shape[1] // LANES
    return jnp.concatenate([x[:, LANES * b:LANES * (b + 1)] + blocks[b % len(blocks)] for b in range(n)], axis=1)


def _swa_stack(refs, heads):
    return jnp.concatenate([_swa_place(r[:, LANES * (h // 2):LANES * (h // 2 + 1)], h) for h in heads for r in refs],
                           axis=0)


def _swa_unstack_t(ot, j, out_ref):
    grp = SWA_HEADS // SWA_KV
    for pair in range(grp // 2):
        h = grp * j + 2 * pair
        a = _swa_unplace(ot[:, SWA_BLOCK * 2 * pair:SWA_BLOCK * (2 * pair + 1)].T, h)
        b = _swa_unplace(ot[:, SWA_BLOCK * (2 * pair + 1):SWA_BLOCK * (2 * pair + 2)].T, h + 1)
        out_ref[:, LANES * (h // 2):LANES * (h // 2 + 1)] = jnp.where(_lane(a.shape) < 64, a, b).astype(out_ref.dtype)


def _swa_fwd(name, q, k, v, sink):
    S = q.shape[0]
    nb = S // SWA_BLOCK
    grp = SWA_HEADS // SWA_KV
    smem = pl.BlockSpec(memory_space=pltpu.SMEM)

    def body(sink_ref, q_ref, kp, kc, kn, vp, vc, vn, o_ref, lse_ref):
        i = pl.program_id(0)
        bias = [_swa_bias_t(i, S)]
        kcat = [jnp.concatenate([r[:, LANES * u:LANES * (u + 1)] for r in (kp, kc, kn)], axis=0) for u in range(2)]
        vcat = [jnp.concatenate([r[:, LANES * u:LANES * (u + 1)] for r in (vp, vc, vn)], axis=0) for u in range(2)]
        for j in range(SWA_KV):
            heads = range(grp * j, grp * (j + 1))
            xq = _swa_stack([q_ref], heads)
            sk = jnp.concatenate([jnp.full((1, SWA_BLOCK), sink_ref[0, h], F32) for h in heads], axis=1)
            st = lax.dot_general(kcat[j // 2], xq, _NT, preferred_element_type=F32)
            st = _add_blocks(st, bias)
            m = jnp.maximum(jnp.max(st, axis=0, keepdims=True), sk)
            pt = jnp.exp(st - m)
            den = jnp.sum(pt, axis=0, keepdims=True) + jnp.exp(sk - m)
            ot = lax.dot_general(vcat[j // 2], pt.astype(BF16), _TN, preferred_element_type=F32) * (1.0 / den)
            lse = m + jnp.log(den)
            for e, h in enumerate(heads):
                lse_ref[h] = lse[:, SWA_BLOCK * e:SWA_BLOCK * (e + 1)]
            _swa_unstack_t(ot, j, o_ref)

    return pl.pallas_call(
        body, name=name, grid=(nb,),
        in_specs=[smem, pl.BlockSpec((SWA_BLOCK, q.shape[1]), lambda i: (i, 0))]
        + _swa_specs(k.shape[1], nb) + _swa_specs(v.shape[1], nb),
        out_specs=[pl.BlockSpec((SWA_BLOCK, q.shape[1]), lambda i: (i, 0)),
                   pl.BlockSpec((SWA_HEADS, 1, SWA_BLOCK), lambda i: (0, 0, i))],
        out_shape=[_sds(q.shape, BF16), _sds((SWA_HEADS, 1, S), F32)],
        compiler_params=_params(("parallel",)),
    )(sink, q, k, k, k, v, v, v)


def _swa_bwd_q(name, q, k, v, do, lse_t, delta_t, sink):
    S = q.shape[0]
    nb = S // SWA_BLOCK
    grp = SWA_HEADS // SWA_KV
    smem = pl.BlockSpec(memory_space=pltpu.SMEM)
    row = pl.BlockSpec((SWA_BLOCK, q.shape[1]), lambda i: (i, 0))
    hrow = pl.BlockSpec((SWA_HEADS, 1, SWA_BLOCK), lambda i: (0, 0, i))

    def body(sink_ref, q_ref, do_ref, lse_ref, dl_ref, kp, kc, kn, vp, vc, vn, dq_ref, ds_ref):
        i = pl.program_id(0)

        @pl.when(i == 0)
        def _():
            ds_ref[...] = jnp.zeros_like(ds_ref)

        bias = [_swa_bias_t(i, S)]
        kcat = [jnp.concatenate([r[:, LANES * u:LANES * (u + 1)] for r in (kp, kc, kn)], axis=0) for u in range(2)]
        vcat = [jnp.concatenate([r[:, LANES * u:LANES * (u + 1)] for r in (vp, vc, vn)], axis=0) for u in range(2)]
        for j in range(SWA_KV):
            heads = range(grp * j, grp * (j + 1))
            xq = _swa_stack([q_ref], heads)
            xdo = _swa_stack([do_ref], heads)
            lse_r = jnp.concatenate([lse_ref[h] for h in heads], axis=1)
            dl_r = jnp.concatenate([dl_ref[h] for h in heads], axis=1)
            st = lax.dot_general(kcat[j // 2], xq, _NT, preferred_element_type=F32)
            pt = jnp.exp(_add_blocks(st, bias) - lse_r)
            dpt = lax.dot_general(vcat[j // 2], xdo, _NT, preferred_element_type=F32)
            dst = (pt * (dpt - dl_r)).astype(BF16)
            _swa_unstack_t(lax.dot_general(kcat[j // 2], dst, _TN, preferred_element_type=F32), j, dq_ref)
            for h in heads:
                dsink = -_rsum(jnp.exp(sink_ref[0, h] - lse_ref[h]) * dl_ref[h])
                ds_ref[h:h + 1, :] += jnp.broadcast_to(dsink, (1, LANES))

    return pl.pallas_call(
        body, name=name, grid=(nb,),
        in_specs=[smem, row, row, hrow, hrow] + _swa_specs(k.shape[1], nb) + _swa_specs(v.shape[1], nb),
        out_specs=[row, pl.BlockSpec((SWA_HEADS, LANES), lambda i: (0, 0))],
        out_shape=[_sds(q.shape, F32), _sds((SWA_HEADS, LANES), F32)],
        compiler_params=_params(("arbitrary",)),
    )(sink, q, do, lse_t, delta_t, k, k, k, v, v, v)


def _swa_bwd_kv(name, q, k, v, do, lse_t, delta_t):
    S = q.shape[0]
    nb = S // SWA_BLOCK
    grp = SWA_HEADS // SWA_KV
    krow = pl.BlockSpec((SWA_BLOCK, k.shape[1]), lambda i: (i, 0))

    def stat3():
        return [pl.BlockSpec((SWA_HEADS, 1, SWA_BLOCK), lambda i: (0, 0, jnp.maximum(i - 1, 0))),
                pl.BlockSpec((SWA_HEADS, 1, SWA_BLOCK), lambda i: (0, 0, i)),
                pl.BlockSpec((SWA_HEADS, 1, SWA_BLOCK), lambda i: (0, 0, jnp.minimum(i + 1, nb - 1)))]

    def body(qp, qc, qn, dop, doc, don, lp, lc, ln, dp_, dc_, dn_, k_ref, v_ref, dk_ref, dv_ref):
        j = pl.program_id(0)
        nh = 2 * grp
        kpos = j * SWA_BLOCK + lax.broadcasted_iota(jnp.int32, (SWA_BLOCK, SWA_BLOCK), 0)
        bias = []
        for b in range(3):
            qpos = (j - 1 + b) * SWA_BLOCK + lax.broadcasted_iota(jnp.int32, (SWA_BLOCK, SWA_BLOCK), 1)
            ok = (jnp.abs(qpos - kpos) <= SWA_WINDOW) & (qpos >= 0) & (qpos < S)
            bias.append(jnp.where(ok, 0.0, -jnp.inf))
        for u in range(SWA_KV // 2):
            heads = range(nh * u, nh * (u + 1))
            kc = k_ref[:, LANES * u:LANES * (u + 1)]
            vc = v_ref[:, LANES * u:LANES * (u + 1)]
            xq = _swa_stack([qp, qc, qn], heads)
            xdo = _swa_stack([dop, doc, don], heads)
            lse_r = jnp.concatenate([r[h] for h in heads for r in (lp, lc, ln)], axis=1)
            dl_r = jnp.concatenate([r[h] for h in heads for r in (dp_, dc_, dn_)], axis=1)
            st = lax.dot_general(kc, xq, _NT, preferred_element_type=F32)
            pt = jnp.exp(_add_blocks(st, bias) - lse_r)
            dpt = lax.dot_general(vc, xdo, _NT, preferred_element_type=F32)
            dst = (pt * (dpt - dl_r)).astype(BF16)
            dv_ref[:, LANES * u:LANES * (u + 1)] = lax.dot_general(pt.astype(BF16), xdo, _NN, preferred_element_type=F32)
            dk_ref[:, LANES * u:LANES * (u + 1)] = lax.dot_general(dst, xq, _NN, preferred_element_type=F32)

    return pl.pallas_call(
        body, name=name, grid=(nb,),
        in_specs=_swa_specs(q.shape[1], nb) + _swa_specs(do.shape[1], nb) + stat3() + stat3() + [krow, krow],
        out_specs=[krow, krow],
        out_shape=[_sds(k.shape, F32), _sds(v.shape, F32)],
        compiler_params=_params(("parallel",)),
    )(q, q, q, do, do, do, lse_t, lse_t, lse_t, delta_t, delta_t, delta_t, k, v)


def _peer_of(x, y, c, kk):
    return (x ^ ((kk >> 2) & 1), y ^ ((kk >> 1) & 1), c ^ (kk & 1))


def _own_slot(a, scatter):
    me = 4 * lax.axis_index("x") + 2 * lax.axis_index("y") + lax.axis_index("c")
    shape = a.shape if scatter else (N_DEV,) + a.shape
    own = lax.dynamic_slice_in_dim(a, me, 1, 0) if scatter else a[None]
    return lax.dynamic_update_slice_in_dim(lax.empty(shape, a.dtype), own, me, 0)


def _exchange_start(name, tensors, lands=None):
    n = len(tensors)
    hbm = pl.BlockSpec(memory_space=pltpu.HBM)
    sem = pl.BlockSpec(memory_space=pltpu.SEMAPHORE)
    srcs = [pltpu.with_memory_space_constraint(a, pltpu.HBM) for a, _ in tensors]
    lands = [_own_slot(a, sc) if z is None else z for (a, sc), z in zip(tensors, lands or [None] * n)]
    lands = [pltpu.with_memory_space_constraint(z, pltpu.HBM) for z in lands]

    def body(*refs):
        ins, dst = refs[:n], refs[n:2 * n]
        send, recv, token = refs[2 * n], refs[2 * n + 1], refs[4 * n + 2]
        x, y, c = lax.axis_index("x"), lax.axis_index("y"), lax.axis_index("c")
        me = 4 * x + 2 * y + c
        for t in range(n):
            for kk in range(1, N_DEV):
                px, py, pc = _peer_of(x, y, c, kk)
                src = ins[t].at[4 * px + 2 * py + pc] if tensors[t][1] else ins[t]
                k1 = t * (N_DEV - 1) + kk - 1
                pltpu.make_async_remote_copy(src_ref=src, dst_ref=dst[t].at[me], send_sem=send.at[k1],
                                             recv_sem=recv.at[k1], device_id=(px, py, pc),
                                             device_id_type=pl.DeviceIdType.MESH).start()
        token[...] = jnp.zeros_like(token)

    out = pl.pallas_call(
        body, name=name,
        in_specs=[hbm] * (2 * n),
        out_specs=[sem, sem] + [hbm] * (2 * n) + [pl.BlockSpec(memory_space=pltpu.VMEM)],
        out_shape=[pltpu.SemaphoreType.DMA((n * (N_DEV - 1),)), pltpu.SemaphoreType.DMA((n * (N_DEV - 1),))]
        + [pltpu.HBM(a.shape, a.dtype) for a in srcs] + [pltpu.HBM(a.shape, a.dtype) for a in lands]
        + [_sds((8, LANES), F32)],
        input_output_aliases={i: i + 2 for i in range(2 * n)},
        compiler_params=pltpu.CompilerParams(has_side_effects=pltpu.SideEffectType.DATAFLOW_SIDE_EFFECTING),
    )(*srcs, *lands)
    return (out[0], out[1], out[2:2 + n], out[2 + n:2 + 2 * n], [sc for _, sc in tensors]), out[2 + 2 * n]


def _exchange_wait(name, started, after):
    send_s, recv_s, srcs, lands, flags = started
    n = len(srcs)
    after = list(after) if isinstance(after, (list, tuple)) else [after]
    hbm = pl.BlockSpec(memory_space=pltpu.HBM)
    sem = pl.BlockSpec(memory_space=pltpu.SEMAPHORE)

    def body(*refs):
        ins, dst, send, recv = refs[:n], refs[n:2 * n], refs[2 * n], refs[2 * n + 1]
        x, y, c = lax.axis_index("x"), lax.axis_index("y"), lax.axis_index("c")
        me = 4 * x + 2 * y + c
        for t in range(n):
            for kk in range(1, N_DEV):
                px, py, pc = _peer_of(x, y, c, kk)
                src = ins[t].at[me] if flags[t] else ins[t]
                k1 = t * (N_DEV - 1) + kk - 1
                cp = pltpu.make_async_remote_copy(src_ref=src, dst_ref=dst[t].at[4 * px + 2 * py + pc],
                                                  send_sem=send.at[k1], recv_sem=recv.at[k1],
                                                  device_id=(px, py, pc), device_id_type=pl.DeviceIdType.MESH)
                cp.wait_send()
                cp.wait_recv()

    out = pl.pallas_call(
        body, name=name,
        in_specs=[hbm] * (2 * n) + [sem, sem] + [pl.BlockSpec(memory_space=pl.ANY)] * len(after),
        out_specs=[hbm] * (2 * n),
        out_shape=[pltpu.HBM(a.shape, a.dtype) for a in srcs] + [pltpu.HBM(a.shape, a.dtype) for a in lands],
        input_output_aliases={i: i for i in range(2 * n)},
        compiler_params=pltpu.CompilerParams(has_side_effects=pltpu.SideEffectType.DATAFLOW_SIDE_EFFECTING),
    )(*srcs, *lands, send_s, recv_s, *after)
    return out[n:]


def _sum8(name, parts):
    _, R, C = parts.shape

    def fn(r, c, o):
        acc = r[0][0].astype(F32)
        for s in range(1, N_DEV):
            acc = acc + r[0][s].astype(F32)
        o[0][...] = acc
        return ()

    tm = R if R % 256 else 256
    return _rowwise(name, fn, [parts], [], [_sds((R, C), F32)], tm=tm)[0]


def _adamw(name, w, g, m, v):
    bc1 = 1.0 - ADAM_B1 ** ADAM_STEP
    bc2 = 1.0 - ADAM_B2 ** ADAM_STEP

    def fn(r, c, o):
        wv, gv, mv, vv = (x[...] for x in r)
        mn = ADAM_B1 * mv + (1.0 - ADAM_B1) * gv
        vn = ADAM_B2 * vv + (1.0 - ADAM_B2) * (gv * gv)
        o[0][...] = -ADAM_LR * ((mn / bc1) / (jnp.sqrt(vn / bc2) + ADAM_EPS) + ADAM_WD * wv)
        o[1][...] = mn
        o[2][...] = vn
        return ()

    tm = max(8, min(512, 1 << ((16 << 20) // (56 * w.shape[1])).bit_length() - 1))
    return _rowwise(name, fn, [w, g, m, v], [], [_sds(w.shape, F32)] * 3, tm=tm)


def _rope_cs(pos, dim, theta):
    inv = jnp.float32(theta) ** (-jnp.arange(0, dim, 2, dtype=jnp.float32) / dim)
    ang = pos.astype(jnp.float32)[:, None] * inv[None, :]
    return jnp.cos(ang), jnp.sin(ang)


def _tables(S, zero):
    pos = jnp.arange(S) + zero
    c, s = _rope_cs(pos, MLA_ROPE, ROPE_THETA)
    mla = (jnp.concatenate([c, c, c, c], 1), jnp.concatenate([-s, s, -s, s], 1))
    rc, rs = _rope_cs(pos // GRID_W, HD // 2, AXIAL_THETA)
    cc, cs = _rope_cs(pos % GRID_W, HD // 2, AXIAL_THETA)
    axial = (jnp.concatenate([rc, rc, cc, cc], 1), jnp.concatenate([-rs, rs, -cs, cs], 1))
    c, s = _rope_cs(pos, SWA_ROT, ROPE_THETA)
    one, zero = jnp.ones((S, SWA_D - SWA_ROT), F32), jnp.zeros((S, SWA_D - SWA_ROT), F32)
    swa = (jnp.concatenate([c, c, one, c, c, one], 1), jnp.concatenate([-s, s, zero, -s, s, zero], 1))
    return mla, axial, swa


_WEIGHTS = ['even_norm', 'even_w_in', 'mla_q_lat_norm', 'mla_kv_lat_norm', 'mla_w_uq', 'mla_w_ukv', 'mla_q_norm',
            'mla_k_nope_norm', 'mla_k_rope_norm', 'gqa_q_norm', 'gqa_k_norm', 'even_w_out', 'odd_norm', 'odd_w_qkv',
            'swa_q_norm', 'swa_k_norm', 'swa_sink', 'odd_w_out', 'mlp_norm', 'mlp_w_up', 'mlp_w_down']
_SMALL = ['even_norm', 'mla_q_lat_norm', 'mla_kv_lat_norm', 'mla_q_norm', 'mla_k_nope_norm', 'mla_k_rope_norm',
          'gqa_q_norm', 'gqa_k_norm', 'swa_q_norm', 'swa_k_norm', 'swa_sink', 'mlp_norm']


def _relu2(acc):
    rl = jnp.maximum(acc, 0.0)
    return rl * rl, rl


def _mul2(acc, rl):
    return (acc * (2.0 * rl.astype(F32)),)


def _add(acc, res):
    return (acc + res,)


def _step(x, tgt, w, m, v):
    S, D = x.shape
    nw = len(_WEIGHTS)
    bf = lambda a: a.astype(BF16)

    w_up_s, w_dn_s = w['mlp_w_up'], w['mlp_w_down']
    first, tok0 = _exchange_start("gather_first_start", [
        (bf(w['even_w_in'][0].T), False), (bf(w['mla_w_uq'][0].T), False), (bf(w['mla_w_ukv'][0].T), False)])
    h0 = _rmsnorm("even_norm", x, w['even_norm'], after=[tok0])
    tab_mla, tab_ax, tab_swa = _tables(S, tok0[0, 0].astype(jnp.int32))
    rest = [(bf(w['even_w_out'][0]), False), (bf(w['odd_w_qkv'][0].T), False), (bf(w['odd_w_out'][0]), False),
            (bf(w_up_s[0].T), False), (bf(w_up_s[1].T), False), (bf(w_dn_s[0]), False), (bf(w_dn_s[1]), False)]
    rest_lands = [_own_slot(a, sc) for a, sc in rest]
    gathered = _exchange_wait("gather_first_wait", first, [h0, *tab_mla, *tab_ax, *tab_swa, *rest_lands])
    zero = jnp.minimum(jnp.abs(gathered[2][0, 0:1, 0:1].astype(F32)), 0.0)
    later, tok = _exchange_start("gather_rest_start", rest + [(w['odd_norm'] + zero, False)], rest_lands + [None])
    flat = lambda a: a.reshape((a.shape[0] * a.shape[1],) + a.shape[2:])
    win_t, wuq_g, wukv_t = [flat(a) for a in gathered]
    win_t = jnp.concatenate([win_t[:KR_END], jnp.zeros((LANES - MLA_ROPE, D), BF16), win_t[KR_END:]], 0)
    wuq_g = gathered[1]
    wuq_t = jnp.concatenate([wuq_g[:, :HD].reshape(HEADS * HD, Q_LORA),
                             wuq_g[:, HD:].reshape(HEADS * MLA_ROPE, Q_LORA)], 0)

    z64 = jnp.zeros((1, 64), F32)
    qn = w['mla_q_norm']
    g_even = [w['mla_q_lat_norm'], w['mla_kv_lat_norm'], jnp.concatenate([w['mla_k_rope_norm'], z64], 1),
              w['gqa_q_norm'], w['gqa_k_norm']]
    g_mla = [qn[:, :HD], jnp.concatenate([qn[:, HD:], qn[:, HD:]], 1), w['mla_k_nope_norm']]
    g_swa = [jnp.concatenate([w['swa_q_norm']] * 2, 1), jnp.concatenate([w['swa_k_norm']] * 2, 1)]

    def loss_head(acc, res, t):
        e = acc + res - t
        return e * (1.0 / D), _csum(_rsum(e * e))

    def mlp_fwd(l, xin, last=False):
        hn = _rmsnorm(f"mlp{l}_norm", xin, w['mlp_norm'][l:l + 1])
        a, rl = _mm(f"mlp{l}_up", hn, wup_t[l], "nt", (BF16, BF16), epilogue=_relu2)
        if last:
            xout = _mm(f"mlp{l}_down", a, wdn[l], "nn", (F32,), epilogue=loss_head, extras=(xin, tgt),
                       sums=[_sds((1, 1), F32)])
        else:
            xout = _mm(f"mlp{l}_down", a, wdn[l], "nn", (F32,), epilogue=_add, extras=(xin,))
        return xout, (hn, a, rl)

    proj = _mm("even_in", h0, win_t, "nt", (F32,), tm=512, tn=P_END)
    cqn, ckvn, krd, qg, kg, vg = _even_prep("even_prep", proj, tab_mla + tab_ax, g_even, after=[tok])
    qa = _mm("mla_uq", cqn, wuq_t, "nt", (F32,), tn=wuq_t.shape[0])
    kv = _mm("mla_ukv", ckvn, wukv_t, "nt", (F32,))
    q_a, k_a, v_a = _mla_prep("mla_prep", qa, kv, krd, tab_mla, g_mla)
    o_a, lse_a = _flash_fwd("mla_attn", q_a, k_a, v_a)
    o_g, lse_g = _flash_fwd("gqa_attn", qg, kg, vg)
    merged = jnp.concatenate([o_a, o_g], 1)
    rest = _exchange_wait("gather_rest_wait", later, merged)
    wout_e, wqkv_t, wout_o, wup0_t, wup1_t, wdn0, wdn1 = [flat(a) for a in rest[:7]]
    odd_norm = rest[7].reshape(1, D)
    wup_t, wdn = (wup0_t, wup1_t), (wdn0, wdn1)
    x1 = _mm("even_out", merged, wout_e, "nn", (F32,), epilogue=_add, extras=(x,))
    x2, mlp0 = mlp_fwd(0, x1)

    h1 = _rmsnorm("odd_norm", x2, odd_norm)
    qkv = _mm("odd_qkv", h1, wqkv_t, "nt", (F32,), tn=wqkv_t.shape[0] // 2)
    q_s, k_s, v_s = _swa_prep("swa_prep", qkv, tab_swa, g_swa)
    o_s, lse_s = _swa_fwd("swa_attn", q_s, k_s, v_s, w['swa_sink'])
    x3 = _mm("odd_out", o_s, wout_o, "nn", (F32,), epilogue=_add, extras=(x2,))
    (dy, loss_acc), mlp1 = mlp_fwd(1, x3, last=True)
    loss = lax.psum(0.5 / D * loss_acc[0, 0], ("x", "y", "c"))

    gsm = {}

    def mlp_bwd(l, dout, xin, saved):
        hn, a, rl = saved
        du = _mm(f"mlp{l}_dact", dout, wdn[l], "nt", (BF16,), epilogue=_mul2, extras=(rl,))
        g_dn = _mm(f"mlp{l}_gdown", a, dout, "tn", (BF16,))
        g_up = _mm(f"mlp{l}_gup", du, hn, "tn", (BF16,))
        dhn = _mm(f"mlp{l}_dnorm", du, wup_t[l], "nn", (F32,))
        din, g_n = _rmsnorm_bwd(f"mlp{l}_norm_bwd", dout, xin, dhn, w['mlp_norm'][l:l + 1])
        return din, g_dn, g_up, g_n

    dx3, g_dn1, g_up1, g_mn1 = mlp_bwd(1, dy, x3, mlp1)
    split = lambda a: a.reshape((N_DEV, a.shape[0] // N_DEV) + a.shape[1:])
    sc1, tok1 = _exchange_start("scatter_mlp1_start", [(split(g_up1), True), (split(g_dn1), True)])

    g_wout_o = _mm("odd_gout", o_s, dx3, "tn", (BF16,))
    do_s = _mm("odd_dattn", dx3, wout_o, "nt", (BF16,))
    dl_s = _delta_t("swa_delta", do_s, o_s, after=[tok1])
    dq_s, dsink = _swa_bwd_q("swa_bwd_q", q_s, k_s, v_s, do_s, lse_s, dl_s, w['swa_sink'])
    dk_s, dv_s = _swa_bwd_kv("swa_bwd_kv", q_s, k_s, v_s, do_s, lse_s, dl_s)
    dqkv, g_sq, g_sk = _swa_prep_bwd("swa_prep_bwd", dq_s, dk_s, dv_s, qkv, tab_swa, g_swa)
    g_wqkv = _mm("odd_gqkv", dqkv, h1, "tn", (BF16,), tm=dqkv.shape[1] // 2)
    dh1 = _mm("odd_dnorm", dqkv, wqkv_t, "nn", (F32,), tk=dqkv.shape[1])
    dx2, g_on = _rmsnorm_bwd("odd_norm_bwd", dx3, x2, dh1, odd_norm)
    gsm['swa_q_norm'] = g_sq[:, :64] + g_sq[:, 64:]
    gsm['swa_k_norm'] = g_sk[:, :64] + g_sk[:, 64:]
    gsm['swa_sink'] = dsink[:, 0].reshape(1, SWA_HEADS)

    dx1, g_dn0, g_up0, g_mn0 = mlp_bwd(0, dx2, x1, mlp0)
    sc2, tok2 = _exchange_start("scatter_mid_start", [(split(g_wout_o), True), (split(g_wqkv), True),
                                                      (split(g_up0), True), (split(g_dn0), True)])
    gsm['mlp_norm'] = jnp.concatenate([g_mn0, g_mn1], 0)

    g_wout_e = _mm("even_gout", merged, dx1, "tn", (BF16,))
    dmerged = _mm("even_dattn", dx1, wout_e, "nt", (BF16,))
    dl_e = _delta("even_delta", dmerged, merged, after=[tok2])
    dq_a, dk_a, dv_a = _flash_bwd("mla_attn_bwd", q_a, k_a, v_a, dmerged, 0, lse_a, dl_e)
    dq_g, dk_g, dv_g = _flash_bwd("gqa_attn_bwd", qg, kg, vg, dmerged, HEADS, lse_g, dl_e)
    dqa, dkv, dkrd, g_qnn, g_qnr, g_kn = _mla_prep_bwd("mla_prep_bwd", dq_a, dk_a, dv_a, qa, kv, tab_mla, g_mla)
    g_wuq = _mm("mla_guq", dqa, cqn, "tn", (BF16,))
    dcqn = _mm("mla_dq_lat", dqa, wuq_t, "nn", (F32,))
    g_wukv = _mm("mla_gukv", dkv, ckvn, "tn", (BF16,))
    dckvn = _mm("mla_dkv_lat", dkv, wukv_t, "nn", (F32,))
    g_wuq = jnp.concatenate([g_wuq[:HEADS * HD].reshape(HEADS, HD, Q_LORA),
                             g_wuq[HEADS * HD:].reshape(HEADS, MLA_ROPE, Q_LORA)], 1)
    sc3, tok_e = _exchange_start("scatter_even_start", [(g_wuq, True), (split(g_wukv), True), (split(g_wout_e), True)])
    dproj, g_ql, g_kvl, g_kr, g_gq, g_gk = _even_prep_bwd("even_prep_bwd", dcqn, dckvn, dkrd, dq_g, dk_g, dv_g,
                                                          proj, tab_mla + tab_ax, g_even, after=[tok_e])
    g_win = _mm("even_gin", dproj, h0, "tn", (BF16,), tm=P_END, tk=1024)
    dh0 = _mm("even_dnorm", dproj, win_t, "nn", (F32,), tk=P_END)
    grad_x, g_en = _rmsnorm_bwd("even_norm_bwd", dx1, x, dh0, w['even_norm'])
    gsm.update(even_norm=g_en, mla_q_lat_norm=g_ql, mla_kv_lat_norm=g_kvl,
               mla_q_norm=jnp.concatenate([g_qnn, g_qnr[:, :64] + g_qnr[:, 64:]], 1), mla_k_nope_norm=g_kn,
               mla_k_rope_norm=g_kr[:, :64], gqa_q_norm=g_gq, gqa_k_norm=g_gk)

    g_win = jnp.concatenate([g_win[:KR_END], g_win[P_QG:]], 0)
    small_sizes = [w[n].size for n in _SMALL] + [D]
    small_vec = jnp.concatenate([gsm[n].reshape(1, -1) for n in _SMALL] + [g_on], 1)
    pad = (-small_vec.shape[1]) % LANES
    small_vec = jnp.pad(small_vec, ((0, 0), (0, pad)))
    last_st, tok3 = _exchange_start("scatter_last_start", [(split(g_win), True), (small_vec, False)])
    p_up1, p_dn1 = _exchange_wait("scatter_mlp1_wait", sc1, tok3)
    p_wout_o, p_wqkv, p_up0, p_dn0 = _exchange_wait("scatter_mid_wait", sc2, tok3)
    p_wuq, p_wukv, p_wout_e = _exchange_wait("scatter_even_wait", sc3, tok3)
    parts = dict(odd_w_qkv=p_wqkv, odd_w_out=p_wout_o, up0=p_up0, up1=p_up1, dn0=p_dn0, dn1=p_dn1,
                 mla_w_uq=p_wuq, mla_w_ukv=p_wukv, even_w_out=p_wout_e)
    red = {n: _sum8("sum_" + n, p) for n, p in parts.items()}
    grads = {
        'odd_w_qkv': red['odd_w_qkv'].T[None], 'odd_w_out': red['odd_w_out'][None],
        'mlp_w_up': jnp.stack([red['up0'].T, red['up1'].T]), 'mlp_w_down': jnp.stack([red['dn0'], red['dn1']]),
        'mla_w_uq': red['mla_w_uq'].T[None], 'mla_w_ukv': red['mla_w_ukv'].T[None],
        'even_w_out': red['even_w_out'][None],
    }
    delta, new_m, new_v = {}, {}, {}

    def apply(names):
        for n in names:
            shp = w[n].shape
            two = lambda a: a.reshape(shp[0] * shp[1], shp[2])
            d_, m_, v_ = _adamw("adamw_" + n, two(w[n]), two(grads[n]), two(m[n]), two(v[n]))
            delta[n], new_m[n], new_v[n] = d_.reshape(shp), m_.reshape(shp), v_.reshape(shp)

    apply(['mla_w_uq', 'mla_w_ukv', 'even_w_out', 'odd_w_qkv', 'odd_w_out', 'mlp_w_up', 'mlp_w_down'])
    last = _exchange_wait("scatter_last_wait", last_st, new_v['mlp_w_down'])
    small_g = _sum8("sum_small", last[1])
    grads['even_w_in'] = _sum8("sum_even_w_in", last[0]).T[None]
    apply(['even_w_in'])
    off = 0
    for n, sz in zip(_SMALL + ['odd_norm_full'], small_sizes):
        seg = small_g[:, off:off + sz]
        off += sz
        if n == 'odd_norm_full':
            me = 4 * lax.axis_index("x") + 2 * lax.axis_index("y") + lax.axis_index("c")
            grads['odd_norm'] = lax.dynamic_slice(seg, (0, me * (D // N_DEV)), (1, D // N_DEV))
        else:
            grads[n] = seg.reshape(w[n].shape)

    sm_names = _SMALL + ['odd_norm']
    pack = lambda d: jnp.concatenate([d[n].reshape(1, -1) for n in sm_names], 1)
    pw, pg, pm, pv = pack(w), pack(grads), pack(m), pack(v)
    padw = (-pw.shape[1]) % LANES
    padf = lambda a: jnp.pad(a, ((0, 0), (0, padw)))
    d_, m_, v_ = _adamw("adamw_small", padf(pw), padf(pg), padf(pm), jnp.pad(pv, ((0, 0), (0, padw)), constant_values=1.0))
    off = 0
    for n in sm_names:
        sz = w[n].size
        delta[n] = d_[:, off:off + sz].reshape(w[n].shape)
        new_m[n] = m_[:, off:off + sz].reshape(w[n].shape)
        new_v[n] = v_[:, off:off + sz].reshape(w[n].shape)
        off += sz

    return (loss, grad_x[None], *[grads[n] for n in _WEIGHTS], *[delta[n] for n in _WEIGHTS],
            *[new_m[n] for n in _WEIGHTS], *[new_v[n] for n in _WEIGHTS])


def kernel(x, even_norm, even_w_in, mla_q_lat_norm, mla_kv_lat_norm, mla_w_uq, mla_w_ukv, mla_q_norm, mla_k_nope_norm, mla_k_rope_norm, gqa_q_norm, gqa_k_norm, even_w_out, odd_norm, odd_w_qkv, swa_q_norm, swa_k_norm, swa_sink, odd_w_out, mlp_norm, mlp_w_up, mlp_w_down, loss_target, m_even_norm, m_even_w_in, m_mla_q_lat_norm, m_mla_kv_lat_norm, m_mla_w_uq, m_mla_w_ukv, m_mla_q_norm, m_mla_k_nope_norm, m_mla_k_rope_norm, m_gqa_q_norm, m_gqa_k_norm, m_even_w_out, m_odd_norm, m_odd_w_qkv, m_swa_q_norm, m_swa_k_norm, m_swa_sink, m_odd_w_out, m_mlp_norm, m_mlp_w_up, m_mlp_w_down, v_even_norm, v_even_w_in, v_mla_q_lat_norm, v_mla_kv_lat_norm, v_mla_w_uq, v_mla_w_ukv, v_mla_q_norm, v_mla_k_nope_norm, v_mla_k_rope_norm, v_gqa_q_norm, v_gqa_k_norm, v_even_w_out, v_odd_norm, v_odd_w_qkv, v_swa_q_norm, v_swa_k_norm, v_swa_sink, v_odd_w_out, v_mlp_norm, v_mlp_w_up, v_mlp_w_down):
    ws = (even_norm, even_w_in, mla_q_lat_norm, mla_kv_lat_norm, mla_w_uq, mla_w_ukv, mla_q_norm, mla_k_nope_norm, mla_k_rope_norm, gqa_q_norm, gqa_k_norm, even_w_out, odd_norm, odd_w_qkv, swa_q_norm, swa_k_norm, swa_sink, odd_w_out, mlp_norm, mlp_w_up, mlp_w_down)
    ms = (m_even_norm, m_even_w_in, m_mla_q_lat_norm, m_mla_kv_lat_norm, m_mla_w_uq, m_mla_w_ukv, m_mla_q_norm, m_mla_k_nope_norm, m_mla_k_rope_norm, m_gqa_q_norm, m_gqa_k_norm, m_even_w_out, m_odd_norm, m_odd_w_qkv, m_swa_q_norm, m_swa_k_norm, m_swa_sink, m_odd_w_out, m_mlp_norm, m_mlp_w_up, m_mlp_w_down)
    vs = (v_even_norm, v_even_w_in, v_mla_q_lat_norm, v_mla_kv_lat_norm, v_mla_w_uq, v_mla_w_ukv, v_mla_q_norm, v_mla_k_nope_norm, v_mla_k_rope_norm, v_gqa_q_norm, v_gqa_k_norm, v_even_w_out, v_odd_norm, v_odd_w_qkv, v_swa_q_norm, v_swa_k_norm, v_swa_sink, v_odd_w_out, v_mlp_norm, v_mlp_w_up, v_mlp_w_down)
    return _step(x[0], loss_target[0], dict(zip(_WEIGHTS, ws)), dict(zip(_WEIGHTS, ms)), dict(zip(_WEIGHTS, vs)))
```
